```python
import jax, jax.numpy as jnp
from jax import lax
import numpy as np

D_MODEL = 1024
BATCH = 16
SEQ = 4096
DEPTH = 1

N_META = 16
EXPAND = 2
D_MIX = EXPAND * D_MODEL
POOL_WINDOWS = (2, 4, 8, 16)
N_POOL_GROUPS = len(POOL_WINDOWS)
D_POOL = D_MIX // 4
POOL_GROUP = D_POOL // N_POOL_GROUPS
MAX_WINDOW = max(POOL_WINDOWS)
D_SSM = D_MIX - D_POOL
SSM_HEAD_DIM = 64
N_SSM_HEADS = D_SSM // SSM_HEAD_DIM
N_SSM_GROUPS = 4
HEADS_PER_GROUP = N_SSM_HEADS // N_SSM_GROUPS
D_STATE = 128
CONV_WIDTH = 4
CHUNK = 128
D_XBC = D_SSM + 2 * N_SSM_GROUPS * D_STATE
D_IN_PROJ = D_POOL + D_SSM + D_XBC + N_SSM_HEADS
META_PAD = (-N_META) % CHUNK
D_FF = 4 * D_MODEL
EPS = 1e-5

kernel_name = "hymba_pool_ssd_hybrid"


def rms_norm(x, w):
    x32 = x.astype(jnp.float32)
    y = x32 * lax.rsqrt(jnp.mean(x32 * x32, axis=-1, keepdims=True) + EPS)
    return (y * w.astype(jnp.float32)).astype(x.dtype)


def pool_mixer(u, pool_w, pool_scale):
    bsz, L, _ = u.shape
    ug = u.astype(jnp.float32).reshape(bsz, L, N_POOL_GROUPS, POOL_GROUP)
    csum = jnp.cumsum(ug, axis=1)
    cp = jnp.pad(csum, ((0, 0), (MAX_WINDOW, 0), (0, 0), (0, 0)))
    pos = jnp.arange(L)
    pooled = []
    for gi, w in enumerate(POOL_WINDOWS):
        win_sum = cp[:, MAX_WINDOW:, gi] - cp[:, MAX_WINDOW - w:MAX_WINDOW - w + L, gi]
        count = jnp.minimum(pos + 1, w).astype(jnp.float32)[None, :, None]
        pooled.append(win_sum / count - ug[:, :, gi])
    pooled = jnp.stack(pooled, axis=2)
    mixed = jnp.einsum('blgc,gcd->blgd', pooled, pool_w.astype(jnp.float32))
    out = mixed.reshape(bsz, L, D_POOL) * pool_scale.astype(jnp.float32)
    return out.astype(u.dtype)


def causal_dwconv(x, w, b):
    y = lax.conv_general_dilated(
        x, w[:, None, :].astype(x.dtype), window_strides=(1,),
        padding=[(CONV_WIDTH - 1, 0)], dimension_numbers=('NWC', 'WIO', 'NWC'),
        feature_group_count=x.shape[-1])
    return y + b.astype(x.dtype)


def ssd_chunked(xs, dt, a, bs, cs):
    bsz, T = xs.shape[:2]
    nc = T // CHUNK
    G, R, P, N = N_SSM_GROUPS, HEADS_PER_GROUP, SSM_HEAD_DIM, D_STATE
    x = xs.reshape(bsz, nc, CHUNK, G, R, P)
    d = dt.reshape(bsz, nc, CHUNK, G, R)
    bc = bs.reshape(bsz, nc, CHUNK, G, N)
    cc = cs.reshape(bsz, nc, CHUNK, G, N)
    a_cs = jnp.cumsum(d * a, axis=2)
    xdt = x * d[..., None]
    causal = jnp.tril(jnp.ones((CHUNK, CHUNK), dtype=bool))
    seg = a_cs[:, :, :, None] - a_cs[:, :, None, :]
    decay = jnp.exp(jnp.where(causal[None, None, :, :, None, None], seg, -jnp.inf))
    cb = jnp.einsum('bclgn,bcsgn->bclsg', cc, bc)
    y_diag = jnp.einsum('bclsgr,bcsgrp->bclgrp', cb[..., None] * decay, xdt)
    decay_to_end = jnp.exp(a_cs[:, :, -1:] - a_cs)
    states = jnp.einsum('bclgn,bclgr,bclgrp->bcgrpn', bc, decay_to_end, xdt)
    chunk_decay = jnp.exp(a_cs[:, :, -1])

    def step(carry, inp):
        st, dec = inp
        return carry * dec[..., None, None] + st, carry

    init = jnp.zeros((bsz, G, R, P, N), dtype=xs.dtype)
    _, prev = lax.scan(step, init, (jnp.moveaxis(states, 1, 0), jnp.moveaxis(chunk_decay, 1, 0)))
    prev = jnp.moveaxis(prev, 0, 1)
    y_off = jnp.einsum('bclgn,bcgrpn,bclgr->bclgrp', cc, prev, jnp.exp(a_cs))
    return (y_diag + y_off).reshape(bsz, T, G, R, P)


def ssd_mixer(z, xbc, dt_raw, conv_w, conv_b, dt_bias, a_log, d_skip, ssm_norm_w):
    bsz, L, _ = z.shape
    G, R, P, N = N_SSM_GROUPS, HEADS_PER_GROUP, SSM_HEAD_DIM, D_STATE
    T = L + META_PAD
    xbc_p = jnp.pad(xbc, ((0, 0), (META_PAD, 0), (0, 0)))
    dt_p = jnp.pad(dt_raw, ((0, 0), (META_PAD, 0), (0, 0)))
    xbc_c = jax.nn.silu(causal_dwconv(xbc_p, conv_w, conv_b)).astype(jnp.float32)
    valid = (jnp.arange(T) >= META_PAD)[None, :, None]
    dt = jnp.where(valid, jax.nn.softplus(dt_p.astype(jnp.float32) + dt_bias.astype(jnp.float32)), 0.0)
    xs = xbc_c[..., :D_SSM].reshape(bsz, T, G, R, P)
    bs = xbc_c[..., D_SSM:D_SSM + G * N].reshape(bsz, T, G, N)
    cs = xbc_c[..., D_SSM + G * N:].reshape(bsz, T, G, N)
    a = -jnp.exp(a_log.astype(jnp.float32)).reshape(G, R)
    y = ssd_chunked(xs, dt.reshape(bsz, T, G, R), a, bs, cs)
    y = y + d_skip.astype(jnp.float32).reshape(G, R)[..., None] * xs
    y = y[:, META_PAD:].reshape(bsz, L, G, R * P)
    y = y * jax.nn.silu(z.astype(jnp.float32).reshape(bsz, L, G, R * P))
    y = y * lax.rsqrt(jnp.mean(y * y, axis=-1, keepdims=True) + EPS)
    y = y * ssm_norm_w.astype(jnp.float32).reshape(G, R * P)
    return y.reshape(bsz, L, D_SSM).astype(z.dtype)


def _fwd_setup_inputs(seed: int = 0) -> dict:
    key = jax.random.key(seed)
    ks = jax.random.split(key, 20)
    f32 = jnp.float32
    nrm = lambda k, shape, s: jax.random.normal(k, shape, f32) * s
    dt0 = jnp.exp(jax.random.uniform(ks[8], (DEPTH, N_SSM_HEADS), f32, np.log(1e-3), np.log(1e-1)))
    return {
        "x": nrm(ks[0], (BATCH, SEQ, D_MODEL), 1.0),
        "meta": nrm(ks[1], (N_META, D_MODEL), 1.0),
        "norm_mix_w": 1.0 + nrm(ks[2], (DEPTH, D_MODEL), 0.05),
        "w_in": nrm(ks[3], (DEPTH, D_MODEL, D_IN_PROJ), D_MODEL ** -0.5),
        "pool_w": nrm(ks[4], (DEPTH, N_POOL_GROUPS, POOL_GROUP, POOL_GROUP), POOL_GROUP ** -0.5),
        "pool_scale": 1.0 + nrm(ks[5], (DEPTH, D_POOL), 0.1),
        "conv_w": nrm(ks[6], (DEPTH, CONV_WIDTH, D_XBC), CONV_WIDTH ** -0.5),
        "conv_b": nrm(ks[7], (DEPTH, D_XBC), 0.02),
        "dt_bias": dt0 + jnp.log(-jnp.expm1(-dt0)),
        "a_log": jnp.log(jax.random.uniform(ks[9], (DEPTH, N_SSM_HEADS), f32, 1.0, 16.0)),
        "d_skip": 1.0 + nrm(ks[10], (DEPTH, N_SSM_HEADS), 0.1),
        "ssm_norm_w": 1.0 + nrm(ks[11], (DEPTH, D_SSM), 0.05),
        "w_out": nrm(ks[12], (DEPTH, D_MIX, D_MODEL), D_MIX ** -0.5),
        "norm_ffn_w": 1.0 + nrm(ks[13], (DEPTH, D_MODEL), 0.05),
        "w_ff1": nrm(ks[14], (DEPTH, D_MODEL, D_FF), D_MODEL ** -0.5),
        "w_ff2": nrm(ks[15], (DEPTH, D_FF, D_MODEL), D_FF ** -0.5),
        "norm_f_w": 1.0 + nrm(ks[16], (D_MODEL,), 0.05),
    }


def _fwd_reference(x, meta, norm_mix_w, w_in, pool_w, pool_scale, conv_w, conv_b, dt_bias, a_log,
              d_skip, ssm_norm_w, w_out, norm_ffn_w, w_ff1, w_ff2, norm_f_w):
    bsz = x.shape[0]
    h = jnp.concatenate(
        [jnp.broadcast_to(meta[None].astype(x.dtype), (bsz, N_META, D_MODEL)), x], axis=1)
    for i in range(DEPTH):
        hn = rms_norm(h, norm_mix_w[i])
        proj = jnp.einsum('bld,de->ble', hn, w_in[i])
        u = proj[..., :D_POOL]
        z = proj[..., D_POOL:D_POOL + D_SSM]
        xbc = proj[..., D_POOL + D_SSM:D_POOL + D_SSM + D_XBC]
        dt_raw = proj[..., D_POOL + D_SSM + D_XBC:]
        y_pool = pool_mixer(u, pool_w[i], pool_scale[i])
        y_ssm = ssd_mixer(z, xbc, dt_raw, conv_w[i], conv_b[i], dt_bias[i], a_log[i],
                          d_skip[i], ssm_norm_w[i])
        y = jnp.concatenate([y_pool, y_ssm], axis=-1)
        h = h + jnp.einsum('ble,ed->bld', y, w_out[i])
        hn = rms_norm(h, norm_ffn_w[i])
        ff = jnp.square(jax.nn.relu(jnp.einsum('bld,df->blf', hn, w_ff1[i])))
        h = h + jnp.einsum('blf,fd->bld', ff, w_ff2[i])
    out = rms_norm(h, norm_f_w)
    return out[:, N_META:]


import jax as _jax
import jax.numpy as _jnp

TWIN_FORMAT = 'train_step'
FWD_PARAMS = ['x', 'meta', 'norm_mix_w', 'w_in', 'pool_w', 'pool_scale', 'conv_w', 'conv_b', 'dt_bias', 'a_log', 'd_skip', 'ssm_norm_w', 'w_out', 'norm_ffn_w', 'w_ff1', 'w_ff2', 'norm_f_w']
TWIN_WEIGHTS = ['meta', 'norm_mix_w', 'w_in', 'pool_w', 'pool_scale', 'conv_w', 'conv_b', 'dt_bias', 'a_log', 'd_skip', 'ssm_norm_w', 'w_out', 'norm_ffn_w', 'w_ff1', 'w_ff2', 'norm_f_w']
TWIN_DIFF_INPUT = 'x'
TWIN_INPUTS = ['x', 'meta', 'norm_mix_w', 'w_in', 'pool_w', 'pool_scale', 'conv_w', 'conv_b', 'dt_bias', 'a_log', 'd_skip', 'ssm_norm_w', 'w_out', 'norm_ffn_w', 'w_ff1', 'w_ff2', 'norm_f_w', 'loss_target', 'm_meta', 'm_norm_mix_w', 'm_w_in', 'm_pool_w', 'm_pool_scale', 'm_conv_w', 'm_conv_b', 'm_dt_bias', 'm_a_log', 'm_d_skip', 'm_ssm_norm_w', 'm_w_out', 'm_norm_ffn_w', 'm_w_ff1', 'm_w_ff2', 'm_norm_f_w', 'v_meta', 'v_norm_mix_w', 'v_w_in', 'v_pool_w', 'v_pool_scale', 'v_conv_w', 'v_conv_b', 'v_dt_bias', 'v_a_log', 'v_d_skip', 'v_ssm_norm_w', 'v_w_out', 'v_norm_ffn_w', 'v_w_ff1', 'v_w_ff2', 'v_norm_f_w']
TWIN_OUTPUTS = ['loss', 'grad_x', 'grad_meta', 'grad_norm_mix_w', 'grad_w_in', 'grad_pool_w', 'grad_pool_scale', 'grad_conv_w', 'grad_conv_b', 'grad_dt_bias', 'grad_a_log', 'grad_d_skip', 'grad_ssm_norm_w', 'grad_w_out', 'grad_norm_ffn_w', 'grad_w_ff1', 'grad_w_ff2', 'grad_norm_f_w', 'delta_meta', 'delta_norm_mix_w', 'delta_w_in', 'delta_pool_w', 'delta_pool_scale', 'delta_conv_w', 'delta_conv_b', 'delta_dt_bias', 'delta_a_log', 'delta_d_skip', 'delta_ssm_norm_w', 'delta_w_out', 'delta_norm_ffn_w', 'delta_w_ff1', 'delta_w_ff2', 'delta_norm_f_w', 'new_m_meta', 'new_m_norm_mix_w', 'new_m_w_in', 'new_m_pool_w', 'new_m_pool_scale', 'new_m_conv_w', 'new_m_conv_b', 'new_m_dt_bias', 'new_m_a_log', 'new_m_d_skip', 'new_m_ssm_norm_w', 'new_m_w_out', 'new_m_norm_ffn_w', 'new_m_w_ff1', 'new_m_w_ff2', 'new_m_norm_f_w', 'new_v_meta', 'new_v_norm_mix_w', 'new_v_w_in', 'new_v_pool_w', 'new_v_pool_scale', 'new_v_conv_w', 'new_v_conv_b', 'new_v_dt_bias', 'new_v_a_log', 'new_v_d_skip', 'new_v_ssm_norm_w', 'new_v_w_out', 'new_v_norm_ffn_w', 'new_v_w_ff1', 'new_v_w_ff2', 'new_v_norm_f_w']
TWIN_LEAF_KINDS = {'loss': 'loss', 'grad_x': 'grad_x', 'grad_meta': 'grad_w', 'grad_norm_mix_w': 'grad_w', 'grad_w_in': 'grad_w', 'grad_pool_w': 'grad_w', 'grad_pool_scale': 'grad_w', 'grad_conv_w': 'grad_w', 'grad_conv_b': 'grad_w', 'grad_dt_bias': 'grad_w', 'grad_a_log': 'grad_w', 'grad_d_skip': 'grad_w', 'grad_ssm_norm_w': 'grad_w', 'grad_w_out': 'grad_w', 'grad_norm_ffn_w': 'grad_w', 'grad_w_ff1': 'grad_w', 'grad_w_ff2': 'grad_w', 'grad_norm_f_w': 'grad_w', 'delta_meta': 'delta_w', 'delta_norm_mix_w': 'delta_w', 'delta_w_in': 'delta_w', 'delta_pool_w': 'delta_w', 'delta_pool_scale': 'delta_w', 'delta_conv_w': 'delta_w', 'delta_conv_b': 'delta_w', 'delta_dt_bias': 'delta_w', 'delta_a_log': 'delta_w', 'delta_d_skip': 'delta_w', 'delta_ssm_norm_w': 'delta_w', 'delta_w_out': 'delta_w', 'delta_norm_ffn_w': 'delta_w', 'delta_w_ff1': 'delta_w', 'delta_w_ff2': 'delta_w', 'delta_norm_f_w': 'delta_w', 'new_m_meta': 'new_m', 'new_m_norm_mix_w': 'new_m', 'new_m_w_in': 'new_m', 'new_m_pool_w': 'new_m', 'new_m_pool_scale': 'new_m', 'new_m_conv_w': 'new_m', 'new_m_conv_b': 'new_m', 'new_m_dt_bias': 'new_m', 'new_m_a_log': 'new_m', 'new_m_d_skip': 'new_m', 'new_m_ssm_norm_w': 'new_m', 'new_m_w_out': 'new_m', 'new_m_norm_ffn_w': 'new_m', 'new_m_w_ff1': 'new_m', 'new_m_w_ff2': 'new_m', 'new_m_norm_f_w': 'new_m', 'new_v_meta': 'new_v', 'new_v_norm_mix_w': 'new_v', 'new_v_w_in': 'new_v', 'new_v_pool_w': 'new_v', 'new_v_pool_scale': 'new_v', 'new_v_conv_w': 'new_v', 'new_v_conv_b': 'new_v', 'new_v_dt_bias': 'new_v', 'new_v_a_log': 'new_v', 'new_v_d_skip': 'new_v', 'new_v_ssm_norm_w': 'new_v', 'new_v_w_out': 'new_v', 'new_v_norm_ffn_w': 'new_v', 'new_v_w_ff1': 'new_v', 'new_v_w_ff2': 'new_v', 'new_v_norm_f_w': 'new_v'}


def _forward(args):
    return _fwd_reference(*[args[k] for k in FWD_PARAMS])


def _output_shape():
    out = _jax.eval_shape(lambda: _forward(_fwd_setup_inputs(0)))
    return out.shape, out.dtype

N_MICROBATCH = 1
ADAM_LR = 0.001
ADAM_B1 = 0.9
ADAM_B2 = 0.999
ADAM_EPS = 1e-08
ADAM_WD = 0.01
ADAM_STEP = 10
PER_EXAMPLE_BATCH_AXIS = {'x': 0, 'loss_target': 0}
SHARED_INPUTS = []
_WEIGHT_DTYPES = {'meta': _jnp.float32, 'norm_mix_w': _jnp.float32, 'w_in': _jnp.float32, 'pool_w': _jnp.float32, 'pool_scale': _jnp.float32, 'conv_w': _jnp.float32, 'conv_b': _jnp.float32, 'dt_bias': _jnp.float32, 'a_log': _jnp.float32, 'd_skip': _jnp.float32, 'ssm_norm_w': _jnp.float32, 'w_out': _jnp.float32, 'norm_ffn_w': _jnp.float32, 'w_ff1': _jnp.float32, 'w_ff2': _jnp.float32, 'norm_f_w': _jnp.float32}
MOMENT_SCALE = {'meta': 4.178860e-03, 'norm_mix_w': 2.582019e-01, 'w_in': 1.236155e-01, 'pool_w': 1.241161e-01, 'pool_scale': 1.324668e-01, 'conv_w': 1.189168e-01, 'conv_b': 2.305440e-01, 'dt_bias': 2.531806e-01, 'a_log': 4.280830e-01, 'd_skip': 1.728481e+00, 'ssm_norm_w': 1.577881e-01, 'w_out': 2.088703e-01, 'norm_ffn_w': 2.100496e-01, 'w_ff1': 1.008856e-01, 'w_ff2': 3.008027e-01, 'norm_f_w': 6.445708e+01}


def _to_microbatches(a, axis):
    t = _jnp.moveaxis(a, axis, 0)
    t = t.reshape((N_MICROBATCH, t.shape[0] // N_MICROBATCH) + t.shape[1:])
    return _jnp.moveaxis(t, 1, axis + 1)


def setup_inputs(seed: int = 0) -> dict:
    inp = _fwd_setup_inputs(seed)
    key = _jax.random.fold_in(_jax.random.key(seed), 7919)
    shape, _ = _output_shape()
    out = dict(inp)
    out["loss_target"] = _jax.random.normal(_jax.random.fold_in(key, 0), shape, _jnp.float32)
    for i, name in enumerate(TWIN_WEIGHTS):
        w = inp[name].astype(_jnp.float32)
        if MOMENT_SCALE is None:
            s = _jnp.sqrt(_jnp.mean(_jnp.square(w)) + 1e-30)
        else:
            s = MOMENT_SCALE[name]
        km, kv = _jax.random.split(_jax.random.fold_in(key, i + 1))
        out[name] = w
        out["m_" + name] = s * _jax.random.normal(km, w.shape, _jnp.float32)
        out["v_" + name] = (s * s) * _jax.random.uniform(kv, w.shape, _jnp.float32, 0.5, 1.5)
    if N_MICROBATCH > 1:
        for name, axis in PER_EXAMPLE_BATCH_AXIS.items():
            out[name] = _to_microbatches(out[name], axis)
    return {'x': out['x'], 'meta': out['meta'], 'norm_mix_w': out['norm_mix_w'], 'w_in': out['w_in'], 'pool_w': out['pool_w'], 'pool_scale': out['pool_scale'], 'conv_w': out['conv_w'], 'conv_b': out['conv_b'], 'dt_bias': out['dt_bias'], 'a_log': out['a_log'], 'd_skip': out['d_skip'], 'ssm_norm_w': out['ssm_norm_w'], 'w_out': out['w_out'], 'norm_ffn_w': out['norm_ffn_w'], 'w_ff1': out['w_ff1'], 'w_ff2': out['w_ff2'], 'norm_f_w': out['norm_f_w'], 'loss_target': out['loss_target'], 'm_meta': out['m_meta'], 'm_norm_mix_w': out['m_norm_mix_w'], 'm_w_in': out['m_w_in'], 'm_pool_w': out['m_pool_w'], 'm_pool_scale': out['m_pool_scale'], 'm_conv_w': out['m_conv_w'], 'm_conv_b': out['m_conv_b'], 'm_dt_bias': out['m_dt_bias'], 'm_a_log': out['m_a_log'], 'm_d_skip': out['m_d_skip'], 'm_ssm_norm_w': out['m_ssm_norm_w'], 'm_w_out': out['m_w_out'], 'm_norm_ffn_w': out['m_norm_ffn_w'], 'm_w_ff1': out['m_w_ff1'], 'm_w_ff2': out['m_w_ff2'], 'm_norm_f_w': out['m_norm_f_w'], 'v_meta': out['v_meta'], 'v_norm_mix_w': out['v_norm_mix_w'], 'v_w_in': out['v_w_in'], 'v_pool_w': out['v_pool_w'], 'v_pool_scale': out['v_pool_scale'], 'v_conv_w': out['v_conv_w'], 'v_conv_b': out['v_conv_b'], 'v_dt_bias': out['v_dt_bias'], 'v_a_log': out['v_a_log'], 'v_d_skip': out['v_d_skip'], 'v_ssm_norm_w': out['v_ssm_norm_w'], 'v_w_out': out['v_w_out'], 'v_norm_ffn_w': out['v_norm_ffn_w'], 'v_w_ff1': out['v_w_ff1'], 'v_w_ff2': out['v_w_ff2'], 'v_norm_f_w': out['v_norm_f_w']}


def _loss(weights, diff, rest, loss_target):
    with _jax.named_scope("forward"):
        args = {**rest, TWIN_DIFF_INPUT: diff, **{k: w.astype(_WEIGHT_DTYPES[k]) for k, w in weights.items()}}
        y = _forward(args)
    with _jax.named_scope("loss_head"):
        err = _jnp.square(y.astype(_jnp.float32) - loss_target)
        return 0.5 * _jnp.sum(_jnp.mean(err, axis=-1)) if err.ndim else 0.5 * err


def _adamw(w, g, m, v):
    m = ADAM_B1 * m + (1.0 - ADAM_B1) * g
    v = ADAM_B2 * v + (1.0 - ADAM_B2) * _jnp.square(g)
    m_hat = m / (1.0 - ADAM_B1 ** ADAM_STEP)
    v_hat = v / (1.0 - ADAM_B2 ** ADAM_STEP)
    delta = -ADAM_LR * (m_hat / (_jnp.sqrt(v_hat) + ADAM_EPS) + ADAM_WD * w)
    return delta, m, v


def reference(x, meta, norm_mix_w, w_in, pool_w, pool_scale, conv_w, conv_b, dt_bias, a_log, d_skip, ssm_norm_w, w_out, norm_ffn_w, w_ff1, w_ff2, norm_f_w, loss_target, m_meta, m_norm_mix_w, m_w_in, m_pool_w, m_pool_scale, m_conv_w, m_conv_b, m_dt_bias, m_a_log, m_d_skip, m_ssm_norm_w, m_w_out, m_norm_ffn_w, m_w_ff1, m_w_ff2, m_norm_f_w, v_meta, v_norm_mix_w, v_w_in, v_pool_w, v_pool_scale, v_conv_w, v_conv_b, v_dt_bias, v_a_log, v_d_skip, v_ssm_norm_w, v_w_out, v_norm_ffn_w, v_w_ff1, v_w_ff2, v_norm_f_w):
    given = dict(x=x, meta=meta, norm_mix_w=norm_mix_w, w_in=w_in, pool_w=pool_w, pool_scale=pool_scale, conv_w=conv_w, conv_b=conv_b, dt_bias=dt_bias, a_log=a_log, d_skip=d_skip, ssm_norm_w=ssm_norm_w, w_out=w_out, norm_ffn_w=norm_ffn_w, w_ff1=w_ff1, w_ff2=w_ff2, norm_f_w=norm_f_w, loss_target=loss_target, m_meta=m_meta, m_norm_mix_w=m_norm_mix_w, m_w_in=m_w_in, m_pool_w=m_pool_w, m_pool_scale=m_pool_scale, m_conv_w=m_conv_w, m_conv_b=m_conv_b, m_dt_bias=m_dt_bias, m_a_log=m_a_log, m_d_skip=m_d_skip, m_ssm_norm_w=m_ssm_norm_w, m_w_out=m_w_out, m_norm_ffn_w=m_norm_ffn_w, m_w_ff1=m_w_ff1, m_w_ff2=m_w_ff2, m_norm_f_w=m_norm_f_w, v_meta=v_meta, v_norm_mix_w=v_norm_mix_w, v_w_in=v_w_in, v_pool_w=v_pool_w, v_pool_scale=v_pool_scale, v_conv_w=v_conv_w, v_conv_b=v_conv_b, v_dt_bias=v_dt_bias, v_a_log=v_a_log, v_d_skip=v_d_skip, v_ssm_norm_w=v_ssm_norm_w, v_w_out=v_w_out, v_norm_ffn_w=v_norm_ffn_w, v_w_ff1=v_w_ff1, v_w_ff2=v_w_ff2, v_norm_f_w=v_norm_f_w)
    weights = {n: given[n] for n in TWIN_WEIGHTS}
    shared = {n: given[n] for n in SHARED_INPUTS}
    per_example = {n: given[n] for n in ['x']}
    grad_fn = _jax.value_and_grad(_loss, argnums=(0, 1))

    def one_microbatch(ex, loss_target):
        ex = dict(ex)
        diff = ex.pop(TWIN_DIFF_INPUT)
        return grad_fn(weights, diff, {**shared, **ex}, loss_target)

    if N_MICROBATCH == 1:
        loss, (grad_w, grad_x) = one_microbatch(per_example, given["loss_target"])
    else:
        def body(carry, xs):
            loss_sum, grad_sum = carry
            l_k, (gw_k, gx_k) = one_microbatch(xs[0], xs[1])
            with _jax.named_scope("update"):
                return (loss_sum + l_k, _jax.tree.map(_jnp.add, grad_sum, gw_k)), gx_k

        init = (_jnp.zeros((), _jnp.float32), _jax.tree.map(_jnp.zeros_like, weights))
        (loss, grad_w), grad_x = _jax.lax.scan(body, init, (per_example, given["loss_target"]))
    with _jax.named_scope("update"):
        delta_w, new_m, new_v = {}, {}, {}
        for n in TWIN_WEIGHTS:
            delta_w[n], new_m[n], new_v[n] = _adamw(weights[n], grad_w[n], given["m_" + n], given["v_" + n])
    return (loss, grad_x, *[grad_w[n] for n in TWIN_WEIGHTS], *[delta_w[n] for n in TWIN_WEIGHTS],
            *[new_m[n] for n in TWIN_WEIGHTS], *[new_v[n] for n in TWIN_WEIGHTS])
```

```python
import functools

import numpy as np
import jax
import jax.numpy as jnp
from jax import lax
from jax.experimental import pallas as pl
from jax.experimental.pallas import tpu as pltpu

F32 = jnp.float32
MXU_DTYPE = jnp.bfloat16

D_MODEL = 1024
D_POOL = 512
D_SSM = 1536
D_XBC = 2560
N_HEADS = 24
HEAD_DIM = 64
N_GROUPS = 4
GROUP_CH = D_SSM // N_GROUPS
D_STATE = 128
CHUNK = 128
N_META = 16
LEAD = CHUNK
PAD_ROWS = LEAD - N_META
D_MIX = D_POOL + D_SSM
D_FF = 4096
PROJ_W = 4736
OFF_Z = D_POOL
OFF_X = D_POOL + D_SSM
OFF_DT = OFF_X + D_XBC
D_IN_PROJ = OFF_DT + N_HEADS
POOL_WINDOWS = (2, 4, 8, 16)
HALO = 16
EPS = 1e-5
N_DEV = 8

ADAM_LR, ADAM_B1, ADAM_B2, ADAM_EPS, ADAM_WD, ADAM_STEP = 0.001, 0.9, 0.999, 1e-08, 0.01, 10

VMEM_LIMIT = 60 * 1024 * 1024


def _mx(a):
    return a.astype(MXU_DTYPE)


def _dot(a, b):
    return jnp.dot(_mx(a), _mx(b), preferred_element_type=F32)


def _dot_nt(a, b):
    return lax.dot_general(_mx(a), _mx(b), (((1,), (1,)), ((), ())), preferred_element_type=F32)


def _dot_tn(a, b):
    return lax.dot_general(_mx(a), _mx(b), (((0,), (0,)), ((), ())), preferred_element_type=F32)


def _split3(x):
    hi = x.astype(MXU_DTYPE)
    r = x - hi.astype(F32)
    mid = r.astype(MXU_DTYPE)
    lo = (r - mid.astype(F32)).astype(MXU_DTYPE)
    return hi, mid, lo


def _exact_l(c, x):
    hi, mid, lo = _split3(x)
    f = lambda p: jnp.dot(c, p, preferred_element_type=F32)
    return f(hi) + f(mid) + f(lo)


def _exact_r(x, c):
    hi, mid, lo = _split3(x)
    f = lambda p: jnp.dot(p, c, preferred_element_type=F32)
    return f(hi) + f(mid) + f(lo)


def _sigmoid(x):
    return jax.nn.sigmoid(x)


def _softplus(x):
    return jnp.maximum(x, 0.0) + jnp.log1p(jnp.exp(-jnp.abs(x)))


def _silu_and_grad(x):
    s = _sigmoid(x)
    return x * s, s * (1.0 + x * (1.0 - s))


def _shift_down(ext, s):
    if s == 0:
        return ext[HALO:, :]
    return pltpu.roll(ext, s, 0)[HALO:, :]


def _shift_up(ext, s):
    if s == 0:
        return ext[:CHUNK, :]
    return pltpu.roll(ext, ext.shape[0] - s, 0)[:CHUNK, :]


def _by_pool_group(lane, a2, a4, a8, a16):
    return jnp.where(lane < 128, a2, jnp.where(lane < 256, a4, jnp.where(lane < 384, a8, a16)))


def _pool_inv_count(chunk_idx):
    row = lax.broadcasted_iota(jnp.int32, (CHUNK, D_POOL), 0)
    lane = lax.broadcasted_iota(jnp.int32, (CHUNK, D_POOL), 1)
    pos1 = jnp.maximum(chunk_idx * CHUNK + row - (PAD_ROWS - 1), 1)
    w = _by_pool_group(lane, 2, 4, 8, 16)
    return 1.0 / jnp.minimum(pos1, w).astype(F32), lane


def _pool_window_sums(u_ext, lane):
    s2 = u_ext + pltpu.roll(u_ext, 1, 0)
    s4 = s2 + pltpu.roll(s2, 2, 0)
    s8 = s4 + pltpu.roll(s4, 4, 0)
    s16 = s8 + pltpu.roll(s8, 8, 0)
    return _by_pool_group(lane, s2[HALO:], s4[HALO:], s8[HALO:], s16[HALO:])


def _pool_window_sums_ahead(q_ext, lane):
    n = q_ext.shape[0]
    r2 = q_ext + pltpu.roll(q_ext, n - 1, 0)
    r4 = r2 + pltpu.roll(r2, n - 2, 0)
    r8 = r4 + pltpu.roll(r4, n - 4, 0)
    r16 = r8 + pltpu.roll(r8, n - 8, 0)
    return _by_pool_group(lane, r2[:CHUNK], r4[:CHUNK], r8[:CHUNK], r16[:CHUNK])


def _conv_pre(ext, xbc, cw, cb):
    return (cb + cw[3:4, :] * xbc + cw[2:3, :] * _shift_down(ext, 1)
            + cw[1:2, :] * _shift_down(ext, 2) + cw[0:1, :] * _shift_down(ext, 3))


def _dt_and_cumsum(dtr, dt_bias, a_log, valid, tril):
    lane = lax.broadcasted_iota(jnp.int32, (CHUNK, 128), 1)
    head = lane < N_HEADS
    pre = dtr + dt_bias
    dt = jnp.where(valid & head, _softplus(pre), 0.0)
    a_row = jnp.where(head[0:1, :], -jnp.exp(a_log), 0.0)
    a_col = _exact_l(tril, dt * a_row)
    return dt, a_row, a_col, pre, head


def _decay(a_col, a_row_t, h, causal):
    seg = a_col[:, h:h + 1] - a_row_t[h:h + 1, :]
    return jnp.where(causal, jnp.exp(jnp.minimum(seg, 0.0)), 0.0)


def _ssd_chunk_fwd(xs, bm, cm, dt, a_col, s_prev, d_x, e_mat, et_mat):
    lane = lax.broadcasted_iota(jnp.int32, (CHUNK, 128), 1)
    rowi = lax.broadcasted_iota(jnp.int32, (CHUNK, CHUNK), 0)
    coli = lax.broadcasted_iota(jnp.int32, (CHUNK, CHUNK), 1)
    causal = rowi >= coli
    a_row_t = a_col.T
    ax = _exact_r(a_col, e_mat)
    dtx = _exact_r(dt, e_mat)
    xdt = xs * dtx
    ax_last = ax[CHUNK - 1:CHUNK, :]
    e_a = jnp.exp(ax)
    w_end = xdt * jnp.exp(ax_last - ax)
    cd_col = jnp.exp(_exact_l(et_mat, a_row_t)[:, CHUNK - 1:CHUNK])
    ys, s_new = [], []
    for g in range(N_GROUPS):
        gs = slice(g * GROUP_CH, (g + 1) * GROUP_CH)
        bg = bm[:, g * D_STATE:(g + 1) * D_STATE]
        cg = cm[:, g * D_STATE:(g + 1) * D_STATE]
        sg = s_prev[gs, :]
        cb = _dot_nt(cg, bg)
        y_off = _dot_nt(cg, sg) * e_a[:, gs]
        s_new.append(sg * cd_col[gs, :] + _dot_tn(w_end[:, gs], bg))
        for pr in range(3):
            c0 = g * GROUP_CH + pr * 128
            xdt_p = xdt[:, c0:c0 + 128]
            h0 = g * 6 + pr * 2
            y0 = _dot(cb * _decay(a_col, a_row_t, h0, causal), xdt_p)
            y1 = _dot(cb * _decay(a_col, a_row_t, h0 + 1, causal), xdt_p)
            ys.append(jnp.where(lane < HEAD_DIM, y0, y1) + y_off[:, pr * 128:(pr + 1) * 128])
    y = jnp.concatenate(ys, axis=1) + d_x * xs
    return y, jnp.concatenate(s_new, axis=0)


def _ssd_chunk_bwd(xs, bm, cm, dt, a_row, a_col, s_prev, ds_new, dy, d_x, e_mat, et_mat, triu):
    lane = lax.broadcasted_iota(jnp.int32, (CHUNK, 128), 1)
    sub = lax.broadcasted_iota(jnp.int32, (CHUNK, 128), 0)
    rowi = lax.broadcasted_iota(jnp.int32, (CHUNK, CHUNK), 0)
    coli = lax.broadcasted_iota(jnp.int32, (CHUNK, CHUNK), 1)
    causal = rowi >= coli
    a_row_t = a_col.T
    ax = _exact_r(a_col, e_mat)
    dtx = _exact_r(dt, e_mat)
    xdt = xs * dtx
    ax_last = ax[CHUNK - 1:CHUNK, :]
    e_a = jnp.exp(ax)
    dte = jnp.exp(ax_last - ax)
    w_end = xdt * dte
    cd_col = jnp.exp(_exact_l(et_mat, a_row_t)[:, CHUNK - 1:CHUNK])
    dye = dy * e_a

    dxdt, zc, ww_all, dbs, dcs, dsp, t1s = [], [], [], [], [], [], []
    zcol = jnp.zeros((CHUNK, 128), F32)
    zrow = jnp.zeros((128, CHUNK), F32)
    for g in range(N_GROUPS):
        gs = slice(g * GROUP_CH, (g + 1) * GROUP_CH)
        bg = bm[:, g * D_STATE:(g + 1) * D_STATE]
        cg = cm[:, g * D_STATE:(g + 1) * D_STATE]
        sg = s_prev[gs, :]
        dsg = ds_new[gs, :]
        cb = _dot_nt(cg, bg)
        cs = _dot_nt(cg, sg)
        dcg = _dot(dye[:, gs], sg)
        dsp.append(dsg * cd_col[gs, :] + _dot_tn(dye[:, gs], cg))
        dwg = _dot_nt(bg, dsg)
        dbg = _dot(w_end[:, gs], dsg)
        ww = dwg * w_end[:, gs]
        ww_all.append(ww)
        zc.append(dye[:, gs] * cs - ww)
        t1s.append(jnp.sum(dsg * sg, axis=1, keepdims=True) * cd_col[gs, :])
        dxdt_g = dwg * dte[:, gs]
        dcb = jnp.zeros((CHUNK, CHUNK), F32)
        pairs = []
        for pr in range(3):
            c0 = g * GROUP_CH + pr * 128
            xdt_p = xdt[:, c0:c0 + 128]
            dy_p = dy[:, c0:c0 + 128]
            acc = None
            for half in range(2):
                h = g * 6 + pr * 2 + half
                ld = _decay(a_col, a_row_t, h, causal)
                gm = cb * ld
                dym = jnp.where((lane < HEAD_DIM) if half == 0 else (lane >= HEAD_DIM), dy_p, 0.0)
                dg = _dot_nt(dym, xdt_p)
                dseg = dg * gm
                dcb = dcb + dg * ld
                t = _dot_tn(gm, dym)
                acc = t if acc is None else acc + t
                zcol = jnp.where(lane == h, jnp.sum(dseg, axis=1, keepdims=True), zcol)
                zrow = jnp.where(sub == h, jnp.sum(dseg, axis=0, keepdims=True), zrow)
            pairs.append(acc)
        dxdt.append(dxdt_g + jnp.concatenate(pairs, axis=1))
        dcs.append(dcg + _dot(dcb, bg))
        dbs.append(dbg + _dot_tn(dcb, cg))
    dxdt = jnp.concatenate(dxdt, axis=1)
    zc = jnp.concatenate(zc, axis=1)
    ww_all = jnp.concatenate(ww_all, axis=1)
    t1 = jnp.concatenate(t1s, axis=0)

    da_col = _exact_r(zc, et_mat) + zcol - zrow.T
    q_row = (jnp.sum(_exact_r(ww_all, et_mat), axis=0, keepdims=True)
             + jnp.sum(et_mat.astype(F32) * t1, axis=0, keepdims=True))
    da_col = da_col + jnp.where(sub == CHUNK - 1, q_row, 0.0)
    ddt = _exact_r(dxdt * xs, et_mat)
    dxs = dxdt * dtx + d_x * dy
    dd = jnp.sum(_exact_r(dy * xs, et_mat), axis=0, keepdims=True)
    rc = _exact_l(triu, da_col)
    ddt = ddt + rc * a_row
    da = jnp.sum(rc * dt, axis=0, keepdims=True)
    return (dxs, jnp.concatenate(dbs, axis=1), jnp.concatenate(dcs, axis=1), ddt, da, dd,
            jnp.concatenate(dsp, axis=0))


def _ssd_constants():
    ch = np.arange(D_SSM) // HEAD_DIM
    e = (np.arange(128)[:, None] == ch[None, :]).astype(np.float32)
    tril = np.tril(np.ones((CHUNK, CHUNK), np.float32))
    as_mx = lambda a: jnp.asarray(a, MXU_DTYPE)
    return as_mx(e), as_mx(e.T), as_mx(tril), as_mx(tril.T)


def _full(shape):
    nd = len(shape)
    return pl.BlockSpec(shape, lambda *_: (0,) * nd)


def _params(*sem):
    return pltpu.CompilerParams(dimension_semantics=sem, vmem_limit_bytes=VMEM_LIMIT)


def _in_proj(hp, w1, win):
    m = hp.shape[0]
    tm = 384

    def body(x_ref, w1_ref, win_hbm, hn_ref, proj_ref, win_v, sem):
        @pl.when(pl.program_id(0) == 0)
        def _():
            cp = pltpu.make_async_copy(win_hbm, win_v, sem)
            cp.start()
            cp.wait()

        x = x_ref[...]
        r = lax.rsqrt(jnp.mean(x * x, axis=-1, keepdims=True) + EPS)
        hn = _mx(x * r * w1_ref[...])
        hn_ref[...] = hn
        for j in range(0, PROJ_W, 512):
            w = min(512, PROJ_W - j)
            proj_ref[:, j:j + w] = jnp.dot(hn, win_v[:, j:j + w], preferred_element_type=F32)

    return pl.pallas_call(
        body, grid=(m // tm,), name="in_proj",
        in_specs=[pl.BlockSpec((tm, D_MODEL), lambda i: (i, 0)), _full((1, D_MODEL)), pl.BlockSpec(memory_space=pl.ANY)],
        out_specs=[pl.BlockSpec((tm, D_MODEL), lambda i: (i, 0)), pl.BlockSpec((tm, PROJ_W), lambda i: (i, 0))],
        out_shape=[jax.ShapeDtypeStruct((m, D_MODEL), MXU_DTYPE), jax.ShapeDtypeStruct((m, PROJ_W), F32)],
        scratch_shapes=[pltpu.VMEM((D_MODEL, PROJ_W), MXU_DTYPE), pltpu.SemaphoreType.DMA],
        compiler_params=_params("arbitrary"),
    )(hp, w1, win)


def _ffn_fwd_bwd(hp, y, tgt, wout, w2n, wff1, wff2, wfn, t_rows):
    m = hp.shape[0]
    tm = 256
    nb = m // t_rows
    nj = D_FF // 1024

    def body(hp_ref, y_ref, tgt_ref, w2n_ref, wfn_ref, wout_hbm, wff1_hbm, wff2_hbm,
             loss_ref, gwf_ref, gw2_ref, ff_ref, da_ref, hn2_ref, dh1_ref, dh2_ref, dy_ref,
             wout_v, wff1_v, wff2_v, a_s, sems):
        i = pl.program_id(0)

        @pl.when(i == 0)
        def _():
            cps = [pltpu.make_async_copy(s, d, sems.at[k])
                   for k, (s, d) in enumerate(((wout_hbm, wout_v), (wff1_hbm, wff1_v), (wff2_hbm, wff2_v)))]
            for cp in cps:
                cp.start()
            for cp in cps:
                cp.wait()
            loss_ref[...] = jnp.zeros_like(loss_ref)
            gwf_ref[...] = jnp.zeros_like(gwf_ref)
            gw2_ref[...] = jnp.zeros_like(gw2_ref)

        h1 = hp_ref[...] + jnp.dot(y_ref[...], wout_v[...], preferred_element_type=F32)
        r2 = lax.rsqrt(jnp.mean(h1 * h1, axis=-1, keepdims=True) + EPS)
        n2 = h1 * r2
        w2n_row = w2n_ref[...]
        hn2 = _mx(n2 * w2n_row)
        hn2_ref[...] = hn2
        h2 = h1
        for j in range(nj):
            js = slice(j * 1024, (j + 1) * 1024)
            a = jnp.dot(hn2, wff1_v[:, js], preferred_element_type=F32)
            a_s[:, js] = a
            ra = jnp.maximum(a, 0.0)
            ff = _mx(ra * ra)
            ff_ref[:, js] = ff
            h2 = h2 + jnp.dot(ff, wff2_v[js, :], preferred_element_type=F32)

        r3 = lax.rsqrt(jnp.mean(h2 * h2, axis=-1, keepdims=True) + EPS)
        n3 = h2 * r3
        wf_row = wfn_ref[...]
        err = n3 * wf_row - tgt_ref[...]
        row = i * tm + lax.broadcasted_iota(jnp.int32, (tm, 1), 0)
        tok = row < 0
        for b in range(nb):
            tok = tok | ((row >= b * t_rows + LEAD) & (row < (b + 1) * t_rows))
        tokf = tok.astype(F32)
        loss_ref[...] += 0.5 * jnp.sum(jnp.mean(err * err, axis=-1, keepdims=True) * tokf)
        dout = err * (tokf / D_MODEL)
        gwf_ref[...] += jnp.sum(dout * n3, axis=0, keepdims=True)
        dn3 = dout * wf_row
        dh2 = r3 * (dn3 - n3 * jnp.mean(dn3 * n3, axis=-1, keepdims=True))
        dh2m = _mx(dh2)
        dh2_ref[...] = dh2m

        dhn2 = jnp.zeros((tm, D_MODEL), F32)
        for j in range(nj):
            js = slice(j * 1024, (j + 1) * 1024)
            dff = lax.dot_general(dh2m, wff2_v[js, :], (((1,), (1,)), ((), ())), preferred_element_type=F32)
            da = _mx(dff * (2.0 * jnp.maximum(a_s[:, js], 0.0)))
            da_ref[:, js] = da
            dhn2 = dhn2 + lax.dot_general(da, wff1_v[:, js], (((1,), (1,)), ((), ())), preferred_element_type=F32)
        gw2_ref[...] += jnp.sum(dhn2 * n2, axis=0, keepdims=True)
        dn2 = dhn2 * w2n_row
        dh1 = dh2 + r2 * (dn2 - n2 * jnp.mean(dn2 * n2, axis=-1, keepdims=True))
        dh1_ref[...] = dh1
        dy_ref[...] = lax.dot_general(_mx(dh1), wout_v[...], (((1,), (1,)), ((), ())), preferred_element_type=F32)

    rows = lambda w: pl.BlockSpec((tm, w), lambda i: (i, 0))
    hbm = pl.BlockSpec(memory_space=pl.ANY)
    return pl.pallas_call(
        body, grid=(m // tm,), name="ffn_fwd_bwd",
        in_specs=[rows(D_MODEL), rows(D_MIX), rows(D_MODEL), _full((1, D_MODEL)), _full((1, D_MODEL)), hbm, hbm, hbm],
        out_specs=[_full((1, 128)), _full((1, D_MODEL)), _full((1, D_MODEL)), rows(D_FF), rows(D_FF), rows(D_MODEL),
                   rows(D_MODEL), rows(D_MODEL), rows(D_MIX)],
        out_shape=[jax.ShapeDtypeStruct((1, 128), F32), jax.ShapeDtypeStruct((1, D_MODEL), F32),
                   jax.ShapeDtypeStruct((1, D_MODEL), F32), jax.ShapeDtypeStruct((m, D_FF), MXU_DTYPE),
                   jax.ShapeDtypeStruct((m, D_FF), MXU_DTYPE), jax.ShapeDtypeStruct((m, D_MODEL), MXU_DTYPE),
                   jax.ShapeDtypeStruct((m, D_MODEL), F32), jax.ShapeDtypeStruct((m, D_MODEL), MXU_DTYPE),
                   jax.ShapeDtypeStruct((m, D_MIX), F32)],
        scratch_shapes=[pltpu.VMEM((D_MIX, D_MODEL), MXU_DTYPE), pltpu.VMEM((D_MODEL, D_FF), MXU_DTYPE),
                        pltpu.VMEM((D_FF, D_MODEL), MXU_DTYPE), pltpu.VMEM((tm, D_FF), F32),
                        pltpu.SemaphoreType.DMA((3,))],
        compiler_params=_params("arbitrary"),
    )(hp, y, tgt, w2n, wfn, wout, wff1, wff2)


def _in_proj_bwd(dproj, hp, dh1, w1, win, t_rows):
    m = hp.shape[0]
    tm = 384
    tiles_per_seq = t_rows // tm

    def body(dp_ref, hp_ref, dh1_ref, w1_ref, win_hbm, dh0_ref, gw1_ref, gmeta_ref, win_v, sem):
        i = pl.program_id(0)

        @pl.when(i == 0)
        def _():
            cp = pltpu.make_async_copy(win_hbm, win_v, sem)
            cp.start()
            cp.wait()
            gw1_ref[...] = jnp.zeros_like(gw1_ref)
            gmeta_ref[...] = jnp.zeros_like(gmeta_ref)

        dhn = lax.dot_general(dp_ref[...], win_v[...], (((1,), (1,)), ((), ())), preferred_element_type=F32)
        x = hp_ref[...]
        r = lax.rsqrt(jnp.mean(x * x, axis=-1, keepdims=True) + EPS)
        n = x * r
        gw1_ref[...] += jnp.sum(dhn * n, axis=0, keepdims=True)
        dn = dhn * w1_ref[...]
        dh0 = dh1_ref[...] + r * (dn - n * jnp.mean(dn * n, axis=-1, keepdims=True))
        dh0_ref[...] = dh0

        @pl.when(i % tiles_per_seq == 0)
        def _():
            gmeta_ref[...] += dh0[PAD_ROWS:LEAD, :]

    rows = lambda w: pl.BlockSpec((tm, w), lambda i: (i, 0))
    return pl.pallas_call(
        body, grid=(m // tm,), name="in_proj_bwd",
        in_specs=[rows(PROJ_W), rows(D_MODEL), rows(D_MODEL), _full((1, D_MODEL)), pl.BlockSpec(memory_space=pl.ANY)],
        out_specs=[rows(D_MODEL), _full((1, D_MODEL)), _full((N_META, D_MODEL))],
        out_shape=[jax.ShapeDtypeStruct((m, D_MODEL), F32), jax.ShapeDtypeStruct((1, D_MODEL), F32),
                   jax.ShapeDtypeStruct((N_META, D_MODEL), F32)],
        scratch_shapes=[pltpu.VMEM((D_MODEL, PROJ_W), MXU_DTYPE), pltpu.SemaphoreType.DMA],
        compiler_params=_params("arbitrary"),
    )(dproj, hp, dh1, w1, win)


def _tn_matmul(a, b, name, tka, tkm=384, tn=512):
    m, ka = a.shape
    nb = b.shape[1]
    n_steps = m // tkm

    def body(a_ref, b_ref, o_ref):
        @pl.when(pl.program_id(1) == 0)
        def _():
            o_ref[...] = jnp.zeros_like(o_ref)

        at = _mx(a_ref[...])
        for j in range(0, nb, tn):
            w = min(tn, nb - j)
            o_ref[:, j:j + w] += lax.dot_general(at, _mx(b_ref[:, j:j + w]), (((0,), (0,)), ((), ())),
                                                 preferred_element_type=F32)

    return pl.pallas_call(
        body, grid=(ka // tka, n_steps), name=name,
        in_specs=[pl.BlockSpec((tkm, tka), lambda i, k: (k, i)), pl.BlockSpec((tkm, nb), lambda i, k: (k, 0))],
        out_specs=pl.BlockSpec((tka, nb), lambda i, k: (i, 0)),
        out_shape=jax.ShapeDtypeStruct((ka, nb), F32),
        compiler_params=_params("arbitrary", "arbitrary"),
    )(a, b)


def _mixer_fwd(proj, cw, cb, dt_bias, a_log, d_x, nw, pool_w, pool_scale, nb):
    m = proj.shape[0]
    nc = m // nb // CHUNK
    e_mat, et_mat, tril, _ = _ssd_constants()

    def body(p_ref, cw_ref, cb_ref, dtb_ref, alog_ref, dx_ref, nw_ref, pw_ref, ps_ref, e_ref, et_ref, tril_ref,
             y_ref, ypre_ref, st_ref, xtail, utail, state):
        c = pl.program_id(1)

        @pl.when(c == 0)
        def _():
            xtail[...] = jnp.zeros_like(xtail)
            utail[...] = jnp.zeros_like(utail)
            state[...] = jnp.zeros_like(state)

        valid = (c > 0) | (lax.broadcasted_iota(jnp.int32, (CHUNK, 1), 0) >= PAD_ROWS)

        u = p_ref[:, 0:D_POOL]
        inv_cnt, lane = _pool_inv_count(c)
        win = _pool_window_sums(jnp.concatenate([utail[...], u], axis=0), lane)
        utail[...] = u[CHUNK - HALO:, :]
        pooled = win * inv_cnt - u
        mixed = jnp.concatenate(
            [_dot(pooled[:, g * 128:(g + 1) * 128], pw_ref[g]) for g in range(len(POOL_WINDOWS))], axis=1)
        y_ref[:, 0:D_POOL] = _mx(mixed * ps_ref[...])

        xbc = p_ref[:, OFF_X:OFF_X + D_XBC]
        pre = _conv_pre(jnp.concatenate([xtail[...], xbc], axis=0), xbc, cw_ref[...], cb_ref[...])
        xtail[...] = xbc[CHUNK - HALO:, :]
        xc = pre * _sigmoid(pre)
        dt, _, a_col, _, _ = _dt_and_cumsum(p_ref[:, OFF_DT:OFF_DT + 128], dtb_ref[...], alog_ref[...], valid,
                                            tril_ref[...])
        s_prev = state[...]
        st_ref[0] = s_prev
        yp, s_new = _ssd_chunk_fwd(xc[:, 0:D_SSM], xc[:, D_SSM:D_SSM + 512], xc[:, D_SSM + 512:], dt, a_col, s_prev,
                                   dx_ref[...], e_ref[...], et_ref[...])
        state[...] = s_new
        ypre_ref[...] = yp
        z = p_ref[:, OFF_Z:OFF_Z + D_SSM]
        yz = yp * (z * _sigmoid(z))
        outs = []
        for g in range(N_GROUPS):
            gs = slice(g * GROUP_CH, (g + 1) * GROUP_CH)
            r = lax.rsqrt(jnp.mean(yz[:, gs] * yz[:, gs], axis=-1, keepdims=True) + EPS)
            outs.append(yz[:, gs] * r)
        y_ref[:, D_POOL:] = _mx(jnp.concatenate(outs, axis=1) * nw_ref[...])

    blk = lambda w: pl.BlockSpec((CHUNK, w), lambda b, c: (b * nc + c, 0))
    return pl.pallas_call(
        body, grid=(nb, nc), name="mixer_fwd",
        in_specs=[blk(PROJ_W), _full((4, D_XBC)), _full((1, D_XBC)), _full((1, 128)), _full((1, 128)),
                  _full((1, D_SSM)), _full((1, D_SSM)), _full((4, 128, 128)), _full((1, D_POOL)),
                  _full((128, D_SSM)), _full((D_SSM, 128)), _full((CHUNK, CHUNK))],
        out_specs=[blk(D_MIX), blk(D_SSM), pl.BlockSpec((1, D_SSM, D_STATE), lambda b, c: (b * nc + c, 0, 0))],
        out_shape=[jax.ShapeDtypeStruct((m, D_MIX), MXU_DTYPE), jax.ShapeDtypeStruct((m, D_SSM), F32),
                   jax.ShapeDtypeStruct((m // CHUNK, D_SSM, D_STATE), F32)],
        scratch_shapes=[pltpu.VMEM((HALO, D_XBC), F32), pltpu.VMEM((HALO, D_POOL), F32),
                        pltpu.VMEM((D_SSM, D_STATE), F32)],
        compiler_params=_params("arbitrary", "arbitrary"),
    )(proj, cw, cb, dt_bias, a_log, d_x, nw, pool_w, pool_scale, e_mat, et_mat, tril)


def _mixer_bwd(proj, dy, ypre, states, cw, cb, dt_bias, a_log, d_x, nw, pool_w, pool_scale, nb):
    m = proj.shape[0]
    nc = m // nb // CHUNK
    e_mat, et_mat, tril, triu = _ssd_constants()
    hb = CHUNK // HALO

    def body(p_ref, halo_ref, dy_ref, ypre_ref, st_ref, cw_ref, cb_ref, dtb_ref, alog_ref, dx_ref, nw_ref, pw_ref,
             ps_ref, e_ref, et_ref, tril_ref, triu_ref,
             dp_ref, gcw_ref, gcb_ref, gdtb_ref, galog_ref, gd_ref, gnw_ref, gpw_ref, gps_ref,
             ds_carry, dpre_next, dq_next):
        b = pl.program_id(0)
        cc = pl.program_id(1)
        c = nc - 1 - cc

        @pl.when((b == 0) & (cc == 0))
        def _():
            for r in (gcw_ref, gcb_ref, gdtb_ref, galog_ref, gd_ref, gnw_ref, gpw_ref, gps_ref):
                r[...] = jnp.zeros_like(r)

        @pl.when(cc == 0)
        def _():
            ds_carry[...] = jnp.zeros_like(ds_carry)
            dpre_next[...] = jnp.zeros_like(dpre_next)
            dq_next[...] = jnp.zeros_like(dq_next)

        valid = (c > 0) | (lax.broadcasted_iota(jnp.int32, (CHUNK, 1), 0) >= PAD_ROWS)
        first = c > 0

        u = p_ref[:, 0:D_POOL]
        u_halo = jnp.where(first, halo_ref[:, 0:D_POOL], 0.0)
        inv_cnt, lane = _pool_inv_count(c)
        pooled = _pool_window_sums(jnp.concatenate([u_halo, u], axis=0), lane) * inv_cnt - u
        dyp = dy_ref[:, 0:D_POOL]
        ps = ps_ref[...]
        dmixed = dyp * ps
        mixed, dpooled = [], []
        for g in range(len(POOL_WINDOWS)):
            gsl = slice(g * 128, (g + 1) * 128)
            pw = pw_ref[g]
            mixed.append(_dot(pooled[:, gsl], pw))
            dpooled.append(_dot_nt(dmixed[:, gsl], pw))
            gpw_ref[g] += _dot_tn(pooled[:, gsl], dmixed[:, gsl])
        gps_ref[...] += jnp.sum(dyp * jnp.concatenate(mixed, axis=1), axis=0, keepdims=True)
        dpooled = jnp.concatenate(dpooled, axis=1)
        dq = dpooled * inv_cnt
        du = _pool_window_sums_ahead(jnp.concatenate([dq, dq_next[...]], axis=0), lane) - dpooled
        dq_next[...] = dq[0:HALO, :]
        dp_ref[:, 0:D_POOL] = _mx(du)

        yp = ypre_ref[...]
        z = p_ref[:, OFF_Z:OFF_Z + D_SSM]
        sz, dsz = _silu_and_grad(z)
        yz = yp * sz
        do = dy_ref[:, D_POOL:]
        nw_row = nw_ref[...]
        dyz = []
        gnw = []
        for g in range(N_GROUPS):
            gs = slice(g * GROUP_CH, (g + 1) * GROUP_CH)
            r = lax.rsqrt(jnp.mean(yz[:, gs] * yz[:, gs], axis=-1, keepdims=True) + EPS)
            n = yz[:, gs] * r
            gnw.append(jnp.sum(do[:, gs] * n, axis=0, keepdims=True))
            dn = do[:, gs] * nw_row[:, gs]
            dyz.append(r * (dn - n * jnp.mean(dn * n, axis=-1, keepdims=True)))
        gnw_ref[...] += jnp.concatenate(gnw, axis=1)
        dyz = jnp.concatenate(dyz, axis=1)
        dp_ref[:, OFF_Z:OFF_Z + D_SSM] = _mx(dyz * yp * dsz)
        dyp_ssm = dyz * sz

        xbc = p_ref[:, OFF_X:OFF_X + D_XBC]
        x_halo = jnp.where(first, halo_ref[:, OFF_X:OFF_X + D_XBC], 0.0)
        ext = jnp.concatenate([x_halo, xbc], axis=0)
        cw = cw_ref[...]
        pre = _conv_pre(ext, xbc, cw, cb_ref[...])
        xc, dsilu = _silu_and_grad(pre)
        dtr = p_ref[:, OFF_DT:OFF_DT + 128]
        dt, a_row, a_col, dt_pre, head = _dt_and_cumsum(dtr, dtb_ref[...], alog_ref[...], valid, tril_ref[...])
        dxs, dbm, dcm, ddt, da, dd, ds_prev = _ssd_chunk_bwd(
            xc[:, 0:D_SSM], xc[:, D_SSM:D_SSM + 512], xc[:, D_SSM + 512:], dt, a_row, a_col, st_ref[0],
            ds_carry[...], dyp_ssm, dx_ref[...], e_ref[...], et_ref[...], triu_ref[...])
        ds_carry[...] = ds_prev
        gd_ref[...] += dd
        galog_ref[...] += da * a_row
        ddtr = jnp.where(valid & head, ddt * _sigmoid(dt_pre), 0.0)
        gdtb_ref[...] += jnp.sum(ddtr, axis=0, keepdims=True)
        dp_ref[:, OFF_DT:OFF_DT + 128] = _mx(ddtr)

        dpre = jnp.concatenate([dxs, dbm, dcm], axis=1) * dsilu
        gcb_ref[...] += jnp.sum(dpre, axis=0, keepdims=True)
        gcw_ref[...] += jnp.concatenate(
            [jnp.sum(dpre * _shift_down(ext, 3 - k), axis=0, keepdims=True) for k in range(4)], axis=0)
        dext = jnp.concatenate([dpre, dpre_next[...]], axis=0)
        dpre_next[...] = dpre[0:HALO, :]
        dp_ref[:, OFF_X:OFF_X + D_XBC] = _mx(cw[3:4, :] * dpre + cw[2:3, :] * _shift_up(dext, 1)
                                             + cw[1:2, :] * _shift_up(dext, 2) + cw[0:1, :] * _shift_up(dext, 3))

    blk = lambda w: pl.BlockSpec((CHUNK, w), lambda b, cc: (b * nc + nc - 1 - cc, 0))
    halo = pl.BlockSpec((HALO, PROJ_W), lambda b, cc: (jnp.maximum((b * nc + nc - 1 - cc) * hb - 1, 0), 0))
    return pl.pallas_call(
        body, grid=(nb, nc), name="mixer_bwd",
        in_specs=[blk(PROJ_W), halo, blk(D_MIX), blk(D_SSM),
                  pl.BlockSpec((1, D_SSM, D_STATE), lambda b, cc: (b * nc + nc - 1 - cc, 0, 0)),
                  _full((4, D_XBC)), _full((1, D_XBC)), _full((1, 128)), _full((1, 128)), _full((1, D_SSM)),
                  _full((1, D_SSM)), _full((4, 128, 128)), _full((1, D_POOL)),
                  _full((128, D_SSM)), _full((D_SSM, 128)), _full((CHUNK, CHUNK)), _full((CHUNK, CHUNK))],
        out_specs=[blk(PROJ_W), _full((4, D_XBC)), _full((1, D_XBC)), _full((1, 128)), _full((1, 128)), _full((1, 128)),
                   _full((1, D_SSM)), _full((4, 128, 128)), _full((1, D_POOL))],
        out_shape=[jax.ShapeDtypeStruct((m, PROJ_W), MXU_DTYPE), jax.ShapeDtypeStruct((4, D_XBC), F32),
                   jax.ShapeDtypeStruct((1, D_XBC), F32), jax.ShapeDtypeStruct((1, 128), F32),
                   jax.ShapeDtypeStruct((1, 128), F32), jax.ShapeDtypeStruct((1, 128), F32),
                   jax.ShapeDtypeStruct((1, D_SSM), F32), jax.ShapeDtypeStruct((4, 128, 128), F32),
                   jax.ShapeDtypeStruct((1, D_POOL), F32)],
        scratch_shapes=[pltpu.VMEM((D_SSM, D_STATE), F32), pltpu.VMEM((HALO, D_XBC), F32),
                        pltpu.VMEM((HALO, D_POOL), F32)],
        compiler_params=_params("arbitrary", "arbitrary"),
    )(proj, proj, dy, ypre, states, cw, cb, dt_bias, a_log, d_x, nw, pool_w, pool_scale, e_mat, et_mat, tril, triu)


def _pad128(v):
    v = v.reshape(1, -1).astype(F32)
    return jnp.pad(v, ((0, 0), (0, 128 - v.shape[1])))


def _local_step(x, tgt, meta, w):
    nb, seq, _ = x.shape
    t_rows = LEAD + seq
    lead = jnp.concatenate([jnp.zeros((PAD_ROWS, D_MODEL), F32), meta.astype(F32)], axis=0)
    hp = jnp.concatenate([jnp.broadcast_to(lead[None], (nb, LEAD, D_MODEL)), x], axis=1).reshape(nb * t_rows, D_MODEL)
    tgt_p = jnp.pad(tgt, ((0, 0), (LEAD, 0), (0, 0))).reshape(nb * t_rows, D_MODEL)

    dt_bias, a_log = _pad128(w["dt_bias"]), _pad128(w["a_log"])
    d_x = jnp.repeat(w["d_skip"].reshape(1, N_HEADS).astype(F32), HEAD_DIM, axis=1)

    hn1, proj = _in_proj(hp, w["norm_mix_w"], w["win"])
    y, ypre, states = _mixer_fwd(proj, w["conv_w"], w["conv_b"], dt_bias, a_log, d_x, w["ssm_norm_w"], w["pool_w"],
                                 w["pool_scale"], nb)
    loss, g_nf, g_nffn, ff, da, hn2, dh1, dh2, dy = _ffn_fwd_bwd(
        hp, y, tgt_p, w["wout"], w["norm_ffn_w"], w["wff1"], w["wff2"], w["norm_f_w"], t_rows)
    g_wff2 = _tn_matmul(ff, dh2, "grad_w_ff2", tka=1024)
    g_wff1 = _tn_matmul(hn2, da, "grad_w_ff1", tka=256)
    g_wout = _tn_matmul(y, dh1, "grad_w_out", tka=1024)
    dproj, g_cw, g_cb, g_dtb, g_alog, g_d, g_nw, g_pw, g_ps = _mixer_bwd(
        proj, dy, ypre, states, w["conv_w"], w["conv_b"], dt_bias, a_log, d_x, w["ssm_norm_w"], w["pool_w"],
        w["pool_scale"], nb)
    g_win = _tn_matmul(hn1, dproj, "grad_w_in", tka=512)
    dh0, g_nmix, g_meta = _in_proj_bwd(dproj, hp, dh1, w["norm_mix_w"], w["win"], t_rows)
    grads = dict(norm_mix_w=g_nmix, win=g_win, pool_w=g_pw, pool_scale=g_ps, conv_w=g_cw, conv_b=g_cb,
                 dt_bias=g_dtb[:, :N_HEADS], a_log=g_alog[:, :N_HEADS], d_skip=g_d[:, :N_HEADS], ssm_norm_w=g_nw,
                 wout=g_wout, norm_ffn_w=g_nffn, wff1=g_wff1, wff2=g_wff2, norm_f_w=g_nf)
    return loss[0, 0], dh0.reshape(nb, t_rows, D_MODEL), g_meta, grads


MESH_IDS = pl.DeviceIdType.MESH
_HBM = pl.BlockSpec(memory_space=pltpu.HBM)


def _coords():
    return lax.axis_index("x"), lax.axis_index("y"), lax.axis_index("c")


def _other_chips(x, y):
    return [(1 - x, y), (x, 1 - y), (1 - x, 1 - y)]


def _weight_gather(shards):
    n = len(shards)

    def body(*refs):
        ins, outs = refs[:n], refs[n:2 * n]
        send_sems, recv_sems, local_sems = refs[2 * n:]
        x, y, c = _coords()
        me, sibling = (x, y, c), (x, y, 1 - c)
        chips = _other_chips(x, y)

        def copy(t, k, block, to, src=None):
            dst = outs[t].at[4 * block[0] + 2 * block[1] + block[2]]
            return pltpu.make_async_remote_copy(
                src_ref=dst if src is None else src, dst_ref=dst, send_sem=send_sems.at[t * 7 + k],
                recv_sem=recv_sems.at[t * 7 + k], device_id=to, device_id_type=MESH_IDS)

        mine = [pltpu.make_async_copy(ins[t], outs[t].at[4 * x + 2 * y + c], local_sems.at[t]) for t in range(n)]
        for cp in mine:
            cp.start()
        first = []
        for t in range(n):
            first.append(copy(t, 0, me, sibling, src=ins[t]))
            first += [copy(t, 1 + j, me, (*chip, c), src=ins[t]) for j, chip in enumerate(chips)]
        for cp in first:
            cp.start()
        passed = []
        for j, chip in enumerate(chips):
            for t in range(n):
                copy(t, 1 + j, (*chip, c), me).wait_recv()
                cp = copy(t, 4 + j, (*chip, c), sibling)
                cp.start()
                passed.append(cp)
        for t in range(n):
            copy(t, 0, sibling, me).wait_recv()
            for j, chip in enumerate(chips):
                copy(t, 4 + j, (*chip, 1 - c), me).wait_recv()
        for cp in first + passed:
            cp.wait_send()
        for cp in mine:
            cp.wait()

    return pl.pallas_call(
        body, name="weight_gather",
        in_specs=[_HBM] * n, out_specs=[_HBM] * n,
        out_shape=[jax.ShapeDtypeStruct((N_DEV,) + s.shape, s.dtype) for s in shards],
        scratch_shapes=[pltpu.SemaphoreType.DMA((7 * n,)), pltpu.SemaphoreType.DMA((7 * n,)),
                        pltpu.SemaphoreType.DMA((n,))],
    )(*shards)


def _grad_exchange_d2d(gs, small):
    n = len(gs)

    def body(*refs):
        ins, small_in = refs[:n], refs[n]
        kept, got, small_got = refs[n + 1:2 * n + 1], refs[2 * n + 1:3 * n + 1], refs[3 * n + 1]
        send_sems, recv_sems, local_sems = refs[3 * n + 2:]
        x, y, c = _coords()
        sibling = (x, y, 1 - c)
        local, remote = [], []
        for t in range(n):
            for k in range(4):
                local.append(pltpu.make_async_copy(ins[t].at[k, c], kept[t].at[k], local_sems.at[t * 4 + k]))
                remote.append(pltpu.make_async_remote_copy(
                    src_ref=ins[t].at[k, 1 - c], dst_ref=got[t].at[k], send_sem=send_sems.at[t * 4 + k],
                    recv_sem=recv_sems.at[t * 4 + k], device_id=sibling, device_id_type=MESH_IDS))
        remote.append(pltpu.make_async_remote_copy(
            src_ref=small_in, dst_ref=small_got, send_sem=send_sems.at[4 * n], recv_sem=recv_sems.at[4 * n],
            device_id=sibling, device_id_type=MESH_IDS))
        for cp in remote + local:
            cp.start()
        for cp in remote:
            cp.wait_recv()
        for cp in remote:
            cp.wait_send()
        for cp in local:
            cp.wait()

    blocks = [jax.ShapeDtypeStruct((4,) + g.shape[2:], g.dtype) for g in gs]
    outs = pl.pallas_call(
        body, name="grad_exchange_d2d",
        in_specs=[_HBM] * (n + 1), out_specs=[_HBM] * (2 * n + 1),
        out_shape=blocks + blocks + [jax.ShapeDtypeStruct(small.shape, small.dtype)],
        scratch_shapes=[pltpu.SemaphoreType.DMA((4 * n + 1,)), pltpu.SemaphoreType.DMA((4 * n + 1,)),
                        pltpu.SemaphoreType.DMA((4 * n,))],
    )(*gs, small)
    return outs[:n], outs[n:2 * n], outs[2 * n]


def _grad_exchange_ici(ps, small):
    arrs = list(ps) + [small]
    n = len(arrs)

    def body(*refs):
        ins, outs = refs[:n], refs[n:2 * n]
        send_sems, recv_sems, local_sems = refs[2 * n:]
        x, y, c = _coords()
        my_chip = 2 * x + y
        chips = _other_chips(x, y)
        src_of = lambda t, k: ins[t] if t == n - 1 else ins[t].at[k]
        local = [pltpu.make_async_copy(src_of(t, my_chip), outs[t].at[my_chip], local_sems.at[t]) for t in range(n)]
        remote = []
        for t in range(n):
            for j, (cx, cy) in enumerate(chips):
                remote.append(pltpu.make_async_remote_copy(
                    src_ref=src_of(t, 2 * cx + cy), dst_ref=outs[t].at[my_chip], send_sem=send_sems.at[t * 3 + j],
                    recv_sem=recv_sems.at[t * 3 + j], device_id=(cx, cy, c), device_id_type=MESH_IDS))
        for cp in remote + local:
            cp.start()
        for t in range(n):
            for j, (cx, cy) in enumerate(chips):
                slot = outs[t].at[2 * cx + cy]
                pltpu.make_async_remote_copy(
                    src_ref=slot, dst_ref=slot, send_sem=send_sems.at[t * 3 + j], recv_sem=recv_sems.at[t * 3 + j],
                    device_id=(cx, cy, c), device_id_type=MESH_IDS).wait_recv()
        for cp in remote:
            cp.wait_send()
        for cp in local:
            cp.wait()

    outs = pl.pallas_call(
        body, name="grad_exchange_ici",
        in_specs=[_HBM] * n, out_specs=[_HBM] * n,
        out_shape=[jax.ShapeDtypeStruct(p.shape, p.dtype) for p in ps]
        + [jax.ShapeDtypeStruct((4,) + small.shape, small.dtype)],
        scratch_shapes=[pltpu.SemaphoreType.DMA((3 * n,)), pltpu.SemaphoreType.DMA((3 * n,)),
                        pltpu.SemaphoreType.DMA((n,))],
    )(*arrs)
    return outs[:n - 1], outs[n - 1]


def _row_tile(rows, cols, n_arrays):
    budget = 24 * 1024 * 1024
    padded = -(-cols // 128) * 128
    tr = max(8, budget // (n_arrays * 2 * 4 * padded) // 8 * 8)
    while rows % tr:
        tr -= 8
    return tr


def _add(a, b, name):
    shape = a.shape
    a2, b2 = a.reshape(-1, shape[-1]), b.reshape(-1, shape[-1])
    rows, cols = a2.shape
    tr = _row_tile(rows, cols, 3)

    def body(a_ref, b_ref, o_ref):
        o_ref[...] = a_ref[...] + b_ref[...]

    blk = pl.BlockSpec((tr, cols), lambda i: (i, 0))
    return pl.pallas_call(body, grid=(rows // tr,), name=name, in_specs=[blk, blk], out_specs=blk,
                          out_shape=jax.ShapeDtypeStruct(a2.shape, a.dtype),
                          compiler_params=_params("arbitrary"))(a2, b2).reshape(shape)


def _sum4(parts, name):
    _, rows, cols = parts.shape
    tr = _row_tile(rows, cols, 5)

    def body(p_ref, o_ref):
        o_ref[...] = ((p_ref[0] + p_ref[1]) + p_ref[2]) + p_ref[3]

    return pl.pallas_call(body, grid=(rows // tr,), name=name,
                          in_specs=[pl.BlockSpec((4, tr, cols), lambda i: (0, i, 0))],
                          out_specs=pl.BlockSpec((tr, cols), lambda i: (i, 0)),
                          out_shape=jax.ShapeDtypeStruct((rows, cols), parts.dtype),
                          compiler_params=_params("arbitrary"))(parts)


def _adamw_math(w, g, m, v):
    m2 = ADAM_B1 * m + (1.0 - ADAM_B1) * g
    v2 = ADAM_B2 * v + (1.0 - ADAM_B2) * (g * g)
    m_hat = m2 / (1.0 - ADAM_B1 ** ADAM_STEP)
    v_hat = v2 / (1.0 - ADAM_B2 ** ADAM_STEP)
    delta = -ADAM_LR * (m_hat / (jnp.sqrt(v_hat) + ADAM_EPS) + ADAM_WD * w)
    return delta, m2, v2


def _adamw(parts, w, m, v, name):
    rows, cols = w.shape
    summed = parts.ndim == 3
    tr = _row_tile(rows, cols, 11 if summed else 8)

    def body(p_ref, w_ref, m_ref, v_ref, g_ref, d_ref, m2_ref, v2_ref):
        g = ((p_ref[0] + p_ref[1]) + p_ref[2]) + p_ref[3] if summed else p_ref[...]
        d, m2, v2 = _adamw_math(w_ref[...], g, m_ref[...], v_ref[...])
        g_ref[...] = g
        d_ref[...] = d
        m2_ref[...] = m2
        v2_ref[...] = v2

    blk = pl.BlockSpec((tr, cols), lambda i: (i, 0))
    pblk = pl.BlockSpec((4, tr, cols), lambda i: (0, i, 0)) if summed else blk
    out = jax.ShapeDtypeStruct((rows, cols), F32)
    return pl.pallas_call(body, grid=(rows // tr,), name=name, in_specs=[pblk, blk, blk, blk], out_specs=[blk] * 4,
                          out_shape=[out] * 4, compiler_params=_params("arbitrary"))(parts, w, m, v)


def _pack(arrays):
    rows = []
    for a in arrays:
        flat = a.astype(F32).reshape(-1)
        rows.append(jnp.pad(flat, (0, -flat.shape[0] % 128)).reshape(-1, 128))
    out = jnp.concatenate(rows, axis=0)
    return jnp.pad(out, ((0, -out.shape[0] % 8), (0, 0)))


def _unpack(pack, shapes):
    out, r = [], 0
    for s in shapes:
        n = int(np.prod(s))
        nr = -(-n // 128)
        out.append(pack[r:r + nr].reshape(-1)[:n].reshape(s))
        r += nr
    return out


_WEIGHTS = ["meta", "norm_mix_w", "w_in", "pool_w", "pool_scale", "conv_w", "conv_b", "dt_bias", "a_log", "d_skip",
            "ssm_norm_w", "w_out", "norm_ffn_w", "w_ff1", "w_ff2", "norm_f_w"]
_BIG = ["w_in", "w_out", "w_ff1", "w_ff2"]
_SMALL = [n for n in _WEIGHTS if n not in _BIG]


def kernel(x, meta, norm_mix_w, w_in, pool_w, pool_scale, conv_w, conv_b, dt_bias, a_log, d_skip, ssm_norm_w, w_out, norm_ffn_w, w_ff1, w_ff2, norm_f_w, loss_target, m_meta, m_norm_mix_w, m_w_in, m_pool_w, m_pool_scale, m_conv_w, m_conv_b, m_dt_bias, m_a_log, m_d_skip, m_ssm_norm_w, m_w_out, m_norm_ffn_w, m_w_ff1, m_w_ff2, m_norm_f_w, v_meta, v_norm_mix_w, v_w_in, v_pool_w, v_pool_scale, v_conv_w, v_conv_b, v_dt_bias, v_a_log, v_d_skip, v_ssm_norm_w, v_w_out, v_norm_ffn_w, v_w_ff1, v_w_ff2, v_norm_f_w):
    wts = dict(meta=meta, norm_mix_w=norm_mix_w, w_in=w_in, pool_w=pool_w, pool_scale=pool_scale, conv_w=conv_w,
               conv_b=conv_b, dt_bias=dt_bias, a_log=a_log, d_skip=d_skip, ssm_norm_w=ssm_norm_w, w_out=w_out,
               norm_ffn_w=norm_ffn_w, w_ff1=w_ff1, w_ff2=w_ff2, norm_f_w=norm_f_w)
    mom1 = dict(zip(_WEIGHTS, (m_meta, m_norm_mix_w, m_w_in, m_pool_w, m_pool_scale, m_conv_w, m_conv_b, m_dt_bias,
                               m_a_log, m_d_skip, m_ssm_norm_w, m_w_out, m_norm_ffn_w, m_w_ff1, m_w_ff2, m_norm_f_w)))
    mom2 = dict(zip(_WEIGHTS, (v_meta, v_norm_mix_w, v_w_in, v_pool_w, v_pool_scale, v_conv_w, v_conv_b, v_dt_bias,
                               v_a_log, v_d_skip, v_ssm_norm_w, v_w_out, v_norm_ffn_w, v_w_ff1, v_w_ff2, v_norm_f_w)))
    xi, yi, ci = _coords()
    dev = 4 * xi + 2 * yi + ci
    win_cols = w_in.shape[-1]
    cw_cols = conv_w.shape[-1]

    lead_pack = jnp.zeros((N_META, 512), F32)
    lead_pack = lead_pack.at[:, :128].set(meta).at[:4, 128:128 + cw_cols].set(conv_w[0])
    g_win, g_wout, g_wff1, g_wff2, g_lead = _weight_gather(
        [_mx(w_in[0]), _mx(w_out[0]), _mx(w_ff1[0]), _mx(w_ff2[0]), lead_pack])
    win_full = jnp.transpose(g_win, (1, 0, 2)).reshape(D_MODEL, N_DEV * win_cols)
    win_full = jnp.pad(win_full, ((0, 0), (0, PROJ_W - N_DEV * win_cols)))
    wff1_full = jnp.transpose(g_wff1, (1, 0, 2)).reshape(D_MODEL, D_FF)
    meta_full = jnp.transpose(g_lead[:, :, :128], (1, 0, 2)).reshape(N_META, D_MODEL)
    cw_full = jnp.transpose(g_lead[:, :4, 128:128 + cw_cols], (1, 0, 2)).reshape(4, D_XBC)

    full = dict(norm_mix_w=norm_mix_w, win=win_full, pool_w=pool_w[0], pool_scale=pool_scale, conv_w=cw_full,
                conv_b=conv_b, dt_bias=dt_bias, a_log=a_log, d_skip=d_skip, ssm_norm_w=ssm_norm_w,
                wout=g_wout.reshape(D_MIX, D_MODEL), norm_ffn_w=norm_ffn_w, wff1=wff1_full,
                wff2=g_wff2.reshape(D_FF, D_MODEL), norm_f_w=norm_f_w.reshape(1, D_MODEL))
    loss, dh0, g_meta, g = _local_step(x, loss_target, meta_full, full)

    owners = lambda a: a.reshape((4, 2) + a.shape[1:])
    big_parts = [
        owners(jnp.transpose(g["win"][:, :N_DEV * win_cols].reshape(D_MODEL, N_DEV, win_cols), (1, 0, 2))),
        owners(g["wout"].reshape(N_DEV, D_MIX // N_DEV, D_MODEL)),
        owners(jnp.transpose(g["wff1"].reshape(D_MODEL, N_DEV, D_FF // N_DEV), (1, 0, 2))),
        owners(g["wff2"].reshape(N_DEV, D_FF // N_DEV, D_MODEL)),
    ]
    small_names = ["meta", "norm_mix_w", "pool_w", "pool_scale", "conv_w", "conv_b", "dt_bias", "a_log", "d_skip",
                   "ssm_norm_w", "norm_ffn_w", "norm_f_w"]
    small_full = dict(g, meta=g_meta)
    small_pack = _pack([small_full[n] for n in small_names])
    kept, got, small_got = _grad_exchange_d2d(big_parts, small_pack)
    chip_sums = [_add(a, b, "chip_sum_%d" % t) for t, (a, b) in enumerate(zip(kept, got))]
    small_chip = _add(small_pack, small_got, "chip_sum_small")
    parts, small_parts = _grad_exchange_ici(chip_sums, small_chip)
    small_sum = _sum4(small_parts, "small_sum")
    small_shapes = [small_full[n].shape for n in small_names]
    gs = dict(zip(small_names, _unpack(small_sum, small_shapes)))
    gs["meta"] = lax.dynamic_slice_in_dim(gs["meta"], dev * 128, 128, axis=1)
    gs["conv_w"] = lax.dynamic_slice_in_dim(gs["conv_w"], dev * cw_cols, cw_cols, axis=1)

    res = {}
    for t, n in enumerate(_BIG):
        shp = wts[n].shape
        res[n] = [o.reshape(shp) for o in _adamw(parts[t], wts[n][0], mom1[n][0], mom2[n][0], "adamw_" + n)]
    packs = [_pack([d[n] for n in _SMALL]) for d in (gs, wts, mom1, mom2)]
    small_out = _adamw(*packs, "adamw_small")
    shapes = [wts[n].shape for n in _SMALL]
    for k, o in enumerate(small_out):
        for n, a in zip(_SMALL, _unpack(o, shapes)):
            res.setdefault(n, [None] * 4)[k] = a

    loss = lax.psum(loss, ("x", "y", "c"))
    grad_x = dh0[:, LEAD:, :]
    return (loss, grad_x, *[res[n][0] for n in _WEIGHTS], *[res[n][1] for n in _WEIGHTS],
            *[res[n][2] for n in _WEIGHTS], *[res[n][3] for n in _WEIGHTS])
```

```python
import functools

import numpy as np
import jax
import jax.numpy as jnp
from jax import lax
from jax.experimental import pallas as pl
from jax.experimental.pallas import tpu as pltpu

F32 = jnp.float32
MXU_DTYPE = jnp.bfloat16

D_MODEL = 1024
D_POOL = 512
D_SSM = 1536
D_XBC = 2560
N_HEADS = 24
HEAD_DIM = 64
N_GROUPS = 4
GROUP_CH = D_SSM // N_GROUPS
D_STATE = 128
CHUNK = 128
N_META = 16
LEAD = CHUNK
PAD_ROWS = LEAD - N_META
D_MIX = D_POOL + D_SSM
D_FF = 4096
PROJ_W = 4736
OFF_Z = D_POOL
OFF_X = D_POOL + D_SSM
OFF_DT = OFF_X + D_XBC
D_IN_PROJ = OFF_DT + N_HEADS
POOL_WINDOWS = (2, 4, 8, 16)
HALO = 16
EPS = 1e-5
N_DEV = 8

ADAM_LR, ADAM_B1, ADAM_B2, ADAM_EPS, ADAM_WD, ADAM_STEP = 0.001, 0.9, 0.999, 1e-08, 0.01, 10

VMEM_LIMIT = 60 * 1024 * 1024


def _mx(a):
    return a.astype(MXU_DTYPE)


def _dot(a, b):
    return jnp.dot(_mx(a), _mx(b), preferred_element_type=F32)


def _dot_nt(a, b):
    return lax.dot_general(_mx(a), _mx(b), (((1,), (1,)), ((), ())), preferred_element_type=F32)


def _dot_tn(a, b):
    return lax.dot_general(_mx(a), _mx(b), (((0,), (0,)), ((), ())), preferred_element_type=F32)


def _split3(x):
    hi = x.astype(MXU_DTYPE)
    r = x - hi.astype(F32)
    mid = r.astype(MXU_DTYPE)
    lo = (r - mid.astype(F32)).astype(MXU_DTYPE)
    return hi, mid, lo


def _exact_l(c, x):
    hi, mid, lo = _split3(x)
    f = lambda p: jnp.dot(c, p, preferred_element_type=F32)
    return f(hi) + f(mid) + f(lo)


def _exact_r(x, c):
    hi, mid, lo = _split3(x)
    f = lambda p: jnp.dot(p, c, preferred_element_type=F32)
    return f(hi) + f(mid) + f(lo)


def _sigmoid(x):
    return jax.nn.sigmoid(x)


def _softplus(x):
    return jnp.maximum(x, 0.0) + jnp.log1p(jnp.exp(-jnp.abs(x)))


def _silu_and_grad(x):
    s = _sigmoid(x)
    return x * s, s * (1.0 + x * (1.0 - s))


def _shift_down(ext, s):
    if s == 0:
        return ext[HALO:, :]
    return pltpu.roll(ext, s, 0)[HALO:, :]


def _shift_up(ext, s):
    if s == 0:
        return ext[:CHUNK, :]
    return pltpu.roll(ext, ext.shape[0] - s, 0)[:CHUNK, :]


def _by_pool_group(lane, a2, a4, a8, a16):
    return jnp.where(lane < 128, a2, jnp.where(lane < 256, a4, jnp.where(lane < 384, a8, a16)))


def _pool_inv_count(chunk_idx):
    row = lax.broadcasted_iota(jnp.int32, (CHUNK, D_POOL), 0)
    lane = lax.broadcasted_iota(jnp.int32, (CHUNK, D_POOL), 1)
    pos1 = jnp.maximum(chunk_idx * CHUNK + row - (PAD_ROWS - 1), 1)
    w = _by_pool_group(lane, 2, 4, 8, 16)
    return 1.0 / jnp.minimum(pos1, w).astype(F32), lane


def _pool_window_sums(u_ext, lane):
    s2 = u_ext + pltpu.roll(u_ext, 1, 0)
    s4 = s2 + pltpu.roll(s2, 2, 0)
    s8 = s4 + pltpu.roll(s4, 4, 0)
    s16 = s8 + pltpu.roll(s8, 8, 0)
    return _by_pool_group(lane, s2[HALO:], s4[HALO:], s8[HALO:], s16[HALO:])


def _pool_window_sums_ahead(q_ext, lane):
    n = q_ext.shape[0]
    r2 = q_ext + pltpu.roll(q_ext, n - 1, 0)
    r4 = r2 + pltpu.roll(r2, n - 2, 0)
    r8 = r4 + pltpu.roll(r4, n - 4, 0)
    r16 = r8 + pltpu.roll(r8, n - 8, 0)
    return _by_pool_group(lane, r2[:CHUNK], r4[:CHUNK], r8[:CHUNK], r16[:CHUNK])


def _conv_pre(ext, xbc, cw, cb):
    return (cb + cw[3:4, :] * xbc + cw[2:3, :] * _shift_down(ext, 1)
            + cw[1:2, :] * _shift_down(ext, 2) + cw[0:1, :] * _shift_down(ext, 3))


def _dt_and_cumsum(dtr, dt_bias, a_log, valid, tril):
    lane = lax.broadcasted_iota(jnp.int32, (CHUNK, 128), 1)
    head = lane < N_HEADS
    pre = dtr + dt_bias
    dt = jnp.where(valid & head, _softplus(pre), 0.0)
    a_row = jnp.where(head[0:1, :], -jnp.exp(a_log), 0.0)
    a_col = _exact_l(tril, dt * a_row)
    return dt, a_row, a_col, pre, head


def _decay(a_col, a_row_t, h, causal):
    seg = a_col[:, h:h + 1] - a_row_t[h:h + 1, :]
    return jnp.where(causal, jnp.exp(jnp.minimum(seg, 0.0)), 0.0)


def _ssd_chunk_fwd(xs, bm, cm, dt, a_col, s_prev, d_x, e_mat, et_mat):
    lane = lax.broadcasted_iota(jnp.int32, (CHUNK, 128), 1)
    rowi = lax.broadcasted_iota(jnp.int32, (CHUNK, CHUNK), 0)
    coli = lax.broadcasted_iota(jnp.int32, (CHUNK, CHUNK), 1)
    causal = rowi >= coli
    a_row_t = a_col.T
    ax = _exact_r(a_col, e_mat)
    dtx = _exact_r(dt, e_mat)
    xdt = xs * dtx
    ax_last = ax[CHUNK - 1:CHUNK, :]
    e_a = jnp.exp(ax)
    w_end = xdt * jnp.exp(ax_last - ax)
    cd_col = jnp.exp(_exact_l(et_mat, a_row_t)[:, CHUNK - 1:CHUNK])
    ys, s_new = [], []
    for g in range(N_GROUPS):
        gs = slice(g * GROUP_CH, (g + 1) * GROUP_CH)
        bg = bm[:, g * D_STATE:(g + 1) * D_STATE]
        cg = cm[:, g * D_STATE:(g + 1) * D_STATE]
        sg = s_prev[gs, :]
        cb = _dot_nt(cg, bg)
        y_off = _dot_nt(cg, sg) * e_a[:, gs]
        s_new.append(sg * cd_col[gs, :] + _dot_tn(w_end[:, gs], bg))
        for pr in range(3):
            c0 = g * GROUP_CH + pr * 128
            xdt_p = xdt[:, c0:c0 + 128]
            h0 = g * 6 + pr * 2
            y0 = _dot(cb * _decay(a_col, a_row_t, h0, causal), xdt_p)
            y1 = _dot(cb * _decay(a_col, a_row_t, h0 + 1, causal), xdt_p)
            ys.append(jnp.where(lane < HEAD_DIM, y0, y1) + y_off[:, pr * 128:(pr + 1) * 128])
    y = jnp.concatenate(ys, axis=1) + d_x * xs
    return y, jnp.concatenate(s_new, axis=0)


def _ssd_chunk_bwd(xs, bm, cm, dt, a_row, a_col, s_prev, ds_new, dy, d_x, e_mat, et_mat, triu):
    lane = lax.broadcasted_iota(jnp.int32, (CHUNK, 128), 1)
    sub = lax.broadcasted_iota(jnp.int32, (CHUNK, 128), 0)
    rowi = lax.broadcasted_iota(jnp.int32, (CHUNK, CHUNK), 0)
    coli = lax.broadcasted_iota(jnp.int32, (CHUNK, CHUNK), 1)
    causal = rowi >= coli
    a_row_t = a_col.T
    ax = _exact_r(a_col, e_mat)
    dtx = _exact_r(dt, e_mat)
    xdt = xs * dtx
    ax_last = ax[CHUNK - 1:CHUNK, :]
    e_a = jnp.exp(ax)
    dte = jnp.exp(ax_last - ax)
    w_end = xdt * dte
    cd_col = jnp.exp(_exact_l(et_mat, a_row_t)[:, CHUNK - 1:CHUNK])
    dye = dy * e_a

    dxdt, zc, ww_all, dbs, dcs, dsp, t1s = [], [], [], [], [], [], []
    zcol = jnp.zeros((CHUNK, 128), F32)
    zrow = jnp.zeros((128, CHUNK), F32)
    for g in range(N_GROUPS):
        gs = slice(g * GROUP_CH, (g + 1) * GROUP_CH)
        bg = bm[:, g * D_STATE:(g + 1) * D_STATE]
        cg = cm[:, g * D_STATE:(g + 1) * D_STATE]
        sg = s_prev[gs, :]
        dsg = ds_new[gs, :]
        cb = _dot_nt(cg, bg)
        cs = _dot_nt(cg, sg)
        dcg = _dot(dye[:, gs], sg)
        dsp.append(dsg * cd_col[gs, :] + _dot_tn(dye[:, gs], cg))
        dwg = _dot_nt(bg, dsg)
        dbg = _dot(w_end[:, gs], dsg)
        ww = dwg * w_end[:, gs]
        ww_all.append(ww)
        zc.append(dye[:, gs] * cs - ww)
        t1s.append(jnp.sum(dsg * sg, axis=1, keepdims=True) * cd_col[gs, :])
        dxdt_g = dwg * dte[:, gs]
        dcb = jnp.zeros((CHUNK, CHUNK), F32)
        pairs = []
        for pr in range(3):
            c0 = g * GROUP_CH + pr * 128
            xdt_p = xdt[:, c0:c0 + 128]
            dy_p = dy[:, c0:c0 + 128]
            acc = None
            for half in range(2):
                h = g * 6 + pr * 2 + half
                ld = _decay(a_col, a_row_t, h, causal)
                gm = cb * ld
                dym = jnp.where((lane < HEAD_DIM) if half == 0 else (lane >= HEAD_DIM), dy_p, 0.0)
                dg = _dot_nt(dym, xdt_p)
                dseg = dg * gm
                dcb = dcb + dg * ld
                t = _dot_tn(gm, dym)
                acc = t if acc is None else acc + t
                zcol = jnp.where(lane == h, jnp.sum(dseg, axis=1, keepdims=True), zcol)
                zrow = jnp.where(sub == h, jnp.sum(dseg, axis=0, keepdims=True), zrow)
            pairs.append(acc)
        dxdt.append(dxdt_g + jnp.concatenate(pairs, axis=1))
        dcs.append(dcg + _dot(dcb, bg))
        dbs.append(dbg + _dot_tn(dcb, cg))
    dxdt = jnp.concatenate(dxdt, axis=1)
    zc = jnp.concatenate(zc, axis=1)
    ww_all = jnp.concatenate(ww_all, axis=1)
    t1 = jnp.concatenate(t1s, axis=0)

    da_col = _exact_r(zc, et_mat) + zcol - zrow.T
    q_row = (jnp.sum(_exact_r(ww_all, et_mat), axis=0, keepdims=True)
             + jnp.sum(et_mat.astype(F32) * t1, axis=0, keepdims=True))
    da_col = da_col + jnp.where(sub == CHUNK - 1, q_row, 0.0)
    ddt = _exact_r(dxdt * xs, et_mat)
    dxs = dxdt * dtx + d_x * dy
    dd = jnp.sum(_exact_r(dy * xs, et_mat), axis=0, keepdims=True)
    rc = _exact_l(triu, da_col)
    ddt = ddt + rc * a_row
    da = jnp.sum(rc * dt, axis=0, keepdims=True)
    return (dxs, jnp.concatenate(dbs, axis=1), jnp.concatenate(dcs, axis=1), ddt, da, dd,
            jnp.concatenate(dsp, axis=0))


def _ssd_constants():
    ch = np.arange(D_SSM) // HEAD_DIM
    e = (np.arange(128)[:, None] == ch[None, :]).astype(np.float32)
    tril = np.tril(np.ones((CHUNK, CHUNK), np.float32))
    as_mx = lambda a: jnp.asarray(a, MXU_DTYPE)
    return as_mx(e), as_mx(e.T), as_mx(tril), as_mx(tril.T)


def _full(shape):
    nd = len(shape)
    return pl.BlockSpec(shape, lambda *_: (0,) * nd)


def _params(*sem):
    return pltpu.CompilerParams(dimension_semantics=sem, vmem_limit_bytes=VMEM_LIMIT)


def _in_proj(hp, w1, win):
    m = hp.shape[0]
    tm = 384

    def body(x_ref, w1_ref, win_hbm, hn_ref, proj_ref, win_v, sem):
        @pl.when(pl.program_id(0) == 0)
        def _():
            cp = pltpu.make_async_copy(win_hbm, win_v, sem)
            cp.start()
            cp.wait()

        x = x_ref[...]
        r = lax.rsqrt(jnp.mean(x * x, axis=-1, keepdims=True) + EPS)
        hn = _mx(x * r * w1_ref[...])
        hn_ref[...] = hn
        for j in range(0, PROJ_W, 512):
            w = min(512, PROJ_W - j)
            proj_ref[:, j:j + w] = jnp.dot(hn, win_v[:, j:j + w], preferred_element_type=F32)

    return pl.pallas_call(
        body, grid=(m // tm,), name="in_proj",
        in_specs=[pl.BlockSpec((tm, D_MODEL), lambda i: (i, 0)), _full((1, D_MODEL)), pl.BlockSpec(memory_space=pl.ANY)],
        out_specs=[pl.BlockSpec((tm, D_MODEL), lambda i: (i, 0)), pl.BlockSpec((tm, PROJ_W), lambda i: (i, 0))],
        out_shape=[jax.ShapeDtypeStruct((m, D_MODEL), MXU_DTYPE), jax.ShapeDtypeStruct((m, PROJ_W), F32)],
        scratch_shapes=[pltpu.VMEM((D_MODEL, PROJ_W), MXU_DTYPE), pltpu.SemaphoreType.DMA],
        compiler_params=_params("arbitrary"),
    )(hp, w1, win)


def _ffn_fwd_bwd(hp, y, tgt, wout, w2n, wff1, wff2, wfn, t_rows):
    m = hp.shape[0]
    tm = 256
    nb = m // t_rows
    nj = D_FF // 1024

    def body(hp_ref, y_ref, tgt_ref, w2n_ref, wfn_ref, wout_hbm, wff1_hbm, wff2_hbm,
             loss_ref, gwf_ref, gw2_ref, ff_ref, da_ref, hn2_ref, dh1_ref, dh2_ref, dy_ref,
             wout_v, wff1_v, wff2_v, a_s, sems):
        i = pl.program_id(0)

        @pl.when(i == 0)
        def _():
            cps = [pltpu.make_async_copy(s, d, sems.at[k])
                   for k, (s, d) in enumerate(((wout_hbm, wout_v), (wff1_hbm, wff1_v), (wff2_hbm, wff2_v)))]
            for cp in cps:
                cp.start()
            for cp in cps:
                cp.wait()
            loss_ref[...] = jnp.zeros_like(loss_ref)
            gwf_ref[...] = jnp.zeros_like(gwf_ref)
            gw2_ref[...] = jnp.zeros_like(gw2_ref)

        h1 = hp_ref[...] + jnp.dot(y_ref[...], wout_v[...], preferred_element_type=F32)
        r2 = lax.rsqrt(jnp.mean(h1 * h1, axis=-1, keepdims=True) + EPS)
        n2 = h1 * r2
        w2n_row = w2n_ref[...]
        hn2 = _mx(n2 * w2n_row)
        hn2_ref[...] = hn2
        h2 = h1
        for j in range(nj):
            js = slice(j * 1024, (j + 1) * 1024)
            a = jnp.dot(hn2, wff1_v[:, js], preferred_element_type=F32)
            a_s[:, js] = a
            ra = jnp.maximum(a, 0.0)
            ff = _mx(ra * ra)
            ff_ref[:, js] = ff
            h2 = h2 + jnp.dot(ff, wff2_v[js, :], preferred_element_type=F32)

        r3 = lax.rsqrt(jnp.mean(h2 * h2, axis=-1, keepdims=True) + EPS)
        n3 = h2 * r3
        wf_row = wfn_ref[...]
        err = n3 * wf_row - tgt_ref[...]
        row = i * tm + lax.broadcasted_iota(jnp.int32, (tm, 1), 0)
        tok = row < 0
        for b in range(nb):
            tok = tok | ((row >= b * t_rows + LEAD) & (row < (b + 1) * t_rows))
        tokf = tok.astype(F32)
        loss_ref[...] += 0.5 * jnp.sum(jnp.mean(err * err, axis=-1, keepdims=True) * tokf)
        dout = err * (tokf / D_MODEL)
        gwf_ref[...] += jnp.sum(dout * n3, axis=0, keepdims=True)
        dn3 = dout * wf_row
        dh2 = r3 * (dn3 - n3 * jnp.mean(dn3 * n3, axis=-1, keepdims=True))
        dh2m = _mx(dh2)
        dh2_ref[...] = dh2m

        dhn2 = jnp.zeros((tm, D_MODEL), F32)
        for j in range(nj):
            js = slice(j * 1024, (j + 1) * 1024)
            dff = lax.dot_general(dh2m, wff2_v[js, :], (((1,), (1,)), ((), ())), preferred_element_type=F32)
            da = _mx(dff * (2.0 * jnp.maximum(a_s[:, js], 0.0)))
            da_ref[:, js] = da
            dhn2 = dhn2 + lax.dot_general(da, wff1_v[:, js], (((1,), (1,)), ((), ())), preferred_element_type=F32)
        gw2_ref[...] += jnp.sum(dhn2 * n2, axis=0, keepdims=True)
        dn2 = dhn2 * w2n_row
        dh1 = dh2 + r2 * (dn2 - n2 * jnp.mean(dn2 * n2, axis=-1, keepdims=True))
        dh1_ref[...] = dh1
        dy_ref[...] = lax.dot_general(_mx(dh1), wout_v[...], (((1,), (1,)), ((), ())), preferred_element_type=F32)

    rows = lambda w: pl.BlockSpec((tm, w), lambda i: (i, 0))
    hbm = pl.BlockSpec(memory_space=pl.ANY)
    return pl.pallas_call(
        body, grid=(m // tm,), name="ffn_fwd_bwd",
        in_specs=[rows(D_MODEL), rows(D_MIX), rows(D_MODEL), _full((1, D_MODEL)), _full((1, D_MODEL)), hbm, hbm, hbm],
        out_specs=[_full((1, 128)), _full((1, D_MODEL)), _full((1, D_MODEL)), rows(D_FF), rows(D_FF), rows(D_MODEL),
                   rows(D_MODEL), rows(D_MODEL), rows(D_MIX)],
        out_shape=[jax.ShapeDtypeStruct((1, 128), F32), jax.ShapeDtypeStruct((1, D_MODEL), F32),
                   jax.ShapeDtypeStruct((1, D_MODEL), F32), jax.ShapeDtypeStruct((m, D_FF), MXU_DTYPE),
                   jax.ShapeDtypeStruct((m, D_FF), MXU_DTYPE), jax.ShapeDtypeStruct((m, D_MODEL), MXU_DTYPE),
                   jax.ShapeDtypeStruct((m, D_MODEL), F32), jax.ShapeDtypeStruct((m, D_MODEL), MXU_DTYPE),
                   jax.ShapeDtypeStruct((m, D_MIX), F32)],
        scratch_shapes=[pltpu.VMEM((D_MIX, D_MODEL), MXU_DTYPE), pltpu.VMEM((D_MODEL, D_FF), MXU_DTYPE),
                        pltpu.VMEM((D_FF, D_MODEL), MXU_DTYPE), pltpu.VMEM((tm, D_FF), F32),
                        pltpu.SemaphoreType.DMA((3,))],
        compiler_params=_params("arbitrary"),
    )(hp, y, tgt, w2n, wfn, wout, wff1, wff2)


def _in_proj_bwd(dproj, hp, dh1, w1, win, t_rows):
    m = hp.shape[0]
    tm = 384
    tiles_per_seq = t_rows // tm

    def body(dp_ref, hp_ref, dh1_ref, w1_ref, win_hbm, dh0_ref, gw1_ref, gmeta_ref, win_v, sem):
        i = pl.program_id(0)

        @pl.when(i == 0)
        def _():
            cp = pltpu.make_async_copy(win_hbm, win_v, sem)
            cp.start()
            cp.wait()
            gw1_ref[...] = jnp.zeros_like(gw1_ref)
            gmeta_ref[...] = jnp.zeros_like(gmeta_ref)

        dhn = lax.dot_general(dp_ref[...], win_v[...], (((1,), (1,)), ((), ())), preferred_element_type=F32)
        x = hp_ref[...]
        r = lax.rsqrt(jnp.mean(x * x, axis=-1, keepdims=True) + EPS)
        n = x * r
        gw1_ref[...] += jnp.sum(dhn * n, axis=0, keepdims=True)
        dn = dhn * w1_ref[...]
        dh0 = dh1_ref[...] + r * (dn - n * jnp.mean(dn * n, axis=-1, keepdims=True))
        dh0_ref[...] = dh0

        @pl.when(i % tiles_per_seq == 0)
        def _():
            gmeta_ref[...] += dh0[PAD_ROWS:LEAD, :]

    rows = lambda w: pl.BlockSpec((tm, w), lambda i: (i, 0))
    return pl.pallas_call(
        body, grid=(m // tm,), name="in_proj_bwd",
        in_specs=[rows(PROJ_W), rows(D_MODEL), rows(D_MODEL), _full((1, D_MODEL)), pl.BlockSpec(memory_space=pl.ANY)],
        out_specs=[rows(D_MODEL), _full((1, D_MODEL)), _full((N_META, D_MODEL))],
        out_shape=[jax.ShapeDtypeStruct((m, D_MODEL), F32), jax.ShapeDtypeStruct((1, D_MODEL), F32),
                   jax.ShapeDtypeStruct((N_META, D_MODEL), F32)],
        scratch_shapes=[pltpu.VMEM((D_MODEL, PROJ_W), MXU_DTYPE), pltpu.SemaphoreType.DMA],
        compiler_params=_params("arbitrary"),
    )(dproj, hp, dh1, w1, win)


def _tn_matmul(a, b, name, tka, tkm=384, tn=512):
    m, ka = a.shape
    nb = b.shape[1]
    n_steps = m // tkm

    def body(a_ref, b_ref, o_ref):
        @pl.when(pl.program_id(1) == 0)
        def _():
            o_ref[...] = jnp.zeros_like(o_ref)

        at = _mx(a_ref[...])
        for j in range(0, nb, tn):
            w = min(tn, nb - j)
            o_ref[:, j:j + w] += lax.dot_general(at, _mx(b_ref[:, j:j + w]), (((0,), (0,)), ((), ())),
                                                 preferred_element_type=F32)

    return pl.pallas_call(
        body, grid=(ka // tka, n_steps), name=name,
        in_specs=[pl.BlockSpec((tkm, tka), lambda i, k: (k, i)), pl.BlockSpec((tkm, nb), lambda i, k: (k, 0))],
        out_specs=pl.BlockSpec((tka, nb), lambda i, k: (i, 0)),
        out_shape=jax.ShapeDtypeStruct((ka, nb), F32),
        compiler_params=_params("arbitrary", "arbitrary"),
    )(a, b)


def _mixer_fwd(proj, cw, cb, dt_bias, a_log, d_x, nw, pool_w, pool_scale, nb):
    m = proj.shape[0]
    nc = m // nb // CHUNK
    e_mat, et_mat, tril, _ = _ssd_constants()

    def body(p_ref, cw_ref, cb_ref, dtb_ref, alog_ref, dx_ref, nw_ref, pw_ref, ps_ref, e_ref, et_ref, tril_ref,
             y_ref, ypre_ref, st_ref, xtail, utail, state):
        c = pl.program_id(1)

        @pl.when(c == 0)
        def _():
            xtail[...] = jnp.zeros_like(xtail)
            utail[...] = jnp.zeros_like(utail)
            state[...] = jnp.zeros_like(state)

        valid = (c > 0) | (lax.broadcasted_iota(jnp.int32, (CHUNK, 1), 0) >= PAD_ROWS)

        u = p_ref[:, 0:D_POOL]
        inv_cnt, lane = _pool_inv_count(c)
        win = _pool_window_sums(jnp.concatenate([utail[...], u], axis=0), lane)
        utail[...] = u[CHUNK - HALO:, :]
        pooled = win * inv_cnt - u
        mixed = jnp.concatenate(
            [_dot(pooled[:, g * 128:(g + 1) * 128], pw_ref[g]) for g in range(len(POOL_WINDOWS))], axis=1)
        y_ref[:, 0:D_POOL] = _mx(mixed * ps_ref[...])

        xbc = p_ref[:, OFF_X:OFF_X + D_XBC]
        pre = _conv_pre(jnp.concatenate([xtail[...], xbc], axis=0), xbc, cw_ref[...], cb_ref[...])
        xtail[...] = xbc[CHUNK - HALO:, :]
        xc = pre * _sigmoid(pre)
        dt, _, a_col, _, _ = _dt_and_cumsum(p_ref[:, OFF_DT:OFF_DT + 128], dtb_ref[...], alog_ref[...], valid,
                                            tril_ref[...])
        s_prev = state[...]
        st_ref[0] = s_prev
        yp, s_new = _ssd_chunk_fwd(xc[:, 0:D_SSM], xc[:, D_SSM:D_SSM + 512], xc[:, D_SSM + 512:], dt, a_col, s_prev,
                                   dx_ref[...], e_ref[...], et_ref[...])
        state[...] = s_new
        ypre_ref[...] = yp
        z = p_ref[:, OFF_Z:OFF_Z + D_SSM]
        yz = yp * (z * _sigmoid(z))
        outs = []
        for g in range(N_GROUPS):
            gs = slice(g * GROUP_CH, (g + 1) * GROUP_CH)
            r = lax.rsqrt(jnp.mean(yz[:, gs] * yz[:, gs], axis=-1, keepdims=True) + EPS)
            outs.append(yz[:, gs] * r)
        y_ref[:, D_POOL:] = _mx(jnp.concatenate(outs, axis=1) * nw_ref[...])

    blk = lambda w: pl.BlockSpec((CHUNK, w), lambda b, c: (b * nc + c, 0))
    return pl.pallas_call(
        body, grid=(nb, nc), name="mixer_fwd",
        in_specs=[blk(PROJ_W), _full((4, D_XBC)), _full((1, D_XBC)), _full((1, 128)), _full((1, 128)),
                  _full((1, D_SSM)), _full((1, D_SSM)), _full((4, 128, 128)), _full((1, D_POOL)),
                  _full((128, D_SSM)), _full((D_SSM, 128)), _full((CHUNK, CHUNK))],
        out_specs=[blk(D_MIX), blk(D_SSM), pl.BlockSpec((1, D_SSM, D_STATE), lambda b, c: (b * nc + c, 0, 0))],
        out_shape=[jax.ShapeDtypeStruct((m, D_MIX), MXU_DTYPE), jax.ShapeDtypeStruct((m, D_SSM), F32),
                   jax.ShapeDtypeStruct((m // CHUNK, D_SSM, D_STATE), F32)],
        scratch_shapes=[pltpu.VMEM((HALO, D_XBC), F32), pltpu.VMEM((HALO, D_POOL), F32),
                        pltpu.VMEM((D_SSM, D_STATE), F32)],
        compiler_params=_params("arbitrary", "arbitrary"),
    )(proj, cw, cb, dt_bias, a_log, d_x, nw, pool_w, pool_scale, e_mat, et_mat, tril)


def _mixer_bwd(proj, dy, ypre, states, cw, cb, dt_bias, a_log, d_x, nw, pool_w, pool_scale, nb):
    m = proj.shape[0]
    nc = m // nb // CHUNK
    e_mat, et_mat, tril, triu = _ssd_constants()
    hb = CHUNK // HALO

    def body(p_ref, halo_ref, dy_ref, ypre_ref, st_ref, cw_ref, cb_ref, dtb_ref, alog_ref, dx_ref, nw_ref, pw_ref,
             ps_ref, e_ref, et_ref, tril_ref, triu_ref,
             dp_ref, gcw_ref, gcb_ref, gdtb_ref, galog_ref, gd_ref, gnw_ref, gpw_ref, gps_ref,
             ds_carry, dpre_next, dq_next):
        b = pl.program_id(0)
        cc = pl.program_id(1)
        c = nc - 1 - cc

        @pl.when((b == 0) & (cc == 0))
        def _():
            for r in (gcw_ref, gcb_ref, gdtb_ref, galog_ref, gd_ref, gnw_ref, gpw_ref, gps_ref):
                r[...] = jnp.zeros_like(r)

        @pl.when(cc == 0)
        def _():
            ds_carry[...] = jnp.zeros_like(ds_carry)
            dpre_next[...] = jnp.zeros_like(dpre_next)
            dq_next[...] = jnp.zeros_like(dq_next)

        valid = (c > 0) | (lax.broadcasted_iota(jnp.int32, (CHUNK, 1), 0) >= PAD_ROWS)
        first = c > 0

        u = p_ref[:, 0:D_POOL]
        u_halo = jnp.where(first, halo_ref[:, 0:D_POOL], 0.0)
        inv_cnt, lane = _pool_inv_count(c)
        pooled = _pool_window_sums(jnp.concatenate([u_halo, u], axis=0), lane) * inv_cnt - u
        dyp = dy_ref[:, 0:D_POOL]
        ps = ps_ref[...]
        dmixed = dyp * ps
        mixed, dpooled = [], []
        for g in range(len(POOL_WINDOWS)):
            gsl = slice(g * 128, (g + 1) * 128)
            pw = pw_ref[g]
            mixed.append(_dot(pooled[:, gsl], pw))
            dpooled.append(_dot_nt(dmixed[:, gsl], pw))
            gpw_ref[g] += _dot_tn(pooled[:, gsl], dmixed[:, gsl])
        gps_ref[...] += jnp.sum(dyp * jnp.concatenate(mixed, axis=1), axis=0, keepdims=True)
        dpooled = jnp.concatenate(dpooled, axis=1)
        dq = dpooled * inv_cnt
        du = _pool_window_sums_ahead(jnp.concatenate([dq, dq_next[...]], axis=0), lane) - dpooled
        dq_next[...] = dq[0:HALO, :]
        dp_ref[:, 0:D_POOL] = _mx(du)

        yp = ypre_ref[...]
        z = p_ref[:, OFF_Z:OFF_Z + D_SSM]
        sz, dsz = _silu_and_grad(z)
        yz = yp * sz
        do = dy_ref[:, D_POOL:]
        nw_row = nw_ref[...]
        dyz = []
        gnw = []
        for g in range(N_GROUPS):
            gs = slice(g * GROUP_CH, (g + 1) * GROUP_CH)
            r = lax.rsqrt(jnp.mean(yz[:, gs] * yz[:, gs], axis=-1, keepdims=True) + EPS)
            n = yz[:, gs] * r
            gnw.append(jnp.sum(do[:, gs] * n, axis=0, keepdims=True))
            dn = do[:, gs] * nw_row[:, gs]
            dyz.append(r * (dn - n * jnp.mean(dn * n, axis=-1, keepdims=True)))
        gnw_ref[...] += jnp.concatenate(gnw, axis=1)
        dyz = jnp.concatenate(dyz, axis=1)
        dp_ref[:, OFF_Z:OFF_Z + D_SSM] = _mx(dyz * yp * dsz)
        dyp_ssm = dyz * sz

        xbc = p_ref[:, OFF_X:OFF_X + D_XBC]
        x_halo = jnp.where(first, halo_ref[:, OFF_X:OFF_X + D_XBC], 0.0)
        ext = jnp.concatenate([x_halo, xbc], axis=0)
        cw = cw_ref[...]
        pre = _conv_pre(ext, xbc, cw, cb_ref[...])
        xc, dsilu = _silu_and_grad(pre)
        dtr = p_ref[:, OFF_DT:OFF_DT + 128]
        dt, a_row, a_col, dt_pre, head = _dt_and_cumsum(dtr, dtb_ref[...], alog_ref[...], valid, tril_ref[...])
        dxs, dbm, dcm, ddt, da, dd, ds_prev = _ssd_chunk_bwd(
            xc[:, 0:D_SSM], xc[:, D_SSM:D_SSM + 512], xc[:, D_SSM + 512:], dt, a_row, a_col, st_ref[0],
            ds_carry[...], dyp_ssm, dx_ref[...], e_ref[...], et_ref[...], triu_ref[...])
        ds_carry[...] = ds_prev
        gd_ref[...] += dd
        galog_ref[...] += da * a_row
        ddtr = jnp.where(valid & head, ddt * _sigmoid(dt_pre), 0.0)
        gdtb_ref[...] += jnp.sum(ddtr, axis=0, keepdims=True)
        dp_ref[:, OFF_DT:OFF_DT + 128] = _mx(ddtr)

        dpre = jnp.concatenate([dxs, dbm, dcm], axis=1) * dsilu
        gcb_ref[...] += jnp.sum(dpre, axis=0, keepdims=True)
        gcw_ref[...] += jnp.concatenate(
            [jnp.sum(dpre * _shift_down(ext, 3 - k), axis=0, keepdims=True) for k in range(4)], axis=0)
        dext = jnp.concatenate([dpre, dpre_next[...]], axis=0)
        dpre_next[...] = dpre[0:HALO, :]
        dp_ref[:, OFF_X:OFF_X + D_XBC] = _mx(cw[3:4, :] * dpre + cw[2:3, :] * _shift_up(dext, 1)
                                             + cw[1:2, :] * _shift_up(dext, 2) + cw[0:1, :] * _shift_up(dext, 3))

    blk = lambda w: pl.BlockSpec((CHUNK, w), lambda b, cc: (b * nc + nc - 1 - cc, 0))
    halo = pl.BlockSpec((HALO, PROJ_W), lambda b, cc: (jnp.maximum((b * nc + nc - 1 - cc) * hb - 1, 0), 0))
    return pl.pallas_call(
        body, grid=(nb, nc), name="mixer_bwd",
        in_specs=[blk(PROJ_W), halo, blk(D_MIX), blk(D_SSM),
                  pl.BlockSpec((1, D_SSM, D_STATE), lambda b, cc: (b * nc + nc - 1 - cc, 0, 0)),
                  _full((4, D_XBC)), _full((1, D_XBC)), _full((1, 128)), _full((1, 128)), _full((1, D_SSM)),
                  _full((1, D_SSM)), _full((4, 128, 128)), _full((1, D_POOL)),
                  _full((128, D_SSM)), _full((D_SSM, 128)), _full((CHUNK, CHUNK)), _full((CHUNK, CHUNK))],
        out_specs=[blk(PROJ_W), _full((4, D_XBC)), _full((1, D_XBC)), _full((1, 128)), _full((1, 128)), _full((1, 128)),
                   _full((1, D_SSM)), _full((4, 128, 128)), _full((1, D_POOL))],
        out_shape=[jax.ShapeDtypeStruct((m, PROJ_W), MXU_DTYPE), jax.ShapeDtypeStruct((4, D_XBC), F32),
                   jax.ShapeDtypeStruct((1, D_XBC), F32), jax.ShapeDtypeStruct((1, 128), F32),
                   jax.ShapeDtypeStruct((1, 128), F32), jax.ShapeDtypeStruct((1, 128), F32),
                   jax.ShapeDtypeStruct((1, D_SSM), F32), jax.ShapeDtypeStruct((4, 128, 128), F32),
                   jax.ShapeDtypeStruct((1, D_POOL), F32)],
        scratch_shapes=[pltpu.VMEM((D_SSM, D_STATE), F32), pltpu.VMEM((HALO, D_XBC), F32),
                        pltpu.VMEM((HALO, D_POOL), F32)],
        compiler_params=_params("arbitrary", "arbitrary"),
    )(proj, proj, dy, ypre, states, cw, cb, dt_bias, a_log, d_x, nw, pool_w, pool_scale, e_mat, et_mat, tril, triu)


def _pad128(v):
    v = v.reshape(1, -1).astype(F32)
    return jnp.pad(v, ((0, 0), (0, 128 - v.shape[1])))


def _local_step(x, tgt, meta, w):
    nb, seq, _ = x.shape
    t_rows = LEAD + seq
    lead = jnp.concatenate([jnp.zeros((PAD_ROWS, D_MODEL), F32), meta.astype(F32)], axis=0)
    hp = jnp.concatenate([jnp.broadcast_to(lead[None], (nb, LEAD, D_MODEL)), x], axis=1).reshape(nb * t_rows, D_MODEL)
    tgt_p = jnp.pad(tgt, ((0, 0), (LEAD, 0), (0, 0))).reshape(nb * t_rows, D_MODEL)

    dt_bias, a_log = _pad128(w["dt_bias"]), _pad128(w["a_log"])
    d_x = jnp.repeat(w["d_skip"].reshape(1, N_HEADS).astype(F32), HEAD_DIM, axis=1)

    hn1, proj = _in_proj(hp, w["norm_mix_w"], w["win"])
    y, ypre, states = _mixer_fwd(proj, w["conv_w"], w["conv_b"], dt_bias, a_log, d_x, w["ssm_norm_w"], w["pool_w"],
                                 w["pool_scale"], nb)
    loss, g_nf, g_nffn, ff, da, hn2, dh1, dh2, dy = _ffn_fwd_bwd(
        hp, y, tgt_p, w["wout"], w["norm_ffn_w"], w["wff1"], w["wff2"], w["norm_f_w"], t_rows)
    g_wff2 = _tn_matmul(ff, dh2, "grad_w_ff2", tka=1024)
    g_wff1 = _tn_matmul(hn2, da, "grad_w_ff1", tka=256)
    g_wout = _tn_matmul(y, dh1, "grad_w_out", tka=1024)
    dproj, g_cw, g_cb, g_dtb, g_alog, g_d, g_nw, g_pw, g_ps = _mixer_bwd(
        proj, dy, ypre, states, w["conv_w"], w["conv_b"], dt_bias, a_log, d_x, w["ssm_norm_w"], w["pool_w"],
        w["pool_scale"], nb)
    g_win = _tn_matmul(hn1, dproj, "grad_w_in", tka=512)
    dh0, g_nmix, g_meta = _in_proj_bwd(dproj, hp, dh1, w["norm_mix_w"], w["win"], t_rows)
    grads = dict(norm_mix_w=g_nmix, win=g_win, pool_w=g_pw, pool_scale=g_ps, conv_w=g_cw, conv_b=g_cb,
                 dt_bias=g_dtb[:, :N_HEADS], a_log=g_alog[:, :N_HEADS], d_skip=g_d[:, :N_HEADS], ssm_norm_w=g_nw,
                 wout=g_wout, norm_ffn_w=g_nffn, wff1=g_wff1, wff2=g_wff2, norm_f_w=g_nf)
    return loss[0, 0], dh0.reshape(nb, t_rows, D_MODEL), g_meta, grads


MESH_IDS = pl.DeviceIdType.MESH
_HBM = pl.BlockSpec(memory_space=pltpu.HBM)


def _coords():
    return lax.axis_index("x"), lax.axis_index("y"), lax.axis_index("c")


def _other_chips(x, y):
    return [(1 - x, y), (x, 1 - y), (1 - x, 1 - y)]


def _weight_gather(shards):
    n = len(shards)

    def body(*refs):
        ins, outs = refs[:n], refs[n:2 * n]
        send_sems, recv_sems, local_sems = refs[2 * n:]
        x, y, c = _coords()
        me, sibling = (x, y, c), (x, y, 1 - c)
        chips = _other_chips(x, y)

        def copy(t, k, block, to, src=None):
            dst = outs[t].at[4 * block[0] + 2 * block[1] + block[2]]
            return pltpu.make_async_remote_copy(
                src_ref=dst if src is None else src, dst_ref=dst, send_sem=send_sems.at[t * 7 + k],
                recv_sem=recv_sems.at[t * 7 + k], device_id=to, device_id_type=MESH_IDS)

        mine = [pltpu.make_async_copy(ins[t], outs[t].at[4 * x + 2 * y + c], local_sems.at[t]) for t in range(n)]
        for cp in mine:
            cp.start()
        first = []
        for t in range(n):
            first.append(copy(t, 0, me, sibling, src=ins[t]))
            first += [copy(t, 1 + j, me, (*chip, c), src=ins[t]) for j, chip in enumerate(chips)]
        for cp in first:
            cp.start()
        passed = []
        for j, chip in enumerate(chips):
            for t in range(n):
                copy(t, 1 + j, (*chip, c), me).wait_recv()
                cp = copy(t, 4 + j, (*chip, c), sibling)
                cp.start()
                passed.append(cp)
        for t in range(n):
            copy(t, 0, sibling, me).wait_recv()
            for j, chip in enumerate(chips):
                copy(t, 4 + j, (*chip, 1 - c), me).wait_recv()
        for cp in first + passed:
            cp.wait_send()
        for cp in mine:
            cp.wait()

    return pl.pallas_call(
        body, name="weight_gather",
        in_specs=[_HBM] * n, out_specs=[_HBM] * n,
        out_shape=[jax.ShapeDtypeStruct((N_DEV,) + s.shape, s.dtype) for s in shards],
        scratch_shapes=[pltpu.SemaphoreType.DMA((7 * n,)), pltpu.SemaphoreType.DMA((7 * n,)),
                        pltpu.SemaphoreType.DMA((n,))],
    )(*shards)


def _grad_exchange_d2d(gs, small):
    n = len(gs)

    def body(*refs):
        ins, small_in = refs[:n], refs[n]
        got, small_got = refs[n + 1:2 * n + 1], refs[2 * n + 1]
        send_sems, recv_sems = refs[2 * n + 2:]
        x, y, c = _coords()
        sibling = (x, y, 1 - c)
        remote = []
        for t in range(n):
            for k in range(4):
                remote.append(pltpu.make_async_remote_copy(
                    src_ref=ins[t].at[k, 1 - c], dst_ref=got[t].at[k], send_sem=send_sems.at[t * 4 + k],
                    recv_sem=recv_sems.at[t * 4 + k], device_id=sibling, device_id_type=MESH_IDS))
        remote.append(pltpu.make_async_remote_copy(
            src_ref=small_in, dst_ref=small_got, send_sem=send_sems.at[4 * n], recv_sem=recv_sems.at[4 * n],
            device_id=sibling, device_id_type=MESH_IDS))
        for cp in remote:
            cp.start()
        for cp in remote:
            cp.wait_recv()
        for cp in remote:
            cp.wait_send()

    blocks = [jax.ShapeDtypeStruct((4,) + g.shape[2:], g.dtype) for g in gs]
    outs = pl.pallas_call(
        body, name="grad_exchange_d2d",
        in_specs=[_HBM] * (n + 1), out_specs=[_HBM] * (n + 1),
        out_shape=blocks + [jax.ShapeDtypeStruct(small.shape, small.dtype)],
        scratch_shapes=[pltpu.SemaphoreType.DMA((4 * n + 1,)), pltpu.SemaphoreType.DMA((4 * n + 1,))],
    )(*gs, small)
    return outs[:n], outs[n]


def _grad_exchange_ici(ps, small):
    arrs = list(ps) + [small]
    n = len(arrs)

    def body(*refs):
        ins, outs = refs[:n], refs[n:2 * n]
        send_sems, recv_sems, local_sems = refs[2 * n:]
        x, y, c = _coords()
        my_chip = 2 * x + y
        chips = _other_chips(x, y)
        src_of = lambda t, k: ins[t] if t == n - 1 else ins[t].at[k]
        local = [pltpu.make_async_copy(src_of(t, my_chip), outs[t].at[my_chip], local_sems.at[t]) for t in range(n)]
        remote = []
        for t in range(n):
            for j, (cx, cy) in enumerate(chips):
                remote.append(pltpu.make_async_remote_copy(
                    src_ref=src_of(t, 2 * cx + cy), dst_ref=outs[t].at[my_chip], send_sem=send_sems.at[t * 3 + j],
                    recv_sem=recv_sems.at[t * 3 + j], device_id=(cx, cy, c), device_id_type=MESH_IDS))
        for cp in remote + local:
            cp.start()
        for t in range(n):
            for j, (cx, cy) in enumerate(chips):
                slot = outs[t].at[2 * cx + cy]
                pltpu.make_async_remote_copy(
                    src_ref=slot, dst_ref=slot, send_sem=send_sems.at[t * 3 + j], recv_sem=recv_sems.at[t * 3 + j],
                    device_id=(cx, cy, c), device_id_type=MESH_IDS).wait_recv()
        for cp in remote:
            cp.wait_send()
        for cp in local:
            cp.wait()

    outs = pl.pallas_call(
        body, name="grad_exchange_ici",
        in_specs=[_HBM] * n, out_specs=[_HBM] * n,
        out_shape=[jax.ShapeDtypeStruct(p.shape, p.dtype) for p in ps]
        + [jax.ShapeDtypeStruct((4,) + small.shape, small.dtype)],
        scratch_shapes=[pltpu.SemaphoreType.DMA((3 * n,)), pltpu.SemaphoreType.DMA((3 * n,)),
                        pltpu.SemaphoreType.DMA((n,))],
    )(*arrs)
    return outs[:n - 1], outs[n - 1]


def _row_tile(rows, cols, n_arrays):
    budget = 24 * 1024 * 1024
    padded = -(-cols // 128) * 128
    step = 16 if rows % 16 == 0 else 8
    tr = max(step, budget // (n_arrays * 2 * 4 * padded) // step * step)
    while rows % tr:
        tr -= step
    return tr


def _chip_sum(g4, got, core, name):
    _, _, rows, cols = g4.shape
    tr = _row_tile(rows, cols, 3)

    def body(c_ref, a_ref, b_ref, o_ref):
        o_ref[...] = (a_ref[...] + b_ref[...]).astype(o_ref.dtype)

    grid_spec = pltpu.PrefetchScalarGridSpec(
        num_scalar_prefetch=1, grid=(4, rows // tr),
        in_specs=[pl.BlockSpec((None, None, tr, cols), lambda k, i, c: (k, c[0], i, 0)),
                  pl.BlockSpec((None, tr, cols), lambda k, i, c: (k, i, 0))],
        out_specs=pl.BlockSpec((None, tr, cols), lambda k, i, c: (k, i, 0)))
    return pl.pallas_call(body, grid_spec=grid_spec, name=name,
                          out_shape=jax.ShapeDtypeStruct((4, rows, cols), MXU_DTYPE),
                          compiler_params=_params("arbitrary", "arbitrary"))(core, g4, got)


def _add(a, b, name):
    shape = a.shape
    a2, b2 = a.reshape(-1, shape[-1]), b.reshape(-1, shape[-1])
    rows, cols = a2.shape
    tr = _row_tile(rows, cols, 3)

    def body(a_ref, b_ref, o_ref):
        o_ref[...] = a_ref[...] + b_ref[...]

    blk = pl.BlockSpec((tr, cols), lambda i: (i, 0))
    return pl.pallas_call(body, grid=(rows // tr,), name=name, in_specs=[blk, blk], out_specs=blk,
                          out_shape=jax.ShapeDtypeStruct(a2.shape, a.dtype),
                          compiler_params=_params("arbitrary"))(a2, b2).reshape(shape)


def _sum4(parts, name):
    _, rows, cols = parts.shape
    tr = _row_tile(rows, cols, 5)

    def body(p_ref, o_ref):
        o_ref[...] = ((p_ref[0] + p_ref[1]) + p_ref[2]) + p_ref[3]

    return pl.pallas_call(body, grid=(rows // tr,), name=name,
                          in_specs=[pl.BlockSpec((4, tr, cols), lambda i: (0, i, 0))],
                          out_specs=pl.BlockSpec((tr, cols), lambda i: (i, 0)),
                          out_shape=jax.ShapeDtypeStruct((rows, cols), parts.dtype),
                          compiler_params=_params("arbitrary"))(parts)


def _adamw_math(w, g, m, v):
    m2 = ADAM_B1 * m + (1.0 - ADAM_B1) * g
    v2 = ADAM_B2 * v + (1.0 - ADAM_B2) * (g * g)
    m_hat = m2 / (1.0 - ADAM_B1 ** ADAM_STEP)
    v_hat = v2 / (1.0 - ADAM_B2 ** ADAM_STEP)
    delta = -ADAM_LR * (m_hat / (jnp.sqrt(v_hat) + ADAM_EPS) + ADAM_WD * w)
    return delta, m2, v2


def _adamw(parts, w, m, v, name):
    rows, cols = w.shape
    summed = parts.ndim == 3
    tr = _row_tile(rows, cols, 11 if summed else 8)

    def body(p_ref, w_ref, m_ref, v_ref, g_ref, d_ref, m2_ref, v2_ref):
        part = lambda k: p_ref[k].astype(F32)
        g = ((part(0) + part(1)) + part(2)) + part(3) if summed else p_ref[...]
        d, m2, v2 = _adamw_math(w_ref[...], g, m_ref[...], v_ref[...])
        g_ref[...] = g
        d_ref[...] = d
        m2_ref[...] = m2
        v2_ref[...] = v2

    blk = pl.BlockSpec((tr, cols), lambda i: (i, 0))
    pblk = pl.BlockSpec((4, tr, cols), lambda i: (0, i, 0)) if summed else blk
    out = jax.ShapeDtypeStruct((rows, cols), F32)
    return pl.pallas_call(body, grid=(rows // tr,), name=name, in_specs=[pblk, blk, blk, blk], out_specs=[blk] * 4,
                          out_shape=[out] * 4, compiler_params=_params("arbitrary"))(parts, w, m, v)


def _pack(arrays):
    rows = []
    for a in arrays:
        flat = a.astype(F32).reshape(-1)
        rows.append(jnp.pad(flat, (0, -flat.shape[0] % 128)).reshape(-1, 128))
    out = jnp.concatenate(rows, axis=0)
    return jnp.pad(out, ((0, -out.shape[0] % 8), (0, 0)))


def _unpack(pack, shapes):
    out, r = [], 0
    for s in shapes:
        n = int(np.prod(s))
        nr = -(-n // 128)
        out.append(pack[r:r + nr].reshape(-1)[:n].reshape(s))
        r += nr
    return out


_WEIGHTS = ["meta", "norm_mix_w", "w_in", "pool_w", "pool_scale", "conv_w", "conv_b", "dt_bias", "a_log", "d_skip",
            "ssm_norm_w", "w_out", "norm_ffn_w", "w_ff1", "w_ff2", "norm_f_w"]
_BIG = ["w_in", "w_out", "w_ff1", "w_ff2"]
_SMALL = [n for n in _WEIGHTS if n not in _BIG]


def kernel(x, meta, norm_mix_w, w_in, pool_w, pool_scale, conv_w, conv_b, dt_bias, a_log, d_skip, ssm_norm_w, w_out, norm_ffn_w, w_ff1, w_ff2, norm_f_w, loss_target, m_meta, m_norm_mix_w, m_w_in, m_pool_w, m_pool_scale, m_conv_w, m_conv_b, m_dt_bias, m_a_log, m_d_skip, m_ssm_norm_w, m_w_out, m_norm_ffn_w, m_w_ff1, m_w_ff2, m_norm_f_w, v_meta, v_norm_mix_w, v_w_in, v_pool_w, v_pool_scale, v_conv_w, v_conv_b, v_dt_bias, v_a_log, v_d_skip, v_ssm_norm_w, v_w_out, v_norm_ffn_w, v_w_ff1, v_w_ff2, v_norm_f_w):
    wts = dict(meta=meta, norm_mix_w=norm_mix_w, w_in=w_in, pool_w=pool_w, pool_scale=pool_scale, conv_w=conv_w,
               conv_b=conv_b, dt_bias=dt_bias, a_log=a_log, d_skip=d_skip, ssm_norm_w=ssm_norm_w, w_out=w_out,
               norm_ffn_w=norm_ffn_w, w_ff1=w_ff1, w_ff2=w_ff2, norm_f_w=norm_f_w)
    mom1 = dict(zip(_WEIGHTS, (m_meta, m_norm_mix_w, m_w_in, m_pool_w, m_pool_scale, m_conv_w, m_conv_b, m_dt_bias,
                               m_a_log, m_d_skip, m_ssm_norm_w, m_w_out, m_norm_ffn_w, m_w_ff1, m_w_ff2, m_norm_f_w)))
    mom2 = dict(zip(_WEIGHTS, (v_meta, v_norm_mix_w, v_w_in, v_pool_w, v_pool_scale, v_conv_w, v_conv_b, v_dt_bias,
                               v_a_log, v_d_skip, v_ssm_norm_w, v_w_out, v_norm_ffn_w, v_w_ff1, v_w_ff2, v_norm_f_w)))
    xi, yi, ci = _coords()
    dev = 4 * xi + 2 * yi + ci
    win_cols = w_in.shape[-1]
    cw_cols = conv_w.shape[-1]

    lead_pack = jnp.zeros((N_META, 512), F32)
    lead_pack = lead_pack.at[:, :128].set(meta).at[:4, 128:128 + cw_cols].set(conv_w[0])
    g_win, g_wout, g_wff1, g_wff2, g_lead = _weight_gather(
        [_mx(w_in[0]), _mx(w_out[0]), _mx(w_ff1[0]), _mx(w_ff2[0]), lead_pack])
    win_full = jnp.transpose(g_win, (1, 0, 2)).reshape(D_MODEL, N_DEV * win_cols)
    win_full = jnp.pad(win_full, ((0, 0), (0, PROJ_W - N_DEV * win_cols)))
    wff1_full = jnp.transpose(g_wff1, (1, 0, 2)).reshape(D_MODEL, D_FF)
    meta_full = jnp.transpose(g_lead[:, :, :128], (1, 0, 2)).reshape(N_META, D_MODEL)
    cw_full = jnp.transpose(g_lead[:, :4, 128:128 + cw_cols], (1, 0, 2)).reshape(4, D_XBC)

    full = dict(norm_mix_w=norm_mix_w, win=win_full, pool_w=pool_w[0], pool_scale=pool_scale, conv_w=cw_full,
                conv_b=conv_b, dt_bias=dt_bias, a_log=a_log, d_skip=d_skip, ssm_norm_w=ssm_norm_w,
                wout=g_wout.reshape(D_MIX, D_MODEL), norm_ffn_w=norm_ffn_w, wff1=wff1_full,
                wff2=g_wff2.reshape(D_FF, D_MODEL), norm_f_w=norm_f_w.reshape(1, D_MODEL))
    loss, dh0, g_meta, g = _local_step(x, loss_target, meta_full, full)

    owners = lambda a: a.reshape((4, 2) + a.shape[1:])
    big_parts = [
        owners(jnp.transpose(g["win"][:, :N_DEV * win_cols].reshape(D_MODEL, N_DEV, win_cols), (1, 0, 2))),
        owners(g["wout"].reshape(N_DEV, D_MIX // N_DEV, D_MODEL)),
        owners(jnp.transpose(g["wff1"].reshape(D_MODEL, N_DEV, D_FF // N_DEV), (1, 0, 2))),
        owners(g["wff2"].reshape(N_DEV, D_FF // N_DEV, D_MODEL)),
    ]
    small_names = ["meta", "norm_mix_w", "pool_w", "pool_scale", "conv_w", "conv_b", "dt_bias", "a_log", "d_skip",
                   "ssm_norm_w", "norm_ffn_w", "norm_f_w"]
    small_full = dict(g, meta=g_meta)
    small_pack = _pack([small_full[n] for n in small_names])
    got, small_got = _grad_exchange_d2d(big_parts, small_pack)
    core = jnp.reshape(ci, (1,)).astype(jnp.int32)
    chip_sums = [_chip_sum(a, b, core, "chip_sum_%d" % t) for t, (a, b) in enumerate(zip(big_parts, got))]
    small_chip = _add(small_pack, small_got, "chip_sum_small")
    parts, small_parts = _grad_exchange_ici(chip_sums, small_chip)
    small_sum = _sum4(small_parts, "small_sum")
    small_shapes = [small_full[n].shape for n in small_names]
    gs = dict(zip(small_names, _unpack(small_sum, small_shapes)))
    gs["meta"] = lax.dynamic_slice_in_dim(gs["meta"], dev * 128, 128, axis=1)
    gs["conv_w"] = lax.dynamic_slice_in_dim(gs["conv_w"], dev * cw_cols, cw_cols, axis=1)

    res = {}
    for t, n in enumerate(_BIG):
        shp = wts[n].shape
        res[n] = [o.reshape(shp) for o in _adamw(parts[t], wts[n][0], mom1[n][0], mom2[n][0], "adamw_" + n)]
    packs = [_pack([d[n] for n in _SMALL]) for d in (gs, wts, mom1, mom2)]
    small_out = _adamw(*packs, "adamw_small")
    shapes = [wts[n].shape for n in _SMALL]
    for k, o in enumerate(small_out):
        for n, a in zip(_SMALL, _unpack(o, shapes)):
            res.setdefault(n, [None] * 4)[k] = a

    loss = lax.psum(loss, ("x", "y", "c"))
    grad_x = dh0[:, LEAD:, :]
    return (loss, grad_x, *[res[n][0] for n in _WEIGHTS], *[res[n][1] for n in _WEIGHTS],
            *[res[n][2] for n in _WEIGHTS], *[res[n][3] for n in _WEIGHTS])
```

```python
import functools

import numpy as np
import jax
import jax.numpy as jnp
from jax import lax
from jax.experimental import pallas as pl
from jax.experimental.pallas import tpu as pltpu

F32 = jnp.float32
MXU_DTYPE = jnp.bfloat16

D_MODEL = 1024
D_POOL = 512
D_SSM = 1536
D_XBC = 2560
N_HEADS = 24
HEAD_DIM = 64
N_GROUPS = 4
GROUP_CH = D_SSM // N_GROUPS
D_STATE = 128
CHUNK = 128
N_META = 16
LEAD = CHUNK
PAD_ROWS = LEAD - N_META
D_MIX = D_POOL + D_SSM
D_FF = 4096
PROJ_W = 4736
OFF_Z = D_POOL
OFF_X = D_POOL + D_SSM
OFF_DT = OFF_X + D_XBC
D_IN_PROJ = OFF_DT + N_HEADS
POOL_WINDOWS = (2, 4, 8, 16)
HALO = 16
EPS = 1e-5
N_DEV = 8

ADAM_LR, ADAM_B1, ADAM_B2, ADAM_EPS, ADAM_WD, ADAM_STEP = 0.001, 0.9, 0.999, 1e-08, 0.01, 10

VMEM_LIMIT = 60 * 1024 * 1024


def _mx(a):
    return a.astype(MXU_DTYPE)


def _dot(a, b):
    return jnp.dot(_mx(a), _mx(b), preferred_element_type=F32)


def _dot_nt(a, b):
    return lax.dot_general(_mx(a), _mx(b), (((1,), (1,)), ((), ())), preferred_element_type=F32)


def _dot_tn(a, b):
    return lax.dot_general(_mx(a), _mx(b), (((0,), (0,)), ((), ())), preferred_element_type=F32)


def _split3(x):
    hi = x.astype(MXU_DTYPE)
    r = x - hi.astype(F32)
    mid = r.astype(MXU_DTYPE)
    lo = (r - mid.astype(F32)).astype(MXU_DTYPE)
    return hi, mid, lo


def _exact_l(c, x):
    hi, mid, lo = _split3(x)
    f = lambda p: jnp.dot(c, p, preferred_element_type=F32)
    return f(hi) + f(mid) + f(lo)


def _exact_r(x, c):
    hi, mid, lo = _split3(x)
    f = lambda p: jnp.dot(p, c, preferred_element_type=F32)
    return f(hi) + f(mid) + f(lo)


def _contract(x, c):
    hi = x.astype(MXU_DTYPE)
    lo = (x - hi.astype(F32)).astype(MXU_DTYPE)
    return jnp.dot(hi, c, preferred_element_type=F32) + jnp.dot(lo, c, preferred_element_type=F32)


def _sigmoid(x):
    return jax.nn.sigmoid(x)


def _softplus(x):
    return jnp.maximum(x, 0.0) + jnp.log1p(jnp.exp(-jnp.abs(x)))


def _silu_and_grad(x):
    s = _sigmoid(x)
    return x * s, s * (1.0 + x * (1.0 - s))


def _shift_down(ext, s):
    if s == 0:
        return ext[HALO:, :]
    return pltpu.roll(ext, s, 0)[HALO:, :]


def _shift_up(ext, s):
    if s == 0:
        return ext[:CHUNK, :]
    return pltpu.roll(ext, ext.shape[0] - s, 0)[:CHUNK, :]


def _by_pool_group(lane, a2, a4, a8, a16):
    return jnp.where(lane < 128, a2, jnp.where(lane < 256, a4, jnp.where(lane < 384, a8, a16)))


def _pool_inv_count(chunk_idx):
    row = lax.broadcasted_iota(jnp.int32, (CHUNK, D_POOL), 0)
    lane = lax.broadcasted_iota(jnp.int32, (CHUNK, D_POOL), 1)
    pos1 = jnp.maximum(chunk_idx * CHUNK + row - (PAD_ROWS - 1), 1)
    w = _by_pool_group(lane, 2, 4, 8, 16)
    return 1.0 / jnp.minimum(pos1, w).astype(F32), lane


def _pool_window_sums(u_ext, lane):
    s2 = u_ext + pltpu.roll(u_ext, 1, 0)
    s4 = s2 + pltpu.roll(s2, 2, 0)
    s8 = s4 + pltpu.roll(s4, 4, 0)
    s16 = s8 + pltpu.roll(s8, 8, 0)
    return _by_pool_group(lane, s2[HALO:], s4[HALO:], s8[HALO:], s16[HALO:])


def _pool_window_sums_ahead(q_ext, lane):
    n = q_ext.shape[0]
    r2 = q_ext + pltpu.roll(q_ext, n - 1, 0)
    r4 = r2 + pltpu.roll(r2, n - 2, 0)
    r8 = r4 + pltpu.roll(r4, n - 4, 0)
    r16 = r8 + pltpu.roll(r8, n - 8, 0)
    return _by_pool_group(lane, r2[:CHUNK], r4[:CHUNK], r8[:CHUNK], r16[:CHUNK])


def _conv_pre(ext, xbc, cw, cb):
    return (cb + cw[3:4, :] * xbc + cw[2:3, :] * _shift_down(ext, 1)
            + cw[1:2, :] * _shift_down(ext, 2) + cw[0:1, :] * _shift_down(ext, 3))


def _dt_and_cumsum(dtr, dt_bias, a_log, valid, tril):
    lane = lax.broadcasted_iota(jnp.int32, (CHUNK, 128), 1)
    head = lane < N_HEADS
    pre = dtr + dt_bias
    dt = jnp.where(valid & head, _softplus(pre), 0.0)
    a_row = jnp.where(head[0:1, :], -jnp.exp(a_log), 0.0)
    a_col = _exact_l(tril, dt * a_row)
    return dt, a_row, a_col, pre, head


def _decay(a_col, a_row_t, h, causal):
    seg = a_col[:, h:h + 1] - a_row_t[h:h + 1, :]
    return jnp.where(causal, jnp.exp(jnp.minimum(seg, 0.0)), 0.0)


def _ssd_chunk_fwd(xs, bm, cm, dt, a_col, s_prev, d_x, e_mat, et_mat):
    lane = lax.broadcasted_iota(jnp.int32, (CHUNK, 128), 1)
    rowi = lax.broadcasted_iota(jnp.int32, (CHUNK, CHUNK), 0)
    coli = lax.broadcasted_iota(jnp.int32, (CHUNK, CHUNK), 1)
    causal = rowi >= coli
    a_row_t = a_col.T
    ax = _exact_r(a_col, e_mat)
    dtx = _exact_r(dt, e_mat)
    xdt = xs * dtx
    ax_last = ax[CHUNK - 1:CHUNK, :]
    e_a = jnp.exp(ax)
    w_end = xdt * jnp.exp(ax_last - ax)
    cd_col = jnp.exp(_exact_l(et_mat, a_row_t)[:, CHUNK - 1:CHUNK])
    ys, s_new = [], []
    for g in range(N_GROUPS):
        gs = slice(g * GROUP_CH, (g + 1) * GROUP_CH)
        bg = bm[:, g * D_STATE:(g + 1) * D_STATE]
        cg = cm[:, g * D_STATE:(g + 1) * D_STATE]
        sg = s_prev[gs, :]
        cb = _dot_nt(cg, bg)
        y_off = _dot_nt(cg, sg) * e_a[:, gs]
        s_new.append(sg * cd_col[gs, :] + _dot_tn(w_end[:, gs], bg))
        for pr in range(3):
            c0 = g * GROUP_CH + pr * 128
            xdt_p = xdt[:, c0:c0 + 128]
            h0 = g * 6 + pr * 2
            y0 = _dot(cb * _decay(a_col, a_row_t, h0, causal), xdt_p)
            y1 = _dot(cb * _decay(a_col, a_row_t, h0 + 1, causal), xdt_p)
            ys.append(jnp.where(lane < HEAD_DIM, y0, y1) + y_off[:, pr * 128:(pr + 1) * 128])
    y = jnp.concatenate(ys, axis=1) + d_x * xs
    return y, jnp.concatenate(s_new, axis=0)


def _ssd_chunk_bwd(xs, bm, cm, dt, a_row, a_col, s_prev, ds_new, dy, d_x, e_mat, et_mat, triu):
    lane = lax.broadcasted_iota(jnp.int32, (CHUNK, 128), 1)
    sub = lax.broadcasted_iota(jnp.int32, (CHUNK, 128), 0)
    rowi = lax.broadcasted_iota(jnp.int32, (CHUNK, CHUNK), 0)
    coli = lax.broadcasted_iota(jnp.int32, (CHUNK, CHUNK), 1)
    causal = rowi >= coli
    a_row_t = a_col.T
    ax = _exact_r(a_col, e_mat)
    dtx = _exact_r(dt, e_mat)
    xdt = xs * dtx
    ax_last = ax[CHUNK - 1:CHUNK, :]
    e_a = jnp.exp(ax)
    dte = jnp.exp(ax_last - ax)
    w_end = xdt * dte
    cd_col = jnp.exp(_exact_l(et_mat, a_row_t)[:, CHUNK - 1:CHUNK])
    dye = dy * e_a

    dxdt, zc, ww_all, dbs, dcs, dsp, t1s = [], [], [], [], [], [], []
    zcol = jnp.zeros((CHUNK, 128), F32)
    zrow = jnp.zeros((128, CHUNK), F32)
    for g in range(N_GROUPS):
        gs = slice(g * GROUP_CH, (g + 1) * GROUP_CH)
        bg = bm[:, g * D_STATE:(g + 1) * D_STATE]
        cg = cm[:, g * D_STATE:(g + 1) * D_STATE]
        sg = s_prev[gs, :]
        dsg = ds_new[gs, :]
        cb = _dot_nt(cg, bg)
        cs = _dot_nt(cg, sg)
        dcg = _dot(dye[:, gs], sg)
        dsp.append(dsg * cd_col[gs, :] + _dot_tn(dye[:, gs], cg))
        dwg = _dot_nt(bg, dsg)
        dbg = _dot(w_end[:, gs], dsg)
        ww = dwg * w_end[:, gs]
        ww_all.append(ww)
        zc.append(dye[:, gs] * cs - ww)
        t1s.append(jnp.sum(dsg * sg, axis=1, keepdims=True) * cd_col[gs, :])
        dxdt_g = dwg * dte[:, gs]
        dcb = jnp.zeros((CHUNK, CHUNK), F32)
        pairs = []
        for pr in range(3):
            c0 = g * GROUP_CH + pr * 128
            xdt_p = xdt[:, c0:c0 + 128]
            dy_p = dy[:, c0:c0 + 128]
            acc = None
            for half in range(2):
                h = g * 6 + pr * 2 + half
                ld = _decay(a_col, a_row_t, h, causal)
                gm = cb * ld
                dym = jnp.where((lane < HEAD_DIM) if half == 0 else (lane >= HEAD_DIM), dy_p, 0.0)
                dg = _dot_nt(dym, xdt_p)
                dseg = dg * gm
                dcb = dcb + dg * ld
                t = _dot_tn(gm, dym)
                acc = t if acc is None else acc + t
                zcol = jnp.where(lane == h, jnp.sum(dseg, axis=1, keepdims=True), zcol)
                zrow = jnp.where(sub == h, jnp.sum(dseg, axis=0, keepdims=True), zrow)
            pairs.append(acc)
        dxdt.append(dxdt_g + jnp.concatenate(pairs, axis=1))
        dcs.append(dcg + _dot(dcb, bg))
        dbs.append(dbg + _dot_tn(dcb, cg))
    dxdt = jnp.concatenate(dxdt, axis=1)
    zc = jnp.concatenate(zc, axis=1)
    ww_all = jnp.concatenate(ww_all, axis=1)
    t1 = jnp.concatenate(t1s, axis=0)

    sub8 = lax.broadcasted_iota(jnp.int32, (8, D_SSM), 0)
    col_sums = jnp.where(sub8 == 0, jnp.sum(dy * xs, axis=0, keepdims=True),
                         jnp.where(sub8 == 1, jnp.sum(ww_all, axis=0, keepdims=True), 0.0))
    head_sums = _exact_r(col_sums, et_mat)
    dd = head_sums[0:1, :]
    q_row = head_sums[1:2, :] + jnp.sum(et_mat.astype(F32) * t1, axis=0, keepdims=True)
    da_col = _contract(zc, et_mat) + zcol - zrow.T
    da_col = da_col + jnp.where(sub == CHUNK - 1, q_row, 0.0)
    ddt = _contract(dxdt * xs, et_mat)
    dxs = dxdt * dtx + d_x * dy
    rc = _exact_l(triu, da_col)
    ddt = ddt + rc * a_row
    da = jnp.sum(rc * dt, axis=0, keepdims=True)
    return (dxs, jnp.concatenate(dbs, axis=1), jnp.concatenate(dcs, axis=1), ddt, da, dd,
            jnp.concatenate(dsp, axis=0))


def _ssd_constants():
    ch = np.arange(D_SSM) // HEAD_DIM
    e = (np.arange(128)[:, None] == ch[None, :]).astype(np.float32)
    tril = np.tril(np.ones((CHUNK, CHUNK), np.float32))
    as_mx = lambda a: jnp.asarray(a, MXU_DTYPE)
    return as_mx(e), as_mx(e.T), as_mx(tril), as_mx(tril.T)


def _full(shape):
    nd = len(shape)
    return pl.BlockSpec(shape, lambda *_: (0,) * nd)


def _params(*sem):
    return pltpu.CompilerParams(dimension_semantics=sem, vmem_limit_bytes=VMEM_LIMIT)


def _in_proj(hp, w1, win):
    m = hp.shape[0]
    tm = 384

    def body(x_ref, w1_ref, win_hbm, hn_ref, proj_ref, win_v, sem):
        @pl.when(pl.program_id(0) == 0)
        def _():
            cp = pltpu.make_async_copy(win_hbm, win_v, sem)
            cp.start()
            cp.wait()

        x = x_ref[...]
        r = lax.rsqrt(jnp.mean(x * x, axis=-1, keepdims=True) + EPS)
        hn = _mx(x * r * w1_ref[...])
        hn_ref[...] = hn
        for j in range(0, PROJ_W, 512):
            w = min(512, PROJ_W - j)
            proj_ref[:, j:j + w] = jnp.dot(hn, win_v[:, j:j + w], preferred_element_type=F32)

    return pl.pallas_call(
        body, grid=(m // tm,), name="in_proj",
        in_specs=[pl.BlockSpec((tm, D_MODEL), lambda i: (i, 0)), _full((1, D_MODEL)), pl.BlockSpec(memory_space=pl.ANY)],
        out_specs=[pl.BlockSpec((tm, D_MODEL), lambda i: (i, 0)), pl.BlockSpec((tm, PROJ_W), lambda i: (i, 0))],
        out_shape=[jax.ShapeDtypeStruct((m, D_MODEL), MXU_DTYPE), jax.ShapeDtypeStruct((m, PROJ_W), F32)],
        scratch_shapes=[pltpu.VMEM((D_MODEL, PROJ_W), MXU_DTYPE), pltpu.SemaphoreType.DMA],
        compiler_params=_params("arbitrary"),
    )(hp, w1, win)


def _ffn_fwd_bwd(hp, y, tgt, wout, w2n, wff1, wff2, wfn, t_rows):
    m = hp.shape[0]
    tm = 256
    nb = m // t_rows
    nj = D_FF // 1024

    def body(hp_ref, y_ref, tgt_ref, w2n_ref, wfn_ref, wout_hbm, wff1_hbm, wff2_hbm,
             loss_ref, gwf_ref, gw2_ref, ff_ref, da_ref, hn2_ref, dh1_ref, dh2_ref, dy_ref,
             wout_v, wff1_v, wff2_v, a_s, sems):
        i = pl.program_id(0)

        @pl.when(i == 0)
        def _():
            cps = [pltpu.make_async_copy(s, d, sems.at[k])
                   for k, (s, d) in enumerate(((wout_hbm, wout_v), (wff1_hbm, wff1_v), (wff2_hbm, wff2_v)))]
            for cp in cps:
                cp.start()
            for cp in cps:
                cp.wait()
            loss_ref[...] = jnp.zeros_like(loss_ref)
            gwf_ref[...] = jnp.zeros_like(gwf_ref)
            gw2_ref[...] = jnp.zeros_like(gw2_ref)

        h1 = hp_ref[...] + jnp.dot(y_ref[...], wout_v[...], preferred_element_type=F32)
        r2 = lax.rsqrt(jnp.mean(h1 * h1, axis=-1, keepdims=True) + EPS)
        n2 = h1 * r2
        w2n_row = w2n_ref[...]
        hn2 = _mx(n2 * w2n_row)
        hn2_ref[...] = hn2
        h2 = h1
        for j in range(nj):
            js = slice(j * 1024, (j + 1) * 1024)
            a = jnp.dot(hn2, wff1_v[:, js], preferred_element_type=F32)
            a_s[:, js] = a
            ra = jnp.maximum(a, 0.0)
            ff = _mx(ra * ra)
            ff_ref[:, js] = ff
            h2 = h2 + jnp.dot(ff, wff2_v[js, :], preferred_element_type=F32)

        r3 = lax.rsqrt(jnp.mean(h2 * h2, axis=-1, keepdims=True) + EPS)
        n3 = h2 * r3
        wf_row = wfn_ref[...]
        err = n3 * wf_row - tgt_ref[...]
        row = i * tm + lax.broadcasted_iota(jnp.int32, (tm, 1), 0)
        tok = row < 0
        for b in range(nb):
            tok = tok | ((row >= b * t_rows + LEAD) & (row < (b + 1) * t_rows))
        tokf = tok.astype(F32)
        loss_ref[...] += 0.5 * jnp.sum(jnp.mean(err * err, axis=-1, keepdims=True) * tokf)
        dout = err * (tokf / D_MODEL)
        gwf_ref[...] += jnp.sum(dout * n3, axis=0, keepdims=True)
        dn3 = dout * wf_row
        dh2 = r3 * (dn3 - n3 * jnp.mean(dn3 * n3, axis=-1, keepdims=True))
        dh2m = _mx(dh2)
        dh2_ref[...] = dh2m

        dhn2 = jnp.zeros((tm, D_MODEL), F32)
        for j in range(nj):
            js = slice(j * 1024, (j + 1) * 1024)
            dff = lax.dot_general(dh2m, wff2_v[js, :], (((1,), (1,)), ((), ())), preferred_element_type=F32)
            da = _mx(dff * (2.0 * jnp.maximum(a_s[:, js], 0.0)))
            da_ref[:, js] = da
            dhn2 = dhn2 + lax.dot_general(da, wff1_v[:, js], (((1,), (1,)), ((), ())), preferred_element_type=F32)
        gw2_ref[...] += jnp.sum(dhn2 * n2, axis=0, keepdims=True)
        dn2 = dhn2 * w2n_row
        dh1 = dh2 + r2 * (dn2 - n2 * jnp.mean(dn2 * n2, axis=-1, keepdims=True))
        dh1_ref[...] = dh1
        dy_ref[...] = lax.dot_general(_mx(dh1), wout_v[...], (((1,), (1,)), ((), ())), preferred_element_type=F32)

    rows = lambda w: pl.BlockSpec((tm, w), lambda i: (i, 0))
    hbm = pl.BlockSpec(memory_space=pl.ANY)
    return pl.pallas_call(
        body, grid=(m // tm,), name="ffn_fwd_bwd",
        in_specs=[rows(D_MODEL), rows(D_MIX), rows(D_MODEL), _full((1, D_MODEL)), _full((1, D_MODEL)), hbm, hbm, hbm],
        out_specs=[_full((1, 128)), _full((1, D_MODEL)), _full((1, D_MODEL)), rows(D_FF), rows(D_FF), rows(D_MODEL),
                   rows(D_MODEL), rows(D_MODEL), rows(D_MIX)],
        out_shape=[jax.ShapeDtypeStruct((1, 128), F32), jax.ShapeDtypeStruct((1, D_MODEL), F32),
                   jax.ShapeDtypeStruct((1, D_MODEL), F32), jax.ShapeDtypeStruct((m, D_FF), MXU_DTYPE),
                   jax.ShapeDtypeStruct((m, D_FF), MXU_DTYPE), jax.ShapeDtypeStruct((m, D_MODEL), MXU_DTYPE),
                   jax.ShapeDtypeStruct((m, D_MODEL), F32), jax.ShapeDtypeStruct((m, D_MODEL), MXU_DTYPE),
                   jax.ShapeDtypeStruct((m, D_MIX), F32)],
        scratch_shapes=[pltpu.VMEM((D_MIX, D_MODEL), MXU_DTYPE), pltpu.VMEM((D_MODEL, D_FF), MXU_DTYPE),
                        pltpu.VMEM((D_FF, D_MODEL), MXU_DTYPE), pltpu.VMEM((tm, D_FF), F32),
                        pltpu.SemaphoreType.DMA((3,))],
        compiler_params=_params("arbitrary"),
    )(hp, y, tgt, w2n, wfn, wout, wff1, wff2)


def _in_proj_bwd(dproj, hp, dh1, w1, win, t_rows):
    m = hp.shape[0]
    tm = 384
    tiles_per_seq = t_rows // tm

    def body(dp_ref, hp_ref, dh1_ref, w1_ref, win_hbm, dh0_ref, gw1_ref, gmeta_ref, win_v, sem):
        i = pl.program_id(0)

        @pl.when(i == 0)
        def _():
            cp = pltpu.make_async_copy(win_hbm, win_v, sem)
            cp.start()
            cp.wait()
            gw1_ref[...] = jnp.zeros_like(gw1_ref)
            gmeta_ref[...] = jnp.zeros_like(gmeta_ref)

        dhn = lax.dot_general(dp_ref[...], win_v[...], (((1,), (1,)), ((), ())), preferred_element_type=F32)
        x = hp_ref[...]
        r = lax.rsqrt(jnp.mean(x * x, axis=-1, keepdims=True) + EPS)
        n = x * r
        gw1_ref[...] += jnp.sum(dhn * n, axis=0, keepdims=True)
        dn = dhn * w1_ref[...]
        dh0 = dh1_ref[...] + r * (dn - n * jnp.mean(dn * n, axis=-1, keepdims=True))
        dh0_ref[...] = dh0

        @pl.when(i % tiles_per_seq == 0)
        def _():
            gmeta_ref[...] += dh0[PAD_ROWS:LEAD, :]

    rows = lambda w: pl.BlockSpec((tm, w), lambda i: (i, 0))
    return pl.pallas_call(
        body, grid=(m // tm,), name="in_proj_bwd",
        in_specs=[rows(PROJ_W), rows(D_MODEL), rows(D_MODEL), _full((1, D_MODEL)), pl.BlockSpec(memory_space=pl.ANY)],
        out_specs=[rows(D_MODEL), _full((1, D_MODEL)), _full((N_META, D_MODEL))],
        out_shape=[jax.ShapeDtypeStruct((m, D_MODEL), F32), jax.ShapeDtypeStruct((1, D_MODEL), F32),
                   jax.ShapeDtypeStruct((N_META, D_MODEL), F32)],
        scratch_shapes=[pltpu.VMEM((D_MODEL, PROJ_W), MXU_DTYPE), pltpu.SemaphoreType.DMA],
        compiler_params=_params("arbitrary"),
    )(dproj, hp, dh1, w1, win)


def _tn_matmul(a, b, name, tka, tkm=768, tn=512):
    m, ka = a.shape
    nb = b.shape[1]
    n_steps = m // tkm

    def body(a_ref, b_ref, o_ref):
        @pl.when(pl.program_id(1) == 0)
        def _():
            o_ref[...] = jnp.zeros_like(o_ref)

        at = _mx(a_ref[...])
        for j in range(0, nb, tn):
            w = min(tn, nb - j)
            o_ref[:, j:j + w] += lax.dot_general(at, _mx(b_ref[:, j:j + w]), (((0,), (0,)), ((), ())),
                                                 preferred_element_type=F32)

    return pl.pallas_call(
        body, grid=(ka // tka, n_steps), name=name,
        in_specs=[pl.BlockSpec((tkm, tka), lambda i, k: (k, i)), pl.BlockSpec((tkm, nb), lambda i, k: (k, 0))],
        out_specs=pl.BlockSpec((tka, nb), lambda i, k: (i, 0)),
        out_shape=jax.ShapeDtypeStruct((ka, nb), F32),
        compiler_params=_params("arbitrary", "arbitrary"),
    )(a, b)


def _mixer_fwd(proj, cw, cb, dt_bias, a_log, d_x, nw, pool_w, pool_scale, nb):
    m = proj.shape[0]
    nc = m // nb // CHUNK
    e_mat, et_mat, tril, _ = _ssd_constants()

    def body(p_ref, cw_ref, cb_ref, dtb_ref, alog_ref, dx_ref, nw_ref, pw_ref, ps_ref, e_ref, et_ref, tril_ref,
             y_ref, ypre_ref, pre_ref, st_ref, xtail, utail, state):
        c = pl.program_id(1)

        @pl.when(c == 0)
        def _():
            xtail[...] = jnp.zeros_like(xtail)
            utail[...] = jnp.zeros_like(utail)
            state[...] = jnp.zeros_like(state)

        valid = (c > 0) | (lax.broadcasted_iota(jnp.int32, (CHUNK, 1), 0) >= PAD_ROWS)

        u = p_ref[:, 0:D_POOL]
        inv_cnt, lane = _pool_inv_count(c)
        win = _pool_window_sums(jnp.concatenate([utail[...], u], axis=0), lane)
        utail[...] = u[CHUNK - HALO:, :]
        pooled = win * inv_cnt - u
        mixed = jnp.concatenate(
            [_dot(pooled[:, g * 128:(g + 1) * 128], pw_ref[g]) for g in range(len(POOL_WINDOWS))], axis=1)
        y_ref[:, 0:D_POOL] = _mx(mixed * ps_ref[...])

        xbc = p_ref[:, OFF_X:OFF_X + D_XBC]
        pre = _conv_pre(jnp.concatenate([xtail[...], xbc], axis=0), xbc, cw_ref[...], cb_ref[...])
        xtail[...] = xbc[CHUNK - HALO:, :]
        pre_ref[...] = pre
        xc = pre * _sigmoid(pre)
        dt, _, a_col, _, _ = _dt_and_cumsum(p_ref[:, OFF_DT:OFF_DT + 128], dtb_ref[...], alog_ref[...], valid,
                                            tril_ref[...])
        s_prev = state[...]
        st_ref[0] = s_prev
        yp, s_new = _ssd_chunk_fwd(xc[:, 0:D_SSM], xc[:, D_SSM:D_SSM + 512], xc[:, D_SSM + 512:], dt, a_col, s_prev,
                                   dx_ref[...], e_ref[...], et_ref[...])
        state[...] = s_new
        ypre_ref[...] = yp
        z = p_ref[:, OFF_Z:OFF_Z + D_SSM]
        yz = yp * (z * _sigmoid(z))
        outs = []
        for g in range(N_GROUPS):
            gs = slice(g * GROUP_CH, (g + 1) * GROUP_CH)
            r = lax.rsqrt(jnp.mean(yz[:, gs] * yz[:, gs], axis=-1, keepdims=True) + EPS)
            outs.append(yz[:, gs] * r)
        y_ref[:, D_POOL:] = _mx(jnp.concatenate(outs, axis=1) * nw_ref[...])

    blk = lambda w: pl.BlockSpec((CHUNK, w), lambda b, c: (b * nc + c, 0))
    return pl.pallas_call(
        body, grid=(nb, nc), name="mixer_fwd",
        in_specs=[blk(PROJ_W), _full((4, D_XBC)), _full((1, D_XBC)), _full((1, 128)), _full((1, 128)),
                  _full((1, D_SSM)), _full((1, D_SSM)), _full((4, 128, 128)), _full((1, D_POOL)),
                  _full((128, D_SSM)), _full((D_SSM, 128)), _full((CHUNK, CHUNK))],
        out_specs=[blk(D_MIX), blk(D_SSM), blk(D_XBC),
                   pl.BlockSpec((1, D_SSM, D_STATE), lambda b, c: (b * nc + c, 0, 0))],
        out_shape=[jax.ShapeDtypeStruct((m, D_MIX), MXU_DTYPE), jax.ShapeDtypeStruct((m, D_SSM), F32),
                   jax.ShapeDtypeStruct((m, D_XBC), F32), jax.ShapeDtypeStruct((m // CHUNK, D_SSM, D_STATE), F32)],
        scratch_shapes=[pltpu.VMEM((HALO, D_XBC), F32), pltpu.VMEM((HALO, D_POOL), F32),
                        pltpu.VMEM((D_SSM, D_STATE), F32)],
        compiler_params=_params("arbitrary", "arbitrary"),
    )(proj, cw, cb, dt_bias, a_log, d_x, nw, pool_w, pool_scale, e_mat, et_mat, tril)


def _mixer_bwd(proj, dy, ypre, conv_pre, states, cw, dt_bias, a_log, d_x, nw, pool_w, pool_scale, nb):
    m = proj.shape[0]
    nc = m // nb // CHUNK
    e_mat, et_mat, tril, triu = _ssd_constants()
    hb = CHUNK // HALO

    def body(p_ref, halo_ref, dy_ref, ypre_ref, pre_ref, st_ref, cw_ref, dtb_ref, alog_ref, dx_ref, nw_ref, pw_ref,
             ps_ref, e_ref, et_ref, tril_ref, triu_ref,
             dp_ref, gcw_ref, gcb_ref, gdtb_ref, galog_ref, gd_ref, gnw_ref, gpw_ref, gps_ref,
             ds_carry, dpre_next, dq_next):
        b = pl.program_id(0)
        cc = pl.program_id(1)
        c = nc - 1 - cc

        @pl.when((b == 0) & (cc == 0))
        def _():
            for r in (gcw_ref, gcb_ref, gdtb_ref, galog_ref, gd_ref, gnw_ref, gpw_ref, gps_ref):
                r[...] = jnp.zeros_like(r)

        @pl.when(cc == 0)
        def _():
            ds_carry[...] = jnp.zeros_like(ds_carry)
            dpre_next[...] = jnp.zeros_like(dpre_next)
            dq_next[...] = jnp.zeros_like(dq_next)

        valid = (c > 0) | (lax.broadcasted_iota(jnp.int32, (CHUNK, 1), 0) >= PAD_ROWS)
        first = c > 0

        u = p_ref[:, 0:D_POOL]
        u_halo = jnp.where(first, halo_ref[...], 0.0)
        inv_cnt, lane = _pool_inv_count(c)
        pooled = _pool_window_sums(jnp.concatenate([u_halo, u], axis=0), lane) * inv_cnt - u
        dyp = dy_ref[:, 0:D_POOL]
        ps = ps_ref[...]
        dmixed = dyp * ps
        mixed, dpooled = [], []
        for g in range(len(POOL_WINDOWS)):
            gsl = slice(g * 128, (g + 1) * 128)
            pw = pw_ref[g]
            mixed.append(_dot(pooled[:, gsl], pw))
            dpooled.append(_dot_nt(dmixed[:, gsl], pw))
            gpw_ref[g] += _dot_tn(pooled[:, gsl], dmixed[:, gsl])
        gps_ref[...] += jnp.sum(dyp * jnp.concatenate(mixed, axis=1), axis=0, keepdims=True)
        dpooled = jnp.concatenate(dpooled, axis=1)
        dq = dpooled * inv_cnt
        du = _pool_window_sums_ahead(jnp.concatenate([dq, dq_next[...]], axis=0), lane) - dpooled
        dq_next[...] = dq[0:HALO, :]
        dp_ref[:, 0:D_POOL] = _mx(du)

        yp = ypre_ref[...]
        z = p_ref[:, OFF_Z:OFF_Z + D_SSM]
        sz, dsz = _silu_and_grad(z)
        yz = yp * sz
        do = dy_ref[:, D_POOL:]
        nw_row = nw_ref[...]
        dyz = []
        gnw = []
        for g in range(N_GROUPS):
            gs = slice(g * GROUP_CH, (g + 1) * GROUP_CH)
            r = lax.rsqrt(jnp.mean(yz[:, gs] * yz[:, gs], axis=-1, keepdims=True) + EPS)
            n = yz[:, gs] * r
            gnw.append(jnp.sum(do[:, gs] * n, axis=0, keepdims=True))
            dn = do[:, gs] * nw_row[:, gs]
            dyz.append(r * (dn - n * jnp.mean(dn * n, axis=-1, keepdims=True)))
        gnw_ref[...] += jnp.concatenate(gnw, axis=1)
        dyz = jnp.concatenate(dyz, axis=1)
        dp_ref[:, OFF_Z:OFF_Z + D_SSM] = _mx(dyz * yp * dsz)
        dyp_ssm = dyz * sz

        xc, dsilu = _silu_and_grad(pre_ref[...])
        dtr = p_ref[:, OFF_DT:OFF_DT + 128]
        dt, a_row, a_col, dt_pre, head = _dt_and_cumsum(dtr, dtb_ref[...], alog_ref[...], valid, tril_ref[...])
        dxs, dbm, dcm, ddt, da, dd, ds_prev = _ssd_chunk_bwd(
            xc[:, 0:D_SSM], xc[:, D_SSM:D_SSM + 512], xc[:, D_SSM + 512:], dt, a_row, a_col, st_ref[0],
            ds_carry[...], dyp_ssm, dx_ref[...], e_ref[...], et_ref[...], triu_ref[...])
        ds_carry[...] = ds_prev
        gd_ref[...] += dd
        galog_ref[...] += da * a_row
        ddtr = jnp.where(valid & head, ddt * _sigmoid(dt_pre), 0.0)
        gdtb_ref[...] += jnp.sum(ddtr, axis=0, keepdims=True)
        dp_ref[:, OFF_DT:OFF_DT + 128] = _mx(ddtr)

        dpre = jnp.concatenate([dxs, dbm, dcm], axis=1) * dsilu
        gcb_ref[...] += jnp.sum(dpre, axis=0, keepdims=True)
        dext = jnp.concatenate([dpre, dpre_next[...]], axis=0)
        dpre_next[...] = dpre[0:HALO, :]
        ups = [_shift_up(dext, 3 - k) for k in range(4)]
        xbc = p_ref[:, OFF_X:OFF_X + D_XBC]
        gcw_ref[...] += jnp.concatenate([jnp.sum(xbc * ups[k], axis=0, keepdims=True) for k in range(4)], axis=0)
        cw = cw_ref[...]
        dp_ref[:, OFF_X:OFF_X + D_XBC] = _mx(cw[3:4, :] * ups[3] + cw[2:3, :] * ups[2]
                                             + cw[1:2, :] * ups[1] + cw[0:1, :] * ups[0])

    blk = lambda w: pl.BlockSpec((CHUNK, w), lambda b, cc: (b * nc + nc - 1 - cc, 0))
    halo = pl.BlockSpec((HALO, D_POOL), lambda b, cc: (jnp.maximum((b * nc + nc - 1 - cc) * hb - 1, 0), 0))
    return pl.pallas_call(
        body, grid=(nb, nc), name="mixer_bwd",
        in_specs=[blk(PROJ_W), halo, blk(D_MIX), blk(D_SSM), blk(D_XBC),
                  pl.BlockSpec((1, D_SSM, D_STATE), lambda b, cc: (b * nc + nc - 1 - cc, 0, 0)),
                  _full((4, D_XBC)), _full((1, 128)), _full((1, 128)), _full((1, D_SSM)),
                  _full((1, D_SSM)), _full((4, 128, 128)), _full((1, D_POOL)),
                  _full((128, D_SSM)), _full((D_SSM, 128)), _full((CHUNK, CHUNK)), _full((CHUNK, CHUNK))],
        out_specs=[blk(PROJ_W), _full((4, D_XBC)), _full((1, D_XBC)), _full((1, 128)), _full((1, 128)), _full((1, 128)),
                   _full((1, D_SSM)), _full((4, 128, 128)), _full((1, D_POOL))],
        out_shape=[jax.ShapeDtypeStruct((m, PROJ_W), MXU_DTYPE), jax.ShapeDtypeStruct((4, D_XBC), F32),
                   jax.ShapeDtypeStruct((1, D_XBC), F32), jax.ShapeDtypeStruct((1, 128), F32),
                   jax.ShapeDtypeStruct((1, 128), F32), jax.ShapeDtypeStruct((1, 128), F32),
                   jax.ShapeDtypeStruct((1, D_SSM), F32), jax.ShapeDtypeStruct((4, 128, 128), F32),
                   jax.ShapeDtypeStruct((1, D_POOL), F32)],
        scratch_shapes=[pltpu.VMEM((D_SSM, D_STATE), F32), pltpu.VMEM((HALO, D_XBC), F32),
                        pltpu.VMEM((HALO, D_POOL), F32)],
        compiler_params=_params("arbitrary", "arbitrary"),
    )(proj, proj, dy, ypre, conv_pre, states, cw, dt_bias, a_log, d_x, nw, pool_w, pool_scale, e_mat, et_mat, tril, triu)


def _pad128(v):
    v = v.reshape(1, -1).astype(F32)
    return jnp.pad(v, ((0, 0), (0, 128 - v.shape[1])))


def _local_step(x, tgt, meta, w):
    nb, seq, _ = x.shape
    t_rows = LEAD + seq
    lead = jnp.concatenate([jnp.zeros((PAD_ROWS, D_MODEL), F32), meta.astype(F32)], axis=0)
    hp = jnp.concatenate([jnp.broadcast_to(lead[None], (nb, LEAD, D_MODEL)), x], axis=1).reshape(nb * t_rows, D_MODEL)
    tgt_p = jnp.pad(tgt, ((0, 0), (LEAD, 0), (0, 0))).reshape(nb * t_rows, D_MODEL)

    dt_bias, a_log = _pad128(w["dt_bias"]), _pad128(w["a_log"])
    d_x = jnp.repeat(w["d_skip"].reshape(1, N_HEADS).astype(F32), HEAD_DIM, axis=1)

    hn1, proj = _in_proj(hp, w["norm_mix_w"], w["win"])
    y, ypre, conv_pre, states = _mixer_fwd(proj, w["conv_w"], w["conv_b"], dt_bias, a_log, d_x, w["ssm_norm_w"],
                                           w["pool_w"], w["pool_scale"], nb)
    loss, g_nf, g_nffn, ff, da, hn2, dh1, dh2, dy = _ffn_fwd_bwd(
        hp, y, tgt_p, w["wout"], w["norm_ffn_w"], w["wff1"], w["wff2"], w["norm_f_w"], t_rows)
    g_wff2 = _tn_matmul(ff, dh2, "grad_w_ff2", tka=1024)
    g_wff1 = _tn_matmul(hn2, da, "grad_w_ff1", tka=512)
    g_wout = _tn_matmul(y, dh1, "grad_w_out", tka=1024)
    dproj, g_cw, g_cb, g_dtb, g_alog, g_d, g_nw, g_pw, g_ps = _mixer_bwd(
        proj, dy, ypre, conv_pre, states, w["conv_w"], dt_bias, a_log, d_x, w["ssm_norm_w"], w["pool_w"],
        w["pool_scale"], nb)
    g_win = _tn_matmul(hn1, dproj, "grad_w_in", tka=512)
    dh0, g_nmix, g_meta = _in_proj_bwd(dproj, hp, dh1, w["norm_mix_w"], w["win"], t_rows)
    grads = dict(norm_mix_w=g_nmix, win=g_win, pool_w=g_pw, pool_scale=g_ps, conv_w=g_cw, conv_b=g_cb,
                 dt_bias=g_dtb[:, :N_HEADS], a_log=g_alog[:, :N_HEADS], d_skip=g_d[:, :N_HEADS], ssm_norm_w=g_nw,
                 wout=g_wout, norm_ffn_w=g_nffn, wff1=g_wff1, wff2=g_wff2, norm_f_w=g_nf)
    return loss[0, 0], dh0.reshape(nb, t_rows, D_MODEL), g_meta, grads


MESH_IDS = pl.DeviceIdType.MESH
_HBM = pl.BlockSpec(memory_space=pltpu.HBM)


def _coords():
    return lax.axis_index("x"), lax.axis_index("y"), lax.axis_index("c")


def _other_chips(x, y):
    return [(1 - x, y), (x, 1 - y), (1 - x, 1 - y)]


def _weight_gather(shards):
    n = len(shards)

    def body(*refs):
        ins, outs = refs[:n], refs[n:2 * n]
        send_sems, recv_sems, local_sems = refs[2 * n:]
        x, y, c = _coords()
        me, sibling = (x, y, c), (x, y, 1 - c)
        chips = _other_chips(x, y)

        def copy(t, k, block, to, src=None):
            dst = outs[t].at[4 * block[0] + 2 * block[1] + block[2]]
            return pltpu.make_async_remote_copy(
                src_ref=dst if src is None else src, dst_ref=dst, send_sem=send_sems.at[t * 7 + k],
                recv_sem=recv_sems.at[t * 7 + k], device_id=to, device_id_type=MESH_IDS)

        mine = [pltpu.make_async_copy(ins[t], outs[t].at[4 * x + 2 * y + c], local_sems.at[t]) for t in range(n)]
        for cp in mine:
            cp.start()
        first = []
        for t in range(n):
            first.append(copy(t, 0, me, sibling, src=ins[t]))
            first += [copy(t, 1 + j, me, (*chip, c), src=ins[t]) for j, chip in enumerate(chips)]
        for cp in first:
            cp.start()
        passed = []
        for j, chip in enumerate(chips):
            for t in range(n):
                copy(t, 1 + j, (*chip, c), me).wait_recv()
                cp = copy(t, 4 + j, (*chip, c), sibling)
                cp.start()
                passed.append(cp)
        for t in range(n):
            copy(t, 0, sibling, me).wait_recv()
            for j, chip in enumerate(chips):
                copy(t, 4 + j, (*chip, 1 - c), me).wait_recv()
        for cp in first + passed:
            cp.wait_send()
        for cp in mine:
            cp.wait()

    return pl.pallas_call(
        body, name="weight_gather",
        in_specs=[_HBM] * n, out_specs=[_HBM] * n,
        out_shape=[jax.ShapeDtypeStruct((N_DEV,) + s.shape, s.dtype) for s in shards],
        scratch_shapes=[pltpu.SemaphoreType.DMA((7 * n,)), pltpu.SemaphoreType.DMA((7 * n,)),
                        pltpu.SemaphoreType.DMA((n,))],
    )(*shards)


def _grad_exchange_d2d(gs, small):
    n = len(gs)

    def body(*refs):
        ins, small_in = refs[:n], refs[n]
        got, small_got = refs[n + 1:2 * n + 1], refs[2 * n + 1]
        send_sems, recv_sems = refs[2 * n + 2:]
        x, y, c = _coords()
        sibling = (x, y, 1 - c)
        remote = []
        for t in range(n):
            for k in range(4):
                remote.append(pltpu.make_async_remote_copy(
                    src_ref=ins[t].at[k, 1 - c], dst_ref=got[t].at[k], send_sem=send_sems.at[t * 4 + k],
                    recv_sem=recv_sems.at[t * 4 + k], device_id=sibling, device_id_type=MESH_IDS))
        remote.append(pltpu.make_async_remote_copy(
            src_ref=small_in, dst_ref=small_got, send_sem=send_sems.at[4 * n], recv_sem=recv_sems.at[4 * n],
            device_id=sibling, device_id_type=MESH_IDS))
        for cp in remote:
            cp.start()
        for cp in remote:
            cp.wait_recv()
        for cp in remote:
            cp.wait_send()

    blocks = [jax.ShapeDtypeStruct((4,) + g.shape[2:], g.dtype) for g in gs]
    outs = pl.pallas_call(
        body, name="grad_exchange_d2d",
        in_specs=[_HBM] * (n + 1), out_specs=[_HBM] * (n + 1),
        out_shape=blocks + [jax.ShapeDtypeStruct(small.shape, small.dtype)],
        scratch_shapes=[pltpu.SemaphoreType.DMA((4 * n + 1,)), pltpu.SemaphoreType.DMA((4 * n + 1,))],
    )(*gs, small)
    return outs[:n], outs[n]


def _grad_exchange_ici(ps, small):
    arrs = list(ps) + [small]
    n = len(arrs)

    def body(*refs):
        ins, outs = refs[:n], refs[n:2 * n]
        send_sems, recv_sems, local_sems = refs[2 * n:]
        x, y, c = _coords()
        my_chip = 2 * x + y
        chips = _other_chips(x, y)
        src_of = lambda t, k: ins[t] if t == n - 1 else ins[t].at[k]
        local = [pltpu.make_async_copy(src_of(t, my_chip), outs[t].at[my_chip], local_sems.at[t]) for t in range(n)]
        remote = []
        for t in range(n):
            for j, (cx, cy) in enumerate(chips):
                remote.append(pltpu.make_async_remote_copy(
                    src_ref=src_of(t, 2 * cx + cy), dst_ref=outs[t].at[my_chip], send_sem=send_sems.at[t * 3 + j],
                    recv_sem=recv_sems.at[t * 3 + j], device_id=(cx, cy, c), device_id_type=MESH_IDS))
        for cp in remote + local:
            cp.start()
        for t in range(n):
            for j, (cx, cy) in enumerate(chips):
                slot = outs[t].at[2 * cx + cy]
                pltpu.make_async_remote_copy(
                    src_ref=slot, dst_ref=slot, send_sem=send_sems.at[t * 3 + j], recv_sem=recv_sems.at[t * 3 + j],
                    device_id=(cx, cy, c), device_id_type=MESH_IDS).wait_recv()
        for cp in remote:
            cp.wait_send()
        for cp in local:
            cp.wait()

    outs = pl.pallas_call(
        body, name="grad_exchange_ici",
        in_specs=[_HBM] * n, out_specs=[_HBM] * n,
        out_shape=[jax.ShapeDtypeStruct(p.shape, p.dtype) for p in ps]
        + [jax.ShapeDtypeStruct((4,) + small.shape, small.dtype)],
        scratch_shapes=[pltpu.SemaphoreType.DMA((3 * n,)), pltpu.SemaphoreType.DMA((3 * n,)),
                        pltpu.SemaphoreType.DMA((n,))],
    )(*arrs)
    return outs[:n - 1], outs[n - 1]


def _row_tile(rows, cols, n_arrays):
    budget = 24 * 1024 * 1024
    padded = -(-cols // 128) * 128
    step = 16 if rows % 16 == 0 else 8
    tr = max(step, budget // (n_arrays * 2 * 4 * padded) // step * step)
    while rows % tr:
        tr -= step
    return tr


def _chip_sum(g4, got, core, name):
    _, _, rows, cols = g4.shape
    tr = _row_tile(rows, cols, 3)

    def body(c_ref, a_ref, b_ref, o_ref):
        o_ref[...] = (a_ref[...] + b_ref[...]).astype(o_ref.dtype)

    grid_spec = pltpu.PrefetchScalarGridSpec(
        num_scalar_prefetch=1, grid=(4, rows // tr),
        in_specs=[pl.BlockSpec((None, None, tr, cols), lambda k, i, c: (k, c[0], i, 0)),
                  pl.BlockSpec((None, tr, cols), lambda k, i, c: (k, i, 0))],
        out_specs=pl.BlockSpec((None, tr, cols), lambda k, i, c: (k, i, 0)))
    return pl.pallas_call(body, grid_spec=grid_spec, name=name,
                          out_shape=jax.ShapeDtypeStruct((4, rows, cols), MXU_DTYPE),
                          compiler_params=_params("arbitrary", "arbitrary"))(core, g4, got)


def _add(a, b, name):
    shape = a.shape
    a2, b2 = a.reshape(-1, shape[-1]), b.reshape(-1, shape[-1])
    rows, cols = a2.shape
    tr = _row_tile(rows, cols, 3)

    def body(a_ref, b_ref, o_ref):
        o_ref[...] = a_ref[...] + b_ref[...]

    blk = pl.BlockSpec((tr, cols), lambda i: (i, 0))
    return pl.pallas_call(body, grid=(rows // tr,), name=name, in_specs=[blk, blk], out_specs=blk,
                          out_shape=jax.ShapeDtypeStruct(a2.shape, a.dtype),
                          compiler_params=_params("arbitrary"))(a2, b2).reshape(shape)


def _sum4(parts, name):
    _, rows, cols = parts.shape
    tr = _row_tile(rows, cols, 5)

    def body(p_ref, o_ref):
        o_ref[...] = ((p_ref[0] + p_ref[1]) + p_ref[2]) + p_ref[3]

    return pl.pallas_call(body, grid=(rows // tr,), name=name,
                          in_specs=[pl.BlockSpec((4, tr, cols), lambda i: (0, i, 0))],
                          out_specs=pl.BlockSpec((tr, cols), lambda i: (i, 0)),
                          out_shape=jax.ShapeDtypeStruct((rows, cols), parts.dtype),
                          compiler_params=_params("arbitrary"))(parts)


def _adamw_math(w, g, m, v):
    m2 = ADAM_B1 * m + (1.0 - ADAM_B1) * g
    v2 = ADAM_B2 * v + (1.0 - ADAM_B2) * (g * g)
    m_hat = m2 / (1.0 - ADAM_B1 ** ADAM_STEP)
    v_hat = v2 / (1.0 - ADAM_B2 ** ADAM_STEP)
    delta = -ADAM_LR * (m_hat / (jnp.sqrt(v_hat) + ADAM_EPS) + ADAM_WD * w)
    return delta, m2, v2


def _adamw(parts, w, m, v, name):
    rows, cols = w.shape
    tr = _row_tile(rows, cols, 11)

    def body(p_ref, w_ref, m_ref, v_ref, g_ref, d_ref, m2_ref, v2_ref):
        part = lambda k: p_ref[k].astype(F32)
        g = ((part(0) + part(1)) + part(2)) + part(3)
        d, m2, v2 = _adamw_math(w_ref[...], g, m_ref[...], v_ref[...])
        g_ref[...] = g
        d_ref[...] = d
        m2_ref[...] = m2
        v2_ref[...] = v2

    blk = pl.BlockSpec((tr, cols), lambda i: (i, 0))
    out = jax.ShapeDtypeStruct((rows, cols), F32)
    return pl.pallas_call(body, grid=(rows // tr,), name=name,
                          in_specs=[pl.BlockSpec((4, tr, cols), lambda i: (0, i, 0)), blk, blk, blk],
                          out_specs=[blk] * 4, out_shape=[out] * 4,
                          compiler_params=_params("arbitrary"))(parts, w, m, v)


def _adamw_small(gs, ws, ms, vs):
    n = len(ws)

    def body(*refs):
        g_refs, w_refs, m_refs, v_refs = (refs[k * n:(k + 1) * n] for k in range(4))
        d_refs, m2_refs, v2_refs = (refs[(4 + k) * n:(5 + k) * n] for k in range(3))
        for t in range(n):
            d, m2, v2 = _adamw_math(w_refs[t][...], g_refs[t][...], m_refs[t][...], v_refs[t][...])
            d_refs[t][...] = d
            m2_refs[t][...] = m2
            v2_refs[t][...] = v2

    outs = pl.pallas_call(body, name="adamw_small",
                          out_shape=[jax.ShapeDtypeStruct(w.shape, F32) for w in ws] * 3)(*gs, *ws, *ms, *vs)
    return outs[:n], outs[n:2 * n], outs[2 * n:]


def _pack(arrays):
    rows = []
    for a in arrays:
        flat = a.astype(F32).reshape(-1)
        rows.append(jnp.pad(flat, (0, -flat.shape[0] % 128)).reshape(-1, 128))
    out = jnp.concatenate(rows, axis=0)
    return jnp.pad(out, ((0, -out.shape[0] % 8), (0, 0)))


def _unpack(pack, shapes):
    out, r = [], 0
    for s in shapes:
        n = int(np.prod(s))
        nr = -(-n // 128)
        out.append(pack[r:r + nr].reshape(-1)[:n].reshape(s))
        r += nr
    return out


_WEIGHTS = ["meta", "norm_mix_w", "w_in", "pool_w", "pool_scale", "conv_w", "conv_b", "dt_bias", "a_log", "d_skip",
            "ssm_norm_w", "w_out", "norm_ffn_w", "w_ff1", "w_ff2", "norm_f_w"]
_BIG = ["w_in", "w_out", "w_ff1", "w_ff2"]
_SMALL = [n for n in _WEIGHTS if n not in _BIG]


def kernel(x, meta, norm_mix_w, w_in, pool_w, pool_scale, conv_w, conv_b, dt_bias, a_log, d_skip, ssm_norm_w, w_out, norm_ffn_w, w_ff1, w_ff2, norm_f_w, loss_target, m_meta, m_norm_mix_w, m_w_in, m_pool_w, m_pool_scale, m_conv_w, m_conv_b, m_dt_bias, m_a_log, m_d_skip, m_ssm_norm_w, m_w_out, m_norm_ffn_w, m_w_ff1, m_w_ff2, m_norm_f_w, v_meta, v_norm_mix_w, v_w_in, v_pool_w, v_pool_scale, v_conv_w, v_conv_b, v_dt_bias, v_a_log, v_d_skip, v_ssm_norm_w, v_w_out, v_norm_ffn_w, v_w_ff1, v_w_ff2, v_norm_f_w):
    wts = dict(meta=meta, norm_mix_w=norm_mix_w, w_in=w_in, pool_w=pool_w, pool_scale=pool_scale, conv_w=conv_w,
               conv_b=conv_b, dt_bias=dt_bias, a_log=a_log, d_skip=d_skip, ssm_norm_w=ssm_norm_w, w_out=w_out,
               norm_ffn_w=norm_ffn_w, w_ff1=w_ff1, w_ff2=w_ff2, norm_f_w=norm_f_w)
    mom1 = dict(zip(_WEIGHTS, (m_meta, m_norm_mix_w, m_w_in, m_pool_w, m_pool_scale, m_conv_w, m_conv_b, m_dt_bias,
                               m_a_log, m_d_skip, m_ssm_norm_w, m_w_out, m_norm_ffn_w, m_w_ff1, m_w_ff2, m_norm_f_w)))
    mom2 = dict(zip(_WEIGHTS, (v_meta, v_norm_mix_w, v_w_in, v_pool_w, v_pool_scale, v_conv_w, v_conv_b, v_dt_bias,
                               v_a_log, v_d_skip, v_ssm_norm_w, v_w_out, v_norm_ffn_w, v_w_ff1, v_w_ff2, v_norm_f_w)))
    xi, yi, ci = _coords()
    dev = 4 * xi + 2 * yi + ci
    win_cols = w_in.shape[-1]
    cw_cols = conv_w.shape[-1]

    lead_pack = jnp.zeros((N_META, 512), F32)
    lead_pack = lead_pack.at[:, :128].set(meta).at[:4, 128:128 + cw_cols].set(conv_w[0])
    g_win, g_wout, g_wff1, g_wff2, g_lead = _weight_gather(
        [_mx(w_in[0]), _mx(w_out[0]), _mx(w_ff1[0]), _mx(w_ff2[0]), lead_pack])
    win_full = jnp.transpose(g_win, (1, 0, 2)).reshape(D_MODEL, N_DEV * win_cols)
    win_full = jnp.pad(win_full, ((0, 0), (0, PROJ_W - N_DEV * win_cols)))
    wff1_full = jnp.transpose(g_wff1, (1, 0, 2)).reshape(D_MODEL, D_FF)
    meta_full = jnp.transpose(g_lead[:, :, :128], (1, 0, 2)).reshape(N_META, D_MODEL)
    cw_full = jnp.transpose(g_lead[:, :4, 128:128 + cw_cols], (1, 0, 2)).reshape(4, D_XBC)

    full = dict(norm_mix_w=norm_mix_w, win=win_full, pool_w=pool_w[0], pool_scale=pool_scale, conv_w=cw_full,
                conv_b=conv_b, dt_bias=dt_bias, a_log=a_log, d_skip=d_skip, ssm_norm_w=ssm_norm_w,
                wout=g_wout.reshape(D_MIX, D_MODEL), norm_ffn_w=norm_ffn_w, wff1=wff1_full,
                wff2=g_wff2.reshape(D_FF, D_MODEL), norm_f_w=norm_f_w.reshape(1, D_MODEL))
    loss, dh0, g_meta, g = _local_step(x, loss_target, meta_full, full)

    owners = lambda a: a.reshape((4, 2) + a.shape[1:])
    big_parts = [
        owners(jnp.transpose(g["win"][:, :N_DEV * win_cols].reshape(D_MODEL, N_DEV, win_cols), (1, 0, 2))),
        owners(g["wout"].reshape(N_DEV, D_MIX // N_DEV, D_MODEL)),
        owners(jnp.transpose(g["wff1"].reshape(D_MODEL, N_DEV, D_FF // N_DEV), (1, 0, 2))),
        owners(g["wff2"].reshape(N_DEV, D_FF // N_DEV, D_MODEL)),
    ]
    small_names = ["meta", "norm_mix_w", "pool_w", "pool_scale", "conv_w", "conv_b", "dt_bias", "a_log", "d_skip",
                   "ssm_norm_w", "norm_ffn_w", "norm_f_w"]
    small_full = dict(g, meta=g_meta, loss=loss.reshape(1, 1))
    small_names = small_names + ["loss"]
    small_pack = _pack([small_full[n] for n in small_names])
    got, small_got = _grad_exchange_d2d(big_parts, small_pack)
    core = jnp.reshape(ci, (1,)).astype(jnp.int32)
    chip_sums = [_chip_sum(a, b, core, "chip_sum_%d" % t) for t, (a, b) in enumerate(zip(big_parts, got))]
    small_chip = _add(small_pack, small_got, "chip_sum_small")
    parts, small_parts = _grad_exchange_ici(chip_sums, small_chip)
    small_sum = _sum4(small_parts, "small_sum")
    small_shapes = [small_full[n].shape for n in small_names]
    gs = dict(zip(small_names, _unpack(small_sum, small_shapes)))
    gs["meta"] = lax.dynamic_slice_in_dim(gs["meta"], dev * 128, 128, axis=1)
    gs["conv_w"] = lax.dynamic_slice_in_dim(gs["conv_w"], dev * cw_cols, cw_cols, axis=1)

    res = {}
    for t, n in enumerate(_BIG):
        shp = wts[n].shape
        res[n] = [o.reshape(shp) for o in _adamw(parts[t], wts[n][0], mom1[n][0], mom2[n][0], "adamw_" + n)]
    as2d = lambda a: a.reshape(-1, a.shape[-1])
    small_g = [as2d(gs[n].reshape(wts[n].shape)) for n in _SMALL]
    small_out = _adamw_small(small_g, *[[as2d(d[n]) for n in _SMALL] for d in (wts, mom1, mom2)])
    for k, n in enumerate(_SMALL):
        res[n] = [o[k].reshape(wts[n].shape) for o in (small_g,) + tuple(small_out)]

    grad_x = dh0[:, LEAD:, :]
    return (gs["loss"][0, 0], grad_x, *[res[n][0] for n in _WEIGHTS], *[res[n][1] for n in _WEIGHTS],
            *[res[n][2] for n in _WEIGHTS], *[res[n][3] for n in _WEIGHTS])
```

```python
import functools

import numpy as np
import jax
import jax.numpy as jnp
from jax import lax
from jax.experimental import pallas as pl
from jax.experimental.pallas import tpu as pltpu

F32 = jnp.float32
MXU_DTYPE = jnp.bfloat16

D_MODEL = 1024
D_POOL = 512
D_SSM = 1536
D_XBC = 2560
N_HEADS = 24
HEAD_DIM = 64
N_GROUPS = 4
GROUP_CH = D_SSM // N_GROUPS
D_STATE = 128
CHUNK = 128
N_META = 16
LEAD = CHUNK
PAD_ROWS = LEAD - N_META
D_MIX = D_POOL + D_SSM
D_FF = 4096
PROJ_W = 4736
OFF_Z = D_POOL
OFF_X = D_POOL + D_SSM
OFF_DT = OFF_X + D_XBC
D_IN_PROJ = OFF_DT + N_HEADS
POOL_WINDOWS = (2, 4, 8, 16)
HALO = 16
EPS = 1e-5
N_DEV = 8

ADAM_LR, ADAM_B1, ADAM_B2, ADAM_EPS, ADAM_WD, ADAM_STEP = 0.001, 0.9, 0.999, 1e-08, 0.01, 10

VMEM_LIMIT = 60 * 1024 * 1024


def _mx(a):
    return a.astype(MXU_DTYPE)


def _dot(a, b):
    return jnp.dot(_mx(a), _mx(b), preferred_element_type=F32)


def _dot_nt(a, b):
    return lax.dot_general(_mx(a), _mx(b), (((1,), (1,)), ((), ())), preferred_element_type=F32)


def _dot_tn(a, b):
    return lax.dot_general(_mx(a), _mx(b), (((0,), (0,)), ((), ())), preferred_element_type=F32)


def _split3(x):
    hi = x.astype(MXU_DTYPE)
    r = x - hi.astype(F32)
    mid = r.astype(MXU_DTYPE)
    lo = (r - mid.astype(F32)).astype(MXU_DTYPE)
    return hi, mid, lo


def _exact_l(c, x):
    hi, mid, lo = _split3(x)
    f = lambda p: jnp.dot(c, p, preferred_element_type=F32)
    return f(hi) + f(mid) + f(lo)


def _exact_r(x, c):
    hi, mid, lo = _split3(x)
    f = lambda p: jnp.dot(p, c, preferred_element_type=F32)
    return f(hi) + f(mid) + f(lo)


def _contract(x, c):
    hi = x.astype(MXU_DTYPE)
    lo = (x - hi.astype(F32)).astype(MXU_DTYPE)
    return jnp.dot(hi, c, preferred_element_type=F32) + jnp.dot(lo, c, preferred_element_type=F32)


def _sigmoid(x):
    return jax.nn.sigmoid(x)


def _softplus(x):
    return jnp.maximum(x, 0.0) + jnp.log1p(jnp.exp(-jnp.abs(x)))


def _silu_and_grad(x):
    s = _sigmoid(x)
    return x * s, s * (1.0 + x * (1.0 - s))


def _shift_down(ext, s):
    if s == 0:
        return ext[HALO:, :]
    return pltpu.roll(ext, s, 0)[HALO:, :]


def _shift_up(ext, s):
    if s == 0:
        return ext[:CHUNK, :]
    return pltpu.roll(ext, ext.shape[0] - s, 0)[:CHUNK, :]


def _by_pool_group(lane, a2, a4, a8, a16):
    return jnp.where(lane < 128, a2, jnp.where(lane < 256, a4, jnp.where(lane < 384, a8, a16)))


def _pool_inv_count(chunk_idx):
    row = lax.broadcasted_iota(jnp.int32, (CHUNK, D_POOL), 0)
    lane = lax.broadcasted_iota(jnp.int32, (CHUNK, D_POOL), 1)
    pos1 = jnp.maximum(chunk_idx * CHUNK + row - (PAD_ROWS - 1), 1)
    w = _by_pool_group(lane, 2, 4, 8, 16)
    return 1.0 / jnp.minimum(pos1, w).astype(F32), lane


def _pool_window_sums(u_ext, lane):
    s2 = u_ext + pltpu.roll(u_ext, 1, 0)
    s4 = s2 + pltpu.roll(s2, 2, 0)
    s8 = s4 + pltpu.roll(s4, 4, 0)
    s16 = s8 + pltpu.roll(s8, 8, 0)
    return _by_pool_group(lane, s2[HALO:], s4[HALO:], s8[HALO:], s16[HALO:])


def _pool_window_sums_ahead(q_ext, lane):
    n = q_ext.shape[0]
    r2 = q_ext + pltpu.roll(q_ext, n - 1, 0)
    r4 = r2 + pltpu.roll(r2, n - 2, 0)
    r8 = r4 + pltpu.roll(r4, n - 4, 0)
    r16 = r8 + pltpu.roll(r8, n - 8, 0)
    return _by_pool_group(lane, r2[:CHUNK], r4[:CHUNK], r8[:CHUNK], r16[:CHUNK])


def _conv_pre(ext, xbc, cw, cb):
    return (cb + cw[3:4, :] * xbc + cw[2:3, :] * _shift_down(ext, 1)
            + cw[1:2, :] * _shift_down(ext, 2) + cw[0:1, :] * _shift_down(ext, 3))


def _dt_and_cumsum(dtr, dt_bias, a_log, valid, tril):
    lane = lax.broadcasted_iota(jnp.int32, (CHUNK, 128), 1)
    head = lane < N_HEADS
    pre = dtr + dt_bias
    dt = jnp.where(valid & head, _softplus(pre), 0.0)
    a_row = jnp.where(head[0:1, :], -jnp.exp(a_log), 0.0)
    a_col = _exact_l(tril, dt * a_row)
    return dt, a_row, a_col, pre, head


def _decay(a_col, a_row_t, h, causal):
    seg = a_col[:, h:h + 1] - a_row_t[h:h + 1, :]
    return jnp.where(causal, jnp.exp(jnp.minimum(seg, 0.0)), 0.0)


def _ssd_chunk_fwd(xs, bm, cm, dt, a_col, s_prev, d_x, e_mat, et_mat):
    lane = lax.broadcasted_iota(jnp.int32, (CHUNK, 128), 1)
    rowi = lax.broadcasted_iota(jnp.int32, (CHUNK, CHUNK), 0)
    coli = lax.broadcasted_iota(jnp.int32, (CHUNK, CHUNK), 1)
    causal = rowi >= coli
    a_row_t = a_col.T
    ax = _exact_r(a_col, e_mat)
    dtx = _exact_r(dt, e_mat)
    xdt = xs * dtx
    ax_last = ax[CHUNK - 1:CHUNK, :]
    e_a = jnp.exp(ax)
    w_end = xdt * jnp.exp(ax_last - ax)
    cd_col = jnp.exp(_exact_l(et_mat, a_row_t)[:, CHUNK - 1:CHUNK])
    ys, s_new = [], []
    for g in range(N_GROUPS):
        gs = slice(g * GROUP_CH, (g + 1) * GROUP_CH)
        bg = bm[:, g * D_STATE:(g + 1) * D_STATE]
        cg = cm[:, g * D_STATE:(g + 1) * D_STATE]
        sg = s_prev[gs, :]
        cb = _dot_nt(cg, bg)
        y_off = _dot_nt(cg, sg) * e_a[:, gs]
        s_new.append(sg * cd_col[gs, :] + _dot_tn(w_end[:, gs], bg))
        for pr in range(3):
            c0 = g * GROUP_CH + pr * 128
            xdt_p = xdt[:, c0:c0 + 128]
            h0 = g * 6 + pr * 2
            y0 = _dot(cb * _decay(a_col, a_row_t, h0, causal), xdt_p)
            y1 = _dot(cb * _decay(a_col, a_row_t, h0 + 1, causal), xdt_p)
            ys.append(jnp.where(lane < HEAD_DIM, y0, y1) + y_off[:, pr * 128:(pr + 1) * 128])
    y = jnp.concatenate(ys, axis=1) + d_x * xs
    return y, jnp.concatenate(s_new, axis=0)


def _ssd_chunk_bwd(xs, bm, cm, dt, a_row, a_col, s_prev, ds_new, dy, d_x, e_mat, et_mat, triu):
    lane = lax.broadcasted_iota(jnp.int32, (CHUNK, 128), 1)
    sub = lax.broadcasted_iota(jnp.int32, (CHUNK, 128), 0)
    rowi = lax.broadcasted_iota(jnp.int32, (CHUNK, CHUNK), 0)
    coli = lax.broadcasted_iota(jnp.int32, (CHUNK, CHUNK), 1)
    causal = rowi >= coli
    a_row_t = a_col.T
    ax = _exact_r(a_col, e_mat)
    dtx = _exact_r(dt, e_mat)
    xdt = xs * dtx
    ax_last = ax[CHUNK - 1:CHUNK, :]
    e_a = jnp.exp(ax)
    dte = jnp.exp(ax_last - ax)
    w_end = xdt * dte
    cd_col = jnp.exp(_exact_l(et_mat, a_row_t)[:, CHUNK - 1:CHUNK])
    dye = dy * e_a

    dxdt, zc, ww_all, dbs, dcs, dsp, t1s = [], [], [], [], [], [], []
    zcol = jnp.zeros((CHUNK, 128), F32)
    zrow = jnp.zeros((128, CHUNK), F32)
    for g in range(N_GROUPS):
        gs = slice(g * GROUP_CH, (g + 1) * GROUP_CH)
        bg = bm[:, g * D_STATE:(g + 1) * D_STATE]
        cg = cm[:, g * D_STATE:(g + 1) * D_STATE]
        sg = s_prev[gs, :]
        dsg = ds_new[gs, :]
        cb = _dot_nt(cg, bg)
        cs = _dot_nt(cg, sg)
        dcg = _dot(dye[:, gs], sg)
        dsp.append(dsg * cd_col[gs, :] + _dot_tn(dye[:, gs], cg))
        dwg = _dot_nt(bg, dsg)
        dbg = _dot(w_end[:, gs], dsg)
        ww = dwg * w_end[:, gs]
        ww_all.append(ww)
        zc.append(dye[:, gs] * cs - ww)
        t1s.append(jnp.sum(dsg * sg, axis=1, keepdims=True) * cd_col[gs, :])
        dxdt_g = dwg * dte[:, gs]
        dcb = jnp.zeros((CHUNK, CHUNK), F32)
        pairs = []
        for pr in range(3):
            c0 = g * GROUP_CH + pr * 128
            xdt_p = xdt[:, c0:c0 + 128]
            dy_p = dy[:, c0:c0 + 128]
            acc = None
            for half in range(2):
                h = g * 6 + pr * 2 + half
                ld = _decay(a_col, a_row_t, h, causal)
                gm = cb * ld
                dym = jnp.where((lane < HEAD_DIM) if half == 0 else (lane >= HEAD_DIM), dy_p, 0.0)
                dg = _dot_nt(dym, xdt_p)
                dseg = dg * gm
                dcb = dcb + dg * ld
                t = _dot_tn(gm, dym)
                acc = t if acc is None else acc + t
                zcol = jnp.where(lane == h, jnp.sum(dseg, axis=1, keepdims=True), zcol)
                zrow = jnp.where(sub == h, jnp.sum(dseg, axis=0, keepdims=True), zrow)
            pairs.append(acc)
        dxdt.append(dxdt_g + jnp.concatenate(pairs, axis=1))
        dcs.append(dcg + _dot(dcb, bg))
        dbs.append(dbg + _dot_tn(dcb, cg))
    dxdt = jnp.concatenate(dxdt, axis=1)
    zc = jnp.concatenate(zc, axis=1)
    ww_all = jnp.concatenate(ww_all, axis=1)
    t1 = jnp.concatenate(t1s, axis=0)

    sub8 = lax.broadcasted_iota(jnp.int32, (8, D_SSM), 0)
    col_sums = jnp.where(sub8 == 0, jnp.sum(dy * xs, axis=0, keepdims=True),
                         jnp.where(sub8 == 1, jnp.sum(ww_all, axis=0, keepdims=True), 0.0))
    head_sums = _exact_r(col_sums, et_mat)
    dd = head_sums[0:1, :]
    q_row = head_sums[1:2, :] + jnp.sum(et_mat.astype(F32) * t1, axis=0, keepdims=True)
    da_col = _contract(zc, et_mat) + zcol - zrow.T
    da_col = da_col + jnp.where(sub == CHUNK - 1, q_row, 0.0)
    ddt = _contract(dxdt * xs, et_mat)
    dxs = dxdt * dtx + d_x * dy
    rc = _exact_l(triu, da_col)
    ddt = ddt + rc * a_row
    da = jnp.sum(rc * dt, axis=0, keepdims=True)
    return (dxs, jnp.concatenate(dbs, axis=1), jnp.concatenate(dcs, axis=1), ddt, da, dd,
            jnp.concatenate(dsp, axis=0))


def _ssd_constants():
    ch = np.arange(D_SSM) // HEAD_DIM
    e = (np.arange(128)[:, None] == ch[None, :]).astype(np.float32)
    tril = np.tril(np.ones((CHUNK, CHUNK), np.float32))
    as_mx = lambda a: jnp.asarray(a, MXU_DTYPE)
    return as_mx(e), as_mx(e.T), as_mx(tril), as_mx(tril.T)


def _full(shape):
    nd = len(shape)
    return pl.BlockSpec(shape, lambda *_: (0,) * nd)


def _params(*sem):
    return pltpu.CompilerParams(dimension_semantics=sem, vmem_limit_bytes=VMEM_LIMIT)


def _in_proj(hp, w1, win):
    m = hp.shape[0]
    tm = 384

    def body(x_ref, w1_ref, win_hbm, hn_ref, proj_ref, win_v, sem):
        @pl.when(pl.program_id(0) == 0)
        def _():
            cp = pltpu.make_async_copy(win_hbm, win_v, sem)
            cp.start()
            cp.wait()

        x = x_ref[...]
        r = lax.rsqrt(jnp.mean(x * x, axis=-1, keepdims=True) + EPS)
        hn = _mx(x * r * w1_ref[...])
        hn_ref[...] = hn
        for j in range(0, PROJ_W, 512):
            w = min(512, PROJ_W - j)
            proj_ref[:, j:j + w] = jnp.dot(hn, win_v[:, j:j + w], preferred_element_type=F32)

    return pl.pallas_call(
        body, grid=(m // tm,), name="in_proj",
        in_specs=[pl.BlockSpec((tm, D_MODEL), lambda i: (i, 0)), _full((1, D_MODEL)), pl.BlockSpec(memory_space=pl.ANY)],
        out_specs=[pl.BlockSpec((tm, D_MODEL), lambda i: (i, 0)), pl.BlockSpec((tm, PROJ_W), lambda i: (i, 0))],
        out_shape=[jax.ShapeDtypeStruct((m, D_MODEL), MXU_DTYPE), jax.ShapeDtypeStruct((m, PROJ_W), F32)],
        scratch_shapes=[pltpu.VMEM((D_MODEL, PROJ_W), MXU_DTYPE), pltpu.SemaphoreType.DMA],
        compiler_params=_params("arbitrary"),
    )(hp, w1, win)


def _ffn_fwd_bwd(hp, y, tgt, wout, w2n, wff1, wff2, wfn, t_rows):
    m = hp.shape[0]
    tm = 256
    nb = m // t_rows
    nj = D_FF // 1024

    def body(hp_ref, y_ref, tgt_ref, w2n_ref, wfn_ref, wout_hbm, wff1_hbm, wff2_hbm,
             loss_ref, gwf_ref, gw2_ref, ff_ref, da_ref, hn2_ref, dh1_ref, dh2_ref, dy_ref,
             wout_v, wff1_v, wff2_v, a_s, sems):
        i = pl.program_id(0)

        @pl.when(i == 0)
        def _():
            cps = [pltpu.make_async_copy(s, d, sems.at[k])
                   for k, (s, d) in enumerate(((wout_hbm, wout_v), (wff1_hbm, wff1_v), (wff2_hbm, wff2_v)))]
            for cp in cps:
                cp.start()
            for cp in cps:
                cp.wait()
            loss_ref[...] = jnp.zeros_like(loss_ref)
            gwf_ref[...] = jnp.zeros_like(gwf_ref)
            gw2_ref[...] = jnp.zeros_like(gw2_ref)

        h1 = hp_ref[...] + jnp.dot(y_ref[...], wout_v[...], preferred_element_type=F32)
        r2 = lax.rsqrt(jnp.mean(h1 * h1, axis=-1, keepdims=True) + EPS)
        n2 = h1 * r2
        w2n_row = w2n_ref[...]
        hn2 = _mx(n2 * w2n_row)
        hn2_ref[...] = hn2
        h2 = h1
        for j in range(nj):
            js = slice(j * 1024, (j + 1) * 1024)
            a = jnp.dot(hn2, wff1_v[:, js], preferred_element_type=F32)
            a_s[:, js] = a
            ra = jnp.maximum(a, 0.0)
            ff = _mx(ra * ra)
            ff_ref[:, js] = ff
            h2 = h2 + jnp.dot(ff, wff2_v[js, :], preferred_element_type=F32)

        r3 = lax.rsqrt(jnp.mean(h2 * h2, axis=-1, keepdims=True) + EPS)
        n3 = h2 * r3
        wf_row = wfn_ref[...]
        err = n3 * wf_row - tgt_ref[...]
        row = i * tm + lax.broadcasted_iota(jnp.int32, (tm, 1), 0)
        tok = row < 0
        for b in range(nb):
            tok = tok | ((row >= b * t_rows + LEAD) & (row < (b + 1) * t_rows))
        tokf = tok.astype(F32)
        loss_ref[...] += 0.5 * jnp.sum(jnp.mean(err * err, axis=-1, keepdims=True) * tokf)
        dout = err * (tokf / D_MODEL)
        gwf_ref[...] += jnp.sum(dout * n3, axis=0, keepdims=True)
        dn3 = dout * wf_row
        dh2 = r3 * (dn3 - n3 * jnp.mean(dn3 * n3, axis=-1, keepdims=True))
        dh2m = _mx(dh2)
        dh2_ref[...] = dh2m

        dhn2 = jnp.zeros((tm, D_MODEL), F32)
        for j in range(nj):
            js = slice(j * 1024, (j + 1) * 1024)
            dff = lax.dot_general(dh2m, wff2_v[js, :], (((1,), (1,)), ((), ())), preferred_element_type=F32)
            da = _mx(dff * (2.0 * jnp.maximum(a_s[:, js], 0.0)))
            da_ref[:, js] = da
            dhn2 = dhn2 + lax.dot_general(da, wff1_v[:, js], (((1,), (1,)), ((), ())), preferred_element_type=F32)
        gw2_ref[...] += jnp.sum(dhn2 * n2, axis=0, keepdims=True)
        dn2 = dhn2 * w2n_row
        dh1 = dh2 + r2 * (dn2 - n2 * jnp.mean(dn2 * n2, axis=-1, keepdims=True))
        dh1_ref[...] = dh1
        dy_ref[...] = lax.dot_general(_mx(dh1), wout_v[...], (((1,), (1,)), ((), ())), preferred_element_type=F32)

    rows = lambda w: pl.BlockSpec((tm, w), lambda i: (i, 0))
    hbm = pl.BlockSpec(memory_space=pl.ANY)
    return pl.pallas_call(
        body, grid=(m // tm,), name="ffn_fwd_bwd",
        in_specs=[rows(D_MODEL), rows(D_MIX), rows(D_MODEL), _full((1, D_MODEL)), _full((1, D_MODEL)), hbm, hbm, hbm],
        out_specs=[_full((1, 128)), _full((1, D_MODEL)), _full((1, D_MODEL)), rows(D_FF), rows(D_FF), rows(D_MODEL),
                   rows(D_MODEL), rows(D_MODEL), rows(D_MIX)],
        out_shape=[jax.ShapeDtypeStruct((1, 128), F32), jax.ShapeDtypeStruct((1, D_MODEL), F32),
                   jax.ShapeDtypeStruct((1, D_MODEL), F32), jax.ShapeDtypeStruct((m, D_FF), MXU_DTYPE),
                   jax.ShapeDtypeStruct((m, D_FF), MXU_DTYPE), jax.ShapeDtypeStruct((m, D_MODEL), MXU_DTYPE),
                   jax.ShapeDtypeStruct((m, D_MODEL), F32), jax.ShapeDtypeStruct((m, D_MODEL), MXU_DTYPE),
                   jax.ShapeDtypeStruct((m, D_MIX), F32)],
        scratch_shapes=[pltpu.VMEM((D_MIX, D_MODEL), MXU_DTYPE), pltpu.VMEM((D_MODEL, D_FF), MXU_DTYPE),
                        pltpu.VMEM((D_FF, D_MODEL), MXU_DTYPE), pltpu.VMEM((tm, D_FF), F32),
                        pltpu.SemaphoreType.DMA((3,))],
        compiler_params=_params("arbitrary"),
    )(hp, y, tgt, w2n, wfn, wout, wff1, wff2)


def _in_proj_bwd(dproj, hp, dh1, w1, win, t_rows, chip_sums):
    m = hp.shape[0]
    tm = 384
    tiles_per_seq = t_rows // tm
    ns = len(chip_sums)

    def body(dp_ref, hp_ref, dh1_ref, w1_ref, win_hbm, *rest):
        cs_refs, (dh0_ref, gw1_ref, gmeta_ref) = rest[:ns], rest[ns:ns + 3]
        part_refs, (win_v, sem) = rest[ns + 3:2 * ns + 3], rest[2 * ns + 3:2 * ns + 5]
        exchange = _ChipExchange(cs_refs, part_refs, [False] * ns, *rest[2 * ns + 5:])
        i = pl.program_id(0)

        @pl.when(i == 0)
        def _():
            exchange.start()
            cp = pltpu.make_async_copy(win_hbm, win_v, sem)
            cp.start()
            cp.wait()
            gw1_ref[...] = jnp.zeros_like(gw1_ref)
            gmeta_ref[...] = jnp.zeros_like(gmeta_ref)

        dhn = lax.dot_general(dp_ref[...], win_v[...], (((1,), (1,)), ((), ())), preferred_element_type=F32)
        x = hp_ref[...]
        r = lax.rsqrt(jnp.mean(x * x, axis=-1, keepdims=True) + EPS)
        n = x * r
        gw1_ref[...] += jnp.sum(dhn * n, axis=0, keepdims=True)
        dn = dhn * w1_ref[...]
        dh0 = dh1_ref[...] + r * (dn - n * jnp.mean(dn * n, axis=-1, keepdims=True))
        dh0_ref[...] = dh0

        @pl.when(i % tiles_per_seq == 0)
        def _():
            gmeta_ref[...] += dh0[PAD_ROWS:LEAD, :]

        @pl.when(i == m // tm - 1)
        def _():
            exchange.finish()

    rows = lambda w: pl.BlockSpec((tm, w), lambda i: (i, 0))
    hbm = pl.BlockSpec(memory_space=pl.ANY)
    outs = pl.pallas_call(
        body, grid=(m // tm,), name="in_proj_bwd",
        in_specs=[rows(PROJ_W), rows(D_MODEL), rows(D_MODEL), _full((1, D_MODEL)), hbm] + [hbm] * ns,
        out_specs=[rows(D_MODEL), _full((1, D_MODEL)), _full((N_META, D_MODEL))] + [hbm] * ns,
        out_shape=[jax.ShapeDtypeStruct((m, D_MODEL), F32), jax.ShapeDtypeStruct((1, D_MODEL), F32),
                   jax.ShapeDtypeStruct((N_META, D_MODEL), F32)]
        + _ChipExchange.out_shapes(chip_sums, [False] * ns),
        scratch_shapes=[pltpu.VMEM((D_MODEL, PROJ_W), MXU_DTYPE), pltpu.SemaphoreType.DMA] + _ChipExchange.scratch(ns),
        compiler_params=_params("arbitrary"),
    )(dproj, hp, dh1, w1, win, *chip_sums)
    return outs[0], outs[1], outs[2], outs[3:]


def _tn_matmul(a, b, name, tka, tkm=768, tn=512):
    m, ka = a.shape
    nb = b.shape[1]
    n_steps = m // tkm

    def body(a_ref, b_ref, o_ref):
        @pl.when(pl.program_id(1) == 0)
        def _():
            o_ref[...] = jnp.zeros_like(o_ref)

        at = _mx(a_ref[...])
        for j in range(0, nb, tn):
            w = min(tn, nb - j)
            o_ref[:, j:j + w] += lax.dot_general(at, _mx(b_ref[:, j:j + w]), (((0,), (0,)), ((), ())),
                                                 preferred_element_type=F32)

    return pl.pallas_call(
        body, grid=(ka // tka, n_steps), name=name,
        in_specs=[pl.BlockSpec((tkm, tka), lambda i, k: (k, i)), pl.BlockSpec((tkm, nb), lambda i, k: (k, 0))],
        out_specs=pl.BlockSpec((tka, nb), lambda i, k: (i, 0)),
        out_shape=jax.ShapeDtypeStruct((ka, nb), F32),
        compiler_params=_params("arbitrary", "arbitrary"),
    )(a, b)


def _mixer_fwd(proj, cw, cb, dt_bias, a_log, d_x, nw, pool_w, pool_scale, nb, shards):
    m = proj.shape[0]
    nc = m // nb // CHUNK
    n_steps = nb * nc
    ns = len(shards)
    e_mat, et_mat, tril, _ = _ssd_constants()

    def body(p_ref, cw_ref, cb_ref, dtb_ref, alog_ref, dx_ref, nw_ref, pw_ref, ps_ref, e_ref, et_ref, tril_ref, *rest):
        shard_refs, (y_ref, ypre_ref, pre_ref, st_ref) = rest[:ns], rest[ns:ns + 4]
        gathered_refs, (xtail, utail, state) = rest[ns + 4:2 * ns + 4], rest[2 * ns + 4:2 * ns + 7]
        gather = _Gather(shard_refs, gathered_refs, *rest[2 * ns + 7:])
        c = pl.program_id(1)
        step = pl.program_id(0) * nc + c

        @pl.when(step == 0)
        def _():
            gather.start()

        @pl.when(step == n_steps // 2)
        def _():
            gather.forward()

        @pl.when(c == 0)
        def _():
            xtail[...] = jnp.zeros_like(xtail)
            utail[...] = jnp.zeros_like(utail)
            state[...] = jnp.zeros_like(state)

        valid = (c > 0) | (lax.broadcasted_iota(jnp.int32, (CHUNK, 1), 0) >= PAD_ROWS)

        u = p_ref[:, 0:D_POOL]
        inv_cnt, lane = _pool_inv_count(c)
        win = _pool_window_sums(jnp.concatenate([utail[...], u], axis=0), lane)
        utail[...] = u[CHUNK - HALO:, :]
        pooled = win * inv_cnt - u
        mixed = jnp.concatenate(
            [_dot(pooled[:, g * 128:(g + 1) * 128], pw_ref[g]) for g in range(len(POOL_WINDOWS))], axis=1)
        y_ref[:, 0:D_POOL] = _mx(mixed * ps_ref[...])

        xbc = p_ref[:, OFF_X:OFF_X + D_XBC]
        pre = _conv_pre(jnp.concatenate([xtail[...], xbc], axis=0), xbc, cw_ref[...], cb_ref[...])
        xtail[...] = xbc[CHUNK - HALO:, :]
        pre_ref[...] = pre
        xc = pre * _sigmoid(pre)
        dt, _, a_col, _, _ = _dt_and_cumsum(p_ref[:, OFF_DT:OFF_DT + 128], dtb_ref[...], alog_ref[...], valid,
                                            tril_ref[...])
        s_prev = state[...]
        st_ref[0] = s_prev
        yp, s_new = _ssd_chunk_fwd(xc[:, 0:D_SSM], xc[:, D_SSM:D_SSM + 512], xc[:, D_SSM + 512:], dt, a_col, s_prev,
                                   dx_ref[...], e_ref[...], et_ref[...])
        state[...] = s_new
        ypre_ref[...] = yp
        z = p_ref[:, OFF_Z:OFF_Z + D_SSM]
        yz = yp * (z * _sigmoid(z))
        outs = []
        for g in range(N_GROUPS):
            gs = slice(g * GROUP_CH, (g + 1) * GROUP_CH)
            r = lax.rsqrt(jnp.mean(yz[:, gs] * yz[:, gs], axis=-1, keepdims=True) + EPS)
            outs.append(yz[:, gs] * r)
        y_ref[:, D_POOL:] = _mx(jnp.concatenate(outs, axis=1) * nw_ref[...])

        @pl.when(step == n_steps - 1)
        def _():
            gather.finish()

    blk = lambda w: pl.BlockSpec((CHUNK, w), lambda b, c: (b * nc + c, 0))
    hbm = pl.BlockSpec(memory_space=pl.ANY)
    outs = pl.pallas_call(
        body, grid=(nb, nc), name="mixer_fwd",
        in_specs=[blk(PROJ_W), _full((4, D_XBC)), _full((1, D_XBC)), _full((1, 128)), _full((1, 128)),
                  _full((1, D_SSM)), _full((1, D_SSM)), _full((4, 128, 128)), _full((1, D_POOL)),
                  _full((128, D_SSM)), _full((D_SSM, 128)), _full((CHUNK, CHUNK))] + [hbm] * ns,
        out_specs=[blk(D_MIX), blk(D_SSM), blk(D_XBC),
                   pl.BlockSpec((1, D_SSM, D_STATE), lambda b, c: (b * nc + c, 0, 0))] + [hbm] * ns,
        out_shape=[jax.ShapeDtypeStruct((m, D_MIX), MXU_DTYPE), jax.ShapeDtypeStruct((m, D_SSM), F32),
                   jax.ShapeDtypeStruct((m, D_XBC), F32), jax.ShapeDtypeStruct((m // CHUNK, D_SSM, D_STATE), F32)]
        + [jax.ShapeDtypeStruct((N_DEV,) + s.shape, s.dtype) for s in shards],
        scratch_shapes=[pltpu.VMEM((HALO, D_XBC), F32), pltpu.VMEM((HALO, D_POOL), F32),
                        pltpu.VMEM((D_SSM, D_STATE), F32)] + _Gather.scratch(ns),
        compiler_params=_params("arbitrary", "arbitrary"),
    )(proj, cw, cb, dt_bias, a_log, d_x, nw, pool_w, pool_scale, e_mat, et_mat, tril, *shards)
    return outs[0], outs[1], outs[2], outs[3], outs[4:]


def _mixer_bwd(proj, dy, ypre, conv_pre, states, cw, dt_bias, a_log, d_x, nw, pool_w, pool_scale, nb, chip_sums):
    m = proj.shape[0]
    nc = m // nb // CHUNK
    e_mat, et_mat, tril, triu = _ssd_constants()
    hb = CHUNK // HALO
    ns = len(chip_sums)

    def body(p_ref, halo_ref, dy_ref, ypre_ref, pre_ref, st_ref, cw_ref, dtb_ref, alog_ref, dx_ref, nw_ref, pw_ref,
             ps_ref, e_ref, et_ref, tril_ref, triu_ref, *rest):
        cs_refs = rest[:ns]
        dp_ref, gcw_ref, gcb_ref, gdtb_ref, galog_ref, gd_ref, gnw_ref, gpw_ref, gps_ref = rest[ns:ns + 9]
        part_refs, (ds_carry, dpre_next, dq_next) = rest[ns + 9:2 * ns + 9], rest[2 * ns + 9:2 * ns + 12]
        exchange = _ChipExchange(cs_refs, part_refs, [False] * ns, *rest[2 * ns + 12:])
        b = pl.program_id(0)
        cc = pl.program_id(1)
        c = nc - 1 - cc

        @pl.when((b == 0) & (cc == 0))
        def _():
            exchange.start()
            for r in (gcw_ref, gcb_ref, gdtb_ref, galog_ref, gd_ref, gnw_ref, gpw_ref, gps_ref):
                r[...] = jnp.zeros_like(r)

        @pl.when(cc == 0)
        def _():
            ds_carry[...] = jnp.zeros_like(ds_carry)
            dpre_next[...] = jnp.zeros_like(dpre_next)
            dq_next[...] = jnp.zeros_like(dq_next)

        valid = (c > 0) | (lax.broadcasted_iota(jnp.int32, (CHUNK, 1), 0) >= PAD_ROWS)
        first = c > 0

        u = p_ref[:, 0:D_POOL]
        u_halo = jnp.where(first, halo_ref[...], 0.0)
        inv_cnt, lane = _pool_inv_count(c)
        pooled = _pool_window_sums(jnp.concatenate([u_halo, u], axis=0), lane) * inv_cnt - u
        dyp = dy_ref[:, 0:D_POOL]
        ps = ps_ref[...]
        dmixed = dyp * ps
        mixed, dpooled = [], []
        for g in range(len(POOL_WINDOWS)):
            gsl = slice(g * 128, (g + 1) * 128)
            pw = pw_ref[g]
            mixed.append(_dot(pooled[:, gsl], pw))
            dpooled.append(_dot_nt(dmixed[:, gsl], pw))
            gpw_ref[g] += _dot_tn(pooled[:, gsl], dmixed[:, gsl])
        gps_ref[...] += jnp.sum(dyp * jnp.concatenate(mixed, axis=1), axis=0, keepdims=True)
        dpooled = jnp.concatenate(dpooled, axis=1)
        dq = dpooled * inv_cnt
        du = _pool_window_sums_ahead(jnp.concatenate([dq, dq_next[...]], axis=0), lane) - dpooled
        dq_next[...] = dq[0:HALO, :]
        dp_ref[:, 0:D_POOL] = _mx(du)

        yp = ypre_ref[...]
        z = p_ref[:, OFF_Z:OFF_Z + D_SSM]
        sz, dsz = _silu_and_grad(z)
        yz = yp * sz
        do = dy_ref[:, D_POOL:]
        nw_row = nw_ref[...]
        dyz = []
        gnw = []
        for g in range(N_GROUPS):
            gs = slice(g * GROUP_CH, (g + 1) * GROUP_CH)
            r = lax.rsqrt(jnp.mean(yz[:, gs] * yz[:, gs], axis=-1, keepdims=True) + EPS)
            n = yz[:, gs] * r
            gnw.append(jnp.sum(do[:, gs] * n, axis=0, keepdims=True))
            dn = do[:, gs] * nw_row[:, gs]
            dyz.append(r * (dn - n * jnp.mean(dn * n, axis=-1, keepdims=True)))
        gnw_ref[...] += jnp.concatenate(gnw, axis=1)
        dyz = jnp.concatenate(dyz, axis=1)
        dp_ref[:, OFF_Z:OFF_Z + D_SSM] = _mx(dyz * yp * dsz)
        dyp_ssm = dyz * sz

        xc, dsilu = _silu_and_grad(pre_ref[...])
        dtr = p_ref[:, OFF_DT:OFF_DT + 128]
        dt, a_row, a_col, dt_pre, head = _dt_and_cumsum(dtr, dtb_ref[...], alog_ref[...], valid, tril_ref[...])
        dxs, dbm, dcm, ddt, da, dd, ds_prev = _ssd_chunk_bwd(
            xc[:, 0:D_SSM], xc[:, D_SSM:D_SSM + 512], xc[:, D_SSM + 512:], dt, a_row, a_col, st_ref[0],
            ds_carry[...], dyp_ssm, dx_ref[...], e_ref[...], et_ref[...], triu_ref[...])
        ds_carry[...] = ds_prev
        gd_ref[...] += dd
        galog_ref[...] += da * a_row
        ddtr = jnp.where(valid & head, ddt * _sigmoid(dt_pre), 0.0)
        gdtb_ref[...] += jnp.sum(ddtr, axis=0, keepdims=True)
        dp_ref[:, OFF_DT:OFF_DT + 128] = _mx(ddtr)

        dpre = jnp.concatenate([dxs, dbm, dcm], axis=1) * dsilu
        gcb_ref[...] += jnp.sum(dpre, axis=0, keepdims=True)
        dext = jnp.concatenate([dpre, dpre_next[...]], axis=0)
        dpre_next[...] = dpre[0:HALO, :]
        ups = [_shift_up(dext, 3 - k) for k in range(4)]
        xbc = p_ref[:, OFF_X:OFF_X + D_XBC]
        gcw_ref[...] += jnp.concatenate([jnp.sum(xbc * ups[k], axis=0, keepdims=True) for k in range(4)], axis=0)
        cw = cw_ref[...]
        dp_ref[:, OFF_X:OFF_X + D_XBC] = _mx(cw[3:4, :] * ups[3] + cw[2:3, :] * ups[2]
                                             + cw[1:2, :] * ups[1] + cw[0:1, :] * ups[0])

        @pl.when((b == nb - 1) & (cc == nc - 1))
        def _():
            exchange.finish()

    blk = lambda w: pl.BlockSpec((CHUNK, w), lambda b, cc: (b * nc + nc - 1 - cc, 0))
    halo = pl.BlockSpec((HALO, D_POOL), lambda b, cc: (jnp.maximum((b * nc + nc - 1 - cc) * hb - 1, 0), 0))
    hbm = pl.BlockSpec(memory_space=pl.ANY)
    outs = pl.pallas_call(
        body, grid=(nb, nc), name="mixer_bwd",
        in_specs=[blk(PROJ_W), halo, blk(D_MIX), blk(D_SSM), blk(D_XBC),
                  pl.BlockSpec((1, D_SSM, D_STATE), lambda b, cc: (b * nc + nc - 1 - cc, 0, 0)),
                  _full((4, D_XBC)), _full((1, 128)), _full((1, 128)), _full((1, D_SSM)),
                  _full((1, D_SSM)), _full((4, 128, 128)), _full((1, D_POOL)),
                  _full((128, D_SSM)), _full((D_SSM, 128)), _full((CHUNK, CHUNK)), _full((CHUNK, CHUNK))] + [hbm] * ns,
        out_specs=[blk(PROJ_W), _full((4, D_XBC)), _full((1, D_XBC)), _full((1, 128)), _full((1, 128)), _full((1, 128)),
                   _full((1, D_SSM)), _full((4, 128, 128)), _full((1, D_POOL))] + [hbm] * ns,
        out_shape=[jax.ShapeDtypeStruct((m, PROJ_W), MXU_DTYPE), jax.ShapeDtypeStruct((4, D_XBC), F32),
                   jax.ShapeDtypeStruct((1, D_XBC), F32), jax.ShapeDtypeStruct((1, 128), F32),
                   jax.ShapeDtypeStruct((1, 128), F32), jax.ShapeDtypeStruct((1, 128), F32),
                   jax.ShapeDtypeStruct((1, D_SSM), F32), jax.ShapeDtypeStruct((4, 128, 128), F32),
                   jax.ShapeDtypeStruct((1, D_POOL), F32)] + _ChipExchange.out_shapes(chip_sums, [False] * ns),
        scratch_shapes=[pltpu.VMEM((D_SSM, D_STATE), F32), pltpu.VMEM((HALO, D_XBC), F32),
                        pltpu.VMEM((HALO, D_POOL), F32)] + _ChipExchange.scratch(ns),
        compiler_params=_params("arbitrary", "arbitrary"),
    )(proj, proj, dy, ypre, conv_pre, states, cw, dt_bias, a_log, d_x, nw, pool_w, pool_scale, e_mat, et_mat, tril, triu,
      *chip_sums)
    return outs[:9], outs[9:]


MESH_IDS = pl.DeviceIdType.MESH
_HBM = pl.BlockSpec(memory_space=pltpu.HBM)


def _coords():
    return lax.axis_index("x"), lax.axis_index("y"), lax.axis_index("c")


def _other_chips(x, y):
    return [(1 - x, y), (x, 1 - y), (1 - x, 1 - y)]


class _Gather:
    def __init__(self, ins, outs, send_sems, recv_sems, local_sems):
        self.ins, self.outs, self.n = ins, outs, len(ins)
        self.send_sems, self.recv_sems, self.local_sems = send_sems, recv_sems, local_sems
        self.x, self.y, self.c = _coords()
        self.me, self.sibling = (self.x, self.y, self.c), (self.x, self.y, 1 - self.c)
        self.chips = _other_chips(self.x, self.y)

    @staticmethod
    def scratch(n):
        return [pltpu.SemaphoreType.DMA((7 * n,)), pltpu.SemaphoreType.DMA((7 * n,)), pltpu.SemaphoreType.DMA((n,))]

    def _copy(self, t, k, block, to, own=False):
        dst = self.outs[t].at[4 * block[0] + 2 * block[1] + block[2]]
        return pltpu.make_async_remote_copy(
            src_ref=self.ins[t] if own else dst, dst_ref=dst, send_sem=self.send_sems.at[t * 7 + k],
            recv_sem=self.recv_sems.at[t * 7 + k], device_id=to, device_id_type=MESH_IDS)

    def _mine(self):
        return [pltpu.make_async_copy(self.ins[t], self.outs[t].at[4 * self.x + 2 * self.y + self.c],
                                      self.local_sems.at[t]) for t in range(self.n)]

    def _first(self):
        cps = []
        for t in range(self.n):
            cps.append(self._copy(t, 0, self.me, self.sibling, own=True))
            cps += [self._copy(t, 1 + j, self.me, (*chip, self.c), own=True) for j, chip in enumerate(self.chips)]
        return cps

    def _passed(self):
        return [self._copy(t, 4 + j, (*chip, self.c), self.sibling)
                for j, chip in enumerate(self.chips) for t in range(self.n)]

    def start(self):
        for cp in self._mine() + self._first():
            cp.start()

    def forward(self):
        for j, chip in enumerate(self.chips):
            for t in range(self.n):
                self._copy(t, 1 + j, (*chip, self.c), self.me).wait_recv()
                self._copy(t, 4 + j, (*chip, self.c), self.sibling).start()

    def finish(self):
        for t in range(self.n):
            self._copy(t, 0, self.sibling, self.me).wait_recv()
            for j, chip in enumerate(self.chips):
                self._copy(t, 4 + j, (*chip, 1 - self.c), self.me).wait_recv()
        for cp in self._first() + self._passed():
            cp.wait_send()
        for cp in self._mine():
            cp.wait()


def _weight_gather(shards):
    n = len(shards)

    def body(*refs):
        g = _Gather(refs[:n], refs[n:2 * n], *refs[2 * n:])
        g.start()
        g.forward()
        g.finish()

    return pl.pallas_call(
        body, name="weight_gather",
        in_specs=[_HBM] * n, out_specs=[_HBM] * n,
        out_shape=[jax.ShapeDtypeStruct((N_DEV,) + s.shape, s.dtype) for s in shards],
        scratch_shapes=_Gather.scratch(n),
    )(*shards)


def _grad_exchange_d2d(gs, name):
    n = len(gs)

    def body(*refs):
        ins, got = refs[:n], refs[n:2 * n]
        send_sems, recv_sems = refs[2 * n:]
        x, y, c = _coords()
        remote = [pltpu.make_async_remote_copy(
            src_ref=ins[t].at[k, 1 - c], dst_ref=got[t].at[k], send_sem=send_sems.at[t * 4 + k],
            recv_sem=recv_sems.at[t * 4 + k], device_id=(x, y, 1 - c), device_id_type=MESH_IDS)
            for t in range(n) for k in range(4)]
        for cp in remote:
            cp.start()
        for cp in remote:
            cp.wait_recv()
        for cp in remote:
            cp.wait_send()

    return pl.pallas_call(
        body, name=name,
        in_specs=[_HBM] * n, out_specs=[_HBM] * n,
        out_shape=[jax.ShapeDtypeStruct((4,) + g.shape[2:], g.dtype) for g in gs],
        scratch_shapes=[pltpu.SemaphoreType.DMA((4 * n,)), pltpu.SemaphoreType.DMA((4 * n,))],
    )(*gs)


def _small_allreduce(pack):
    rows = pack.shape[0]

    def body(p_ref, o_ref, sib_ref, parts_ref, send_sems, recv_sems):
        x, y, c = _coords()
        my_chip = 2 * x + y
        swap = pltpu.make_async_remote_copy(src_ref=p_ref, dst_ref=sib_ref, send_sem=send_sems.at[0],
                                            recv_sem=recv_sems.at[0], device_id=(x, y, 1 - c), device_id_type=MESH_IDS)
        swap.start()
        swap.wait_recv()
        parts_ref[my_chip] = p_ref[...] + sib_ref[...]
        remote = [pltpu.make_async_remote_copy(
            src_ref=parts_ref.at[my_chip], dst_ref=parts_ref.at[my_chip], send_sem=send_sems.at[1 + j],
            recv_sem=recv_sems.at[1 + j], device_id=(cx, cy, c), device_id_type=MESH_IDS)
            for j, (cx, cy) in enumerate(_other_chips(x, y))]
        for cp in remote:
            cp.start()
        for j, (cx, cy) in enumerate(_other_chips(x, y)):
            slot = parts_ref.at[2 * cx + cy]
            pltpu.make_async_remote_copy(src_ref=slot, dst_ref=slot, send_sem=send_sems.at[1 + j],
                                         recv_sem=recv_sems.at[1 + j], device_id=(cx, cy, c),
                                         device_id_type=MESH_IDS).wait_recv()
        o_ref[...] = ((parts_ref[0] + parts_ref[1]) + parts_ref[2]) + parts_ref[3]
        swap.wait_send()
        for cp in remote:
            cp.wait_send()

    vmem = pl.BlockSpec(memory_space=pltpu.VMEM)
    return pl.pallas_call(
        body, name="small_allreduce", in_specs=[vmem], out_specs=vmem,
        out_shape=jax.ShapeDtypeStruct((rows, 128), F32),
        scratch_shapes=[pltpu.VMEM((rows, 128), F32), pltpu.VMEM((4, rows, 128), F32),
                        pltpu.SemaphoreType.DMA((4,)), pltpu.SemaphoreType.DMA((4,))],
    )(pack)


class _ChipExchange:
    def __init__(self, ins, outs, whole, send_sems, recv_sems, local_sems):
        self.ins, self.outs, self.whole, self.n = ins, outs, whole, len(ins)
        self.send_sems, self.recv_sems, self.local_sems = send_sems, recv_sems, local_sems
        self.x, self.y, self.c = _coords()
        self.my_chip = 2 * self.x + self.y
        self.chips = _other_chips(self.x, self.y)

    @staticmethod
    def scratch(n):
        return [pltpu.SemaphoreType.DMA((3 * n,)), pltpu.SemaphoreType.DMA((3 * n,)), pltpu.SemaphoreType.DMA((n,))]

    def _src(self, t, k):
        return self.ins[t] if self.whole[t] else self.ins[t].at[k]

    def _local(self):
        return [pltpu.make_async_copy(self._src(t, self.my_chip), self.outs[t].at[self.my_chip], self.local_sems.at[t])
                for t in range(self.n)]

    def _remote(self):
        return [pltpu.make_async_remote_copy(
            src_ref=self._src(t, 2 * cx + cy), dst_ref=self.outs[t].at[self.my_chip],
            send_sem=self.send_sems.at[t * 3 + j], recv_sem=self.recv_sems.at[t * 3 + j],
            device_id=(cx, cy, self.c), device_id_type=MESH_IDS)
            for t in range(self.n) for j, (cx, cy) in enumerate(self.chips)]

    def start(self):
        for cp in self._remote() + self._local():
            cp.start()

    def finish(self):
        for t in range(self.n):
            for j, (cx, cy) in enumerate(self.chips):
                slot = self.outs[t].at[2 * cx + cy]
                pltpu.make_async_remote_copy(
                    src_ref=slot, dst_ref=slot, send_sem=self.send_sems.at[t * 3 + j],
                    recv_sem=self.recv_sems.at[t * 3 + j], device_id=(cx, cy, self.c),
                    device_id_type=MESH_IDS).wait_recv()
        for cp in self._remote():
            cp.wait_send()
        for cp in self._local():
            cp.wait()

    @staticmethod
    def out_shapes(arrs, whole):
        return [jax.ShapeDtypeStruct(((4,) + a.shape) if w else a.shape, a.dtype) for a, w in zip(arrs, whole)]


def _row_tile(rows, cols, n_arrays):
    budget = 24 * 1024 * 1024
    padded = -(-cols // 128) * 128
    step = 16 if rows % 16 == 0 else 8
    tr = max(step, budget // (n_arrays * 2 * 4 * padded) // step * step)
    while rows % tr:
        tr -= step
    return tr


def _chip_sum(g4, got, core, name):
    _, _, rows, cols = g4.shape
    tr = _row_tile(rows, cols, 3)

    def body(c_ref, a_ref, b_ref, o_ref):
        o_ref[...] = (a_ref[...] + b_ref[...]).astype(o_ref.dtype)

    grid_spec = pltpu.PrefetchScalarGridSpec(
        num_scalar_prefetch=1, grid=(4, rows // tr),
        in_specs=[pl.BlockSpec((None, None, tr, cols), lambda k, i, c: (k, c[0], i, 0)),
                  pl.BlockSpec((None, tr, cols), lambda k, i, c: (k, i, 0))],
        out_specs=pl.BlockSpec((None, tr, cols), lambda k, i, c: (k, i, 0)))
    return pl.pallas_call(body, grid_spec=grid_spec, name=name,
                          out_shape=jax.ShapeDtypeStruct((4, rows, cols), MXU_DTYPE),
                          compiler_params=_params("arbitrary", "arbitrary"))(core, g4, got)


def _adamw_math(w, g, m, v):
    m2 = ADAM_B1 * m + (1.0 - ADAM_B1) * g
    v2 = ADAM_B2 * v + (1.0 - ADAM_B2) * (g * g)
    m_hat = m2 / (1.0 - ADAM_B1 ** ADAM_STEP)
    v_hat = v2 / (1.0 - ADAM_B2 ** ADAM_STEP)
    delta = -ADAM_LR * (m_hat / (jnp.sqrt(v_hat) + ADAM_EPS) + ADAM_WD * w)
    return delta, m2, v2


def _adamw(parts, w, m, v, name):
    rows, cols = w.shape
    tr = _row_tile(rows, cols, 11)

    def body(p_ref, w_ref, m_ref, v_ref, g_ref, d_ref, m2_ref, v2_ref):
        part = lambda k: p_ref[k].astype(F32)
        g = ((part(0) + part(1)) + part(2)) + part(3)
        d, m2, v2 = _adamw_math(w_ref[...], g, m_ref[...], v_ref[...])
        g_ref[...] = g
        d_ref[...] = d
        m2_ref[...] = m2
        v2_ref[...] = v2

    blk = pl.BlockSpec((tr, cols), lambda i: (i, 0))
    out = jax.ShapeDtypeStruct((rows, cols), F32)
    return pl.pallas_call(body, grid=(rows // tr,), name=name,
                          in_specs=[pl.BlockSpec((4, tr, cols), lambda i: (0, i, 0)), blk, blk, blk],
                          out_specs=[blk] * 4, out_shape=[out] * 4,
                          compiler_params=_params("arbitrary"))(parts, w, m, v)


def _adamw_small(gs, ws, ms, vs):
    n = len(ws)

    def body(*refs):
        g_refs, w_refs, m_refs, v_refs = (refs[k * n:(k + 1) * n] for k in range(4))
        d_refs, m2_refs, v2_refs = (refs[(4 + k) * n:(5 + k) * n] for k in range(3))
        for t in range(n):
            d, m2, v2 = _adamw_math(w_refs[t][...], g_refs[t][...], m_refs[t][...], v_refs[t][...])
            d_refs[t][...] = d
            m2_refs[t][...] = m2
            v2_refs[t][...] = v2

    outs = pl.pallas_call(body, name="adamw_small",
                          out_shape=[jax.ShapeDtypeStruct(w.shape, F32) for w in ws] * 3)(*gs, *ws, *ms, *vs)
    return outs[:n], outs[n:2 * n], outs[2 * n:]


_PACK_TILE = 8 * 128


def _pack(arrays):
    rows = []
    for a in arrays:
        flat = a.astype(F32).reshape(-1)
        rows.append(jnp.pad(flat, (0, -flat.shape[0] % _PACK_TILE)).reshape(-1, 128))
    return jnp.concatenate(rows, axis=0)


def _unpack(pack, shapes):
    out, r = [], 0
    for s in shapes:
        n = int(np.prod(s))
        out.append(pack[r:r + -(-n // 128)].reshape(-1)[:n].reshape(s))
        r += -(-n // _PACK_TILE) * 8
    return out


def _pad128(v):
    v = v.reshape(1, -1).astype(F32)
    return jnp.pad(v, ((0, 0), (0, 128 - v.shape[1])))


_WEIGHTS = ["meta", "norm_mix_w", "w_in", "pool_w", "pool_scale", "conv_w", "conv_b", "dt_bias", "a_log", "d_skip",
            "ssm_norm_w", "w_out", "norm_ffn_w", "w_ff1", "w_ff2", "norm_f_w"]
_BIG = ["w_in", "w_out", "w_ff1", "w_ff2"]
_SMALL = [n for n in _WEIGHTS if n not in _BIG]


def kernel(x, meta, norm_mix_w, w_in, pool_w, pool_scale, conv_w, conv_b, dt_bias, a_log, d_skip, ssm_norm_w, w_out, norm_ffn_w, w_ff1, w_ff2, norm_f_w, loss_target, m_meta, m_norm_mix_w, m_w_in, m_pool_w, m_pool_scale, m_conv_w, m_conv_b, m_dt_bias, m_a_log, m_d_skip, m_ssm_norm_w, m_w_out, m_norm_ffn_w, m_w_ff1, m_w_ff2, m_norm_f_w, v_meta, v_norm_mix_w, v_w_in, v_pool_w, v_pool_scale, v_conv_w, v_conv_b, v_dt_bias, v_a_log, v_d_skip, v_ssm_norm_w, v_w_out, v_norm_ffn_w, v_w_ff1, v_w_ff2, v_norm_f_w):
    wts = dict(meta=meta, norm_mix_w=norm_mix_w, w_in=w_in, pool_w=pool_w, pool_scale=pool_scale, conv_w=conv_w,
               conv_b=conv_b, dt_bias=dt_bias, a_log=a_log, d_skip=d_skip, ssm_norm_w=ssm_norm_w, w_out=w_out,
               norm_ffn_w=norm_ffn_w, w_ff1=w_ff1, w_ff2=w_ff2, norm_f_w=norm_f_w)
    mom1 = dict(zip(_WEIGHTS, (m_meta, m_norm_mix_w, m_w_in, m_pool_w, m_pool_scale, m_conv_w, m_conv_b, m_dt_bias,
                               m_a_log, m_d_skip, m_ssm_norm_w, m_w_out, m_norm_ffn_w, m_w_ff1, m_w_ff2, m_norm_f_w)))
    mom2 = dict(zip(_WEIGHTS, (v_meta, v_norm_mix_w, v_w_in, v_pool_w, v_pool_scale, v_conv_w, v_conv_b, v_dt_bias,
                               v_a_log, v_d_skip, v_ssm_norm_w, v_w_out, v_norm_ffn_w, v_w_ff1, v_w_ff2, v_norm_f_w)))
    xi, yi, ci = _coords()
    dev = 4 * xi + 2 * yi + ci
    win_cols = w_in.shape[-1]
    cw_cols = conv_w.shape[-1]

    nb, seq, _ = x.shape
    t_rows = LEAD + seq
    core = jnp.reshape(ci, (1,)).astype(jnp.int32)
    owners = lambda a: a.reshape((4, 2) + a.shape[1:])

    lead_pack = jnp.zeros((N_META, 512), F32)
    lead_pack = lead_pack.at[:, :128].set(meta).at[:4, 128:128 + cw_cols].set(conv_w[0])
    g_win, g_lead = _weight_gather([_mx(w_in[0]), lead_pack])
    win_full = jnp.transpose(g_win, (1, 0, 2)).reshape(D_MODEL, N_DEV * win_cols)
    win_full = jnp.pad(win_full, ((0, 0), (0, PROJ_W - N_DEV * win_cols)))
    meta_full = jnp.transpose(g_lead[:, :, :128], (1, 0, 2)).reshape(N_META, D_MODEL)
    cw_full = jnp.transpose(g_lead[:, :4, 128:128 + cw_cols], (1, 0, 2)).reshape(4, D_XBC)

    lead = jnp.concatenate([jnp.zeros((PAD_ROWS, D_MODEL), F32), meta_full], axis=0)
    hp = jnp.concatenate([jnp.broadcast_to(lead[None], (nb, LEAD, D_MODEL)), x], axis=1).reshape(nb * t_rows, D_MODEL)
    tgt_p = jnp.pad(loss_target, ((0, 0), (LEAD, 0), (0, 0))).reshape(nb * t_rows, D_MODEL)
    dt_bias_p, a_log_p = _pad128(dt_bias), _pad128(a_log)
    d_x = jnp.repeat(d_skip.reshape(1, N_HEADS).astype(F32), HEAD_DIM, axis=1)
    norm_f_row = norm_f_w.reshape(1, D_MODEL)

    hn1, proj = _in_proj(hp, norm_mix_w, win_full)
    y, ypre, conv_pre, states, (g_wout, g_wff1, g_wff2) = _mixer_fwd(
        proj, cw_full, conv_b, dt_bias_p, a_log_p, d_x, ssm_norm_w, pool_w[0], pool_scale, nb,
        [_mx(w_out[0]), _mx(w_ff1[0]), _mx(w_ff2[0])])
    wout_full = g_wout.reshape(D_MIX, D_MODEL)
    wff1_full = jnp.transpose(g_wff1, (1, 0, 2)).reshape(D_MODEL, D_FF)
    wff2_full = g_wff2.reshape(D_FF, D_MODEL)
    loss, gr_nf, gr_nffn, ff, da, hn2, dh1, dh2, dy = _ffn_fwd_bwd(
        hp, y, tgt_p, wout_full, norm_ffn_w, wff1_full, wff2_full, norm_f_row, t_rows)
    gr_wff2 = _tn_matmul(ff, dh2, "grad_w_ff2", tka=1024)
    gr_wff1 = _tn_matmul(hn2, da, "grad_w_ff1", tka=512)
    gr_wout = _tn_matmul(y, dh1, "grad_w_out", tka=1024)

    late_parts = [owners(gr_wout.reshape(N_DEV, D_MIX // N_DEV, D_MODEL)),
                  owners(jnp.transpose(gr_wff1.reshape(D_MODEL, N_DEV, D_FF // N_DEV), (1, 0, 2))),
                  owners(gr_wff2.reshape(N_DEV, D_FF // N_DEV, D_MODEL))]
    late_got = _grad_exchange_d2d(late_parts, "grad_exchange_d2d_late")
    late_sums = [_chip_sum(a, b, core, "chip_sum_late_%d" % t) for t, (a, b) in enumerate(zip(late_parts, late_got))]
    (dproj, gr_cw, gr_cb, gr_dtb, gr_alog, gr_d, gr_nw, gr_pw, gr_ps), late_exchanged = _mixer_bwd(
        proj, dy, ypre, conv_pre, states, cw_full, dt_bias_p, a_log_p, d_x, ssm_norm_w, pool_w[0], pool_scale, nb,
        late_sums)

    gr_win = _tn_matmul(hn1, dproj, "grad_w_in", tka=512)
    win_parts = [owners(jnp.transpose(gr_win[:, :N_DEV * win_cols].reshape(D_MODEL, N_DEV, win_cols), (1, 0, 2)))]
    win_got = _grad_exchange_d2d(win_parts, "grad_exchange_d2d_w_in")
    win_sum = _chip_sum(win_parts[0], win_got[0], core, "chip_sum_w_in")
    dh0, gr_nmix, gr_meta, win_exchanged = _in_proj_bwd(dproj, hp, dh1, norm_mix_w, win_full, t_rows, [win_sum])
    parts = dict(w_in=win_exchanged[0], w_out=late_exchanged[0], w_ff1=late_exchanged[1], w_ff2=late_exchanged[2])

    small_full = dict(meta=gr_meta, norm_mix_w=gr_nmix, pool_w=gr_pw, pool_scale=gr_ps, conv_w=gr_cw, conv_b=gr_cb,
                      dt_bias=gr_dtb[:, :N_HEADS], a_log=gr_alog[:, :N_HEADS], d_skip=gr_d[:, :N_HEADS],
                      ssm_norm_w=gr_nw, norm_ffn_w=gr_nffn, norm_f_w=gr_nf, loss=loss[0:1, 0:1])
    small_names = list(small_full)
    small_sum = _small_allreduce(_pack([small_full[n] for n in small_names]))
    gs = dict(zip(small_names, _unpack(small_sum, [small_full[n].shape for n in small_names])))
    gs["meta"] = lax.dynamic_slice_in_dim(gs["meta"], dev * 128, 128, axis=1)
    gs["conv_w"] = lax.dynamic_slice_in_dim(gs["conv_w"], dev * cw_cols, cw_cols, axis=1)

    res = {}
    for n in _BIG:
        shp = wts[n].shape
        res[n] = [o.reshape(shp) for o in _adamw(parts[n], wts[n][0], mom1[n][0], mom2[n][0], "adamw_" + n)]
    as2d = lambda a: a.reshape(-1, a.shape[-1])
    small_g = [as2d(gs[n].reshape(wts[n].shape)) for n in _SMALL]
    small_out = _adamw_small(small_g, *[[as2d(d[n]) for n in _SMALL] for d in (wts, mom1, mom2)])
    for k, n in enumerate(_SMALL):
        res[n] = [o[k].reshape(wts[n].shape) for o in (small_g,) + tuple(small_out)]

    grad_x = dh0.reshape(nb, t_rows, D_MODEL)[:, LEAD:, :]
    return (gs["loss"][0, 0], grad_x, *[res[n][0] for n in _WEIGHTS], *[res[n][1] for n in _WEIGHTS],
            *[res[n][2] for n in _WEIGHTS], *[res[n][3] for n in _WEIGHTS])
```

```python
import functools

import numpy as np
import jax
import jax.numpy as jnp
from jax import lax
from jax.experimental import pallas as pl
from jax.experimental.pallas import tpu as pltpu

F32 = jnp.float32
MXU_DTYPE = jnp.bfloat16

D_MODEL = 1024
D_POOL = 512
D_SSM = 1536
D_XBC = 2560
N_HEADS = 24
HEAD_DIM = 64
N_GROUPS = 4
GROUP_CH = D_SSM // N_GROUPS
D_STATE = 128
CHUNK = 128
N_META = 16
LEAD = CHUNK
PAD_ROWS = LEAD - N_META
ROW_TILE = 2 * CHUNK
D_MIX = D_POOL + D_SSM
D_FF = 4096
PROJ_W = 4736
OFF_Z = D_POOL
OFF_X = D_POOL + D_SSM
OFF_DT = OFF_X + D_XBC
D_IN_PROJ = OFF_DT + N_HEADS
POOL_WINDOWS = (2, 4, 8, 16)
HALO = 16
EPS = 1e-5
N_DEV = 8

ADAM_LR, ADAM_B1, ADAM_B2, ADAM_EPS, ADAM_WD, ADAM_STEP = 0.001, 0.9, 0.999, 1e-08, 0.01, 10

VMEM_LIMIT = 60 * 1024 * 1024


def _mx(a):
    return a.astype(MXU_DTYPE)


def _dot(a, b):
    return jnp.dot(_mx(a), _mx(b), preferred_element_type=F32)


def _dot_nt(a, b):
    return lax.dot_general(_mx(a), _mx(b), (((1,), (1,)), ((), ())), preferred_element_type=F32)


def _dot_tn(a, b):
    return lax.dot_general(_mx(a), _mx(b), (((0,), (0,)), ((), ())), preferred_element_type=F32)


def _split3(x):
    hi = x.astype(MXU_DTYPE)
    r = x - hi.astype(F32)
    mid = r.astype(MXU_DTYPE)
    lo = (r - mid.astype(F32)).astype(MXU_DTYPE)
    return hi, mid, lo


def _exact_l(c, x):
    hi, mid, lo = _split3(x)
    f = lambda p: jnp.dot(c, p, preferred_element_type=F32)
    return f(hi) + f(mid) + f(lo)


def _exact_r(x, c):
    hi, mid, lo = _split3(x)
    f = lambda p: jnp.dot(p, c, preferred_element_type=F32)
    return f(hi) + f(mid) + f(lo)


def _contract(x, c):
    hi = x.astype(MXU_DTYPE)
    lo = (x - hi.astype(F32)).astype(MXU_DTYPE)
    return jnp.dot(hi, c, preferred_element_type=F32) + jnp.dot(lo, c, preferred_element_type=F32)


def _sigmoid(x):
    return jax.nn.sigmoid(x)


def _softplus(x):
    return jnp.maximum(x, 0.0) + jnp.log1p(jnp.exp(-jnp.abs(x)))


def _silu_and_grad(x):
    s = _sigmoid(x)
    return x * s, s * (1.0 + x * (1.0 - s))


def _shift_down(ext, s):
    if s == 0:
        return ext[HALO:, :]
    return pltpu.roll(ext, s, 0)[HALO:, :]


def _shift_up(ext, s):
    if s == 0:
        return ext[:CHUNK, :]
    return pltpu.roll(ext, ext.shape[0] - s, 0)[:CHUNK, :]


def _by_pool_group(lane, a2, a4, a8, a16):
    return jnp.where(lane < 128, a2, jnp.where(lane < 256, a4, jnp.where(lane < 384, a8, a16)))


def _pool_inv_count(chunk_idx):
    row = lax.broadcasted_iota(jnp.int32, (CHUNK, D_POOL), 0)
    lane = lax.broadcasted_iota(jnp.int32, (CHUNK, D_POOL), 1)
    pos1 = jnp.maximum(chunk_idx * CHUNK + row - (PAD_ROWS - 1), 1)
    w = _by_pool_group(lane, 2, 4, 8, 16)
    return 1.0 / jnp.minimum(pos1, w).astype(F32), lane


def _pool_window_sums(u_ext, lane):
    s2 = u_ext + pltpu.roll(u_ext, 1, 0)
    s4 = s2 + pltpu.roll(s2, 2, 0)
    s8 = s4 + pltpu.roll(s4, 4, 0)
    s16 = s8 + pltpu.roll(s8, 8, 0)
    return _by_pool_group(lane, s2[HALO:], s4[HALO:], s8[HALO:], s16[HALO:])


def _pool_window_sums_ahead(q_ext, lane):
    n = q_ext.shape[0]
    r2 = q_ext + pltpu.roll(q_ext, n - 1, 0)
    r4 = r2 + pltpu.roll(r2, n - 2, 0)
    r8 = r4 + pltpu.roll(r4, n - 4, 0)
    r16 = r8 + pltpu.roll(r8, n - 8, 0)
    return _by_pool_group(lane, r2[:CHUNK], r4[:CHUNK], r8[:CHUNK], r16[:CHUNK])


def _conv_pre(ext, xbc, cw, cb):
    return (cb + cw[3:4, :] * xbc + cw[2:3, :] * _shift_down(ext, 1)
            + cw[1:2, :] * _shift_down(ext, 2) + cw[0:1, :] * _shift_down(ext, 3))


def _dt_and_cumsum(dtr, dt_bias, a_log, valid, tril):
    lane = lax.broadcasted_iota(jnp.int32, (CHUNK, 128), 1)
    head = lane < N_HEADS
    pre = dtr + dt_bias
    dt = jnp.where(valid & head, _softplus(pre), 0.0)
    a_row = jnp.where(head[0:1, :], -jnp.exp(a_log), 0.0)
    a_col = _exact_l(tril, dt * a_row)
    return dt, a_row, a_col, pre, head


def _decay(a_col, a_row_t, h, causal):
    seg = a_col[:, h:h + 1] - a_row_t[h:h + 1, :]
    return jnp.where(causal, jnp.exp(jnp.minimum(seg, 0.0)), 0.0)


def _ssd_chunk_fwd(xs, bm, cm, dt, a_col, s_prev, d_x, e_mat, et_mat):
    lane = lax.broadcasted_iota(jnp.int32, (CHUNK, 128), 1)
    rowi = lax.broadcasted_iota(jnp.int32, (CHUNK, CHUNK), 0)
    coli = lax.broadcasted_iota(jnp.int32, (CHUNK, CHUNK), 1)
    causal = rowi >= coli
    a_row_t = a_col.T
    ax = _exact_r(a_col, e_mat)
    dtx = _exact_r(dt, e_mat)
    xdt = xs * dtx
    ax_last = ax[CHUNK - 1:CHUNK, :]
    e_a = jnp.exp(ax)
    w_end = xdt * jnp.exp(ax_last - ax)
    cd_col = jnp.exp(_exact_l(et_mat, a_row_t)[:, CHUNK - 1:CHUNK])
    ys, s_new = [], []
    for g in range(N_GROUPS):
        gs = slice(g * GROUP_CH, (g + 1) * GROUP_CH)
        bg = bm[:, g * D_STATE:(g + 1) * D_STATE]
        cg = cm[:, g * D_STATE:(g + 1) * D_STATE]
        sg = s_prev[gs, :]
        cb = _dot_nt(cg, bg)
        y_off = _dot_nt(cg, sg) * e_a[:, gs]
        s_new.append(sg * cd_col[gs, :] + _dot_tn(w_end[:, gs], bg))
        for pr in range(3):
            c0 = g * GROUP_CH + pr * 128
            xdt_p = xdt[:, c0:c0 + 128]
            h0 = g * 6 + pr * 2
            y0 = _dot(cb * _decay(a_col, a_row_t, h0, causal), xdt_p)
            y1 = _dot(cb * _decay(a_col, a_row_t, h0 + 1, causal), xdt_p)
            ys.append(jnp.where(lane < HEAD_DIM, y0, y1) + y_off[:, pr * 128:(pr + 1) * 128])
    y = jnp.concatenate(ys, axis=1) + d_x * xs
    return y, jnp.concatenate(s_new, axis=0)


def _ssd_chunk_bwd(xs, bm, cm, dt, a_row, a_col, s_prev, ds_new, dy, d_x, e_mat, et_mat, triu):
    lane = lax.broadcasted_iota(jnp.int32, (CHUNK, 128), 1)
    sub = lax.broadcasted_iota(jnp.int32, (CHUNK, 128), 0)
    rowi = lax.broadcasted_iota(jnp.int32, (CHUNK, CHUNK), 0)
    coli = lax.broadcasted_iota(jnp.int32, (CHUNK, CHUNK), 1)
    causal = rowi >= coli
    a_row_t = a_col.T
    ax = _exact_r(a_col, e_mat)
    dtx = _exact_r(dt, e_mat)
    xdt = xs * dtx
    ax_last = ax[CHUNK - 1:CHUNK, :]
    e_a = jnp.exp(ax)
    dte = jnp.exp(ax_last - ax)
    w_end = xdt * dte
    cd_col = jnp.exp(_exact_l(et_mat, a_row_t)[:, CHUNK - 1:CHUNK])
    dye = dy * e_a

    dxdt, zc, ww_all, dbs, dcs, dsp, t1s = [], [], [], [], [], [], []
    zcol = jnp.zeros((CHUNK, 128), F32)
    zrow = jnp.zeros((128, CHUNK), F32)
    for g in range(N_GROUPS):
        gs = slice(g * GROUP_CH, (g + 1) * GROUP_CH)
        bg = bm[:, g * D_STATE:(g + 1) * D_STATE]
        cg = cm[:, g * D_STATE:(g + 1) * D_STATE]
        sg = s_prev[gs, :]
        dsg = ds_new[gs, :]
        cb = _dot_nt(cg, bg)
        cs = _dot_nt(cg, sg)
        dcg = _dot(dye[:, gs], sg)
        dsp.append(dsg * cd_col[gs, :] + _dot_tn(dye[:, gs], cg))
        dwg = _dot_nt(bg, dsg)
        dbg = _dot(w_end[:, gs], dsg)
        ww = dwg * w_end[:, gs]
        ww_all.append(ww)
        zc.append(dye[:, gs] * cs - ww)
        t1s.append(jnp.sum(dsg * sg, axis=1, keepdims=True) * cd_col[gs, :])
        dxdt_g = dwg * dte[:, gs]
        dcb = jnp.zeros((CHUNK, CHUNK), F32)
        pairs = []
        for pr in range(3):
            c0 = g * GROUP_CH + pr * 128
            xdt_p = xdt[:, c0:c0 + 128]
            dy_p = dy[:, c0:c0 + 128]
            acc = None
            for half in range(2):
                h = g * 6 + pr * 2 + half
                ld = _decay(a_col, a_row_t, h, causal)
                gm = cb * ld
                dym = jnp.where((lane < HEAD_DIM) if half == 0 else (lane >= HEAD_DIM), dy_p, 0.0)
                dg = _dot_nt(dym, xdt_p)
                dseg = dg * gm
                dcb = dcb + dg * ld
                t = _dot_tn(gm, dym)
                acc = t if acc is None else acc + t
                zcol = jnp.where(lane == h, jnp.sum(dseg, axis=1, keepdims=True), zcol)
                zrow = jnp.where(sub == h, jnp.sum(dseg, axis=0, keepdims=True), zrow)
            pairs.append(acc)
        dxdt.append(dxdt_g + jnp.concatenate(pairs, axis=1))
        dcs.append(dcg + _dot(dcb, bg))
        dbs.append(dbg + _dot_tn(dcb, cg))
    dxdt = jnp.concatenate(dxdt, axis=1)
    zc = jnp.concatenate(zc, axis=1)
    ww_all = jnp.concatenate(ww_all, axis=1)
    t1 = jnp.concatenate(t1s, axis=0)

    sub8 = lax.broadcasted_iota(jnp.int32, (8, D_SSM), 0)
    col_sums = jnp.where(sub8 == 0, jnp.sum(dy * xs, axis=0, keepdims=True),
                         jnp.where(sub8 == 1, jnp.sum(ww_all, axis=0, keepdims=True), 0.0))
    head_sums = _exact_r(col_sums, et_mat)
    dd = head_sums[0:1, :]
    q_row = head_sums[1:2, :] + jnp.sum(et_mat.astype(F32) * t1, axis=0, keepdims=True)
    da_col = _contract(zc, et_mat) + zcol - zrow.T
    da_col = da_col + jnp.where(sub == CHUNK - 1, q_row, 0.0)
    ddt = _contract(dxdt * xs, et_mat)
    dxs = dxdt * dtx + d_x * dy
    rc = _exact_l(triu, da_col)
    ddt = ddt + rc * a_row
    da = jnp.sum(rc * dt, axis=0, keepdims=True)
    return (dxs, jnp.concatenate(dbs, axis=1), jnp.concatenate(dcs, axis=1), ddt, da, dd,
            jnp.concatenate(dsp, axis=0))


def _ssd_constants():
    ch = np.arange(D_SSM) // HEAD_DIM
    e = (np.arange(128)[:, None] == ch[None, :]).astype(np.float32)
    tril = np.tril(np.ones((CHUNK, CHUNK), np.float32))
    as_mx = lambda a: jnp.asarray(a, MXU_DTYPE)
    return as_mx(e), as_mx(e.T), as_mx(tril), as_mx(tril.T)


def _full(shape):
    nd = len(shape)
    return pl.BlockSpec(shape, lambda *_: (0,) * nd)


def _params(*sem):
    return pltpu.CompilerParams(dimension_semantics=sem, vmem_limit_bytes=VMEM_LIMIT)


def _token_tiles(width, n_tok_tiles):
    return pl.BlockSpec((ROW_TILE, width), lambda i: (jnp.minimum(i, n_tok_tiles - 1), 0))


def _in_proj(x, lead, w1, win):
    nt = x.shape[0] // ROW_TILE
    m = x.shape[0] + ROW_TILE
    tm = ROW_TILE

    def body(x_ref, lead_ref, w1_ref, win_hbm, hn_ref, proj_ref, win_v, sem):
        i = pl.program_id(0)

        @pl.when(i == 0)
        def _():
            cp = pltpu.make_async_copy(win_hbm, win_v, sem)
            cp.start()
            cp.wait()

        x = jnp.where(i == nt, lead_ref[...], x_ref[...])
        r = lax.rsqrt(jnp.mean(x * x, axis=-1, keepdims=True) + EPS)
        hn = _mx(x * r * w1_ref[...])
        hn_ref[...] = hn
        for j in range(0, PROJ_W, 512):
            w = min(512, PROJ_W - j)
            proj_ref[:, j:j + w] = jnp.dot(hn, win_v[:, j:j + w], preferred_element_type=F32)

    return pl.pallas_call(
        body, grid=(m // tm,), name="in_proj",
        in_specs=[_token_tiles(D_MODEL, nt), _full((ROW_TILE, D_MODEL)), _full((1, D_MODEL)),
                  pl.BlockSpec(memory_space=pl.ANY)],
        out_specs=[pl.BlockSpec((tm, D_MODEL), lambda i: (i, 0)), pl.BlockSpec((tm, PROJ_W), lambda i: (i, 0))],
        out_shape=[jax.ShapeDtypeStruct((m, D_MODEL), MXU_DTYPE), jax.ShapeDtypeStruct((m, PROJ_W), F32)],
        scratch_shapes=[pltpu.VMEM((D_MODEL, PROJ_W), MXU_DTYPE), pltpu.SemaphoreType.DMA],
        compiler_params=_params("arbitrary"),
    )(x, lead, w1, win)


def _ffn_fwd_bwd(x, lead, y, tgt, wout, w2n, wff1, wff2, wfn):
    nt = x.shape[0] // ROW_TILE
    m = x.shape[0] + ROW_TILE
    tm = ROW_TILE
    nj = D_FF // 1024

    def body(x_ref, lead_ref, y_ref, tgt_ref, w2n_ref, wfn_ref, wout_hbm, wff1_hbm, wff2_hbm,
             loss_ref, gwf_ref, gw2_ref, ff_ref, da_ref, hn2_ref, dh1_ref, dh2_ref, dy_ref,
             wout_v, wff1_v, wff2_v, a_s, sems):
        i = pl.program_id(0)
        hp = jnp.where(i == nt, lead_ref[...], x_ref[...])

        @pl.when(i == 0)
        def _():
            cps = [pltpu.make_async_copy(s, d, sems.at[k])
                   for k, (s, d) in enumerate(((wout_hbm, wout_v), (wff1_hbm, wff1_v), (wff2_hbm, wff2_v)))]
            for cp in cps:
                cp.start()
            for cp in cps:
                cp.wait()
            loss_ref[...] = jnp.zeros_like(loss_ref)
            gwf_ref[...] = jnp.zeros_like(gwf_ref)
            gw2_ref[...] = jnp.zeros_like(gw2_ref)

        h1 = hp + jnp.dot(y_ref[...], wout_v[...], preferred_element_type=F32)
        r2 = lax.rsqrt(jnp.mean(h1 * h1, axis=-1, keepdims=True) + EPS)
        n2 = h1 * r2
        w2n_row = w2n_ref[...]
        hn2 = _mx(n2 * w2n_row)
        hn2_ref[...] = hn2
        h2 = h1
        for j in range(nj):
            js = slice(j * 1024, (j + 1) * 1024)
            a = jnp.dot(hn2, wff1_v[:, js], preferred_element_type=F32)
            a_s[:, js] = a
            ra = jnp.maximum(a, 0.0)
            ff = _mx(ra * ra)
            ff_ref[:, js] = ff
            h2 = h2 + jnp.dot(ff, wff2_v[js, :], preferred_element_type=F32)

        r3 = lax.rsqrt(jnp.mean(h2 * h2, axis=-1, keepdims=True) + EPS)
        n3 = h2 * r3
        wf_row = wfn_ref[...]
        err = n3 * wf_row - tgt_ref[...]
        tokf = (i < nt).astype(F32)
        loss_ref[...] += 0.5 * jnp.sum(jnp.mean(err * err, axis=-1, keepdims=True) * tokf)
        dout = err * (tokf / D_MODEL)
        gwf_ref[...] += jnp.sum(dout * n3, axis=0, keepdims=True)
        dn3 = dout * wf_row
        dh2 = r3 * (dn3 - n3 * jnp.mean(dn3 * n3, axis=-1, keepdims=True))
        dh2m = _mx(dh2)
        dh2_ref[...] = dh2m

        dhn2 = jnp.zeros((tm, D_MODEL), F32)
        for j in range(nj):
            js = slice(j * 1024, (j + 1) * 1024)
            dff = lax.dot_general(dh2m, wff2_v[js, :], (((1,), (1,)), ((), ())), preferred_element_type=F32)
            da = _mx(dff * (2.0 * jnp.maximum(a_s[:, js], 0.0)))
            da_ref[:, js] = da
            dhn2 = dhn2 + lax.dot_general(da, wff1_v[:, js], (((1,), (1,)), ((), ())), preferred_element_type=F32)
        gw2_ref[...] += jnp.sum(dhn2 * n2, axis=0, keepdims=True)
        dn2 = dhn2 * w2n_row
        dh1 = dh2 + r2 * (dn2 - n2 * jnp.mean(dn2 * n2, axis=-1, keepdims=True))
        dh1_ref[...] = dh1
        dy_ref[...] = lax.dot_general(_mx(dh1), wout_v[...], (((1,), (1,)), ((), ())), preferred_element_type=F32)

    rows = lambda w: pl.BlockSpec((tm, w), lambda i: (i, 0))
    hbm = pl.BlockSpec(memory_space=pl.ANY)
    return pl.pallas_call(
        body, grid=(m // tm,), name="ffn_fwd_bwd",
        in_specs=[_token_tiles(D_MODEL, nt), _full((ROW_TILE, D_MODEL)), rows(D_MIX), _token_tiles(D_MODEL, nt),
                  _full((1, D_MODEL)), _full((1, D_MODEL)), hbm, hbm, hbm],
        out_specs=[_full((1, 128)), _full((1, D_MODEL)), _full((1, D_MODEL)), rows(D_FF), rows(D_FF), rows(D_MODEL),
                   rows(D_MODEL), rows(D_MODEL), rows(D_MIX)],
        out_shape=[jax.ShapeDtypeStruct((1, 128), F32), jax.ShapeDtypeStruct((1, D_MODEL), F32),
                   jax.ShapeDtypeStruct((1, D_MODEL), F32), jax.ShapeDtypeStruct((m, D_FF), MXU_DTYPE),
                   jax.ShapeDtypeStruct((m, D_FF), MXU_DTYPE), jax.ShapeDtypeStruct((m, D_MODEL), MXU_DTYPE),
                   jax.ShapeDtypeStruct((m, D_MODEL), F32), jax.ShapeDtypeStruct((m, D_MODEL), MXU_DTYPE),
                   jax.ShapeDtypeStruct((m, D_MIX), F32)],
        scratch_shapes=[pltpu.VMEM((D_MIX, D_MODEL), MXU_DTYPE), pltpu.VMEM((D_MODEL, D_FF), MXU_DTYPE),
                        pltpu.VMEM((D_FF, D_MODEL), MXU_DTYPE), pltpu.VMEM((tm, D_FF), F32),
                        pltpu.SemaphoreType.DMA((3,))],
        compiler_params=_params("arbitrary"),
    )(x, lead, y, tgt, w2n, wfn, wout, wff1, wff2)


def _in_proj_bwd(dproj, x, lead, dh1, w1, win, chip_sums):
    nt = x.shape[0] // ROW_TILE
    m = x.shape[0] + ROW_TILE
    tm = ROW_TILE
    ns = len(chip_sums)

    def body(dp_ref, x_ref, lead_ref, dh1_ref, w1_ref, win_hbm, *rest):
        cs_refs, (gx_ref, gw1_ref, gmeta_ref) = rest[:ns], rest[ns:ns + 3]
        part_refs, (win_v, sem) = rest[ns + 3:2 * ns + 3], rest[2 * ns + 3:2 * ns + 5]
        exchange = _ChipExchange(cs_refs, part_refs, [False] * ns, *rest[2 * ns + 5:])
        i = pl.program_id(0)

        @pl.when(i == 0)
        def _():
            exchange.start()
            cp = pltpu.make_async_copy(win_hbm, win_v, sem)
            cp.start()
            cp.wait()
            gw1_ref[...] = jnp.zeros_like(gw1_ref)
            gmeta_ref[...] = jnp.zeros_like(gmeta_ref)

        dhn = lax.dot_general(dp_ref[...], win_v[...], (((1,), (1,)), ((), ())), preferred_element_type=F32)
        x = jnp.where(i == nt, lead_ref[...], x_ref[...])
        r = lax.rsqrt(jnp.mean(x * x, axis=-1, keepdims=True) + EPS)
        n = x * r
        gw1_ref[...] += jnp.sum(dhn * n, axis=0, keepdims=True)
        dn = dhn * w1_ref[...]
        dh0 = dh1_ref[...] + r * (dn - n * jnp.mean(dn * n, axis=-1, keepdims=True))

        @pl.when(i < nt)
        def _():
            gx_ref[...] = dh0

        @pl.when(i == nt)
        def _():
            gmeta_ref[...] = dh0[PAD_ROWS:LEAD, :] + dh0[LEAD + PAD_ROWS:2 * LEAD, :]
            exchange.finish()

    rows = lambda w: pl.BlockSpec((tm, w), lambda i: (i, 0))
    hbm = pl.BlockSpec(memory_space=pl.ANY)
    outs = pl.pallas_call(
        body, grid=(m // tm,), name="in_proj_bwd",
        in_specs=[rows(PROJ_W), _token_tiles(D_MODEL, nt), _full((ROW_TILE, D_MODEL)), rows(D_MODEL),
                  _full((1, D_MODEL)), hbm] + [hbm] * ns,
        out_specs=[_token_tiles(D_MODEL, nt), _full((1, D_MODEL)), _full((N_META, D_MODEL))] + [hbm] * ns,
        out_shape=[jax.ShapeDtypeStruct(x.shape, F32), jax.ShapeDtypeStruct((1, D_MODEL), F32),
                   jax.ShapeDtypeStruct((N_META, D_MODEL), F32)]
        + _ChipExchange.out_shapes(chip_sums, [False] * ns),
        scratch_shapes=[pltpu.VMEM((D_MODEL, PROJ_W), MXU_DTYPE), pltpu.SemaphoreType.DMA] + _ChipExchange.scratch(ns),
        compiler_params=_params("arbitrary"),
    )(dproj, x, lead, dh1, w1, win, *chip_sums)
    return outs[0], outs[1], outs[2], outs[3:]


def _tn_matmul(a, b, name, tka, tkm=768, tn=512):
    m, ka = a.shape
    nb = b.shape[1]
    n_steps = m // tkm

    def body(a_ref, b_ref, o_ref):
        @pl.when(pl.program_id(1) == 0)
        def _():
            o_ref[...] = jnp.zeros_like(o_ref)

        at = _mx(a_ref[...])
        for j in range(0, nb, tn):
            w = min(tn, nb - j)
            o_ref[:, j:j + w] += lax.dot_general(at, _mx(b_ref[:, j:j + w]), (((0,), (0,)), ((), ())),
                                                 preferred_element_type=F32)

    return pl.pallas_call(
        body, grid=(ka // tka, n_steps), name=name,
        in_specs=[pl.BlockSpec((tkm, tka), lambda i, k: (k, i)), pl.BlockSpec((tkm, nb), lambda i, k: (k, 0))],
        out_specs=pl.BlockSpec((tka, nb), lambda i, k: (i, 0)),
        out_shape=jax.ShapeDtypeStruct((ka, nb), F32),
        compiler_params=_params("arbitrary", "arbitrary"),
    )(a, b)


def _chunk_block(b, c, nb, nc):
    return jnp.where(c == 0, nb * (nc - 1) + b, b * (nc - 1) + c - 1)

def _mixer_fwd(proj, cw, cb, dt_bias, a_log, d_x, nw, pool_w, pool_scale, nb, shards, by_cols):
    m = proj.shape[0]
    nc = m // nb // CHUNK
    n_steps = nb * nc
    ns = len(shards)
    e_mat, et_mat, tril, _ = _ssd_constants()

    def body(p_ref, cw_ref, cb_ref, dtb_ref, alog_ref, dx_ref, nw_ref, pw_ref, ps_ref, e_ref, et_ref, tril_ref, *rest):
        shard_refs, (y_ref, ypre_ref, pre_ref, st_ref) = rest[:ns], rest[ns:ns + 4]
        gathered_refs, (xtail, utail, state) = rest[ns + 4:2 * ns + 4], rest[2 * ns + 4:2 * ns + 7]
        gather = _Gather(shard_refs, gathered_refs, by_cols, *rest[2 * ns + 7:])
        c = pl.program_id(1)
        step = pl.program_id(0) * nc + c

        @pl.when(step == 0)
        def _():
            gather.start()

        @pl.when(step == n_steps // 2)
        def _():
            gather.forward()

        @pl.when(c == 0)
        def _():
            xtail[...] = jnp.zeros_like(xtail)
            utail[...] = jnp.zeros_like(utail)
            state[...] = jnp.zeros_like(state)

        valid = (c > 0) | (lax.broadcasted_iota(jnp.int32, (CHUNK, 1), 0) >= PAD_ROWS)

        u = p_ref[:, 0:D_POOL]
        inv_cnt, lane = _pool_inv_count(c)
        win = _pool_window_sums(jnp.concatenate([utail[...], u], axis=0), lane)
        utail[...] = u[CHUNK - HALO:, :]
        pooled = win * inv_cnt - u
        mixed = jnp.concatenate(
            [_dot(pooled[:, g * 128:(g + 1) * 128], pw_ref[g]) for g in range(len(POOL_WINDOWS))], axis=1)
        y_ref[:, 0:D_POOL] = _mx(mixed * ps_ref[...])

        xbc = p_ref[:, OFF_X:OFF_X + D_XBC]
        pre = _conv_pre(jnp.concatenate([xtail[...], xbc], axis=0), xbc, cw_ref[...], cb_ref[...])
        xtail[...] = xbc[CHUNK - HALO:, :]
        pre_ref[...] = pre
        xc = pre * _sigmoid(pre)
        dt, _, a_col, _, _ = _dt_and_cumsum(p_ref[:, OFF_DT:OFF_DT + 128], dtb_ref[...], alog_ref[...], valid,
                                            tril_ref[...])
        s_prev = state[...]
        st_ref[0] = s_prev
        yp, s_new = _ssd_chunk_fwd(xc[:, 0:D_SSM], xc[:, D_SSM:D_SSM + 512], xc[:, D_SSM + 512:], dt, a_col, s_prev,
                                   dx_ref[...], e_ref[...], et_ref[...])
        state[...] = s_new
        ypre_ref[...] = yp
        z = p_ref[:, OFF_Z:OFF_Z + D_SSM]
        yz = yp * (z * _sigmoid(z))
        outs = []
        for g in range(N_GROUPS):
            gs = slice(g * GROUP_CH, (g + 1) * GROUP_CH)
            r = lax.rsqrt(jnp.mean(yz[:, gs] * yz[:, gs], axis=-1, keepdims=True) + EPS)
            outs.append(yz[:, gs] * r)
        y_ref[:, D_POOL:] = _mx(jnp.concatenate(outs, axis=1) * nw_ref[...])

        @pl.when(step == n_steps - 1)
        def _():
            gather.finish()

    blk = lambda w: pl.BlockSpec((CHUNK, w), lambda b, c: (_chunk_block(b, c, nb, nc), 0))
    hbm = pl.BlockSpec(memory_space=pl.ANY)
    outs = pl.pallas_call(
        body, grid=(nb, nc), name="mixer_fwd",
        in_specs=[blk(PROJ_W), _full((4, D_XBC)), _full((1, D_XBC)), _full((1, 128)), _full((1, 128)),
                  _full((1, D_SSM)), _full((1, D_SSM)), _full((4, 128, 128)), _full((1, D_POOL)),
                  _full((128, D_SSM)), _full((D_SSM, 128)), _full((CHUNK, CHUNK))] + [hbm] * ns,
        out_specs=[blk(D_MIX), blk(D_SSM), blk(D_XBC),
                   pl.BlockSpec((1, D_SSM, D_STATE), lambda b, c: (b * nc + c, 0, 0))] + [hbm] * ns,
        out_shape=[jax.ShapeDtypeStruct((m, D_MIX), MXU_DTYPE), jax.ShapeDtypeStruct((m, D_SSM), F32),
                   jax.ShapeDtypeStruct((m, D_XBC), F32), jax.ShapeDtypeStruct((m // CHUNK, D_SSM, D_STATE), F32)]
        + _Gather.out_shapes(shards, by_cols),
        scratch_shapes=[pltpu.VMEM((HALO, D_XBC), F32), pltpu.VMEM((HALO, D_POOL), F32),
                        pltpu.VMEM((D_SSM, D_STATE), F32)] + _Gather.scratch(ns),
        compiler_params=_params("arbitrary", "arbitrary"),
    )(proj, cw, cb, dt_bias, a_log, d_x, nw, pool_w, pool_scale, e_mat, et_mat, tril, *shards)
    return outs[0], outs[1], outs[2], outs[3], outs[4:]


def _mixer_bwd(proj, dy, ypre, conv_pre, states, cw, dt_bias, a_log, d_x, nw, pool_w, pool_scale, nb, chip_sums):
    m = proj.shape[0]
    nc = m // nb // CHUNK
    e_mat, et_mat, tril, triu = _ssd_constants()
    hb = CHUNK // HALO
    ns = len(chip_sums)

    def body(p_ref, halo_ref, dy_ref, ypre_ref, pre_ref, st_ref, cw_ref, dtb_ref, alog_ref, dx_ref, nw_ref, pw_ref,
             ps_ref, e_ref, et_ref, tril_ref, triu_ref, *rest):
        cs_refs = rest[:ns]
        dp_ref, gcw_ref, gcb_ref, gdtb_ref, galog_ref, gd_ref, gnw_ref, gpw_ref, gps_ref = rest[ns:ns + 9]
        part_refs, (ds_carry, dpre_next, dq_next) = rest[ns + 9:2 * ns + 9], rest[2 * ns + 9:2 * ns + 12]
        exchange = _ChipExchange(cs_refs, part_refs, [False] * ns, *rest[2 * ns + 12:])
        b = pl.program_id(0)
        cc = pl.program_id(1)
        c = nc - 1 - cc

        @pl.when((b == 0) & (cc == 0))
        def _():
            exchange.start()
            for r in (gcw_ref, gcb_ref, gdtb_ref, galog_ref, gd_ref, gnw_ref, gpw_ref, gps_ref):
                r[...] = jnp.zeros_like(r)

        @pl.when(cc == 0)
        def _():
            ds_carry[...] = jnp.zeros_like(ds_carry)
            dpre_next[...] = jnp.zeros_like(dpre_next)
            dq_next[...] = jnp.zeros_like(dq_next)

        valid = (c > 0) | (lax.broadcasted_iota(jnp.int32, (CHUNK, 1), 0) >= PAD_ROWS)
        first = c > 0

        u = p_ref[:, 0:D_POOL]
        u_halo = jnp.where(first, halo_ref[...], 0.0)
        inv_cnt, lane = _pool_inv_count(c)
        pooled = _pool_window_sums(jnp.concatenate([u_halo, u], axis=0), lane) * inv_cnt - u
        dyp = dy_ref[:, 0:D_POOL]
        ps = ps_ref[...]
        dmixed = dyp * ps
        mixed, dpooled = [], []
        for g in range(len(POOL_WINDOWS)):
            gsl = slice(g * 128, (g + 1) * 128)
            pw = pw_ref[g]
            mixed.append(_dot(pooled[:, gsl], pw))
            dpooled.append(_dot_nt(dmixed[:, gsl], pw))
            gpw_ref[g] += _dot_tn(pooled[:, gsl], dmixed[:, gsl])
        gps_ref[...] += jnp.sum(dyp * jnp.concatenate(mixed, axis=1), axis=0, keepdims=True)
        dpooled = jnp.concatenate(dpooled, axis=1)
        dq = dpooled * inv_cnt
        du = _pool_window_sums_ahead(jnp.concatenate([dq, dq_next[...]], axis=0), lane) - dpooled
        dq_next[...] = dq[0:HALO, :]
        dp_ref[:, 0:D_POOL] = _mx(du)

        yp = ypre_ref[...]
        z = p_ref[:, OFF_Z:OFF_Z + D_SSM]
        sz, dsz = _silu_and_grad(z)
        yz = yp * sz
        do = dy_ref[:, D_POOL:]
        nw_row = nw_ref[...]
        dyz = []
        gnw = []
        for g in range(N_GROUPS):
            gs = slice(g * GROUP_CH, (g + 1) * GROUP_CH)
            r = lax.rsqrt(jnp.mean(yz[:, gs] * yz[:, gs], axis=-1, keepdims=True) + EPS)
            n = yz[:, gs] * r
            gnw.append(jnp.sum(do[:, gs] * n, axis=0, keepdims=True))
            dn = do[:, gs] * nw_row[:, gs]
            dyz.append(r * (dn - n * jnp.mean(dn * n, axis=-1, keepdims=True)))
        gnw_ref[...] += jnp.concatenate(gnw, axis=1)
        dyz = jnp.concatenate(dyz, axis=1)
        dp_ref[:, OFF_Z:OFF_Z + D_SSM] = _mx(dyz * yp * dsz)
        dyp_ssm = dyz * sz

        xc, dsilu = _silu_and_grad(pre_ref[...])
        dtr = p_ref[:, OFF_DT:OFF_DT + 128]
        dt, a_row, a_col, dt_pre, head = _dt_and_cumsum(dtr, dtb_ref[...], alog_ref[...], valid, tril_ref[...])
        dxs, dbm, dcm, ddt, da, dd, ds_prev = _ssd_chunk_bwd(
            xc[:, 0:D_SSM], xc[:, D_SSM:D_SSM + 512], xc[:, D_SSM + 512:], dt, a_row, a_col, st_ref[0],
            ds_carry[...], dyp_ssm, dx_ref[...], e_ref[...], et_ref[...], triu_ref[...])
        ds_carry[...] = ds_prev
        gd_ref[...] += dd
        galog_ref[...] += da * a_row
        ddtr = jnp.where(valid & head, ddt * _sigmoid(dt_pre), 0.0)
        gdtb_ref[...] += jnp.sum(ddtr, axis=0, keepdims=True)
        dp_ref[:, OFF_DT:OFF_DT + 128] = _mx(ddtr)

        dpre = jnp.concatenate([dxs, dbm, dcm], axis=1) * dsilu
        gcb_ref[...] += jnp.sum(dpre, axis=0, keepdims=True)
        dext = jnp.concatenate([dpre, dpre_next[...]], axis=0)
        dpre_next[...] = dpre[0:HALO, :]
        ups = [_shift_up(dext, 3 - k) for k in range(4)]
        xbc = p_ref[:, OFF_X:OFF_X + D_XBC]
        gcw_ref[...] += jnp.concatenate([jnp.sum(xbc * ups[k], axis=0, keepdims=True) for k in range(4)], axis=0)
        cw = cw_ref[...]
        dp_ref[:, OFF_X:OFF_X + D_XBC] = _mx(cw[3:4, :] * ups[3] + cw[2:3, :] * ups[2]
                                             + cw[1:2, :] * ups[1] + cw[0:1, :] * ups[0])

        @pl.when((b == nb - 1) & (cc == nc - 1))
        def _():
            exchange.finish()

    blk = lambda w: pl.BlockSpec((CHUNK, w), lambda b, cc: (_chunk_block(b, nc - 1 - cc, nb, nc), 0))
    halo = pl.BlockSpec((HALO, D_POOL),
                        lambda b, cc: (_chunk_block(b, jnp.maximum(nc - 2 - cc, 0), nb, nc) * hb + hb - 1, 0))
    hbm = pl.BlockSpec(memory_space=pl.ANY)
    outs = pl.pallas_call(
        body, grid=(nb, nc), name="mixer_bwd",
        in_specs=[blk(PROJ_W), halo, blk(D_MIX), blk(D_SSM), blk(D_XBC),
                  pl.BlockSpec((1, D_SSM, D_STATE), lambda b, cc: (b * nc + nc - 1 - cc, 0, 0)),
                  _full((4, D_XBC)), _full((1, 128)), _full((1, 128)), _full((1, D_SSM)),
                  _full((1, D_SSM)), _full((4, 128, 128)), _full((1, D_POOL)),
                  _full((128, D_SSM)), _full((D_SSM, 128)), _full((CHUNK, CHUNK)), _full((CHUNK, CHUNK))] + [hbm] * ns,
        out_specs=[blk(PROJ_W), _full((4, D_XBC)), _full((1, D_XBC)), _full((1, 128)), _full((1, 128)), _full((1, 128)),
                   _full((1, D_SSM)), _full((4, 128, 128)), _full((1, D_POOL))] + [hbm] * ns,
        out_shape=[jax.ShapeDtypeStruct((m, PROJ_W), MXU_DTYPE), jax.ShapeDtypeStruct((4, D_XBC), F32),
                   jax.ShapeDtypeStruct((1, D_XBC), F32), jax.ShapeDtypeStruct((1, 128), F32),
                   jax.ShapeDtypeStruct((1, 128), F32), jax.ShapeDtypeStruct((1, 128), F32),
                   jax.ShapeDtypeStruct((1, D_SSM), F32), jax.ShapeDtypeStruct((4, 128, 128), F32),
                   jax.ShapeDtypeStruct((1, D_POOL), F32)] + _ChipExchange.out_shapes(chip_sums, [False] * ns),
        scratch_shapes=[pltpu.VMEM((D_SSM, D_STATE), F32), pltpu.VMEM((HALO, D_XBC), F32),
                        pltpu.VMEM((HALO, D_POOL), F32)] + _ChipExchange.scratch(ns),
        compiler_params=_params("arbitrary", "arbitrary"),
    )(proj, proj, dy, ypre, conv_pre, states, cw, dt_bias, a_log, d_x, nw, pool_w, pool_scale, e_mat, et_mat, tril, triu,
      *chip_sums)
    return outs[:9], outs[9:]


MESH_IDS = pl.DeviceIdType.MESH
_HBM = pl.BlockSpec(memory_space=pltpu.HBM)


def _coords():
    return lax.axis_index("x"), lax.axis_index("y"), lax.axis_index("c")


def _other_chips(x, y):
    return [(1 - x, y), (x, 1 - y), (1 - x, 1 - y)]


class _Gather:
    def __init__(self, ins, outs, by_cols, send_sems, recv_sems, local_sems):
        self.ins, self.outs, self.by_cols, self.n = ins, outs, by_cols, len(ins)
        self.send_sems, self.recv_sems, self.local_sems = send_sems, recv_sems, local_sems
        self.x, self.y, self.c = _coords()
        self.me, self.sibling = (self.x, self.y, self.c), (self.x, self.y, 1 - self.c)
        self.chips = _other_chips(self.x, self.y)

    @staticmethod
    def scratch(n):
        return [pltpu.SemaphoreType.DMA((7 * n,)), pltpu.SemaphoreType.DMA((7 * n,)), pltpu.SemaphoreType.DMA((n,))]

    @staticmethod
    def out_shapes(shards, by_cols):
        return [jax.ShapeDtypeStruct((s.shape[0], N_DEV * s.shape[1]) if cols else (N_DEV,) + s.shape, s.dtype)
                for s, cols in zip(shards, by_cols)]

    def _block(self, t, device):
        idx = 4 * device[0] + 2 * device[1] + device[2]
        if not self.by_cols[t]:
            return self.outs[t].at[idx]
        w = self.ins[t].shape[1]
        return self.outs[t].at[:, pl.ds(pl.multiple_of(idx * w, w), w)]

    def _copy(self, t, k, block, to, own=False):
        dst = self._block(t, block)
        return pltpu.make_async_remote_copy(
            src_ref=self.ins[t] if own else dst, dst_ref=dst, send_sem=self.send_sems.at[t * 7 + k],
            recv_sem=self.recv_sems.at[t * 7 + k], device_id=to, device_id_type=MESH_IDS)

    def _mine(self):
        return [pltpu.make_async_copy(self.ins[t], self._block(t, self.me), self.local_sems.at[t])
                for t in range(self.n)]

    def _first(self):
        cps = []
        for t in range(self.n):
            cps.append(self._copy(t, 0, self.me, self.sibling, own=True))
            cps += [self._copy(t, 1 + j, self.me, (*chip, self.c), own=True) for j, chip in enumerate(self.chips)]
        return cps

    def _passed(self):
        return [self._copy(t, 4 + j, (*chip, self.c), self.sibling)
                for j, chip in enumerate(self.chips) for t in range(self.n)]

    def start(self):
        for cp in self._mine() + self._first():
            cp.start()

    def forward(self):
        for j, chip in enumerate(self.chips):
            for t in range(self.n):
                self._copy(t, 1 + j, (*chip, self.c), self.me).wait_recv()
                self._copy(t, 4 + j, (*chip, self.c), self.sibling).start()

    def finish(self):
        for t in range(self.n):
            self._copy(t, 0, self.sibling, self.me).wait_recv()
            for j, chip in enumerate(self.chips):
                self._copy(t, 4 + j, (*chip, 1 - self.c), self.me).wait_recv()
        for cp in self._first() + self._passed():
            cp.wait_send()
        for cp in self._mine():
            cp.wait()


def _weight_gather(shards):
    n = len(shards)

    def body(*refs):
        g = _Gather(refs[:n], refs[n:2 * n], [False] * n, *refs[2 * n:])
        g.start()
        g.forward()
        g.finish()

    return pl.pallas_call(
        body, name="weight_gather",
        in_specs=[_HBM] * n, out_specs=[_HBM] * n,
        out_shape=_Gather.out_shapes(shards, [False] * n),
        scratch_shapes=_Gather.scratch(n),
    )(*shards)


def _owner_blocks(g, by_cols):
    return (g.shape[0], g.shape[1] // N_DEV) if by_cols else g.shape[2:]


def _grad_exchange_d2d(gs, by_cols, name):
    n = len(gs)

    def body(*refs):
        ins, got = refs[:n], refs[n:2 * n]
        send_sems, recv_sems = refs[2 * n:]
        x, y, c = _coords()

        def src(t, k):
            if not by_cols[t]:
                return ins[t].at[k, 1 - c]
            w = ins[t].shape[1] // N_DEV
            return ins[t].at[:, pl.ds(pl.multiple_of((2 * k + 1 - c) * w, w), w)]

        remote = [pltpu.make_async_remote_copy(
            src_ref=src(t, k), dst_ref=got[t].at[k], send_sem=send_sems.at[t * 4 + k],
            recv_sem=recv_sems.at[t * 4 + k], device_id=(x, y, 1 - c), device_id_type=MESH_IDS)
            for t in range(n) for k in range(4)]
        for cp in remote:
            cp.start()
        for cp in remote:
            cp.wait_recv()
        for cp in remote:
            cp.wait_send()

    return pl.pallas_call(
        body, name=name,
        in_specs=[_HBM] * n, out_specs=[_HBM] * n,
        out_shape=[jax.ShapeDtypeStruct((4,) + _owner_blocks(g, cols), g.dtype) for g, cols in zip(gs, by_cols)],
        scratch_shapes=[pltpu.SemaphoreType.DMA((4 * n,)), pltpu.SemaphoreType.DMA((4 * n,))],
    )(*gs)


def _small_allreduce(pack):
    rows = pack.shape[0]

    def body(p_ref, o_ref, sib_ref, parts_ref, send_sems, recv_sems):
        x, y, c = _coords()
        my_chip = 2 * x + y
        swap = pltpu.make_async_remote_copy(src_ref=p_ref, dst_ref=sib_ref, send_sem=send_sems.at[0],
                                            recv_sem=recv_sems.at[0], device_id=(x, y, 1 - c), device_id_type=MESH_IDS)
        swap.start()
        swap.wait_recv()
        parts_ref[my_chip] = p_ref[...] + sib_ref[...]
        remote = [pltpu.make_async_remote_copy(
            src_ref=parts_ref.at[my_chip], dst_ref=parts_ref.at[my_chip], send_sem=send_sems.at[1 + j],
            recv_sem=recv_sems.at[1 + j], device_id=(cx, cy, c), device_id_type=MESH_IDS)
            for j, (cx, cy) in enumerate(_other_chips(x, y))]
        for cp in remote:
            cp.start()
        for j, (cx, cy) in enumerate(_other_chips(x, y)):
            slot = parts_ref.at[2 * cx + cy]
            pltpu.make_async_remote_copy(src_ref=slot, dst_ref=slot, send_sem=send_sems.at[1 + j],
                                         recv_sem=recv_sems.at[1 + j], device_id=(cx, cy, c),
                                         device_id_type=MESH_IDS).wait_recv()
        o_ref[...] = ((parts_ref[0] + parts_ref[1]) + parts_ref[2]) + parts_ref[3]
        swap.wait_send()
        for cp in remote:
            cp.wait_send()

    vmem = pl.BlockSpec(memory_space=pltpu.VMEM)
    return pl.pallas_call(
        body, name="small_allreduce", in_specs=[vmem], out_specs=vmem,
        out_shape=jax.ShapeDtypeStruct((rows, 128), F32),
        scratch_shapes=[pltpu.VMEM((rows, 128), F32), pltpu.VMEM((4, rows, 128), F32),
                        pltpu.SemaphoreType.DMA((4,)), pltpu.SemaphoreType.DMA((4,))],
    )(pack)


class _ChipExchange:
    def __init__(self, ins, outs, whole, send_sems, recv_sems, local_sems):
        self.ins, self.outs, self.whole, self.n = ins, outs, whole, len(ins)
        self.send_sems, self.recv_sems, self.local_sems = send_sems, recv_sems, local_sems
        self.x, self.y, self.c = _coords()
        self.my_chip = 2 * self.x + self.y
        self.chips = _other_chips(self.x, self.y)

    @staticmethod
    def scratch(n):
        return [pltpu.SemaphoreType.DMA((3 * n,)), pltpu.SemaphoreType.DMA((3 * n,)), pltpu.SemaphoreType.DMA((n,))]

    def _src(self, t, k):
        return self.ins[t] if self.whole[t] else self.ins[t].at[k]

    def _local(self):
        return [pltpu.make_async_copy(self._src(t, self.my_chip), self.outs[t].at[self.my_chip], self.local_sems.at[t])
                for t in range(self.n)]

    def _remote(self):
        return [pltpu.make_async_remote_copy(
            src_ref=self._src(t, 2 * cx + cy), dst_ref=self.outs[t].at[self.my_chip],
            send_sem=self.send_sems.at[t * 3 + j], recv_sem=self.recv_sems.at[t * 3 + j],
            device_id=(cx, cy, self.c), device_id_type=MESH_IDS)
            for t in range(self.n) for j, (cx, cy) in enumerate(self.chips)]

    def start(self):
        for cp in self._remote() + self._local():
            cp.start()

    def finish(self):
        for t in range(self.n):
            for j, (cx, cy) in enumerate(self.chips):
                slot = self.outs[t].at[2 * cx + cy]
                pltpu.make_async_remote_copy(
                    src_ref=slot, dst_ref=slot, send_sem=self.send_sems.at[t * 3 + j],
                    recv_sem=self.recv_sems.at[t * 3 + j], device_id=(cx, cy, self.c),
                    device_id_type=MESH_IDS).wait_recv()
        for cp in self._remote():
            cp.wait_send()
        for cp in self._local():
            cp.wait()

    @staticmethod
    def out_shapes(arrs, whole):
        return [jax.ShapeDtypeStruct(((4,) + a.shape) if w else a.shape, a.dtype) for a, w in zip(arrs, whole)]


def _row_tile(rows, cols, n_arrays):
    budget = 24 * 1024 * 1024
    padded = -(-cols // 128) * 128
    step = 16 if rows % 16 == 0 else 8
    tr = max(step, budget // (n_arrays * 2 * 4 * padded) // step * step)
    while rows % tr:
        tr -= step
    return tr


def _chip_sum(g, by_cols, got, core, name):
    rows, cols = _owner_blocks(g, by_cols)
    tr = _row_tile(rows, cols, 3)

    def body(c_ref, a_ref, b_ref, o_ref):
        o_ref[...] = (a_ref[...] + b_ref[...]).astype(o_ref.dtype)

    own = (pl.BlockSpec((tr, cols), lambda k, i, c: (i, 2 * k + c[0])) if by_cols
           else pl.BlockSpec((None, None, tr, cols), lambda k, i, c: (k, c[0], i, 0)))
    grid_spec = pltpu.PrefetchScalarGridSpec(
        num_scalar_prefetch=1, grid=(4, rows // tr),
        in_specs=[own, pl.BlockSpec((None, tr, cols), lambda k, i, c: (k, i, 0))],
        out_specs=pl.BlockSpec((None, tr, cols), lambda k, i, c: (k, i, 0)))
    return pl.pallas_call(body, grid_spec=grid_spec, name=name,
                          out_shape=jax.ShapeDtypeStruct((4, rows, cols), MXU_DTYPE),
                          compiler_params=_params("arbitrary", "arbitrary"))(core, g, got)


def _adamw_math(w, g, m, v):
    m2 = ADAM_B1 * m + (1.0 - ADAM_B1) * g
    v2 = ADAM_B2 * v + (1.0 - ADAM_B2) * (g * g)
    m_hat = m2 / (1.0 - ADAM_B1 ** ADAM_STEP)
    v_hat = v2 / (1.0 - ADAM_B2 ** ADAM_STEP)
    delta = -ADAM_LR * (m_hat / (jnp.sqrt(v_hat) + ADAM_EPS) + ADAM_WD * w)
    return delta, m2, v2


def _adamw(parts, w, m, v, name):
    rows, cols = w.shape
    tr = _row_tile(rows, cols, 11)

    def body(p_ref, w_ref, m_ref, v_ref, g_ref, d_ref, m2_ref, v2_ref):
        part = lambda k: p_ref[k].astype(F32)
        g = ((part(0) + part(1)) + part(2)) + part(3)
        d, m2, v2 = _adamw_math(w_ref[...], g, m_ref[...], v_ref[...])
        g_ref[...] = g
        d_ref[...] = d
        m2_ref[...] = m2
        v2_ref[...] = v2

    blk = pl.BlockSpec((tr, cols), lambda i: (i, 0))
    out = jax.ShapeDtypeStruct((rows, cols), F32)
    return pl.pallas_call(body, grid=(rows // tr,), name=name,
                          in_specs=[pl.BlockSpec((4, tr, cols), lambda i: (0, i, 0)), blk, blk, blk],
                          out_specs=[blk] * 4, out_shape=[out] * 4,
                          compiler_params=_params("arbitrary"))(parts, w, m, v)


def _adamw_small(gs, ws, ms, vs):
    n = len(ws)

    def body(*refs):
        g_refs, w_refs, m_refs, v_refs = (refs[k * n:(k + 1) * n] for k in range(4))
        d_refs, m2_refs, v2_refs = (refs[(4 + k) * n:(5 + k) * n] for k in range(3))
        for t in range(n):
            d, m2, v2 = _adamw_math(w_refs[t][...], g_refs[t][...], m_refs[t][...], v_refs[t][...])
            d_refs[t][...] = d
            m2_refs[t][...] = m2
            v2_refs[t][...] = v2

    outs = pl.pallas_call(body, name="adamw_small",
                          out_shape=[jax.ShapeDtypeStruct(w.shape, F32) for w in ws] * 3)(*gs, *ws, *ms, *vs)
    return outs[:n], outs[n:2 * n], outs[2 * n:]


_PACK_TILE = 8 * 128


def _pack(arrays):
    rows = []
    for a in arrays:
        flat = a.astype(F32).reshape(-1)
        rows.append(jnp.pad(flat, (0, -flat.shape[0] % _PACK_TILE)).reshape(-1, 128))
    return jnp.concatenate(rows, axis=0)


def _unpack(pack, shapes):
    out, r = [], 0
    for s in shapes:
        n = int(np.prod(s))
        out.append(pack[r:r + -(-n // 128)].reshape(-1)[:n].reshape(s))
        r += -(-n // _PACK_TILE) * 8
    return out


def _pad128(v):
    v = v.reshape(1, -1).astype(F32)
    return jnp.pad(v, ((0, 0), (0, 128 - v.shape[1])))


_WEIGHTS = ["meta", "norm_mix_w", "w_in", "pool_w", "pool_scale", "conv_w", "conv_b", "dt_bias", "a_log", "d_skip",
            "ssm_norm_w", "w_out", "norm_ffn_w", "w_ff1", "w_ff2", "norm_f_w"]
_BIG = ["w_in", "w_out", "w_ff1", "w_ff2"]
_SMALL = [n for n in _WEIGHTS if n not in _BIG]


def kernel(x, meta, norm_mix_w, w_in, pool_w, pool_scale, conv_w, conv_b, dt_bias, a_log, d_skip, ssm_norm_w, w_out, norm_ffn_w, w_ff1, w_ff2, norm_f_w, loss_target, m_meta, m_norm_mix_w, m_w_in, m_pool_w, m_pool_scale, m_conv_w, m_conv_b, m_dt_bias, m_a_log, m_d_skip, m_ssm_norm_w, m_w_out, m_norm_ffn_w, m_w_ff1, m_w_ff2, m_norm_f_w, v_meta, v_norm_mix_w, v_w_in, v_pool_w, v_pool_scale, v_conv_w, v_conv_b, v_dt_bias, v_a_log, v_d_skip, v_ssm_norm_w, v_w_out, v_norm_ffn_w, v_w_ff1, v_w_ff2, v_norm_f_w):
    wts = dict(meta=meta, norm_mix_w=norm_mix_w, w_in=w_in, pool_w=pool_w, pool_scale=pool_scale, conv_w=conv_w,
               conv_b=conv_b, dt_bias=dt_bias, a_log=a_log, d_skip=d_skip, ssm_norm_w=ssm_norm_w, w_out=w_out,
               norm_ffn_w=norm_ffn_w, w_ff1=w_ff1, w_ff2=w_ff2, norm_f_w=norm_f_w)
    mom1 = dict(zip(_WEIGHTS, (m_meta, m_norm_mix_w, m_w_in, m_pool_w, m_pool_scale, m_conv_w, m_conv_b, m_dt_bias,
                               m_a_log, m_d_skip, m_ssm_norm_w, m_w_out, m_norm_ffn_w, m_w_ff1, m_w_ff2, m_norm_f_w)))
    mom2 = dict(zip(_WEIGHTS, (v_meta, v_norm_mix_w, v_w_in, v_pool_w, v_pool_scale, v_conv_w, v_conv_b, v_dt_bias,
                               v_a_log, v_d_skip, v_ssm_norm_w, v_w_out, v_norm_ffn_w, v_w_ff1, v_w_ff2, v_norm_f_w)))
    xi, yi, ci = _coords()
    dev = 4 * xi + 2 * yi + ci
    win_cols = w_in.shape[-1]
    cw_cols = conv_w.shape[-1]

    nb, seq, _ = x.shape
    core = jnp.reshape(ci, (1,)).astype(jnp.int32)
    owners = lambda a: a.reshape((4, 2) + a.shape[1:])

    lead_pack = jnp.zeros((N_META, 512), F32)
    lead_pack = lead_pack.at[:, :128].set(meta).at[:4, 128:128 + cw_cols].set(conv_w[0])
    g_win, g_lead = _weight_gather([_mx(w_in[0]), lead_pack])
    win_full = jnp.transpose(g_win, (1, 0, 2)).reshape(D_MODEL, N_DEV * win_cols)
    win_full = jnp.pad(win_full, ((0, 0), (0, PROJ_W - N_DEV * win_cols)))
    meta_full = jnp.transpose(g_lead[:, :, :128], (1, 0, 2)).reshape(N_META, D_MODEL)
    cw_full = jnp.transpose(g_lead[:, :4, 128:128 + cw_cols], (1, 0, 2)).reshape(4, D_XBC)

    lead = jnp.concatenate([jnp.zeros((PAD_ROWS, D_MODEL), F32), meta_full] * nb, axis=0)
    x_rows = x.reshape(nb * seq, D_MODEL)
    tgt_rows = loss_target.reshape(nb * seq, D_MODEL)
    dt_bias_p, a_log_p = _pad128(dt_bias), _pad128(a_log)
    d_x = jnp.repeat(d_skip.reshape(1, N_HEADS).astype(F32), HEAD_DIM, axis=1)
    norm_f_row = norm_f_w.reshape(1, D_MODEL)

    hn1, proj = _in_proj(x_rows, lead, norm_mix_w, win_full)
    late_cols = [False, True, False]
    y, ypre, conv_pre, states, (g_wout, wff1_full, g_wff2) = _mixer_fwd(
        proj, cw_full, conv_b, dt_bias_p, a_log_p, d_x, ssm_norm_w, pool_w[0], pool_scale, nb,
        [_mx(w_out[0]), _mx(w_ff1[0]), _mx(w_ff2[0])], late_cols)
    wout_full = g_wout.reshape(D_MIX, D_MODEL)
    wff2_full = g_wff2.reshape(D_FF, D_MODEL)
    loss, gr_nf, gr_nffn, ff, da, hn2, dh1, dh2, dy = _ffn_fwd_bwd(
        x_rows, lead, y, tgt_rows, wout_full, norm_ffn_w, wff1_full, wff2_full, norm_f_row)
    gr_wff2 = _tn_matmul(ff, dh2, "grad_w_ff2", tka=1024)
    gr_wff1 = _tn_matmul(hn2, da, "grad_w_ff1", tka=512)
    gr_wout = _tn_matmul(y, dh1, "grad_w_out", tka=1024)

    late_parts = [owners(gr_wout.reshape(N_DEV, D_MIX // N_DEV, D_MODEL)), gr_wff1,
                  owners(gr_wff2.reshape(N_DEV, D_FF // N_DEV, D_MODEL))]
    late_got = _grad_exchange_d2d(late_parts, late_cols, "grad_exchange_d2d_late")
    late_sums = [_chip_sum(late_parts[t], late_cols[t], late_got[t], core, "chip_sum_late_%d" % t) for t in range(3)]
    (dproj, gr_cw, gr_cb, gr_dtb, gr_alog, gr_d, gr_nw, gr_pw, gr_ps), late_exchanged = _mixer_bwd(
        proj, dy, ypre, conv_pre, states, cw_full, dt_bias_p, a_log_p, d_x, ssm_norm_w, pool_w[0], pool_scale, nb,
        late_sums)

    gr_win = _tn_matmul(hn1, dproj, "grad_w_in", tka=512)
    win_parts = [owners(jnp.transpose(gr_win[:, :N_DEV * win_cols].reshape(D_MODEL, N_DEV, win_cols), (1, 0, 2)))]
    win_got = _grad_exchange_d2d(win_parts, [False], "grad_exchange_d2d_w_in")
    win_sum = _chip_sum(win_parts[0], False, win_got[0], core, "chip_sum_w_in")
    gx_rows, gr_nmix, gr_meta, win_exchanged = _in_proj_bwd(dproj, x_rows, lead, dh1, norm_mix_w, win_full, [win_sum])
    parts = dict(w_in=win_exchanged[0], w_out=late_exchanged[0], w_ff1=late_exchanged[1], w_ff2=late_exchanged[2])

    small_full = dict(meta=gr_meta, norm_mix_w=gr_nmix, pool_w=gr_pw, pool_scale=gr_ps, conv_w=gr_cw, conv_b=gr_cb,
                      dt_bias=gr_dtb[:, :N_HEADS], a_log=gr_alog[:, :N_HEADS], d_skip=gr_d[:, :N_HEADS],
                      ssm_norm_w=gr_nw, norm_ffn_w=gr_nffn, norm_f_w=gr_nf, loss=loss[0:1, 0:1])
    small_names = list(small_full)
    small_sum = _small_allreduce(_pack([small_full[n] for n in small_names]))
    gs = dict(zip(small_names, _unpack(small_sum, [small_full[n].shape for n in small_names])))
    gs["meta"] = lax.dynamic_slice_in_dim(gs["meta"], dev * 128, 128, axis=1)
    gs["conv_w"] = lax.dynamic_slice_in_dim(gs["conv_w"], dev * cw_cols, cw_cols, axis=1)

    res = {}
    for n in _BIG:
        shp = wts[n].shape
        res[n] = [o.reshape(shp) for o in _adamw(parts[n], wts[n][0], mom1[n][0], mom2[n][0], "adamw_" + n)]
    as2d = lambda a: a.reshape(-1, a.shape[-1])
    small_g = [as2d(gs[n].reshape(wts[n].shape)) for n in _SMALL]
    small_out = _adamw_small(small_g, *[[as2d(d[n]) for n in _SMALL] for d in (wts, mom1, mom2)])
    for k, n in enumerate(_SMALL):
        res[n] = [o[k].reshape(wts[n].shape) for o in (small_g,) + tuple(small_out)]

    grad_x = gx_rows.reshape(nb, seq, D_MODEL)
    return (gs["loss"][0, 0], grad_x, *[res[n][0] for n in _WEIGHTS], *[res[n][1] for n in _WEIGHTS],
            *[res[n][2] for n in _WEIGHTS], *[res[n][3] for n in _WEIGHTS])
```

```python
import functools

import numpy as np
import jax
import jax.numpy as jnp
from jax import lax
from jax.experimental import pallas as pl
from jax.experimental.pallas import tpu as pltpu

F32 = jnp.float32
MXU_DTYPE = jnp.bfloat16

D_MODEL = 1024
D_POOL = 512
D_SSM = 1536
D_XBC = 2560
N_HEADS = 24
HEAD_DIM = 64
N_GROUPS = 4
GROUP_CH = D_SSM // N_GROUPS
D_STATE = 128
CHUNK = 128
N_META = 16
LEAD = CHUNK
PAD_ROWS = LEAD - N_META
ROW_TILE = 2 * CHUNK
D_MIX = D_POOL + D_SSM
D_FF = 4096
PROJ_W = 4736
OFF_Z = D_POOL
OFF_X = D_POOL + D_SSM
OFF_DT = OFF_X + D_XBC
D_IN_PROJ = OFF_DT + N_HEADS
POOL_WINDOWS = (2, 4, 8, 16)
HALO = 16
EPS = 1e-5
N_DEV = 8

ADAM_LR, ADAM_B1, ADAM_B2, ADAM_EPS, ADAM_WD, ADAM_STEP = 0.001, 0.9, 0.999, 1e-08, 0.01, 10

VMEM_LIMIT = 60 * 1024 * 1024


def _mx(a):
    return a.astype(MXU_DTYPE)


def _dot(a, b):
    return jnp.dot(_mx(a), _mx(b), preferred_element_type=F32)


def _dot_nt(a, b):
    return lax.dot_general(_mx(a), _mx(b), (((1,), (1,)), ((), ())), preferred_element_type=F32)


def _dot_tn(a, b):
    return lax.dot_general(_mx(a), _mx(b), (((0,), (0,)), ((), ())), preferred_element_type=F32)


def _split3(x):
    hi = x.astype(MXU_DTYPE)
    r = x - hi.astype(F32)
    mid = r.astype(MXU_DTYPE)
    lo = (r - mid.astype(F32)).astype(MXU_DTYPE)
    return hi, mid, lo


def _exact_l(c, x):
    hi, mid, lo = _split3(x)
    f = lambda p: jnp.dot(c, p, preferred_element_type=F32)
    return f(hi) + f(mid) + f(lo)


def _exact_r(x, c):
    hi, mid, lo = _split3(x)
    f = lambda p: jnp.dot(p, c, preferred_element_type=F32)
    return f(hi) + f(mid) + f(lo)


def _contract(x, c):
    hi = x.astype(MXU_DTYPE)
    lo = (x - hi.astype(F32)).astype(MXU_DTYPE)
    return jnp.dot(hi, c, preferred_element_type=F32) + jnp.dot(lo, c, preferred_element_type=F32)


def _sigmoid(x):
    return jax.nn.sigmoid(x)


def _softplus(x):
    return jnp.maximum(x, 0.0) + jnp.log1p(jnp.exp(-jnp.abs(x)))


def _silu_and_grad(x):
    s = _sigmoid(x)
    return x * s, s * (1.0 + x * (1.0 - s))


def _shift_down(ext, s):
    if s == 0:
        return ext[HALO:, :]
    return pltpu.roll(ext, s, 0)[HALO:, :]


def _shift_up(ext, s):
    if s == 0:
        return ext[:CHUNK, :]
    return pltpu.roll(ext, ext.shape[0] - s, 0)[:CHUNK, :]


def _by_pool_group(lane, a2, a4, a8, a16):
    return jnp.where(lane < 128, a2, jnp.where(lane < 256, a4, jnp.where(lane < 384, a8, a16)))


def _pool_inv_count(chunk_idx):
    row = lax.broadcasted_iota(jnp.int32, (CHUNK, D_POOL), 0)
    lane = lax.broadcasted_iota(jnp.int32, (CHUNK, D_POOL), 1)
    pos1 = jnp.maximum(chunk_idx * CHUNK + row - (PAD_ROWS - 1), 1)
    w = _by_pool_group(lane, 2, 4, 8, 16)
    return 1.0 / jnp.minimum(pos1, w).astype(F32), lane


def _pool_window_sums(u_ext, lane):
    s2 = u_ext + pltpu.roll(u_ext, 1, 0)
    s4 = s2 + pltpu.roll(s2, 2, 0)
    s8 = s4 + pltpu.roll(s4, 4, 0)
    s16 = s8 + pltpu.roll(s8, 8, 0)
    return _by_pool_group(lane, s2[HALO:], s4[HALO:], s8[HALO:], s16[HALO:])


def _pool_window_sums_ahead(q_ext, lane):
    n = q_ext.shape[0]
    r2 = q_ext + pltpu.roll(q_ext, n - 1, 0)
    r4 = r2 + pltpu.roll(r2, n - 2, 0)
    r8 = r4 + pltpu.roll(r4, n - 4, 0)
    r16 = r8 + pltpu.roll(r8, n - 8, 0)
    return _by_pool_group(lane, r2[:CHUNK], r4[:CHUNK], r8[:CHUNK], r16[:CHUNK])


def _conv_pre(ext, xbc, cw, cb):
    return (cb + cw[3:4, :] * xbc + cw[2:3, :] * _shift_down(ext, 1)
            + cw[1:2, :] * _shift_down(ext, 2) + cw[0:1, :] * _shift_down(ext, 3))


def _dt_and_cumsum(dtr, dt_bias, a_log, valid, tril):
    lane = lax.broadcasted_iota(jnp.int32, (CHUNK, 128), 1)
    head = lane < N_HEADS
    pre = dtr + dt_bias
    dt = jnp.where(valid & head, _softplus(pre), 0.0)
    a_row = jnp.where(head[0:1, :], -jnp.exp(a_log), 0.0)
    a_col = _exact_l(tril, dt * a_row)
    return dt, a_row, a_col, pre, head


def _decay(a_col, a_row_t, h, causal):
    seg = a_col[:, h:h + 1] - a_row_t[h:h + 1, :]
    return jnp.where(causal, jnp.exp(jnp.minimum(seg, 0.0)), 0.0)


def _ssd_chunk_fwd(xs, bm, cm, dt, a_col, s_prev, d_x, e_mat, et_mat):
    lane = lax.broadcasted_iota(jnp.int32, (CHUNK, 128), 1)
    rowi = lax.broadcasted_iota(jnp.int32, (CHUNK, CHUNK), 0)
    coli = lax.broadcasted_iota(jnp.int32, (CHUNK, CHUNK), 1)
    causal = rowi >= coli
    a_row_t = a_col.T
    ax = _exact_r(a_col, e_mat)
    dtx = _exact_r(dt, e_mat)
    xdt = xs * dtx
    ax_last = ax[CHUNK - 1:CHUNK, :]
    e_a = jnp.exp(ax)
    w_end = xdt * jnp.exp(ax_last - ax)
    cd_col = jnp.exp(_exact_l(et_mat, a_row_t)[:, CHUNK - 1:CHUNK])
    ys, s_new = [], []
    for g in range(N_GROUPS):
        gs = slice(g * GROUP_CH, (g + 1) * GROUP_CH)
        bg = bm[:, g * D_STATE:(g + 1) * D_STATE]
        cg = cm[:, g * D_STATE:(g + 1) * D_STATE]
        sg = s_prev[gs, :]
        cb = _dot_nt(cg, bg)
        y_off = _dot_nt(cg, sg) * e_a[:, gs]
        s_new.append(sg * cd_col[gs, :] + _dot_tn(w_end[:, gs], bg))
        for pr in range(3):
            c0 = g * GROUP_CH + pr * 128
            xdt_p = xdt[:, c0:c0 + 128]
            h0 = g * 6 + pr * 2
            y0 = _dot(cb * _decay(a_col, a_row_t, h0, causal), xdt_p)
            y1 = _dot(cb * _decay(a_col, a_row_t, h0 + 1, causal), xdt_p)
            ys.append(jnp.where(lane < HEAD_DIM, y0, y1) + y_off[:, pr * 128:(pr + 1) * 128])
    y = jnp.concatenate(ys, axis=1) + d_x * xs
    return y, jnp.concatenate(s_new, axis=0)


def _ssd_chunk_bwd(xs, bm, cm, dt, a_row, a_col, s_prev, ds_new, dy, d_x, e_mat, et_mat, triu):
    lane = lax.broadcasted_iota(jnp.int32, (CHUNK, 128), 1)
    sub = lax.broadcasted_iota(jnp.int32, (CHUNK, 128), 0)
    rowi = lax.broadcasted_iota(jnp.int32, (CHUNK, CHUNK), 0)
    coli = lax.broadcasted_iota(jnp.int32, (CHUNK, CHUNK), 1)
    causal = rowi >= coli
    a_row_t = a_col.T
    ax = _exact_r(a_col, e_mat)
    dtx = _exact_r(dt, e_mat)
    xdt = xs * dtx
    ax_last = ax[CHUNK - 1:CHUNK, :]
    e_a = jnp.exp(ax)
    dte = jnp.exp(ax_last - ax)
    w_end = xdt * dte
    cd_col = jnp.exp(_exact_l(et_mat, a_row_t)[:, CHUNK - 1:CHUNK])
    dye = dy * e_a

    dxdt, zc, ww_all, dbs, dcs, dsp, t1s = [], [], [], [], [], [], []
    zcol = jnp.zeros((CHUNK, 128), F32)
    zrow = jnp.zeros((128, CHUNK), F32)
    for g in range(N_GROUPS):
        gs = slice(g * GROUP_CH, (g + 1) * GROUP_CH)
        bg = bm[:, g * D_STATE:(g + 1) * D_STATE]
        cg = cm[:, g * D_STATE:(g + 1) * D_STATE]
        sg = s_prev[gs, :]
        dsg = ds_new[gs, :]
        cb = _dot_nt(cg, bg)
        cs = _dot_nt(cg, sg)
        dcg = _dot(dye[:, gs], sg)
        dsp.append(dsg * cd_col[gs, :] + _dot_tn(dye[:, gs], cg))
        dwg = _dot_nt(bg, dsg)
        dbg = _dot(w_end[:, gs], dsg)
        ww = dwg * w_end[:, gs]
        ww_all.append(ww)
        zc.append(dye[:, gs] * cs - ww)
        t1s.append(jnp.sum(dsg * sg, axis=1, keepdims=True) * cd_col[gs, :])
        dxdt_g = dwg * dte[:, gs]
        dcb = jnp.zeros((CHUNK, CHUNK), F32)
        pairs = []
        for pr in range(3):
            c0 = g * GROUP_CH + pr * 128
            xdt_p = xdt[:, c0:c0 + 128]
            dy_p = dy[:, c0:c0 + 128]
            acc = None
            for half in range(2):
                h = g * 6 + pr * 2 + half
                ld = _decay(a_col, a_row_t, h, causal)
                gm = cb * ld
                dym = jnp.where((lane < HEAD_DIM) if half == 0 else (lane >= HEAD_DIM), dy_p, 0.0)
                dg = _dot_nt(dym, xdt_p)
                dseg = dg * gm
                dcb = dcb + dg * ld
                t = _dot_tn(gm, dym)
                acc = t if acc is None else acc + t
                zcol = jnp.where(lane == h, jnp.sum(dseg, axis=1, keepdims=True), zcol)
                zrow = jnp.where(sub == h, jnp.sum(dseg, axis=0, keepdims=True), zrow)
            pairs.append(acc)
        dxdt.append(dxdt_g + jnp.concatenate(pairs, axis=1))
        dcs.append(dcg + _dot(dcb, bg))
        dbs.append(dbg + _dot_tn(dcb, cg))
    dxdt = jnp.concatenate(dxdt, axis=1)
    zc = jnp.concatenate(zc, axis=1)
    ww_all = jnp.concatenate(ww_all, axis=1)
    t1 = jnp.concatenate(t1s, axis=0)

    sub8 = lax.broadcasted_iota(jnp.int32, (8, D_SSM), 0)
    col_sums = jnp.where(sub8 == 0, jnp.sum(dy * xs, axis=0, keepdims=True),
                         jnp.where(sub8 == 1, jnp.sum(ww_all, axis=0, keepdims=True), 0.0))
    head_sums = _exact_r(col_sums, et_mat)
    dd = head_sums[0:1, :]
    q_row = head_sums[1:2, :] + jnp.sum(et_mat.astype(F32) * t1, axis=0, keepdims=True)
    da_col = _contract(zc, et_mat) + zcol - zrow.T
    da_col = da_col + jnp.where(sub == CHUNK - 1, q_row, 0.0)
    ddt = _contract(dxdt * xs, et_mat)
    dxs = dxdt * dtx + d_x * dy
    rc = _exact_l(triu, da_col)
    ddt = ddt + rc * a_row
    da = jnp.sum(rc * dt, axis=0, keepdims=True)
    return (dxs, jnp.concatenate(dbs, axis=1), jnp.concatenate(dcs, axis=1), ddt, da, dd,
            jnp.concatenate(dsp, axis=0))


def _ssd_constants():
    ch = np.arange(D_SSM) // HEAD_DIM
    e = (np.arange(128)[:, None] == ch[None, :]).astype(np.float32)
    tril = np.tril(np.ones((CHUNK, CHUNK), np.float32))
    as_mx = lambda a: jnp.asarray(a, MXU_DTYPE)
    return as_mx(e), as_mx(e.T), as_mx(tril), as_mx(tril.T)


def _full(shape):
    nd = len(shape)
    return pl.BlockSpec(shape, lambda *_: (0,) * nd)


def _params(*sem):
    return pltpu.CompilerParams(dimension_semantics=sem, vmem_limit_bytes=VMEM_LIMIT)


def _token_tiles(width, n_tok_tiles):
    return pl.BlockSpec((ROW_TILE, width), lambda i: (jnp.minimum(i, n_tok_tiles - 1), 0))


def _in_proj(x, lead, w1, win):
    nt = x.shape[0] // ROW_TILE
    m = x.shape[0] + ROW_TILE
    tm = ROW_TILE

    def body(x_ref, lead_ref, w1_ref, win_hbm, hn_ref, proj_ref, win_v, sem):
        i = pl.program_id(0)

        @pl.when(i == 0)
        def _():
            cp = pltpu.make_async_copy(win_hbm, win_v, sem)
            cp.start()
            cp.wait()

        x = jnp.where(i == nt, lead_ref[...], x_ref[...])
        r = lax.rsqrt(jnp.mean(x * x, axis=-1, keepdims=True) + EPS)
        hn = _mx(x * r * w1_ref[...])
        hn_ref[...] = hn
        for j in range(0, PROJ_W, 512):
            w = min(512, PROJ_W - j)
            proj_ref[:, j:j + w] = jnp.dot(hn, win_v[:, j:j + w], preferred_element_type=F32)

    return pl.pallas_call(
        body, grid=(m // tm,), name="in_proj",
        in_specs=[_token_tiles(D_MODEL, nt), _full((ROW_TILE, D_MODEL)), _full((1, D_MODEL)),
                  pl.BlockSpec(memory_space=pl.ANY)],
        out_specs=[pl.BlockSpec((tm, D_MODEL), lambda i: (i, 0)), pl.BlockSpec((tm, PROJ_W), lambda i: (i, 0))],
        out_shape=[jax.ShapeDtypeStruct((m, D_MODEL), MXU_DTYPE), jax.ShapeDtypeStruct((m, PROJ_W), F32)],
        scratch_shapes=[pltpu.VMEM((D_MODEL, PROJ_W), MXU_DTYPE), pltpu.SemaphoreType.DMA],
        compiler_params=_params("arbitrary"),
    )(x, lead, w1, win)


def _ffn_fwd_bwd(x, lead, y, tgt, wout, w2n, wff1, wff2, wfn):
    nt = x.shape[0] // ROW_TILE
    m = x.shape[0] + ROW_TILE
    tm = ROW_TILE
    nj = D_FF // 1024

    def body(x_ref, lead_ref, y_ref, tgt_ref, w2n_ref, wfn_ref, wout_hbm, wff1_hbm, wff2_hbm,
             loss_ref, gwf_ref, gw2_ref, ff_ref, da_ref, hn2_ref, dh1_ref, dh2_ref, dy_ref,
             wout_v, wff1_v, wff2_v, a_s, sems):
        i = pl.program_id(0)
        hp = jnp.where(i == nt, lead_ref[...], x_ref[...])

        @pl.when(i == 0)
        def _():
            cps = [pltpu.make_async_copy(s, d, sems.at[k])
                   for k, (s, d) in enumerate(((wout_hbm, wout_v), (wff1_hbm, wff1_v), (wff2_hbm, wff2_v)))]
            for cp in cps:
                cp.start()
            for cp in cps:
                cp.wait()
            loss_ref[...] = jnp.zeros_like(loss_ref)
            gwf_ref[...] = jnp.zeros_like(gwf_ref)
            gw2_ref[...] = jnp.zeros_like(gw2_ref)

        h1 = hp + jnp.dot(y_ref[...], wout_v[...], preferred_element_type=F32)
        r2 = lax.rsqrt(jnp.mean(h1 * h1, axis=-1, keepdims=True) + EPS)
        n2 = h1 * r2
        w2n_row = w2n_ref[...]
        hn2 = _mx(n2 * w2n_row)
        hn2_ref[...] = hn2
        h2 = h1
        for j in range(nj):
            js = slice(j * 1024, (j + 1) * 1024)
            a = jnp.dot(hn2, wff1_v[:, js], preferred_element_type=F32)
            a_s[:, js] = a
            ra = jnp.maximum(a, 0.0)
            ff = _mx(ra * ra)
            ff_ref[:, js] = ff
            h2 = h2 + jnp.dot(ff, wff2_v[js, :], preferred_element_type=F32)

        r3 = lax.rsqrt(jnp.mean(h2 * h2, axis=-1, keepdims=True) + EPS)
        n3 = h2 * r3
        wf_row = wfn_ref[...]
        err = n3 * wf_row - tgt_ref[...]
        tokf = (i < nt).astype(F32)
        loss_ref[...] += 0.5 * jnp.sum(jnp.mean(err * err, axis=-1, keepdims=True) * tokf)
        dout = err * (tokf / D_MODEL)
        gwf_ref[...] += jnp.sum(dout * n3, axis=0, keepdims=True)
        dn3 = dout * wf_row
        dh2 = r3 * (dn3 - n3 * jnp.mean(dn3 * n3, axis=-1, keepdims=True))
        dh2m = _mx(dh2)
        dh2_ref[...] = dh2m

        dhn2 = jnp.zeros((tm, D_MODEL), F32)
        for j in range(nj):
            js = slice(j * 1024, (j + 1) * 1024)
            dff = lax.dot_general(dh2m, wff2_v[js, :], (((1,), (1,)), ((), ())), preferred_element_type=F32)
            da = _mx(dff * (2.0 * jnp.maximum(a_s[:, js], 0.0)))
            da_ref[:, js] = da
            dhn2 = dhn2 + lax.dot_general(da, wff1_v[:, js], (((1,), (1,)), ((), ())), preferred_element_type=F32)
        gw2_ref[...] += jnp.sum(dhn2 * n2, axis=0, keepdims=True)
        dn2 = dhn2 * w2n_row
        dh1 = dh2 + r2 * (dn2 - n2 * jnp.mean(dn2 * n2, axis=-1, keepdims=True))
        dh1_ref[...] = dh1
        dy_ref[...] = lax.dot_general(_mx(dh1), wout_v[...], (((1,), (1,)), ((), ())), preferred_element_type=F32)

    rows = lambda w: pl.BlockSpec((tm, w), lambda i: (i, 0))
    hbm = pl.BlockSpec(memory_space=pl.ANY)
    return pl.pallas_call(
        body, grid=(m // tm,), name="ffn_fwd_bwd",
        in_specs=[_token_tiles(D_MODEL, nt), _full((ROW_TILE, D_MODEL)), rows(D_MIX), _token_tiles(D_MODEL, nt),
                  _full((1, D_MODEL)), _full((1, D_MODEL)), hbm, hbm, hbm],
        out_specs=[_full((1, 128)), _full((1, D_MODEL)), _full((1, D_MODEL)), rows(D_FF), rows(D_FF), rows(D_MODEL),
                   rows(D_MODEL), rows(D_MODEL), rows(D_MIX)],
        out_shape=[jax.ShapeDtypeStruct((1, 128), F32), jax.ShapeDtypeStruct((1, D_MODEL), F32),
                   jax.ShapeDtypeStruct((1, D_MODEL), F32), jax.ShapeDtypeStruct((m, D_FF), MXU_DTYPE),
                   jax.ShapeDtypeStruct((m, D_FF), MXU_DTYPE), jax.ShapeDtypeStruct((m, D_MODEL), MXU_DTYPE),
                   jax.ShapeDtypeStruct((m, D_MODEL), F32), jax.ShapeDtypeStruct((m, D_MODEL), MXU_DTYPE),
                   jax.ShapeDtypeStruct((m, D_MIX), F32)],
        scratch_shapes=[pltpu.VMEM((D_MIX, D_MODEL), MXU_DTYPE), pltpu.VMEM((D_MODEL, D_FF), MXU_DTYPE),
                        pltpu.VMEM((D_FF, D_MODEL), MXU_DTYPE), pltpu.VMEM((tm, D_FF), F32),
                        pltpu.SemaphoreType.DMA((3,))],
        compiler_params=_params("arbitrary"),
    )(x, lead, y, tgt, w2n, wfn, wout, wff1, wff2)


def _in_proj_bwd(dproj, x, lead, dh1, w1, win, chip_sums):
    nt = x.shape[0] // ROW_TILE
    m = x.shape[0] + ROW_TILE
    tm = ROW_TILE
    ns = len(chip_sums)

    def body(dp_ref, x_ref, lead_ref, dh1_ref, w1_ref, win_hbm, *rest):
        cs_refs, (gx_ref, gw1_ref, gmeta_ref) = rest[:ns], rest[ns:ns + 3]
        part_refs, (win_v, sem) = rest[ns + 3:2 * ns + 3], rest[2 * ns + 3:2 * ns + 5]
        exchange = _ChipExchange(cs_refs, part_refs, [False] * ns, *rest[2 * ns + 5:])
        i = pl.program_id(0)

        @pl.when(i == 0)
        def _():
            exchange.start()
            cp = pltpu.make_async_copy(win_hbm, win_v, sem)
            cp.start()
            cp.wait()
            gw1_ref[...] = jnp.zeros_like(gw1_ref)
            gmeta_ref[...] = jnp.zeros_like(gmeta_ref)

        dhn = lax.dot_general(dp_ref[...], win_v[...], (((1,), (1,)), ((), ())), preferred_element_type=F32)
        x = jnp.where(i == nt, lead_ref[...], x_ref[...])
        r = lax.rsqrt(jnp.mean(x * x, axis=-1, keepdims=True) + EPS)
        n = x * r
        gw1_ref[...] += jnp.sum(dhn * n, axis=0, keepdims=True)
        dn = dhn * w1_ref[...]
        dh0 = dh1_ref[...] + r * (dn - n * jnp.mean(dn * n, axis=-1, keepdims=True))

        @pl.when(i < nt)
        def _():
            gx_ref[...] = dh0

        @pl.when(i == nt)
        def _():
            gmeta_ref[...] = dh0[PAD_ROWS:LEAD, :] + dh0[LEAD + PAD_ROWS:2 * LEAD, :]
            exchange.finish()

    rows = lambda w: pl.BlockSpec((tm, w), lambda i: (i, 0))
    hbm = pl.BlockSpec(memory_space=pl.ANY)
    outs = pl.pallas_call(
        body, grid=(m // tm,), name="in_proj_bwd",
        in_specs=[rows(PROJ_W), _token_tiles(D_MODEL, nt), _full((ROW_TILE, D_MODEL)), rows(D_MODEL),
                  _full((1, D_MODEL)), hbm] + [hbm] * ns,
        out_specs=[_token_tiles(D_MODEL, nt), _full((1, D_MODEL)), _full((N_META, D_MODEL))] + [hbm] * ns,
        out_shape=[jax.ShapeDtypeStruct(x.shape, F32), jax.ShapeDtypeStruct((1, D_MODEL), F32),
                   jax.ShapeDtypeStruct((N_META, D_MODEL), F32)]
        + _ChipExchange.out_shapes(chip_sums, [False] * ns),
        scratch_shapes=[pltpu.VMEM((D_MODEL, PROJ_W), MXU_DTYPE), pltpu.SemaphoreType.DMA] + _ChipExchange.scratch(ns),
        compiler_params=_params("arbitrary"),
    )(dproj, x, lead, dh1, w1, win, *chip_sums)
    return outs[0], outs[1], outs[2], outs[3:]


def _tn_matmul(a, b, name, tka, tkm=768, tn=512):
    m, ka = a.shape
    nb = b.shape[1]
    n_steps = m // tkm

    def body(a_ref, b_ref, o_ref):
        @pl.when(pl.program_id(1) == 0)
        def _():
            o_ref[...] = jnp.zeros_like(o_ref)

        at = _mx(a_ref[...])
        for j in range(0, nb, tn):
            w = min(tn, nb - j)
            o_ref[:, j:j + w] += lax.dot_general(at, _mx(b_ref[:, j:j + w]), (((0,), (0,)), ((), ())),
                                                 preferred_element_type=F32)

    return pl.pallas_call(
        body, grid=(ka // tka, n_steps), name=name,
        in_specs=[pl.BlockSpec((tkm, tka), lambda i, k: (k, i)), pl.BlockSpec((tkm, nb), lambda i, k: (k, 0))],
        out_specs=pl.BlockSpec((tka, nb), lambda i, k: (i, 0)),
        out_shape=jax.ShapeDtypeStruct((ka, nb), F32),
        compiler_params=_params("arbitrary", "arbitrary"),
    )(a, b)


def _tn_matmul_banded(a, b, name, band, tka, tkm=768, tn=512):
    m, ka = a.shape
    nb = b.shape[1]
    n_steps = m // tkm

    def body(a_ref, b_ref, o_ref, acc):
        k = pl.program_id(1)

        @pl.when(k == 0)
        def _():
            acc[...] = jnp.zeros_like(acc)

        at = _mx(a_ref[...])
        for j in range(0, nb, tn):
            w = min(tn, nb - j)
            acc[:, j:j + w] += lax.dot_general(at, _mx(b_ref[:, j:j + w]), (((0,), (0,)), ((), ())),
                                               preferred_element_type=F32)

        @pl.when(k == n_steps - 1)
        def _():
            for j in range(N_DEV):
                o_ref[j] = acc[:, j * band:(j + 1) * band]

    return pl.pallas_call(
        body, grid=(ka // tka, n_steps), name=name,
        in_specs=[pl.BlockSpec((tkm, tka), lambda i, k: (k, i)), pl.BlockSpec((tkm, nb), lambda i, k: (k, 0))],
        out_specs=pl.BlockSpec((N_DEV, tka, band), lambda i, k: (0, i, 0)),
        out_shape=jax.ShapeDtypeStruct((N_DEV, ka, band), F32),
        scratch_shapes=[pltpu.VMEM((tka, nb), F32)],
        compiler_params=_params("arbitrary", "arbitrary"),
    )(a, b)


def _assemble_bands(g, width):
    n, rows, band = g.shape
    tr = 256

    def body(g_ref, o_ref):
        parts = [g_ref[j] for j in range(n)] + [jnp.zeros((tr, width - n * band), g.dtype)]
        o_ref[...] = jnp.concatenate(parts, axis=1)

    return pl.pallas_call(
        body, grid=(rows // tr,), name="assemble_w_in",
        in_specs=[pl.BlockSpec((n, tr, band), lambda i: (0, i, 0))],
        out_specs=pl.BlockSpec((tr, width), lambda i: (i, 0)),
        out_shape=jax.ShapeDtypeStruct((rows, width), g.dtype),
        compiler_params=_params("arbitrary"),
    )(g)


def _chunk_block(b, c, nb, nc):
    return jnp.where(c == 0, nb * (nc - 1) + b, b * (nc - 1) + c - 1)

def _mixer_fwd(proj, cw, cb, dt_bias, a_log, d_x, nw, pool_w, pool_scale, nb, shards, by_cols):
    m = proj.shape[0]
    nc = m // nb // CHUNK
    n_steps = nb * nc
    ns = len(shards)
    e_mat, et_mat, tril, _ = _ssd_constants()

    def body(p_ref, cw_ref, cb_ref, dtb_ref, alog_ref, dx_ref, nw_ref, pw_ref, ps_ref, e_ref, et_ref, tril_ref, *rest):
        shard_refs, (y_ref, ypre_ref, pre_ref, st_ref) = rest[:ns], rest[ns:ns + 4]
        gathered_refs, (xtail, utail, state) = rest[ns + 4:2 * ns + 4], rest[2 * ns + 4:2 * ns + 7]
        gather = _Gather(shard_refs, gathered_refs, by_cols, *rest[2 * ns + 7:])
        c = pl.program_id(1)
        step = pl.program_id(0) * nc + c

        @pl.when(step == 0)
        def _():
            gather.start()

        @pl.when(step == n_steps // 2)
        def _():
            gather.forward()

        @pl.when(c == 0)
        def _():
            xtail[...] = jnp.zeros_like(xtail)
            utail[...] = jnp.zeros_like(utail)
            state[...] = jnp.zeros_like(state)

        valid = (c > 0) | (lax.broadcasted_iota(jnp.int32, (CHUNK, 1), 0) >= PAD_ROWS)

        u = p_ref[:, 0:D_POOL]
        inv_cnt, lane = _pool_inv_count(c)
        win = _pool_window_sums(jnp.concatenate([utail[...], u], axis=0), lane)
        utail[...] = u[CHUNK - HALO:, :]
        pooled = win * inv_cnt - u
        mixed = jnp.concatenate(
            [_dot(pooled[:, g * 128:(g + 1) * 128], pw_ref[g]) for g in range(len(POOL_WINDOWS))], axis=1)
        y_ref[:, 0:D_POOL] = _mx(mixed * ps_ref[...])

        xbc = p_ref[:, OFF_X:OFF_X + D_XBC]
        pre = _conv_pre(jnp.concatenate([xtail[...], xbc], axis=0), xbc, cw_ref[...], cb_ref[...])
        xtail[...] = xbc[CHUNK - HALO:, :]
        pre_ref[...] = pre
        xc = pre * _sigmoid(pre)
        dt, _, a_col, _, _ = _dt_and_cumsum(p_ref[:, OFF_DT:OFF_DT + 128], dtb_ref[...], alog_ref[...], valid,
                                            tril_ref[...])
        s_prev = state[...]
        st_ref[0] = s_prev
        yp, s_new = _ssd_chunk_fwd(xc[:, 0:D_SSM], xc[:, D_SSM:D_SSM + 512], xc[:, D_SSM + 512:], dt, a_col, s_prev,
                                   dx_ref[...], e_ref[...], et_ref[...])
        state[...] = s_new
        ypre_ref[...] = yp
        z = p_ref[:, OFF_Z:OFF_Z + D_SSM]
        yz = yp * (z * _sigmoid(z))
        outs = []
        for g in range(N_GROUPS):
            gs = slice(g * GROUP_CH, (g + 1) * GROUP_CH)
            r = lax.rsqrt(jnp.mean(yz[:, gs] * yz[:, gs], axis=-1, keepdims=True) + EPS)
            outs.append(yz[:, gs] * r)
        y_ref[:, D_POOL:] = _mx(jnp.concatenate(outs, axis=1) * nw_ref[...])

        @pl.when(step == n_steps - 1)
        def _():
            gather.finish()

    blk = lambda w: pl.BlockSpec((CHUNK, w), lambda b, c: (_chunk_block(b, c, nb, nc), 0))
    hbm = pl.BlockSpec(memory_space=pl.ANY)
    outs = pl.pallas_call(
        body, grid=(nb, nc), name="mixer_fwd",
        in_specs=[blk(PROJ_W), _full((4, D_XBC)), _full((1, D_XBC)), _full((1, 128)), _full((1, 128)),
                  _full((1, D_SSM)), _full((1, D_SSM)), _full((4, 128, 128)), _full((1, D_POOL)),
                  _full((128, D_SSM)), _full((D_SSM, 128)), _full((CHUNK, CHUNK))] + [hbm] * ns,
        out_specs=[blk(D_MIX), blk(D_SSM), blk(D_XBC),
                   pl.BlockSpec((1, D_SSM, D_STATE), lambda b, c: (b * nc + c, 0, 0))] + [hbm] * ns,
        out_shape=[jax.ShapeDtypeStruct((m, D_MIX), MXU_DTYPE), jax.ShapeDtypeStruct((m, D_SSM), F32),
                   jax.ShapeDtypeStruct((m, D_XBC), F32), jax.ShapeDtypeStruct((m // CHUNK, D_SSM, D_STATE), F32)]
        + _Gather.out_shapes(shards, by_cols),
        scratch_shapes=[pltpu.VMEM((HALO, D_XBC), F32), pltpu.VMEM((HALO, D_POOL), F32),
                        pltpu.VMEM((D_SSM, D_STATE), F32)] + _Gather.scratch(ns),
        compiler_params=_params("arbitrary", "arbitrary"),
    )(proj, cw, cb, dt_bias, a_log, d_x, nw, pool_w, pool_scale, e_mat, et_mat, tril, *shards)
    return outs[0], outs[1], outs[2], outs[3], outs[4:]


def _mixer_bwd(proj, dy, ypre, conv_pre, states, cw, dt_bias, a_log, d_x, nw, pool_w, pool_scale, nb, chip_sums):
    m = proj.shape[0]
    nc = m // nb // CHUNK
    e_mat, et_mat, tril, triu = _ssd_constants()
    hb = CHUNK // HALO
    ns = len(chip_sums)

    def body(p_ref, halo_ref, dy_ref, ypre_ref, pre_ref, st_ref, cw_ref, dtb_ref, alog_ref, dx_ref, nw_ref, pw_ref,
             ps_ref, e_ref, et_ref, tril_ref, triu_ref, *rest):
        cs_refs = rest[:ns]
        dp_ref, gcw_ref, gcb_ref, gdtb_ref, galog_ref, gd_ref, gnw_ref, gpw_ref, gps_ref = rest[ns:ns + 9]
        part_refs, (ds_carry, dpre_next, dq_next) = rest[ns + 9:2 * ns + 9], rest[2 * ns + 9:2 * ns + 12]
        exchange = _ChipExchange(cs_refs, part_refs, [False] * ns, *rest[2 * ns + 12:])
        b = pl.program_id(0)
        cc = pl.program_id(1)
        c = nc - 1 - cc

        @pl.when((b == 0) & (cc == 0))
        def _():
            exchange.start()
            for r in (gcw_ref, gcb_ref, gdtb_ref, galog_ref, gd_ref, gnw_ref, gpw_ref, gps_ref):
                r[...] = jnp.zeros_like(r)

        @pl.when(cc == 0)
        def _():
            ds_carry[...] = jnp.zeros_like(ds_carry)
            dpre_next[...] = jnp.zeros_like(dpre_next)
            dq_next[...] = jnp.zeros_like(dq_next)

        valid = (c > 0) | (lax.broadcasted_iota(jnp.int32, (CHUNK, 1), 0) >= PAD_ROWS)
        first = c > 0

        u = p_ref[:, 0:D_POOL]
        u_halo = jnp.where(first, halo_ref[...], 0.0)
        inv_cnt, lane = _pool_inv_count(c)
        pooled = _pool_window_sums(jnp.concatenate([u_halo, u], axis=0), lane) * inv_cnt - u
        dyp = dy_ref[:, 0:D_POOL]
        ps = ps_ref[...]
        dmixed = dyp * ps
        mixed, dpooled = [], []
        for g in range(len(POOL_WINDOWS)):
            gsl = slice(g * 128, (g + 1) * 128)
            pw = pw_ref[g]
            mixed.append(_dot(pooled[:, gsl], pw))
            dpooled.append(_dot_nt(dmixed[:, gsl], pw))
            gpw_ref[g] += _dot_tn(pooled[:, gsl], dmixed[:, gsl])
        gps_ref[...] += jnp.sum(dyp * jnp.concatenate(mixed, axis=1), axis=0, keepdims=True)
        dpooled = jnp.concatenate(dpooled, axis=1)
        dq = dpooled * inv_cnt
        du = _pool_window_sums_ahead(jnp.concatenate([dq, dq_next[...]], axis=0), lane) - dpooled
        dq_next[...] = dq[0:HALO, :]
        dp_ref[:, 0:D_POOL] = _mx(du)

        yp = ypre_ref[...]
        z = p_ref[:, OFF_Z:OFF_Z + D_SSM]
        sz, dsz = _silu_and_grad(z)
        yz = yp * sz
        do = dy_ref[:, D_POOL:]
        nw_row = nw_ref[...]
        dyz = []
        gnw = []
        for g in range(N_GROUPS):
            gs = slice(g * GROUP_CH, (g + 1) * GROUP_CH)
            r = lax.rsqrt(jnp.mean(yz[:, gs] * yz[:, gs], axis=-1, keepdims=True) + EPS)
            n = yz[:, gs] * r
            gnw.append(jnp.sum(do[:, gs] * n, axis=0, keepdims=True))
            dn = do[:, gs] * nw_row[:, gs]
            dyz.append(r * (dn - n * jnp.mean(dn * n, axis=-1, keepdims=True)))
        gnw_ref[...] += jnp.concatenate(gnw, axis=1)
        dyz = jnp.concatenate(dyz, axis=1)
        dp_ref[:, OFF_Z:OFF_Z + D_SSM] = _mx(dyz * yp * dsz)
        dyp_ssm = dyz * sz

        xc, dsilu = _silu_and_grad(pre_ref[...])
        dtr = p_ref[:, OFF_DT:OFF_DT + 128]
        dt, a_row, a_col, dt_pre, head = _dt_and_cumsum(dtr, dtb_ref[...], alog_ref[...], valid, tril_ref[...])
        dxs, dbm, dcm, ddt, da, dd, ds_prev = _ssd_chunk_bwd(
            xc[:, 0:D_SSM], xc[:, D_SSM:D_SSM + 512], xc[:, D_SSM + 512:], dt, a_row, a_col, st_ref[0],
            ds_carry[...], dyp_ssm, dx_ref[...], e_ref[...], et_ref[...], triu_ref[...])
        ds_carry[...] = ds_prev
        gd_ref[...] += dd
        galog_ref[...] += da * a_row
        ddtr = jnp.where(valid & head, ddt * _sigmoid(dt_pre), 0.0)
        gdtb_ref[...] += jnp.sum(ddtr, axis=0, keepdims=True)
        dp_ref[:, OFF_DT:OFF_DT + 128] = _mx(ddtr)

        dpre = jnp.concatenate([dxs, dbm, dcm], axis=1) * dsilu
        gcb_ref[...] += jnp.sum(dpre, axis=0, keepdims=True)
        dext = jnp.concatenate([dpre, dpre_next[...]], axis=0)
        dpre_next[...] = dpre[0:HALO, :]
        ups = [_shift_up(dext, 3 - k) for k in range(4)]
        xbc = p_ref[:, OFF_X:OFF_X + D_XBC]
        gcw_ref[...] += jnp.concatenate([jnp.sum(xbc * ups[k], axis=0, keepdims=True) for k in range(4)], axis=0)
        cw = cw_ref[...]
        dp_ref[:, OFF_X:OFF_X + D_XBC] = _mx(cw[3:4, :] * ups[3] + cw[2:3, :] * ups[2]
                                             + cw[1:2, :] * ups[1] + cw[0:1, :] * ups[0])

        @pl.when((b == nb - 1) & (cc == nc - 1))
        def _():
            exchange.finish()

    blk = lambda w: pl.BlockSpec((CHUNK, w), lambda b, cc: (_chunk_block(b, nc - 1 - cc, nb, nc), 0))
    halo = pl.BlockSpec((HALO, D_POOL),
                        lambda b, cc: (_chunk_block(b, jnp.maximum(nc - 2 - cc, 0), nb, nc) * hb + hb - 1, 0))
    hbm = pl.BlockSpec(memory_space=pl.ANY)
    outs = pl.pallas_call(
        body, grid=(nb, nc), name="mixer_bwd",
        in_specs=[blk(PROJ_W), halo, blk(D_MIX), blk(D_SSM), blk(D_XBC),
                  pl.BlockSpec((1, D_SSM, D_STATE), lambda b, cc: (b * nc + nc - 1 - cc, 0, 0)),
                  _full((4, D_XBC)), _full((1, 128)), _full((1, 128)), _full((1, D_SSM)),
                  _full((1, D_SSM)), _full((4, 128, 128)), _full((1, D_POOL)),
                  _full((128, D_SSM)), _full((D_SSM, 128)), _full((CHUNK, CHUNK)), _full((CHUNK, CHUNK))] + [hbm] * ns,
        out_specs=[blk(PROJ_W), _full((4, D_XBC)), _full((1, D_XBC)), _full((1, 128)), _full((1, 128)), _full((1, 128)),
                   _full((1, D_SSM)), _full((4, 128, 128)), _full((1, D_POOL))] + [hbm] * ns,
        out_shape=[jax.ShapeDtypeStruct((m, PROJ_W), MXU_DTYPE), jax.ShapeDtypeStruct((4, D_XBC), F32),
                   jax.ShapeDtypeStruct((1, D_XBC), F32), jax.ShapeDtypeStruct((1, 128), F32),
                   jax.ShapeDtypeStruct((1, 128), F32), jax.ShapeDtypeStruct((1, 128), F32),
                   jax.ShapeDtypeStruct((1, D_SSM), F32), jax.ShapeDtypeStruct((4, 128, 128), F32),
                   jax.ShapeDtypeStruct((1, D_POOL), F32)] + _ChipExchange.out_shapes(chip_sums, [False] * ns),
        scratch_shapes=[pltpu.VMEM((D_SSM, D_STATE), F32), pltpu.VMEM((HALO, D_XBC), F32),
                        pltpu.VMEM((HALO, D_POOL), F32)] + _ChipExchange.scratch(ns),
        compiler_params=_params("arbitrary", "arbitrary"),
    )(proj, proj, dy, ypre, conv_pre, states, cw, dt_bias, a_log, d_x, nw, pool_w, pool_scale, e_mat, et_mat, tril, triu,
      *chip_sums)
    return outs[:9], outs[9:]


MESH_IDS = pl.DeviceIdType.MESH
_HBM = pl.BlockSpec(memory_space=pltpu.HBM)


def _coords():
    return lax.axis_index("x"), lax.axis_index("y"), lax.axis_index("c")


def _other_chips(x, y):
    return [(1 - x, y), (x, 1 - y), (1 - x, 1 - y)]


class _Gather:
    def __init__(self, ins, outs, by_cols, send_sems, recv_sems, local_sems):
        self.ins, self.outs, self.by_cols, self.n = ins, outs, by_cols, len(ins)
        self.send_sems, self.recv_sems, self.local_sems = send_sems, recv_sems, local_sems
        self.x, self.y, self.c = _coords()
        self.me, self.sibling = (self.x, self.y, self.c), (self.x, self.y, 1 - self.c)
        self.chips = _other_chips(self.x, self.y)

    @staticmethod
    def scratch(n):
        return [pltpu.SemaphoreType.DMA((7 * n,)), pltpu.SemaphoreType.DMA((7 * n,)), pltpu.SemaphoreType.DMA((n,))]

    @staticmethod
    def out_shapes(shards, by_cols):
        return [jax.ShapeDtypeStruct((s.shape[0], N_DEV * s.shape[1]) if cols else (N_DEV,) + s.shape, s.dtype)
                for s, cols in zip(shards, by_cols)]

    def _block(self, t, device):
        idx = 4 * device[0] + 2 * device[1] + device[2]
        if not self.by_cols[t]:
            return self.outs[t].at[idx]
        w = self.ins[t].shape[1]
        return self.outs[t].at[:, pl.ds(pl.multiple_of(idx * w, w), w)]

    def _copy(self, t, k, block, to, own=False):
        dst = self._block(t, block)
        return pltpu.make_async_remote_copy(
            src_ref=self.ins[t] if own else dst, dst_ref=dst, send_sem=self.send_sems.at[t * 7 + k],
            recv_sem=self.recv_sems.at[t * 7 + k], device_id=to, device_id_type=MESH_IDS)

    def _mine(self):
        return [pltpu.make_async_copy(self.ins[t], self._block(t, self.me), self.local_sems.at[t])
                for t in range(self.n)]

    def _first(self):
        cps = []
        for t in range(self.n):
            cps.append(self._copy(t, 0, self.me, self.sibling, own=True))
            cps += [self._copy(t, 1 + j, self.me, (*chip, self.c), own=True) for j, chip in enumerate(self.chips)]
        return cps

    def _passed(self):
        return [self._copy(t, 4 + j, (*chip, self.c), self.sibling)
                for j, chip in enumerate(self.chips) for t in range(self.n)]

    def start(self):
        for cp in self._mine() + self._first():
            cp.start()

    def forward(self):
        for j, chip in enumerate(self.chips):
            for t in range(self.n):
                self._copy(t, 1 + j, (*chip, self.c), self.me).wait_recv()
                self._copy(t, 4 + j, (*chip, self.c), self.sibling).start()

    def finish(self):
        for t in range(self.n):
            self._copy(t, 0, self.sibling, self.me).wait_recv()
            for j, chip in enumerate(self.chips):
                self._copy(t, 4 + j, (*chip, 1 - self.c), self.me).wait_recv()
        for cp in self._first() + self._passed():
            cp.wait_send()
        for cp in self._mine():
            cp.wait()


def _weight_gather(shards):
    n = len(shards)

    def body(*refs):
        g = _Gather(refs[:n], refs[n:2 * n], [False] * n, *refs[2 * n:])
        g.start()
        g.forward()
        g.finish()

    return pl.pallas_call(
        body, name="weight_gather",
        in_specs=[_HBM] * n, out_specs=[_HBM] * n,
        out_shape=_Gather.out_shapes(shards, [False] * n),
        scratch_shapes=_Gather.scratch(n),
    )(*shards)


def _owner_blocks(g, by_cols):
    return (g.shape[0], g.shape[1] // N_DEV) if by_cols else g.shape[2:]


def _grad_exchange_d2d(gs, by_cols, name):
    n = len(gs)

    def body(*refs):
        ins, got = refs[:n], refs[n:2 * n]
        send_sems, recv_sems = refs[2 * n:]
        x, y, c = _coords()

        def src(t, k):
            if not by_cols[t]:
                return ins[t].at[k, 1 - c]
            w = ins[t].shape[1] // N_DEV
            return ins[t].at[:, pl.ds(pl.multiple_of((2 * k + 1 - c) * w, w), w)]

        remote = [pltpu.make_async_remote_copy(
            src_ref=src(t, k), dst_ref=got[t].at[k], send_sem=send_sems.at[t * 4 + k],
            recv_sem=recv_sems.at[t * 4 + k], device_id=(x, y, 1 - c), device_id_type=MESH_IDS)
            for t in range(n) for k in range(4)]
        for cp in remote:
            cp.start()
        for cp in remote:
            cp.wait_recv()
        for cp in remote:
            cp.wait_send()

    return pl.pallas_call(
        body, name=name,
        in_specs=[_HBM] * n, out_specs=[_HBM] * n,
        out_shape=[jax.ShapeDtypeStruct((4,) + _owner_blocks(g, cols), g.dtype) for g, cols in zip(gs, by_cols)],
        scratch_shapes=[pltpu.SemaphoreType.DMA((4 * n,)), pltpu.SemaphoreType.DMA((4 * n,))],
    )(*gs)


def _small_allreduce(pack):
    rows = pack.shape[0]

    def body(p_ref, o_ref, sib_ref, parts_ref, send_sems, recv_sems):
        x, y, c = _coords()
        my_chip = 2 * x + y
        swap = pltpu.make_async_remote_copy(src_ref=p_ref, dst_ref=sib_ref, send_sem=send_sems.at[0],
                                            recv_sem=recv_sems.at[0], device_id=(x, y, 1 - c), device_id_type=MESH_IDS)
        swap.start()
        swap.wait_recv()
        parts_ref[my_chip] = p_ref[...] + sib_ref[...]
        remote = [pltpu.make_async_remote_copy(
            src_ref=parts_ref.at[my_chip], dst_ref=parts_ref.at[my_chip], send_sem=send_sems.at[1 + j],
            recv_sem=recv_sems.at[1 + j], device_id=(cx, cy, c), device_id_type=MESH_IDS)
            for j, (cx, cy) in enumerate(_other_chips(x, y))]
        for cp in remote:
            cp.start()
        for j, (cx, cy) in enumerate(_other_chips(x, y)):
            slot = parts_ref.at[2 * cx + cy]
            pltpu.make_async_remote_copy(src_ref=slot, dst_ref=slot, send_sem=send_sems.at[1 + j],
                                         recv_sem=recv_sems.at[1 + j], device_id=(cx, cy, c),
                                         device_id_type=MESH_IDS).wait_recv()
        o_ref[...] = ((parts_ref[0] + parts_ref[1]) + parts_ref[2]) + parts_ref[3]
        swap.wait_send()
        for cp in remote:
            cp.wait_send()

    vmem = pl.BlockSpec(memory_space=pltpu.VMEM)
    return pl.pallas_call(
        body, name="small_allreduce", in_specs=[vmem], out_specs=vmem,
        out_shape=jax.ShapeDtypeStruct((rows, 128), F32),
        scratch_shapes=[pltpu.VMEM((rows, 128), F32), pltpu.VMEM((4, rows, 128), F32),
                        pltpu.SemaphoreType.DMA((4,)), pltpu.SemaphoreType.DMA((4,))],
    )(pack)


class _ChipExchange:
    def __init__(self, ins, outs, whole, send_sems, recv_sems, local_sems):
        self.ins, self.outs, self.whole, self.n = ins, outs, whole, len(ins)
        self.send_sems, self.recv_sems, self.local_sems = send_sems, recv_sems, local_sems
        self.x, self.y, self.c = _coords()
        self.my_chip = 2 * self.x + self.y
        self.chips = _other_chips(self.x, self.y)

    @staticmethod
    def scratch(n):
        return [pltpu.SemaphoreType.DMA((3 * n,)), pltpu.SemaphoreType.DMA((3 * n,)), pltpu.SemaphoreType.DMA((n,))]

    def _src(self, t, k):
        return self.ins[t] if self.whole[t] else self.ins[t].at[k]

    def _local(self):
        return [pltpu.make_async_copy(self._src(t, self.my_chip), self.outs[t].at[self.my_chip], self.local_sems.at[t])
                for t in range(self.n)]

    def _remote(self):
        return [pltpu.make_async_remote_copy(
            src_ref=self._src(t, 2 * cx + cy), dst_ref=self.outs[t].at[self.my_chip],
            send_sem=self.send_sems.at[t * 3 + j], recv_sem=self.recv_sems.at[t * 3 + j],
            device_id=(cx, cy, self.c), device_id_type=MESH_IDS)
            for t in range(self.n) for j, (cx, cy) in enumerate(self.chips)]

    def start(self):
        for cp in self._remote() + self._local():
            cp.start()

    def finish(self):
        for t in range(self.n):
            for j, (cx, cy) in enumerate(self.chips):
                slot = self.outs[t].at[2 * cx + cy]
                pltpu.make_async_remote_copy(
                    src_ref=slot, dst_ref=slot, send_sem=self.send_sems.at[t * 3 + j],
                    recv_sem=self.recv_sems.at[t * 3 + j], device_id=(cx, cy, self.c),
                    device_id_type=MESH_IDS).wait_recv()
        for cp in self._remote():
            cp.wait_send()
        for cp in self._local():
            cp.wait()

    @staticmethod
    def out_shapes(arrs, whole):
        return [jax.ShapeDtypeStruct(((4,) + a.shape) if w else a.shape, a.dtype) for a, w in zip(arrs, whole)]


def _row_tile(rows, cols, n_arrays):
    budget = 24 * 1024 * 1024
    padded = -(-cols // 128) * 128
    step = 16 if rows % 16 == 0 else 8
    tr = max(step, budget // (n_arrays * 2 * 4 * padded) // step * step)
    while rows % tr:
        tr -= step
    return tr


def _chip_sum(g, by_cols, got, core, name):
    rows, cols = _owner_blocks(g, by_cols)
    tr = _row_tile(rows, cols, 3)

    def body(c_ref, a_ref, b_ref, o_ref):
        o_ref[...] = (a_ref[...] + b_ref[...]).astype(o_ref.dtype)

    own = (pl.BlockSpec((tr, cols), lambda k, i, c: (i, 2 * k + c[0])) if by_cols
           else pl.BlockSpec((None, None, tr, cols), lambda k, i, c: (k, c[0], i, 0)))
    grid_spec = pltpu.PrefetchScalarGridSpec(
        num_scalar_prefetch=1, grid=(4, rows // tr),
        in_specs=[own, pl.BlockSpec((None, tr, cols), lambda k, i, c: (k, i, 0))],
        out_specs=pl.BlockSpec((None, tr, cols), lambda k, i, c: (k, i, 0)))
    return pl.pallas_call(body, grid_spec=grid_spec, name=name,
                          out_shape=jax.ShapeDtypeStruct((4, rows, cols), MXU_DTYPE),
                          compiler_params=_params("arbitrary", "arbitrary"))(core, g, got)


def _adamw_math(w, g, m, v):
    m2 = ADAM_B1 * m + (1.0 - ADAM_B1) * g
    v2 = ADAM_B2 * v + (1.0 - ADAM_B2) * (g * g)
    m_hat = m2 / (1.0 - ADAM_B1 ** ADAM_STEP)
    v_hat = v2 / (1.0 - ADAM_B2 ** ADAM_STEP)
    delta = -ADAM_LR * (m_hat / (jnp.sqrt(v_hat) + ADAM_EPS) + ADAM_WD * w)
    return delta, m2, v2


def _adamw(parts, w, m, v, name):
    rows, cols = w.shape
    tr = _row_tile(rows, cols, 11)

    def body(p_ref, w_ref, m_ref, v_ref, g_ref, d_ref, m2_ref, v2_ref):
        part = lambda k: p_ref[k].astype(F32)
        g = ((part(0) + part(1)) + part(2)) + part(3)
        d, m2, v2 = _adamw_math(w_ref[...], g, m_ref[...], v_ref[...])
        g_ref[...] = g
        d_ref[...] = d
        m2_ref[...] = m2
        v2_ref[...] = v2

    blk = pl.BlockSpec((tr, cols), lambda i: (i, 0))
    out = jax.ShapeDtypeStruct((rows, cols), F32)
    return pl.pallas_call(body, grid=(rows // tr,), name=name,
                          in_specs=[pl.BlockSpec((4, tr, cols), lambda i: (0, i, 0)), blk, blk, blk],
                          out_specs=[blk] * 4, out_shape=[out] * 4,
                          compiler_params=_params("arbitrary"))(parts, w, m, v)


def _adamw_small(gs, ws, ms, vs):
    n = len(ws)

    def body(*refs):
        g_refs, w_refs, m_refs, v_refs = (refs[k * n:(k + 1) * n] for k in range(4))
        d_refs, m2_refs, v2_refs = (refs[(4 + k) * n:(5 + k) * n] for k in range(3))
        for t in range(n):
            d, m2, v2 = _adamw_math(w_refs[t][...], g_refs[t][...], m_refs[t][...], v_refs[t][...])
            d_refs[t][...] = d
            m2_refs[t][...] = m2
            v2_refs[t][...] = v2

    outs = pl.pallas_call(body, name="adamw_small",
                          out_shape=[jax.ShapeDtypeStruct(w.shape, F32) for w in ws] * 3)(*gs, *ws, *ms, *vs)
    return outs[:n], outs[n:2 * n], outs[2 * n:]


_PACK_TILE = 8 * 128


def _pack(arrays):
    rows = []
    for a in arrays:
        flat = a.astype(F32).reshape(-1)
        rows.append(jnp.pad(flat, (0, -flat.shape[0] % _PACK_TILE)).reshape(-1, 128))
    return jnp.concatenate(rows, axis=0)


def _unpack(pack, shapes):
    out, r = [], 0
    for s in shapes:
        n = int(np.prod(s))
        out.append(pack[r:r + -(-n // 128)].reshape(-1)[:n].reshape(s))
        r += -(-n // _PACK_TILE) * 8
    return out


def _pad128(v):
    v = v.reshape(1, -1).astype(F32)
    return jnp.pad(v, ((0, 0), (0, 128 - v.shape[1])))


_WEIGHTS = ["meta", "norm_mix_w", "w_in", "pool_w", "pool_scale", "conv_w", "conv_b", "dt_bias", "a_log", "d_skip",
            "ssm_norm_w", "w_out", "norm_ffn_w", "w_ff1", "w_ff2", "norm_f_w"]
_BIG = ["w_in", "w_out", "w_ff1", "w_ff2"]
_SMALL = [n for n in _WEIGHTS if n not in _BIG]


def kernel(x, meta, norm_mix_w, w_in, pool_w, pool_scale, conv_w, conv_b, dt_bias, a_log, d_skip, ssm_norm_w, w_out, norm_ffn_w, w_ff1, w_ff2, norm_f_w, loss_target, m_meta, m_norm_mix_w, m_w_in, m_pool_w, m_pool_scale, m_conv_w, m_conv_b, m_dt_bias, m_a_log, m_d_skip, m_ssm_norm_w, m_w_out, m_norm_ffn_w, m_w_ff1, m_w_ff2, m_norm_f_w, v_meta, v_norm_mix_w, v_w_in, v_pool_w, v_pool_scale, v_conv_w, v_conv_b, v_dt_bias, v_a_log, v_d_skip, v_ssm_norm_w, v_w_out, v_norm_ffn_w, v_w_ff1, v_w_ff2, v_norm_f_w):
    wts = dict(meta=meta, norm_mix_w=norm_mix_w, w_in=w_in, pool_w=pool_w, pool_scale=pool_scale, conv_w=conv_w,
               conv_b=conv_b, dt_bias=dt_bias, a_log=a_log, d_skip=d_skip, ssm_norm_w=ssm_norm_w, w_out=w_out,
               norm_ffn_w=norm_ffn_w, w_ff1=w_ff1, w_ff2=w_ff2, norm_f_w=norm_f_w)
    mom1 = dict(zip(_WEIGHTS, (m_meta, m_norm_mix_w, m_w_in, m_pool_w, m_pool_scale, m_conv_w, m_conv_b, m_dt_bias,
                               m_a_log, m_d_skip, m_ssm_norm_w, m_w_out, m_norm_ffn_w, m_w_ff1, m_w_ff2, m_norm_f_w)))
    mom2 = dict(zip(_WEIGHTS, (v_meta, v_norm_mix_w, v_w_in, v_pool_w, v_pool_scale, v_conv_w, v_conv_b, v_dt_bias,
                               v_a_log, v_d_skip, v_ssm_norm_w, v_w_out, v_norm_ffn_w, v_w_ff1, v_w_ff2, v_norm_f_w)))
    xi, yi, ci = _coords()
    dev = 4 * xi + 2 * yi + ci
    win_cols = w_in.shape[-1]
    cw_cols = conv_w.shape[-1]

    nb, seq, _ = x.shape
    core = jnp.reshape(ci, (1,)).astype(jnp.int32)
    owners = lambda a: a.reshape((4, 2) + a.shape[1:])

    lead_pack = jnp.zeros((N_META, 512), F32)
    lead_pack = lead_pack.at[:, :128].set(meta).at[:4, 128:128 + cw_cols].set(conv_w[0])
    g_win, g_lead = _weight_gather([_mx(w_in[0]), lead_pack])
    win_full = _assemble_bands(g_win, PROJ_W)
    meta_full = jnp.transpose(g_lead[:, :, :128], (1, 0, 2)).reshape(N_META, D_MODEL)
    cw_full = jnp.transpose(g_lead[:, :4, 128:128 + cw_cols], (1, 0, 2)).reshape(4, D_XBC)

    lead = jnp.concatenate([jnp.zeros((PAD_ROWS, D_MODEL), F32), meta_full] * nb, axis=0)
    x_rows = x.reshape(nb * seq, D_MODEL)
    tgt_rows = loss_target.reshape(nb * seq, D_MODEL)
    dt_bias_p, a_log_p = _pad128(dt_bias), _pad128(a_log)
    d_x = jnp.repeat(d_skip.reshape(1, N_HEADS).astype(F32), HEAD_DIM, axis=1)
    norm_f_row = norm_f_w.reshape(1, D_MODEL)

    hn1, proj = _in_proj(x_rows, lead, norm_mix_w, win_full)
    late_cols = [False, True, False]
    y, ypre, conv_pre, states, (g_wout, wff1_full, g_wff2) = _mixer_fwd(
        proj, cw_full, conv_b, dt_bias_p, a_log_p, d_x, ssm_norm_w, pool_w[0], pool_scale, nb,
        [_mx(w_out[0]), _mx(w_ff1[0]), _mx(w_ff2[0])], late_cols)
    wout_full = g_wout.reshape(D_MIX, D_MODEL)
    wff2_full = g_wff2.reshape(D_FF, D_MODEL)
    loss, gr_nf, gr_nffn, ff, da, hn2, dh1, dh2, dy = _ffn_fwd_bwd(
        x_rows, lead, y, tgt_rows, wout_full, norm_ffn_w, wff1_full, wff2_full, norm_f_row)
    gr_wff2 = _tn_matmul(ff, dh2, "grad_w_ff2", tka=1024)
    gr_wff1 = _tn_matmul(hn2, da, "grad_w_ff1", tka=512)
    gr_wout = _tn_matmul(y, dh1, "grad_w_out", tka=1024)

    late_parts = [owners(gr_wout.reshape(N_DEV, D_MIX // N_DEV, D_MODEL)), gr_wff1,
                  owners(gr_wff2.reshape(N_DEV, D_FF // N_DEV, D_MODEL))]
    late_got = _grad_exchange_d2d(late_parts, late_cols, "grad_exchange_d2d_late")
    late_sums = [_chip_sum(late_parts[t], late_cols[t], late_got[t], core, "chip_sum_late_%d" % t) for t in range(3)]
    (dproj, gr_cw, gr_cb, gr_dtb, gr_alog, gr_d, gr_nw, gr_pw, gr_ps), late_exchanged = _mixer_bwd(
        proj, dy, ypre, conv_pre, states, cw_full, dt_bias_p, a_log_p, d_x, ssm_norm_w, pool_w[0], pool_scale, nb,
        late_sums)

    win_parts = [owners(_tn_matmul_banded(hn1, dproj, "grad_w_in", win_cols, tka=512))]
    win_got = _grad_exchange_d2d(win_parts, [False], "grad_exchange_d2d_w_in")
    win_sum = _chip_sum(win_parts[0], False, win_got[0], core, "chip_sum_w_in")
    gx_rows, gr_nmix, gr_meta, win_exchanged = _in_proj_bwd(dproj, x_rows, lead, dh1, norm_mix_w, win_full, [win_sum])
    parts = dict(w_in=win_exchanged[0], w_out=late_exchanged[0], w_ff1=late_exchanged[1], w_ff2=late_exchanged[2])

    small_full = dict(meta=gr_meta, norm_mix_w=gr_nmix, pool_w=gr_pw, pool_scale=gr_ps, conv_w=gr_cw, conv_b=gr_cb,
                      dt_bias=gr_dtb[:, :N_HEADS], a_log=gr_alog[:, :N_HEADS], d_skip=gr_d[:, :N_HEADS],
                      ssm_norm_w=gr_nw, norm_ffn_w=gr_nffn, norm_f_w=gr_nf, loss=loss[0:1, 0:1])
    small_names = list(small_full)
    small_sum = _small_allreduce(_pack([small_full[n] for n in small_names]))
    gs = dict(zip(small_names, _unpack(small_sum, [small_full[n].shape for n in small_names])))
    gs["meta"] = lax.dynamic_slice_in_dim(gs["meta"], dev * 128, 128, axis=1)
    gs["conv_w"] = lax.dynamic_slice_in_dim(gs["conv_w"], dev * cw_cols, cw_cols, axis=1)

    res = {}
    for n in _BIG:
        shp = wts[n].shape
        res[n] = [o.reshape(shp) for o in _adamw(parts[n], wts[n][0], mom1[n][0], mom2[n][0], "adamw_" + n)]
    as2d = lambda a: a.reshape(-1, a.shape[-1])
    small_g = [as2d(gs[n].reshape(wts[n].shape)) for n in _SMALL]
    small_out = _adamw_small(small_g, *[[as2d(d[n]) for n in _SMALL] for d in (wts, mom1, mom2)])
    for k, n in enumerate(_SMALL):
        res[n] = [o[k].reshape(wts[n].shape) for o in (small_g,) + tuple(small_out)]

    grad_x = gx_rows.reshape(nb, seq, D_MODEL)
    return (gs["loss"][0, 0], grad_x, *[res[n][0] for n in _WEIGHTS], *[res[n][1] for n in _WEIGHTS],
            *[res[n][2] for n in _WEIGHTS], *[res[n][3] for n in _WEIGHTS])
```

```python
import functools

import numpy as np
import jax
import jax.numpy as jnp
from jax import lax
from jax.experimental import pallas as pl
from jax.experimental.pallas import tpu as pltpu

F32 = jnp.float32
MXU_DTYPE = jnp.bfloat16

D_MODEL = 1024
D_POOL = 512
D_SSM = 1536
D_XBC = 2560
N_HEADS = 24
HEAD_DIM = 64
N_GROUPS = 4
GROUP_CH = D_SSM // N_GROUPS
D_STATE = 128
CHUNK = 128
N_META = 16
LEAD = CHUNK
PAD_ROWS = LEAD - N_META
ROW_TILE = 2 * CHUNK
D_MIX = D_POOL + D_SSM
D_FF = 4096
PROJ_W = 4736
OFF_Z = D_POOL
OFF_X = D_POOL + D_SSM
OFF_DT = OFF_X + D_XBC
D_IN_PROJ = OFF_DT + N_HEADS
POOL_WINDOWS = (2, 4, 8, 16)
HALO = 16
EPS = 1e-5
N_DEV = 8

ADAM_LR, ADAM_B1, ADAM_B2, ADAM_EPS, ADAM_WD, ADAM_STEP = 0.001, 0.9, 0.999, 1e-08, 0.01, 10

VMEM_LIMIT = 60 * 1024 * 1024


def _mx(a):
    return a.astype(MXU_DTYPE)


def _dot(a, b):
    return jnp.dot(_mx(a), _mx(b), preferred_element_type=F32)


def _dot_nt(a, b):
    return lax.dot_general(_mx(a), _mx(b), (((1,), (1,)), ((), ())), preferred_element_type=F32)


def _dot_tn(a, b):
    return lax.dot_general(_mx(a), _mx(b), (((0,), (0,)), ((), ())), preferred_element_type=F32)


def _split3(x):
    hi = x.astype(MXU_DTYPE)
    r = x - hi.astype(F32)
    mid = r.astype(MXU_DTYPE)
    lo = (r - mid.astype(F32)).astype(MXU_DTYPE)
    return hi, mid, lo


def _exact_l(c, x):
    hi, mid, lo = _split3(x)
    f = lambda p: jnp.dot(c, p, preferred_element_type=F32)
    return f(hi) + f(mid) + f(lo)


def _exact_r(x, c):
    hi, mid, lo = _split3(x)
    f = lambda p: jnp.dot(p, c, preferred_element_type=F32)
    return f(hi) + f(mid) + f(lo)


def _contract(x, c):
    hi = x.astype(MXU_DTYPE)
    lo = (x - hi.astype(F32)).astype(MXU_DTYPE)
    return jnp.dot(hi, c, preferred_element_type=F32) + jnp.dot(lo, c, preferred_element_type=F32)


def _sigmoid(x):
    return jax.nn.sigmoid(x)


def _softplus(x):
    return jnp.maximum(x, 0.0) + jnp.log1p(jnp.exp(-jnp.abs(x)))


def _silu_and_grad(x):
    s = _sigmoid(x)
    return x * s, s * (1.0 + x * (1.0 - s))


def _shift_up(ext, s):
    if s == 0:
        return ext[:CHUNK, :]
    return pltpu.roll(ext, ext.shape[0] - s, 0)[:CHUNK, :]


def _by_pool_group(lane, a2, a4, a8, a16):
    return jnp.where(lane < 128, a2, jnp.where(lane < 256, a4, jnp.where(lane < 384, a8, a16)))


def _pool_inv_count(chunk_idx):
    row = lax.broadcasted_iota(jnp.int32, (CHUNK, D_POOL), 0)
    lane = lax.broadcasted_iota(jnp.int32, (CHUNK, D_POOL), 1)
    pos1 = jnp.maximum(chunk_idx * CHUNK + row - (PAD_ROWS - 1), 1)
    w = _by_pool_group(lane, 2, 4, 8, 16)
    return 1.0 / jnp.minimum(pos1, w).astype(F32), lane


def _pool_window_sums(u_ext, lane):
    s2 = u_ext + pltpu.roll(u_ext, 1, 0)
    s4 = s2 + pltpu.roll(s2, 2, 0)
    s8 = s4 + pltpu.roll(s4, 4, 0)
    s16 = s8 + pltpu.roll(s8, 8, 0)
    return _by_pool_group(lane, s2[HALO:], s4[HALO:], s8[HALO:], s16[HALO:])


def _pool_window_sums_ahead(q_ext, lane):
    n = q_ext.shape[0]
    r2 = q_ext + pltpu.roll(q_ext, n - 1, 0)
    r4 = r2 + pltpu.roll(r2, n - 2, 0)
    r8 = r4 + pltpu.roll(r4, n - 4, 0)
    r16 = r8 + pltpu.roll(r8, n - 8, 0)
    return _by_pool_group(lane, r2[:CHUNK], r4[:CHUNK], r8[:CHUNK], r16[:CHUNK])


def _conv_pre(ext, xbc, cw, cb):
    s1 = pltpu.roll(ext, 1, 0)
    near = cw[3:4, :] * xbc + cw[2:3, :] * s1[HALO:, :]
    far = cw[1:2, :] * ext + cw[0:1, :] * s1
    return cb + near + pltpu.roll(far, 2, 0)[HALO:, :]


def _dt_and_cumsum(dtr, dt_bias, a_log, valid, tril):
    lane = lax.broadcasted_iota(jnp.int32, (CHUNK, 128), 1)
    head = lane < N_HEADS
    pre = dtr + dt_bias
    dt = jnp.where(valid & head, _softplus(pre), 0.0)
    a_row = jnp.where(head[0:1, :], -jnp.exp(a_log), 0.0)
    a_col = _exact_l(tril, dt * a_row)
    return dt, a_row, a_col, pre, head


def _decay(a_col, a_row_t, h, causal):
    seg = a_col[:, h:h + 1] - a_row_t[h:h + 1, :]
    return jnp.where(causal, jnp.exp(jnp.minimum(seg, 0.0)), 0.0)


def _ssd_chunk_fwd(xs, bm, cm, dt, a_col, s_prev, d_x, e_mat, et_f32):
    lane = lax.broadcasted_iota(jnp.int32, (CHUNK, 128), 1)
    rowi = lax.broadcasted_iota(jnp.int32, (CHUNK, CHUNK), 0)
    coli = lax.broadcasted_iota(jnp.int32, (CHUNK, CHUNK), 1)
    causal = rowi >= coli
    a_row_t = a_col.T
    ax = _exact_r(a_col, e_mat)
    dtx = _exact_r(dt, e_mat)
    xdt = xs * dtx
    ax_last = ax[CHUNK - 1:CHUNK, :]
    e_a = jnp.exp(ax)
    w_end = xdt * jnp.exp(ax_last - ax)
    cd_col = jnp.exp(jnp.sum(et_f32 * a_col[CHUNK - 1:CHUNK, :], axis=1, keepdims=True))
    ys, s_new = [], []
    for g in range(N_GROUPS):
        gs = slice(g * GROUP_CH, (g + 1) * GROUP_CH)
        bg = bm[:, g * D_STATE:(g + 1) * D_STATE]
        cg = cm[:, g * D_STATE:(g + 1) * D_STATE]
        sg = s_prev[gs, :]
        cb = _dot_nt(cg, bg)
        y_off = _dot_nt(cg, sg) * e_a[:, gs]
        s_new.append(sg * cd_col[gs, :] + _dot_tn(w_end[:, gs], bg))
        for pr in range(3):
            c0 = g * GROUP_CH + pr * 128
            xdt_p = xdt[:, c0:c0 + 128]
            h0 = g * 6 + pr * 2
            y0 = _dot(cb * _decay(a_col, a_row_t, h0, causal), xdt_p)
            y1 = _dot(cb * _decay(a_col, a_row_t, h0 + 1, causal), xdt_p)
            ys.append(jnp.where(lane < HEAD_DIM, y0, y1) + y_off[:, pr * 128:(pr + 1) * 128])
    y = jnp.concatenate(ys, axis=1) + d_x * xs
    return y, jnp.concatenate(s_new, axis=0)


def _ssd_chunk_bwd(xs, bm, cm, dt, a_row, a_col, s_prev, ds_new, dy, d_x, e_mat, et_mat, et_f32, triu):
    lane = lax.broadcasted_iota(jnp.int32, (CHUNK, 128), 1)
    sub = lax.broadcasted_iota(jnp.int32, (CHUNK, 128), 0)
    rowi = lax.broadcasted_iota(jnp.int32, (CHUNK, CHUNK), 0)
    coli = lax.broadcasted_iota(jnp.int32, (CHUNK, CHUNK), 1)
    causal = rowi >= coli
    a_row_t = a_col.T
    ax = _exact_r(a_col, e_mat)
    dtx = _exact_r(dt, e_mat)
    xdt = xs * dtx
    ax_last = ax[CHUNK - 1:CHUNK, :]
    e_a = jnp.exp(ax)
    dte = jnp.exp(ax_last - ax)
    w_end = xdt * dte
    cd_col = jnp.exp(jnp.sum(et_f32 * a_col[CHUNK - 1:CHUNK, :], axis=1, keepdims=True))
    dye = dy * e_a

    dxdt, zc, ww_all, dbs, dcs, dsp, t1s = [], [], [], [], [], [], []
    zcol = jnp.zeros((CHUNK, 128), F32)
    zrow = jnp.zeros((128, CHUNK), F32)
    for g in range(N_GROUPS):
        gs = slice(g * GROUP_CH, (g + 1) * GROUP_CH)
        bg = bm[:, g * D_STATE:(g + 1) * D_STATE]
        cg = cm[:, g * D_STATE:(g + 1) * D_STATE]
        sg = s_prev[gs, :]
        dsg = ds_new[gs, :]
        cb = _dot_nt(cg, bg)
        cs = _dot_nt(cg, sg)
        dcg = _dot(dye[:, gs], sg)
        dsp.append(dsg * cd_col[gs, :] + _dot_tn(dye[:, gs], cg))
        dwg = _dot_nt(bg, dsg)
        dbg = _dot(w_end[:, gs], dsg)
        ww = dwg * w_end[:, gs]
        ww_all.append(ww)
        zc.append(dye[:, gs] * cs - ww)
        t1s.append(jnp.sum(dsg * sg, axis=1, keepdims=True) * cd_col[gs, :])
        dxdt_g = dwg * dte[:, gs]
        dcb = jnp.zeros((CHUNK, CHUNK), F32)
        pairs = []
        for pr in range(3):
            c0 = g * GROUP_CH + pr * 128
            xdt_p = xdt[:, c0:c0 + 128]
            dy_p = dy[:, c0:c0 + 128]
            acc = None
            for half in range(2):
                h = g * 6 + pr * 2 + half
                ld = _decay(a_col, a_row_t, h, causal)
                gm = cb * ld
                dym = jnp.where((lane < HEAD_DIM) if half == 0 else (lane >= HEAD_DIM), dy_p, 0.0)
                dg = _dot_nt(dym, xdt_p)
                dseg = dg * gm
                dcb = dcb + dg * ld
                t = _dot_tn(gm, dym)
                acc = t if acc is None else acc + t
                zcol = jnp.where(lane == h, jnp.sum(dseg, axis=1, keepdims=True), zcol)
                zrow = jnp.where(sub == h, jnp.sum(dseg, axis=0, keepdims=True), zrow)
            pairs.append(acc)
        dxdt.append(dxdt_g + jnp.concatenate(pairs, axis=1))
        dcs.append(dcg + _dot(dcb, bg))
        dbs.append(dbg + _dot_tn(dcb, cg))
    dxdt = jnp.concatenate(dxdt, axis=1)
    zc = jnp.concatenate(zc, axis=1)
    ww_all = jnp.concatenate(ww_all, axis=1)
    t1 = jnp.concatenate(t1s, axis=0)

    sub8 = lax.broadcasted_iota(jnp.int32, (8, D_SSM), 0)
    col_sums = jnp.where(sub8 == 0, jnp.sum(dy * xs, axis=0, keepdims=True),
                         jnp.where(sub8 == 1, jnp.sum(ww_all, axis=0, keepdims=True), 0.0))
    head_sums = _exact_r(col_sums, et_mat)
    dd = head_sums[0:1, :]
    q_row = head_sums[1:2, :] + jnp.sum(et_f32 * t1, axis=0, keepdims=True)
    da_col = _contract(zc, et_mat) + zcol - zrow.T
    da_col = da_col + jnp.where(sub == CHUNK - 1, q_row, 0.0)
    ddt = _contract(dxdt * xs, et_mat)
    dxs = dxdt * dtx + d_x * dy
    rc = _exact_l(triu, da_col)
    ddt = ddt + rc * a_row
    da = jnp.sum(rc * dt, axis=0, keepdims=True)
    return (dxs, jnp.concatenate(dbs, axis=1), jnp.concatenate(dcs, axis=1), ddt, da, dd,
            jnp.concatenate(dsp, axis=0))


def _ssd_constants():
    iota = lambda shape, d: lax.broadcasted_iota(jnp.int32, shape, d)
    e = iota((128, D_SSM), 0) == iota((128, D_SSM), 1) // HEAD_DIM
    et = iota((D_SSM, 128), 1) == iota((D_SSM, 128), 0) // HEAD_DIM
    tril = iota((CHUNK, CHUNK), 1) <= iota((CHUNK, CHUNK), 0)
    triu = iota((CHUNK, CHUNK), 1) >= iota((CHUNK, CHUNK), 0)
    return (e.astype(MXU_DTYPE), et.astype(MXU_DTYPE), et.astype(F32), tril.astype(MXU_DTYPE), triu.astype(MXU_DTYPE))


def _full(shape):
    nd = len(shape)
    return pl.BlockSpec(shape, lambda *_: (0,) * nd)


def _params(*sem):
    return pltpu.CompilerParams(dimension_semantics=sem, vmem_limit_bytes=VMEM_LIMIT)


def _token_tiles(width, n_tok_tiles):
    return pl.BlockSpec((ROW_TILE, width), lambda i: (jnp.minimum(i, n_tok_tiles - 1), 0))


def _in_proj(x, lead, w1, win):
    nt = x.shape[0] // ROW_TILE
    m = x.shape[0] + ROW_TILE
    tm = ROW_TILE

    def body(x_ref, lead_ref, w1_ref, win_hbm, hn_ref, proj_ref, win_v, sem):
        i = pl.program_id(0)

        @pl.when(i == 0)
        def _():
            cp = pltpu.make_async_copy(win_hbm, win_v, sem)
            cp.start()
            cp.wait()

        x = jnp.where(i == nt, lead_ref[...], x_ref[...])
        r = lax.rsqrt(jnp.mean(x * x, axis=-1, keepdims=True) + EPS)
        hn = _mx(x * r * w1_ref[...])
        hn_ref[...] = hn
        for j in range(0, PROJ_W, 512):
            w = min(512, PROJ_W - j)
            proj_ref[:, j:j + w] = jnp.dot(hn, win_v[:, j:j + w], preferred_element_type=F32)

    return pl.pallas_call(
        body, grid=(m // tm,), name="in_proj",
        in_specs=[_token_tiles(D_MODEL, nt), _full((ROW_TILE, D_MODEL)), _full((1, D_MODEL)),
                  pl.BlockSpec(memory_space=pl.ANY)],
        out_specs=[pl.BlockSpec((tm, D_MODEL), lambda i: (i, 0)), pl.BlockSpec((tm, PROJ_W), lambda i: (i, 0))],
        out_shape=[jax.ShapeDtypeStruct((m, D_MODEL), MXU_DTYPE), jax.ShapeDtypeStruct((m, PROJ_W), F32)],
        scratch_shapes=[pltpu.VMEM((D_MODEL, PROJ_W), MXU_DTYPE), pltpu.SemaphoreType.DMA],
        compiler_params=_params("arbitrary"),
    )(x, lead, w1, win)


def _ffn_fwd_bwd(x, lead, y, tgt, wout, w2n, wff1, wff2, wfn):
    nt = x.shape[0] // ROW_TILE
    m = x.shape[0] + ROW_TILE
    tm = ROW_TILE
    nj = D_FF // 1024

    def body(x_ref, lead_ref, y_ref, tgt_ref, w2n_ref, wfn_ref, wout_hbm, wff1_hbm, wff2_hbm,
             loss_ref, gwf_ref, gw2_ref, ff_ref, da_ref, hn2_ref, dh1_ref, dh2_ref, dy_ref,
             wout_v, wff1_v, wff2_v, a_s, sems):
        i = pl.program_id(0)
        hp = jnp.where(i == nt, lead_ref[...], x_ref[...])

        @pl.when(i == 0)
        def _():
            cps = [pltpu.make_async_copy(s, d, sems.at[k])
                   for k, (s, d) in enumerate(((wout_hbm, wout_v), (wff1_hbm, wff1_v), (wff2_hbm, wff2_v)))]
            for cp in cps:
                cp.start()
            for cp in cps:
                cp.wait()
            loss_ref[...] = jnp.zeros_like(loss_ref)
            gwf_ref[...] = jnp.zeros_like(gwf_ref)
            gw2_ref[...] = jnp.zeros_like(gw2_ref)

        h1 = hp + jnp.dot(y_ref[...], wout_v[...], preferred_element_type=F32)
        r2 = lax.rsqrt(jnp.mean(h1 * h1, axis=-1, keepdims=True) + EPS)
        n2 = h1 * r2
        w2n_row = w2n_ref[...]
        hn2 = _mx(n2 * w2n_row)
        hn2_ref[...] = hn2
        h2 = h1
        for j in range(nj):
            js = slice(j * 1024, (j + 1) * 1024)
            a = jnp.dot(hn2, wff1_v[:, js], preferred_element_type=F32)
            a_s[:, js] = a
            ra = jnp.maximum(a, 0.0)
            ff = _mx(ra * ra)
            ff_ref[:, js] = ff
            h2 = h2 + jnp.dot(ff, wff2_v[js, :], preferred_element_type=F32)

        r3 = lax.rsqrt(jnp.mean(h2 * h2, axis=-1, keepdims=True) + EPS)
        n3 = h2 * r3
        wf_row = wfn_ref[...]
        err = n3 * wf_row - tgt_ref[...]
        tokf = (i < nt).astype(F32)
        loss_ref[...] += 0.5 * jnp.sum(jnp.mean(err * err, axis=-1, keepdims=True) * tokf)
        dout = err * (tokf / D_MODEL)
        gwf_ref[...] += jnp.sum(dout * n3, axis=0, keepdims=True)
        dn3 = dout * wf_row
        dh2 = r3 * (dn3 - n3 * jnp.mean(dn3 * n3, axis=-1, keepdims=True))
        dh2m = _mx(dh2)
        dh2_ref[...] = dh2m

        dhn2 = jnp.zeros((tm, D_MODEL), F32)
        for j in range(nj):
            js = slice(j * 1024, (j + 1) * 1024)
            dff = lax.dot_general(dh2m, wff2_v[js, :], (((1,), (1,)), ((), ())), preferred_element_type=F32)
            da = _mx(dff * (2.0 * jnp.maximum(a_s[:, js], 0.0)))
            da_ref[:, js] = da
            dhn2 = dhn2 + lax.dot_general(da, wff1_v[:, js], (((1,), (1,)), ((), ())), preferred_element_type=F32)
        gw2_ref[...] += jnp.sum(dhn2 * n2, axis=0, keepdims=True)
        dn2 = dhn2 * w2n_row
        dh1 = dh2 + r2 * (dn2 - n2 * jnp.mean(dn2 * n2, axis=-1, keepdims=True))
        dh1_ref[...] = dh1
        dy_ref[...] = lax.dot_general(_mx(dh1), wout_v[...], (((1,), (1,)), ((), ())), preferred_element_type=F32)

    rows = lambda w: pl.BlockSpec((tm, w), lambda i: (i, 0))
    hbm = pl.BlockSpec(memory_space=pl.ANY)
    return pl.pallas_call(
        body, grid=(m // tm,), name="ffn_fwd_bwd",
        in_specs=[_token_tiles(D_MODEL, nt), _full((ROW_TILE, D_MODEL)), rows(D_MIX), _token_tiles(D_MODEL, nt),
                  _full((1, D_MODEL)), _full((1, D_MODEL)), hbm, hbm, hbm],
        out_specs=[_full((1, 128)), _full((1, D_MODEL)), _full((1, D_MODEL)), rows(D_FF), rows(D_FF), rows(D_MODEL),
                   rows(D_MODEL), rows(D_MODEL), rows(D_MIX)],
        out_shape=[jax.ShapeDtypeStruct((1, 128), F32), jax.ShapeDtypeStruct((1, D_MODEL), F32),
                   jax.ShapeDtypeStruct((1, D_MODEL), F32), jax.ShapeDtypeStruct((m, D_FF), MXU_DTYPE),
                   jax.ShapeDtypeStruct((m, D_FF), MXU_DTYPE), jax.ShapeDtypeStruct((m, D_MODEL), MXU_DTYPE),
                   jax.ShapeDtypeStruct((m, D_MODEL), F32), jax.ShapeDtypeStruct((m, D_MODEL), MXU_DTYPE),
                   jax.ShapeDtypeStruct((m, D_MIX), F32)],
        scratch_shapes=[pltpu.VMEM((D_MIX, D_MODEL), MXU_DTYPE), pltpu.VMEM((D_MODEL, D_FF), MXU_DTYPE),
                        pltpu.VMEM((D_FF, D_MODEL), MXU_DTYPE), pltpu.VMEM((tm, D_FF), F32),
                        pltpu.SemaphoreType.DMA((3,))],
        compiler_params=_params("arbitrary"),
    )(x, lead, y, tgt, w2n, wfn, wout, wff1, wff2)


def _in_proj_bwd(dproj, x, lead, dh1, w1, win, chip_sums):
    nt = x.shape[0] // ROW_TILE
    m = x.shape[0] + ROW_TILE
    tm = ROW_TILE
    ns = len(chip_sums)

    def body(dp_ref, x_ref, lead_ref, dh1_ref, w1_ref, win_hbm, *rest):
        cs_refs, (gx_ref, gw1_ref, gmeta_ref) = rest[:ns], rest[ns:ns + 3]
        part_refs, (win_v, sem) = rest[ns + 3:2 * ns + 3], rest[2 * ns + 3:2 * ns + 5]
        exchange = _ChipExchange(cs_refs, part_refs, [False] * ns, *rest[2 * ns + 5:])
        i = pl.program_id(0)

        @pl.when(i == 0)
        def _():
            exchange.start()
            cp = pltpu.make_async_copy(win_hbm, win_v, sem)
            cp.start()
            cp.wait()
            gw1_ref[...] = jnp.zeros_like(gw1_ref)
            gmeta_ref[...] = jnp.zeros_like(gmeta_ref)

        dhn = lax.dot_general(dp_ref[...], win_v[...], (((1,), (1,)), ((), ())), preferred_element_type=F32)
        x = jnp.where(i == nt, lead_ref[...], x_ref[...])
        r = lax.rsqrt(jnp.mean(x * x, axis=-1, keepdims=True) + EPS)
        n = x * r
        gw1_ref[...] += jnp.sum(dhn * n, axis=0, keepdims=True)
        dn = dhn * w1_ref[...]
        dh0 = dh1_ref[...] + r * (dn - n * jnp.mean(dn * n, axis=-1, keepdims=True))

        @pl.when(i < nt)
        def _():
            gx_ref[...] = dh0

        @pl.when(i == nt)
        def _():
            gmeta_ref[...] = dh0[PAD_ROWS:LEAD, :] + dh0[LEAD + PAD_ROWS:2 * LEAD, :]
            exchange.finish()

    rows = lambda w: pl.BlockSpec((tm, w), lambda i: (i, 0))
    hbm = pl.BlockSpec(memory_space=pl.ANY)
    outs = pl.pallas_call(
        body, grid=(m // tm,), name="in_proj_bwd",
        in_specs=[rows(PROJ_W), _token_tiles(D_MODEL, nt), _full((ROW_TILE, D_MODEL)), rows(D_MODEL),
                  _full((1, D_MODEL)), hbm] + [hbm] * ns,
        out_specs=[_token_tiles(D_MODEL, nt), _full((1, D_MODEL)), _full((N_META, D_MODEL))] + [hbm] * ns,
        out_shape=[jax.ShapeDtypeStruct(x.shape, F32), jax.ShapeDtypeStruct((1, D_MODEL), F32),
                   jax.ShapeDtypeStruct((N_META, D_MODEL), F32)]
        + _ChipExchange.out_shapes(chip_sums, [False] * ns),
        scratch_shapes=[pltpu.VMEM((D_MODEL, PROJ_W), MXU_DTYPE), pltpu.SemaphoreType.DMA] + _ChipExchange.scratch(ns),
        compiler_params=_params("arbitrary"),
    )(dproj, x, lead, dh1, w1, win, *chip_sums)
    return outs[0], outs[1], outs[2], outs[3:]


def _tn_matmul(a, b, name, tka, tkm=768, tn=512):
    m, ka = a.shape
    nb = b.shape[1]
    n_steps = m // tkm

    def body(a_ref, b_ref, o_ref):
        @pl.when(pl.program_id(1) == 0)
        def _():
            o_ref[...] = jnp.zeros_like(o_ref)

        at = _mx(a_ref[...])
        for j in range(0, nb, tn):
            w = min(tn, nb - j)
            o_ref[:, j:j + w] += lax.dot_general(at, _mx(b_ref[:, j:j + w]), (((0,), (0,)), ((), ())),
                                                 preferred_element_type=F32)

    return pl.pallas_call(
        body, grid=(ka // tka, n_steps), name=name,
        in_specs=[pl.BlockSpec((tkm, tka), lambda i, k: (k, i)), pl.BlockSpec((tkm, nb), lambda i, k: (k, 0))],
        out_specs=pl.BlockSpec((tka, nb), lambda i, k: (i, 0)),
        out_shape=jax.ShapeDtypeStruct((ka, nb), F32),
        compiler_params=_params("arbitrary", "arbitrary"),
    )(a, b)


def _tn_matmul_banded(a, b, name, band, tka, tkm=768, tn=512):
    m, ka = a.shape
    nb = b.shape[1]
    n_steps = m // tkm

    def body(a_ref, b_ref, o_ref, acc):
        k = pl.program_id(1)

        @pl.when(k == 0)
        def _():
            acc[...] = jnp.zeros_like(acc)

        at = _mx(a_ref[...])
        for j in range(0, nb, tn):
            w = min(tn, nb - j)
            acc[:, j:j + w] += lax.dot_general(at, _mx(b_ref[:, j:j + w]), (((0,), (0,)), ((), ())),
                                               preferred_element_type=F32)

        @pl.when(k == n_steps - 1)
        def _():
            for j in range(N_DEV):
                o_ref[j] = acc[:, j * band:(j + 1) * band]

    return pl.pallas_call(
        body, grid=(ka // tka, n_steps), name=name,
        in_specs=[pl.BlockSpec((tkm, tka), lambda i, k: (k, i)), pl.BlockSpec((tkm, nb), lambda i, k: (k, 0))],
        out_specs=pl.BlockSpec((N_DEV, tka, band), lambda i, k: (0, i, 0)),
        out_shape=jax.ShapeDtypeStruct((N_DEV, ka, band), F32),
        scratch_shapes=[pltpu.VMEM((tka, nb), F32)],
        compiler_params=_params("arbitrary", "arbitrary"),
    )(a, b)


def _assemble_bands(g, width):
    n, rows, band = g.shape
    tr = 256

    def body(g_ref, o_ref):
        parts = [g_ref[j] for j in range(n)] + [jnp.zeros((tr, width - n * band), g.dtype)]
        o_ref[...] = jnp.concatenate(parts, axis=1)

    return pl.pallas_call(
        body, grid=(rows // tr,), name="assemble_w_in",
        in_specs=[pl.BlockSpec((n, tr, band), lambda i: (0, i, 0))],
        out_specs=pl.BlockSpec((tr, width), lambda i: (i, 0)),
        out_shape=jax.ShapeDtypeStruct((rows, width), g.dtype),
        compiler_params=_params("arbitrary"),
    )(g)


def _chunk_block(b, c, nb, nc):
    return jnp.where(c == 0, nb * (nc - 1) + b, b * (nc - 1) + c - 1)

def _mixer_fwd(proj, cw, cb, dt_bias, a_log, d_x, nw, pool_w, pool_scale, nb, shards, by_cols):
    m = proj.shape[0]
    nc = m // nb // CHUNK
    n_steps = nb * nc
    ns = len(shards)
    e_mat, _, et_f32, tril, _ = _ssd_constants()

    def body(p_ref, cw_ref, cb_ref, dtb_ref, alog_ref, dx_ref, nw_ref, pw_ref, ps_ref, e_ref, et_ref, tril_ref, *rest):
        shard_refs, (y_ref, ypre_ref, pre_ref, st_ref) = rest[:ns], rest[ns:ns + 4]
        gathered_refs, (xtail, utail, state) = rest[ns + 4:2 * ns + 4], rest[2 * ns + 4:2 * ns + 7]
        gather = _Gather(shard_refs, gathered_refs, by_cols, *rest[2 * ns + 7:])
        c = pl.program_id(1)
        step = pl.program_id(0) * nc + c

        @pl.when(step == 0)
        def _():
            gather.start()

        @pl.when(step == n_steps // 2)
        def _():
            gather.forward()

        @pl.when(c == 0)
        def _():
            xtail[...] = jnp.zeros_like(xtail)
            utail[...] = jnp.zeros_like(utail)
            state[...] = jnp.zeros_like(state)

        valid = (c > 0) | (lax.broadcasted_iota(jnp.int32, (CHUNK, 1), 0) >= PAD_ROWS)

        u = p_ref[:, 0:D_POOL]
        inv_cnt, lane = _pool_inv_count(c)
        win = _pool_window_sums(jnp.concatenate([utail[...], u], axis=0), lane)
        utail[...] = u[CHUNK - HALO:, :]
        pooled = win * inv_cnt - u
        mixed = jnp.concatenate(
            [_dot(pooled[:, g * 128:(g + 1) * 128], pw_ref[g]) for g in range(len(POOL_WINDOWS))], axis=1)
        y_ref[:, 0:D_POOL] = _mx(mixed * ps_ref[...])

        xbc = p_ref[:, OFF_X:OFF_X + D_XBC]
        pre = _conv_pre(jnp.concatenate([xtail[...], xbc], axis=0), xbc, cw_ref[...], cb_ref[...])
        xtail[...] = xbc[CHUNK - HALO:, :]
        pre_ref[...] = pre
        xc = pre * _sigmoid(pre)
        dt, _, a_col, _, _ = _dt_and_cumsum(p_ref[:, OFF_DT:OFF_DT + 128], dtb_ref[...], alog_ref[...], valid,
                                            tril_ref[...])
        s_prev = state[...]
        st_ref[0] = s_prev
        yp, s_new = _ssd_chunk_fwd(xc[:, 0:D_SSM], xc[:, D_SSM:D_SSM + 512], xc[:, D_SSM + 512:], dt, a_col, s_prev,
                                   dx_ref[...], e_ref[...], et_ref[...])
        state[...] = s_new
        ypre_ref[...] = yp
        z = p_ref[:, OFF_Z:OFF_Z + D_SSM]
        yz = yp * (z * _sigmoid(z))
        outs = []
        for g in range(N_GROUPS):
            gs = slice(g * GROUP_CH, (g + 1) * GROUP_CH)
            r = lax.rsqrt(jnp.mean(yz[:, gs] * yz[:, gs], axis=-1, keepdims=True) + EPS)
            outs.append(yz[:, gs] * r)
        y_ref[:, D_POOL:] = _mx(jnp.concatenate(outs, axis=1) * nw_ref[...])

        @pl.when(step == n_steps - 1)
        def _():
            gather.finish()

    blk = lambda w: pl.BlockSpec((CHUNK, w), lambda b, c: (_chunk_block(b, c, nb, nc), 0))
    hbm = pl.BlockSpec(memory_space=pl.ANY)
    outs = pl.pallas_call(
        body, grid=(nb, nc), name="mixer_fwd",
        in_specs=[blk(PROJ_W), _full((4, D_XBC)), _full((1, D_XBC)), _full((1, 128)), _full((1, 128)),
                  _full((1, D_SSM)), _full((1, D_SSM)), _full((4, 128, 128)), _full((1, D_POOL)),
                  _full((128, D_SSM)), _full((D_SSM, 128)), _full((CHUNK, CHUNK))] + [hbm] * ns,
        out_specs=[blk(D_MIX), blk(D_SSM), blk(D_XBC),
                   pl.BlockSpec((1, D_SSM, D_STATE), lambda b, c: (b * nc + c, 0, 0))] + [hbm] * ns,
        out_shape=[jax.ShapeDtypeStruct((m, D_MIX), MXU_DTYPE), jax.ShapeDtypeStruct((m, D_SSM), F32),
                   jax.ShapeDtypeStruct((m, D_XBC), F32), jax.ShapeDtypeStruct((m // CHUNK, D_SSM, D_STATE), F32)]
        + _Gather.out_shapes(shards, by_cols),
        scratch_shapes=[pltpu.VMEM((HALO, D_XBC), F32), pltpu.VMEM((HALO, D_POOL), F32),
                        pltpu.VMEM((D_SSM, D_STATE), F32)] + _Gather.scratch(ns),
        compiler_params=_params("arbitrary", "arbitrary"),
    )(proj, cw, cb, dt_bias, a_log, d_x, nw, pool_w, pool_scale, e_mat, et_f32, tril, *shards)
    return outs[0], outs[1], outs[2], outs[3], outs[4:]


def _mixer_bwd(proj, dy, ypre, conv_pre, states, cw, dt_bias, a_log, d_x, nw, pool_w, pool_scale, nb, chip_sums):
    m = proj.shape[0]
    nc = m // nb // CHUNK
    e_mat, et_mat, et_f32, tril, triu = _ssd_constants()
    hb = CHUNK // HALO
    ns = len(chip_sums)

    def body(p_ref, halo_ref, dy_ref, ypre_ref, pre_ref, st_ref, cw_ref, dtb_ref, alog_ref, dx_ref, nw_ref, pw_ref,
             ps_ref, e_ref, et_ref, etf_ref, tril_ref, triu_ref, *rest):
        cs_refs = rest[:ns]
        dp_ref, gcw_ref, gcb_ref, gdtb_ref, galog_ref, gd_ref, gnw_ref, gpw_ref, gps_ref = rest[ns:ns + 9]
        part_refs, (ds_carry, dpre_next, dq_next) = rest[ns + 9:2 * ns + 9], rest[2 * ns + 9:2 * ns + 12]
        exchange = _ChipExchange(cs_refs, part_refs, [False] * ns, *rest[2 * ns + 12:])
        b = pl.program_id(0)
        cc = pl.program_id(1)
        c = nc - 1 - cc

        @pl.when((b == 0) & (cc == 0))
        def _():
            exchange.start()
            for r in (gcw_ref, gcb_ref, gdtb_ref, galog_ref, gd_ref, gnw_ref, gpw_ref, gps_ref):
                r[...] = jnp.zeros_like(r)

        @pl.when(cc == 0)
        def _():
            ds_carry[...] = jnp.zeros_like(ds_carry)
            dpre_next[...] = jnp.zeros_like(dpre_next)
            dq_next[...] = jnp.zeros_like(dq_next)

        valid = (c > 0) | (lax.broadcasted_iota(jnp.int32, (CHUNK, 1), 0) >= PAD_ROWS)
        first = c > 0

        u = p_ref[:, 0:D_POOL]
        u_halo = jnp.where(first, halo_ref[...], 0.0)
        inv_cnt, lane = _pool_inv_count(c)
        pooled = _pool_window_sums(jnp.concatenate([u_halo, u], axis=0), lane) * inv_cnt - u
        dyp = dy_ref[:, 0:D_POOL]
        ps = ps_ref[...]
        dmixed = dyp * ps
        mixed, dpooled = [], []
        for g in range(len(POOL_WINDOWS)):
            gsl = slice(g * 128, (g + 1) * 128)
            pw = pw_ref[g]
            mixed.append(_dot(pooled[:, gsl], pw))
            dpooled.append(_dot_nt(dmixed[:, gsl], pw))
            gpw_ref[g] += _dot_tn(pooled[:, gsl], dmixed[:, gsl])
        gps_ref[...] += jnp.sum(dyp * jnp.concatenate(mixed, axis=1), axis=0, keepdims=True)
        dpooled = jnp.concatenate(dpooled, axis=1)
        dq = dpooled * inv_cnt
        du = _pool_window_sums_ahead(jnp.concatenate([dq, dq_next[...]], axis=0), lane) - dpooled
        dq_next[...] = dq[0:HALO, :]
        dp_ref[:, 0:D_POOL] = _mx(du)

        yp = ypre_ref[...]
        z = p_ref[:, OFF_Z:OFF_Z + D_SSM]
        sz, dsz = _silu_and_grad(z)
        yz = yp * sz
        do = dy_ref[:, D_POOL:]
        nw_row = nw_ref[...]
        dyz = []
        gnw = []
        for g in range(N_GROUPS):
            gs = slice(g * GROUP_CH, (g + 1) * GROUP_CH)
            r = lax.rsqrt(jnp.mean(yz[:, gs] * yz[:, gs], axis=-1, keepdims=True) + EPS)
            n = yz[:, gs] * r
            gnw.append(jnp.sum(do[:, gs] * n, axis=0, keepdims=True))
            dn = do[:, gs] * nw_row[:, gs]
            dyz.append(r * (dn - n * jnp.mean(dn * n, axis=-1, keepdims=True)))
        gnw_ref[...] += jnp.concatenate(gnw, axis=1)
        dyz = jnp.concatenate(dyz, axis=1)
        dp_ref[:, OFF_Z:OFF_Z + D_SSM] = _mx(dyz * yp * dsz)
        dyp_ssm = dyz * sz

        xc, dsilu = _silu_and_grad(pre_ref[...])
        dtr = p_ref[:, OFF_DT:OFF_DT + 128]
        dt, a_row, a_col, dt_pre, head = _dt_and_cumsum(dtr, dtb_ref[...], alog_ref[...], valid, tril_ref[...])
        dxs, dbm, dcm, ddt, da, dd, ds_prev = _ssd_chunk_bwd(
            xc[:, 0:D_SSM], xc[:, D_SSM:D_SSM + 512], xc[:, D_SSM + 512:], dt, a_row, a_col, st_ref[0],
            ds_carry[...], dyp_ssm, dx_ref[...], e_ref[...], et_ref[...], etf_ref[...], triu_ref[...])
        ds_carry[...] = ds_prev
        gd_ref[...] += dd
        galog_ref[...] += da * a_row
        ddtr = jnp.where(valid & head, ddt * _sigmoid(dt_pre), 0.0)
        gdtb_ref[...] += jnp.sum(ddtr, axis=0, keepdims=True)
        dp_ref[:, OFF_DT:OFF_DT + 128] = _mx(ddtr)

        dpre = jnp.concatenate([dxs, dbm, dcm], axis=1) * dsilu
        gcb_ref[...] += jnp.sum(dpre, axis=0, keepdims=True)
        dext = jnp.concatenate([dpre, dpre_next[...]], axis=0)
        dpre_next[...] = dpre[0:HALO, :]
        ups = [_shift_up(dext, 3 - k) for k in range(4)]
        xbc = p_ref[:, OFF_X:OFF_X + D_XBC]
        gcw_ref[...] += jnp.concatenate([jnp.sum(xbc * ups[k], axis=0, keepdims=True) for k in range(4)], axis=0)
        cw = cw_ref[...]
        dp_ref[:, OFF_X:OFF_X + D_XBC] = _mx(cw[3:4, :] * ups[3] + cw[2:3, :] * ups[2]
                                             + cw[1:2, :] * ups[1] + cw[0:1, :] * ups[0])

        @pl.when((b == nb - 1) & (cc == nc - 1))
        def _():
            exchange.finish()

    blk = lambda w: pl.BlockSpec((CHUNK, w), lambda b, cc: (_chunk_block(b, nc - 1 - cc, nb, nc), 0))
    halo = pl.BlockSpec((HALO, D_POOL),
                        lambda b, cc: (_chunk_block(b, jnp.maximum(nc - 2 - cc, 0), nb, nc) * hb + hb - 1, 0))
    hbm = pl.BlockSpec(memory_space=pl.ANY)
    outs = pl.pallas_call(
        body, grid=(nb, nc), name="mixer_bwd",
        in_specs=[blk(PROJ_W), halo, blk(D_MIX), blk(D_SSM), blk(D_XBC),
                  pl.BlockSpec((1, D_SSM, D_STATE), lambda b, cc: (b * nc + nc - 1 - cc, 0, 0)),
                  _full((4, D_XBC)), _full((1, 128)), _full((1, 128)), _full((1, D_SSM)),
                  _full((1, D_SSM)), _full((4, 128, 128)), _full((1, D_POOL)),
                  _full((128, D_SSM)), _full((D_SSM, 128)), _full((D_SSM, 128)), _full((CHUNK, CHUNK)),
                  _full((CHUNK, CHUNK))] + [hbm] * ns,
        out_specs=[blk(PROJ_W), _full((4, D_XBC)), _full((1, D_XBC)), _full((1, 128)), _full((1, 128)), _full((1, 128)),
                   _full((1, D_SSM)), _full((4, 128, 128)), _full((1, D_POOL))] + [hbm] * ns,
        out_shape=[jax.ShapeDtypeStruct((m, PROJ_W), MXU_DTYPE), jax.ShapeDtypeStruct((4, D_XBC), F32),
                   jax.ShapeDtypeStruct((1, D_XBC), F32), jax.ShapeDtypeStruct((1, 128), F32),
                   jax.ShapeDtypeStruct((1, 128), F32), jax.ShapeDtypeStruct((1, 128), F32),
                   jax.ShapeDtypeStruct((1, D_SSM), F32), jax.ShapeDtypeStruct((4, 128, 128), F32),
                   jax.ShapeDtypeStruct((1, D_POOL), F32)] + _ChipExchange.out_shapes(chip_sums, [False] * ns),
        scratch_shapes=[pltpu.VMEM((D_SSM, D_STATE), F32), pltpu.VMEM((HALO, D_XBC), F32),
                        pltpu.VMEM((HALO, D_POOL), F32)] + _ChipExchange.scratch(ns),
        compiler_params=_params("arbitrary", "arbitrary"),
    )(proj, proj, dy, ypre, conv_pre, states, cw, dt_bias, a_log, d_x, nw, pool_w, pool_scale, e_mat, et_mat, et_f32, tril, triu,
      *chip_sums)
    return outs[:9], outs[9:]


MESH_IDS = pl.DeviceIdType.MESH
_HBM = pl.BlockSpec(memory_space=pltpu.HBM)


def _coords():
    return lax.axis_index("x"), lax.axis_index("y"), lax.axis_index("c")


def _other_chips(x, y):
    return [(1 - x, y), (x, 1 - y), (1 - x, 1 - y)]


class _Gather:
    def __init__(self, ins, outs, by_cols, send_sems, recv_sems, local_sems):
        self.ins, self.outs, self.by_cols, self.n = ins, outs, by_cols, len(ins)
        self.send_sems, self.recv_sems, self.local_sems = send_sems, recv_sems, local_sems
        self.x, self.y, self.c = _coords()
        self.me, self.sibling = (self.x, self.y, self.c), (self.x, self.y, 1 - self.c)
        self.chips = _other_chips(self.x, self.y)

    @staticmethod
    def scratch(n):
        return [pltpu.SemaphoreType.DMA((7 * n,)), pltpu.SemaphoreType.DMA((7 * n,)), pltpu.SemaphoreType.DMA((n,))]

    @staticmethod
    def out_shapes(shards, by_cols):
        return [jax.ShapeDtypeStruct((s.shape[0], N_DEV * s.shape[1]) if cols else (N_DEV,) + s.shape, s.dtype)
                for s, cols in zip(shards, by_cols)]

    def _block(self, t, device):
        idx = 4 * device[0] + 2 * device[1] + device[2]
        if not self.by_cols[t]:
            return self.outs[t].at[idx]
        w = self.ins[t].shape[1]
        return self.outs[t].at[:, pl.ds(pl.multiple_of(idx * w, w), w)]

    def _copy(self, t, k, block, to, own=False):
        dst = self._block(t, block)
        return pltpu.make_async_remote_copy(
            src_ref=self.ins[t] if own else dst, dst_ref=dst, send_sem=self.send_sems.at[t * 7 + k],
            recv_sem=self.recv_sems.at[t * 7 + k], device_id=to, device_id_type=MESH_IDS)

    def _mine(self):
        return [pltpu.make_async_copy(self.ins[t], self._block(t, self.me), self.local_sems.at[t])
                for t in range(self.n)]

    def _first(self):
        cps = []
        for t in range(self.n):
            cps.append(self._copy(t, 0, self.me, self.sibling, own=True))
            cps += [self._copy(t, 1 + j, self.me, (*chip, self.c), own=True) for j, chip in enumerate(self.chips)]
        return cps

    def _passed(self):
        return [self._copy(t, 4 + j, (*chip, self.c), self.sibling)
                for j, chip in enumerate(self.chips) for t in range(self.n)]

    def start(self):
        for cp in self._mine() + self._first():
            cp.start()

    def forward(self):
        for j, chip in enumerate(self.chips):
            for t in range(self.n):
                self._copy(t, 1 + j, (*chip, self.c), self.me).wait_recv()
                self._copy(t, 4 + j, (*chip, self.c), self.sibling).start()

    def finish(self):
        for t in range(self.n):
            self._copy(t, 0, self.sibling, self.me).wait_recv()
            for j, chip in enumerate(self.chips):
                self._copy(t, 4 + j, (*chip, 1 - self.c), self.me).wait_recv()
        for cp in self._first() + self._passed():
            cp.wait_send()
        for cp in self._mine():
            cp.wait()


def _weight_gather(shards):
    n = len(shards)

    def body(*refs):
        g = _Gather(refs[:n], refs[n:2 * n], [False] * n, *refs[2 * n:])
        g.start()
        g.forward()
        g.finish()

    return pl.pallas_call(
        body, name="weight_gather",
        in_specs=[_HBM] * n, out_specs=[_HBM] * n,
        out_shape=_Gather.out_shapes(shards, [False] * n),
        scratch_shapes=_Gather.scratch(n),
    )(*shards)


def _owner_blocks(g, by_cols):
    return (g.shape[0], g.shape[1] // N_DEV) if by_cols else g.shape[2:]


def _grad_exchange_d2d(gs, by_cols, name):
    n = len(gs)

    def body(*refs):
        ins, got = refs[:n], refs[n:2 * n]
        send_sems, recv_sems = refs[2 * n:]
        x, y, c = _coords()

        def src(t, k):
            if not by_cols[t]:
                return ins[t].at[k, 1 - c]
            w = ins[t].shape[1] // N_DEV
            return ins[t].at[:, pl.ds(pl.multiple_of((2 * k + 1 - c) * w, w), w)]

        remote = [pltpu.make_async_remote_copy(
            src_ref=src(t, k), dst_ref=got[t].at[k], send_sem=send_sems.at[t * 4 + k],
            recv_sem=recv_sems.at[t * 4 + k], device_id=(x, y, 1 - c), device_id_type=MESH_IDS)
            for t in range(n) for k in range(4)]
        for cp in remote:
            cp.start()
        for cp in remote:
            cp.wait_recv()
        for cp in remote:
            cp.wait_send()

    return pl.pallas_call(
        body, name=name,
        in_specs=[_HBM] * n, out_specs=[_HBM] * n,
        out_shape=[jax.ShapeDtypeStruct((4,) + _owner_blocks(g, cols), g.dtype) for g, cols in zip(gs, by_cols)],
        scratch_shapes=[pltpu.SemaphoreType.DMA((4 * n,)), pltpu.SemaphoreType.DMA((4 * n,))],
    )(*gs)


def _small_allreduce(pack):
    rows = pack.shape[0]

    def body(p_ref, o_ref, sib_ref, parts_ref, send_sems, recv_sems):
        x, y, c = _coords()
        my_chip = 2 * x + y
        swap = pltpu.make_async_remote_copy(src_ref=p_ref, dst_ref=sib_ref, send_sem=send_sems.at[0],
                                            recv_sem=recv_sems.at[0], device_id=(x, y, 1 - c), device_id_type=MESH_IDS)
        swap.start()
        swap.wait_recv()
        parts_ref[my_chip] = p_ref[...] + sib_ref[...]
        remote = [pltpu.make_async_remote_copy(
            src_ref=parts_ref.at[my_chip], dst_ref=parts_ref.at[my_chip], send_sem=send_sems.at[1 + j],
            recv_sem=recv_sems.at[1 + j], device_id=(cx, cy, c), device_id_type=MESH_IDS)
            for j, (cx, cy) in enumerate(_other_chips(x, y))]
        for cp in remote:
            cp.start()
        for j, (cx, cy) in enumerate(_other_chips(x, y)):
            slot = parts_ref.at[2 * cx + cy]
            pltpu.make_async_remote_copy(src_ref=slot, dst_ref=slot, send_sem=send_sems.at[1 + j],
                                         recv_sem=recv_sems.at[1 + j], device_id=(cx, cy, c),
                                         device_id_type=MESH_IDS).wait_recv()
        o_ref[...] = ((parts_ref[0] + parts_ref[1]) + parts_ref[2]) + parts_ref[3]
        swap.wait_send()
        for cp in remote:
            cp.wait_send()

    vmem = pl.BlockSpec(memory_space=pltpu.VMEM)
    return pl.pallas_call(
        body, name="small_allreduce", in_specs=[vmem], out_specs=vmem,
        out_shape=jax.ShapeDtypeStruct((rows, 128), F32),
        scratch_shapes=[pltpu.VMEM((rows, 128), F32), pltpu.VMEM((4, rows, 128), F32),
                        pltpu.SemaphoreType.DMA((4,)), pltpu.SemaphoreType.DMA((4,))],
    )(pack)


class _ChipExchange:
    def __init__(self, ins, outs, whole, send_sems, recv_sems, local_sems):
        self.ins, self.outs, self.whole, self.n = ins, outs, whole, len(ins)
        self.send_sems, self.recv_sems, self.local_sems = send_sems, recv_sems, local_sems
        self.x, self.y, self.c = _coords()
        self.my_chip = 2 * self.x + self.y
        self.chips = _other_chips(self.x, self.y)

    @staticmethod
    def scratch(n):
        return [pltpu.SemaphoreType.DMA((3 * n,)), pltpu.SemaphoreType.DMA((3 * n,)), pltpu.SemaphoreType.DMA((n,))]

    def _src(self, t, k):
        return self.ins[t] if self.whole[t] else self.ins[t].at[k]

    def _local(self):
        return [pltpu.make_async_copy(self._src(t, self.my_chip), self.outs[t].at[self.my_chip], self.local_sems.at[t])
                for t in range(self.n)]

    def _remote(self):
        return [pltpu.make_async_remote_copy(
            src_ref=self._src(t, 2 * cx + cy), dst_ref=self.outs[t].at[self.my_chip],
            send_sem=self.send_sems.at[t * 3 + j], recv_sem=self.recv_sems.at[t * 3 + j],
            device_id=(cx, cy, self.c), device_id_type=MESH_IDS)
            for t in range(self.n) for j, (cx, cy) in enumerate(self.chips)]

    def start(self):
        for cp in self._remote() + self._local():
            cp.start()

    def finish(self):
        for t in range(self.n):
            for j, (cx, cy) in enumerate(self.chips):
                slot = self.outs[t].at[2 * cx + cy]
                pltpu.make_async_remote_copy(
                    src_ref=slot, dst_ref=slot, send_sem=self.send_sems.at[t * 3 + j],
                    recv_sem=self.recv_sems.at[t * 3 + j], device_id=(cx, cy, self.c),
                    device_id_type=MESH_IDS).wait_recv()
        for cp in self._remote():
            cp.wait_send()
        for cp in self._local():
            cp.wait()

    @staticmethod
    def out_shapes(arrs, whole):
        return [jax.ShapeDtypeStruct(((4,) + a.shape) if w else a.shape, a.dtype) for a, w in zip(arrs, whole)]


def _row_tile(rows, cols, n_arrays):
    budget = 24 * 1024 * 1024
    padded = -(-cols // 128) * 128
    step = 16 if rows % 16 == 0 else 8
    tr = max(step, budget // (n_arrays * 2 * 4 * padded) // step * step)
    while rows % tr:
        tr -= step
    return tr


def _chip_sum(g, by_cols, got, core, name):
    rows, cols = _owner_blocks(g, by_cols)
    tr = _row_tile(rows, cols, 3)

    def body(c_ref, a_ref, b_ref, o_ref):
        o_ref[...] = (a_ref[...] + b_ref[...]).astype(o_ref.dtype)

    own = (pl.BlockSpec((tr, cols), lambda k, i, c: (i, 2 * k + c[0])) if by_cols
           else pl.BlockSpec((None, None, tr, cols), lambda k, i, c: (k, c[0], i, 0)))
    grid_spec = pltpu.PrefetchScalarGridSpec(
        num_scalar_prefetch=1, grid=(4, rows // tr),
        in_specs=[own, pl.BlockSpec((None, tr, cols), lambda k, i, c: (k, i, 0))],
        out_specs=pl.BlockSpec((None, tr, cols), lambda k, i, c: (k, i, 0)))
    return pl.pallas_call(body, grid_spec=grid_spec, name=name,
                          out_shape=jax.ShapeDtypeStruct((4, rows, cols), MXU_DTYPE),
                          compiler_params=_params("arbitrary", "arbitrary"))(core, g, got)


def _adamw_math(w, g, m, v):
    m2 = ADAM_B1 * m + (1.0 - ADAM_B1) * g
    v2 = ADAM_B2 * v + (1.0 - ADAM_B2) * (g * g)
    m_hat = m2 / (1.0 - ADAM_B1 ** ADAM_STEP)
    v_hat = v2 / (1.0 - ADAM_B2 ** ADAM_STEP)
    delta = -ADAM_LR * (m_hat / (jnp.sqrt(v_hat) + ADAM_EPS) + ADAM_WD * w)
    return delta, m2, v2


def _adamw(parts, w, m, v, name):
    rows, cols = w.shape
    tr = _row_tile(rows, cols, 11)

    def body(p_ref, w_ref, m_ref, v_ref, g_ref, d_ref, m2_ref, v2_ref):
        part = lambda k: p_ref[k].astype(F32)
        g = ((part(0) + part(1)) + part(2)) + part(3)
        d, m2, v2 = _adamw_math(w_ref[...], g, m_ref[...], v_ref[...])
        g_ref[...] = g
        d_ref[...] = d
        m2_ref[...] = m2
        v2_ref[...] = v2

    blk = pl.BlockSpec((tr, cols), lambda i: (i, 0))
    out = jax.ShapeDtypeStruct((rows, cols), F32)
    return pl.pallas_call(body, grid=(rows // tr,), name=name,
                          in_specs=[pl.BlockSpec((4, tr, cols), lambda i: (0, i, 0)), blk, blk, blk],
                          out_specs=[blk] * 4, out_shape=[out] * 4,
                          compiler_params=_params("arbitrary"))(parts, w, m, v)


def _adamw_small(gs, ws, ms, vs):
    n = len(ws)

    def body(*refs):
        g_refs, w_refs, m_refs, v_refs = (refs[k * n:(k + 1) * n] for k in range(4))
        d_refs, m2_refs, v2_refs = (refs[(4 + k) * n:(5 + k) * n] for k in range(3))
        for t in range(n):
            d, m2, v2 = _adamw_math(w_refs[t][...], g_refs[t][...], m_refs[t][...], v_refs[t][...])
            d_refs[t][...] = d
            m2_refs[t][...] = m2
            v2_refs[t][...] = v2

    outs = pl.pallas_call(body, name="adamw_small",
                          out_shape=[jax.ShapeDtypeStruct(w.shape, F32) for w in ws] * 3)(*gs, *ws, *ms, *vs)
    return outs[:n], outs[n:2 * n], outs[2 * n:]


_PACK_TILE = 8 * 128


def _pack(arrays):
    rows = []
    for a in arrays:
        flat = a.astype(F32).reshape(-1)
        rows.append(jnp.pad(flat, (0, -flat.shape[0] % _PACK_TILE)).reshape(-1, 128))
    return jnp.concatenate(rows, axis=0)


def _unpack(pack, shapes):
    out, r = [], 0
    for s in shapes:
        n = int(np.prod(s))
        out.append(pack[r:r + -(-n // 128)].reshape(-1)[:n].reshape(s))
        r += -(-n // _PACK_TILE) * 8
    return out


def _pad128(v):
    v = v.reshape(1, -1).astype(F32)
    return jnp.pad(v, ((0, 0), (0, 128 - v.shape[1])))


_WEIGHTS = ["meta", "norm_mix_w", "w_in", "pool_w", "pool_scale", "conv_w", "conv_b", "dt_bias", "a_log", "d_skip",
            "ssm_norm_w", "w_out", "norm_ffn_w", "w_ff1", "w_ff2", "norm_f_w"]
_BIG = ["w_in", "w_out", "w_ff1", "w_ff2"]
_SMALL = [n for n in _WEIGHTS if n not in _BIG]


def kernel(x, meta, norm_mix_w, w_in, pool_w, pool_scale, conv_w, conv_b, dt_bias, a_log, d_skip, ssm_norm_w, w_out, norm_ffn_w, w_ff1, w_ff2, norm_f_w, loss_target, m_meta, m_norm_mix_w, m_w_in, m_pool_w, m_pool_scale, m_conv_w, m_conv_b, m_dt_bias, m_a_log, m_d_skip, m_ssm_norm_w, m_w_out, m_norm_ffn_w, m_w_ff1, m_w_ff2, m_norm_f_w, v_meta, v_norm_mix_w, v_w_in, v_pool_w, v_pool_scale, v_conv_w, v_conv_b, v_dt_bias, v_a_log, v_d_skip, v_ssm_norm_w, v_w_out, v_norm_ffn_w, v_w_ff1, v_w_ff2, v_norm_f_w):
    wts = dict(meta=meta, norm_mix_w=norm_mix_w, w_in=w_in, pool_w=pool_w, pool_scale=pool_scale, conv_w=conv_w,
               conv_b=conv_b, dt_bias=dt_bias, a_log=a_log, d_skip=d_skip, ssm_norm_w=ssm_norm_w, w_out=w_out,
               norm_ffn_w=norm_ffn_w, w_ff1=w_ff1, w_ff2=w_ff2, norm_f_w=norm_f_w)
    mom1 = dict(zip(_WEIGHTS, (m_meta, m_norm_mix_w, m_w_in, m_pool_w, m_pool_scale, m_conv_w, m_conv_b, m_dt_bias,
                               m_a_log, m_d_skip, m_ssm_norm_w, m_w_out, m_norm_ffn_w, m_w_ff1, m_w_ff2, m_norm_f_w)))
    mom2 = dict(zip(_WEIGHTS, (v_meta, v_norm_mix_w, v_w_in, v_pool_w, v_pool_scale, v_conv_w, v_conv_b, v_dt_bias,
                               v_a_log, v_d_skip, v_ssm_norm_w, v_w_out, v_norm_ffn_w, v_w_ff1, v_w_ff2, v_norm_f_w)))
    xi, yi, ci = _coords()
    dev = 4 * xi + 2 * yi + ci
    win_cols = w_in.shape[-1]
    cw_cols = conv_w.shape[-1]

    nb, seq, _ = x.shape
    core = jnp.reshape(ci, (1,)).astype(jnp.int32)
    owners = lambda a: a.reshape((4, 2) + a.shape[1:])

    lead_pack = jnp.zeros((N_META, 512), F32)
    lead_pack = lead_pack.at[:, :128].set(meta).at[:4, 128:128 + cw_cols].set(conv_w[0])
    g_win, g_lead = _weight_gather([_mx(w_in[0]), lead_pack])
    win_full = _assemble_bands(g_win, PROJ_W)
    meta_full = jnp.transpose(g_lead[:, :, :128], (1, 0, 2)).reshape(N_META, D_MODEL)
    cw_full = jnp.transpose(g_lead[:, :4, 128:128 + cw_cols], (1, 0, 2)).reshape(4, D_XBC)

    lead = jnp.concatenate([jnp.zeros((PAD_ROWS, D_MODEL), F32), meta_full] * nb, axis=0)
    x_rows = x.reshape(nb * seq, D_MODEL)
    tgt_rows = loss_target.reshape(nb * seq, D_MODEL)
    dt_bias_p, a_log_p = _pad128(dt_bias), _pad128(a_log)
    d_x = jnp.repeat(d_skip.reshape(1, N_HEADS).astype(F32), HEAD_DIM, axis=1)
    norm_f_row = norm_f_w.reshape(1, D_MODEL)

    hn1, proj = _in_proj(x_rows, lead, norm_mix_w, win_full)
    late_cols = [False, True, False]
    y, ypre, conv_pre, states, (g_wout, wff1_full, g_wff2) = _mixer_fwd(
        proj, cw_full, conv_b, dt_bias_p, a_log_p, d_x, ssm_norm_w, pool_w[0], pool_scale, nb,
        [_mx(w_out[0]), _mx(w_ff1[0]), _mx(w_ff2[0])], late_cols)
    wout_full = g_wout.reshape(D_MIX, D_MODEL)
    wff2_full = g_wff2.reshape(D_FF, D_MODEL)
    loss, gr_nf, gr_nffn, ff, da, hn2, dh1, dh2, dy = _ffn_fwd_bwd(
        x_rows, lead, y, tgt_rows, wout_full, norm_ffn_w, wff1_full, wff2_full, norm_f_row)
    gr_wff2 = _tn_matmul(ff, dh2, "grad_w_ff2", tka=1024)
    gr_wff1 = _tn_matmul(hn2, da, "grad_w_ff1", tka=512)
    gr_wout = _tn_matmul(y, dh1, "grad_w_out", tka=1024)

    late_parts = [owners(gr_wout.reshape(N_DEV, D_MIX // N_DEV, D_MODEL)), gr_wff1,
                  owners(gr_wff2.reshape(N_DEV, D_FF // N_DEV, D_MODEL))]
    late_got = _grad_exchange_d2d(late_parts, late_cols, "grad_exchange_d2d_late")
    late_sums = [_chip_sum(late_parts[t], late_cols[t], late_got[t], core, "chip_sum_late_%d" % t) for t in range(3)]
    (dproj, gr_cw, gr_cb, gr_dtb, gr_alog, gr_d, gr_nw, gr_pw, gr_ps), late_exchanged = _mixer_bwd(
        proj, dy, ypre, conv_pre, states, cw_full, dt_bias_p, a_log_p, d_x, ssm_norm_w, pool_w[0], pool_scale, nb,
        late_sums)

    win_parts = [owners(_tn_matmul_banded(hn1, dproj, "grad_w_in", win_cols, tka=512))]
    win_got = _grad_exchange_d2d(win_parts, [False], "grad_exchange_d2d_w_in")
    win_sum = _chip_sum(win_parts[0], False, win_got[0], core, "chip_sum_w_in")
    gx_rows, gr_nmix, gr_meta, win_exchanged = _in_proj_bwd(dproj, x_rows, lead, dh1, norm_mix_w, win_full, [win_sum])
    parts = dict(w_in=win_exchanged[0], w_out=late_exchanged[0], w_ff1=late_exchanged[1], w_ff2=late_exchanged[2])

    small_full = dict(meta=gr_meta, norm_mix_w=gr_nmix, pool_w=gr_pw, pool_scale=gr_ps, conv_w=gr_cw, conv_b=gr_cb,
                      dt_bias=gr_dtb[:, :N_HEADS], a_log=gr_alog[:, :N_HEADS], d_skip=gr_d[:, :N_HEADS],
                      ssm_norm_w=gr_nw, norm_ffn_w=gr_nffn, norm_f_w=gr_nf, loss=loss[0:1, 0:1])
    small_names = list(small_full)
    small_sum = _small_allreduce(_pack([small_full[n] for n in small_names]))
    gs = dict(zip(small_names, _unpack(small_sum, [small_full[n].shape for n in small_names])))
    gs["meta"] = lax.dynamic_slice_in_dim(gs["meta"], dev * 128, 128, axis=1)
    gs["conv_w"] = lax.dynamic_slice_in_dim(gs["conv_w"], dev * cw_cols, cw_cols, axis=1)

    res = {}
    for n in _BIG:
        shp = wts[n].shape
        res[n] = [o.reshape(shp) for o in _adamw(parts[n], wts[n][0], mom1[n][0], mom2[n][0], "adamw_" + n)]
    as2d = lambda a: a.reshape(-1, a.shape[-1])
    small_g = [as2d(gs[n].reshape(wts[n].shape)) for n in _SMALL]
    small_out = _adamw_small(small_g, *[[as2d(d[n]) for n in _SMALL] for d in (wts, mom1, mom2)])
    for k, n in enumerate(_SMALL):
        res[n] = [o[k].reshape(wts[n].shape) for o in (small_g,) + tuple(small_out)]

    grad_x = gx_rows.reshape(nb, seq, D_MODEL)
    return (gs["loss"][0, 0], grad_x, *[res[n][0] for n in _WEIGHTS], *[res[n][1] for n in _WEIGHTS],
            *[res[n][2] for n in _WEIGHTS], *[res[n][3] for n in _WEIGHTS])
```

```python
import functools

import numpy as np
import jax
import jax.numpy as jnp
from jax import lax
from jax.experimental import pallas as pl
from jax.experimental.pallas import tpu as pltpu

F32 = jnp.float32
MXU_DTYPE = jnp.bfloat16

D_MODEL = 1024
D_POOL = 512
D_SSM = 1536
D_XBC = 2560
N_HEADS = 24
HEAD_DIM = 64
N_GROUPS = 4
GROUP_CH = D_SSM // N_GROUPS
D_STATE = 128
CHUNK = 128
N_META = 16
LEAD = CHUNK
PAD_ROWS = LEAD - N_META
ROW_TILE = 2 * CHUNK
D_MIX = D_POOL + D_SSM
D_FF = 4096
PROJ_W = 4736
OFF_Z = D_POOL
OFF_X = D_POOL + D_SSM
OFF_DT = OFF_X + D_XBC
D_IN_PROJ = OFF_DT + N_HEADS
POOL_WINDOWS = (2, 4, 8, 16)
HALO = 16
EPS = 1e-5
N_DEV = 8

ADAM_LR, ADAM_B1, ADAM_B2, ADAM_EPS, ADAM_WD, ADAM_STEP = 0.001, 0.9, 0.999, 1e-08, 0.01, 10

VMEM_LIMIT = 60 * 1024 * 1024


def _mx(a):
    return a.astype(MXU_DTYPE)


def _dot(a, b):
    return jnp.dot(_mx(a), _mx(b), preferred_element_type=F32)


def _dot_nt(a, b):
    return lax.dot_general(_mx(a), _mx(b), (((1,), (1,)), ((), ())), preferred_element_type=F32)


def _dot_tn(a, b):
    return lax.dot_general(_mx(a), _mx(b), (((0,), (0,)), ((), ())), preferred_element_type=F32)


def _split3(x):
    hi = x.astype(MXU_DTYPE)
    r = x - hi.astype(F32)
    mid = r.astype(MXU_DTYPE)
    lo = (r - mid.astype(F32)).astype(MXU_DTYPE)
    return hi, mid, lo


def _exact_l(c, x):
    hi, mid, lo = _split3(x)
    f = lambda p: jnp.dot(c, p, preferred_element_type=F32)
    return f(hi) + f(mid) + f(lo)


def _exact_r(x, c):
    hi, mid, lo = _split3(x)
    f = lambda p: jnp.dot(p, c, preferred_element_type=F32)
    return f(hi) + f(mid) + f(lo)


def _contract(x, c):
    hi = x.astype(MXU_DTYPE)
    lo = (x - hi.astype(F32)).astype(MXU_DTYPE)
    return jnp.dot(hi, c, preferred_element_type=F32) + jnp.dot(lo, c, preferred_element_type=F32)


def _sigmoid(x):
    return jax.nn.sigmoid(x)


def _softplus(x):
    return jnp.maximum(x, 0.0) + jnp.log1p(jnp.exp(-jnp.abs(x)))


def _silu_and_grad(x):
    s = _sigmoid(x)
    return x * s, s * (1.0 + x * (1.0 - s))


def _shift_up(ext, s):
    if s == 0:
        return ext[:CHUNK, :]
    return pltpu.roll(ext, ext.shape[0] - s, 0)[:CHUNK, :]


def _by_pool_group(lane, a2, a4, a8, a16):
    return jnp.where(lane < 128, a2, jnp.where(lane < 256, a4, jnp.where(lane < 384, a8, a16)))


def _pool_inv_count(chunk_idx):
    row = lax.broadcasted_iota(jnp.int32, (CHUNK, D_POOL), 0)
    lane = lax.broadcasted_iota(jnp.int32, (CHUNK, D_POOL), 1)
    pos1 = jnp.maximum(chunk_idx * CHUNK + row - (PAD_ROWS - 1), 1)
    w = _by_pool_group(lane, 2, 4, 8, 16)
    return 1.0 / jnp.minimum(pos1, w).astype(F32), lane


def _pool_window_sums(u_ext, lane):
    s2 = u_ext + pltpu.roll(u_ext, 1, 0)
    s4 = s2 + pltpu.roll(s2, 2, 0)
    s8 = s4 + pltpu.roll(s4, 4, 0)
    s16 = s8 + pltpu.roll(s8, 8, 0)
    return _by_pool_group(lane, s2[HALO:], s4[HALO:], s8[HALO:], s16[HALO:])


def _pool_window_sums_ahead(q_ext, lane):
    n = q_ext.shape[0]
    r2 = q_ext + pltpu.roll(q_ext, n - 1, 0)
    r4 = r2 + pltpu.roll(r2, n - 2, 0)
    r8 = r4 + pltpu.roll(r4, n - 4, 0)
    r16 = r8 + pltpu.roll(r8, n - 8, 0)
    return _by_pool_group(lane, r2[:CHUNK], r4[:CHUNK], r8[:CHUNK], r16[:CHUNK])


def _conv_pre(ext, xbc, cw, cb):
    s1 = pltpu.roll(ext, 1, 0)
    near = cw[3:4, :] * xbc + cw[2:3, :] * s1[HALO:, :]
    far = cw[1:2, :] * ext + cw[0:1, :] * s1
    return cb + near + pltpu.roll(far, 2, 0)[HALO:, :]


def _dt_and_cumsum(dtr, dt_bias, a_log, valid, tril):
    lane = lax.broadcasted_iota(jnp.int32, (CHUNK, 128), 1)
    head = lane < N_HEADS
    pre = dtr + dt_bias
    dt = jnp.where(valid & head, _softplus(pre), 0.0)
    a_row = jnp.where(head[0:1, :], -jnp.exp(a_log), 0.0)
    a_col = _exact_l(tril, dt * a_row)
    return dt, a_row, a_col, pre, head


def _decay(a_col, a_row_t, h, causal):
    seg = a_col[:, h:h + 1] - a_row_t[h:h + 1, :]
    return jnp.where(causal, jnp.exp(jnp.minimum(seg, 0.0)), 0.0)


def _ssd_chunk_fwd(xs, bm, cm, dt, a_col, s_prev, d_x, e_mat, et_f32):
    lane = lax.broadcasted_iota(jnp.int32, (CHUNK, 128), 1)
    rowi = lax.broadcasted_iota(jnp.int32, (CHUNK, CHUNK), 0)
    coli = lax.broadcasted_iota(jnp.int32, (CHUNK, CHUNK), 1)
    causal = rowi >= coli
    a_row_t = a_col.T
    ax = _exact_r(a_col, e_mat)
    dtx = _exact_r(dt, e_mat)
    xdt = xs * dtx
    ax_last = ax[CHUNK - 1:CHUNK, :]
    e_a = jnp.exp(ax)
    w_end = xdt * jnp.exp(ax_last - ax)
    cd_col = jnp.exp(jnp.sum(et_f32 * a_col[CHUNK - 1:CHUNK, :], axis=1, keepdims=True))
    ys, s_new = [], []
    for g in range(N_GROUPS):
        gs = slice(g * GROUP_CH, (g + 1) * GROUP_CH)
        bg = bm[:, g * D_STATE:(g + 1) * D_STATE]
        cg = cm[:, g * D_STATE:(g + 1) * D_STATE]
        sg = s_prev[gs, :]
        cb = _dot_nt(cg, bg)
        y_off = _dot_nt(cg, sg) * e_a[:, gs]
        s_new.append(sg * cd_col[gs, :] + _dot_tn(w_end[:, gs], bg))
        for pr in range(3):
            c0 = g * GROUP_CH + pr * 128
            xdt_p = xdt[:, c0:c0 + 128]
            h0 = g * 6 + pr * 2
            y0 = _dot(cb * _decay(a_col, a_row_t, h0, causal), xdt_p)
            y1 = _dot(cb * _decay(a_col, a_row_t, h0 + 1, causal), xdt_p)
            ys.append(jnp.where(lane < HEAD_DIM, y0, y1) + y_off[:, pr * 128:(pr + 1) * 128])
    y = jnp.concatenate(ys, axis=1) + d_x * xs
    return y, jnp.concatenate(s_new, axis=0)


def _ssd_chunk_bwd(xs, bm, cm, dt, a_row, a_col, s_prev, ds_new, dy, d_x, e_mat, et_mat, et_f32, triu):
    lane = lax.broadcasted_iota(jnp.int32, (CHUNK, 128), 1)
    sub = lax.broadcasted_iota(jnp.int32, (CHUNK, 128), 0)
    rowi = lax.broadcasted_iota(jnp.int32, (CHUNK, CHUNK), 0)
    coli = lax.broadcasted_iota(jnp.int32, (CHUNK, CHUNK), 1)
    causal = rowi >= coli
    a_row_t = a_col.T
    ax = _exact_r(a_col, e_mat)
    dtx = _exact_r(dt, e_mat)
    xdt = xs * dtx
    ax_last = ax[CHUNK - 1:CHUNK, :]
    e_a = jnp.exp(ax)
    dte = jnp.exp(ax_last - ax)
    w_end = xdt * dte
    cd_col = jnp.exp(jnp.sum(et_f32 * a_col[CHUNK - 1:CHUNK, :], axis=1, keepdims=True))
    dye = dy * e_a

    dxdt, zc, ww_all, dbs, dcs, dsp, t1s = [], [], [], [], [], [], []
    zcol = jnp.zeros((CHUNK, 128), F32)
    zrow = jnp.zeros((128, CHUNK), F32)
    for g in range(N_GROUPS):
        gs = slice(g * GROUP_CH, (g + 1) * GROUP_CH)
        bg = bm[:, g * D_STATE:(g + 1) * D_STATE]
        cg = cm[:, g * D_STATE:(g + 1) * D_STATE]
        sg = s_prev[gs, :]
        dsg = ds_new[gs, :]
        cb = _dot_nt(cg, bg)
        cs = _dot_nt(cg, sg)
        dcg = _dot(dye[:, gs], sg)
        dsp.append(dsg * cd_col[gs, :] + _dot_tn(dye[:, gs], cg))
        dwg = _dot_nt(bg, dsg)
        dbg = _dot(w_end[:, gs], dsg)
        ww = dwg * w_end[:, gs]
        ww_all.append(ww)
        zc.append(dye[:, gs] * cs - ww)
        t1s.append(jnp.sum(dsg * sg, axis=1, keepdims=True) * cd_col[gs, :])
        dxdt_g = dwg * dte[:, gs]
        dcb = jnp.zeros((CHUNK, CHUNK), F32)
        pairs = []
        for pr in range(3):
            c0 = g * GROUP_CH + pr * 128
            xdt_p = xdt[:, c0:c0 + 128]
            dy_p = dy[:, c0:c0 + 128]
            acc = None
            for half in range(2):
                h = g * 6 + pr * 2 + half
                ld = _decay(a_col, a_row_t, h, causal)
                gm = cb * ld
                dym = jnp.where((lane < HEAD_DIM) if half == 0 else (lane >= HEAD_DIM), dy_p, 0.0)
                dg = _dot_nt(dym, xdt_p)
                dseg = dg * gm
                dcb = dcb + dg * ld
                t = _dot_tn(gm, dym)
                acc = t if acc is None else acc + t
                zcol = jnp.where(lane == h, jnp.sum(dseg, axis=1, keepdims=True), zcol)
                zrow = jnp.where(sub == h, jnp.sum(dseg, axis=0, keepdims=True), zrow)
            pairs.append(acc)
        dxdt.append(dxdt_g + jnp.concatenate(pairs, axis=1))
        dcs.append(dcg + _dot(dcb, bg))
        dbs.append(dbg + _dot_tn(dcb, cg))
    dxdt = jnp.concatenate(dxdt, axis=1)
    zc = jnp.concatenate(zc, axis=1)
    ww_all = jnp.concatenate(ww_all, axis=1)
    t1 = jnp.concatenate(t1s, axis=0)

    sub8 = lax.broadcasted_iota(jnp.int32, (8, D_SSM), 0)
    col_sums = jnp.where(sub8 == 0, jnp.sum(dy * xs, axis=0, keepdims=True),
                         jnp.where(sub8 == 1, jnp.sum(ww_all, axis=0, keepdims=True), 0.0))
    head_sums = _exact_r(col_sums, et_mat)
    dd = head_sums[0:1, :]
    q_row = head_sums[1:2, :] + jnp.sum(et_f32 * t1, axis=0, keepdims=True)
    da_col = _contract(zc, et_mat) + zcol - zrow.T
    da_col = da_col + jnp.where(sub == CHUNK - 1, q_row, 0.0)
    ddt = _contract(dxdt * xs, et_mat)
    dxs = dxdt * dtx + d_x * dy
    rc = _exact_l(triu, da_col)
    ddt = ddt + rc * a_row
    da = jnp.sum(rc * dt, axis=0, keepdims=True)
    return (dxs, jnp.concatenate(dbs, axis=1), jnp.concatenate(dcs, axis=1), ddt, da, dd,
            jnp.concatenate(dsp, axis=0))


def _ssd_constants():
    iota = lambda shape, d: lax.broadcasted_iota(jnp.int32, shape, d)
    e = iota((128, D_SSM), 0) == iota((128, D_SSM), 1) // HEAD_DIM
    et = iota((D_SSM, 128), 1) == iota((D_SSM, 128), 0) // HEAD_DIM
    tril = iota((CHUNK, CHUNK), 1) <= iota((CHUNK, CHUNK), 0)
    triu = iota((CHUNK, CHUNK), 1) >= iota((CHUNK, CHUNK), 0)
    return (e.astype(MXU_DTYPE), et.astype(MXU_DTYPE), et.astype(F32), tril.astype(MXU_DTYPE), triu.astype(MXU_DTYPE))


def _full(shape):
    nd = len(shape)
    return pl.BlockSpec(shape, lambda *_: (0,) * nd)


def _params(*sem):
    return pltpu.CompilerParams(dimension_semantics=sem, vmem_limit_bytes=VMEM_LIMIT)


def _token_tiles(width, n_tok_tiles):
    return pl.BlockSpec((ROW_TILE, width), lambda i: (jnp.minimum(i, n_tok_tiles - 1), 0))


def _in_proj(x, lead, w1, win):
    nt = x.shape[0] // ROW_TILE
    m = x.shape[0] + ROW_TILE
    tm = ROW_TILE

    def body(x_ref, lead_ref, w1_ref, win_hbm, hn_ref, proj_ref, win_v, sem):
        i = pl.program_id(0)

        @pl.when(i == 0)
        def _():
            cp = pltpu.make_async_copy(win_hbm, win_v, sem)
            cp.start()
            cp.wait()

        x = jnp.where(i == nt, lead_ref[...], x_ref[...])
        r = lax.rsqrt(jnp.mean(x * x, axis=-1, keepdims=True) + EPS)
        hn = _mx(x * r * w1_ref[...])
        hn_ref[...] = hn
        for j in range(0, PROJ_W, 512):
            w = min(512, PROJ_W - j)
            proj_ref[:, j:j + w] = jnp.dot(hn, win_v[:, j:j + w], preferred_element_type=F32)

    return pl.pallas_call(
        body, grid=(m // tm,), name="in_proj",
        in_specs=[_token_tiles(D_MODEL, nt), _full((ROW_TILE, D_MODEL)), _full((1, D_MODEL)),
                  pl.BlockSpec(memory_space=pl.ANY)],
        out_specs=[pl.BlockSpec((tm, D_MODEL), lambda i: (i, 0)), pl.BlockSpec((tm, PROJ_W), lambda i: (i, 0))],
        out_shape=[jax.ShapeDtypeStruct((m, D_MODEL), MXU_DTYPE), jax.ShapeDtypeStruct((m, PROJ_W), F32)],
        scratch_shapes=[pltpu.VMEM((D_MODEL, PROJ_W), MXU_DTYPE), pltpu.SemaphoreType.DMA],
        compiler_params=_params("arbitrary"),
    )(x, lead, w1, win)


def _ffn_fwd_bwd(x, lead, y, tgt, wout, w2n, wff1, wff2, wfn):
    nt = x.shape[0] // ROW_TILE
    m = x.shape[0] + ROW_TILE
    tm = ROW_TILE
    nj = D_FF // 1024

    def body(x_ref, lead_ref, y_ref, tgt_ref, w2n_ref, wfn_ref, wout_hbm, wff1_hbm, wff2_hbm,
             loss_ref, gwf_ref, gw2_ref, ff_ref, da_ref, hn2_ref, dh1_ref, dh2_ref, dy_ref,
             wout_v, wff1_v, wff2_v, a_s, sems):
        i = pl.program_id(0)
        hp = jnp.where(i == nt, lead_ref[...], x_ref[...])

        @pl.when(i == 0)
        def _():
            cps = [pltpu.make_async_copy(s, d, sems.at[k])
                   for k, (s, d) in enumerate(((wout_hbm, wout_v), (wff1_hbm, wff1_v), (wff2_hbm, wff2_v)))]
            for cp in cps:
                cp.start()
            for cp in cps:
                cp.wait()
            loss_ref[...] = jnp.zeros_like(loss_ref)
            gwf_ref[...] = jnp.zeros_like(gwf_ref)
            gw2_ref[...] = jnp.zeros_like(gw2_ref)

        h1 = hp + jnp.dot(y_ref[...], wout_v[...], preferred_element_type=F32)
        r2 = lax.rsqrt(jnp.mean(h1 * h1, axis=-1, keepdims=True) + EPS)
        n2 = h1 * r2
        w2n_row = w2n_ref[...]
        hn2 = _mx(n2 * w2n_row)
        hn2_ref[...] = hn2
        h2 = h1
        for j in range(nj):
            js = slice(j * 1024, (j + 1) * 1024)
            a = jnp.dot(hn2, wff1_v[:, js], preferred_element_type=F32)
            a_s[:, js] = a
            ra = jnp.maximum(a, 0.0)
            ff = _mx(ra * ra)
            ff_ref[:, js] = ff
            h2 = h2 + jnp.dot(ff, wff2_v[js, :], preferred_element_type=F32)

        r3 = lax.rsqrt(jnp.mean(h2 * h2, axis=-1, keepdims=True) + EPS)
        n3 = h2 * r3
        wf_row = wfn_ref[...]
        err = n3 * wf_row - tgt_ref[...]
        tokf = (i < nt).astype(F32)
        loss_ref[...] += 0.5 * jnp.sum(jnp.mean(err * err, axis=-1, keepdims=True) * tokf)
        dout = err * (tokf / D_MODEL)
        gwf_ref[...] += jnp.sum(dout * n3, axis=0, keepdims=True)
        dn3 = dout * wf_row
        dh2 = r3 * (dn3 - n3 * jnp.mean(dn3 * n3, axis=-1, keepdims=True))
        dh2m = _mx(dh2)
        dh2_ref[...] = dh2m

        dhn2 = jnp.zeros((tm, D_MODEL), F32)
        for j in range(nj):
            js = slice(j * 1024, (j + 1) * 1024)
            dff = lax.dot_general(dh2m, wff2_v[js, :], (((1,), (1,)), ((), ())), preferred_element_type=F32)
            da = _mx(dff * (2.0 * jnp.maximum(a_s[:, js], 0.0)))
            da_ref[:, js] = da
            dhn2 = dhn2 + lax.dot_general(da, wff1_v[:, js], (((1,), (1,)), ((), ())), preferred_element_type=F32)
        gw2_ref[...] += jnp.sum(dhn2 * n2, axis=0, keepdims=True)
        dn2 = dhn2 * w2n_row
        dh1 = dh2 + r2 * (dn2 - n2 * jnp.mean(dn2 * n2, axis=-1, keepdims=True))
        dh1_ref[...] = dh1
        dy_ref[...] = lax.dot_general(_mx(dh1), wout_v[...], (((1,), (1,)), ((), ())), preferred_element_type=F32)

    rows = lambda w: pl.BlockSpec((tm, w), lambda i: (i, 0))
    hbm = pl.BlockSpec(memory_space=pl.ANY)
    return pl.pallas_call(
        body, grid=(m // tm,), name="ffn_fwd_bwd",
        in_specs=[_token_tiles(D_MODEL, nt), _full((ROW_TILE, D_MODEL)), rows(D_MIX), _token_tiles(D_MODEL, nt),
                  _full((1, D_MODEL)), _full((1, D_MODEL)), hbm, hbm, hbm],
        out_specs=[_full((1, 128)), _full((1, D_MODEL)), _full((1, D_MODEL)), rows(D_FF), rows(D_FF), rows(D_MODEL),
                   rows(D_MODEL), rows(D_MODEL), rows(D_MIX)],
        out_shape=[jax.ShapeDtypeStruct((1, 128), F32), jax.ShapeDtypeStruct((1, D_MODEL), F32),
                   jax.ShapeDtypeStruct((1, D_MODEL), F32), jax.ShapeDtypeStruct((m, D_FF), MXU_DTYPE),
                   jax.ShapeDtypeStruct((m, D_FF), MXU_DTYPE), jax.ShapeDtypeStruct((m, D_MODEL), MXU_DTYPE),
                   jax.ShapeDtypeStruct((m, D_MODEL), F32), jax.ShapeDtypeStruct((m, D_MODEL), MXU_DTYPE),
                   jax.ShapeDtypeStruct((m, D_MIX), F32)],
        scratch_shapes=[pltpu.VMEM((D_MIX, D_MODEL), MXU_DTYPE), pltpu.VMEM((D_MODEL, D_FF), MXU_DTYPE),
                        pltpu.VMEM((D_FF, D_MODEL), MXU_DTYPE), pltpu.VMEM((tm, D_FF), F32),
                        pltpu.SemaphoreType.DMA((3,))],
        compiler_params=_params("arbitrary"),
    )(x, lead, y, tgt, w2n, wfn, wout, wff1, wff2)


def _in_proj_bwd(dproj, x, lead, dh1, w1, win, chip_sums):
    nt = x.shape[0] // ROW_TILE
    m = x.shape[0] + ROW_TILE
    tm = ROW_TILE
    ns = len(chip_sums)

    def body(dp_ref, x_ref, lead_ref, dh1_ref, w1_ref, win_hbm, *rest):
        cs_refs, (gx_ref, gw1_ref, gmeta_ref) = rest[:ns], rest[ns:ns + 3]
        part_refs, (win_v, sem) = rest[ns + 3:2 * ns + 3], rest[2 * ns + 3:2 * ns + 5]
        exchange = _ChipExchange(cs_refs, part_refs, [False] * ns, *rest[2 * ns + 5:])
        i = pl.program_id(0)

        @pl.when(i == 0)
        def _():
            exchange.start()
            cp = pltpu.make_async_copy(win_hbm, win_v, sem)
            cp.start()
            cp.wait()
            gw1_ref[...] = jnp.zeros_like(gw1_ref)
            gmeta_ref[...] = jnp.zeros_like(gmeta_ref)

        dhn = lax.dot_general(dp_ref[...], win_v[...], (((1,), (1,)), ((), ())), preferred_element_type=F32)
        x = jnp.where(i == nt, lead_ref[...], x_ref[...])
        r = lax.rsqrt(jnp.mean(x * x, axis=-1, keepdims=True) + EPS)
        n = x * r
        gw1_ref[...] += jnp.sum(dhn * n, axis=0, keepdims=True)
        dn = dhn * w1_ref[...]
        dh0 = dh1_ref[...] + r * (dn - n * jnp.mean(dn * n, axis=-1, keepdims=True))

        @pl.when(i < nt)
        def _():
            gx_ref[...] = dh0

        @pl.when(i == nt)
        def _():
            gmeta_ref[...] = dh0[PAD_ROWS:LEAD, :] + dh0[LEAD + PAD_ROWS:2 * LEAD, :]
            exchange.finish()

    rows = lambda w: pl.BlockSpec((tm, w), lambda i: (i, 0))
    hbm = pl.BlockSpec(memory_space=pl.ANY)
    outs = pl.pallas_call(
        body, grid=(m // tm,), name="in_proj_bwd",
        in_specs=[rows(PROJ_W), _token_tiles(D_MODEL, nt), _full((ROW_TILE, D_MODEL)), rows(D_MODEL),
                  _full((1, D_MODEL)), hbm] + [hbm] * ns,
        out_specs=[_token_tiles(D_MODEL, nt), _full((1, D_MODEL)), _full((N_META, D_MODEL))] + [hbm] * ns,
        out_shape=[jax.ShapeDtypeStruct(x.shape, F32), jax.ShapeDtypeStruct((1, D_MODEL), F32),
                   jax.ShapeDtypeStruct((N_META, D_MODEL), F32)]
        + _ChipExchange.out_shapes(chip_sums, [False] * ns),
        scratch_shapes=[pltpu.VMEM((D_MODEL, PROJ_W), MXU_DTYPE), pltpu.SemaphoreType.DMA] + _ChipExchange.scratch(ns),
        compiler_params=_params("arbitrary"),
    )(dproj, x, lead, dh1, w1, win, *chip_sums)
    return outs[0], outs[1], outs[2], outs[3:]


MXU_DEPTH = 256


def _row_slab(m, cap):
    return max(k for k in range(MXU_DEPTH, cap + 1, MXU_DEPTH) if m % k == 0)


def _tn_matmul(a, b, name, tka, max_slab=768, tn=512):
    m, ka = a.shape
    nb = b.shape[1]
    tkm = _row_slab(m, max_slab)
    n_steps = m // tkm

    def body(a_ref, b_ref, o_ref, omx_ref):
        k = pl.program_id(1)

        @pl.when(k == 0)
        def _():
            o_ref[...] = jnp.zeros_like(o_ref)

        at = _mx(a_ref[...])
        for j in range(0, nb, tn):
            w = min(tn, nb - j)
            o_ref[:, j:j + w] += lax.dot_general(at, _mx(b_ref[:, j:j + w]), (((0,), (0,)), ((), ())),
                                                 preferred_element_type=F32)

        @pl.when(k == n_steps - 1)
        def _():
            omx_ref[...] = _mx(o_ref[...])

    out = pl.BlockSpec((tka, nb), lambda i, k: (i, 0))
    return pl.pallas_call(
        body, grid=(ka // tka, n_steps), name=name,
        in_specs=[pl.BlockSpec((tkm, tka), lambda i, k: (k, i)), pl.BlockSpec((tkm, nb), lambda i, k: (k, 0))],
        out_specs=[out, out],
        out_shape=[jax.ShapeDtypeStruct((ka, nb), F32), jax.ShapeDtypeStruct((ka, nb), MXU_DTYPE)],
        compiler_params=_params("arbitrary", "arbitrary"),
    )(a, b)


def _tn_matmul_banded(a, b, name, band, tka, tn=512):
    m, ka = a.shape
    nb = b.shape[1]
    tkm = _row_slab(m, 768)
    n_steps = m // tkm

    def body(a_ref, b_ref, o_ref, acc):
        k = pl.program_id(1)

        @pl.when(k == 0)
        def _():
            acc[...] = jnp.zeros_like(acc)

        at = _mx(a_ref[...])
        for j in range(0, nb, tn):
            w = min(tn, nb - j)
            acc[:, j:j + w] += lax.dot_general(at, _mx(b_ref[:, j:j + w]), (((0,), (0,)), ((), ())),
                                               preferred_element_type=F32)

        @pl.when(k == n_steps - 1)
        def _():
            for j in range(N_DEV):
                o_ref[j] = acc[:, j * band:(j + 1) * band]

    return pl.pallas_call(
        body, grid=(ka // tka, n_steps), name=name,
        in_specs=[pl.BlockSpec((tkm, tka), lambda i, k: (k, i)), pl.BlockSpec((tkm, nb), lambda i, k: (k, 0))],
        out_specs=pl.BlockSpec((N_DEV, tka, band), lambda i, k: (0, i, 0)),
        out_shape=jax.ShapeDtypeStruct((N_DEV, ka, band), F32),
        scratch_shapes=[pltpu.VMEM((tka, nb), F32)],
        compiler_params=_params("arbitrary", "arbitrary"),
    )(a, b)


def _assemble_bands(g, width):
    n, rows, band = g.shape
    tr = 256

    def body(g_ref, o_ref):
        parts = [g_ref[j] for j in range(n)] + [jnp.zeros((tr, width - n * band), g.dtype)]
        o_ref[...] = jnp.concatenate(parts, axis=1)

    return pl.pallas_call(
        body, grid=(rows // tr,), name="assemble_w_in",
        in_specs=[pl.BlockSpec((n, tr, band), lambda i: (0, i, 0))],
        out_specs=pl.BlockSpec((tr, width), lambda i: (i, 0)),
        out_shape=jax.ShapeDtypeStruct((rows, width), g.dtype),
        compiler_params=_params("arbitrary"),
    )(g)


def _chunk_block(b, c, nb, nc):
    return jnp.where(c == 0, nb * (nc - 1) + b, b * (nc - 1) + c - 1)

def _mixer_fwd(proj, cw, cb, dt_bias, a_log, d_x, nw, pool_w, pool_scale, nb, shards, by_cols):
    m = proj.shape[0]
    nc = m // nb // CHUNK
    n_steps = nb * nc
    ns = len(shards)
    e_mat, _, et_f32, tril, _ = _ssd_constants()

    def body(p_ref, cw_ref, cb_ref, dtb_ref, alog_ref, dx_ref, nw_ref, pw_ref, ps_ref, e_ref, et_ref, tril_ref, *rest):
        shard_refs, (y_ref, ypre_ref, pre_ref, st_ref) = rest[:ns], rest[ns:ns + 4]
        gathered_refs, (xtail, utail, state) = rest[ns + 4:2 * ns + 4], rest[2 * ns + 4:2 * ns + 7]
        gather = _Gather(shard_refs, gathered_refs, by_cols, *rest[2 * ns + 7:])
        c = pl.program_id(1)
        step = pl.program_id(0) * nc + c

        @pl.when(step == 0)
        def _():
            gather.start()

        @pl.when(step == n_steps // 2)
        def _():
            gather.forward()

        @pl.when(c == 0)
        def _():
            xtail[...] = jnp.zeros_like(xtail)
            utail[...] = jnp.zeros_like(utail)
            state[...] = jnp.zeros_like(state)

        valid = (c > 0) | (lax.broadcasted_iota(jnp.int32, (CHUNK, 1), 0) >= PAD_ROWS)

        u = p_ref[:, 0:D_POOL]
        inv_cnt, lane = _pool_inv_count(c)
        win = _pool_window_sums(jnp.concatenate([utail[...], u], axis=0), lane)
        utail[...] = u[CHUNK - HALO:, :]
        pooled = win * inv_cnt - u
        mixed = jnp.concatenate(
            [_dot(pooled[:, g * 128:(g + 1) * 128], pw_ref[g]) for g in range(len(POOL_WINDOWS))], axis=1)
        y_ref[:, 0:D_POOL] = _mx(mixed * ps_ref[...])

        xbc = p_ref[:, OFF_X:OFF_X + D_XBC]
        pre = _conv_pre(jnp.concatenate([xtail[...], xbc], axis=0), xbc, cw_ref[...], cb_ref[...])
        xtail[...] = xbc[CHUNK - HALO:, :]
        pre_ref[...] = pre
        xc = pre * _sigmoid(pre)
        dt, _, a_col, _, _ = _dt_and_cumsum(p_ref[:, OFF_DT:OFF_DT + 128], dtb_ref[...], alog_ref[...], valid,
                                            tril_ref[...])
        s_prev = state[...]
        st_ref[0] = s_prev
        yp, s_new = _ssd_chunk_fwd(xc[:, 0:D_SSM], xc[:, D_SSM:D_SSM + 512], xc[:, D_SSM + 512:], dt, a_col, s_prev,
                                   dx_ref[...], e_ref[...], et_ref[...])
        state[...] = s_new
        ypre_ref[...] = yp
        z = p_ref[:, OFF_Z:OFF_Z + D_SSM]
        yz = yp * (z * _sigmoid(z))
        outs = []
        for g in range(N_GROUPS):
            gs = slice(g * GROUP_CH, (g + 1) * GROUP_CH)
            r = lax.rsqrt(jnp.mean(yz[:, gs] * yz[:, gs], axis=-1, keepdims=True) + EPS)
            outs.append(yz[:, gs] * r)
        y_ref[:, D_POOL:] = _mx(jnp.concatenate(outs, axis=1) * nw_ref[...])

        @pl.when(step == n_steps - 1)
        def _():
            gather.finish()

    blk = lambda w: pl.BlockSpec((CHUNK, w), lambda b, c: (_chunk_block(b, c, nb, nc), 0))
    hbm = pl.BlockSpec(memory_space=pl.ANY)
    outs = pl.pallas_call(
        body, grid=(nb, nc), name="mixer_fwd",
        in_specs=[blk(PROJ_W), _full((4, D_XBC)), _full((1, D_XBC)), _full((1, 128)), _full((1, 128)),
                  _full((1, D_SSM)), _full((1, D_SSM)), _full((4, 128, 128)), _full((1, D_POOL)),
                  _full((128, D_SSM)), _full((D_SSM, 128)), _full((CHUNK, CHUNK))] + [hbm] * ns,
        out_specs=[blk(D_MIX), blk(D_SSM), blk(D_XBC),
                   pl.BlockSpec((1, D_SSM, D_STATE), lambda b, c: (b * nc + c, 0, 0))] + [hbm] * ns,
        out_shape=[jax.ShapeDtypeStruct((m, D_MIX), MXU_DTYPE), jax.ShapeDtypeStruct((m, D_SSM), F32),
                   jax.ShapeDtypeStruct((m, D_XBC), F32), jax.ShapeDtypeStruct((m // CHUNK, D_SSM, D_STATE), F32)]
        + _Gather.out_shapes(shards, by_cols),
        scratch_shapes=[pltpu.VMEM((HALO, D_XBC), F32), pltpu.VMEM((HALO, D_POOL), F32),
                        pltpu.VMEM((D_SSM, D_STATE), F32)] + _Gather.scratch(ns),
        compiler_params=_params("arbitrary", "arbitrary"),
    )(proj, cw, cb, dt_bias, a_log, d_x, nw, pool_w, pool_scale, e_mat, et_f32, tril, *shards)
    return outs[0], outs[1], outs[2], outs[3], outs[4:]


def _mixer_bwd(proj, dy, ypre, conv_pre, states, cw, dt_bias, a_log, d_x, nw, pool_w, pool_scale, nb, chip_sums):
    m = proj.shape[0]
    nc = m // nb // CHUNK
    e_mat, et_mat, et_f32, tril, triu = _ssd_constants()
    hb = CHUNK // HALO
    ns = len(chip_sums)

    def body(p_ref, halo_ref, dy_ref, ypre_ref, pre_ref, st_ref, cw_ref, dtb_ref, alog_ref, dx_ref, nw_ref, pw_ref,
             ps_ref, e_ref, et_ref, etf_ref, tril_ref, triu_ref, *rest):
        cs_refs = rest[:ns]
        dp_ref, gcw_ref, gcb_ref, gdtb_ref, galog_ref, gd_ref, gnw_ref, gpw_ref, gps_ref = rest[ns:ns + 9]
        part_refs, (ds_carry, dpre_next, dq_next) = rest[ns + 9:2 * ns + 9], rest[2 * ns + 9:2 * ns + 12]
        exchange = _ChipExchange(cs_refs, part_refs, [False] * ns, *rest[2 * ns + 12:])
        b = pl.program_id(0)
        cc = pl.program_id(1)
        c = nc - 1 - cc

        @pl.when((b == 0) & (cc == 0))
        def _():
            exchange.start()
            for r in (gcw_ref, gcb_ref, gdtb_ref, galog_ref, gd_ref, gnw_ref, gpw_ref, gps_ref):
                r[...] = jnp.zeros_like(r)

        @pl.when(cc == 0)
        def _():
            ds_carry[...] = jnp.zeros_like(ds_carry)
            dpre_next[...] = jnp.zeros_like(dpre_next)
            dq_next[...] = jnp.zeros_like(dq_next)

        valid = (c > 0) | (lax.broadcasted_iota(jnp.int32, (CHUNK, 1), 0) >= PAD_ROWS)
        first = c > 0

        u = p_ref[:, 0:D_POOL]
        u_halo = jnp.where(first, halo_ref[...], 0.0)
        inv_cnt, lane = _pool_inv_count(c)
        pooled = _pool_window_sums(jnp.concatenate([u_halo, u], axis=0), lane) * inv_cnt - u
        dyp = dy_ref[:, 0:D_POOL]
        ps = ps_ref[...]
        dmixed = dyp * ps
        mixed, dpooled = [], []
        for g in range(len(POOL_WINDOWS)):
            gsl = slice(g * 128, (g + 1) * 128)
            pw = pw_ref[g]
            mixed.append(_dot(pooled[:, gsl], pw))
            dpooled.append(_dot_nt(dmixed[:, gsl], pw))
            gpw_ref[g] += _dot_tn(pooled[:, gsl], dmixed[:, gsl])
        gps_ref[...] += jnp.sum(dyp * jnp.concatenate(mixed, axis=1), axis=0, keepdims=True)
        dpooled = jnp.concatenate(dpooled, axis=1)
        dq = dpooled * inv_cnt
        du = _pool_window_sums_ahead(jnp.concatenate([dq, dq_next[...]], axis=0), lane) - dpooled
        dq_next[...] = dq[0:HALO, :]
        dp_ref[:, 0:D_POOL] = _mx(du)

        yp = ypre_ref[...]
        z = p_ref[:, OFF_Z:OFF_Z + D_SSM]
        sz, dsz = _silu_and_grad(z)
        yz = yp * sz
        do = dy_ref[:, D_POOL:]
        nw_row = nw_ref[...]
        dyz = []
        gnw = []
        for g in range(N_GROUPS):
            gs = slice(g * GROUP_CH, (g + 1) * GROUP_CH)
            r = lax.rsqrt(jnp.mean(yz[:, gs] * yz[:, gs], axis=-1, keepdims=True) + EPS)
            n = yz[:, gs] * r
            gnw.append(jnp.sum(do[:, gs] * n, axis=0, keepdims=True))
            dn = do[:, gs] * nw_row[:, gs]
            dyz.append(r * (dn - n * jnp.mean(dn * n, axis=-1, keepdims=True)))
        gnw_ref[...] += jnp.concatenate(gnw, axis=1)
        dyz = jnp.concatenate(dyz, axis=1)
        dp_ref[:, OFF_Z:OFF_Z + D_SSM] = _mx(dyz * yp * dsz)
        dyp_ssm = dyz * sz

        xc, dsilu = _silu_and_grad(pre_ref[...])
        dtr = p_ref[:, OFF_DT:OFF_DT + 128]
        dt, a_row, a_col, dt_pre, head = _dt_and_cumsum(dtr, dtb_ref[...], alog_ref[...], valid, tril_ref[...])
        dxs, dbm, dcm, ddt, da, dd, ds_prev = _ssd_chunk_bwd(
            xc[:, 0:D_SSM], xc[:, D_SSM:D_SSM + 512], xc[:, D_SSM + 512:], dt, a_row, a_col, st_ref[0],
            ds_carry[...], dyp_ssm, dx_ref[...], e_ref[...], et_ref[...], etf_ref[...], triu_ref[...])
        ds_carry[...] = ds_prev
        gd_ref[...] += dd
        galog_ref[...] += da * a_row
        ddtr = jnp.where(valid & head, ddt * _sigmoid(dt_pre), 0.0)
        gdtb_ref[...] += jnp.sum(ddtr, axis=0, keepdims=True)
        dp_ref[:, OFF_DT:OFF_DT + 128] = _mx(ddtr)

        dpre = jnp.concatenate([dxs, dbm, dcm], axis=1) * dsilu
        gcb_ref[...] += jnp.sum(dpre, axis=0, keepdims=True)
        dext = jnp.concatenate([dpre, dpre_next[...]], axis=0)
        dpre_next[...] = dpre[0:HALO, :]
        ups = [_shift_up(dext, 3 - k) for k in range(4)]
        xbc = p_ref[:, OFF_X:OFF_X + D_XBC]
        gcw_ref[...] += jnp.concatenate([jnp.sum(xbc * ups[k], axis=0, keepdims=True) for k in range(4)], axis=0)
        cw = cw_ref[...]
        dp_ref[:, OFF_X:OFF_X + D_XBC] = _mx(cw[3:4, :] * ups[3] + cw[2:3, :] * ups[2]
                                             + cw[1:2, :] * ups[1] + cw[0:1, :] * ups[0])

        @pl.when((b == nb - 1) & (cc == nc - 1))
        def _():
            exchange.finish()

    blk = lambda w: pl.BlockSpec((CHUNK, w), lambda b, cc: (_chunk_block(b, nc - 1 - cc, nb, nc), 0))
    halo = pl.BlockSpec((HALO, D_POOL),
                        lambda b, cc: (_chunk_block(b, jnp.maximum(nc - 2 - cc, 0), nb, nc) * hb + hb - 1, 0))
    hbm = pl.BlockSpec(memory_space=pl.ANY)
    outs = pl.pallas_call(
        body, grid=(nb, nc), name="mixer_bwd",
        in_specs=[blk(PROJ_W), halo, blk(D_MIX), blk(D_SSM), blk(D_XBC),
                  pl.BlockSpec((1, D_SSM, D_STATE), lambda b, cc: (b * nc + nc - 1 - cc, 0, 0)),
                  _full((4, D_XBC)), _full((1, 128)), _full((1, 128)), _full((1, D_SSM)),
                  _full((1, D_SSM)), _full((4, 128, 128)), _full((1, D_POOL)),
                  _full((128, D_SSM)), _full((D_SSM, 128)), _full((D_SSM, 128)), _full((CHUNK, CHUNK)),
                  _full((CHUNK, CHUNK))] + [hbm] * ns,
        out_specs=[blk(PROJ_W), _full((4, D_XBC)), _full((1, D_XBC)), _full((1, 128)), _full((1, 128)), _full((1, 128)),
                   _full((1, D_SSM)), _full((4, 128, 128)), _full((1, D_POOL))] + [hbm] * ns,
        out_shape=[jax.ShapeDtypeStruct((m, PROJ_W), MXU_DTYPE), jax.ShapeDtypeStruct((4, D_XBC), F32),
                   jax.ShapeDtypeStruct((1, D_XBC), F32), jax.ShapeDtypeStruct((1, 128), F32),
                   jax.ShapeDtypeStruct((1, 128), F32), jax.ShapeDtypeStruct((1, 128), F32),
                   jax.ShapeDtypeStruct((1, D_SSM), F32), jax.ShapeDtypeStruct((4, 128, 128), F32),
                   jax.ShapeDtypeStruct((1, D_POOL), F32)] + _ChipExchange.out_shapes(chip_sums, [False] * ns),
        scratch_shapes=[pltpu.VMEM((D_SSM, D_STATE), F32), pltpu.VMEM((HALO, D_XBC), F32),
                        pltpu.VMEM((HALO, D_POOL), F32)] + _ChipExchange.scratch(ns),
        compiler_params=_params("arbitrary", "arbitrary"),
    )(proj, proj, dy, ypre, conv_pre, states, cw, dt_bias, a_log, d_x, nw, pool_w, pool_scale, e_mat, et_mat, et_f32, tril, triu,
      *chip_sums)
    return outs[:9], outs[9:]


MESH_IDS = pl.DeviceIdType.MESH
_HBM = pl.BlockSpec(memory_space=pltpu.HBM)


def _coords():
    return lax.axis_index("x"), lax.axis_index("y"), lax.axis_index("c")


def _other_chips(x, y):
    return [(1 - x, y), (x, 1 - y), (1 - x, 1 - y)]


class _Gather:
    def __init__(self, ins, outs, by_cols, send_sems, recv_sems, local_sems):
        self.ins, self.outs, self.by_cols, self.n = ins, outs, by_cols, len(ins)
        self.send_sems, self.recv_sems, self.local_sems = send_sems, recv_sems, local_sems
        self.x, self.y, self.c = _coords()
        self.me, self.sibling = (self.x, self.y, self.c), (self.x, self.y, 1 - self.c)
        self.chips = _other_chips(self.x, self.y)

    @staticmethod
    def scratch(n):
        return [pltpu.SemaphoreType.DMA((7 * n,)), pltpu.SemaphoreType.DMA((7 * n,)), pltpu.SemaphoreType.DMA((n,))]

    @staticmethod
    def out_shapes(shards, by_cols):
        return [jax.ShapeDtypeStruct((s.shape[0], N_DEV * s.shape[1]) if cols else (N_DEV,) + s.shape, s.dtype)
                for s, cols in zip(shards, by_cols)]

    def _block(self, t, device):
        idx = 4 * device[0] + 2 * device[1] + device[2]
        if not self.by_cols[t]:
            return self.outs[t].at[idx]
        w = self.ins[t].shape[1]
        return self.outs[t].at[:, pl.ds(pl.multiple_of(idx * w, w), w)]

    def _copy(self, t, k, block, to, own=False):
        dst = self._block(t, block)
        return pltpu.make_async_remote_copy(
            src_ref=self.ins[t] if own else dst, dst_ref=dst, send_sem=self.send_sems.at[t * 7 + k],
            recv_sem=self.recv_sems.at[t * 7 + k], device_id=to, device_id_type=MESH_IDS)

    def _mine(self):
        return [pltpu.make_async_copy(self.ins[t], self._block(t, self.me), self.local_sems.at[t])
                for t in range(self.n)]

    def _first(self):
        cps = []
        for t in range(self.n):
            cps.append(self._copy(t, 0, self.me, self.sibling, own=True))
            cps += [self._copy(t, 1 + j, self.me, (*chip, self.c), own=True) for j, chip in enumerate(self.chips)]
        return cps

    def _passed(self):
        return [self._copy(t, 4 + j, (*chip, self.c), self.sibling)
                for j, chip in enumerate(self.chips) for t in range(self.n)]

    def start(self):
        for cp in self._mine() + self._first():
            cp.start()

    def forward(self):
        for j, chip in enumerate(self.chips):
            for t in range(self.n):
                self._copy(t, 1 + j, (*chip, self.c), self.me).wait_recv()
                self._copy(t, 4 + j, (*chip, self.c), self.sibling).start()

    def finish(self):
        for t in range(self.n):
            self._copy(t, 0, self.sibling, self.me).wait_recv()
            for j, chip in enumerate(self.chips):
                self._copy(t, 4 + j, (*chip, 1 - self.c), self.me).wait_recv()
        for cp in self._first() + self._passed():
            cp.wait_send()
        for cp in self._mine():
            cp.wait()


def _weight_gather(shards):
    n = len(shards)

    def body(*refs):
        g = _Gather(refs[:n], refs[n:2 * n], [False] * n, *refs[2 * n:])
        g.start()
        g.forward()
        g.finish()

    return pl.pallas_call(
        body, name="weight_gather",
        in_specs=[_HBM] * n, out_specs=[_HBM] * n,
        out_shape=_Gather.out_shapes(shards, [False] * n),
        scratch_shapes=_Gather.scratch(n),
    )(*shards)


def _owner_blocks(g, by_cols):
    return (g.shape[0], g.shape[1] // N_DEV) if by_cols else g.shape[2:]


def _grad_exchange_d2d(gs, by_cols, name):
    n = len(gs)

    def body(*refs):
        ins, got = refs[:n], refs[n:2 * n]
        send_sems, recv_sems = refs[2 * n:]
        x, y, c = _coords()

        def src(t, k):
            if not by_cols[t]:
                return ins[t].at[k, 1 - c]
            w = ins[t].shape[1] // N_DEV
            return ins[t].at[:, pl.ds(pl.multiple_of((2 * k + 1 - c) * w, w), w)]

        remote = [pltpu.make_async_remote_copy(
            src_ref=src(t, k), dst_ref=got[t].at[k], send_sem=send_sems.at[t * 4 + k],
            recv_sem=recv_sems.at[t * 4 + k], device_id=(x, y, 1 - c), device_id_type=MESH_IDS)
            for t in range(n) for k in range(4)]
        for cp in remote:
            cp.start()
        for cp in remote:
            cp.wait_recv()
        for cp in remote:
            cp.wait_send()

    return pl.pallas_call(
        body, name=name,
        in_specs=[_HBM] * n, out_specs=[_HBM] * n,
        out_shape=[jax.ShapeDtypeStruct((4,) + _owner_blocks(g, cols), g.dtype) for g, cols in zip(gs, by_cols)],
        scratch_shapes=[pltpu.SemaphoreType.DMA((4 * n,)), pltpu.SemaphoreType.DMA((4 * n,))],
    )(*gs)


def _small_allreduce(pack):
    rows = pack.shape[0]

    def body(p_ref, o_ref, sib_ref, parts_ref, send_sems, recv_sems):
        x, y, c = _coords()
        my_chip = 2 * x + y
        swap = pltpu.make_async_remote_copy(src_ref=p_ref, dst_ref=sib_ref, send_sem=send_sems.at[0],
                                            recv_sem=recv_sems.at[0], device_id=(x, y, 1 - c), device_id_type=MESH_IDS)
        swap.start()
        swap.wait_recv()
        parts_ref[my_chip] = p_ref[...] + sib_ref[...]
        remote = [pltpu.make_async_remote_copy(
            src_ref=parts_ref.at[my_chip], dst_ref=parts_ref.at[my_chip], send_sem=send_sems.at[1 + j],
            recv_sem=recv_sems.at[1 + j], device_id=(cx, cy, c), device_id_type=MESH_IDS)
            for j, (cx, cy) in enumerate(_other_chips(x, y))]
        for cp in remote:
            cp.start()
        for j, (cx, cy) in enumerate(_other_chips(x, y)):
            slot = parts_ref.at[2 * cx + cy]
            pltpu.make_async_remote_copy(src_ref=slot, dst_ref=slot, send_sem=send_sems.at[1 + j],
                                         recv_sem=recv_sems.at[1 + j], device_id=(cx, cy, c),
                                         device_id_type=MESH_IDS).wait_recv()
        o_ref[...] = ((parts_ref[0] + parts_ref[1]) + parts_ref[2]) + parts_ref[3]
        swap.wait_send()
        for cp in remote:
            cp.wait_send()

    vmem = pl.BlockSpec(memory_space=pltpu.VMEM)
    return pl.pallas_call(
        body, name="small_allreduce", in_specs=[vmem], out_specs=vmem,
        out_shape=jax.ShapeDtypeStruct((rows, 128), F32),
        scratch_shapes=[pltpu.VMEM((rows, 128), F32), pltpu.VMEM((4, rows, 128), F32),
                        pltpu.SemaphoreType.DMA((4,)), pltpu.SemaphoreType.DMA((4,))],
    )(pack)


class _ChipExchange:
    def __init__(self, ins, outs, whole, send_sems, recv_sems, local_sems):
        self.ins, self.outs, self.whole, self.n = ins, outs, whole, len(ins)
        self.send_sems, self.recv_sems, self.local_sems = send_sems, recv_sems, local_sems
        self.x, self.y, self.c = _coords()
        self.my_chip = 2 * self.x + self.y
        self.chips = _other_chips(self.x, self.y)

    @staticmethod
    def scratch(n):
        return [pltpu.SemaphoreType.DMA((3 * n,)), pltpu.SemaphoreType.DMA((3 * n,)), pltpu.SemaphoreType.DMA((n,))]

    def _src(self, t, k):
        return self.ins[t] if self.whole[t] else self.ins[t].at[k]

    def _local(self):
        return [pltpu.make_async_copy(self._src(t, self.my_chip), self.outs[t].at[self.my_chip], self.local_sems.at[t])
                for t in range(self.n)]

    def _remote(self):
        return [pltpu.make_async_remote_copy(
            src_ref=self._src(t, 2 * cx + cy), dst_ref=self.outs[t].at[self.my_chip],
            send_sem=self.send_sems.at[t * 3 + j], recv_sem=self.recv_sems.at[t * 3 + j],
            device_id=(cx, cy, self.c), device_id_type=MESH_IDS)
            for t in range(self.n) for j, (cx, cy) in enumerate(self.chips)]

    def start(self):
        for cp in self._remote() + self._local():
            cp.start()

    def finish(self):
        for t in range(self.n):
            for j, (cx, cy) in enumerate(self.chips):
                slot = self.outs[t].at[2 * cx + cy]
                pltpu.make_async_remote_copy(
                    src_ref=slot, dst_ref=slot, send_sem=self.send_sems.at[t * 3 + j],
                    recv_sem=self.recv_sems.at[t * 3 + j], device_id=(cx, cy, self.c),
                    device_id_type=MESH_IDS).wait_recv()
        for cp in self._remote():
            cp.wait_send()
        for cp in self._local():
            cp.wait()

    @staticmethod
    def out_shapes(arrs, whole):
        return [jax.ShapeDtypeStruct(((4,) + a.shape) if w else a.shape, a.dtype) for a, w in zip(arrs, whole)]


def _row_tile(rows, cols, n_arrays):
    budget = 24 * 1024 * 1024
    padded = -(-cols // 128) * 128
    step = 16 if rows % 16 == 0 else 8
    tr = max(step, budget // (n_arrays * 2 * 4 * padded) // step * step)
    while rows % tr:
        tr -= step
    return tr


def _chip_sum(g, by_cols, got, core, name):
    rows, cols = _owner_blocks(g, by_cols)
    tr = _row_tile(rows, cols, 3)

    def body(c_ref, a_ref, b_ref, o_ref):
        o_ref[...] = (a_ref[...] + b_ref[...].astype(F32)).astype(o_ref.dtype)

    own = (pl.BlockSpec((tr, cols), lambda k, i, c: (i, 2 * k + c[0])) if by_cols
           else pl.BlockSpec((None, None, tr, cols), lambda k, i, c: (k, c[0], i, 0)))
    grid_spec = pltpu.PrefetchScalarGridSpec(
        num_scalar_prefetch=1, grid=(4, rows // tr),
        in_specs=[own, pl.BlockSpec((None, tr, cols), lambda k, i, c: (k, i, 0))],
        out_specs=pl.BlockSpec((None, tr, cols), lambda k, i, c: (k, i, 0)))
    return pl.pallas_call(body, grid_spec=grid_spec, name=name,
                          out_shape=jax.ShapeDtypeStruct((4, rows, cols), MXU_DTYPE),
                          compiler_params=_params("arbitrary", "arbitrary"))(core, g, got)


def _adamw_math(w, g, m, v):
    m2 = ADAM_B1 * m + (1.0 - ADAM_B1) * g
    v2 = ADAM_B2 * v + (1.0 - ADAM_B2) * (g * g)
    m_hat = m2 / (1.0 - ADAM_B1 ** ADAM_STEP)
    v_hat = v2 / (1.0 - ADAM_B2 ** ADAM_STEP)
    delta = -ADAM_LR * (m_hat / (jnp.sqrt(v_hat) + ADAM_EPS) + ADAM_WD * w)
    return delta, m2, v2


def _adamw(parts, w, m, v, name):
    rows, cols = w.shape
    tr = _row_tile(rows, cols, 11)

    def body(p_ref, w_ref, m_ref, v_ref, g_ref, d_ref, m2_ref, v2_ref):
        part = lambda k: p_ref[k].astype(F32)
        g = ((part(0) + part(1)) + part(2)) + part(3)
        d, m2, v2 = _adamw_math(w_ref[...], g, m_ref[...], v_ref[...])
        g_ref[...] = g
        d_ref[...] = d
        m2_ref[...] = m2
        v2_ref[...] = v2

    blk = pl.BlockSpec((tr, cols), lambda i: (i, 0))
    out = jax.ShapeDtypeStruct((rows, cols), F32)
    return pl.pallas_call(body, grid=(rows // tr,), name=name,
                          in_specs=[pl.BlockSpec((4, tr, cols), lambda i: (0, i, 0)), blk, blk, blk],
                          out_specs=[blk] * 4, out_shape=[out] * 4,
                          compiler_params=_params("arbitrary"))(parts, w, m, v)


def _adamw_small(gs, ws, ms, vs):
    n = len(ws)

    def body(*refs):
        g_refs, w_refs, m_refs, v_refs = (refs[k * n:(k + 1) * n] for k in range(4))
        d_refs, m2_refs, v2_refs = (refs[(4 + k) * n:(5 + k) * n] for k in range(3))
        for t in range(n):
            d, m2, v2 = _adamw_math(w_refs[t][...], g_refs[t][...], m_refs[t][...], v_refs[t][...])
            d_refs[t][...] = d
            m2_refs[t][...] = m2
            v2_refs[t][...] = v2

    outs = pl.pallas_call(body, name="adamw_small",
                          out_shape=[jax.ShapeDtypeStruct(w.shape, F32) for w in ws] * 3)(*gs, *ws, *ms, *vs)
    return outs[:n], outs[n:2 * n], outs[2 * n:]


_PACK_TILE = 8 * 128


def _pack(arrays):
    rows = []
    for a in arrays:
        flat = a.astype(F32).reshape(-1)
        rows.append(jnp.pad(flat, (0, -flat.shape[0] % _PACK_TILE)).reshape(-1, 128))
    return jnp.concatenate(rows, axis=0)


def _unpack(pack, shapes):
    out, r = [], 0
    for s in shapes:
        n = int(np.prod(s))
        out.append(pack[r:r + -(-n // 128)].reshape(-1)[:n].reshape(s))
        r += -(-n // _PACK_TILE) * 8
    return out


def _pad128(v):
    v = v.reshape(1, -1).astype(F32)
    return jnp.pad(v, ((0, 0), (0, 128 - v.shape[1])))


_WEIGHTS = ["meta", "norm_mix_w", "w_in", "pool_w", "pool_scale", "conv_w", "conv_b", "dt_bias", "a_log", "d_skip",
            "ssm_norm_w", "w_out", "norm_ffn_w", "w_ff1", "w_ff2", "norm_f_w"]
_BIG = ["w_in", "w_out", "w_ff1", "w_ff2"]
_SMALL = [n for n in _WEIGHTS if n not in _BIG]


def kernel(x, meta, norm_mix_w, w_in, pool_w, pool_scale, conv_w, conv_b, dt_bias, a_log, d_skip, ssm_norm_w, w_out, norm_ffn_w, w_ff1, w_ff2, norm_f_w, loss_target, m_meta, m_norm_mix_w, m_w_in, m_pool_w, m_pool_scale, m_conv_w, m_conv_b, m_dt_bias, m_a_log, m_d_skip, m_ssm_norm_w, m_w_out, m_norm_ffn_w, m_w_ff1, m_w_ff2, m_norm_f_w, v_meta, v_norm_mix_w, v_w_in, v_pool_w, v_pool_scale, v_conv_w, v_conv_b, v_dt_bias, v_a_log, v_d_skip, v_ssm_norm_w, v_w_out, v_norm_ffn_w, v_w_ff1, v_w_ff2, v_norm_f_w):
    wts = dict(meta=meta, norm_mix_w=norm_mix_w, w_in=w_in, pool_w=pool_w, pool_scale=pool_scale, conv_w=conv_w,
               conv_b=conv_b, dt_bias=dt_bias, a_log=a_log, d_skip=d_skip, ssm_norm_w=ssm_norm_w, w_out=w_out,
               norm_ffn_w=norm_ffn_w, w_ff1=w_ff1, w_ff2=w_ff2, norm_f_w=norm_f_w)
    mom1 = dict(zip(_WEIGHTS, (m_meta, m_norm_mix_w, m_w_in, m_pool_w, m_pool_scale, m_conv_w, m_conv_b, m_dt_bias,
                               m_a_log, m_d_skip, m_ssm_norm_w, m_w_out, m_norm_ffn_w, m_w_ff1, m_w_ff2, m_norm_f_w)))
    mom2 = dict(zip(_WEIGHTS, (v_meta, v_norm_mix_w, v_w_in, v_pool_w, v_pool_scale, v_conv_w, v_conv_b, v_dt_bias,
                               v_a_log, v_d_skip, v_ssm_norm_w, v_w_out, v_norm_ffn_w, v_w_ff1, v_w_ff2, v_norm_f_w)))
    xi, yi, ci = _coords()
    dev = 4 * xi + 2 * yi + ci
    win_cols = w_in.shape[-1]
    cw_cols = conv_w.shape[-1]

    nb, seq, _ = x.shape
    core = jnp.reshape(ci, (1,)).astype(jnp.int32)
    owners = lambda a: a.reshape((4, 2) + a.shape[1:])

    lead_pack = jnp.zeros((N_META, 512), F32)
    lead_pack = lead_pack.at[:, :128].set(meta).at[:4, 128:128 + cw_cols].set(conv_w[0])
    g_win, g_lead = _weight_gather([_mx(w_in[0]), lead_pack])
    win_full = _assemble_bands(g_win, PROJ_W)
    meta_full = jnp.transpose(g_lead[:, :, :128], (1, 0, 2)).reshape(N_META, D_MODEL)
    cw_full = jnp.transpose(g_lead[:, :4, 128:128 + cw_cols], (1, 0, 2)).reshape(4, D_XBC)

    lead = jnp.concatenate([jnp.zeros((PAD_ROWS, D_MODEL), F32), meta_full] * nb, axis=0)
    x_rows = x.reshape(nb * seq, D_MODEL)
    tgt_rows = loss_target.reshape(nb * seq, D_MODEL)
    dt_bias_p, a_log_p = _pad128(dt_bias), _pad128(a_log)
    d_x = jnp.repeat(d_skip.reshape(1, N_HEADS).astype(F32), HEAD_DIM, axis=1)
    norm_f_row = norm_f_w.reshape(1, D_MODEL)

    hn1, proj = _in_proj(x_rows, lead, norm_mix_w, win_full)
    late_cols = [False, True, False]
    y, ypre, conv_pre, states, (g_wout, wff1_full, g_wff2) = _mixer_fwd(
        proj, cw_full, conv_b, dt_bias_p, a_log_p, d_x, ssm_norm_w, pool_w[0], pool_scale, nb,
        [_mx(w_out[0]), _mx(w_ff1[0]), _mx(w_ff2[0])], late_cols)
    wout_full = g_wout.reshape(D_MIX, D_MODEL)
    wff2_full = g_wff2.reshape(D_FF, D_MODEL)
    loss, gr_nf, gr_nffn, ff, da, hn2, dh1, dh2, dy = _ffn_fwd_bwd(
        x_rows, lead, y, tgt_rows, wout_full, norm_ffn_w, wff1_full, wff2_full, norm_f_row)
    gr_wff2 = _tn_matmul(ff, dh2, "grad_w_ff2", tka=1024, max_slab=2816)
    gr_wff1 = _tn_matmul(hn2, da, "grad_w_ff1", tka=512)
    gr_wout = _tn_matmul(y, dh1, "grad_w_out", tka=1024, max_slab=2816)

    by_owner = lambda k: [owners(gr_wout[k].reshape(N_DEV, D_MIX // N_DEV, D_MODEL)), gr_wff1[k],
                          owners(gr_wff2[k].reshape(N_DEV, D_FF // N_DEV, D_MODEL))]
    late_parts = by_owner(0)
    late_got = _grad_exchange_d2d(by_owner(1), late_cols, "grad_exchange_d2d_late")
    late_sums = [_chip_sum(late_parts[t], late_cols[t], late_got[t], core, "chip_sum_late_%d" % t) for t in range(3)]
    (dproj, gr_cw, gr_cb, gr_dtb, gr_alog, gr_d, gr_nw, gr_pw, gr_ps), late_exchanged = _mixer_bwd(
        proj, dy, ypre, conv_pre, states, cw_full, dt_bias_p, a_log_p, d_x, ssm_norm_w, pool_w[0], pool_scale, nb,
        late_sums)

    win_parts = [owners(_tn_matmul_banded(hn1, dproj, "grad_w_in", win_cols, tka=512))]
    win_got = _grad_exchange_d2d(win_parts, [False], "grad_exchange_d2d_w_in")
    win_sum = _chip_sum(win_parts[0], False, win_got[0], core, "chip_sum_w_in")
    gx_rows, gr_nmix, gr_meta, win_exchanged = _in_proj_bwd(dproj, x_rows, lead, dh1, norm_mix_w, win_full, [win_sum])
    parts = dict(w_in=win_exchanged[0], w_out=late_exchanged[0], w_ff1=late_exchanged[1], w_ff2=late_exchanged[2])

    small_full = dict(meta=gr_meta, norm_mix_w=gr_nmix, pool_w=gr_pw, pool_scale=gr_ps, conv_w=gr_cw, conv_b=gr_cb,
                      dt_bias=gr_dtb[:, :N_HEADS], a_log=gr_alog[:, :N_HEADS], d_skip=gr_d[:, :N_HEADS],
                      ssm_norm_w=gr_nw, norm_ffn_w=gr_nffn, norm_f_w=gr_nf, loss=loss[0:1, 0:1])
    small_names = list(small_full)
    small_sum = _small_allreduce(_pack([small_full[n] for n in small_names]))
    gs = dict(zip(small_names, _unpack(small_sum, [small_full[n].shape for n in small_names])))
    gs["meta"] = lax.dynamic_slice_in_dim(gs["meta"], dev * 128, 128, axis=1)
    gs["conv_w"] = lax.dynamic_slice_in_dim(gs["conv_w"], dev * cw_cols, cw_cols, axis=1)

    res = {}
    for n in _BIG:
        shp = wts[n].shape
        res[n] = [o.reshape(shp) for o in _adamw(parts[n], wts[n][0], mom1[n][0], mom2[n][0], "adamw_" + n)]
    as2d = lambda a: a.reshape(-1, a.shape[-1])
    small_g = [as2d(gs[n].reshape(wts[n].shape)) for n in _SMALL]
    small_out = _adamw_small(small_g, *[[as2d(d[n]) for n in _SMALL] for d in (wts, mom1, mom2)])
    for k, n in enumerate(_SMALL):
        res[n] = [o[k].reshape(wts[n].shape) for o in (small_g,) + tuple(small_out)]

    grad_x = gx_rows.reshape(nb, seq, D_MODEL)
    return (gs["loss"][0, 0], grad_x, *[res[n][0] for n in _WEIGHTS], *[res[n][1] for n in _WEIGHTS],
            *[res[n][2] for n in _WEIGHTS], *[res[n][3] for n in _WEIGHTS])
```

```python
import functools

import numpy as np
import jax
import jax.numpy as jnp
from jax import lax
from jax.experimental import pallas as pl
from jax.experimental.pallas import tpu as pltpu

F32 = jnp.float32
MXU_DTYPE = jnp.bfloat16

D_MODEL = 1024
D_POOL = 512
D_SSM = 1536
D_XBC = 2560
N_HEADS = 24
HEAD_DIM = 64
N_GROUPS = 4
GROUP_CH = D_SSM // N_GROUPS
D_STATE = 128
CHUNK = 128
N_META = 16
LEAD = CHUNK
PAD_ROWS = LEAD - N_META
ROW_TILE = 2 * CHUNK
D_MIX = D_POOL + D_SSM
D_FF = 4096
PROJ_W = 4736
OFF_Z = D_POOL
OFF_X = D_POOL + D_SSM
OFF_DT = OFF_X + D_XBC
D_IN_PROJ = OFF_DT + N_HEADS
POOL_WINDOWS = (2, 4, 8, 16)
HALO = 16
EPS = 1e-5
N_DEV = 8

ADAM_LR, ADAM_B1, ADAM_B2, ADAM_EPS, ADAM_WD, ADAM_STEP = 0.001, 0.9, 0.999, 1e-08, 0.01, 10

VMEM_LIMIT = 60 * 1024 * 1024


def _mx(a):
    return a.astype(MXU_DTYPE)


def _dot(a, b):
    return jnp.dot(_mx(a), _mx(b), preferred_element_type=F32)


def _dot_nt(a, b):
    return lax.dot_general(_mx(a), _mx(b), (((1,), (1,)), ((), ())), preferred_element_type=F32)


def _dot_tn(a, b):
    return lax.dot_general(_mx(a), _mx(b), (((0,), (0,)), ((), ())), preferred_element_type=F32)


def _split3(x):
    hi = x.astype(MXU_DTYPE)
    r = x - hi.astype(F32)
    mid = r.astype(MXU_DTYPE)
    lo = (r - mid.astype(F32)).astype(MXU_DTYPE)
    return hi, mid, lo


def _exact_l(c, x):
    hi, mid, lo = _split3(x)
    f = lambda p: jnp.dot(c, p, preferred_element_type=F32)
    return f(hi) + f(mid) + f(lo)


def _exact_r(x, c):
    hi, mid, lo = x if isinstance(x, tuple) else _split3(x)
    f = lambda p: jnp.dot(p, c, preferred_element_type=F32)
    return f(hi) + f(mid) + f(lo)


def _contract(x, c):
    hi = x.astype(MXU_DTYPE)
    lo = (x - hi.astype(F32)).astype(MXU_DTYPE)
    return jnp.dot(hi, c, preferred_element_type=F32) + jnp.dot(lo, c, preferred_element_type=F32)


def _sigmoid(x):
    return jax.nn.sigmoid(x)


def _softplus(x):
    return jnp.maximum(x, 0.0) + jnp.log1p(jnp.exp(-jnp.abs(x)))


def _silu_and_grad(x):
    s = _sigmoid(x)
    y = x * s
    return y, s + y * (1.0 - s)


def _shift_up(ext, s):
    if s == 0:
        return ext[:CHUNK, :]
    return pltpu.roll(ext, ext.shape[0] - s, 0)[:CHUNK, :]


def _by_pool_group(lane, a2, a4, a8, a16):
    return jnp.where(lane < 128, a2, jnp.where(lane < 256, a4, jnp.where(lane < 384, a8, a16)))


def _pool_inv_count(chunk_idx):
    row = lax.broadcasted_iota(jnp.int32, (CHUNK, D_POOL), 0)
    lane = lax.broadcasted_iota(jnp.int32, (CHUNK, D_POOL), 1)
    pos1 = jnp.maximum(chunk_idx * CHUNK + row - (PAD_ROWS - 1), 1)
    w = _by_pool_group(lane, 2, 4, 8, 16)
    return 1.0 / jnp.minimum(pos1, w).astype(F32), lane


def _pool_window_sums(u_ext, lane):
    s2 = u_ext + pltpu.roll(u_ext, 1, 0)
    s4 = s2 + pltpu.roll(s2, 2, 0)
    s8 = s4 + pltpu.roll(s4, 4, 0)
    s16 = s8 + pltpu.roll(s8, 8, 0)
    return _by_pool_group(lane, s2[HALO:], s4[HALO:], s8[HALO:], s16[HALO:])


def _pool_window_sums_ahead(q_ext, lane):
    n = q_ext.shape[0]
    r2 = q_ext + pltpu.roll(q_ext, n - 1, 0)
    r4 = r2 + pltpu.roll(r2, n - 2, 0)
    r8 = r4 + pltpu.roll(r4, n - 4, 0)
    r16 = r8 + pltpu.roll(r8, n - 8, 0)
    return _by_pool_group(lane, r2[:CHUNK], r4[:CHUNK], r8[:CHUNK], r16[:CHUNK])


def _conv_pre(ext, xbc, cw, cb):
    s1 = pltpu.roll(ext, 1, 0)
    near = cw[3:4, :] * xbc + cw[2:3, :] * s1[HALO:, :]
    far = cw[1:2, :] * ext + cw[0:1, :] * s1
    return cb + near + pltpu.roll(far, 2, 0)[HALO:, :]


def _dt_and_cumsum(dtr, dt_bias, a_log, valid, tril):
    lane = lax.broadcasted_iota(jnp.int32, (CHUNK, 128), 1)
    head = lane < N_HEADS
    pre = dtr + dt_bias
    dt = jnp.where(valid & head, _softplus(pre), 0.0)
    a_row = jnp.where(head[0:1, :], -jnp.exp(a_log), 0.0)
    a_col = _exact_l(tril, dt * a_row)
    return dt, a_row, a_col, pre, head


NEVER = -1e30


def _decay(a_col, a_row_t, h, causal):
    seg = a_col[:, h:h + 1] - a_row_t[h:h + 1, :]
    return jnp.exp(jnp.where(causal, seg, NEVER))


def _ssd_chunk_fwd(xs, bm, cm, dt, a_col, s_prev, d_x, e_mat, et_f32):
    lane = lax.broadcasted_iota(jnp.int32, (CHUNK, 128), 1)
    rowi = lax.broadcasted_iota(jnp.int32, (CHUNK, CHUNK), 0)
    coli = lax.broadcasted_iota(jnp.int32, (CHUNK, CHUNK), 1)
    causal = rowi >= coli
    a_row_t = a_col.T
    a_last = a_col[CHUNK - 1:CHUNK, :]
    a_split, dt_split = _split3(a_col), _split3(dt)
    ys, s_new = [], []
    for g in range(N_GROUPS):
        gs = slice(g * GROUP_CH, (g + 1) * GROUP_CH)
        e_g = e_mat[:, gs]
        xs_g = xs[:, gs]
        ax = _exact_r(a_split, e_g)
        xdt = xs_g * _exact_r(dt_split, e_g)
        w_end = xdt * jnp.exp(ax[CHUNK - 1:CHUNK, :] - ax)
        cd_col = jnp.exp(jnp.sum(et_f32[gs, :] * a_last, axis=1, keepdims=True))
        bg = bm[:, g * D_STATE:(g + 1) * D_STATE]
        cg = cm[:, g * D_STATE:(g + 1) * D_STATE]
        sg = s_prev[gs, :]
        cb = _dot_nt(cg, bg)
        y_off = _dot_nt(cg, sg) * jnp.exp(ax) + d_x[:, gs] * xs_g
        s_new.append(sg * cd_col + _dot_tn(w_end, bg))
        for pr in range(3):
            ps = slice(pr * 128, (pr + 1) * 128)
            h0 = g * 6 + pr * 2
            y0 = _dot(cb * _decay(a_col, a_row_t, h0, causal), xdt[:, ps])
            y1 = _dot(cb * _decay(a_col, a_row_t, h0 + 1, causal), xdt[:, ps])
            ys.append(jnp.where(lane < HEAD_DIM, y0, y1) + y_off[:, ps])
    return jnp.concatenate(ys, axis=1), jnp.concatenate(s_new, axis=0)


def _ssd_chunk_bwd(xs, bm, cm, dt, a_row, a_col, s_prev, ds_new, dy, d_x, e_mat, et_mat, et_f32, triu):
    lane = lax.broadcasted_iota(jnp.int32, (CHUNK, 128), 1)
    sub = lax.broadcasted_iota(jnp.int32, (CHUNK, 128), 0)
    rowi = lax.broadcasted_iota(jnp.int32, (CHUNK, CHUNK), 0)
    coli = lax.broadcasted_iota(jnp.int32, (CHUNK, CHUNK), 1)
    causal = rowi >= coli
    a_row_t = a_col.T
    a_last = a_col[CHUNK - 1:CHUNK, :]
    a_split, dt_split = _split3(a_col), _split3(dt)
    sub8 = lax.broadcasted_iota(jnp.int32, (8, GROUP_CH), 0)

    dxs, dbs, dcs, dsp = [], [], [], []
    zcol = jnp.zeros((CHUNK, 128), F32)
    zrows = []
    da_col = jnp.zeros((CHUNK, 128), F32)
    ddt = jnp.zeros((CHUNK, 128), F32)
    head_sums = jnp.zeros((8, 128), F32)
    q_row = jnp.zeros((1, 128), F32)
    for g in range(N_GROUPS):
        gs = slice(g * GROUP_CH, (g + 1) * GROUP_CH)
        e_g, et_g = e_mat[:, gs], et_mat[gs, :]
        xs_g, dy_g = xs[:, gs], dy[:, gs]
        ax = _exact_r(a_split, e_g)
        dtx = _exact_r(dt_split, e_g)
        xdt = xs_g * dtx
        dte = jnp.exp(ax[CHUNK - 1:CHUNK, :] - ax)
        w_end = xdt * dte
        cd_col = jnp.exp(jnp.sum(et_f32[gs, :] * a_last, axis=1, keepdims=True))
        dye = dy_g * jnp.exp(ax)
        bg = bm[:, g * D_STATE:(g + 1) * D_STATE]
        cg = cm[:, g * D_STATE:(g + 1) * D_STATE]
        sg = s_prev[gs, :]
        dsg = ds_new[gs, :]
        cb = _dot_nt(cg, bg)
        cs = _dot_nt(cg, sg)
        dcg = _dot(dye, sg)
        dsp.append(dsg * cd_col + _dot_tn(dye, cg))
        dwg = _dot_nt(bg, dsg)
        dbg = _dot(w_end, dsg)
        ww = dwg * w_end
        t1 = jnp.sum(dsg * sg, axis=1, keepdims=True) * cd_col
        dcb = jnp.zeros((CHUNK, CHUNK), F32)
        pairs = []
        for pr in range(3):
            ps = slice(pr * 128, (pr + 1) * 128)
            xdt_p, dy_p = xdt[:, ps], dy_g[:, ps]
            acc = None
            for half in range(2):
                h = g * 6 + pr * 2 + half
                ld = _decay(a_col, a_row_t, h, causal)
                gm = cb * ld
                dym = jnp.where((lane < HEAD_DIM) if half == 0 else (lane >= HEAD_DIM), dy_p, 0.0)
                dg = _dot_nt(dym, xdt_p)
                dseg = dg * gm
                dcb = dcb + dg * ld
                t = _dot_tn(gm, dym)
                acc = t if acc is None else acc + t
                zcol = jnp.where(lane == h, jnp.sum(dseg, axis=1, keepdims=True), zcol)
                zrows.append(jnp.sum(dseg, axis=0, keepdims=True))
            pairs.append(acc)
        dxdt = dwg * dte + jnp.concatenate(pairs, axis=1)
        dcs.append(dcg + _dot(dcb, bg))
        dbs.append(dbg + _dot_tn(dcb, cg))
        dxs.append(dxdt * dtx + d_x[:, gs] * dy_g)
        da_col = da_col + _contract(dye * cs - ww, et_g)
        ddt = ddt + _contract(dxdt * xs_g, et_g)
        col_sums = jnp.where(sub8 == 0, jnp.sum(dy_g * xs_g, axis=0, keepdims=True),
                             jnp.where(sub8 == 1, jnp.sum(ww, axis=0, keepdims=True), 0.0))
        head_sums = head_sums + _exact_r(col_sums, et_g)
        q_row = q_row + jnp.sum(et_f32[gs, :] * t1, axis=0, keepdims=True)

    dd = head_sums[0:1, :]
    q_row = q_row + head_sums[1:2, :]
    zrow = jnp.concatenate(zrows + [jnp.zeros((128 - N_HEADS, CHUNK), F32)], axis=0)
    da_col = da_col + zcol - zrow.T + jnp.where(sub == CHUNK - 1, q_row, 0.0)
    rc = _exact_l(triu, da_col)
    ddt = ddt + rc * a_row
    da = jnp.sum(rc * dt, axis=0, keepdims=True)
    return (jnp.concatenate(dxs, axis=1), jnp.concatenate(dbs, axis=1), jnp.concatenate(dcs, axis=1), ddt, da, dd,
            jnp.concatenate(dsp, axis=0))


def _ssd_constants():
    iota = lambda shape, d: lax.broadcasted_iota(jnp.int32, shape, d)
    e = iota((128, D_SSM), 0) == iota((128, D_SSM), 1) // HEAD_DIM
    et = iota((D_SSM, 128), 1) == iota((D_SSM, 128), 0) // HEAD_DIM
    tril = iota((CHUNK, CHUNK), 1) <= iota((CHUNK, CHUNK), 0)
    triu = iota((CHUNK, CHUNK), 1) >= iota((CHUNK, CHUNK), 0)
    return (e.astype(MXU_DTYPE), et.astype(MXU_DTYPE), et.astype(F32), tril.astype(MXU_DTYPE), triu.astype(MXU_DTYPE))


def _full(shape):
    nd = len(shape)
    return pl.BlockSpec(shape, lambda *_: (0,) * nd)


def _params(*sem):
    return pltpu.CompilerParams(dimension_semantics=sem, vmem_limit_bytes=VMEM_LIMIT)


def _token_tiles(width, n_tok_tiles):
    return pl.BlockSpec((ROW_TILE, width), lambda i: (jnp.minimum(i, n_tok_tiles - 1), 0))


def _in_proj(x, lead, w1, win):
    nt = x.shape[0] // ROW_TILE
    m = x.shape[0] + ROW_TILE
    tm = ROW_TILE

    def body(x_ref, lead_ref, w1_ref, win_hbm, hn_ref, proj_ref, win_v, sem):
        i = pl.program_id(0)

        @pl.when(i == 0)
        def _():
            cp = pltpu.make_async_copy(win_hbm, win_v, sem)
            cp.start()
            cp.wait()

        x = jnp.where(i == nt, lead_ref[...], x_ref[...])
        r = lax.rsqrt(jnp.mean(x * x, axis=-1, keepdims=True) + EPS)
        hn = _mx(x * r * w1_ref[...])
        hn_ref[...] = hn
        for j in range(0, PROJ_W, 512):
            w = min(512, PROJ_W - j)
            proj_ref[:, j:j + w] = jnp.dot(hn, win_v[:, j:j + w], preferred_element_type=F32)

    return pl.pallas_call(
        body, grid=(m // tm,), name="in_proj",
        in_specs=[_token_tiles(D_MODEL, nt), _full((ROW_TILE, D_MODEL)), _full((1, D_MODEL)),
                  pl.BlockSpec(memory_space=pl.ANY)],
        out_specs=[pl.BlockSpec((tm, D_MODEL), lambda i: (i, 0)), pl.BlockSpec((tm, PROJ_W), lambda i: (i, 0))],
        out_shape=[jax.ShapeDtypeStruct((m, D_MODEL), MXU_DTYPE), jax.ShapeDtypeStruct((m, PROJ_W), F32)],
        scratch_shapes=[pltpu.VMEM((D_MODEL, PROJ_W), MXU_DTYPE), pltpu.SemaphoreType.DMA],
        compiler_params=_params("arbitrary"),
    )(x, lead, w1, win)


def _ffn_fwd_bwd(x, lead, y, tgt, wout, w2n, wff1, wff2, wfn):
    nt = x.shape[0] // ROW_TILE
    m = x.shape[0] + ROW_TILE
    tm = ROW_TILE
    nj = D_FF // 1024

    def body(x_ref, lead_ref, y_ref, tgt_ref, w2n_ref, wfn_ref, wout_hbm, wff1_hbm, wff2_hbm,
             loss_ref, gwf_ref, gw2_ref, ff_ref, da_ref, hn2_ref, dh1_ref, dh2_ref, dy_ref,
             wout_v, wff1_v, wff2_v, a_s, sems):
        i = pl.program_id(0)
        hp = jnp.where(i == nt, lead_ref[...], x_ref[...])

        @pl.when(i == 0)
        def _():
            cps = [pltpu.make_async_copy(s, d, sems.at[k])
                   for k, (s, d) in enumerate(((wout_hbm, wout_v), (wff1_hbm, wff1_v), (wff2_hbm, wff2_v)))]
            for cp in cps:
                cp.start()
            for cp in cps:
                cp.wait()
            loss_ref[...] = jnp.zeros_like(loss_ref)
            gwf_ref[...] = jnp.zeros_like(gwf_ref)
            gw2_ref[...] = jnp.zeros_like(gw2_ref)

        h1 = hp + jnp.dot(y_ref[...], wout_v[...], preferred_element_type=F32)
        r2 = lax.rsqrt(jnp.mean(h1 * h1, axis=-1, keepdims=True) + EPS)
        n2 = h1 * r2
        w2n_row = w2n_ref[...]
        hn2 = _mx(n2 * w2n_row)
        hn2_ref[...] = hn2
        h2 = h1
        for j in range(nj):
            js = slice(j * 1024, (j + 1) * 1024)
            a = jnp.dot(hn2, wff1_v[:, js], preferred_element_type=F32)
            a_s[:, js] = a
            ra = jnp.maximum(a, 0.0)
            ff = _mx(ra * ra)
            ff_ref[:, js] = ff
            h2 = h2 + jnp.dot(ff, wff2_v[js, :], preferred_element_type=F32)

        r3 = lax.rsqrt(jnp.mean(h2 * h2, axis=-1, keepdims=True) + EPS)
        n3 = h2 * r3
        wf_row = wfn_ref[...]
        err = n3 * wf_row - tgt_ref[...]
        tokf = (i < nt).astype(F32)
        loss_ref[...] += 0.5 * jnp.sum(jnp.mean(err * err, axis=-1, keepdims=True) * tokf)
        dout = err * (tokf / D_MODEL)
        gwf_ref[...] += jnp.sum(dout * n3, axis=0, keepdims=True)
        dn3 = dout * wf_row
        dh2 = r3 * (dn3 - n3 * jnp.mean(dn3 * n3, axis=-1, keepdims=True))
        dh2m = _mx(dh2)
        dh2_ref[...] = dh2m

        dhn2 = jnp.zeros((tm, D_MODEL), F32)
        for j in range(nj):
            js = slice(j * 1024, (j + 1) * 1024)
            dff = lax.dot_general(dh2m, wff2_v[js, :], (((1,), (1,)), ((), ())), preferred_element_type=F32)
            da = _mx(dff * (2.0 * jnp.maximum(a_s[:, js], 0.0)))
            da_ref[:, js] = da
            dhn2 = dhn2 + lax.dot_general(da, wff1_v[:, js], (((1,), (1,)), ((), ())), preferred_element_type=F32)
        gw2_ref[...] += jnp.sum(dhn2 * n2, axis=0, keepdims=True)
        dn2 = dhn2 * w2n_row
        dh1 = dh2 + r2 * (dn2 - n2 * jnp.mean(dn2 * n2, axis=-1, keepdims=True))
        dh1_ref[...] = dh1
        dy_ref[...] = lax.dot_general(_mx(dh1), wout_v[...], (((1,), (1,)), ((), ())), preferred_element_type=F32)

    rows = lambda w: pl.BlockSpec((tm, w), lambda i: (i, 0))
    hbm = pl.BlockSpec(memory_space=pl.ANY)
    return pl.pallas_call(
        body, grid=(m // tm,), name="ffn_fwd_bwd",
        in_specs=[_token_tiles(D_MODEL, nt), _full((ROW_TILE, D_MODEL)), rows(D_MIX), _token_tiles(D_MODEL, nt),
                  _full((1, D_MODEL)), _full((1, D_MODEL)), hbm, hbm, hbm],
        out_specs=[_full((1, 128)), _full((1, D_MODEL)), _full((1, D_MODEL)), rows(D_FF), rows(D_FF), rows(D_MODEL),
                   rows(D_MODEL), rows(D_MODEL), rows(D_MIX)],
        out_shape=[jax.ShapeDtypeStruct((1, 128), F32), jax.ShapeDtypeStruct((1, D_MODEL), F32),
                   jax.ShapeDtypeStruct((1, D_MODEL), F32), jax.ShapeDtypeStruct((m, D_FF), MXU_DTYPE),
                   jax.ShapeDtypeStruct((m, D_FF), MXU_DTYPE), jax.ShapeDtypeStruct((m, D_MODEL), MXU_DTYPE),
                   jax.ShapeDtypeStruct((m, D_MODEL), F32), jax.ShapeDtypeStruct((m, D_MODEL), MXU_DTYPE),
                   jax.ShapeDtypeStruct((m, D_MIX), F32)],
        scratch_shapes=[pltpu.VMEM((D_MIX, D_MODEL), MXU_DTYPE), pltpu.VMEM((D_MODEL, D_FF), MXU_DTYPE),
                        pltpu.VMEM((D_FF, D_MODEL), MXU_DTYPE), pltpu.VMEM((tm, D_FF), F32),
                        pltpu.SemaphoreType.DMA((3,))],
        compiler_params=_params("arbitrary"),
    )(x, lead, y, tgt, w2n, wfn, wout, wff1, wff2)


def _in_proj_bwd(dproj, x, lead, dh1, w1, win, chip_sums):
    nt = x.shape[0] // ROW_TILE
    m = x.shape[0] + ROW_TILE
    tm = ROW_TILE
    ns = len(chip_sums)

    def body(dp_ref, x_ref, lead_ref, dh1_ref, w1_ref, win_hbm, *rest):
        cs_refs, (gx_ref, gw1_ref, gmeta_ref) = rest[:ns], rest[ns:ns + 3]
        part_refs, (win_v, sem) = rest[ns + 3:2 * ns + 3], rest[2 * ns + 3:2 * ns + 5]
        exchange = _ChipExchange(cs_refs, part_refs, [False] * ns, *rest[2 * ns + 5:])
        i = pl.program_id(0)

        @pl.when(i == 0)
        def _():
            exchange.start()
            cp = pltpu.make_async_copy(win_hbm, win_v, sem)
            cp.start()
            cp.wait()
            gw1_ref[...] = jnp.zeros_like(gw1_ref)
            gmeta_ref[...] = jnp.zeros_like(gmeta_ref)

        dhn = lax.dot_general(dp_ref[...], win_v[...], (((1,), (1,)), ((), ())), preferred_element_type=F32)
        x = jnp.where(i == nt, lead_ref[...], x_ref[...])
        r = lax.rsqrt(jnp.mean(x * x, axis=-1, keepdims=True) + EPS)
        n = x * r
        gw1_ref[...] += jnp.sum(dhn * n, axis=0, keepdims=True)
        dn = dhn * w1_ref[...]
        dh0 = dh1_ref[...] + r * (dn - n * jnp.mean(dn * n, axis=-1, keepdims=True))

        @pl.when(i < nt)
        def _():
            gx_ref[...] = dh0

        @pl.when(i == nt)
        def _():
            gmeta_ref[...] = dh0[PAD_ROWS:LEAD, :] + dh0[LEAD + PAD_ROWS:2 * LEAD, :]
            exchange.finish()

    rows = lambda w: pl.BlockSpec((tm, w), lambda i: (i, 0))
    hbm = pl.BlockSpec(memory_space=pl.ANY)
    outs = pl.pallas_call(
        body, grid=(m // tm,), name="in_proj_bwd",
        in_specs=[rows(PROJ_W), _token_tiles(D_MODEL, nt), _full((ROW_TILE, D_MODEL)), rows(D_MODEL),
                  _full((1, D_MODEL)), hbm] + [hbm] * ns,
        out_specs=[_token_tiles(D_MODEL, nt), _full((1, D_MODEL)), _full((N_META, D_MODEL))] + [hbm] * ns,
        out_shape=[jax.ShapeDtypeStruct(x.shape, F32), jax.ShapeDtypeStruct((1, D_MODEL), F32),
                   jax.ShapeDtypeStruct((N_META, D_MODEL), F32)]
        + _ChipExchange.out_shapes(chip_sums, [False] * ns),
        scratch_shapes=[pltpu.VMEM((D_MODEL, PROJ_W), MXU_DTYPE), pltpu.SemaphoreType.DMA] + _ChipExchange.scratch(ns),
        compiler_params=_params("arbitrary"),
    )(dproj, x, lead, dh1, w1, win, *chip_sums)
    return outs[0], outs[1], outs[2], outs[3:]


MXU_DEPTH = 256


def _row_slab(m, cap):
    return max(k for k in range(MXU_DEPTH, cap + 1, MXU_DEPTH) if m % k == 0)


def _tn_matmul(a, b, name, tka, max_slab=768, tn=512):
    m, ka = a.shape
    nb = b.shape[1]
    tkm = _row_slab(m, max_slab)
    n_steps = m // tkm

    def body(a_ref, b_ref, o_ref, omx_ref):
        k = pl.program_id(1)

        @pl.when(k == 0)
        def _():
            o_ref[...] = jnp.zeros_like(o_ref)

        at = _mx(a_ref[...])
        for j in range(0, nb, tn):
            w = min(tn, nb - j)
            o_ref[:, j:j + w] += lax.dot_general(at, _mx(b_ref[:, j:j + w]), (((0,), (0,)), ((), ())),
                                                 preferred_element_type=F32)

        @pl.when(k == n_steps - 1)
        def _():
            omx_ref[...] = _mx(o_ref[...])

    out = pl.BlockSpec((tka, nb), lambda i, k: (i, 0))
    return pl.pallas_call(
        body, grid=(ka // tka, n_steps), name=name,
        in_specs=[pl.BlockSpec((tkm, tka), lambda i, k: (k, i)), pl.BlockSpec((tkm, nb), lambda i, k: (k, 0))],
        out_specs=[out, out],
        out_shape=[jax.ShapeDtypeStruct((ka, nb), F32), jax.ShapeDtypeStruct((ka, nb), MXU_DTYPE)],
        compiler_params=_params("arbitrary", "arbitrary"),
    )(a, b)


def _tn_matmul_banded(a, b, name, band, tka, tn=512):
    m, ka = a.shape
    nb = b.shape[1]
    tkm = _row_slab(m, 768)
    n_steps = m // tkm

    def body(a_ref, b_ref, o_ref, acc):
        k = pl.program_id(1)

        @pl.when(k == 0)
        def _():
            acc[...] = jnp.zeros_like(acc)

        at = _mx(a_ref[...])
        for j in range(0, nb, tn):
            w = min(tn, nb - j)
            acc[:, j:j + w] += lax.dot_general(at, _mx(b_ref[:, j:j + w]), (((0,), (0,)), ((), ())),
                                               preferred_element_type=F32)

        @pl.when(k == n_steps - 1)
        def _():
            for j in range(N_DEV):
                o_ref[j] = acc[:, j * band:(j + 1) * band]

    return pl.pallas_call(
        body, grid=(ka // tka, n_steps), name=name,
        in_specs=[pl.BlockSpec((tkm, tka), lambda i, k: (k, i)), pl.BlockSpec((tkm, nb), lambda i, k: (k, 0))],
        out_specs=pl.BlockSpec((N_DEV, tka, band), lambda i, k: (0, i, 0)),
        out_shape=jax.ShapeDtypeStruct((N_DEV, ka, band), F32),
        scratch_shapes=[pltpu.VMEM((tka, nb), F32)],
        compiler_params=_params("arbitrary", "arbitrary"),
    )(a, b)


def _assemble_bands(g, width):
    n, rows, band = g.shape
    tr = 256

    def body(g_ref, o_ref):
        parts = [g_ref[j] for j in range(n)] + [jnp.zeros((tr, width - n * band), g.dtype)]
        o_ref[...] = jnp.concatenate(parts, axis=1)

    return pl.pallas_call(
        body, grid=(rows // tr,), name="assemble_w_in",
        in_specs=[pl.BlockSpec((n, tr, band), lambda i: (0, i, 0))],
        out_specs=pl.BlockSpec((tr, width), lambda i: (i, 0)),
        out_shape=jax.ShapeDtypeStruct((rows, width), g.dtype),
        compiler_params=_params("arbitrary"),
    )(g)


def _chunk_block(b, c, nb, nc):
    return jnp.where(c == 0, nb * (nc - 1) + b, b * (nc - 1) + c - 1)

def _mixer_fwd(proj, cw, cb, dt_bias, a_log, d_x, nw, pool_w, pool_scale, nb, shards, by_cols):
    m = proj.shape[0]
    nc = m // nb // CHUNK
    n_steps = nb * nc
    ns = len(shards)
    e_mat, _, et_f32, tril, _ = _ssd_constants()

    def body(p_ref, cw_ref, cb_ref, dtb_ref, alog_ref, dx_ref, nw_ref, pw_ref, ps_ref, e_ref, et_ref, tril_ref, *rest):
        shard_refs, (y_ref, ypre_ref, pre_ref, st_ref) = rest[:ns], rest[ns:ns + 4]
        gathered_refs, (xtail, utail, state) = rest[ns + 4:2 * ns + 4], rest[2 * ns + 4:2 * ns + 7]
        gather = _Gather(shard_refs, gathered_refs, by_cols, *rest[2 * ns + 7:])
        c = pl.program_id(1)
        step = pl.program_id(0) * nc + c

        @pl.when(step == 0)
        def _():
            gather.start()

        @pl.when(step == n_steps // 2)
        def _():
            gather.forward()

        @pl.when(c == 0)
        def _():
            xtail[...] = jnp.zeros_like(xtail)
            utail[...] = jnp.zeros_like(utail)
            state[...] = jnp.zeros_like(state)

        valid = (c > 0) | (lax.broadcasted_iota(jnp.int32, (CHUNK, 1), 0) >= PAD_ROWS)

        u = p_ref[:, 0:D_POOL]
        inv_cnt, lane = _pool_inv_count(c)
        win = _pool_window_sums(jnp.concatenate([utail[...], u], axis=0), lane)
        utail[...] = u[CHUNK - HALO:, :]
        pooled = win * inv_cnt - u
        mixed = jnp.concatenate(
            [_dot(pooled[:, g * 128:(g + 1) * 128], pw_ref[g]) for g in range(len(POOL_WINDOWS))], axis=1)
        y_ref[:, 0:D_POOL] = _mx(mixed * ps_ref[...])

        xbc = p_ref[:, OFF_X:OFF_X + D_XBC]
        pre = _conv_pre(jnp.concatenate([xtail[...], xbc], axis=0), xbc, cw_ref[...], cb_ref[...])
        xtail[...] = xbc[CHUNK - HALO:, :]
        pre_ref[...] = pre
        xc = pre * _sigmoid(pre)
        dt, _, a_col, _, _ = _dt_and_cumsum(p_ref[:, OFF_DT:OFF_DT + 128], dtb_ref[...], alog_ref[...], valid,
                                            tril_ref[...])
        s_prev = state[...]
        st_ref[0] = s_prev
        yp, s_new = _ssd_chunk_fwd(xc[:, 0:D_SSM], xc[:, D_SSM:D_SSM + 512], xc[:, D_SSM + 512:], dt, a_col, s_prev,
                                   dx_ref[...], e_ref[...], et_ref[...])
        state[...] = s_new
        ypre_ref[...] = yp
        z = p_ref[:, OFF_Z:OFF_Z + D_SSM]
        yz = yp * (z * _sigmoid(z))
        outs = []
        for g in range(N_GROUPS):
            gs = slice(g * GROUP_CH, (g + 1) * GROUP_CH)
            r = lax.rsqrt(jnp.mean(yz[:, gs] * yz[:, gs], axis=-1, keepdims=True) + EPS)
            outs.append(yz[:, gs] * r)
        y_ref[:, D_POOL:] = _mx(jnp.concatenate(outs, axis=1) * nw_ref[...])

        @pl.when(step == n_steps - 1)
        def _():
            gather.finish()

    blk = lambda w: pl.BlockSpec((CHUNK, w), lambda b, c: (_chunk_block(b, c, nb, nc), 0))
    hbm = pl.BlockSpec(memory_space=pl.ANY)
    outs = pl.pallas_call(
        body, grid=(nb, nc), name="mixer_fwd",
        in_specs=[blk(PROJ_W), _full((4, D_XBC)), _full((1, D_XBC)), _full((1, 128)), _full((1, 128)),
                  _full((1, D_SSM)), _full((1, D_SSM)), _full((4, 128, 128)), _full((1, D_POOL)),
                  _full((128, D_SSM)), _full((D_SSM, 128)), _full((CHUNK, CHUNK))] + [hbm] * ns,
        out_specs=[blk(D_MIX), blk(D_SSM), blk(D_XBC),
                   pl.BlockSpec((1, D_SSM, D_STATE), lambda b, c: (b * nc + c, 0, 0))] + [hbm] * ns,
        out_shape=[jax.ShapeDtypeStruct((m, D_MIX), MXU_DTYPE), jax.ShapeDtypeStruct((m, D_SSM), F32),
                   jax.ShapeDtypeStruct((m, D_XBC), F32), jax.ShapeDtypeStruct((m // CHUNK, D_SSM, D_STATE), F32)]
        + _Gather.out_shapes(shards, by_cols),
        scratch_shapes=[pltpu.VMEM((HALO, D_XBC), F32), pltpu.VMEM((HALO, D_POOL), F32),
                        pltpu.VMEM((D_SSM, D_STATE), F32)] + _Gather.scratch(ns),
        compiler_params=_params("arbitrary", "arbitrary"),
    )(proj, cw, cb, dt_bias, a_log, d_x, nw, pool_w, pool_scale, e_mat, et_f32, tril, *shards)
    return outs[0], outs[1], outs[2], outs[3], outs[4:]


def _mixer_bwd(proj, dy, ypre, conv_pre, states, cw, dt_bias, a_log, d_x, nw, pool_w, pool_scale, nb, chip_sums):
    m = proj.shape[0]
    nc = m // nb // CHUNK
    e_mat, et_mat, et_f32, tril, triu = _ssd_constants()
    hb = CHUNK // HALO
    ns = len(chip_sums)

    def body(p_ref, halo_ref, dy_ref, ypre_ref, pre_ref, st_ref, cw_ref, dtb_ref, alog_ref, dx_ref, nw_ref, pw_ref,
             ps_ref, e_ref, et_ref, etf_ref, tril_ref, triu_ref, *rest):
        cs_refs = rest[:ns]
        dp_ref, gcw_ref, gcb_ref, gdtb_ref, galog_ref, gd_ref, gnw_ref, gpw_ref, gps_ref = rest[ns:ns + 9]
        part_refs, (ds_carry, dpre_next, dq_next) = rest[ns + 9:2 * ns + 9], rest[2 * ns + 9:2 * ns + 12]
        exchange = _ChipExchange(cs_refs, part_refs, [False] * ns, *rest[2 * ns + 12:])
        b = pl.program_id(0)
        cc = pl.program_id(1)
        c = nc - 1 - cc

        @pl.when((b == 0) & (cc == 0))
        def _():
            exchange.start()
            for r in (gcw_ref, gcb_ref, gdtb_ref, galog_ref, gd_ref, gnw_ref, gpw_ref, gps_ref):
                r[...] = jnp.zeros_like(r)

        @pl.when(cc == 0)
        def _():
            ds_carry[...] = jnp.zeros_like(ds_carry)
            dpre_next[...] = jnp.zeros_like(dpre_next)
            dq_next[...] = jnp.zeros_like(dq_next)

        valid = (c > 0) | (lax.broadcasted_iota(jnp.int32, (CHUNK, 1), 0) >= PAD_ROWS)
        first = c > 0

        u = p_ref[:, 0:D_POOL]
        u_halo = jnp.where(first, halo_ref[...], 0.0)
        inv_cnt, lane = _pool_inv_count(c)
        pooled = _pool_window_sums(jnp.concatenate([u_halo, u], axis=0), lane) * inv_cnt - u
        dyp = dy_ref[:, 0:D_POOL]
        ps = ps_ref[...]
        dmixed = dyp * ps
        mixed, dpooled = [], []
        for g in range(len(POOL_WINDOWS)):
            gsl = slice(g * 128, (g + 1) * 128)
            pw = pw_ref[g]
            mixed.append(_dot(pooled[:, gsl], pw))
            dpooled.append(_dot_nt(dmixed[:, gsl], pw))
            gpw_ref[g] += _dot_tn(pooled[:, gsl], dmixed[:, gsl])
        gps_ref[...] += jnp.sum(dyp * jnp.concatenate(mixed, axis=1), axis=0, keepdims=True)
        dpooled = jnp.concatenate(dpooled, axis=1)
        dq = dpooled * inv_cnt
        du = _pool_window_sums_ahead(jnp.concatenate([dq, dq_next[...]], axis=0), lane) - dpooled
        dq_next[...] = dq[0:HALO, :]
        dp_ref[:, 0:D_POOL] = _mx(du)

        yp = ypre_ref[...]
        z = p_ref[:, OFF_Z:OFF_Z + D_SSM]
        sz, dsz = _silu_and_grad(z)
        yz = yp * sz
        do = dy_ref[:, D_POOL:]
        nw_row = nw_ref[...]
        dyz = []
        gnw = []
        for g in range(N_GROUPS):
            gs = slice(g * GROUP_CH, (g + 1) * GROUP_CH)
            r = lax.rsqrt(jnp.mean(yz[:, gs] * yz[:, gs], axis=-1, keepdims=True) + EPS)
            n = yz[:, gs] * r
            gnw.append(jnp.sum(do[:, gs] * n, axis=0, keepdims=True))
            dn = do[:, gs] * nw_row[:, gs]
            dyz.append(r * (dn - n * jnp.mean(dn * n, axis=-1, keepdims=True)))
        gnw_ref[...] += jnp.concatenate(gnw, axis=1)
        dyz = jnp.concatenate(dyz, axis=1)
        dp_ref[:, OFF_Z:OFF_Z + D_SSM] = _mx(dyz * yp * dsz)
        dyp_ssm = dyz * sz

        xc, dsilu = _silu_and_grad(pre_ref[...])
        dtr = p_ref[:, OFF_DT:OFF_DT + 128]
        dt, a_row, a_col, dt_pre, head = _dt_and_cumsum(dtr, dtb_ref[...], alog_ref[...], valid, tril_ref[...])
        dxs, dbm, dcm, ddt, da, dd, ds_prev = _ssd_chunk_bwd(
            xc[:, 0:D_SSM], xc[:, D_SSM:D_SSM + 512], xc[:, D_SSM + 512:], dt, a_row, a_col, st_ref[0],
            ds_carry[...], dyp_ssm, dx_ref[...], e_ref[...], et_ref[...], etf_ref[...], triu_ref[...])
        ds_carry[...] = ds_prev
        gd_ref[...] += dd
        galog_ref[...] += da * a_row
        ddtr = jnp.where(valid & head, ddt * _sigmoid(dt_pre), 0.0)
        gdtb_ref[...] += jnp.sum(ddtr, axis=0, keepdims=True)
        dp_ref[:, OFF_DT:OFF_DT + 128] = _mx(ddtr)

        dpre = jnp.concatenate([dxs, dbm, dcm], axis=1) * dsilu
        gcb_ref[...] += jnp.sum(dpre, axis=0, keepdims=True)
        dext = jnp.concatenate([dpre, dpre_next[...]], axis=0)
        dpre_next[...] = dpre[0:HALO, :]
        ups = [_shift_up(dext, 3 - k) for k in range(4)]
        xbc = p_ref[:, OFF_X:OFF_X + D_XBC]
        gcw_ref[...] += jnp.concatenate([jnp.sum(xbc * ups[k], axis=0, keepdims=True) for k in range(4)], axis=0)
        cw = cw_ref[...]
        dp_ref[:, OFF_X:OFF_X + D_XBC] = _mx(cw[3:4, :] * ups[3] + cw[2:3, :] * ups[2]
                                             + cw[1:2, :] * ups[1] + cw[0:1, :] * ups[0])

        @pl.when((b == nb - 1) & (cc == nc - 1))
        def _():
            exchange.finish()

    blk = lambda w: pl.BlockSpec((CHUNK, w), lambda b, cc: (_chunk_block(b, nc - 1 - cc, nb, nc), 0))
    halo = pl.BlockSpec((HALO, D_POOL),
                        lambda b, cc: (_chunk_block(b, jnp.maximum(nc - 2 - cc, 0), nb, nc) * hb + hb - 1, 0))
    hbm = pl.BlockSpec(memory_space=pl.ANY)
    outs = pl.pallas_call(
        body, grid=(nb, nc), name="mixer_bwd",
        in_specs=[blk(PROJ_W), halo, blk(D_MIX), blk(D_SSM), blk(D_XBC),
                  pl.BlockSpec((1, D_SSM, D_STATE), lambda b, cc: (b * nc + nc - 1 - cc, 0, 0)),
                  _full((4, D_XBC)), _full((1, 128)), _full((1, 128)), _full((1, D_SSM)),
                  _full((1, D_SSM)), _full((4, 128, 128)), _full((1, D_POOL)),
                  _full((128, D_SSM)), _full((D_SSM, 128)), _full((D_SSM, 128)), _full((CHUNK, CHUNK)),
                  _full((CHUNK, CHUNK))] + [hbm] * ns,
        out_specs=[blk(PROJ_W), _full((4, D_XBC)), _full((1, D_XBC)), _full((1, 128)), _full((1, 128)), _full((1, 128)),
                   _full((1, D_SSM)), _full((4, 128, 128)), _full((1, D_POOL))] + [hbm] * ns,
        out_shape=[jax.ShapeDtypeStruct((m, PROJ_W), MXU_DTYPE), jax.ShapeDtypeStruct((4, D_XBC), F32),
                   jax.ShapeDtypeStruct((1, D_XBC), F32), jax.ShapeDtypeStruct((1, 128), F32),
                   jax.ShapeDtypeStruct((1, 128), F32), jax.ShapeDtypeStruct((1, 128), F32),
                   jax.ShapeDtypeStruct((1, D_SSM), F32), jax.ShapeDtypeStruct((4, 128, 128), F32),
                   jax.ShapeDtypeStruct((1, D_POOL), F32)] + _ChipExchange.out_shapes(chip_sums, [False] * ns),
        scratch_shapes=[pltpu.VMEM((D_SSM, D_STATE), F32), pltpu.VMEM((HALO, D_XBC), F32),
                        pltpu.VMEM((HALO, D_POOL), F32)] + _ChipExchange.scratch(ns),
        compiler_params=_params("arbitrary", "arbitrary"),
    )(proj, proj, dy, ypre, conv_pre, states, cw, dt_bias, a_log, d_x, nw, pool_w, pool_scale, e_mat, et_mat, et_f32, tril, triu,
      *chip_sums)
    return outs[:9], outs[9:]


MESH_IDS = pl.DeviceIdType.MESH
_HBM = pl.BlockSpec(memory_space=pltpu.HBM)


def _coords():
    return lax.axis_index("x"), lax.axis_index("y"), lax.axis_index("c")


def _other_chips(x, y):
    return [(1 - x, y), (x, 1 - y), (1 - x, 1 - y)]


class _Gather:
    def __init__(self, ins, outs, by_cols, send_sems, recv_sems, local_sems):
        self.ins, self.outs, self.by_cols, self.n = ins, outs, by_cols, len(ins)
        self.send_sems, self.recv_sems, self.local_sems = send_sems, recv_sems, local_sems
        self.x, self.y, self.c = _coords()
        self.me, self.sibling = (self.x, self.y, self.c), (self.x, self.y, 1 - self.c)
        self.chips = _other_chips(self.x, self.y)

    @staticmethod
    def scratch(n):
        return [pltpu.SemaphoreType.DMA((7 * n,)), pltpu.SemaphoreType.DMA((7 * n,)), pltpu.SemaphoreType.DMA((n,))]

    @staticmethod
    def out_shapes(shards, by_cols):
        return [jax.ShapeDtypeStruct((s.shape[0], N_DEV * s.shape[1]) if cols else (N_DEV,) + s.shape, s.dtype)
                for s, cols in zip(shards, by_cols)]

    def _block(self, t, device):
        idx = 4 * device[0] + 2 * device[1] + device[2]
        if not self.by_cols[t]:
            return self.outs[t].at[idx]
        w = self.ins[t].shape[1]
        return self.outs[t].at[:, pl.ds(pl.multiple_of(idx * w, w), w)]

    def _copy(self, t, k, block, to, own=False):
        dst = self._block(t, block)
        return pltpu.make_async_remote_copy(
            src_ref=self.ins[t] if own else dst, dst_ref=dst, send_sem=self.send_sems.at[t * 7 + k],
            recv_sem=self.recv_sems.at[t * 7 + k], device_id=to, device_id_type=MESH_IDS)

    def _mine(self):
        return [pltpu.make_async_copy(self.ins[t], self._block(t, self.me), self.local_sems.at[t])
                for t in range(self.n)]

    def _first(self):
        cps = []
        for t in range(self.n):
            cps.append(self._copy(t, 0, self.me, self.sibling, own=True))
            cps += [self._copy(t, 1 + j, self.me, (*chip, self.c), own=True) for j, chip in enumerate(self.chips)]
        return cps

    def _passed(self):
        return [self._copy(t, 4 + j, (*chip, self.c), self.sibling)
                for j, chip in enumerate(self.chips) for t in range(self.n)]

    def start(self):
        for cp in self._mine() + self._first():
            cp.start()

    def forward(self):
        for j, chip in enumerate(self.chips):
            for t in range(self.n):
                self._copy(t, 1 + j, (*chip, self.c), self.me).wait_recv()
                self._copy(t, 4 + j, (*chip, self.c), self.sibling).start()

    def finish(self):
        for t in range(self.n):
            self._copy(t, 0, self.sibling, self.me).wait_recv()
            for j, chip in enumerate(self.chips):
                self._copy(t, 4 + j, (*chip, 1 - self.c), self.me).wait_recv()
        for cp in self._first() + self._passed():
            cp.wait_send()
        for cp in self._mine():
            cp.wait()


def _weight_gather(shards):
    n = len(shards)

    def body(*refs):
        g = _Gather(refs[:n], refs[n:2 * n], [False] * n, *refs[2 * n:])
        g.start()
        g.forward()
        g.finish()

    return pl.pallas_call(
        body, name="weight_gather",
        in_specs=[_HBM] * n, out_specs=[_HBM] * n,
        out_shape=_Gather.out_shapes(shards, [False] * n),
        scratch_shapes=_Gather.scratch(n),
    )(*shards)


def _owner_blocks(g, by_cols):
    return (g.shape[0], g.shape[1] // N_DEV) if by_cols else g.shape[2:]


def _grad_exchange_d2d(gs, by_cols, name):
    n = len(gs)

    def body(*refs):
        ins, got = refs[:n], refs[n:2 * n]
        send_sems, recv_sems = refs[2 * n:]
        x, y, c = _coords()

        def src(t, k):
            if not by_cols[t]:
                return ins[t].at[k, 1 - c]
            w = ins[t].shape[1] // N_DEV
            return ins[t].at[:, pl.ds(pl.multiple_of((2 * k + 1 - c) * w, w), w)]

        remote = [pltpu.make_async_remote_copy(
            src_ref=src(t, k), dst_ref=got[t].at[k], send_sem=send_sems.at[t * 4 + k],
            recv_sem=recv_sems.at[t * 4 + k], device_id=(x, y, 1 - c), device_id_type=MESH_IDS)
            for t in range(n) for k in range(4)]
        for cp in remote:
            cp.start()
        for cp in remote:
            cp.wait_recv()
        for cp in remote:
            cp.wait_send()

    return pl.pallas_call(
        body, name=name,
        in_specs=[_HBM] * n, out_specs=[_HBM] * n,
        out_shape=[jax.ShapeDtypeStruct((4,) + _owner_blocks(g, cols), g.dtype) for g, cols in zip(gs, by_cols)],
        scratch_shapes=[pltpu.SemaphoreType.DMA((4 * n,)), pltpu.SemaphoreType.DMA((4 * n,))],
    )(*gs)


def _small_allreduce(pack):
    rows = pack.shape[0]

    def body(p_ref, o_ref, sib_ref, parts_ref, send_sems, recv_sems):
        x, y, c = _coords()
        my_chip = 2 * x + y
        swap = pltpu.make_async_remote_copy(src_ref=p_ref, dst_ref=sib_ref, send_sem=send_sems.at[0],
                                            recv_sem=recv_sems.at[0], device_id=(x, y, 1 - c), device_id_type=MESH_IDS)
        swap.start()
        swap.wait_recv()
        parts_ref[my_chip] = p_ref[...] + sib_ref[...]
        remote = [pltpu.make_async_remote_copy(
            src_ref=parts_ref.at[my_chip], dst_ref=parts_ref.at[my_chip], send_sem=send_sems.at[1 + j],
            recv_sem=recv_sems.at[1 + j], device_id=(cx, cy, c), device_id_type=MESH_IDS)
            for j, (cx, cy) in enumerate(_other_chips(x, y))]
        for cp in remote:
            cp.start()
        for j, (cx, cy) in enumerate(_other_chips(x, y)):
            slot = parts_ref.at[2 * cx + cy]
            pltpu.make_async_remote_copy(src_ref=slot, dst_ref=slot, send_sem=send_sems.at[1 + j],
                                         recv_sem=recv_sems.at[1 + j], device_id=(cx, cy, c),
                                         device_id_type=MESH_IDS).wait_recv()
        o_ref[...] = ((parts_ref[0] + parts_ref[1]) + parts_ref[2]) + parts_ref[3]
        swap.wait_send()
        for cp in remote:
            cp.wait_send()

    vmem = pl.BlockSpec(memory_space=pltpu.VMEM)
    return pl.pallas_call(
        body, name="small_allreduce", in_specs=[vmem], out_specs=vmem,
        out_shape=jax.ShapeDtypeStruct((rows, 128), F32),
        scratch_shapes=[pltpu.VMEM((rows, 128), F32), pltpu.VMEM((4, rows, 128), F32),
                        pltpu.SemaphoreType.DMA((4,)), pltpu.SemaphoreType.DMA((4,))],
    )(pack)


class _ChipExchange:
    def __init__(self, ins, outs, whole, send_sems, recv_sems, local_sems):
        self.ins, self.outs, self.whole, self.n = ins, outs, whole, len(ins)
        self.send_sems, self.recv_sems, self.local_sems = send_sems, recv_sems, local_sems
        self.x, self.y, self.c = _coords()
        self.my_chip = 2 * self.x + self.y
        self.chips = _other_chips(self.x, self.y)

    @staticmethod
    def scratch(n):
        return [pltpu.SemaphoreType.DMA((3 * n,)), pltpu.SemaphoreType.DMA((3 * n,)), pltpu.SemaphoreType.DMA((n,))]

    def _src(self, t, k):
        return self.ins[t] if self.whole[t] else self.ins[t].at[k]

    def _local(self):
        return [pltpu.make_async_copy(self._src(t, self.my_chip), self.outs[t].at[self.my_chip], self.local_sems.at[t])
                for t in range(self.n)]

    def _remote(self):
        return [pltpu.make_async_remote_copy(
            src_ref=self._src(t, 2 * cx + cy), dst_ref=self.outs[t].at[self.my_chip],
            send_sem=self.send_sems.at[t * 3 + j], recv_sem=self.recv_sems.at[t * 3 + j],
            device_id=(cx, cy, self.c), device_id_type=MESH_IDS)
            for t in range(self.n) for j, (cx, cy) in enumerate(self.chips)]

    def start(self):
        for cp in self._remote() + self._local():
            cp.start()

    def finish(self):
        for t in range(self.n):
            for j, (cx, cy) in enumerate(self.chips):
                slot = self.outs[t].at[2 * cx + cy]
                pltpu.make_async_remote_copy(
                    src_ref=slot, dst_ref=slot, send_sem=self.send_sems.at[t * 3 + j],
                    recv_sem=self.recv_sems.at[t * 3 + j], device_id=(cx, cy, self.c),
                    device_id_type=MESH_IDS).wait_recv()
        for cp in self._remote():
            cp.wait_send()
        for cp in self._local():
            cp.wait()

    @staticmethod
    def out_shapes(arrs, whole):
        return [jax.ShapeDtypeStruct(((4,) + a.shape) if w else a.shape, a.dtype) for a, w in zip(arrs, whole)]


def _row_tile(rows, cols, n_arrays):
    budget = 24 * 1024 * 1024
    padded = -(-cols // 128) * 128
    step = 16 if rows % 16 == 0 else 8
    tr = max(step, budget // (n_arrays * 2 * 4 * padded) // step * step)
    while rows % tr:
        tr -= step
    return tr


def _chip_sum(g, by_cols, got, core, name):
    rows, cols = _owner_blocks(g, by_cols)
    tr = _row_tile(rows, cols, 3)

    def body(c_ref, a_ref, b_ref, o_ref):
        o_ref[...] = (a_ref[...] + b_ref[...].astype(F32)).astype(o_ref.dtype)

    own = (pl.BlockSpec((tr, cols), lambda k, i, c: (i, 2 * k + c[0])) if by_cols
           else pl.BlockSpec((None, None, tr, cols), lambda k, i, c: (k, c[0], i, 0)))
    grid_spec = pltpu.PrefetchScalarGridSpec(
        num_scalar_prefetch=1, grid=(4, rows // tr),
        in_specs=[own, pl.BlockSpec((None, tr, cols), lambda k, i, c: (k, i, 0))],
        out_specs=pl.BlockSpec((None, tr, cols), lambda k, i, c: (k, i, 0)))
    return pl.pallas_call(body, grid_spec=grid_spec, name=name,
                          out_shape=jax.ShapeDtypeStruct((4, rows, cols), MXU_DTYPE),
                          compiler_params=_params("arbitrary", "arbitrary"))(core, g, got)


def _adamw_math(w, g, m, v):
    m2 = ADAM_B1 * m + (1.0 - ADAM_B1) * g
    v2 = ADAM_B2 * v + (1.0 - ADAM_B2) * (g * g)
    m_hat = m2 / (1.0 - ADAM_B1 ** ADAM_STEP)
    v_hat = v2 / (1.0 - ADAM_B2 ** ADAM_STEP)
    delta = -ADAM_LR * (m_hat / (jnp.sqrt(v_hat) + ADAM_EPS) + ADAM_WD * w)
    return delta, m2, v2


def _adamw(parts, w, m, v, name):
    rows, cols = w.shape
    tr = _row_tile(rows, cols, 11)

    def body(p_ref, w_ref, m_ref, v_ref, g_ref, d_ref, m2_ref, v2_ref):
        part = lambda k: p_ref[k].astype(F32)
        g = ((part(0) + part(1)) + part(2)) + part(3)
        d, m2, v2 = _adamw_math(w_ref[...], g, m_ref[...], v_ref[...])
        g_ref[...] = g
        d_ref[...] = d
        m2_ref[...] = m2
        v2_ref[...] = v2

    blk = pl.BlockSpec((tr, cols), lambda i: (i, 0))
    out = jax.ShapeDtypeStruct((rows, cols), F32)
    return pl.pallas_call(body, grid=(rows // tr,), name=name,
                          in_specs=[pl.BlockSpec((4, tr, cols), lambda i: (0, i, 0)), blk, blk, blk],
                          out_specs=[blk] * 4, out_shape=[out] * 4,
                          compiler_params=_params("arbitrary"))(parts, w, m, v)


def _adamw_small(gs, ws, ms, vs):
    n = len(ws)

    def body(*refs):
        g_refs, w_refs, m_refs, v_refs = (refs[k * n:(k + 1) * n] for k in range(4))
        d_refs, m2_refs, v2_refs = (refs[(4 + k) * n:(5 + k) * n] for k in range(3))
        for t in range(n):
            d, m2, v2 = _adamw_math(w_refs[t][...], g_refs[t][...], m_refs[t][...], v_refs[t][...])
            d_refs[t][...] = d
            m2_refs[t][...] = m2
            v2_refs[t][...] = v2

    outs = pl.pallas_call(body, name="adamw_small",
                          out_shape=[jax.ShapeDtypeStruct(w.shape, F32) for w in ws] * 3)(*gs, *ws, *ms, *vs)
    return outs[:n], outs[n:2 * n], outs[2 * n:]


_PACK_TILE = 8 * 128


def _pack(arrays):
    rows = []
    for a in arrays:
        flat = a.astype(F32).reshape(-1)
        rows.append(jnp.pad(flat, (0, -flat.shape[0] % _PACK_TILE)).reshape(-1, 128))
    return jnp.concatenate(rows, axis=0)


def _unpack(pack, shapes):
    out, r = [], 0
    for s in shapes:
        n = int(np.prod(s))
        out.append(pack[r:r + -(-n // 128)].reshape(-1)[:n].reshape(s))
        r += -(-n // _PACK_TILE) * 8
    return out


def _pad128(v):
    v = v.reshape(1, -1).astype(F32)
    return jnp.pad(v, ((0, 0), (0, 128 - v.shape[1])))


_WEIGHTS = ["meta", "norm_mix_w", "w_in", "pool_w", "pool_scale", "conv_w", "conv_b", "dt_bias", "a_log", "d_skip",
            "ssm_norm_w", "w_out", "norm_ffn_w", "w_ff1", "w_ff2", "norm_f_w"]
_BIG = ["w_in", "w_out", "w_ff1", "w_ff2"]
_SMALL = [n for n in _WEIGHTS if n not in _BIG]


def kernel(x, meta, norm_mix_w, w_in, pool_w, pool_scale, conv_w, conv_b, dt_bias, a_log, d_skip, ssm_norm_w, w_out, norm_ffn_w, w_ff1, w_ff2, norm_f_w, loss_target, m_meta, m_norm_mix_w, m_w_in, m_pool_w, m_pool_scale, m_conv_w, m_conv_b, m_dt_bias, m_a_log, m_d_skip, m_ssm_norm_w, m_w_out, m_norm_ffn_w, m_w_ff1, m_w_ff2, m_norm_f_w, v_meta, v_norm_mix_w, v_w_in, v_pool_w, v_pool_scale, v_conv_w, v_conv_b, v_dt_bias, v_a_log, v_d_skip, v_ssm_norm_w, v_w_out, v_norm_ffn_w, v_w_ff1, v_w_ff2, v_norm_f_w):
    wts = dict(meta=meta, norm_mix_w=norm_mix_w, w_in=w_in, pool_w=pool_w, pool_scale=pool_scale, conv_w=conv_w,
               conv_b=conv_b, dt_bias=dt_bias, a_log=a_log, d_skip=d_skip, ssm_norm_w=ssm_norm_w, w_out=w_out,
               norm_ffn_w=norm_ffn_w, w_ff1=w_ff1, w_ff2=w_ff2, norm_f_w=norm_f_w)
    mom1 = dict(zip(_WEIGHTS, (m_meta, m_norm_mix_w, m_w_in, m_pool_w, m_pool_scale, m_conv_w, m_conv_b, m_dt_bias,
                               m_a_log, m_d_skip, m_ssm_norm_w, m_w_out, m_norm_ffn_w, m_w_ff1, m_w_ff2, m_norm_f_w)))
    mom2 = dict(zip(_WEIGHTS, (v_meta, v_norm_mix_w, v_w_in, v_pool_w, v_pool_scale, v_conv_w, v_conv_b, v_dt_bias,
                               v_a_log, v_d_skip, v_ssm_norm_w, v_w_out, v_norm_ffn_w, v_w_ff1, v_w_ff2, v_norm_f_w)))
    xi, yi, ci = _coords()
    dev = 4 * xi + 2 * yi + ci
    win_cols = w_in.shape[-1]
    cw_cols = conv_w.shape[-1]

    nb, seq, _ = x.shape
    core = jnp.reshape(ci, (1,)).astype(jnp.int32)
    owners = lambda a: a.reshape((4, 2) + a.shape[1:])

    lead_pack = jnp.zeros((N_META, 512), F32)
    lead_pack = lead_pack.at[:, :128].set(meta).at[:4, 128:128 + cw_cols].set(conv_w[0])
    g_win, g_lead = _weight_gather([_mx(w_in[0]), lead_pack])
    win_full = _assemble_bands(g_win, PROJ_W)
    meta_full = jnp.transpose(g_lead[:, :, :128], (1, 0, 2)).reshape(N_META, D_MODEL)
    cw_full = jnp.transpose(g_lead[:, :4, 128:128 + cw_cols], (1, 0, 2)).reshape(4, D_XBC)

    lead = jnp.concatenate([jnp.zeros((PAD_ROWS, D_MODEL), F32), meta_full] * nb, axis=0)
    x_rows = x.reshape(nb * seq, D_MODEL)
    tgt_rows = loss_target.reshape(nb * seq, D_MODEL)
    dt_bias_p, a_log_p = _pad128(dt_bias), _pad128(a_log)
    d_x = jnp.repeat(d_skip.reshape(1, N_HEADS).astype(F32), HEAD_DIM, axis=1)
    norm_f_row = norm_f_w.reshape(1, D_MODEL)

    hn1, proj = _in_proj(x_rows, lead, norm_mix_w, win_full)
    late_cols = [False, True, False]
    y, ypre, conv_pre, states, (g_wout, wff1_full, g_wff2) = _mixer_fwd(
        proj, cw_full, conv_b, dt_bias_p, a_log_p, d_x, ssm_norm_w, pool_w[0], pool_scale, nb,
        [_mx(w_out[0]), _mx(w_ff1[0]), _mx(w_ff2[0])], late_cols)
    wout_full = g_wout.reshape(D_MIX, D_MODEL)
    wff2_full = g_wff2.reshape(D_FF, D_MODEL)
    loss, gr_nf, gr_nffn, ff, da, hn2, dh1, dh2, dy = _ffn_fwd_bwd(
        x_rows, lead, y, tgt_rows, wout_full, norm_ffn_w, wff1_full, wff2_full, norm_f_row)
    gr_wff2 = _tn_matmul(ff, dh2, "grad_w_ff2", tka=1024, max_slab=2816)
    gr_wff1 = _tn_matmul(hn2, da, "grad_w_ff1", tka=512)
    gr_wout = _tn_matmul(y, dh1, "grad_w_out", tka=1024, max_slab=2816)

    by_owner = lambda k: [owners(gr_wout[k].reshape(N_DEV, D_MIX // N_DEV, D_MODEL)), gr_wff1[k],
                          owners(gr_wff2[k].reshape(N_DEV, D_FF // N_DEV, D_MODEL))]
    late_parts = by_owner(0)
    late_got = _grad_exchange_d2d(by_owner(1), late_cols, "grad_exchange_d2d_late")
    late_sums = [_chip_sum(late_parts[t], late_cols[t], late_got[t], core, "chip_sum_late_%d" % t) for t in range(3)]
    (dproj, gr_cw, gr_cb, gr_dtb, gr_alog, gr_d, gr_nw, gr_pw, gr_ps), late_exchanged = _mixer_bwd(
        proj, dy, ypre, conv_pre, states, cw_full, dt_bias_p, a_log_p, d_x, ssm_norm_w, pool_w[0], pool_scale, nb,
        late_sums)

    win_parts = [owners(_tn_matmul_banded(hn1, dproj, "grad_w_in", win_cols, tka=512))]
    win_got = _grad_exchange_d2d(win_parts, [False], "grad_exchange_d2d_w_in")
    win_sum = _chip_sum(win_parts[0], False, win_got[0], core, "chip_sum_w_in")
    gx_rows, gr_nmix, gr_meta, win_exchanged = _in_proj_bwd(dproj, x_rows, lead, dh1, norm_mix_w, win_full, [win_sum])
    parts = dict(w_in=win_exchanged[0], w_out=late_exchanged[0], w_ff1=late_exchanged[1], w_ff2=late_exchanged[2])

    small_full = dict(meta=gr_meta, norm_mix_w=gr_nmix, pool_w=gr_pw, pool_scale=gr_ps, conv_w=gr_cw, conv_b=gr_cb,
                      dt_bias=gr_dtb[:, :N_HEADS], a_log=gr_alog[:, :N_HEADS], d_skip=gr_d[:, :N_HEADS],
                      ssm_norm_w=gr_nw, norm_ffn_w=gr_nffn, norm_f_w=gr_nf, loss=loss[0:1, 0:1])
    small_names = list(small_full)
    small_sum = _small_allreduce(_pack([small_full[n] for n in small_names]))
    gs = dict(zip(small_names, _unpack(small_sum, [small_full[n].shape for n in small_names])))
    gs["meta"] = lax.dynamic_slice_in_dim(gs["meta"], dev * 128, 128, axis=1)
    gs["conv_w"] = lax.dynamic_slice_in_dim(gs["conv_w"], dev * cw_cols, cw_cols, axis=1)

    res = {}
    for n in _BIG:
        shp = wts[n].shape
        res[n] = [o.reshape(shp) for o in _adamw(parts[n], wts[n][0], mom1[n][0], mom2[n][0], "adamw_" + n)]
    as2d = lambda a: a.reshape(-1, a.shape[-1])
    small_g = [as2d(gs[n].reshape(wts[n].shape)) for n in _SMALL]
    small_out = _adamw_small(small_g, *[[as2d(d[n]) for n in _SMALL] for d in (wts, mom1, mom2)])
    for k, n in enumerate(_SMALL):
        res[n] = [o[k].reshape(wts[n].shape) for o in (small_g,) + tuple(small_out)]

    grad_x = gx_rows.reshape(nb, seq, D_MODEL)
    return (gs["loss"][0, 0], grad_x, *[res[n][0] for n in _WEIGHTS], *[res[n][1] for n in _WEIGHTS],
            *[res[n][2] for n in _WEIGHTS], *[res[n][3] for n in _WEIGHTS])
```

```python
import functools

import numpy as np
import jax
import jax.numpy as jnp
from jax import lax
from jax.experimental import pallas as pl
from jax.experimental.pallas import tpu as pltpu

F32 = jnp.float32
MXU_DTYPE = jnp.bfloat16

D_MODEL = 1024
D_POOL = 512
D_SSM = 1536
D_XBC = 2560
N_HEADS = 24
HEAD_DIM = 64
N_GROUPS = 4
GROUP_CH = D_SSM // N_GROUPS
D_STATE = 128
CHUNK = 128
N_META = 16
LEAD = CHUNK
PAD_ROWS = LEAD - N_META
ROW_TILE = 2 * CHUNK
D_MIX = D_POOL + D_SSM
D_FF = 4096
PROJ_W = 4736
OFF_Z = D_POOL
OFF_X = D_POOL + D_SSM
OFF_DT = OFF_X + D_XBC
D_IN_PROJ = OFF_DT + N_HEADS
POOL_WINDOWS = (2, 4, 8, 16)
HALO = 16
EPS = 1e-5
N_DEV = 8

ADAM_LR, ADAM_B1, ADAM_B2, ADAM_EPS, ADAM_WD, ADAM_STEP = 0.001, 0.9, 0.999, 1e-08, 0.01, 10

VMEM_LIMIT = 60 * 1024 * 1024


def _mx(a):
    return a.astype(MXU_DTYPE)


def _dot(a, b):
    return jnp.dot(_mx(a), _mx(b), preferred_element_type=F32)


def _dot_nt(a, b):
    return lax.dot_general(_mx(a), _mx(b), (((1,), (1,)), ((), ())), preferred_element_type=F32)


def _dot_tn(a, b):
    return lax.dot_general(_mx(a), _mx(b), (((0,), (0,)), ((), ())), preferred_element_type=F32)


def _split3(x):
    hi = x.astype(MXU_DTYPE)
    r = x - hi.astype(F32)
    mid = r.astype(MXU_DTYPE)
    lo = (r - mid.astype(F32)).astype(MXU_DTYPE)
    return hi, mid, lo


def _exact_l(c, x):
    hi, mid, lo = _split3(x)
    f = lambda p: jnp.dot(c, p, preferred_element_type=F32)
    return f(hi) + f(mid) + f(lo)


def _exact_r(x, c):
    hi, mid, lo = x if isinstance(x, tuple) else _split3(x)
    f = lambda p: jnp.dot(p, c, preferred_element_type=F32)
    return f(hi) + f(mid) + f(lo)


def _contract(x, c):
    hi = x.astype(MXU_DTYPE)
    lo = (x - hi.astype(F32)).astype(MXU_DTYPE)
    return jnp.dot(hi, c, preferred_element_type=F32) + jnp.dot(lo, c, preferred_element_type=F32)


def _sigmoid(x):
    return jax.nn.sigmoid(x)


def _softplus(x):
    return jnp.maximum(x, 0.0) + jnp.log1p(jnp.exp(-jnp.abs(x)))


def _silu_and_grad(x):
    s = _sigmoid(x)
    y = x * s
    return y, s + y * (1.0 - s)


def _shift_up(ext, s):
    if s == 0:
        return ext[:CHUNK, :]
    return pltpu.roll(ext, ext.shape[0] - s, 0)[:CHUNK, :]


def _by_pool_group(lane, a2, a4, a8, a16):
    return jnp.where(lane < 128, a2, jnp.where(lane < 256, a4, jnp.where(lane < 384, a8, a16)))


def _pool_inv_count(chunk_idx):
    row = lax.broadcasted_iota(jnp.int32, (CHUNK, D_POOL), 0)
    lane = lax.broadcasted_iota(jnp.int32, (CHUNK, D_POOL), 1)
    pos1 = jnp.maximum(chunk_idx * CHUNK + row - (PAD_ROWS - 1), 1)
    w = _by_pool_group(lane, 2, 4, 8, 16)
    return 1.0 / jnp.minimum(pos1, w).astype(F32), lane


def _pool_window_sums(u_ext, lane):
    s2 = u_ext + pltpu.roll(u_ext, 1, 0)
    s4 = s2 + pltpu.roll(s2, 2, 0)
    s8 = s4 + pltpu.roll(s4, 4, 0)
    s16 = s8 + pltpu.roll(s8, 8, 0)
    return _by_pool_group(lane, s2[HALO:], s4[HALO:], s8[HALO:], s16[HALO:])


def _pool_window_sums_ahead(q_ext, lane):
    n = q_ext.shape[0]
    r2 = q_ext + pltpu.roll(q_ext, n - 1, 0)
    r4 = r2 + pltpu.roll(r2, n - 2, 0)
    r8 = r4 + pltpu.roll(r4, n - 4, 0)
    r16 = r8 + pltpu.roll(r8, n - 8, 0)
    return _by_pool_group(lane, r2[:CHUNK], r4[:CHUNK], r8[:CHUNK], r16[:CHUNK])


def _conv_pre(ext, xbc, cw, cb):
    s1 = pltpu.roll(ext, 1, 0)
    near = cw[3:4, :] * xbc + cw[2:3, :] * s1[HALO:, :]
    far = cw[1:2, :] * ext + cw[0:1, :] * s1
    return cb + near + pltpu.roll(far, 2, 0)[HALO:, :]


def _dt_and_cumsum(dtr, dt_bias, a_log, valid, tril):
    lane = lax.broadcasted_iota(jnp.int32, (CHUNK, 128), 1)
    head = lane < N_HEADS
    pre = dtr + dt_bias
    dt = jnp.where(valid & head, _softplus(pre), 0.0)
    a_row = jnp.where(head[0:1, :], -jnp.exp(a_log), 0.0)
    a_col = _exact_l(tril, dt * a_row)
    return dt, a_row, a_col, pre, head


def _decay(a_col, a_row_t, h, causal):
    seg = a_col[:, h:h + 1] - a_row_t[h:h + 1, :]
    return jnp.where(causal, jnp.exp(jnp.minimum(seg, 0.0)), 0.0)


def _ssd_chunk_fwd(xs, bm, cm, dt, a_col, s_prev, d_x, e_mat, et_f32):
    lane = lax.broadcasted_iota(jnp.int32, (CHUNK, 128), 1)
    rowi = lax.broadcasted_iota(jnp.int32, (CHUNK, CHUNK), 0)
    coli = lax.broadcasted_iota(jnp.int32, (CHUNK, CHUNK), 1)
    causal = rowi >= coli
    a_row_t = a_col.T
    ax = _exact_r(a_col, e_mat)
    dtx = _exact_r(dt, e_mat)
    xdt = xs * dtx
    ax_last = ax[CHUNK - 1:CHUNK, :]
    e_a = jnp.exp(ax)
    w_end = xdt * jnp.exp(ax_last - ax)
    cd_col = jnp.exp(jnp.sum(et_f32 * a_col[CHUNK - 1:CHUNK, :], axis=1, keepdims=True))
    ys, s_new = [], []
    for g in range(N_GROUPS):
        gs = slice(g * GROUP_CH, (g + 1) * GROUP_CH)
        bg = bm[:, g * D_STATE:(g + 1) * D_STATE]
        cg = cm[:, g * D_STATE:(g + 1) * D_STATE]
        sg = s_prev[gs, :]
        cb = _dot_nt(cg, bg)
        y_off = _dot_nt(cg, sg) * e_a[:, gs]
        s_new.append(sg * cd_col[gs, :] + _dot_tn(w_end[:, gs], bg))
        for pr in range(3):
            c0 = g * GROUP_CH + pr * 128
            xdt_p = xdt[:, c0:c0 + 128]
            h0 = g * 6 + pr * 2
            y0 = _dot(cb * _decay(a_col, a_row_t, h0, causal), xdt_p)
            y1 = _dot(cb * _decay(a_col, a_row_t, h0 + 1, causal), xdt_p)
            ys.append(jnp.where(lane < HEAD_DIM, y0, y1) + y_off[:, pr * 128:(pr + 1) * 128])
    y = jnp.concatenate(ys, axis=1) + d_x * xs
    return y, jnp.concatenate(s_new, axis=0)


def _ssd_chunk_bwd(xs, bm, cm, dt, a_row, a_col, s_prev, ds_new, dy, d_x, e_mat, et_mat, et_f32, triu):
    lane = lax.broadcasted_iota(jnp.int32, (CHUNK, 128), 1)
    sub = lax.broadcasted_iota(jnp.int32, (CHUNK, 128), 0)
    rowi = lax.broadcasted_iota(jnp.int32, (CHUNK, CHUNK), 0)
    coli = lax.broadcasted_iota(jnp.int32, (CHUNK, CHUNK), 1)
    causal = rowi >= coli
    a_row_t = a_col.T
    a_last = a_col[CHUNK - 1:CHUNK, :]
    a_split, dt_split = _split3(a_col), _split3(dt)
    sub8 = lax.broadcasted_iota(jnp.int32, (8, GROUP_CH), 0)

    dxs, dbs, dcs, dsp = [], [], [], []
    zcol = jnp.zeros((CHUNK, 128), F32)
    zrows = []
    da_col = jnp.zeros((CHUNK, 128), F32)
    ddt = jnp.zeros((CHUNK, 128), F32)
    head_sums = jnp.zeros((8, 128), F32)
    q_row = jnp.zeros((1, 128), F32)
    for g in range(N_GROUPS):
        gs = slice(g * GROUP_CH, (g + 1) * GROUP_CH)
        e_g, et_g = e_mat[:, gs], et_mat[gs, :]
        xs_g, dy_g = xs[:, gs], dy[:, gs]
        ax = _exact_r(a_split, e_g)
        dtx = _exact_r(dt_split, e_g)
        xdt = xs_g * dtx
        dte = jnp.exp(ax[CHUNK - 1:CHUNK, :] - ax)
        w_end = xdt * dte
        cd_col = jnp.exp(jnp.sum(et_f32[gs, :] * a_last, axis=1, keepdims=True))
        dye = dy_g * jnp.exp(ax)
        bg = bm[:, g * D_STATE:(g + 1) * D_STATE]
        cg = cm[:, g * D_STATE:(g + 1) * D_STATE]
        sg = s_prev[gs, :]
        dsg = ds_new[gs, :]
        cb = _dot_nt(cg, bg)
        cs = _dot_nt(cg, sg)
        dcg = _dot(dye, sg)
        dsp.append(dsg * cd_col + _dot_tn(dye, cg))
        dwg = _dot_nt(bg, dsg)
        dbg = _dot(w_end, dsg)
        ww = dwg * w_end
        t1 = jnp.sum(dsg * sg, axis=1, keepdims=True) * cd_col
        dcb = jnp.zeros((CHUNK, CHUNK), F32)
        pairs = []
        for pr in range(3):
            ps = slice(pr * 128, (pr + 1) * 128)
            xdt_p, dy_p = xdt[:, ps], dy_g[:, ps]
            acc = None
            for half in range(2):
                h = g * 6 + pr * 2 + half
                ld = _decay(a_col, a_row_t, h, causal)
                gm = cb * ld
                dym = jnp.where((lane < HEAD_DIM) if half == 0 else (lane >= HEAD_DIM), dy_p, 0.0)
                dg = _dot_nt(dym, xdt_p)
                dseg = dg * gm
                dcb = dcb + dg * ld
                t = _dot_tn(gm, dym)
                acc = t if acc is None else acc + t
                zcol = jnp.where(lane == h, jnp.sum(dseg, axis=1, keepdims=True), zcol)
                zrows.append(jnp.sum(dseg, axis=0, keepdims=True))
            pairs.append(acc)
        dxdt = dwg * dte + jnp.concatenate(pairs, axis=1)
        dcs.append(dcg + _dot(dcb, bg))
        dbs.append(dbg + _dot_tn(dcb, cg))
        dxs.append(dxdt * dtx + d_x[:, gs] * dy_g)
        da_col = da_col + _contract(dye * cs - ww, et_g)
        ddt = ddt + _contract(dxdt * xs_g, et_g)
        col_sums = jnp.where(sub8 == 0, jnp.sum(dy_g * xs_g, axis=0, keepdims=True),
                             jnp.where(sub8 == 1, jnp.sum(ww, axis=0, keepdims=True), 0.0))
        head_sums = head_sums + _exact_r(col_sums, et_g)
        q_row = q_row + jnp.sum(et_f32[gs, :] * t1, axis=0, keepdims=True)

    dd = head_sums[0:1, :]
    q_row = q_row + head_sums[1:2, :]
    zrow = jnp.concatenate(zrows + [jnp.zeros((128 - N_HEADS, CHUNK), F32)], axis=0)
    da_col = da_col + zcol - zrow.T + jnp.where(sub == CHUNK - 1, q_row, 0.0)
    rc = _exact_l(triu, da_col)
    ddt = ddt + rc * a_row
    da = jnp.sum(rc * dt, axis=0, keepdims=True)
    return (jnp.concatenate(dxs, axis=1), jnp.concatenate(dbs, axis=1), jnp.concatenate(dcs, axis=1), ddt, da, dd,
            jnp.concatenate(dsp, axis=0))


def _ssd_constants():
    iota = lambda shape, d: lax.broadcasted_iota(jnp.int32, shape, d)
    e = iota((128, D_SSM), 0) == iota((128, D_SSM), 1) // HEAD_DIM
    et = iota((D_SSM, 128), 1) == iota((D_SSM, 128), 0) // HEAD_DIM
    tril = iota((CHUNK, CHUNK), 1) <= iota((CHUNK, CHUNK), 0)
    triu = iota((CHUNK, CHUNK), 1) >= iota((CHUNK, CHUNK), 0)
    return (e.astype(MXU_DTYPE), et.astype(MXU_DTYPE), et.astype(F32), tril.astype(MXU_DTYPE), triu.astype(MXU_DTYPE))


def _full(shape):
    nd = len(shape)
    return pl.BlockSpec(shape, lambda *_: (0,) * nd)


def _params(*sem):
    return pltpu.CompilerParams(dimension_semantics=sem, vmem_limit_bytes=VMEM_LIMIT)


def _token_tiles(width, n_tok_tiles):
    return pl.BlockSpec((ROW_TILE, width), lambda i: (jnp.minimum(i, n_tok_tiles - 1), 0))


def _in_proj(x, lead, w1, win):
    nt = x.shape[0] // ROW_TILE
    m = x.shape[0] + ROW_TILE
    tm = ROW_TILE

    def body(x_ref, lead_ref, w1_ref, win_hbm, hn_ref, proj_ref, win_v, sem):
        i = pl.program_id(0)

        @pl.when(i == 0)
        def _():
            cp = pltpu.make_async_copy(win_hbm, win_v, sem)
            cp.start()
            cp.wait()

        x = jnp.where(i == nt, lead_ref[...], x_ref[...])
        r = lax.rsqrt(jnp.mean(x * x, axis=-1, keepdims=True) + EPS)
        hn = _mx(x * r * w1_ref[...])
        hn_ref[...] = hn
        for j in range(0, PROJ_W, 512):
            w = min(512, PROJ_W - j)
            proj_ref[:, j:j + w] = jnp.dot(hn, win_v[:, j:j + w], preferred_element_type=F32)

    return pl.pallas_call(
        body, grid=(m // tm,), name="in_proj",
        in_specs=[_token_tiles(D_MODEL, nt), _full((ROW_TILE, D_MODEL)), _full((1, D_MODEL)),
                  pl.BlockSpec(memory_space=pl.ANY)],
        out_specs=[pl.BlockSpec((tm, D_MODEL), lambda i: (i, 0)), pl.BlockSpec((tm, PROJ_W), lambda i: (i, 0))],
        out_shape=[jax.ShapeDtypeStruct((m, D_MODEL), MXU_DTYPE), jax.ShapeDtypeStruct((m, PROJ_W), F32)],
        scratch_shapes=[pltpu.VMEM((D_MODEL, PROJ_W), MXU_DTYPE), pltpu.SemaphoreType.DMA],
        compiler_params=_params("arbitrary"),
    )(x, lead, w1, win)


def _ffn_fwd_bwd(x, lead, y, tgt, wout, w2n, wff1, wff2, wfn):
    nt = x.shape[0] // ROW_TILE
    m = x.shape[0] + ROW_TILE
    tm = ROW_TILE
    nj = D_FF // 1024

    def body(x_ref, lead_ref, y_ref, tgt_ref, w2n_ref, wfn_ref, wout_hbm, wff1_hbm, wff2_hbm,
             loss_ref, gwf_ref, gw2_ref, ff_ref, da_ref, hn2_ref, dh1_ref, dh2_ref, dy_ref,
             wout_v, wff1_v, wff2_v, a_s, sems):
        i = pl.program_id(0)
        hp = jnp.where(i == nt, lead_ref[...], x_ref[...])

        @pl.when(i == 0)
        def _():
            cps = [pltpu.make_async_copy(s, d, sems.at[k])
                   for k, (s, d) in enumerate(((wout_hbm, wout_v), (wff1_hbm, wff1_v), (wff2_hbm, wff2_v)))]
            for cp in cps:
                cp.start()
            for cp in cps:
                cp.wait()
            loss_ref[...] = jnp.zeros_like(loss_ref)
            gwf_ref[...] = jnp.zeros_like(gwf_ref)
            gw2_ref[...] = jnp.zeros_like(gw2_ref)

        h1 = hp + jnp.dot(y_ref[...], wout_v[...], preferred_element_type=F32)
        r2 = lax.rsqrt(jnp.mean(h1 * h1, axis=-1, keepdims=True) + EPS)
        n2 = h1 * r2
        w2n_row = w2n_ref[...]
        hn2 = _mx(n2 * w2n_row)
        hn2_ref[...] = hn2
        h2 = h1
        for j in range(nj):
            js = slice(j * 1024, (j + 1) * 1024)
            a = jnp.dot(hn2, wff1_v[:, js], preferred_element_type=F32)
            a_s[:, js] = a
            ra = jnp.maximum(a, 0.0)
            ff = _mx(ra * ra)
            ff_ref[:, js] = ff
            h2 = h2 + jnp.dot(ff, wff2_v[js, :], preferred_element_type=F32)

        r3 = lax.rsqrt(jnp.mean(h2 * h2, axis=-1, keepdims=True) + EPS)
        n3 = h2 * r3
        wf_row = wfn_ref[...]
        err = n3 * wf_row - tgt_ref[...]
        tokf = (i < nt).astype(F32)
        loss_ref[...] += 0.5 * jnp.sum(jnp.mean(err * err, axis=-1, keepdims=True) * tokf)
        dout = err * (tokf / D_MODEL)
        gwf_ref[...] += jnp.sum(dout * n3, axis=0, keepdims=True)
        dn3 = dout * wf_row
        dh2 = r3 * (dn3 - n3 * jnp.mean(dn3 * n3, axis=-1, keepdims=True))
        dh2m = _mx(dh2)
        dh2_ref[...] = dh2m

        dhn2 = jnp.zeros((tm, D_MODEL), F32)
        for j in range(nj):
            js = slice(j * 1024, (j + 1) * 1024)
            dff = lax.dot_general(dh2m, wff2_v[js, :], (((1,), (1,)), ((), ())), preferred_element_type=F32)
            da = _mx(dff * (2.0 * jnp.maximum(a_s[:, js], 0.0)))
            da_ref[:, js] = da
            dhn2 = dhn2 + lax.dot_general(da, wff1_v[:, js], (((1,), (1,)), ((), ())), preferred_element_type=F32)
        gw2_ref[...] += jnp.sum(dhn2 * n2, axis=0, keepdims=True)
        dn2 = dhn2 * w2n_row
        dh1 = dh2 + r2 * (dn2 - n2 * jnp.mean(dn2 * n2, axis=-1, keepdims=True))
        dh1_ref[...] = dh1
        dy_ref[...] = lax.dot_general(_mx(dh1), wout_v[...], (((1,), (1,)), ((), ())), preferred_element_type=F32)

    rows = lambda w: pl.BlockSpec((tm, w), lambda i: (i, 0))
    hbm = pl.BlockSpec(memory_space=pl.ANY)
    return pl.pallas_call(
        body, grid=(m // tm,), name="ffn_fwd_bwd",
        in_specs=[_token_tiles(D_MODEL, nt), _full((ROW_TILE, D_MODEL)), rows(D_MIX), _token_tiles(D_MODEL, nt),
                  _full((1, D_MODEL)), _full((1, D_MODEL)), hbm, hbm, hbm],
        out_specs=[_full((1, 128)), _full((1, D_MODEL)), _full((1, D_MODEL)), rows(D_FF), rows(D_FF), rows(D_MODEL),
                   rows(D_MODEL), rows(D_MODEL), rows(D_MIX)],
        out_shape=[jax.ShapeDtypeStruct((1, 128), F32), jax.ShapeDtypeStruct((1, D_MODEL), F32),
                   jax.ShapeDtypeStruct((1, D_MODEL), F32), jax.ShapeDtypeStruct((m, D_FF), MXU_DTYPE),
                   jax.ShapeDtypeStruct((m, D_FF), MXU_DTYPE), jax.ShapeDtypeStruct((m, D_MODEL), MXU_DTYPE),
                   jax.ShapeDtypeStruct((m, D_MODEL), F32), jax.ShapeDtypeStruct((m, D_MODEL), MXU_DTYPE),
                   jax.ShapeDtypeStruct((m, D_MIX), F32)],
        scratch_shapes=[pltpu.VMEM((D_MIX, D_MODEL), MXU_DTYPE), pltpu.VMEM((D_MODEL, D_FF), MXU_DTYPE),
                        pltpu.VMEM((D_FF, D_MODEL), MXU_DTYPE), pltpu.VMEM((tm, D_FF), F32),
                        pltpu.SemaphoreType.DMA((3,))],
        compiler_params=_params("arbitrary"),
    )(x, lead, y, tgt, w2n, wfn, wout, wff1, wff2)


def _in_proj_bwd(dproj, x, lead, dh1, w1, win, chip_sums):
    nt = x.shape[0] // ROW_TILE
    m = x.shape[0] + ROW_TILE
    tm = ROW_TILE
    ns = len(chip_sums)

    def body(dp_ref, x_ref, lead_ref, dh1_ref, w1_ref, win_hbm, *rest):
        cs_refs, (gx_ref, gw1_ref, gmeta_ref) = rest[:ns], rest[ns:ns + 3]
        part_refs, (win_v, sem) = rest[ns + 3:2 * ns + 3], rest[2 * ns + 3:2 * ns + 5]
        exchange = _ChipExchange(cs_refs, part_refs, [False] * ns, *rest[2 * ns + 5:])
        i = pl.program_id(0)

        @pl.when(i == 0)
        def _():
            exchange.start()
            cp = pltpu.make_async_copy(win_hbm, win_v, sem)
            cp.start()
            cp.wait()
            gw1_ref[...] = jnp.zeros_like(gw1_ref)
            gmeta_ref[...] = jnp.zeros_like(gmeta_ref)

        dhn = lax.dot_general(dp_ref[...], win_v[...], (((1,), (1,)), ((), ())), preferred_element_type=F32)
        x = jnp.where(i == nt, lead_ref[...], x_ref[...])
        r = lax.rsqrt(jnp.mean(x * x, axis=-1, keepdims=True) + EPS)
        n = x * r
        gw1_ref[...] += jnp.sum(dhn * n, axis=0, keepdims=True)
        dn = dhn * w1_ref[...]
        dh0 = dh1_ref[...] + r * (dn - n * jnp.mean(dn * n, axis=-1, keepdims=True))

        @pl.when(i < nt)
        def _():
            gx_ref[...] = dh0

        @pl.when(i == nt)
        def _():
            gmeta_ref[...] = dh0[PAD_ROWS:LEAD, :] + dh0[LEAD + PAD_ROWS:2 * LEAD, :]
            exchange.finish()

    rows = lambda w: pl.BlockSpec((tm, w), lambda i: (i, 0))
    hbm = pl.BlockSpec(memory_space=pl.ANY)
    outs = pl.pallas_call(
        body, grid=(m // tm,), name="in_proj_bwd",
        in_specs=[rows(PROJ_W), _token_tiles(D_MODEL, nt), _full((ROW_TILE, D_MODEL)), rows(D_MODEL),
                  _full((1, D_MODEL)), hbm] + [hbm] * ns,
        out_specs=[_token_tiles(D_MODEL, nt), _full((1, D_MODEL)), _full((N_META, D_MODEL))] + [hbm] * ns,
        out_shape=[jax.ShapeDtypeStruct(x.shape, F32), jax.ShapeDtypeStruct((1, D_MODEL), F32),
                   jax.ShapeDtypeStruct((N_META, D_MODEL), F32)]
        + _ChipExchange.out_shapes(chip_sums, [False] * ns),
        scratch_shapes=[pltpu.VMEM((D_MODEL, PROJ_W), MXU_DTYPE), pltpu.SemaphoreType.DMA] + _ChipExchange.scratch(ns),
        compiler_params=_params("arbitrary"),
    )(dproj, x, lead, dh1, w1, win, *chip_sums)
    return outs[0], outs[1], outs[2], outs[3:]


MXU_DEPTH = 256


def _row_slab(m, cap):
    return max(k for k in range(MXU_DEPTH, cap + 1, MXU_DEPTH) if m % k == 0)


def _tn_matmul(a, b, name, tka, max_slab=768, tn=512):
    m, ka = a.shape
    nb = b.shape[1]
    tkm = _row_slab(m, max_slab)
    n_steps = m // tkm

    def body(a_ref, b_ref, o_ref, omx_ref):
        k = pl.program_id(1)

        @pl.when(k == 0)
        def _():
            o_ref[...] = jnp.zeros_like(o_ref)

        at = _mx(a_ref[...])
        for j in range(0, nb, tn):
            w = min(tn, nb - j)
            o_ref[:, j:j + w] += lax.dot_general(at, _mx(b_ref[:, j:j + w]), (((0,), (0,)), ((), ())),
                                                 preferred_element_type=F32)

        @pl.when(k == n_steps - 1)
        def _():
            omx_ref[...] = _mx(o_ref[...])

    out = pl.BlockSpec((tka, nb), lambda i, k: (i, 0))
    return pl.pallas_call(
        body, grid=(ka // tka, n_steps), name=name,
        in_specs=[pl.BlockSpec((tkm, tka), lambda i, k: (k, i)), pl.BlockSpec((tkm, nb), lambda i, k: (k, 0))],
        out_specs=[out, out],
        out_shape=[jax.ShapeDtypeStruct((ka, nb), F32), jax.ShapeDtypeStruct((ka, nb), MXU_DTYPE)],
        compiler_params=_params("arbitrary", "arbitrary"),
    )(a, b)


def _tn_matmul_banded(a, b, name, band, tka, tn=512):
    m, ka = a.shape
    nb = b.shape[1]
    tkm = _row_slab(m, 768)
    n_steps = m // tkm

    def body(a_ref, b_ref, o_ref, acc):
        k = pl.program_id(1)

        @pl.when(k == 0)
        def _():
            acc[...] = jnp.zeros_like(acc)

        at = _mx(a_ref[...])
        for j in range(0, nb, tn):
            w = min(tn, nb - j)
            acc[:, j:j + w] += lax.dot_general(at, _mx(b_ref[:, j:j + w]), (((0,), (0,)), ((), ())),
                                               preferred_element_type=F32)

        @pl.when(k == n_steps - 1)
        def _():
            for j in range(N_DEV):
                o_ref[j] = acc[:, j * band:(j + 1) * band]

    return pl.pallas_call(
        body, grid=(ka // tka, n_steps), name=name,
        in_specs=[pl.BlockSpec((tkm, tka), lambda i, k: (k, i)), pl.BlockSpec((tkm, nb), lambda i, k: (k, 0))],
        out_specs=pl.BlockSpec((N_DEV, tka, band), lambda i, k: (0, i, 0)),
        out_shape=jax.ShapeDtypeStruct((N_DEV, ka, band), F32),
        scratch_shapes=[pltpu.VMEM((tka, nb), F32)],
        compiler_params=_params("arbitrary", "arbitrary"),
    )(a, b)


def _assemble_bands(g, width):
    n, rows, band = g.shape
    tr = 256

    def body(g_ref, o_ref):
        parts = [g_ref[j] for j in range(n)] + [jnp.zeros((tr, width - n * band), g.dtype)]
        o_ref[...] = jnp.concatenate(parts, axis=1)

    return pl.pallas_call(
        body, grid=(rows // tr,), name="assemble_w_in",
        in_specs=[pl.BlockSpec((n, tr, band), lambda i: (0, i, 0))],
        out_specs=pl.BlockSpec((tr, width), lambda i: (i, 0)),
        out_shape=jax.ShapeDtypeStruct((rows, width), g.dtype),
        compiler_params=_params("arbitrary"),
    )(g)


def _chunk_block(b, c, nb, nc):
    return jnp.where(c == 0, nb * (nc - 1) + b, b * (nc - 1) + c - 1)

def _mixer_fwd(proj, cw, cb, dt_bias, a_log, d_x, nw, pool_w, pool_scale, nb, shards, by_cols):
    m = proj.shape[0]
    nc = m // nb // CHUNK
    n_steps = nb * nc
    ns = len(shards)
    e_mat, _, et_f32, tril, _ = _ssd_constants()

    def body(p_ref, cw_ref, cb_ref, dtb_ref, alog_ref, dx_ref, nw_ref, pw_ref, ps_ref, e_ref, et_ref, tril_ref, *rest):
        shard_refs, (y_ref, ypre_ref, pre_ref, st_ref) = rest[:ns], rest[ns:ns + 4]
        gathered_refs, (xtail, utail, state) = rest[ns + 4:2 * ns + 4], rest[2 * ns + 4:2 * ns + 7]
        gather = _Gather(shard_refs, gathered_refs, by_cols, *rest[2 * ns + 7:])
        c = pl.program_id(1)
        step = pl.program_id(0) * nc + c

        @pl.when(step == 0)
        def _():
            gather.start()

        @pl.when(step == n_steps // 2)
        def _():
            gather.forward()

        @pl.when(c == 0)
        def _():
            xtail[...] = jnp.zeros_like(xtail)
            utail[...] = jnp.zeros_like(utail)
            state[...] = jnp.zeros_like(state)

        valid = (c > 0) | (lax.broadcasted_iota(jnp.int32, (CHUNK, 1), 0) >= PAD_ROWS)

        u = p_ref[:, 0:D_POOL]
        inv_cnt, lane = _pool_inv_count(c)
        win = _pool_window_sums(jnp.concatenate([utail[...], u], axis=0), lane)
        utail[...] = u[CHUNK - HALO:, :]
        pooled = win * inv_cnt - u
        mixed = jnp.concatenate(
            [_dot(pooled[:, g * 128:(g + 1) * 128], pw_ref[g]) for g in range(len(POOL_WINDOWS))], axis=1)
        y_ref[:, 0:D_POOL] = _mx(mixed * ps_ref[...])

        xbc = p_ref[:, OFF_X:OFF_X + D_XBC]
        pre = _conv_pre(jnp.concatenate([xtail[...], xbc], axis=0), xbc, cw_ref[...], cb_ref[...])
        xtail[...] = xbc[CHUNK - HALO:, :]
        pre_ref[...] = pre
        xc = pre * _sigmoid(pre)
        dt, _, a_col, _, _ = _dt_and_cumsum(p_ref[:, OFF_DT:OFF_DT + 128], dtb_ref[...], alog_ref[...], valid,
                                            tril_ref[...])
        s_prev = state[...]
        st_ref[0] = s_prev
        yp, s_new = _ssd_chunk_fwd(xc[:, 0:D_SSM], xc[:, D_SSM:D_SSM + 512], xc[:, D_SSM + 512:], dt, a_col, s_prev,
                                   dx_ref[...], e_ref[...], et_ref[...])
        state[...] = s_new
        ypre_ref[...] = yp
        z = p_ref[:, OFF_Z:OFF_Z + D_SSM]
        yz = yp * (z * _sigmoid(z))
        outs = []
        for g in range(N_GROUPS):
            gs = slice(g * GROUP_CH, (g + 1) * GROUP_CH)
            r = lax.rsqrt(jnp.mean(yz[:, gs] * yz[:, gs], axis=-1, keepdims=True) + EPS)
            outs.append(yz[:, gs] * r)
        y_ref[:, D_POOL:] = _mx(jnp.concatenate(outs, axis=1) * nw_ref[...])

        @pl.when(step == n_steps - 1)
        def _():
            gather.finish()

    blk = lambda w: pl.BlockSpec((CHUNK, w), lambda b, c: (_chunk_block(b, c, nb, nc), 0))
    hbm = pl.BlockSpec(memory_space=pl.ANY)
    outs = pl.pallas_call(
        body, grid=(nb, nc), name="mixer_fwd",
        in_specs=[blk(PROJ_W), _full((4, D_XBC)), _full((1, D_XBC)), _full((1, 128)), _full((1, 128)),
                  _full((1, D_SSM)), _full((1, D_SSM)), _full((4, 128, 128)), _full((1, D_POOL)),
                  _full((128, D_SSM)), _full((D_SSM, 128)), _full((CHUNK, CHUNK))] + [hbm] * ns,
        out_specs=[blk(D_MIX), blk(D_SSM), blk(D_XBC),
                   pl.BlockSpec((1, D_SSM, D_STATE), lambda b, c: (b * nc + c, 0, 0))] + [hbm] * ns,
        out_shape=[jax.ShapeDtypeStruct((m, D_MIX), MXU_DTYPE), jax.ShapeDtypeStruct((m, D_SSM), F32),
                   jax.ShapeDtypeStruct((m, D_XBC), F32), jax.ShapeDtypeStruct((m // CHUNK, D_SSM, D_STATE), F32)]
        + _Gather.out_shapes(shards, by_cols),
        scratch_shapes=[pltpu.VMEM((HALO, D_XBC), F32), pltpu.VMEM((HALO, D_POOL), F32),
                        pltpu.VMEM((D_SSM, D_STATE), F32)] + _Gather.scratch(ns),
        compiler_params=_params("arbitrary", "arbitrary"),
    )(proj, cw, cb, dt_bias, a_log, d_x, nw, pool_w, pool_scale, e_mat, et_f32, tril, *shards)
    return outs[0], outs[1], outs[2], outs[3], outs[4:]


def _mixer_bwd(proj, dy, ypre, conv_pre, states, cw, dt_bias, a_log, d_x, nw, pool_w, pool_scale, nb, chip_sums):
    m = proj.shape[0]
    nc = m // nb // CHUNK
    e_mat, et_mat, et_f32, tril, triu = _ssd_constants()
    hb = CHUNK // HALO
    ns = len(chip_sums)

    def body(p_ref, halo_ref, dy_ref, ypre_ref, pre_ref, st_ref, cw_ref, dtb_ref, alog_ref, dx_ref, nw_ref, pw_ref,
             ps_ref, e_ref, et_ref, etf_ref, tril_ref, triu_ref, *rest):
        cs_refs = rest[:ns]
        dp_ref, gcw_ref, gcb_ref, gdtb_ref, galog_ref, gd_ref, gnw_ref, gpw_ref, gps_ref = rest[ns:ns + 9]
        part_refs, (ds_carry, dpre_next, dq_next) = rest[ns + 9:2 * ns + 9], rest[2 * ns + 9:2 * ns + 12]
        exchange = _ChipExchange(cs_refs, part_refs, [False] * ns, *rest[2 * ns + 12:])
        b = pl.program_id(0)
        cc = pl.program_id(1)
        c = nc - 1 - cc

        @pl.when((b == 0) & (cc == 0))
        def _():
            exchange.start()
            for r in (gcw_ref, gcb_ref, gdtb_ref, galog_ref, gd_ref, gnw_ref, gpw_ref, gps_ref):
                r[...] = jnp.zeros_like(r)

        @pl.when(cc == 0)
        def _():
            ds_carry[...] = jnp.zeros_like(ds_carry)
            dpre_next[...] = jnp.zeros_like(dpre_next)
            dq_next[...] = jnp.zeros_like(dq_next)

        valid = (c > 0) | (lax.broadcasted_iota(jnp.int32, (CHUNK, 1), 0) >= PAD_ROWS)
        first = c > 0

        u = p_ref[:, 0:D_POOL]
        u_halo = jnp.where(first, halo_ref[...], 0.0)
        inv_cnt, lane = _pool_inv_count(c)
        pooled = _pool_window_sums(jnp.concatenate([u_halo, u], axis=0), lane) * inv_cnt - u
        dyp = dy_ref[:, 0:D_POOL]
        ps = ps_ref[...]
        dmixed = dyp * ps
        mixed, dpooled = [], []
        for g in range(len(POOL_WINDOWS)):
            gsl = slice(g * 128, (g + 1) * 128)
            pw = pw_ref[g]
            mixed.append(_dot(pooled[:, gsl], pw))
            dpooled.append(_dot_nt(dmixed[:, gsl], pw))
            gpw_ref[g] += _dot_tn(pooled[:, gsl], dmixed[:, gsl])
        gps_ref[...] += jnp.sum(dyp * jnp.concatenate(mixed, axis=1), axis=0, keepdims=True)
        dpooled = jnp.concatenate(dpooled, axis=1)
        dq = dpooled * inv_cnt
        du = _pool_window_sums_ahead(jnp.concatenate([dq, dq_next[...]], axis=0), lane) - dpooled
        dq_next[...] = dq[0:HALO, :]
        dp_ref[:, 0:D_POOL] = _mx(du)

        yp = ypre_ref[...]
        z = p_ref[:, OFF_Z:OFF_Z + D_SSM]
        sz, dsz = _silu_and_grad(z)
        yz = yp * sz
        do = dy_ref[:, D_POOL:]
        nw_row = nw_ref[...]
        dyz = []
        gnw = []
        for g in range(N_GROUPS):
            gs = slice(g * GROUP_CH, (g + 1) * GROUP_CH)
            r = lax.rsqrt(jnp.mean(yz[:, gs] * yz[:, gs], axis=-1, keepdims=True) + EPS)
            n = yz[:, gs] * r
            gnw.append(jnp.sum(do[:, gs] * n, axis=0, keepdims=True))
            dn = do[:, gs] * nw_row[:, gs]
            dyz.append(r * (dn - n * jnp.mean(dn * n, axis=-1, keepdims=True)))
        gnw_ref[...] += jnp.concatenate(gnw, axis=1)
        dyz = jnp.concatenate(dyz, axis=1)
        dp_ref[:, OFF_Z:OFF_Z + D_SSM] = _mx(dyz * yp * dsz)
        dyp_ssm = dyz * sz

        xc, dsilu = _silu_and_grad(pre_ref[...])
        dtr = p_ref[:, OFF_DT:OFF_DT + 128]
        dt, a_row, a_col, dt_pre, head = _dt_and_cumsum(dtr, dtb_ref[...], alog_ref[...], valid, tril_ref[...])
        dxs, dbm, dcm, ddt, da, dd, ds_prev = _ssd_chunk_bwd(
            xc[:, 0:D_SSM], xc[:, D_SSM:D_SSM + 512], xc[:, D_SSM + 512:], dt, a_row, a_col, st_ref[0],
            ds_carry[...], dyp_ssm, dx_ref[...], e_ref[...], et_ref[...], etf_ref[...], triu_ref[...])
        ds_carry[...] = ds_prev
        gd_ref[...] += dd
        galog_ref[...] += da * a_row
        ddtr = jnp.where(valid & head, ddt * _sigmoid(dt_pre), 0.0)
        gdtb_ref[...] += jnp.sum(ddtr, axis=0, keepdims=True)
        dp_ref[:, OFF_DT:OFF_DT + 128] = _mx(ddtr)

        dpre = jnp.concatenate([dxs, dbm, dcm], axis=1) * dsilu
        gcb_ref[...] += jnp.sum(dpre, axis=0, keepdims=True)
        dext = jnp.concatenate([dpre, dpre_next[...]], axis=0)
        dpre_next[...] = dpre[0:HALO, :]
        ups = [_shift_up(dext, 3 - k) for k in range(4)]
        xbc = p_ref[:, OFF_X:OFF_X + D_XBC]
        gcw_ref[...] += jnp.concatenate([jnp.sum(xbc * ups[k], axis=0, keepdims=True) for k in range(4)], axis=0)
        cw = cw_ref[...]
        dp_ref[:, OFF_X:OFF_X + D_XBC] = _mx(cw[3:4, :] * ups[3] + cw[2:3, :] * ups[2]
                                             + cw[1:2, :] * ups[1] + cw[0:1, :] * ups[0])

        @pl.when((b == nb - 1) & (cc == nc - 1))
        def _():
            exchange.finish()

    blk = lambda w: pl.BlockSpec((CHUNK, w), lambda b, cc: (_chunk_block(b, nc - 1 - cc, nb, nc), 0))
    halo = pl.BlockSpec((HALO, D_POOL),
                        lambda b, cc: (_chunk_block(b, jnp.maximum(nc - 2 - cc, 0), nb, nc) * hb + hb - 1, 0))
    hbm = pl.BlockSpec(memory_space=pl.ANY)
    outs = pl.pallas_call(
        body, grid=(nb, nc), name="mixer_bwd",
        in_specs=[blk(PROJ_W), halo, blk(D_MIX), blk(D_SSM), blk(D_XBC),
                  pl.BlockSpec((1, D_SSM, D_STATE), lambda b, cc: (b * nc + nc - 1 - cc, 0, 0)),
                  _full((4, D_XBC)), _full((1, 128)), _full((1, 128)), _full((1, D_SSM)),
                  _full((1, D_SSM)), _full((4, 128, 128)), _full((1, D_POOL)),
                  _full((128, D_SSM)), _full((D_SSM, 128)), _full((D_SSM, 128)), _full((CHUNK, CHUNK)),
                  _full((CHUNK, CHUNK))] + [hbm] * ns,
        out_specs=[blk(PROJ_W), _full((4, D_XBC)), _full((1, D_XBC)), _full((1, 128)), _full((1, 128)), _full((1, 128)),
                   _full((1, D_SSM)), _full((4, 128, 128)), _full((1, D_POOL))] + [hbm] * ns,
        out_shape=[jax.ShapeDtypeStruct((m, PROJ_W), MXU_DTYPE), jax.ShapeDtypeStruct((4, D_XBC), F32),
                   jax.ShapeDtypeStruct((1, D_XBC), F32), jax.ShapeDtypeStruct((1, 128), F32),
                   jax.ShapeDtypeStruct((1, 128), F32), jax.ShapeDtypeStruct((1, 128), F32),
                   jax.ShapeDtypeStruct((1, D_SSM), F32), jax.ShapeDtypeStruct((4, 128, 128), F32),
                   jax.ShapeDtypeStruct((1, D_POOL), F32)] + _ChipExchange.out_shapes(chip_sums, [False] * ns),
        scratch_shapes=[pltpu.VMEM((D_SSM, D_STATE), F32), pltpu.VMEM((HALO, D_XBC), F32),
                        pltpu.VMEM((HALO, D_POOL), F32)] + _ChipExchange.scratch(ns),
        compiler_params=_params("arbitrary", "arbitrary"),
    )(proj, proj, dy, ypre, conv_pre, states, cw, dt_bias, a_log, d_x, nw, pool_w, pool_scale, e_mat, et_mat, et_f32, tril, triu,
      *chip_sums)
    return outs[:9], outs[9:]


MESH_IDS = pl.DeviceIdType.MESH
_HBM = pl.BlockSpec(memory_space=pltpu.HBM)


def _coords():
    return lax.axis_index("x"), lax.axis_index("y"), lax.axis_index("c")


def _other_chips(x, y):
    return [(1 - x, y), (x, 1 - y), (1 - x, 1 - y)]


class _Gather:
    def __init__(self, ins, outs, by_cols, send_sems, recv_sems, local_sems):
        self.ins, self.outs, self.by_cols, self.n = ins, outs, by_cols, len(ins)
        self.send_sems, self.recv_sems, self.local_sems = send_sems, recv_sems, local_sems
        self.x, self.y, self.c = _coords()
        self.me, self.sibling = (self.x, self.y, self.c), (self.x, self.y, 1 - self.c)
        self.chips = _other_chips(self.x, self.y)

    @staticmethod
    def scratch(n):
        return [pltpu.SemaphoreType.DMA((7 * n,)), pltpu.SemaphoreType.DMA((7 * n,)), pltpu.SemaphoreType.DMA((n,))]

    @staticmethod
    def out_shapes(shards, by_cols):
        return [jax.ShapeDtypeStruct((s.shape[0], N_DEV * s.shape[1]) if cols else (N_DEV,) + s.shape, s.dtype)
                for s, cols in zip(shards, by_cols)]

    def _block(self, t, device):
        idx = 4 * device[0] + 2 * device[1] + device[2]
        if not self.by_cols[t]:
            return self.outs[t].at[idx]
        w = self.ins[t].shape[1]
        return self.outs[t].at[:, pl.ds(pl.multiple_of(idx * w, w), w)]

    def _copy(self, t, k, block, to, own=False):
        dst = self._block(t, block)
        return pltpu.make_async_remote_copy(
            src_ref=self.ins[t] if own else dst, dst_ref=dst, send_sem=self.send_sems.at[t * 7 + k],
            recv_sem=self.recv_sems.at[t * 7 + k], device_id=to, device_id_type=MESH_IDS)

    def _mine(self):
        return [pltpu.make_async_copy(self.ins[t], self._block(t, self.me), self.local_sems.at[t])
                for t in range(self.n)]

    def _first(self):
        cps = []
        for t in range(self.n):
            cps.append(self._copy(t, 0, self.me, self.sibling, own=True))
            cps += [self._copy(t, 1 + j, self.me, (*chip, self.c), own=True) for j, chip in enumerate(self.chips)]
        return cps

    def _passed(self):
        return [self._copy(t, 4 + j, (*chip, self.c), self.sibling)
                for j, chip in enumerate(self.chips) for t in range(self.n)]

    def start(self):
        for cp in self._mine() + self._first():
            cp.start()

    def forward(self):
        for j, chip in enumerate(self.chips):
            for t in range(self.n):
                self._copy(t, 1 + j, (*chip, self.c), self.me).wait_recv()
                self._copy(t, 4 + j, (*chip, self.c), self.sibling).start()

    def finish(self):
        for t in range(self.n):
            self._copy(t, 0, self.sibling, self.me).wait_recv()
            for j, chip in enumerate(self.chips):
                self._copy(t, 4 + j, (*chip, 1 - self.c), self.me).wait_recv()
        for cp in self._first() + self._passed():
            cp.wait_send()
        for cp in self._mine():
            cp.wait()


def _weight_gather(shards):
    n = len(shards)

    def body(*refs):
        g = _Gather(refs[:n], refs[n:2 * n], [False] * n, *refs[2 * n:])
        g.start()
        g.forward()
        g.finish()

    return pl.pallas_call(
        body, name="weight_gather",
        in_specs=[_HBM] * n, out_specs=[_HBM] * n,
        out_shape=_Gather.out_shapes(shards, [False] * n),
        scratch_shapes=_Gather.scratch(n),
    )(*shards)


def _owner_blocks(g, by_cols):
    return (g.shape[0], g.shape[1] // N_DEV) if by_cols else g.shape[2:]


def _grad_exchange_d2d(gs, by_cols, name):
    n = len(gs)

    def body(*refs):
        ins, got = refs[:n], refs[n:2 * n]
        send_sems, recv_sems = refs[2 * n:]
        x, y, c = _coords()

        def src(t, k):
            if not by_cols[t]:
                return ins[t].at[k, 1 - c]
            w = ins[t].shape[1] // N_DEV
            return ins[t].at[:, pl.ds(pl.multiple_of((2 * k + 1 - c) * w, w), w)]

        remote = [pltpu.make_async_remote_copy(
            src_ref=src(t, k), dst_ref=got[t].at[k], send_sem=send_sems.at[t * 4 + k],
            recv_sem=recv_sems.at[t * 4 + k], device_id=(x, y, 1 - c), device_id_type=MESH_IDS)
            for t in range(n) for k in range(4)]
        for cp in remote:
            cp.start()
        for cp in remote:
            cp.wait_recv()
        for cp in remote:
            cp.wait_send()

    return pl.pallas_call(
        body, name=name,
        in_specs=[_HBM] * n, out_specs=[_HBM] * n,
        out_shape=[jax.ShapeDtypeStruct((4,) + _owner_blocks(g, cols), g.dtype) for g, cols in zip(gs, by_cols)],
        scratch_shapes=[pltpu.SemaphoreType.DMA((4 * n,)), pltpu.SemaphoreType.DMA((4 * n,))],
    )(*gs)


def _small_allreduce(pack):
    rows = pack.shape[0]

    def body(p_ref, o_ref, sib_ref, parts_ref, send_sems, recv_sems):
        x, y, c = _coords()
        my_chip = 2 * x + y
        swap = pltpu.make_async_remote_copy(src_ref=p_ref, dst_ref=sib_ref, send_sem=send_sems.at[0],
                                            recv_sem=recv_sems.at[0], device_id=(x, y, 1 - c), device_id_type=MESH_IDS)
        swap.start()
        swap.wait_recv()
        parts_ref[my_chip] = p_ref[...] + sib_ref[...]
        remote = [pltpu.make_async_remote_copy(
            src_ref=parts_ref.at[my_chip], dst_ref=parts_ref.at[my_chip], send_sem=send_sems.at[1 + j],
            recv_sem=recv_sems.at[1 + j], device_id=(cx, cy, c), device_id_type=MESH_IDS)
            for j, (cx, cy) in enumerate(_other_chips(x, y))]
        for cp in remote:
            cp.start()
        for j, (cx, cy) in enumerate(_other_chips(x, y)):
            slot = parts_ref.at[2 * cx + cy]
            pltpu.make_async_remote_copy(src_ref=slot, dst_ref=slot, send_sem=send_sems.at[1 + j],
                                         recv_sem=recv_sems.at[1 + j], device_id=(cx, cy, c),
                                         device_id_type=MESH_IDS).wait_recv()
        o_ref[...] = ((parts_ref[0] + parts_ref[1]) + parts_ref[2]) + parts_ref[3]
        swap.wait_send()
        for cp in remote:
            cp.wait_send()

    vmem = pl.BlockSpec(memory_space=pltpu.VMEM)
    return pl.pallas_call(
        body, name="small_allreduce", in_specs=[vmem], out_specs=vmem,
        out_shape=jax.ShapeDtypeStruct((rows, 128), F32),
        scratch_shapes=[pltpu.VMEM((rows, 128), F32), pltpu.VMEM((4, rows, 128), F32),
                        pltpu.SemaphoreType.DMA((4,)), pltpu.SemaphoreType.DMA((4,))],
    )(pack)


class _ChipExchange:
    def __init__(self, ins, outs, whole, send_sems, recv_sems, local_sems):
        self.ins, self.outs, self.whole, self.n = ins, outs, whole, len(ins)
        self.send_sems, self.recv_sems, self.local_sems = send_sems, recv_sems, local_sems
        self.x, self.y, self.c = _coords()
        self.my_chip = 2 * self.x + self.y
        self.chips = _other_chips(self.x, self.y)

    @staticmethod
    def scratch(n):
        return [pltpu.SemaphoreType.DMA((3 * n,)), pltpu.SemaphoreType.DMA((3 * n,)), pltpu.SemaphoreType.DMA((n,))]

    def _src(self, t, k):
        return self.ins[t] if self.whole[t] else self.ins[t].at[k]

    def _local(self):
        return [pltpu.make_async_copy(self._src(t, self.my_chip), self.outs[t].at[self.my_chip], self.local_sems.at[t])
                for t in range(self.n)]

    def _remote(self):
        return [pltpu.make_async_remote_copy(
            src_ref=self._src(t, 2 * cx + cy), dst_ref=self.outs[t].at[self.my_chip],
            send_sem=self.send_sems.at[t * 3 + j], recv_sem=self.recv_sems.at[t * 3 + j],
            device_id=(cx, cy, self.c), device_id_type=MESH_IDS)
            for t in range(self.n) for j, (cx, cy) in enumerate(self.chips)]

    def start(self):
        for cp in self._remote() + self._local():
            cp.start()

    def finish(self):
        for t in range(self.n):
            for j, (cx, cy) in enumerate(self.chips):
                slot = self.outs[t].at[2 * cx + cy]
                pltpu.make_async_remote_copy(
                    src_ref=slot, dst_ref=slot, send_sem=self.send_sems.at[t * 3 + j],
                    recv_sem=self.recv_sems.at[t * 3 + j], device_id=(cx, cy, self.c),
                    device_id_type=MESH_IDS).wait_recv()
        for cp in self._remote():
            cp.wait_send()
        for cp in self._local():
            cp.wait()

    @staticmethod
    def out_shapes(arrs, whole):
        return [jax.ShapeDtypeStruct(((4,) + a.shape) if w else a.shape, a.dtype) for a, w in zip(arrs, whole)]


def _row_tile(rows, cols, n_arrays):
    budget = 24 * 1024 * 1024
    padded = -(-cols // 128) * 128
    step = 16 if rows % 16 == 0 else 8
    tr = max(step, budget // (n_arrays * 2 * 4 * padded) // step * step)
    while rows % tr:
        tr -= step
    return tr


def _chip_sum(g, by_cols, got, core, name):
    rows, cols = _owner_blocks(g, by_cols)
    tr = _row_tile(rows, cols, 3)

    def body(c_ref, a_ref, b_ref, o_ref):
        o_ref[...] = (a_ref[...] + b_ref[...].astype(F32)).astype(o_ref.dtype)

    own = (pl.BlockSpec((tr, cols), lambda k, i, c: (i, 2 * k + c[0])) if by_cols
           else pl.BlockSpec((None, None, tr, cols), lambda k, i, c: (k, c[0], i, 0)))
    grid_spec = pltpu.PrefetchScalarGridSpec(
        num_scalar_prefetch=1, grid=(4, rows // tr),
        in_specs=[own, pl.BlockSpec((None, tr, cols), lambda k, i, c: (k, i, 0))],
        out_specs=pl.BlockSpec((None, tr, cols), lambda k, i, c: (k, i, 0)))
    return pl.pallas_call(body, grid_spec=grid_spec, name=name,
                          out_shape=jax.ShapeDtypeStruct((4, rows, cols), MXU_DTYPE),
                          compiler_params=_params("arbitrary", "arbitrary"))(core, g, got)


def _adamw_math(w, g, m, v):
    m2 = ADAM_B1 * m + (1.0 - ADAM_B1) * g
    v2 = ADAM_B2 * v + (1.0 - ADAM_B2) * (g * g)
    m_hat = m2 / (1.0 - ADAM_B1 ** ADAM_STEP)
    v_hat = v2 / (1.0 - ADAM_B2 ** ADAM_STEP)
    delta = -ADAM_LR * (m_hat / (jnp.sqrt(v_hat) + ADAM_EPS) + ADAM_WD * w)
    return delta, m2, v2


def _adamw(parts, w, m, v, name):
    rows, cols = w.shape
    tr = _row_tile(rows, cols, 11)

    def body(p_ref, w_ref, m_ref, v_ref, g_ref, d_ref, m2_ref, v2_ref):
        part = lambda k: p_ref[k].astype(F32)
        g = ((part(0) + part(1)) + part(2)) + part(3)
        d, m2, v2 = _adamw_math(w_ref[...], g, m_ref[...], v_ref[...])
        g_ref[...] = g
        d_ref[...] = d
        m2_ref[...] = m2
        v2_ref[...] = v2

    blk = pl.BlockSpec((tr, cols), lambda i: (i, 0))
    out = jax.ShapeDtypeStruct((rows, cols), F32)
    return pl.pallas_call(body, grid=(rows // tr,), name=name,
                          in_specs=[pl.BlockSpec((4, tr, cols), lambda i: (0, i, 0)), blk, blk, blk],
                          out_specs=[blk] * 4, out_shape=[out] * 4,
                          compiler_params=_params("arbitrary"))(parts, w, m, v)


def _adamw_small(gs, ws, ms, vs):
    n = len(ws)

    def body(*refs):
        g_refs, w_refs, m_refs, v_refs = (refs[k * n:(k + 1) * n] for k in range(4))
        d_refs, m2_refs, v2_refs = (refs[(4 + k) * n:(5 + k) * n] for k in range(3))
        for t in range(n):
            d, m2, v2 = _adamw_math(w_refs[t][...], g_refs[t][...], m_refs[t][...], v_refs[t][...])
            d_refs[t][...] = d
            m2_refs[t][...] = m2
            v2_refs[t][...] = v2

    outs = pl.pallas_call(body, name="adamw_small",
                          out_shape=[jax.ShapeDtypeStruct(w.shape, F32) for w in ws] * 3)(*gs, *ws, *ms, *vs)
    return outs[:n], outs[n:2 * n], outs[2 * n:]


_PACK_TILE = 8 * 128


def _pack(arrays):
    rows = []
    for a in arrays:
        flat = a.astype(F32).reshape(-1)
        rows.append(jnp.pad(flat, (0, -flat.shape[0] % _PACK_TILE)).reshape(-1, 128))
    return jnp.concatenate(rows, axis=0)


def _unpack(pack, shapes):
    out, r = [], 0
    for s in shapes:
        n = int(np.prod(s))
        out.append(pack[r:r + -(-n // 128)].reshape(-1)[:n].reshape(s))
        r += -(-n // _PACK_TILE) * 8
    return out


def _pad128(v):
    v = v.reshape(1, -1).astype(F32)
    return jnp.pad(v, ((0, 0), (0, 128 - v.shape[1])))


_WEIGHTS = ["meta", "norm_mix_w", "w_in", "pool_w", "pool_scale", "conv_w", "conv_b", "dt_bias", "a_log", "d_skip",
            "ssm_norm_w", "w_out", "norm_ffn_w", "w_ff1", "w_ff2", "norm_f_w"]
_BIG = ["w_in", "w_out", "w_ff1", "w_ff2"]
_SMALL = [n for n in _WEIGHTS if n not in _BIG]


def kernel(x, meta, norm_mix_w, w_in, pool_w, pool_scale, conv_w, conv_b, dt_bias, a_log, d_skip, ssm_norm_w, w_out, norm_ffn_w, w_ff1, w_ff2, norm_f_w, loss_target, m_meta, m_norm_mix_w, m_w_in, m_pool_w, m_pool_scale, m_conv_w, m_conv_b, m_dt_bias, m_a_log, m_d_skip, m_ssm_norm_w, m_w_out, m_norm_ffn_w, m_w_ff1, m_w_ff2, m_norm_f_w, v_meta, v_norm_mix_w, v_w_in, v_pool_w, v_pool_scale, v_conv_w, v_conv_b, v_dt_bias, v_a_log, v_d_skip, v_ssm_norm_w, v_w_out, v_norm_ffn_w, v_w_ff1, v_w_ff2, v_norm_f_w):
    wts = dict(meta=meta, norm_mix_w=norm_mix_w, w_in=w_in, pool_w=pool_w, pool_scale=pool_scale, conv_w=conv_w,
               conv_b=conv_b, dt_bias=dt_bias, a_log=a_log, d_skip=d_skip, ssm_norm_w=ssm_norm_w, w_out=w_out,
               norm_ffn_w=norm_ffn_w, w_ff1=w_ff1, w_ff2=w_ff2, norm_f_w=norm_f_w)
    mom1 = dict(zip(_WEIGHTS, (m_meta, m_norm_mix_w, m_w_in, m_pool_w, m_pool_scale, m_conv_w, m_conv_b, m_dt_bias,
                               m_a_log, m_d_skip, m_ssm_norm_w, m_w_out, m_norm_ffn_w, m_w_ff1, m_w_ff2, m_norm_f_w)))
    mom2 = dict(zip(_WEIGHTS, (v_meta, v_norm_mix_w, v_w_in, v_pool_w, v_pool_scale, v_conv_w, v_conv_b, v_dt_bias,
                               v_a_log, v_d_skip, v_ssm_norm_w, v_w_out, v_norm_ffn_w, v_w_ff1, v_w_ff2, v_norm_f_w)))
    xi, yi, ci = _coords()
    dev = 4 * xi + 2 * yi + ci
    win_cols = w_in.shape[-1]
    cw_cols = conv_w.shape[-1]

    nb, seq, _ = x.shape
    core = jnp.reshape(ci, (1,)).astype(jnp.int32)
    owners = lambda a: a.reshape((4, 2) + a.shape[1:])

    lead_pack = jnp.zeros((N_META, 512), F32)
    lead_pack = lead_pack.at[:, :128].set(meta).at[:4, 128:128 + cw_cols].set(conv_w[0])
    g_win, g_lead = _weight_gather([_mx(w_in[0]), lead_pack])
    win_full = _assemble_bands(g_win, PROJ_W)
    meta_full = jnp.transpose(g_lead[:, :, :128], (1, 0, 2)).reshape(N_META, D_MODEL)
    cw_full = jnp.transpose(g_lead[:, :4, 128:128 + cw_cols], (1, 0, 2)).reshape(4, D_XBC)

    lead = jnp.concatenate([jnp.zeros((PAD_ROWS, D_MODEL), F32), meta_full] * nb, axis=0)
    x_rows = x.reshape(nb * seq, D_MODEL)
    tgt_rows = loss_target.reshape(nb * seq, D_MODEL)
    dt_bias_p, a_log_p = _pad128(dt_bias), _pad128(a_log)
    d_x = jnp.repeat(d_skip.reshape(1, N_HEADS).astype(F32), HEAD_DIM, axis=1)
    norm_f_row = norm_f_w.reshape(1, D_MODEL)

    hn1, proj = _in_proj(x_rows, lead, norm_mix_w, win_full)
    late_cols = [False, True, False]
    y, ypre, conv_pre, states, (g_wout, wff1_full, g_wff2) = _mixer_fwd(
        proj, cw_full, conv_b, dt_bias_p, a_log_p, d_x, ssm_norm_w, pool_w[0], pool_scale, nb,
        [_mx(w_out[0]), _mx(w_ff1[0]), _mx(w_ff2[0])], late_cols)
    wout_full = g_wout.reshape(D_MIX, D_MODEL)
    wff2_full = g_wff2.reshape(D_FF, D_MODEL)
    loss, gr_nf, gr_nffn, ff, da, hn2, dh1, dh2, dy = _ffn_fwd_bwd(
        x_rows, lead, y, tgt_rows, wout_full, norm_ffn_w, wff1_full, wff2_full, norm_f_row)
    gr_wff2 = _tn_matmul(ff, dh2, "grad_w_ff2", tka=1024, max_slab=2816)
    gr_wff1 = _tn_matmul(hn2, da, "grad_w_ff1", tka=512)
    gr_wout = _tn_matmul(y, dh1, "grad_w_out", tka=1024, max_slab=2816)

    by_owner = lambda k: [owners(gr_wout[k].reshape(N_DEV, D_MIX // N_DEV, D_MODEL)), gr_wff1[k],
                          owners(gr_wff2[k].reshape(N_DEV, D_FF // N_DEV, D_MODEL))]
    late_parts = by_owner(0)
    late_got = _grad_exchange_d2d(by_owner(1), late_cols, "grad_exchange_d2d_late")
    late_sums = [_chip_sum(late_parts[t], late_cols[t], late_got[t], core, "chip_sum_late_%d" % t) for t in range(3)]
    (dproj, gr_cw, gr_cb, gr_dtb, gr_alog, gr_d, gr_nw, gr_pw, gr_ps), late_exchanged = _mixer_bwd(
        proj, dy, ypre, conv_pre, states, cw_full, dt_bias_p, a_log_p, d_x, ssm_norm_w, pool_w[0], pool_scale, nb,
        late_sums)

    win_parts = [owners(_tn_matmul_banded(hn1, dproj, "grad_w_in", win_cols, tka=512))]
    win_got = _grad_exchange_d2d(win_parts, [False], "grad_exchange_d2d_w_in")
    win_sum = _chip_sum(win_parts[0], False, win_got[0], core, "chip_sum_w_in")
    gx_rows, gr_nmix, gr_meta, win_exchanged = _in_proj_bwd(dproj, x_rows, lead, dh1, norm_mix_w, win_full, [win_sum])
    parts = dict(w_in=win_exchanged[0], w_out=late_exchanged[0], w_ff1=late_exchanged[1], w_ff2=late_exchanged[2])

    small_full = dict(meta=gr_meta, norm_mix_w=gr_nmix, pool_w=gr_pw, pool_scale=gr_ps, conv_w=gr_cw, conv_b=gr_cb,
                      dt_bias=gr_dtb[:, :N_HEADS], a_log=gr_alog[:, :N_HEADS], d_skip=gr_d[:, :N_HEADS],
                      ssm_norm_w=gr_nw, norm_ffn_w=gr_nffn, norm_f_w=gr_nf, loss=loss[0:1, 0:1])
    small_names = list(small_full)
    small_sum = _small_allreduce(_pack([small_full[n] for n in small_names]))
    gs = dict(zip(small_names, _unpack(small_sum, [small_full[n].shape for n in small_names])))
    gs["meta"] = lax.dynamic_slice_in_dim(gs["meta"], dev * 128, 128, axis=1)
    gs["conv_w"] = lax.dynamic_slice_in_dim(gs["conv_w"], dev * cw_cols, cw_cols, axis=1)

    res = {}
    for n in _BIG:
        shp = wts[n].shape
        res[n] = [o.reshape(shp) for o in _adamw(parts[n], wts[n][0], mom1[n][0], mom2[n][0], "adamw_" + n)]
    as2d = lambda a: a.reshape(-1, a.shape[-1])
    small_g = [as2d(gs[n].reshape(wts[n].shape)) for n in _SMALL]
    small_out = _adamw_small(small_g, *[[as2d(d[n]) for n in _SMALL] for d in (wts, mom1, mom2)])
    for k, n in enumerate(_SMALL):
        res[n] = [o[k].reshape(wts[n].shape) for o in (small_g,) + tuple(small_out)]

    grad_x = gx_rows.reshape(nb, seq, D_MODEL)
    return (gs["loss"][0, 0], grad_x, *[res[n][0] for n in _WEIGHTS], *[res[n][1] for n in _WEIGHTS],
            *[res[n][2] for n in _WEIGHTS], *[res[n][3] for n in _WEIGHTS])
```

```python
import functools

import numpy as np
import jax
import jax.numpy as jnp
from jax import lax
from jax.experimental import pallas as pl
from jax.experimental.pallas import tpu as pltpu

F32 = jnp.float32
MXU_DTYPE = jnp.bfloat16

D_MODEL = 1024
D_POOL = 512
D_SSM = 1536
D_XBC = 2560
N_HEADS = 24
HEAD_DIM = 64
N_GROUPS = 4
GROUP_CH = D_SSM // N_GROUPS
D_STATE = 128
CHUNK = 128
N_META = 16
LEAD = CHUNK
PAD_ROWS = LEAD - N_META
ROW_TILE = 2 * CHUNK
D_MIX = D_POOL + D_SSM
D_FF = 4096
PROJ_W = 4736
OFF_Z = D_POOL
OFF_X = D_POOL + D_SSM
OFF_DT = OFF_X + D_XBC
D_IN_PROJ = OFF_DT + N_HEADS
POOL_WINDOWS = (2, 4, 8, 16)
HALO = 16
EPS = 1e-5
N_DEV = 8

ADAM_LR, ADAM_B1, ADAM_B2, ADAM_EPS, ADAM_WD, ADAM_STEP = 0.001, 0.9, 0.999, 1e-08, 0.01, 10

VMEM_LIMIT = 60 * 1024 * 1024


def _mx(a):
    return a.astype(MXU_DTYPE)


def _dot(a, b):
    return jnp.dot(_mx(a), _mx(b), preferred_element_type=F32)


def _dot_nt(a, b):
    return lax.dot_general(_mx(a), _mx(b), (((1,), (1,)), ((), ())), preferred_element_type=F32)


def _dot_tn(a, b):
    return lax.dot_general(_mx(a), _mx(b), (((0,), (0,)), ((), ())), preferred_element_type=F32)


def _split3(x):
    hi = x.astype(MXU_DTYPE)
    r = x - hi.astype(F32)
    mid = r.astype(MXU_DTYPE)
    lo = (r - mid.astype(F32)).astype(MXU_DTYPE)
    return hi, mid, lo


def _exact_l(c, x):
    hi, mid, lo = _split3(x)
    f = lambda p: jnp.dot(c, p, preferred_element_type=F32)
    return f(hi) + f(mid) + f(lo)


def _exact_r(x, c):
    hi, mid, lo = x if isinstance(x, tuple) else _split3(x)
    f = lambda p: jnp.dot(p, c, preferred_element_type=F32)
    return f(hi) + f(mid) + f(lo)


def _contract(x, c):
    hi = x.astype(MXU_DTYPE)
    lo = (x - hi.astype(F32)).astype(MXU_DTYPE)
    return jnp.dot(hi, c, preferred_element_type=F32) + jnp.dot(lo, c, preferred_element_type=F32)


def _sigmoid(x):
    return jax.nn.sigmoid(x)


def _softplus(x):
    return jnp.maximum(x, 0.0) + jnp.log1p(jnp.exp(-jnp.abs(x)))


def _silu_and_grad(x):
    s = _sigmoid(x)
    y = x * s
    return y, s + y * (1.0 - s)


def _shift_up(ext, s):
    if s == 0:
        return ext[:CHUNK, :]
    return pltpu.roll(ext, ext.shape[0] - s, 0)[:CHUNK, :]


def _by_pool_group(lane, a2, a4, a8, a16):
    return jnp.where(lane < 128, a2, jnp.where(lane < 256, a4, jnp.where(lane < 384, a8, a16)))


def _pool_inv_count(chunk_idx):
    row = lax.broadcasted_iota(jnp.int32, (CHUNK, D_POOL), 0)
    lane = lax.broadcasted_iota(jnp.int32, (CHUNK, D_POOL), 1)
    pos1 = jnp.maximum(chunk_idx * CHUNK + row - (PAD_ROWS - 1), 1)
    w = _by_pool_group(lane, 2, 4, 8, 16)
    return 1.0 / jnp.minimum(pos1, w).astype(F32), lane


def _pool_window_sums(u_ext, lane):
    s2 = u_ext + pltpu.roll(u_ext, 1, 0)
    s4 = s2 + pltpu.roll(s2, 2, 0)
    s8 = s4 + pltpu.roll(s4, 4, 0)
    s16 = s8 + pltpu.roll(s8, 8, 0)
    return _by_pool_group(lane, s2[HALO:], s4[HALO:], s8[HALO:], s16[HALO:])


def _pool_window_sums_ahead(q_ext, lane):
    n = q_ext.shape[0]
    r2 = q_ext + pltpu.roll(q_ext, n - 1, 0)
    r4 = r2 + pltpu.roll(r2, n - 2, 0)
    r8 = r4 + pltpu.roll(r4, n - 4, 0)
    r16 = r8 + pltpu.roll(r8, n - 8, 0)
    return _by_pool_group(lane, r2[:CHUNK], r4[:CHUNK], r8[:CHUNK], r16[:CHUNK])


def _conv_pre(ext, xbc, cw, cb):
    s1 = pltpu.roll(ext, 1, 0)
    near = cw[3:4, :] * xbc + cw[2:3, :] * s1[HALO:, :]
    far = cw[1:2, :] * ext + cw[0:1, :] * s1
    return cb + near + pltpu.roll(far, 2, 0)[HALO:, :]


def _dt_and_cumsum(dtr, dt_bias, a_log, valid, tril):
    lane = lax.broadcasted_iota(jnp.int32, (CHUNK, 128), 1)
    head = lane < N_HEADS
    pre = dtr + dt_bias
    dt = jnp.where(valid & head, _softplus(pre), 0.0)
    a_row = jnp.where(head[0:1, :], -jnp.exp(a_log), 0.0)
    a_col = _exact_l(tril, dt * a_row)
    return dt, a_row, a_col, pre, head


def _decay(a_col, a_row_t, h, causal):
    seg = a_col[:, h:h + 1] - a_row_t[h:h + 1, :]
    return jnp.where(causal, jnp.exp(jnp.minimum(seg, 0.0)), 0.0)


def _ssd_chunk_fwd(xs, bm, cm, dt, a_col, s_prev, d_x, e_mat, et_f32):
    lane = lax.broadcasted_iota(jnp.int32, (CHUNK, 128), 1)
    rowi = lax.broadcasted_iota(jnp.int32, (CHUNK, CHUNK), 0)
    coli = lax.broadcasted_iota(jnp.int32, (CHUNK, CHUNK), 1)
    causal = rowi >= coli
    a_row_t = a_col.T
    ax = _exact_r(a_col, e_mat)
    dtx = _exact_r(dt, e_mat)
    xdt = xs * dtx
    ax_last = ax[CHUNK - 1:CHUNK, :]
    e_a = jnp.exp(ax)
    w_end = xdt * jnp.exp(ax_last - ax)
    cd_col = jnp.exp(jnp.sum(et_f32 * a_col[CHUNK - 1:CHUNK, :], axis=1, keepdims=True))
    ys, s_new = [], []
    for g in range(N_GROUPS):
        gs = slice(g * GROUP_CH, (g + 1) * GROUP_CH)
        bg = bm[:, g * D_STATE:(g + 1) * D_STATE]
        cg = cm[:, g * D_STATE:(g + 1) * D_STATE]
        sg = s_prev[gs, :]
        cb = _dot_nt(cg, bg)
        y_off = _dot_nt(cg, sg) * e_a[:, gs]
        s_new.append(sg * cd_col[gs, :] + _dot_tn(w_end[:, gs], bg))
        for pr in range(3):
            c0 = g * GROUP_CH + pr * 128
            xdt_p = xdt[:, c0:c0 + 128]
            h0 = g * 6 + pr * 2
            y0 = _dot(cb * _decay(a_col, a_row_t, h0, causal), xdt_p)
            y1 = _dot(cb * _decay(a_col, a_row_t, h0 + 1, causal), xdt_p)
            ys.append(jnp.where(lane < HEAD_DIM, y0, y1) + y_off[:, pr * 128:(pr + 1) * 128])
    y = jnp.concatenate(ys, axis=1) + d_x * xs
    return y, jnp.concatenate(s_new, axis=0)


def _ssd_chunk_bwd(xs, bm, cm, dt, a_row, a_col, s_prev, ds_new, dy, d_x, e_mat, et_mat, et_f32, triu):
    lane = lax.broadcasted_iota(jnp.int32, (CHUNK, 128), 1)
    sub = lax.broadcasted_iota(jnp.int32, (CHUNK, 128), 0)
    rowi = lax.broadcasted_iota(jnp.int32, (CHUNK, CHUNK), 0)
    coli = lax.broadcasted_iota(jnp.int32, (CHUNK, CHUNK), 1)
    causal = rowi >= coli
    a_row_t = a_col.T
    a_last = a_col[CHUNK - 1:CHUNK, :]
    a_split, dt_split = _split3(a_col), _split3(dt)
    sub8 = lax.broadcasted_iota(jnp.int32, (8, GROUP_CH), 0)

    dxs, dbs, dcs, dsp = [], [], [], []
    zcol = jnp.zeros((CHUNK, 128), F32)
    zrows = []
    da_col = jnp.zeros((CHUNK, 128), F32)
    ddt = jnp.zeros((CHUNK, 128), F32)
    head_sums = jnp.zeros((8, 128), F32)
    q_row = jnp.zeros((1, 128), F32)
    for g in range(N_GROUPS):
        gs = slice(g * GROUP_CH, (g + 1) * GROUP_CH)
        e_g, et_g = e_mat[:, gs], et_mat[gs, :]
        xs_g, dy_g = xs[:, gs], dy[:, gs]
        ax = _exact_r(a_split, e_g)
        dtx = _exact_r(dt_split, e_g)
        xdt = xs_g * dtx
        dte = jnp.exp(ax[CHUNK - 1:CHUNK, :] - ax)
        w_end = xdt * dte
        cd_col = jnp.exp(jnp.sum(et_f32[gs, :] * a_last, axis=1, keepdims=True))
        dye = dy_g * jnp.exp(ax)
        bg = bm[:, g * D_STATE:(g + 1) * D_STATE]
        cg = cm[:, g * D_STATE:(g + 1) * D_STATE]
        sg = s_prev[gs, :]
        dsg = ds_new[gs, :]
        cb = _dot_nt(cg, bg)
        cs = _dot_nt(cg, sg)
        dcg = _dot(dye, sg)
        dsp.append(dsg * cd_col + _dot_tn(dye, cg))
        dwg = _dot_nt(bg, dsg)
        dbg = _dot(w_end, dsg)
        ww = dwg * w_end
        t1 = jnp.sum(dsg * sg, axis=1, keepdims=True) * cd_col
        dcb = jnp.zeros((CHUNK, CHUNK), F32)
        pairs = []
        for pr in range(3):
            ps = slice(pr * 128, (pr + 1) * 128)
            xdt_p, dy_p = xdt[:, ps], dy_g[:, ps]
            acc = None
            for half in range(2):
                h = g * 6 + pr * 2 + half
                ld = _decay(a_col, a_row_t, h, causal)
                gm = cb * ld
                dym = jnp.where((lane < HEAD_DIM) if half == 0 else (lane >= HEAD_DIM), dy_p, 0.0)
                dg = _dot_nt(dym, xdt_p)
                dseg = dg * gm
                dcb = dcb + dg * ld
                t = _dot_tn(gm, dym)
                acc = t if acc is None else acc + t
                zcol = jnp.where(lane == h, jnp.sum(dseg, axis=1, keepdims=True), zcol)
                zrows.append(jnp.sum(dseg, axis=0, keepdims=True))
            pairs.append(acc)
        dxdt = dwg * dte + jnp.concatenate(pairs, axis=1)
        dcs.append(dcg + _dot(dcb, bg))
        dbs.append(dbg + _dot_tn(dcb, cg))
        dxs.append(dxdt * dtx + d_x[:, gs] * dy_g)
        da_col = da_col + _contract(dye * cs - ww, et_g)
        ddt = ddt + _contract(dxdt * xs_g, et_g)
        col_sums = jnp.where(sub8 == 0, jnp.sum(dy_g * xs_g, axis=0, keepdims=True),
                             jnp.where(sub8 == 1, jnp.sum(ww, axis=0, keepdims=True), 0.0))
        head_sums = head_sums + _exact_r(col_sums, et_g)
        q_row = q_row + jnp.sum(et_f32[gs, :] * t1, axis=0, keepdims=True)

    dd = head_sums[0:1, :]
    q_row = q_row + head_sums[1:2, :]
    zrow = jnp.concatenate(zrows + [jnp.zeros((128 - N_HEADS, CHUNK), F32)], axis=0)
    da_col = da_col + zcol - zrow.T + jnp.where(sub == CHUNK - 1, q_row, 0.0)
    rc = _exact_l(triu, da_col)
    ddt = ddt + rc * a_row
    da = jnp.sum(rc * dt, axis=0, keepdims=True)
    return (jnp.concatenate(dxs, axis=1), jnp.concatenate(dbs, axis=1), jnp.concatenate(dcs, axis=1), ddt, da, dd,
            jnp.concatenate(dsp, axis=0))


def _ssd_constants():
    iota = lambda shape, d: lax.broadcasted_iota(jnp.int32, shape, d)
    e = iota((128, D_SSM), 0) == iota((128, D_SSM), 1) // HEAD_DIM
    et = iota((D_SSM, 128), 1) == iota((D_SSM, 128), 0) // HEAD_DIM
    tril = iota((CHUNK, CHUNK), 1) <= iota((CHUNK, CHUNK), 0)
    triu = iota((CHUNK, CHUNK), 1) >= iota((CHUNK, CHUNK), 0)
    return (e.astype(MXU_DTYPE), et.astype(MXU_DTYPE), et.astype(F32), tril.astype(MXU_DTYPE), triu.astype(MXU_DTYPE))


def _full(shape):
    nd = len(shape)
    return pl.BlockSpec(shape, lambda *_: (0,) * nd)


def _params(*sem):
    return pltpu.CompilerParams(dimension_semantics=sem, vmem_limit_bytes=VMEM_LIMIT)


def _token_tiles(width, n_tok_tiles):
    return pl.BlockSpec((ROW_TILE, width), lambda i: (jnp.minimum(i, n_tok_tiles - 1), 0))


def _in_proj(x, lead, w1, win):
    nt = x.shape[0] // ROW_TILE
    m = x.shape[0] + ROW_TILE
    tm = ROW_TILE

    def body(x_ref, lead_ref, w1_ref, win_hbm, hn_ref, proj_ref, win_v, sem):
        i = pl.program_id(0)

        @pl.when(i == 0)
        def _():
            cp = pltpu.make_async_copy(win_hbm, win_v, sem)
            cp.start()
            cp.wait()

        x = jnp.where(i == nt, lead_ref[...], x_ref[...])
        r = lax.rsqrt(jnp.mean(x * x, axis=-1, keepdims=True) + EPS)
        hn = _mx(x * r * w1_ref[...])
        hn_ref[...] = hn
        for j in range(0, PROJ_W, 512):
            w = min(512, PROJ_W - j)
            proj_ref[:, j:j + w] = jnp.dot(hn, win_v[:, j:j + w], preferred_element_type=F32)

    return pl.pallas_call(
        body, grid=(m // tm,), name="in_proj",
        in_specs=[_token_tiles(D_MODEL, nt), _full((ROW_TILE, D_MODEL)), _full((1, D_MODEL)),
                  pl.BlockSpec(memory_space=pl.ANY)],
        out_specs=[pl.BlockSpec((tm, D_MODEL), lambda i: (i, 0)), pl.BlockSpec((tm, PROJ_W), lambda i: (i, 0))],
        out_shape=[jax.ShapeDtypeStruct((m, D_MODEL), MXU_DTYPE), jax.ShapeDtypeStruct((m, PROJ_W), F32)],
        scratch_shapes=[pltpu.VMEM((D_MODEL, PROJ_W), MXU_DTYPE), pltpu.SemaphoreType.DMA],
        compiler_params=_params("arbitrary"),
    )(x, lead, w1, win)


def _ffn_fwd_bwd(x, lead, y, tgt, wout, w2n, wff1, wff2, wfn):
    nt = x.shape[0] // ROW_TILE
    m = x.shape[0] + ROW_TILE
    tm = ROW_TILE
    nj = D_FF // 1024

    def body(x_ref, lead_ref, y_ref, tgt_ref, w2n_ref, wfn_ref, wout_hbm, wff1_hbm, wff2_hbm,
             loss_ref, gwf_ref, gw2_ref, ff_ref, da_ref, hn2_ref, dh1_ref, dh2_ref, dy_ref,
             wout_v, wff1_v, wff2_v, a_s, sems):
        i = pl.program_id(0)
        hp = jnp.where(i == nt, lead_ref[...], x_ref[...])

        @pl.when(i == 0)
        def _():
            cps = [pltpu.make_async_copy(s, d, sems.at[k])
                   for k, (s, d) in enumerate(((wout_hbm, wout_v), (wff1_hbm, wff1_v), (wff2_hbm, wff2_v)))]
            for cp in cps:
                cp.start()
            for cp in cps:
                cp.wait()
            loss_ref[...] = jnp.zeros_like(loss_ref)
            gwf_ref[...] = jnp.zeros_like(gwf_ref)
            gw2_ref[...] = jnp.zeros_like(gw2_ref)

        h1 = hp + jnp.dot(y_ref[...], wout_v[...], preferred_element_type=F32)
        r2 = lax.rsqrt(jnp.mean(h1 * h1, axis=-1, keepdims=True) + EPS)
        n2 = h1 * r2
        w2n_row = w2n_ref[...]
        hn2 = _mx(n2 * w2n_row)
        hn2_ref[...] = hn2
        h2 = h1
        for j in range(nj):
            js = slice(j * 1024, (j + 1) * 1024)
            a = jnp.dot(hn2, wff1_v[:, js], preferred_element_type=F32)
            a_s[:, js] = a
            ra = jnp.maximum(a, 0.0)
            ff = _mx(ra * ra)
            ff_ref[:, js] = ff
            h2 = h2 + jnp.dot(ff, wff2_v[js, :], preferred_element_type=F32)

        r3 = lax.rsqrt(jnp.mean(h2 * h2, axis=-1, keepdims=True) + EPS)
        n3 = h2 * r3
        wf_row = wfn_ref[...]
        err = n3 * wf_row - tgt_ref[...]
        tokf = (i < nt).astype(F32)
        loss_ref[...] += 0.5 * jnp.sum(jnp.mean(err * err, axis=-1, keepdims=True) * tokf)
        dout = err * (tokf / D_MODEL)
        gwf_ref[...] += jnp.sum(dout * n3, axis=0, keepdims=True)
        dn3 = dout * wf_row
        dh2 = r3 * (dn3 - n3 * jnp.mean(dn3 * n3, axis=-1, keepdims=True))
        dh2m = _mx(dh2)
        dh2_ref[...] = dh2m

        dhn2 = jnp.zeros((tm, D_MODEL), F32)
        for j in range(nj):
            js = slice(j * 1024, (j + 1) * 1024)
            dff = lax.dot_general(dh2m, wff2_v[js, :], (((1,), (1,)), ((), ())), preferred_element_type=F32)
            da = _mx(dff * (2.0 * jnp.maximum(a_s[:, js], 0.0)))
            da_ref[:, js] = da
            dhn2 = dhn2 + lax.dot_general(da, wff1_v[:, js], (((1,), (1,)), ((), ())), preferred_element_type=F32)
        gw2_ref[...] += jnp.sum(dhn2 * n2, axis=0, keepdims=True)
        dn2 = dhn2 * w2n_row
        dh1 = dh2 + r2 * (dn2 - n2 * jnp.mean(dn2 * n2, axis=-1, keepdims=True))
        dh1_ref[...] = dh1
        dy_ref[...] = lax.dot_general(_mx(dh1), wout_v[...], (((1,), (1,)), ((), ())), preferred_element_type=F32)

    rows = lambda w: pl.BlockSpec((tm, w), lambda i: (i, 0))
    hbm = pl.BlockSpec(memory_space=pl.ANY)
    return pl.pallas_call(
        body, grid=(m // tm,), name="ffn_fwd_bwd",
        in_specs=[_token_tiles(D_MODEL, nt), _full((ROW_TILE, D_MODEL)), rows(D_MIX), _token_tiles(D_MODEL, nt),
                  _full((1, D_MODEL)), _full((1, D_MODEL)), hbm, hbm, hbm],
        out_specs=[_full((1, 128)), _full((1, D_MODEL)), _full((1, D_MODEL)), rows(D_FF), rows(D_FF), rows(D_MODEL),
                   rows(D_MODEL), rows(D_MODEL), rows(D_MIX)],
        out_shape=[jax.ShapeDtypeStruct((1, 128), F32), jax.ShapeDtypeStruct((1, D_MODEL), F32),
                   jax.ShapeDtypeStruct((1, D_MODEL), F32), jax.ShapeDtypeStruct((m, D_FF), MXU_DTYPE),
                   jax.ShapeDtypeStruct((m, D_FF), MXU_DTYPE), jax.ShapeDtypeStruct((m, D_MODEL), MXU_DTYPE),
                   jax.ShapeDtypeStruct((m, D_MODEL), F32), jax.ShapeDtypeStruct((m, D_MODEL), MXU_DTYPE),
                   jax.ShapeDtypeStruct((m, D_MIX), F32)],
        scratch_shapes=[pltpu.VMEM((D_MIX, D_MODEL), MXU_DTYPE), pltpu.VMEM((D_MODEL, D_FF), MXU_DTYPE),
                        pltpu.VMEM((D_FF, D_MODEL), MXU_DTYPE), pltpu.VMEM((tm, D_FF), F32),
                        pltpu.SemaphoreType.DMA((3,))],
        compiler_params=_params("arbitrary"),
    )(x, lead, y, tgt, w2n, wfn, wout, wff1, wff2)


def _in_proj_bwd(dproj, x, lead, dh1, w1, win, after):
    nt = x.shape[0] // ROW_TILE
    m = x.shape[0] + ROW_TILE
    tm = ROW_TILE

    def body(dp_ref, x_ref, lead_ref, dh1_ref, w1_ref, win_hbm, after_ref, gx_ref, gw1_ref, gmeta_ref, win_v, sem):
        i = pl.program_id(0)

        @pl.when(i == 0)
        def _():
            cp = pltpu.make_async_copy(win_hbm, win_v, sem)
            cp.start()
            cp.wait()
            gw1_ref[...] = jnp.zeros_like(gw1_ref)
            gmeta_ref[...] = jnp.zeros_like(gmeta_ref)

        dhn = lax.dot_general(dp_ref[...], win_v[...], (((1,), (1,)), ((), ())), preferred_element_type=F32)
        x = jnp.where(i == nt, lead_ref[...], x_ref[...])
        r = lax.rsqrt(jnp.mean(x * x, axis=-1, keepdims=True) + EPS)
        n = x * r
        gw1_ref[...] += jnp.sum(dhn * n, axis=0, keepdims=True)
        dn = dhn * w1_ref[...]
        dh0 = dh1_ref[...] + r * (dn - n * jnp.mean(dn * n, axis=-1, keepdims=True))

        @pl.when(i < nt)
        def _():
            gx_ref[...] = dh0

        @pl.when(i == nt)
        def _():
            gmeta_ref[...] = dh0[PAD_ROWS:LEAD, :] + dh0[LEAD + PAD_ROWS:2 * LEAD, :]

    rows = lambda w: pl.BlockSpec((tm, w), lambda i: (i, 0))
    hbm = pl.BlockSpec(memory_space=pl.ANY)
    return pl.pallas_call(
        body, grid=(m // tm,), name="in_proj_bwd",
        in_specs=[rows(PROJ_W), _token_tiles(D_MODEL, nt), _full((ROW_TILE, D_MODEL)), rows(D_MODEL),
                  _full((1, D_MODEL)), hbm, hbm],
        out_specs=[_token_tiles(D_MODEL, nt), _full((1, D_MODEL)), _full((N_META, D_MODEL))],
        out_shape=[jax.ShapeDtypeStruct(x.shape, F32), jax.ShapeDtypeStruct((1, D_MODEL), F32),
                   jax.ShapeDtypeStruct((N_META, D_MODEL), F32)],
        scratch_shapes=[pltpu.VMEM((D_MODEL, PROJ_W), MXU_DTYPE), pltpu.SemaphoreType.DMA],
        compiler_params=_params("arbitrary"),
    )(dproj, x, lead, dh1, w1, win, after)


MXU_DEPTH = 256


def _row_slab(m, cap):
    return max(k for k in range(MXU_DEPTH, cap + 1, MXU_DEPTH) if m % k == 0)


def _tn_matmul(a, b, name, tka, max_slab=768, tn=512):
    m, ka = a.shape
    nb = b.shape[1]
    tkm = _row_slab(m, max_slab)
    n_steps = m // tkm

    def body(a_ref, b_ref, o_ref, omx_ref):
        k = pl.program_id(1)

        @pl.when(k == 0)
        def _():
            o_ref[...] = jnp.zeros_like(o_ref)

        at = _mx(a_ref[...])
        for j in range(0, nb, tn):
            w = min(tn, nb - j)
            o_ref[:, j:j + w] += lax.dot_general(at, _mx(b_ref[:, j:j + w]), (((0,), (0,)), ((), ())),
                                                 preferred_element_type=F32)

        @pl.when(k == n_steps - 1)
        def _():
            omx_ref[...] = _mx(o_ref[...])

    out = pl.BlockSpec((tka, nb), lambda i, k: (i, 0))
    return pl.pallas_call(
        body, grid=(ka // tka, n_steps), name=name,
        in_specs=[pl.BlockSpec((tkm, tka), lambda i, k: (k, i)), pl.BlockSpec((tkm, nb), lambda i, k: (k, 0))],
        out_specs=[out, out],
        out_shape=[jax.ShapeDtypeStruct((ka, nb), F32), jax.ShapeDtypeStruct((ka, nb), MXU_DTYPE)],
        compiler_params=_params("arbitrary", "arbitrary"),
    )(a, b)


def _tn_matmul_banded(a, b, name, band, tka, tn=512):
    m, ka = a.shape
    nb = b.shape[1]
    tkm = _row_slab(m, 768)
    n_steps = m // tkm

    def body(a_ref, b_ref, o_ref, acc):
        k = pl.program_id(1)

        @pl.when(k == 0)
        def _():
            acc[...] = jnp.zeros_like(acc)

        at = _mx(a_ref[...])
        for j in range(0, nb, tn):
            w = min(tn, nb - j)
            acc[:, j:j + w] += lax.dot_general(at, _mx(b_ref[:, j:j + w]), (((0,), (0,)), ((), ())),
                                               preferred_element_type=F32)

        @pl.when(k == n_steps - 1)
        def _():
            for j in range(N_DEV):
                o_ref[j] = acc[:, j * band:(j + 1) * band]

    return pl.pallas_call(
        body, grid=(ka // tka, n_steps), name=name,
        in_specs=[pl.BlockSpec((tkm, tka), lambda i, k: (k, i)), pl.BlockSpec((tkm, nb), lambda i, k: (k, 0))],
        out_specs=pl.BlockSpec((N_DEV, tka, band), lambda i, k: (0, i, 0)),
        out_shape=jax.ShapeDtypeStruct((N_DEV, ka, band), F32),
        scratch_shapes=[pltpu.VMEM((tka, nb), F32)],
        compiler_params=_params("arbitrary", "arbitrary"),
    )(a, b)


def _assemble_bands(g, width):
    n, rows, band = g.shape
    tr = 256

    def body(g_ref, o_ref):
        parts = [g_ref[j] for j in range(n)] + [jnp.zeros((tr, width - n * band), g.dtype)]
        o_ref[...] = jnp.concatenate(parts, axis=1)

    return pl.pallas_call(
        body, grid=(rows // tr,), name="assemble_w_in",
        in_specs=[pl.BlockSpec((n, tr, band), lambda i: (0, i, 0))],
        out_specs=pl.BlockSpec((tr, width), lambda i: (i, 0)),
        out_shape=jax.ShapeDtypeStruct((rows, width), g.dtype),
        compiler_params=_params("arbitrary"),
    )(g)


def _chunk_block(b, c, nb, nc):
    return jnp.where(c == 0, nb * (nc - 1) + b, b * (nc - 1) + c - 1)

def _mixer_fwd(proj, cw, cb, dt_bias, a_log, d_x, nw, pool_w, pool_scale, nb, shards, by_cols):
    m = proj.shape[0]
    nc = m // nb // CHUNK
    n_steps = nb * nc
    ns = len(shards)
    e_mat, _, et_f32, tril, _ = _ssd_constants()

    def body(p_ref, cw_ref, cb_ref, dtb_ref, alog_ref, dx_ref, nw_ref, pw_ref, ps_ref, e_ref, et_ref, tril_ref, *rest):
        shard_refs, (y_ref, ypre_ref, pre_ref, st_ref) = rest[:ns], rest[ns:ns + 4]
        gathered_refs, (xtail, utail, state) = rest[ns + 4:2 * ns + 4], rest[2 * ns + 4:2 * ns + 7]
        gather = _Gather(shard_refs, gathered_refs, by_cols, *rest[2 * ns + 7:])
        c = pl.program_id(1)
        step = pl.program_id(0) * nc + c

        @pl.when(step == 0)
        def _():
            gather.start()

        @pl.when(step == n_steps // 2)
        def _():
            gather.forward()

        @pl.when(c == 0)
        def _():
            xtail[...] = jnp.zeros_like(xtail)
            utail[...] = jnp.zeros_like(utail)
            state[...] = jnp.zeros_like(state)

        valid = (c > 0) | (lax.broadcasted_iota(jnp.int32, (CHUNK, 1), 0) >= PAD_ROWS)

        u = p_ref[:, 0:D_POOL]
        inv_cnt, lane = _pool_inv_count(c)
        win = _pool_window_sums(jnp.concatenate([utail[...], u], axis=0), lane)
        utail[...] = u[CHUNK - HALO:, :]
        pooled = win * inv_cnt - u
        mixed = jnp.concatenate(
            [_dot(pooled[:, g * 128:(g + 1) * 128], pw_ref[g]) for g in range(len(POOL_WINDOWS))], axis=1)
        y_ref[:, 0:D_POOL] = _mx(mixed * ps_ref[...])

        xbc = p_ref[:, OFF_X:OFF_X + D_XBC]
        pre = _conv_pre(jnp.concatenate([xtail[...], xbc], axis=0), xbc, cw_ref[...], cb_ref[...])
        xtail[...] = xbc[CHUNK - HALO:, :]
        pre_ref[...] = pre
        xc = pre * _sigmoid(pre)
        dt, _, a_col, _, _ = _dt_and_cumsum(p_ref[:, OFF_DT:OFF_DT + 128], dtb_ref[...], alog_ref[...], valid,
                                            tril_ref[...])
        s_prev = state[...]
        st_ref[0] = s_prev
        yp, s_new = _ssd_chunk_fwd(xc[:, 0:D_SSM], xc[:, D_SSM:D_SSM + 512], xc[:, D_SSM + 512:], dt, a_col, s_prev,
                                   dx_ref[...], e_ref[...], et_ref[...])
        state[...] = s_new
        ypre_ref[...] = yp
        z = p_ref[:, OFF_Z:OFF_Z + D_SSM]
        yz = yp * (z * _sigmoid(z))
        outs = []
        for g in range(N_GROUPS):
            gs = slice(g * GROUP_CH, (g + 1) * GROUP_CH)
            r = lax.rsqrt(jnp.mean(yz[:, gs] * yz[:, gs], axis=-1, keepdims=True) + EPS)
            outs.append(yz[:, gs] * r)
        y_ref[:, D_POOL:] = _mx(jnp.concatenate(outs, axis=1) * nw_ref[...])

        @pl.when(step == n_steps - 1)
        def _():
            gather.finish()

    blk = lambda w: pl.BlockSpec((CHUNK, w), lambda b, c: (_chunk_block(b, c, nb, nc), 0))
    hbm = pl.BlockSpec(memory_space=pl.ANY)
    outs = pl.pallas_call(
        body, grid=(nb, nc), name="mixer_fwd",
        in_specs=[blk(PROJ_W), _full((4, D_XBC)), _full((1, D_XBC)), _full((1, 128)), _full((1, 128)),
                  _full((1, D_SSM)), _full((1, D_SSM)), _full((4, 128, 128)), _full((1, D_POOL)),
                  _full((128, D_SSM)), _full((D_SSM, 128)), _full((CHUNK, CHUNK))] + [hbm] * ns,
        out_specs=[blk(D_MIX), blk(D_SSM), blk(D_XBC),
                   pl.BlockSpec((1, D_SSM, D_STATE), lambda b, c: (b * nc + c, 0, 0))] + [hbm] * ns,
        out_shape=[jax.ShapeDtypeStruct((m, D_MIX), MXU_DTYPE), jax.ShapeDtypeStruct((m, D_SSM), F32),
                   jax.ShapeDtypeStruct((m, D_XBC), F32), jax.ShapeDtypeStruct((m // CHUNK, D_SSM, D_STATE), F32)]
        + _Gather.out_shapes(shards, by_cols),
        scratch_shapes=[pltpu.VMEM((HALO, D_XBC), F32), pltpu.VMEM((HALO, D_POOL), F32),
                        pltpu.VMEM((D_SSM, D_STATE), F32)] + _Gather.scratch(ns),
        compiler_params=_params("arbitrary", "arbitrary"),
    )(proj, cw, cb, dt_bias, a_log, d_x, nw, pool_w, pool_scale, e_mat, et_f32, tril, *shards)
    return outs[0], outs[1], outs[2], outs[3], outs[4:]


def _mixer_bwd(proj, dy, ypre, conv_pre, states, cw, dt_bias, a_log, d_x, nw, pool_w, pool_scale, nb, chip_sums):
    m = proj.shape[0]
    nc = m // nb // CHUNK
    e_mat, et_mat, et_f32, tril, triu = _ssd_constants()
    hb = CHUNK // HALO
    ns = len(chip_sums)

    def body(p_ref, halo_ref, dy_ref, ypre_ref, pre_ref, st_ref, cw_ref, dtb_ref, alog_ref, dx_ref, nw_ref, pw_ref,
             ps_ref, e_ref, et_ref, etf_ref, tril_ref, triu_ref, *rest):
        cs_refs = rest[:ns]
        dp_ref, gcw_ref, gcb_ref, gdtb_ref, galog_ref, gd_ref, gnw_ref, gpw_ref, gps_ref = rest[ns:ns + 9]
        part_refs, (ds_carry, dpre_next, dq_next) = rest[ns + 9:2 * ns + 9], rest[2 * ns + 9:2 * ns + 12]
        exchange = _ChipExchange(cs_refs, part_refs, [False] * ns, *rest[2 * ns + 12:])
        b = pl.program_id(0)
        cc = pl.program_id(1)
        c = nc - 1 - cc

        @pl.when((b == 0) & (cc == 0))
        def _():
            exchange.start()
            for r in (gcw_ref, gcb_ref, gdtb_ref, galog_ref, gd_ref, gnw_ref, gpw_ref, gps_ref):
                r[...] = jnp.zeros_like(r)

        @pl.when(cc == 0)
        def _():
            ds_carry[...] = jnp.zeros_like(ds_carry)
            dpre_next[...] = jnp.zeros_like(dpre_next)
            dq_next[...] = jnp.zeros_like(dq_next)

        valid = (c > 0) | (lax.broadcasted_iota(jnp.int32, (CHUNK, 1), 0) >= PAD_ROWS)
        first = c > 0

        u = p_ref[:, 0:D_POOL]
        u_halo = jnp.where(first, halo_ref[...], 0.0)
        inv_cnt, lane = _pool_inv_count(c)
        pooled = _pool_window_sums(jnp.concatenate([u_halo, u], axis=0), lane) * inv_cnt - u
        dyp = dy_ref[:, 0:D_POOL]
        ps = ps_ref[...]
        dmixed = dyp * ps
        mixed, dpooled = [], []
        for g in range(len(POOL_WINDOWS)):
            gsl = slice(g * 128, (g + 1) * 128)
            pw = pw_ref[g]
            mixed.append(_dot(pooled[:, gsl], pw))
            dpooled.append(_dot_nt(dmixed[:, gsl], pw))
            gpw_ref[g] += _dot_tn(pooled[:, gsl], dmixed[:, gsl])
        gps_ref[...] += jnp.sum(dyp * jnp.concatenate(mixed, axis=1), axis=0, keepdims=True)
        dpooled = jnp.concatenate(dpooled, axis=1)
        dq = dpooled * inv_cnt
        du = _pool_window_sums_ahead(jnp.concatenate([dq, dq_next[...]], axis=0), lane) - dpooled
        dq_next[...] = dq[0:HALO, :]
        dp_ref[:, 0:D_POOL] = _mx(du)

        yp = ypre_ref[...]
        z = p_ref[:, OFF_Z:OFF_Z + D_SSM]
        sz, dsz = _silu_and_grad(z)
        yz = yp * sz
        do = dy_ref[:, D_POOL:]
        nw_row = nw_ref[...]
        dyz = []
        gnw = []
        for g in range(N_GROUPS):
            gs = slice(g * GROUP_CH, (g + 1) * GROUP_CH)
            r = lax.rsqrt(jnp.mean(yz[:, gs] * yz[:, gs], axis=-1, keepdims=True) + EPS)
            n = yz[:, gs] * r
            gnw.append(jnp.sum(do[:, gs] * n, axis=0, keepdims=True))
            dn = do[:, gs] * nw_row[:, gs]
            dyz.append(r * (dn - n * jnp.mean(dn * n, axis=-1, keepdims=True)))
        gnw_ref[...] += jnp.concatenate(gnw, axis=1)
        dyz = jnp.concatenate(dyz, axis=1)
        dp_ref[:, OFF_Z:OFF_Z + D_SSM] = _mx(dyz * yp * dsz)
        dyp_ssm = dyz * sz

        xc, dsilu = _silu_and_grad(pre_ref[...])
        dtr = p_ref[:, OFF_DT:OFF_DT + 128]
        dt, a_row, a_col, dt_pre, head = _dt_and_cumsum(dtr, dtb_ref[...], alog_ref[...], valid, tril_ref[...])
        dxs, dbm, dcm, ddt, da, dd, ds_prev = _ssd_chunk_bwd(
            xc[:, 0:D_SSM], xc[:, D_SSM:D_SSM + 512], xc[:, D_SSM + 512:], dt, a_row, a_col, st_ref[0],
            ds_carry[...], dyp_ssm, dx_ref[...], e_ref[...], et_ref[...], etf_ref[...], triu_ref[...])
        ds_carry[...] = ds_prev
        gd_ref[...] += dd
        galog_ref[...] += da * a_row
        ddtr = jnp.where(valid & head, ddt * _sigmoid(dt_pre), 0.0)
        gdtb_ref[...] += jnp.sum(ddtr, axis=0, keepdims=True)
        dp_ref[:, OFF_DT:OFF_DT + 128] = _mx(ddtr)

        dpre = jnp.concatenate([dxs, dbm, dcm], axis=1) * dsilu
        gcb_ref[...] += jnp.sum(dpre, axis=0, keepdims=True)
        dext = jnp.concatenate([dpre, dpre_next[...]], axis=0)
        dpre_next[...] = dpre[0:HALO, :]
        ups = [_shift_up(dext, 3 - k) for k in range(4)]
        xbc = p_ref[:, OFF_X:OFF_X + D_XBC]
        gcw_ref[...] += jnp.concatenate([jnp.sum(xbc * ups[k], axis=0, keepdims=True) for k in range(4)], axis=0)
        cw = cw_ref[...]
        dp_ref[:, OFF_X:OFF_X + D_XBC] = _mx(cw[3:4, :] * ups[3] + cw[2:3, :] * ups[2]
                                             + cw[1:2, :] * ups[1] + cw[0:1, :] * ups[0])

        @pl.when((b == nb - 1) & (cc == nc - 1))
        def _():
            exchange.finish()

    blk = lambda w: pl.BlockSpec((CHUNK, w), lambda b, cc: (_chunk_block(b, nc - 1 - cc, nb, nc), 0))
    halo = pl.BlockSpec((HALO, D_POOL),
                        lambda b, cc: (_chunk_block(b, jnp.maximum(nc - 2 - cc, 0), nb, nc) * hb + hb - 1, 0))
    hbm = pl.BlockSpec(memory_space=pl.ANY)
    outs = pl.pallas_call(
        body, grid=(nb, nc), name="mixer_bwd",
        in_specs=[blk(PROJ_W), halo, blk(D_MIX), blk(D_SSM), blk(D_XBC),
                  pl.BlockSpec((1, D_SSM, D_STATE), lambda b, cc: (b * nc + nc - 1 - cc, 0, 0)),
                  _full((4, D_XBC)), _full((1, 128)), _full((1, 128)), _full((1, D_SSM)),
                  _full((1, D_SSM)), _full((4, 128, 128)), _full((1, D_POOL)),
                  _full((128, D_SSM)), _full((D_SSM, 128)), _full((D_SSM, 128)), _full((CHUNK, CHUNK)),
                  _full((CHUNK, CHUNK))] + [hbm] * ns,
        out_specs=[blk(PROJ_W), _full((4, D_XBC)), _full((1, D_XBC)), _full((1, 128)), _full((1, 128)), _full((1, 128)),
                   _full((1, D_SSM)), _full((4, 128, 128)), _full((1, D_POOL))] + [hbm] * ns,
        out_shape=[jax.ShapeDtypeStruct((m, PROJ_W), MXU_DTYPE), jax.ShapeDtypeStruct((4, D_XBC), F32),
                   jax.ShapeDtypeStruct((1, D_XBC), F32), jax.ShapeDtypeStruct((1, 128), F32),
                   jax.ShapeDtypeStruct((1, 128), F32), jax.ShapeDtypeStruct((1, 128), F32),
                   jax.ShapeDtypeStruct((1, D_SSM), F32), jax.ShapeDtypeStruct((4, 128, 128), F32),
                   jax.ShapeDtypeStruct((1, D_POOL), F32)] + _ChipExchange.out_shapes(chip_sums, [False] * ns),
        scratch_shapes=[pltpu.VMEM((D_SSM, D_STATE), F32), pltpu.VMEM((HALO, D_XBC), F32),
                        pltpu.VMEM((HALO, D_POOL), F32)] + _ChipExchange.scratch(ns),
        compiler_params=_params("arbitrary", "arbitrary"),
    )(proj, proj, dy, ypre, conv_pre, states, cw, dt_bias, a_log, d_x, nw, pool_w, pool_scale, e_mat, et_mat, et_f32, tril, triu,
      *chip_sums)
    return outs[:9], outs[9:]


MESH_IDS = pl.DeviceIdType.MESH
_HBM = pl.BlockSpec(memory_space=pltpu.HBM)


def _coords():
    return lax.axis_index("x"), lax.axis_index("y"), lax.axis_index("c")


def _other_chips(x, y):
    return [(1 - x, y), (x, 1 - y), (1 - x, 1 - y)]


class _Gather:
    def __init__(self, ins, outs, by_cols, send_sems, recv_sems, local_sems):
        self.ins, self.outs, self.by_cols, self.n = ins, outs, by_cols, len(ins)
        self.send_sems, self.recv_sems, self.local_sems = send_sems, recv_sems, local_sems
        self.x, self.y, self.c = _coords()
        self.me, self.sibling = (self.x, self.y, self.c), (self.x, self.y, 1 - self.c)
        self.chips = _other_chips(self.x, self.y)

    @staticmethod
    def scratch(n):
        return [pltpu.SemaphoreType.DMA((7 * n,)), pltpu.SemaphoreType.DMA((7 * n,)), pltpu.SemaphoreType.DMA((n,))]

    @staticmethod
    def out_shapes(shards, by_cols):
        return [jax.ShapeDtypeStruct((s.shape[0], N_DEV * s.shape[1]) if cols else (N_DEV,) + s.shape, s.dtype)
                for s, cols in zip(shards, by_cols)]

    def _block(self, t, device):
        idx = 4 * device[0] + 2 * device[1] + device[2]
        if not self.by_cols[t]:
            return self.outs[t].at[idx]
        w = self.ins[t].shape[1]
        return self.outs[t].at[:, pl.ds(pl.multiple_of(idx * w, w), w)]

    def _copy(self, t, k, block, to, own=False):
        dst = self._block(t, block)
        return pltpu.make_async_remote_copy(
            src_ref=self.ins[t] if own else dst, dst_ref=dst, send_sem=self.send_sems.at[t * 7 + k],
            recv_sem=self.recv_sems.at[t * 7 + k], device_id=to, device_id_type=MESH_IDS)

    def _mine(self):
        return [pltpu.make_async_copy(self.ins[t], self._block(t, self.me), self.local_sems.at[t])
                for t in range(self.n)]

    def _first(self):
        cps = []
        for t in range(self.n):
            cps.append(self._copy(t, 0, self.me, self.sibling, own=True))
            cps += [self._copy(t, 1 + j, self.me, (*chip, self.c), own=True) for j, chip in enumerate(self.chips)]
        return cps

    def _passed(self):
        return [self._copy(t, 4 + j, (*chip, self.c), self.sibling)
                for j, chip in enumerate(self.chips) for t in range(self.n)]

    def start(self):
        for cp in self._mine() + self._first():
            cp.start()

    def forward(self):
        for j, chip in enumerate(self.chips):
            for t in range(self.n):
                self._copy(t, 1 + j, (*chip, self.c), self.me).wait_recv()
                self._copy(t, 4 + j, (*chip, self.c), self.sibling).start()

    def finish(self):
        for t in range(self.n):
            self._copy(t, 0, self.sibling, self.me).wait_recv()
            for j, chip in enumerate(self.chips):
                self._copy(t, 4 + j, (*chip, 1 - self.c), self.me).wait_recv()
        for cp in self._first() + self._passed():
            cp.wait_send()
        for cp in self._mine():
            cp.wait()


def _weight_gather(shards):
    n = len(shards)

    def body(*refs):
        g = _Gather(refs[:n], refs[n:2 * n], [False] * n, *refs[2 * n:])
        g.start()
        g.forward()
        g.finish()

    return pl.pallas_call(
        body, name="weight_gather",
        in_specs=[_HBM] * n, out_specs=[_HBM] * n,
        out_shape=_Gather.out_shapes(shards, [False] * n),
        scratch_shapes=_Gather.scratch(n),
    )(*shards)


def _owner_blocks(g, by_cols):
    return (g.shape[0], g.shape[1] // N_DEV) if by_cols else g.shape[2:]


def _grad_exchange_d2d(gs, by_cols, name):
    n = len(gs)

    def body(*refs):
        ins, got = refs[:n], refs[n:2 * n]
        send_sems, recv_sems = refs[2 * n:]
        x, y, c = _coords()

        def src(t, k):
            if not by_cols[t]:
                return ins[t].at[k, 1 - c]
            w = ins[t].shape[1] // N_DEV
            return ins[t].at[:, pl.ds(pl.multiple_of((2 * k + 1 - c) * w, w), w)]

        remote = [pltpu.make_async_remote_copy(
            src_ref=src(t, k), dst_ref=got[t].at[k], send_sem=send_sems.at[t * 4 + k],
            recv_sem=recv_sems.at[t * 4 + k], device_id=(x, y, 1 - c), device_id_type=MESH_IDS)
            for t in range(n) for k in range(4)]
        for cp in remote:
            cp.start()
        for cp in remote:
            cp.wait_recv()
        for cp in remote:
            cp.wait_send()

    return pl.pallas_call(
        body, name=name,
        in_specs=[_HBM] * n, out_specs=[_HBM] * n,
        out_shape=[jax.ShapeDtypeStruct((4,) + _owner_blocks(g, cols), g.dtype) for g, cols in zip(gs, by_cols)],
        scratch_shapes=[pltpu.SemaphoreType.DMA((4 * n,)), pltpu.SemaphoreType.DMA((4 * n,))],
    )(*gs)


def _small_allreduce(pack):
    rows = pack.shape[0]

    def body(p_ref, o_ref, sib_ref, parts_ref, send_sems, recv_sems):
        x, y, c = _coords()
        my_chip = 2 * x + y
        swap = pltpu.make_async_remote_copy(src_ref=p_ref, dst_ref=sib_ref, send_sem=send_sems.at[0],
                                            recv_sem=recv_sems.at[0], device_id=(x, y, 1 - c), device_id_type=MESH_IDS)
        swap.start()
        swap.wait_recv()
        parts_ref[my_chip] = p_ref[...] + sib_ref[...]
        remote = [pltpu.make_async_remote_copy(
            src_ref=parts_ref.at[my_chip], dst_ref=parts_ref.at[my_chip], send_sem=send_sems.at[1 + j],
            recv_sem=recv_sems.at[1 + j], device_id=(cx, cy, c), device_id_type=MESH_IDS)
            for j, (cx, cy) in enumerate(_other_chips(x, y))]
        for cp in remote:
            cp.start()
        for j, (cx, cy) in enumerate(_other_chips(x, y)):
            slot = parts_ref.at[2 * cx + cy]
            pltpu.make_async_remote_copy(src_ref=slot, dst_ref=slot, send_sem=send_sems.at[1 + j],
                                         recv_sem=recv_sems.at[1 + j], device_id=(cx, cy, c),
                                         device_id_type=MESH_IDS).wait_recv()
        o_ref[...] = ((parts_ref[0] + parts_ref[1]) + parts_ref[2]) + parts_ref[3]
        swap.wait_send()
        for cp in remote:
            cp.wait_send()

    vmem = pl.BlockSpec(memory_space=pltpu.VMEM)
    return pl.pallas_call(
        body, name="small_allreduce", in_specs=[vmem], out_specs=vmem,
        out_shape=jax.ShapeDtypeStruct((rows, 128), F32),
        scratch_shapes=[pltpu.VMEM((rows, 128), F32), pltpu.VMEM((4, rows, 128), F32),
                        pltpu.SemaphoreType.DMA((4,)), pltpu.SemaphoreType.DMA((4,))],
    )(pack)


class _ChipExchange:
    def __init__(self, ins, outs, whole, send_sems, recv_sems, local_sems):
        self.ins, self.outs, self.whole, self.n = ins, outs, whole, len(ins)
        self.send_sems, self.recv_sems, self.local_sems = send_sems, recv_sems, local_sems
        self.x, self.y, self.c = _coords()
        self.my_chip = 2 * self.x + self.y
        self.chips = _other_chips(self.x, self.y)

    @staticmethod
    def scratch(n):
        return [pltpu.SemaphoreType.DMA((3 * n,)), pltpu.SemaphoreType.DMA((3 * n,)), pltpu.SemaphoreType.DMA((n,))]

    def _src(self, t, k):
        return self.ins[t] if self.whole[t] else self.ins[t].at[k]

    def _local(self):
        return [pltpu.make_async_copy(self._src(t, self.my_chip), self.outs[t].at[self.my_chip], self.local_sems.at[t])
                for t in range(self.n)]

    def _remote(self):
        return [pltpu.make_async_remote_copy(
            src_ref=self._src(t, 2 * cx + cy), dst_ref=self.outs[t].at[self.my_chip],
            send_sem=self.send_sems.at[t * 3 + j], recv_sem=self.recv_sems.at[t * 3 + j],
            device_id=(cx, cy, self.c), device_id_type=MESH_IDS)
            for t in range(self.n) for j, (cx, cy) in enumerate(self.chips)]

    def start(self):
        for cp in self._remote() + self._local():
            cp.start()

    def finish(self):
        for t in range(self.n):
            for j, (cx, cy) in enumerate(self.chips):
                slot = self.outs[t].at[2 * cx + cy]
                pltpu.make_async_remote_copy(
                    src_ref=slot, dst_ref=slot, send_sem=self.send_sems.at[t * 3 + j],
                    recv_sem=self.recv_sems.at[t * 3 + j], device_id=(cx, cy, self.c),
                    device_id_type=MESH_IDS).wait_recv()
        for cp in self._remote():
            cp.wait_send()
        for cp in self._local():
            cp.wait()

    @staticmethod
    def out_shapes(arrs, whole):
        return [jax.ShapeDtypeStruct(((4,) + a.shape) if w else a.shape, a.dtype) for a, w in zip(arrs, whole)]


_SEMAPHORES = pl.BlockSpec(memory_space=pltpu.SEMAPHORE)
_SIDE_EFFECT = pltpu.SideEffectType.DATAFLOW_SIDE_EFFECTING


def _chip_exchange_start(chip_sum, name):
    def body(src_ref, land_ref, send_sems, recv_sems, src_thru, land_thru, token):
        x, y, c = _coords()
        for j, (cx, cy) in enumerate(_other_chips(x, y)):
            pltpu.make_async_remote_copy(
                src_ref=src_ref.at[2 * cx + cy], dst_ref=land_ref.at[2 * x + y], send_sem=send_sems.at[j],
                recv_sem=recv_sems.at[j], device_id=(cx, cy, c), device_id_type=MESH_IDS).start()
        token[...] = jnp.zeros_like(token)

    zone = pltpu.HBM(chip_sum.shape, chip_sum.dtype)
    return pl.pallas_call(
        body, name=name,
        out_shape=(pltpu.SemaphoreType.DMA((3,)), pltpu.SemaphoreType.DMA((3,)), zone, zone,
                   jax.ShapeDtypeStruct((8, 128), F32)),
        in_specs=(_HBM, _HBM), out_specs=(_SEMAPHORES, _SEMAPHORES, _HBM, _HBM, pl.BlockSpec(memory_space=pltpu.VMEM)),
        input_output_aliases={0: 2, 1: 3},
        compiler_params=pltpu.CompilerParams(has_side_effects=_SIDE_EFFECT),
    )(pltpu.with_memory_space_constraint(chip_sum, pltpu.HBM),
      pltpu.with_memory_space_constraint(lax.empty(chip_sum.shape, chip_sum.dtype), pltpu.HBM))


def _chip_exchange_wait(send_sems, recv_sems, src_thru, land_thru, after, name):
    def body(src_ref, land_ref, send_sems, recv_sems, after_ref, src_out, land_out):
        x, y, c = _coords()
        for j, (cx, cy) in enumerate(_other_chips(x, y)):
            cp = pltpu.make_async_remote_copy(
                src_ref=src_ref.at[2 * cx + cy], dst_ref=land_ref.at[2 * cx + cy], send_sem=send_sems.at[j],
                recv_sem=recv_sems.at[j], device_id=(cx, cy, c), device_id_type=MESH_IDS)
            cp.wait_send()
            cp.wait_recv()

    zone = pltpu.HBM(src_thru.shape, src_thru.dtype)
    return pl.pallas_call(
        body, name=name, out_shape=(zone, zone),
        in_specs=(_HBM, _HBM, _SEMAPHORES, _SEMAPHORES, pl.BlockSpec(memory_space=pl.ANY)), out_specs=(_HBM, _HBM),
        input_output_aliases={0: 0, 1: 1},
        compiler_params=pltpu.CompilerParams(has_side_effects=_SIDE_EFFECT),
    )(src_thru, land_thru, send_sems, recv_sems, after)


def _row_tile(rows, cols, n_arrays):
    budget = 24 * 1024 * 1024
    padded = -(-cols // 128) * 128
    step = 16 if rows % 16 == 0 else 8
    tr = max(step, budget // (n_arrays * 2 * 4 * padded) // step * step)
    while rows % tr:
        tr -= step
    return tr


def _chip_sum(g, by_cols, got, core, name):
    rows, cols = _owner_blocks(g, by_cols)
    tr = _row_tile(rows, cols, 3)

    def body(c_ref, a_ref, b_ref, o_ref):
        o_ref[...] = (a_ref[...] + b_ref[...].astype(F32)).astype(o_ref.dtype)

    own = (pl.BlockSpec((tr, cols), lambda k, i, c: (i, 2 * k + c[0])) if by_cols
           else pl.BlockSpec((None, None, tr, cols), lambda k, i, c: (k, c[0], i, 0)))
    grid_spec = pltpu.PrefetchScalarGridSpec(
        num_scalar_prefetch=1, grid=(4, rows // tr),
        in_specs=[own, pl.BlockSpec((None, tr, cols), lambda k, i, c: (k, i, 0))],
        out_specs=pl.BlockSpec((None, tr, cols), lambda k, i, c: (k, i, 0)))
    return pl.pallas_call(body, grid_spec=grid_spec, name=name,
                          out_shape=jax.ShapeDtypeStruct((4, rows, cols), MXU_DTYPE),
                          compiler_params=_params("arbitrary", "arbitrary"))(core, g, got)


def _adamw_math(w, g, m, v):
    m2 = ADAM_B1 * m + (1.0 - ADAM_B1) * g
    v2 = ADAM_B2 * v + (1.0 - ADAM_B2) * (g * g)
    m_hat = m2 / (1.0 - ADAM_B1 ** ADAM_STEP)
    v_hat = v2 / (1.0 - ADAM_B2 ** ADAM_STEP)
    delta = -ADAM_LR * (m_hat / (jnp.sqrt(v_hat) + ADAM_EPS) + ADAM_WD * w)
    return delta, m2, v2


def _adamw(parts, w, m, v, name, own=None, chip=None):
    rows, cols = w.shape
    tr = _row_tile(rows, cols, 11 if own is None else 15)

    def body(*refs):
        if own is None:
            p_ref, w_ref, m_ref, v_ref, g_ref, d_ref, m2_ref, v2_ref = refs
            part = lambda k: p_ref[k].astype(F32)
        else:
            chip_ref, p_ref, own_ref, w_ref, m_ref, v_ref, g_ref, d_ref, m2_ref, v2_ref = refs
            part = lambda k: jnp.where(chip_ref[0] == k, own_ref[k], p_ref[k]).astype(F32)
        g = ((part(0) + part(1)) + part(2)) + part(3)
        d, m2, v2 = _adamw_math(w_ref[...], g, m_ref[...], v_ref[...])
        g_ref[...] = g
        d_ref[...] = d
        m2_ref[...] = m2
        v2_ref[...] = v2

    blk = pl.BlockSpec((tr, cols), lambda i, *_: (i, 0))
    pblk = pl.BlockSpec((4, tr, cols), lambda i, *_: (0, i, 0))
    out = jax.ShapeDtypeStruct((rows, cols), F32)
    if own is None:
        return pl.pallas_call(body, grid=(rows // tr,), name=name, in_specs=[pblk, blk, blk, blk],
                              out_specs=[blk] * 4, out_shape=[out] * 4,
                              compiler_params=_params("arbitrary"))(parts, w, m, v)
    grid_spec = pltpu.PrefetchScalarGridSpec(num_scalar_prefetch=1, grid=(rows // tr,),
                                             in_specs=[pblk, pblk, blk, blk, blk], out_specs=[blk] * 4)
    return pl.pallas_call(body, grid_spec=grid_spec, name=name, out_shape=[out] * 4,
                          compiler_params=_params("arbitrary"))(chip, parts, own, w, m, v)


def _adamw_small(gs, ws, ms, vs):
    n = len(ws)

    def body(*refs):
        g_refs, w_refs, m_refs, v_refs = (refs[k * n:(k + 1) * n] for k in range(4))
        d_refs, m2_refs, v2_refs = (refs[(4 + k) * n:(5 + k) * n] for k in range(3))
        for t in range(n):
            d, m2, v2 = _adamw_math(w_refs[t][...], g_refs[t][...], m_refs[t][...], v_refs[t][...])
            d_refs[t][...] = d
            m2_refs[t][...] = m2
            v2_refs[t][...] = v2

    outs = pl.pallas_call(body, name="adamw_small",
                          out_shape=[jax.ShapeDtypeStruct(w.shape, F32) for w in ws] * 3)(*gs, *ws, *ms, *vs)
    return outs[:n], outs[n:2 * n], outs[2 * n:]


_PACK_TILE = 8 * 128


def _pack(arrays):
    rows = []
    for a in arrays:
        flat = a.astype(F32).reshape(-1)
        rows.append(jnp.pad(flat, (0, -flat.shape[0] % _PACK_TILE)).reshape(-1, 128))
    return jnp.concatenate(rows, axis=0)


def _unpack(pack, shapes):
    out, r = [], 0
    for s in shapes:
        n = int(np.prod(s))
        out.append(pack[r:r + -(-n // 128)].reshape(-1)[:n].reshape(s))
        r += -(-n // _PACK_TILE) * 8
    return out


def _pad128(v):
    v = v.reshape(1, -1).astype(F32)
    return jnp.pad(v, ((0, 0), (0, 128 - v.shape[1])))


_WEIGHTS = ["meta", "norm_mix_w", "w_in", "pool_w", "pool_scale", "conv_w", "conv_b", "dt_bias", "a_log", "d_skip",
            "ssm_norm_w", "w_out", "norm_ffn_w", "w_ff1", "w_ff2", "norm_f_w"]
_BIG = ["w_in", "w_out", "w_ff1", "w_ff2"]
_SMALL = [n for n in _WEIGHTS if n not in _BIG]


def kernel(x, meta, norm_mix_w, w_in, pool_w, pool_scale, conv_w, conv_b, dt_bias, a_log, d_skip, ssm_norm_w, w_out, norm_ffn_w, w_ff1, w_ff2, norm_f_w, loss_target, m_meta, m_norm_mix_w, m_w_in, m_pool_w, m_pool_scale, m_conv_w, m_conv_b, m_dt_bias, m_a_log, m_d_skip, m_ssm_norm_w, m_w_out, m_norm_ffn_w, m_w_ff1, m_w_ff2, m_norm_f_w, v_meta, v_norm_mix_w, v_w_in, v_pool_w, v_pool_scale, v_conv_w, v_conv_b, v_dt_bias, v_a_log, v_d_skip, v_ssm_norm_w, v_w_out, v_norm_ffn_w, v_w_ff1, v_w_ff2, v_norm_f_w):
    wts = dict(meta=meta, norm_mix_w=norm_mix_w, w_in=w_in, pool_w=pool_w, pool_scale=pool_scale, conv_w=conv_w,
               conv_b=conv_b, dt_bias=dt_bias, a_log=a_log, d_skip=d_skip, ssm_norm_w=ssm_norm_w, w_out=w_out,
               norm_ffn_w=norm_ffn_w, w_ff1=w_ff1, w_ff2=w_ff2, norm_f_w=norm_f_w)
    mom1 = dict(zip(_WEIGHTS, (m_meta, m_norm_mix_w, m_w_in, m_pool_w, m_pool_scale, m_conv_w, m_conv_b, m_dt_bias,
                               m_a_log, m_d_skip, m_ssm_norm_w, m_w_out, m_norm_ffn_w, m_w_ff1, m_w_ff2, m_norm_f_w)))
    mom2 = dict(zip(_WEIGHTS, (v_meta, v_norm_mix_w, v_w_in, v_pool_w, v_pool_scale, v_conv_w, v_conv_b, v_dt_bias,
                               v_a_log, v_d_skip, v_ssm_norm_w, v_w_out, v_norm_ffn_w, v_w_ff1, v_w_ff2, v_norm_f_w)))
    xi, yi, ci = _coords()
    dev = 4 * xi + 2 * yi + ci
    win_cols = w_in.shape[-1]
    cw_cols = conv_w.shape[-1]

    nb, seq, _ = x.shape
    core = jnp.reshape(ci, (1,)).astype(jnp.int32)
    owners = lambda a: a.reshape((4, 2) + a.shape[1:])

    lead_pack = jnp.zeros((N_META, 512), F32)
    lead_pack = lead_pack.at[:, :128].set(meta).at[:4, 128:128 + cw_cols].set(conv_w[0])
    g_win, g_lead = _weight_gather([_mx(w_in[0]), lead_pack])
    win_full = _assemble_bands(g_win, PROJ_W)
    meta_full = jnp.transpose(g_lead[:, :, :128], (1, 0, 2)).reshape(N_META, D_MODEL)
    cw_full = jnp.transpose(g_lead[:, :4, 128:128 + cw_cols], (1, 0, 2)).reshape(4, D_XBC)

    lead = jnp.concatenate([jnp.zeros((PAD_ROWS, D_MODEL), F32), meta_full] * nb, axis=0)
    x_rows = x.reshape(nb * seq, D_MODEL)
    tgt_rows = loss_target.reshape(nb * seq, D_MODEL)
    dt_bias_p, a_log_p = _pad128(dt_bias), _pad128(a_log)
    d_x = jnp.repeat(d_skip.reshape(1, N_HEADS).astype(F32), HEAD_DIM, axis=1)
    norm_f_row = norm_f_w.reshape(1, D_MODEL)

    hn1, proj = _in_proj(x_rows, lead, norm_mix_w, win_full)
    late_cols = [False, True, False]
    y, ypre, conv_pre, states, (g_wout, wff1_full, g_wff2) = _mixer_fwd(
        proj, cw_full, conv_b, dt_bias_p, a_log_p, d_x, ssm_norm_w, pool_w[0], pool_scale, nb,
        [_mx(w_out[0]), _mx(w_ff1[0]), _mx(w_ff2[0])], late_cols)
    wout_full = g_wout.reshape(D_MIX, D_MODEL)
    wff2_full = g_wff2.reshape(D_FF, D_MODEL)
    loss, gr_nf, gr_nffn, ff, da, hn2, dh1, dh2, dy = _ffn_fwd_bwd(
        x_rows, lead, y, tgt_rows, wout_full, norm_ffn_w, wff1_full, wff2_full, norm_f_row)
    gr_wff2 = _tn_matmul(ff, dh2, "grad_w_ff2", tka=1024, max_slab=2816)
    gr_wff1 = _tn_matmul(hn2, da, "grad_w_ff1", tka=512)
    gr_wout = _tn_matmul(y, dh1, "grad_w_out", tka=1024, max_slab=2816)

    by_owner = lambda k: [owners(gr_wout[k].reshape(N_DEV, D_MIX // N_DEV, D_MODEL)), gr_wff1[k],
                          owners(gr_wff2[k].reshape(N_DEV, D_FF // N_DEV, D_MODEL))]
    late_parts = by_owner(0)
    late_got = _grad_exchange_d2d(by_owner(1), late_cols, "grad_exchange_d2d_late")
    late_sums = [_chip_sum(late_parts[t], late_cols[t], late_got[t], core, "chip_sum_late_%d" % t) for t in range(3)]
    (dproj, gr_cw, gr_cb, gr_dtb, gr_alog, gr_d, gr_nw, gr_pw, gr_ps), late_exchanged = _mixer_bwd(
        proj, dy, ypre, conv_pre, states, cw_full, dt_bias_p, a_log_p, d_x, ssm_norm_w, pool_w[0], pool_scale, nb,
        late_sums)

    win_parts = [owners(_tn_matmul_banded(hn1, dproj, "grad_w_in", win_cols, tka=512))]
    win_got = _grad_exchange_d2d(win_parts, [False], "grad_exchange_d2d_w_in")
    win_sum = _chip_sum(win_parts[0], False, win_got[0], core, "chip_sum_w_in")
    send_sems, recv_sems, win_sum, win_landing, started = _chip_exchange_start(win_sum, "w_in_exchange_start")
    gx_rows, gr_nmix, gr_meta = _in_proj_bwd(dproj, x_rows, lead, dh1, norm_mix_w, win_full, started)
    win_sum, win_landed = _chip_exchange_wait(send_sems, recv_sems, win_sum, win_landing, gr_nmix, "w_in_exchange_wait")
    parts = dict(w_in=win_landed, w_out=late_exchanged[0], w_ff1=late_exchanged[1], w_ff2=late_exchanged[2])
    my_chip = jnp.reshape(2 * xi + yi, (1,)).astype(jnp.int32)

    small_full = dict(meta=gr_meta, norm_mix_w=gr_nmix, pool_w=gr_pw, pool_scale=gr_ps, conv_w=gr_cw, conv_b=gr_cb,
                      dt_bias=gr_dtb[:, :N_HEADS], a_log=gr_alog[:, :N_HEADS], d_skip=gr_d[:, :N_HEADS],
                      ssm_norm_w=gr_nw, norm_ffn_w=gr_nffn, norm_f_w=gr_nf, loss=loss[0:1, 0:1])
    small_names = list(small_full)
    small_sum = _small_allreduce(_pack([small_full[n] for n in small_names]))
    gs = dict(zip(small_names, _unpack(small_sum, [small_full[n].shape for n in small_names])))
    gs["meta"] = lax.dynamic_slice_in_dim(gs["meta"], dev * 128, 128, axis=1)
    gs["conv_w"] = lax.dynamic_slice_in_dim(gs["conv_w"], dev * cw_cols, cw_cols, axis=1)

    res = {}
    for n in _BIG:
        shp = wts[n].shape
        own = dict(own=win_sum, chip=my_chip) if n == "w_in" else {}
        res[n] = [o.reshape(shp) for o in _adamw(parts[n], wts[n][0], mom1[n][0], mom2[n][0], "adamw_" + n, **own)]
    as2d = lambda a: a.reshape(-1, a.shape[-1])
    small_g = [as2d(gs[n].reshape(wts[n].shape)) for n in _SMALL]
    small_out = _adamw_small(small_g, *[[as2d(d[n]) for n in _SMALL] for d in (wts, mom1, mom2)])
    for k, n in enumerate(_SMALL):
        res[n] = [o[k].reshape(wts[n].shape) for o in (small_g,) + tuple(small_out)]

    grad_x = gx_rows.reshape(nb, seq, D_MODEL)
    return (gs["loss"][0, 0], grad_x, *[res[n][0] for n in _WEIGHTS], *[res[n][1] for n in _WEIGHTS],
            *[res[n][2] for n in _WEIGHTS], *[res[n][3] for n in _WEIGHTS])
```

```python
import functools

import numpy as np
import jax
import jax.numpy as jnp
from jax import lax
from jax.experimental import pallas as pl
from jax.experimental.pallas import tpu as pltpu

F32 = jnp.float32
MXU_DTYPE = jnp.bfloat16

D_MODEL = 1024
D_POOL = 512
D_SSM = 1536
D_XBC = 2560
N_HEADS = 24
HEAD_DIM = 64
N_GROUPS = 4
GROUP_CH = D_SSM // N_GROUPS
D_STATE = 128
CHUNK = 128
N_META = 16
LEAD = CHUNK
PAD_ROWS = LEAD - N_META
ROW_TILE = 2 * CHUNK
D_MIX = D_POOL + D_SSM
D_FF = 4096
PROJ_W = 4736
OFF_Z = D_POOL
OFF_X = D_POOL + D_SSM
OFF_DT = OFF_X + D_XBC
D_IN_PROJ = OFF_DT + N_HEADS
POOL_WINDOWS = (2, 4, 8, 16)
HALO = 16
EPS = 1e-5
N_DEV = 8

ADAM_LR, ADAM_B1, ADAM_B2, ADAM_EPS, ADAM_WD, ADAM_STEP = 0.001, 0.9, 0.999, 1e-08, 0.01, 10

VMEM_LIMIT = 60 * 1024 * 1024


def _mx(a):
    return a.astype(MXU_DTYPE)


def _dot(a, b):
    return jnp.dot(_mx(a), _mx(b), preferred_element_type=F32)


def _dot_nt(a, b):
    return lax.dot_general(_mx(a), _mx(b), (((1,), (1,)), ((), ())), preferred_element_type=F32)


def _dot_tn(a, b):
    return lax.dot_general(_mx(a), _mx(b), (((0,), (0,)), ((), ())), preferred_element_type=F32)


def _split3(x):
    hi = x.astype(MXU_DTYPE)
    r = x - hi.astype(F32)
    mid = r.astype(MXU_DTYPE)
    lo = (r - mid.astype(F32)).astype(MXU_DTYPE)
    return hi, mid, lo


def _exact_l(c, x):
    hi, mid, lo = _split3(x)
    f = lambda p: jnp.dot(c, p, preferred_element_type=F32)
    return f(hi) + f(mid) + f(lo)


def _exact_r(x, c):
    hi, mid, lo = x if isinstance(x, tuple) else _split3(x)
    f = lambda p: jnp.dot(p, c, preferred_element_type=F32)
    return f(hi) + f(mid) + f(lo)


def _contract(x, c):
    hi = x.astype(MXU_DTYPE)
    lo = (x - hi.astype(F32)).astype(MXU_DTYPE)
    return jnp.dot(hi, c, preferred_element_type=F32) + jnp.dot(lo, c, preferred_element_type=F32)


def _sigmoid(x):
    return jax.nn.sigmoid(x)


def _softplus(x):
    return jnp.maximum(x, 0.0) + jnp.log1p(jnp.exp(-jnp.abs(x)))


def _silu_and_grad(x):
    s = _sigmoid(x)
    y = x * s
    return y, s + y * (1.0 - s)


def _shift_up(ext, s):
    if s == 0:
        return ext[:CHUNK, :]
    return pltpu.roll(ext, ext.shape[0] - s, 0)[:CHUNK, :]


def _by_pool_group(lane, a2, a4, a8, a16):
    return jnp.where(lane < 128, a2, jnp.where(lane < 256, a4, jnp.where(lane < 384, a8, a16)))


def _pool_inv_count(chunk_idx):
    row = lax.broadcasted_iota(jnp.int32, (CHUNK, D_POOL), 0)
    lane = lax.broadcasted_iota(jnp.int32, (CHUNK, D_POOL), 1)
    pos1 = jnp.maximum(chunk_idx * CHUNK + row - (PAD_ROWS - 1), 1)
    w = _by_pool_group(lane, 2, 4, 8, 16)
    return 1.0 / jnp.minimum(pos1, w).astype(F32), lane


def _pool_window_sums(u_ext, lane):
    s2 = u_ext + pltpu.roll(u_ext, 1, 0)
    s4 = s2 + pltpu.roll(s2, 2, 0)
    s8 = s4 + pltpu.roll(s4, 4, 0)
    s16 = s8 + pltpu.roll(s8, 8, 0)
    return _by_pool_group(lane, s2[HALO:], s4[HALO:], s8[HALO:], s16[HALO:])


def _pool_window_sums_ahead(q_ext, lane):
    n = q_ext.shape[0]
    r2 = q_ext + pltpu.roll(q_ext, n - 1, 0)
    r4 = r2 + pltpu.roll(r2, n - 2, 0)
    r8 = r4 + pltpu.roll(r4, n - 4, 0)
    r16 = r8 + pltpu.roll(r8, n - 8, 0)
    return _by_pool_group(lane, r2[:CHUNK], r4[:CHUNK], r8[:CHUNK], r16[:CHUNK])


def _conv_pre(ext, xbc, cw, cb):
    s1 = pltpu.roll(ext, 1, 0)
    near = cw[3:4, :] * xbc + cw[2:3, :] * s1[HALO:, :]
    far = cw[1:2, :] * ext + cw[0:1, :] * s1
    return cb + near + pltpu.roll(far, 2, 0)[HALO:, :]


def _dt_and_cumsum(dtr, dt_bias, a_log, valid, tril):
    lane = lax.broadcasted_iota(jnp.int32, (CHUNK, 128), 1)
    head = lane < N_HEADS
    pre = dtr + dt_bias
    dt = jnp.where(valid & head, _softplus(pre), 0.0)
    a_row = jnp.where(head[0:1, :], -jnp.exp(a_log), 0.0)
    a_col = _exact_l(tril, dt * a_row)
    return dt, a_row, a_col, pre, head


def _decay(a_col, a_row_t, h, causal):
    seg = a_col[:, h:h + 1] - a_row_t[h:h + 1, :]
    return jnp.where(causal, jnp.exp(jnp.minimum(seg, 0.0)), 0.0)


def _ssd_chunk_fwd(xs, bm, cm, dt, a_col, s_prev, d_x, e_mat, et_f32):
    lane = lax.broadcasted_iota(jnp.int32, (CHUNK, 128), 1)
    rowi = lax.broadcasted_iota(jnp.int32, (CHUNK, CHUNK), 0)
    coli = lax.broadcasted_iota(jnp.int32, (CHUNK, CHUNK), 1)
    causal = rowi >= coli
    a_row_t = a_col.T
    ax = _exact_r(a_col, e_mat)
    dtx = _exact_r(dt, e_mat)
    xdt = xs * dtx
    ax_last = ax[CHUNK - 1:CHUNK, :]
    e_a = jnp.exp(ax)
    w_end = xdt * jnp.exp(ax_last - ax)
    cd_col = jnp.exp(jnp.sum(et_f32 * a_col[CHUNK - 1:CHUNK, :], axis=1, keepdims=True))
    ys, s_new = [], []
    for g in range(N_GROUPS):
        gs = slice(g * GROUP_CH, (g + 1) * GROUP_CH)
        bg = bm[:, g * D_STATE:(g + 1) * D_STATE]
        cg = cm[:, g * D_STATE:(g + 1) * D_STATE]
        sg = s_prev[gs, :]
        cb = _dot_nt(cg, bg)
        y_off = _dot_nt(cg, sg) * e_a[:, gs]
        s_new.append(sg * cd_col[gs, :] + _dot_tn(w_end[:, gs], bg))
        for pr in range(3):
            c0 = g * GROUP_CH + pr * 128
            xdt_p = xdt[:, c0:c0 + 128]
            h0 = g * 6 + pr * 2
            y0 = _dot(cb * _decay(a_col, a_row_t, h0, causal), xdt_p)
            y1 = _dot(cb * _decay(a_col, a_row_t, h0 + 1, causal), xdt_p)
            ys.append(jnp.where(lane < HEAD_DIM, y0, y1) + y_off[:, pr * 128:(pr + 1) * 128])
    y = jnp.concatenate(ys, axis=1) + d_x * xs
    return y, jnp.concatenate(s_new, axis=0)


def _ssd_chunk_bwd(xs, bm, cm, dt, a_row, a_col, s_prev, ds_new, dy, d_x, e_mat, et_mat, et_f32, triu):
    lane = lax.broadcasted_iota(jnp.int32, (CHUNK, 128), 1)
    sub = lax.broadcasted_iota(jnp.int32, (CHUNK, 128), 0)
    rowi = lax.broadcasted_iota(jnp.int32, (CHUNK, CHUNK), 0)
    coli = lax.broadcasted_iota(jnp.int32, (CHUNK, CHUNK), 1)
    causal = rowi >= coli
    a_row_t = a_col.T
    a_last = a_col[CHUNK - 1:CHUNK, :]
    a_split, dt_split = _split3(a_col), _split3(dt)
    sub8 = lax.broadcasted_iota(jnp.int32, (8, GROUP_CH), 0)

    dxs, dbs, dcs, dsp = [], [], [], []
    zcol = jnp.zeros((CHUNK, 128), F32)
    zrows = []
    da_col = jnp.zeros((CHUNK, 128), F32)
    ddt = jnp.zeros((CHUNK, 128), F32)
    head_sums = jnp.zeros((8, 128), F32)
    q_row = jnp.zeros((1, 128), F32)
    for g in range(N_GROUPS):
        gs = slice(g * GROUP_CH, (g + 1) * GROUP_CH)
        e_g, et_g = e_mat[:, gs], et_mat[gs, :]
        xs_g, dy_g = xs[:, gs], dy[:, gs]
        ax = _exact_r(a_split, e_g)
        dtx = _exact_r(dt_split, e_g)
        xdt = xs_g * dtx
        dte = jnp.exp(ax[CHUNK - 1:CHUNK, :] - ax)
        w_end = xdt * dte
        cd_col = jnp.exp(jnp.sum(et_f32[gs, :] * a_last, axis=1, keepdims=True))
        dye = dy_g * jnp.exp(ax)
        bg = bm[:, g * D_STATE:(g + 1) * D_STATE]
        cg = cm[:, g * D_STATE:(g + 1) * D_STATE]
        sg = s_prev[gs, :]
        dsg = ds_new[gs, :]
        cb = _dot_nt(cg, bg)
        cs = _dot_nt(cg, sg)
        dcg = _dot(dye, sg)
        dsp.append(dsg * cd_col + _dot_tn(dye, cg))
        dwg = _dot_nt(bg, dsg)
        dbg = _dot(w_end, dsg)
        ww = dwg * w_end
        t1 = jnp.sum(dsg * sg, axis=1, keepdims=True) * cd_col
        dcb = jnp.zeros((CHUNK, CHUNK), F32)
        pairs = []
        for pr in range(3):
            ps = slice(pr * 128, (pr + 1) * 128)
            xdt_p, dy_p = xdt[:, ps], dy_g[:, ps]
            acc = None
            for half in range(2):
                h = g * 6 + pr * 2 + half
                ld = _decay(a_col, a_row_t, h, causal)
                gm = cb * ld
                dym = jnp.where((lane < HEAD_DIM) if half == 0 else (lane >= HEAD_DIM), dy_p, 0.0)
                dg = _dot_nt(dym, xdt_p)
                dseg = dg * gm
                dcb = dcb + dg * ld
                t = _dot_tn(gm, dym)
                acc = t if acc is None else acc + t
                zcol = jnp.where(lane == h, jnp.sum(dseg, axis=1, keepdims=True), zcol)
                zrows.append(jnp.sum(dseg, axis=0, keepdims=True))
            pairs.append(acc)
        dxdt = dwg * dte + jnp.concatenate(pairs, axis=1)
        dcs.append(dcg + _dot(dcb, bg))
        dbs.append(dbg + _dot_tn(dcb, cg))
        dxs.append(dxdt * dtx + d_x[:, gs] * dy_g)
        da_col = da_col + _contract(dye * cs - ww, et_g)
        ddt = ddt + _contract(dxdt * xs_g, et_g)
        col_sums = jnp.where(sub8 == 0, jnp.sum(dy_g * xs_g, axis=0, keepdims=True),
                             jnp.where(sub8 == 1, jnp.sum(ww, axis=0, keepdims=True), 0.0))
        head_sums = head_sums + _exact_r(col_sums, et_g)
        q_row = q_row + jnp.sum(et_f32[gs, :] * t1, axis=0, keepdims=True)

    dd = head_sums[0:1, :]
    q_row = q_row + head_sums[1:2, :]
    zrow = jnp.concatenate(zrows + [jnp.zeros((128 - N_HEADS, CHUNK), F32)], axis=0)
    da_col = da_col + zcol - zrow.T + jnp.where(sub == CHUNK - 1, q_row, 0.0)
    rc = _exact_l(triu, da_col)
    ddt = ddt + rc * a_row
    da = jnp.sum(rc * dt, axis=0, keepdims=True)
    return (jnp.concatenate(dxs, axis=1), jnp.concatenate(dbs, axis=1), jnp.concatenate(dcs, axis=1), ddt, da, dd,
            jnp.concatenate(dsp, axis=0))


_SSD_CONSTANT_SHAPES = dict(e=((128, D_SSM), MXU_DTYPE), et=((D_SSM, 128), MXU_DTYPE), et_f32=((D_SSM, 128), F32),
                            tril=((CHUNK, CHUNK), MXU_DTYPE), triu=((CHUNK, CHUNK), MXU_DTYPE))


def _ssd_constant_scratch(names):
    return [pltpu.VMEM(*_SSD_CONSTANT_SHAPES[n]) for n in names]


def _fill_ssd_constants(**refs):
    iota = lambda shape, d: lax.broadcasted_iota(jnp.int32, shape, d)
    shift = HEAD_DIM.bit_length() - 1
    marks = dict(
        e=lambda: iota((128, D_SSM), 0) == (iota((128, D_SSM), 1) >> shift),
        et=lambda: iota((D_SSM, 128), 1) == (iota((D_SSM, 128), 0) >> shift),
        et_f32=lambda: iota((D_SSM, 128), 1) == (iota((D_SSM, 128), 0) >> shift),
        tril=lambda: iota((CHUNK, CHUNK), 1) <= iota((CHUNK, CHUNK), 0),
        triu=lambda: iota((CHUNK, CHUNK), 1) >= iota((CHUNK, CHUNK), 0))
    for name, ref in refs.items():
        ref[...] = jnp.where(marks[name](), 1.0, 0.0).astype(ref.dtype)


def _row_views(ref, widths):
    views, off = [], 0
    for w in widths:
        views.append(ref.at[:, off:off + w])
        off += w
    return views


def _full(shape):
    nd = len(shape)
    return pl.BlockSpec(shape, lambda *_: (0,) * nd)


def _params(*sem):
    return pltpu.CompilerParams(dimension_semantics=sem, vmem_limit_bytes=VMEM_LIMIT)


def _token_tiles(width, n_tok_tiles):
    return pl.BlockSpec((ROW_TILE, width), lambda i: (jnp.minimum(i, n_tok_tiles - 1), 0))


def _in_proj(x, lead, w1, win):
    nt = x.shape[0] // ROW_TILE
    m = x.shape[0] + ROW_TILE
    tm = ROW_TILE

    def body(x_ref, lead_ref, w1_ref, win_hbm, hn_ref, proj_ref, win_v, sem):
        i = pl.program_id(0)

        @pl.when(i == 0)
        def _():
            cp = pltpu.make_async_copy(win_hbm, win_v, sem)
            cp.start()
            cp.wait()

        x = jnp.where(i == nt, lead_ref[...], x_ref[...])
        r = lax.rsqrt(jnp.mean(x * x, axis=-1, keepdims=True) + EPS)
        hn = _mx(x * r * w1_ref[...])
        hn_ref[...] = hn
        for j in range(0, PROJ_W, 512):
            w = min(512, PROJ_W - j)
            proj_ref[:, j:j + w] = jnp.dot(hn, win_v[:, j:j + w], preferred_element_type=F32)

    return pl.pallas_call(
        body, grid=(m // tm,), name="in_proj",
        in_specs=[_token_tiles(D_MODEL, nt), _full((ROW_TILE, D_MODEL)), _full((1, D_MODEL)),
                  pl.BlockSpec(memory_space=pl.ANY)],
        out_specs=[pl.BlockSpec((tm, D_MODEL), lambda i: (i, 0)), pl.BlockSpec((tm, PROJ_W), lambda i: (i, 0))],
        out_shape=[jax.ShapeDtypeStruct((m, D_MODEL), MXU_DTYPE), jax.ShapeDtypeStruct((m, PROJ_W), F32)],
        scratch_shapes=[pltpu.VMEM((D_MODEL, PROJ_W), MXU_DTYPE), pltpu.SemaphoreType.DMA],
        compiler_params=_params("arbitrary"),
    )(x, lead, w1, win)


def _ffn_fwd_bwd(x, lead, y, tgt, wout, w2n, wff1, wff2, wfn):
    nt = x.shape[0] // ROW_TILE
    m = x.shape[0] + ROW_TILE
    tm = ROW_TILE
    nj = D_FF // 1024

    def body(x_ref, lead_ref, y_ref, tgt_ref, w2n_ref, wfn_ref, wout_hbm, wff1_hbm, wff2_hbm,
             loss_ref, gwf_ref, gw2_ref, ff_ref, da_ref, hn2_ref, dh1_ref, dh2_ref, dy_ref,
             wout_v, wff1_v, wff2_v, a_s, sems):
        i = pl.program_id(0)
        hp = jnp.where(i == nt, lead_ref[...], x_ref[...])

        @pl.when(i == 0)
        def _():
            cps = [pltpu.make_async_copy(s, d, sems.at[k])
                   for k, (s, d) in enumerate(((wout_hbm, wout_v), (wff1_hbm, wff1_v), (wff2_hbm, wff2_v)))]
            for cp in cps:
                cp.start()
            for cp in cps:
                cp.wait()
            loss_ref[...] = jnp.zeros_like(loss_ref)
            gwf_ref[...] = jnp.zeros_like(gwf_ref)
            gw2_ref[...] = jnp.zeros_like(gw2_ref)

        h1 = hp + jnp.dot(y_ref[...], wout_v[...], preferred_element_type=F32)
        r2 = lax.rsqrt(jnp.mean(h1 * h1, axis=-1, keepdims=True) + EPS)
        n2 = h1 * r2
        w2n_row = w2n_ref[...]
        hn2 = _mx(n2 * w2n_row)
        hn2_ref[...] = hn2
        h2 = h1
        for j in range(nj):
            js = slice(j * 1024, (j + 1) * 1024)
            a = jnp.dot(hn2, wff1_v[:, js], preferred_element_type=F32)
            a_s[:, js] = a
            ra = jnp.maximum(a, 0.0)
            ff = _mx(ra * ra)
            ff_ref[:, js] = ff
            h2 = h2 + jnp.dot(ff, wff2_v[js, :], preferred_element_type=F32)

        r3 = lax.rsqrt(jnp.mean(h2 * h2, axis=-1, keepdims=True) + EPS)
        n3 = h2 * r3
        wf_row = wfn_ref[...]
        err = n3 * wf_row - tgt_ref[...]
        tokf = (i < nt).astype(F32)
        loss_ref[...] += 0.5 * jnp.sum(jnp.mean(err * err, axis=-1, keepdims=True) * tokf)
        dout = err * (tokf / D_MODEL)
        gwf_ref[...] += jnp.sum(dout * n3, axis=0, keepdims=True)
        dn3 = dout * wf_row
        dh2 = r3 * (dn3 - n3 * jnp.mean(dn3 * n3, axis=-1, keepdims=True))
        dh2m = _mx(dh2)
        dh2_ref[...] = dh2m

        dhn2 = jnp.zeros((tm, D_MODEL), F32)
        for j in range(nj):
            js = slice(j * 1024, (j + 1) * 1024)
            dff = lax.dot_general(dh2m, wff2_v[js, :], (((1,), (1,)), ((), ())), preferred_element_type=F32)
            da = _mx(dff * (2.0 * jnp.maximum(a_s[:, js], 0.0)))
            da_ref[:, js] = da
            dhn2 = dhn2 + lax.dot_general(da, wff1_v[:, js], (((1,), (1,)), ((), ())), preferred_element_type=F32)
        gw2_ref[...] += jnp.sum(dhn2 * n2, axis=0, keepdims=True)
        dn2 = dhn2 * w2n_row
        dh1 = dh2 + r2 * (dn2 - n2 * jnp.mean(dn2 * n2, axis=-1, keepdims=True))
        dh1_ref[...] = dh1
        dy_ref[...] = lax.dot_general(_mx(dh1), wout_v[...], (((1,), (1,)), ((), ())), preferred_element_type=F32)

    rows = lambda w: pl.BlockSpec((tm, w), lambda i: (i, 0))
    hbm = pl.BlockSpec(memory_space=pl.ANY)
    return pl.pallas_call(
        body, grid=(m // tm,), name="ffn_fwd_bwd",
        in_specs=[_token_tiles(D_MODEL, nt), _full((ROW_TILE, D_MODEL)), rows(D_MIX), _token_tiles(D_MODEL, nt),
                  _full((1, D_MODEL)), _full((1, D_MODEL)), hbm, hbm, hbm],
        out_specs=[_full((1, 128)), _full((1, D_MODEL)), _full((1, D_MODEL)), rows(D_FF), rows(D_FF), rows(D_MODEL),
                   rows(D_MODEL), rows(D_MODEL), rows(D_MIX)],
        out_shape=[jax.ShapeDtypeStruct((1, 128), F32), jax.ShapeDtypeStruct((1, D_MODEL), F32),
                   jax.ShapeDtypeStruct((1, D_MODEL), F32), jax.ShapeDtypeStruct((m, D_FF), MXU_DTYPE),
                   jax.ShapeDtypeStruct((m, D_FF), MXU_DTYPE), jax.ShapeDtypeStruct((m, D_MODEL), MXU_DTYPE),
                   jax.ShapeDtypeStruct((m, D_MODEL), F32), jax.ShapeDtypeStruct((m, D_MODEL), MXU_DTYPE),
                   jax.ShapeDtypeStruct((m, D_MIX), F32)],
        scratch_shapes=[pltpu.VMEM((D_MIX, D_MODEL), MXU_DTYPE), pltpu.VMEM((D_MODEL, D_FF), MXU_DTYPE),
                        pltpu.VMEM((D_FF, D_MODEL), MXU_DTYPE), pltpu.VMEM((tm, D_FF), F32),
                        pltpu.SemaphoreType.DMA((3,))],
        compiler_params=_params("arbitrary"),
    )(x, lead, y, tgt, w2n, wfn, wout, wff1, wff2)


def _in_proj_bwd(dproj, x, lead, dh1, w1, win, after):
    nt = x.shape[0] // ROW_TILE
    m = x.shape[0] + ROW_TILE
    tm = ROW_TILE

    def body(dp_ref, x_ref, lead_ref, dh1_ref, w1_ref, win_hbm, after_ref, gx_ref, gw1_ref, gmeta_ref, win_v, sem):
        i = pl.program_id(0)

        @pl.when(i == 0)
        def _():
            cp = pltpu.make_async_copy(win_hbm, win_v, sem)
            cp.start()
            cp.wait()
            gw1_ref[...] = jnp.zeros_like(gw1_ref)
            gmeta_ref[...] = jnp.zeros_like(gmeta_ref)

        dhn = lax.dot_general(dp_ref[...], win_v[...], (((1,), (1,)), ((), ())), preferred_element_type=F32)
        x = jnp.where(i == nt, lead_ref[...], x_ref[...])
        r = lax.rsqrt(jnp.mean(x * x, axis=-1, keepdims=True) + EPS)
        n = x * r
        gw1_ref[...] += jnp.sum(dhn * n, axis=0, keepdims=True)
        dn = dhn * w1_ref[...]
        dh0 = dh1_ref[...] + r * (dn - n * jnp.mean(dn * n, axis=-1, keepdims=True))

        @pl.when(i < nt)
        def _():
            gx_ref[...] = dh0

        @pl.when(i == nt)
        def _():
            gmeta_ref[...] = dh0[PAD_ROWS:LEAD, :] + dh0[LEAD + PAD_ROWS:2 * LEAD, :]

    rows = lambda w: pl.BlockSpec((tm, w), lambda i: (i, 0))
    hbm = pl.BlockSpec(memory_space=pl.ANY)
    return pl.pallas_call(
        body, grid=(m // tm,), name="in_proj_bwd",
        in_specs=[rows(PROJ_W), _token_tiles(D_MODEL, nt), _full((ROW_TILE, D_MODEL)), rows(D_MODEL),
                  _full((1, D_MODEL)), hbm, hbm],
        out_specs=[_token_tiles(D_MODEL, nt), _full((1, D_MODEL)), _full((N_META, D_MODEL))],
        out_shape=[jax.ShapeDtypeStruct(x.shape, F32), jax.ShapeDtypeStruct((1, D_MODEL), F32),
                   jax.ShapeDtypeStruct((N_META, D_MODEL), F32)],
        scratch_shapes=[pltpu.VMEM((D_MODEL, PROJ_W), MXU_DTYPE), pltpu.SemaphoreType.DMA],
        compiler_params=_params("arbitrary"),
    )(dproj, x, lead, dh1, w1, win, after)


MXU_DEPTH = 256


def _row_slab(m, cap):
    return max(k for k in range(MXU_DEPTH, cap + 1, MXU_DEPTH) if m % k == 0)


def _tn_matmul(a, b, name, tka, max_slab=768, tn=512):
    m, ka = a.shape
    nb = b.shape[1]
    tkm = _row_slab(m, max_slab)
    n_steps = m // tkm

    def body(a_ref, b_ref, o_ref, omx_ref):
        k = pl.program_id(1)

        @pl.when(k == 0)
        def _():
            o_ref[...] = jnp.zeros_like(o_ref)

        at = _mx(a_ref[...])
        for j in range(0, nb, tn):
            w = min(tn, nb - j)
            o_ref[:, j:j + w] += lax.dot_general(at, _mx(b_ref[:, j:j + w]), (((0,), (0,)), ((), ())),
                                                 preferred_element_type=F32)

        @pl.when(k == n_steps - 1)
        def _():
            omx_ref[...] = _mx(o_ref[...])

    out = pl.BlockSpec((tka, nb), lambda i, k: (i, 0))
    return pl.pallas_call(
        body, grid=(ka // tka, n_steps), name=name,
        in_specs=[pl.BlockSpec((tkm, tka), lambda i, k: (k, i)), pl.BlockSpec((tkm, nb), lambda i, k: (k, 0))],
        out_specs=[out, out],
        out_shape=[jax.ShapeDtypeStruct((ka, nb), F32), jax.ShapeDtypeStruct((ka, nb), MXU_DTYPE)],
        compiler_params=_params("arbitrary", "arbitrary"),
    )(a, b)


def _tn_matmul_banded(a, b, name, band, tka, tn=512):
    m, ka = a.shape
    nb = b.shape[1]
    tkm = _row_slab(m, 768)
    n_steps = m // tkm

    def body(a_ref, b_ref, o_ref, acc):
        k = pl.program_id(1)

        @pl.when(k == 0)
        def _():
            acc[...] = jnp.zeros_like(acc)

        at = _mx(a_ref[...])
        for j in range(0, nb, tn):
            w = min(tn, nb - j)
            acc[:, j:j + w] += lax.dot_general(at, _mx(b_ref[:, j:j + w]), (((0,), (0,)), ((), ())),
                                               preferred_element_type=F32)

        @pl.when(k == n_steps - 1)
        def _():
            for j in range(N_DEV):
                o_ref[j] = acc[:, j * band:(j + 1) * band]

    return pl.pallas_call(
        body, grid=(ka // tka, n_steps), name=name,
        in_specs=[pl.BlockSpec((tkm, tka), lambda i, k: (k, i)), pl.BlockSpec((tkm, nb), lambda i, k: (k, 0))],
        out_specs=pl.BlockSpec((N_DEV, tka, band), lambda i, k: (0, i, 0)),
        out_shape=jax.ShapeDtypeStruct((N_DEV, ka, band), F32),
        scratch_shapes=[pltpu.VMEM((tka, nb), F32)],
        compiler_params=_params("arbitrary", "arbitrary"),
    )(a, b)


def _assemble_bands(g, width):
    n, rows, band = g.shape
    tr = 256

    def body(g_ref, o_ref):
        parts = [g_ref[j] for j in range(n)] + [jnp.zeros((tr, width - n * band), g.dtype)]
        o_ref[...] = jnp.concatenate(parts, axis=1)

    return pl.pallas_call(
        body, grid=(rows // tr,), name="assemble_w_in",
        in_specs=[pl.BlockSpec((n, tr, band), lambda i: (0, i, 0))],
        out_specs=pl.BlockSpec((tr, width), lambda i: (i, 0)),
        out_shape=jax.ShapeDtypeStruct((rows, width), g.dtype),
        compiler_params=_params("arbitrary"),
    )(g)


def _chunk_block(b, c, nb, nc):
    return jnp.where(c == 0, nb * (nc - 1) + b, b * (nc - 1) + c - 1)

def _mixer_fwd(proj, cw, cb, dt_bias, a_log, d_x, nw, pool_w, pool_scale, nb, shards, by_cols):
    m = proj.shape[0]
    nc = m // nb // CHUNK
    n_steps = nb * nc
    ns = len(shards)
    row_widths = [D_XBC, 128, 128, D_SSM, D_SSM, D_POOL]
    rows = jnp.concatenate([cb, dt_bias, a_log, d_x, nw, pool_scale], axis=1)

    def body(p_ref, cw_ref, rows_ref, pw_ref, *rest):
        shard_refs, (y_ref, ypre_ref, pre_ref, st_ref) = rest[:ns], rest[ns:ns + 4]
        gathered_refs, (xtail, utail, state, e_ref, et_ref, tril_ref) = rest[ns + 4:2 * ns + 4], rest[2 * ns + 4:2 * ns + 10]
        gather = _Gather(shard_refs, gathered_refs, by_cols, *rest[2 * ns + 10:])
        cb_ref, dtb_ref, alog_ref, dx_ref, nw_ref, ps_ref = _row_views(rows_ref, row_widths)
        c = pl.program_id(1)
        step = pl.program_id(0) * nc + c

        @pl.when(step == 0)
        def _():
            gather.start()
            _fill_ssd_constants(e=e_ref, et_f32=et_ref, tril=tril_ref)

        @pl.when(step == n_steps // 2)
        def _():
            gather.forward()

        @pl.when(c == 0)
        def _():
            xtail[...] = jnp.zeros_like(xtail)
            utail[...] = jnp.zeros_like(utail)
            state[...] = jnp.zeros_like(state)

        valid = (c > 0) | (lax.broadcasted_iota(jnp.int32, (CHUNK, 1), 0) >= PAD_ROWS)

        u = p_ref[:, 0:D_POOL]
        inv_cnt, lane = _pool_inv_count(c)
        win = _pool_window_sums(jnp.concatenate([utail[...], u], axis=0), lane)
        utail[...] = u[CHUNK - HALO:, :]
        pooled = win * inv_cnt - u
        mixed = jnp.concatenate(
            [_dot(pooled[:, g * 128:(g + 1) * 128], pw_ref[g]) for g in range(len(POOL_WINDOWS))], axis=1)
        y_ref[:, 0:D_POOL] = _mx(mixed * ps_ref[...])

        xbc = p_ref[:, OFF_X:OFF_X + D_XBC]
        pre = _conv_pre(jnp.concatenate([xtail[...], xbc], axis=0), xbc, cw_ref[...], cb_ref[...])
        xtail[...] = xbc[CHUNK - HALO:, :]
        pre_ref[...] = pre
        xc = pre * _sigmoid(pre)
        dt, _, a_col, _, _ = _dt_and_cumsum(p_ref[:, OFF_DT:OFF_DT + 128], dtb_ref[...], alog_ref[...], valid,
                                            tril_ref[...])
        s_prev = state[...]
        st_ref[0] = s_prev
        yp, s_new = _ssd_chunk_fwd(xc[:, 0:D_SSM], xc[:, D_SSM:D_SSM + 512], xc[:, D_SSM + 512:], dt, a_col, s_prev,
                                   dx_ref[...], e_ref[...], et_ref[...])
        state[...] = s_new
        ypre_ref[...] = yp
        z = p_ref[:, OFF_Z:OFF_Z + D_SSM]
        yz = yp * (z * _sigmoid(z))
        outs = []
        for g in range(N_GROUPS):
            gs = slice(g * GROUP_CH, (g + 1) * GROUP_CH)
            r = lax.rsqrt(jnp.mean(yz[:, gs] * yz[:, gs], axis=-1, keepdims=True) + EPS)
            outs.append(yz[:, gs] * r)
        y_ref[:, D_POOL:] = _mx(jnp.concatenate(outs, axis=1) * nw_ref[...])

        @pl.when(step == n_steps - 1)
        def _():
            gather.finish()

    blk = lambda w: pl.BlockSpec((CHUNK, w), lambda b, c: (_chunk_block(b, c, nb, nc), 0))
    hbm = pl.BlockSpec(memory_space=pl.ANY)
    outs = pl.pallas_call(
        body, grid=(nb, nc), name="mixer_fwd",
        in_specs=[blk(PROJ_W), _full((4, D_XBC)), _full((1, sum(row_widths))), _full((4, 128, 128))] + [hbm] * ns,
        out_specs=[blk(D_MIX), blk(D_SSM), blk(D_XBC),
                   pl.BlockSpec((1, D_SSM, D_STATE), lambda b, c: (b * nc + c, 0, 0))] + [hbm] * ns,
        out_shape=[jax.ShapeDtypeStruct((m, D_MIX), MXU_DTYPE), jax.ShapeDtypeStruct((m, D_SSM), F32),
                   jax.ShapeDtypeStruct((m, D_XBC), F32), jax.ShapeDtypeStruct((m // CHUNK, D_SSM, D_STATE), F32)]
        + _Gather.out_shapes(shards, by_cols),
        scratch_shapes=[pltpu.VMEM((HALO, D_XBC), F32), pltpu.VMEM((HALO, D_POOL), F32),
                        pltpu.VMEM((D_SSM, D_STATE), F32)] + _ssd_constant_scratch(["e", "et_f32", "tril"])
        + _Gather.scratch(ns),
        compiler_params=_params("arbitrary", "arbitrary"),
    )(proj, cw, rows, pool_w, *shards)
    return outs[0], outs[1], outs[2], outs[3], outs[4:]


def _mixer_bwd(proj, dy, ypre, conv_pre, states, cw, dt_bias, a_log, d_x, nw, pool_w, pool_scale, nb, chip_sums):
    m = proj.shape[0]
    nc = m // nb // CHUNK
    hb = CHUNK // HALO
    ns = len(chip_sums)
    row_widths = [128, 128, D_SSM, D_SSM, D_POOL]
    rows = jnp.concatenate([dt_bias, a_log, d_x, nw, pool_scale], axis=1)
    grad_row_widths = [D_XBC, 128, 128, 128, D_SSM, D_POOL]
    constants = ["e", "et", "et_f32", "tril", "triu"]

    def body(p_ref, halo_ref, dy_ref, ypre_ref, pre_ref, st_ref, cw_ref, rows_ref, pw_ref, *rest):
        cs_refs = rest[:ns]
        dp_ref, gcw_ref, grows_ref, gpw_ref = rest[ns:ns + 4]
        part_refs, (ds_carry, dpre_next, dq_next) = rest[ns + 4:2 * ns + 4], rest[2 * ns + 4:2 * ns + 7]
        e_ref, et_ref, etf_ref, tril_ref, triu_ref = rest[2 * ns + 7:2 * ns + 12]
        exchange = _ChipExchange(cs_refs, part_refs, [False] * ns, *rest[2 * ns + 12:])
        dtb_ref, alog_ref, dx_ref, nw_ref, ps_ref = _row_views(rows_ref, row_widths)
        gcb_ref, gdtb_ref, galog_ref, gd_ref, gnw_ref, gps_ref = _row_views(grows_ref, grad_row_widths)
        b = pl.program_id(0)
        cc = pl.program_id(1)
        c = nc - 1 - cc

        @pl.when((b == 0) & (cc == 0))
        def _():
            exchange.start()
            _fill_ssd_constants(e=e_ref, et=et_ref, et_f32=etf_ref, tril=tril_ref, triu=triu_ref)
            for r in (gcw_ref, grows_ref, gpw_ref):
                r[...] = jnp.zeros_like(r)

        @pl.when(cc == 0)
        def _():
            ds_carry[...] = jnp.zeros_like(ds_carry)
            dpre_next[...] = jnp.zeros_like(dpre_next)
            dq_next[...] = jnp.zeros_like(dq_next)

        valid = (c > 0) | (lax.broadcasted_iota(jnp.int32, (CHUNK, 1), 0) >= PAD_ROWS)
        first = c > 0

        u = p_ref[:, 0:D_POOL]
        u_halo = jnp.where(first, halo_ref[...], 0.0)
        inv_cnt, lane = _pool_inv_count(c)
        pooled = _pool_window_sums(jnp.concatenate([u_halo, u], axis=0), lane) * inv_cnt - u
        dyp = dy_ref[:, 0:D_POOL]
        ps = ps_ref[...]
        dmixed = dyp * ps
        mixed, dpooled = [], []
        for g in range(len(POOL_WINDOWS)):
            gsl = slice(g * 128, (g + 1) * 128)
            pw = pw_ref[g]
            mixed.append(_dot(pooled[:, gsl], pw))
            dpooled.append(_dot_nt(dmixed[:, gsl], pw))
            gpw_ref[g] += _dot_tn(pooled[:, gsl], dmixed[:, gsl])
        gps_ref[...] += jnp.sum(dyp * jnp.concatenate(mixed, axis=1), axis=0, keepdims=True)
        dpooled = jnp.concatenate(dpooled, axis=1)
        dq = dpooled * inv_cnt
        du = _pool_window_sums_ahead(jnp.concatenate([dq, dq_next[...]], axis=0), lane) - dpooled
        dq_next[...] = dq[0:HALO, :]
        dp_ref[:, 0:D_POOL] = _mx(du)

        yp = ypre_ref[...]
        z = p_ref[:, OFF_Z:OFF_Z + D_SSM]
        sz, dsz = _silu_and_grad(z)
        yz = yp * sz
        do = dy_ref[:, D_POOL:]
        nw_row = nw_ref[...]
        dyz = []
        gnw = []
        for g in range(N_GROUPS):
            gs = slice(g * GROUP_CH, (g + 1) * GROUP_CH)
            r = lax.rsqrt(jnp.mean(yz[:, gs] * yz[:, gs], axis=-1, keepdims=True) + EPS)
            n = yz[:, gs] * r
            gnw.append(jnp.sum(do[:, gs] * n, axis=0, keepdims=True))
            dn = do[:, gs] * nw_row[:, gs]
            dyz.append(r * (dn - n * jnp.mean(dn * n, axis=-1, keepdims=True)))
        gnw_ref[...] += jnp.concatenate(gnw, axis=1)
        dyz = jnp.concatenate(dyz, axis=1)
        dp_ref[:, OFF_Z:OFF_Z + D_SSM] = _mx(dyz * yp * dsz)
        dyp_ssm = dyz * sz

        xc, dsilu = _silu_and_grad(pre_ref[...])
        dtr = p_ref[:, OFF_DT:OFF_DT + 128]
        dt, a_row, a_col, dt_pre, head = _dt_and_cumsum(dtr, dtb_ref[...], alog_ref[...], valid, tril_ref[...])
        dxs, dbm, dcm, ddt, da, dd, ds_prev = _ssd_chunk_bwd(
            xc[:, 0:D_SSM], xc[:, D_SSM:D_SSM + 512], xc[:, D_SSM + 512:], dt, a_row, a_col, st_ref[0],
            ds_carry[...], dyp_ssm, dx_ref[...], e_ref[...], et_ref[...], etf_ref[...], triu_ref[...])
        ds_carry[...] = ds_prev
        gd_ref[...] += dd
        galog_ref[...] += da * a_row
        ddtr = jnp.where(valid & head, ddt * _sigmoid(dt_pre), 0.0)
        gdtb_ref[...] += jnp.sum(ddtr, axis=0, keepdims=True)
        dp_ref[:, OFF_DT:OFF_DT + 128] = _mx(ddtr)

        dpre = jnp.concatenate([dxs, dbm, dcm], axis=1) * dsilu
        gcb_ref[...] += jnp.sum(dpre, axis=0, keepdims=True)
        dext = jnp.concatenate([dpre, dpre_next[...]], axis=0)
        dpre_next[...] = dpre[0:HALO, :]
        ups = [_shift_up(dext, 3 - k) for k in range(4)]
        xbc = p_ref[:, OFF_X:OFF_X + D_XBC]
        gcw_ref[...] += jnp.concatenate([jnp.sum(xbc * ups[k], axis=0, keepdims=True) for k in range(4)], axis=0)
        cw = cw_ref[...]
        dp_ref[:, OFF_X:OFF_X + D_XBC] = _mx(cw[3:4, :] * ups[3] + cw[2:3, :] * ups[2]
                                             + cw[1:2, :] * ups[1] + cw[0:1, :] * ups[0])

        @pl.when((b == nb - 1) & (cc == nc - 1))
        def _():
            exchange.finish()

    blk = lambda w: pl.BlockSpec((CHUNK, w), lambda b, cc: (_chunk_block(b, nc - 1 - cc, nb, nc), 0))
    halo = pl.BlockSpec((HALO, D_POOL),
                        lambda b, cc: (_chunk_block(b, jnp.maximum(nc - 2 - cc, 0), nb, nc) * hb + hb - 1, 0))
    hbm = pl.BlockSpec(memory_space=pl.ANY)
    outs = pl.pallas_call(
        body, grid=(nb, nc), name="mixer_bwd",
        in_specs=[blk(PROJ_W), halo, blk(D_MIX), blk(D_SSM), blk(D_XBC),
                  pl.BlockSpec((1, D_SSM, D_STATE), lambda b, cc: (b * nc + nc - 1 - cc, 0, 0)),
                  _full((4, D_XBC)), _full((1, sum(row_widths))), _full((4, 128, 128))] + [hbm] * ns,
        out_specs=[blk(PROJ_W), _full((4, D_XBC)), _full((1, sum(grad_row_widths))), _full((4, 128, 128))] + [hbm] * ns,
        out_shape=[jax.ShapeDtypeStruct((m, PROJ_W), MXU_DTYPE), jax.ShapeDtypeStruct((4, D_XBC), F32),
                   jax.ShapeDtypeStruct((1, sum(grad_row_widths)), F32), jax.ShapeDtypeStruct((4, 128, 128), F32)]
        + _ChipExchange.out_shapes(chip_sums, [False] * ns),
        scratch_shapes=[pltpu.VMEM((D_SSM, D_STATE), F32), pltpu.VMEM((HALO, D_XBC), F32),
                        pltpu.VMEM((HALO, D_POOL), F32)] + _ssd_constant_scratch(constants) + _ChipExchange.scratch(ns),
        compiler_params=_params("arbitrary", "arbitrary"),
    )(proj, proj, dy, ypre, conv_pre, states, cw, rows, pool_w, *chip_sums)
    dproj, g_cw, g_rows, g_pw = outs[:4]
    offs = np.cumsum([0] + grad_row_widths)
    g_cb, g_dtb, g_alog, g_d, g_nw, g_ps = (g_rows[:, a:b] for a, b in zip(offs[:-1], offs[1:]))
    return (dproj, g_cw, g_cb, g_dtb, g_alog, g_d, g_nw, g_pw, g_ps), outs[4:]


MESH_IDS = pl.DeviceIdType.MESH
_HBM = pl.BlockSpec(memory_space=pltpu.HBM)


def _coords():
    return lax.axis_index("x"), lax.axis_index("y"), lax.axis_index("c")


def _other_chips(x, y):
    return [(1 - x, y), (x, 1 - y), (1 - x, 1 - y)]


class _Gather:
    def __init__(self, ins, outs, by_cols, send_sems, recv_sems, local_sems):
        self.ins, self.outs, self.by_cols, self.n = ins, outs, by_cols, len(ins)
        self.send_sems, self.recv_sems, self.local_sems = send_sems, recv_sems, local_sems
        self.x, self.y, self.c = _coords()
        self.me, self.sibling = (self.x, self.y, self.c), (self.x, self.y, 1 - self.c)
        self.chips = _other_chips(self.x, self.y)

    @staticmethod
    def scratch(n):
        return [pltpu.SemaphoreType.DMA((7 * n,)), pltpu.SemaphoreType.DMA((7 * n,)), pltpu.SemaphoreType.DMA((n,))]

    @staticmethod
    def out_shapes(shards, by_cols):
        return [jax.ShapeDtypeStruct((s.shape[0], N_DEV * s.shape[1]) if cols else (N_DEV,) + s.shape, s.dtype)
                for s, cols in zip(shards, by_cols)]

    def _block(self, t, device):
        idx = 4 * device[0] + 2 * device[1] + device[2]
        if not self.by_cols[t]:
            return self.outs[t].at[idx]
        w = self.ins[t].shape[1]
        return self.outs[t].at[:, pl.ds(pl.multiple_of(idx * w, w), w)]

    def _copy(self, t, k, block, to, own=False):
        dst = self._block(t, block)
        return pltpu.make_async_remote_copy(
            src_ref=self.ins[t] if own else dst, dst_ref=dst, send_sem=self.send_sems.at[t * 7 + k],
            recv_sem=self.recv_sems.at[t * 7 + k], device_id=to, device_id_type=MESH_IDS)

    def _mine(self):
        return [pltpu.make_async_copy(self.ins[t], self._block(t, self.me), self.local_sems.at[t])
                for t in range(self.n)]

    def _first(self):
        cps = []
        for t in range(self.n):
            cps.append(self._copy(t, 0, self.me, self.sibling, own=True))
            cps += [self._copy(t, 1 + j, self.me, (*chip, self.c), own=True) for j, chip in enumerate(self.chips)]
        return cps

    def _passed(self):
        return [self._copy(t, 4 + j, (*chip, self.c), self.sibling)
                for j, chip in enumerate(self.chips) for t in range(self.n)]

    def start(self):
        for cp in self._mine() + self._first():
            cp.start()

    def forward(self):
        for j, chip in enumerate(self.chips):
            for t in range(self.n):
                self._copy(t, 1 + j, (*chip, self.c), self.me).wait_recv()
                self._copy(t, 4 + j, (*chip, self.c), self.sibling).start()

    def finish(self):
        for t in range(self.n):
            self._copy(t, 0, self.sibling, self.me).wait_recv()
            for j, chip in enumerate(self.chips):
                self._copy(t, 4 + j, (*chip, 1 - self.c), self.me).wait_recv()
        for cp in self._first() + self._passed():
            cp.wait_send()
        for cp in self._mine():
            cp.wait()


def _weight_gather(shards):
    n = len(shards)

    def body(*refs):
        g = _Gather(refs[:n], refs[n:2 * n], [False] * n, *refs[2 * n:])
        g.start()
        g.forward()
        g.finish()

    return pl.pallas_call(
        body, name="weight_gather",
        in_specs=[_HBM] * n, out_specs=[_HBM] * n,
        out_shape=_Gather.out_shapes(shards, [False] * n),
        scratch_shapes=_Gather.scratch(n),
    )(*shards)


def _owner_blocks(g, by_cols):
    return (g.shape[0], g.shape[1] // N_DEV) if by_cols else g.shape[2:]


def _grad_exchange_d2d(gs, by_cols, name):
    n = len(gs)

    def body(*refs):
        ins, got = refs[:n], refs[n:2 * n]
        send_sems, recv_sems = refs[2 * n:]
        x, y, c = _coords()

        def src(t, k):
            if not by_cols[t]:
                return ins[t].at[k, 1 - c]
            w = ins[t].shape[1] // N_DEV
            return ins[t].at[:, pl.ds(pl.multiple_of((2 * k + 1 - c) * w, w), w)]

        remote = [pltpu.make_async_remote_copy(
            src_ref=src(t, k), dst_ref=got[t].at[k], send_sem=send_sems.at[t * 4 + k],
            recv_sem=recv_sems.at[t * 4 + k], device_id=(x, y, 1 - c), device_id_type=MESH_IDS)
            for t in range(n) for k in range(4)]
        for cp in remote:
            cp.start()
        for cp in remote:
            cp.wait_recv()
        for cp in remote:
            cp.wait_send()

    return pl.pallas_call(
        body, name=name,
        in_specs=[_HBM] * n, out_specs=[_HBM] * n,
        out_shape=[jax.ShapeDtypeStruct((4,) + _owner_blocks(g, cols), g.dtype) for g, cols in zip(gs, by_cols)],
        scratch_shapes=[pltpu.SemaphoreType.DMA((4 * n,)), pltpu.SemaphoreType.DMA((4 * n,))],
    )(*gs)


def _small_allreduce(pack):
    rows = pack.shape[0]

    def body(p_ref, o_ref, sib_ref, parts_ref, send_sems, recv_sems):
        x, y, c = _coords()
        my_chip = 2 * x + y
        swap = pltpu.make_async_remote_copy(src_ref=p_ref, dst_ref=sib_ref, send_sem=send_sems.at[0],
                                            recv_sem=recv_sems.at[0], device_id=(x, y, 1 - c), device_id_type=MESH_IDS)
        swap.start()
        swap.wait_recv()
        parts_ref[my_chip] = p_ref[...] + sib_ref[...]
        remote = [pltpu.make_async_remote_copy(
            src_ref=parts_ref.at[my_chip], dst_ref=parts_ref.at[my_chip], send_sem=send_sems.at[1 + j],
            recv_sem=recv_sems.at[1 + j], device_id=(cx, cy, c), device_id_type=MESH_IDS)
            for j, (cx, cy) in enumerate(_other_chips(x, y))]
        for cp in remote:
            cp.start()
        for j, (cx, cy) in enumerate(_other_chips(x, y)):
            slot = parts_ref.at[2 * cx + cy]
            pltpu.make_async_remote_copy(src_ref=slot, dst_ref=slot, send_sem=send_sems.at[1 + j],
                                         recv_sem=recv_sems.at[1 + j], device_id=(cx, cy, c),
                                         device_id_type=MESH_IDS).wait_recv()
        o_ref[...] = ((parts_ref[0] + parts_ref[1]) + parts_ref[2]) + parts_ref[3]
        swap.wait_send()
        for cp in remote:
            cp.wait_send()

    vmem = pl.BlockSpec(memory_space=pltpu.VMEM)
    return pl.pallas_call(
        body, name="small_allreduce", in_specs=[vmem], out_specs=vmem,
        out_shape=jax.ShapeDtypeStruct((rows, 128), F32),
        scratch_shapes=[pltpu.VMEM((rows, 128), F32), pltpu.VMEM((4, rows, 128), F32),
                        pltpu.SemaphoreType.DMA((4,)), pltpu.SemaphoreType.DMA((4,))],
    )(pack)


class _ChipExchange:
    def __init__(self, ins, outs, whole, send_sems, recv_sems, local_sems):
        self.ins, self.outs, self.whole, self.n = ins, outs, whole, len(ins)
        self.send_sems, self.recv_sems, self.local_sems = send_sems, recv_sems, local_sems
        self.x, self.y, self.c = _coords()
        self.my_chip = 2 * self.x + self.y
        self.chips = _other_chips(self.x, self.y)

    @staticmethod
    def scratch(n):
        return [pltpu.SemaphoreType.DMA((3 * n,)), pltpu.SemaphoreType.DMA((3 * n,)), pltpu.SemaphoreType.DMA((n,))]

    def _src(self, t, k):
        return self.ins[t] if self.whole[t] else self.ins[t].at[k]

    def _local(self):
        return [pltpu.make_async_copy(self._src(t, self.my_chip), self.outs[t].at[self.my_chip], self.local_sems.at[t])
                for t in range(self.n)]

    def _remote(self):
        return [pltpu.make_async_remote_copy(
            src_ref=self._src(t, 2 * cx + cy), dst_ref=self.outs[t].at[self.my_chip],
            send_sem=self.send_sems.at[t * 3 + j], recv_sem=self.recv_sems.at[t * 3 + j],
            device_id=(cx, cy, self.c), device_id_type=MESH_IDS)
            for t in range(self.n) for j, (cx, cy) in enumerate(self.chips)]

    def start(self):
        for cp in self._remote() + self._local():
            cp.start()

    def finish(self):
        for t in range(self.n):
            for j, (cx, cy) in enumerate(self.chips):
                slot = self.outs[t].at[2 * cx + cy]
                pltpu.make_async_remote_copy(
                    src_ref=slot, dst_ref=slot, send_sem=self.send_sems.at[t * 3 + j],
                    recv_sem=self.recv_sems.at[t * 3 + j], device_id=(cx, cy, self.c),
                    device_id_type=MESH_IDS).wait_recv()
        for cp in self._remote():
            cp.wait_send()
        for cp in self._local():
            cp.wait()

    @staticmethod
    def out_shapes(arrs, whole):
        return [jax.ShapeDtypeStruct(((4,) + a.shape) if w else a.shape, a.dtype) for a, w in zip(arrs, whole)]


_SEMAPHORES = pl.BlockSpec(memory_space=pltpu.SEMAPHORE)
_SIDE_EFFECT = pltpu.SideEffectType.DATAFLOW_SIDE_EFFECTING


def _chip_exchange_start(chip_sum, name):
    def body(src_ref, land_ref, send_sems, recv_sems, src_thru, land_thru, token):
        x, y, c = _coords()
        for j, (cx, cy) in enumerate(_other_chips(x, y)):
            pltpu.make_async_remote_copy(
                src_ref=src_ref.at[2 * cx + cy], dst_ref=land_ref.at[2 * x + y], send_sem=send_sems.at[j],
                recv_sem=recv_sems.at[j], device_id=(cx, cy, c), device_id_type=MESH_IDS).start()
        token[...] = jnp.zeros_like(token)

    zone = pltpu.HBM(chip_sum.shape, chip_sum.dtype)
    return pl.pallas_call(
        body, name=name,
        out_shape=(pltpu.SemaphoreType.DMA((3,)), pltpu.SemaphoreType.DMA((3,)), zone, zone,
                   jax.ShapeDtypeStruct((8, 128), F32)),
        in_specs=(_HBM, _HBM), out_specs=(_SEMAPHORES, _SEMAPHORES, _HBM, _HBM, pl.BlockSpec(memory_space=pltpu.VMEM)),
        input_output_aliases={0: 2, 1: 3},
        compiler_params=pltpu.CompilerParams(has_side_effects=_SIDE_EFFECT),
    )(pltpu.with_memory_space_constraint(chip_sum, pltpu.HBM),
      pltpu.with_memory_space_constraint(lax.empty(chip_sum.shape, chip_sum.dtype), pltpu.HBM))


def _chip_exchange_wait(send_sems, recv_sems, src_thru, land_thru, after, name):
    def body(src_ref, land_ref, send_sems, recv_sems, after_ref, src_out, land_out):
        x, y, c = _coords()
        for j, (cx, cy) in enumerate(_other_chips(x, y)):
            cp = pltpu.make_async_remote_copy(
                src_ref=src_ref.at[2 * cx + cy], dst_ref=land_ref.at[2 * cx + cy], send_sem=send_sems.at[j],
                recv_sem=recv_sems.at[j], device_id=(cx, cy, c), device_id_type=MESH_IDS)
            cp.wait_send()
            cp.wait_recv()

    zone = pltpu.HBM(src_thru.shape, src_thru.dtype)
    return pl.pallas_call(
        body, name=name, out_shape=(zone, zone),
        in_specs=(_HBM, _HBM, _SEMAPHORES, _SEMAPHORES, pl.BlockSpec(memory_space=pl.ANY)), out_specs=(_HBM, _HBM),
        input_output_aliases={0: 0, 1: 1},
        compiler_params=pltpu.CompilerParams(has_side_effects=_SIDE_EFFECT),
    )(src_thru, land_thru, send_sems, recv_sems, after)


def _row_tile(rows, cols, n_arrays):
    budget = 24 * 1024 * 1024
    padded = -(-cols // 128) * 128
    step = 16 if rows % 16 == 0 else 8
    tr = max(step, budget // (n_arrays * 2 * 4 * padded) // step * step)
    while rows % tr:
        tr -= step
    return tr


def _chip_sum(g, by_cols, got, core, name):
    rows, cols = _owner_blocks(g, by_cols)
    tr = _row_tile(rows, cols, 3)

    def body(c_ref, a_ref, b_ref, o_ref):
        o_ref[...] = (a_ref[...] + b_ref[...].astype(F32)).astype(o_ref.dtype)

    own = (pl.BlockSpec((tr, cols), lambda k, i, c: (i, 2 * k + c[0])) if by_cols
           else pl.BlockSpec((None, None, tr, cols), lambda k, i, c: (k, c[0], i, 0)))
    grid_spec = pltpu.PrefetchScalarGridSpec(
        num_scalar_prefetch=1, grid=(4, rows // tr),
        in_specs=[own, pl.BlockSpec((None, tr, cols), lambda k, i, c: (k, i, 0))],
        out_specs=pl.BlockSpec((None, tr, cols), lambda k, i, c: (k, i, 0)))
    return pl.pallas_call(body, grid_spec=grid_spec, name=name,
                          out_shape=jax.ShapeDtypeStruct((4, rows, cols), MXU_DTYPE),
                          compiler_params=_params("arbitrary", "arbitrary"))(core, g, got)


def _adamw_math(w, g, m, v):
    m2 = ADAM_B1 * m + (1.0 - ADAM_B1) * g
    v2 = ADAM_B2 * v + (1.0 - ADAM_B2) * (g * g)
    m_hat = m2 / (1.0 - ADAM_B1 ** ADAM_STEP)
    v_hat = v2 / (1.0 - ADAM_B2 ** ADAM_STEP)
    delta = -ADAM_LR * (m_hat / (jnp.sqrt(v_hat) + ADAM_EPS) + ADAM_WD * w)
    return delta, m2, v2


def _adamw(parts, w, m, v, name, own=None, chip=None):
    rows, cols = w.shape
    tr = _row_tile(rows, cols, 11 if own is None else 15)

    def body(*refs):
        if own is None:
            p_ref, w_ref, m_ref, v_ref, g_ref, d_ref, m2_ref, v2_ref = refs
            part = lambda k: p_ref[k].astype(F32)
        else:
            chip_ref, p_ref, own_ref, w_ref, m_ref, v_ref, g_ref, d_ref, m2_ref, v2_ref = refs
            part = lambda k: jnp.where(chip_ref[0] == k, own_ref[k], p_ref[k]).astype(F32)
        g = ((part(0) + part(1)) + part(2)) + part(3)
        d, m2, v2 = _adamw_math(w_ref[...], g, m_ref[...], v_ref[...])
        g_ref[...] = g
        d_ref[...] = d
        m2_ref[...] = m2
        v2_ref[...] = v2

    blk = pl.BlockSpec((tr, cols), lambda i, *_: (i, 0))
    pblk = pl.BlockSpec((4, tr, cols), lambda i, *_: (0, i, 0))
    out = jax.ShapeDtypeStruct((rows, cols), F32)
    if own is None:
        return pl.pallas_call(body, grid=(rows // tr,), name=name, in_specs=[pblk, blk, blk, blk],
                              out_specs=[blk] * 4, out_shape=[out] * 4,
                              compiler_params=_params("arbitrary"))(parts, w, m, v)
    grid_spec = pltpu.PrefetchScalarGridSpec(num_scalar_prefetch=1, grid=(rows // tr,),
                                             in_specs=[pblk, pblk, blk, blk, blk], out_specs=[blk] * 4)
    return pl.pallas_call(body, grid_spec=grid_spec, name=name, out_shape=[out] * 4,
                          compiler_params=_params("arbitrary"))(chip, parts, own, w, m, v)


def _adamw_small(gs, ws, ms, vs):
    n = len(ws)

    def body(*refs):
        g_refs, w_refs, m_refs, v_refs = (refs[k * n:(k + 1) * n] for k in range(4))
        d_refs, m2_refs, v2_refs = (refs[(4 + k) * n:(5 + k) * n] for k in range(3))
        for t in range(n):
            d, m2, v2 = _adamw_math(w_refs[t][...], g_refs[t][...], m_refs[t][...], v_refs[t][...])
            d_refs[t][...] = d
            m2_refs[t][...] = m2
            v2_refs[t][...] = v2

    outs = pl.pallas_call(body, name="adamw_small",
                          out_shape=[jax.ShapeDtypeStruct(w.shape, F32) for w in ws] * 3)(*gs, *ws, *ms, *vs)
    return outs[:n], outs[n:2 * n], outs[2 * n:]


_PACK_TILE = 8 * 128


def _pack(arrays):
    rows = []
    for a in arrays:
        flat = a.astype(F32).reshape(-1)
        rows.append(jnp.pad(flat, (0, -flat.shape[0] % _PACK_TILE)).reshape(-1, 128))
    return jnp.concatenate(rows, axis=0)


def _unpack(pack, shapes):
    out, r = [], 0
    for s in shapes:
        n = int(np.prod(s))
        out.append(pack[r:r + -(-n // 128)].reshape(-1)[:n].reshape(s))
        r += -(-n // _PACK_TILE) * 8
    return out


def _pad128(v):
    v = v.reshape(1, -1).astype(F32)
    return jnp.pad(v, ((0, 0), (0, 128 - v.shape[1])))


_WEIGHTS = ["meta", "norm_mix_w", "w_in", "pool_w", "pool_scale", "conv_w", "conv_b", "dt_bias", "a_log", "d_skip",
            "ssm_norm_w", "w_out", "norm_ffn_w", "w_ff1", "w_ff2", "norm_f_w"]
_BIG = ["w_in", "w_out", "w_ff1", "w_ff2"]
_SMALL = [n for n in _WEIGHTS if n not in _BIG]


def kernel(x, meta, norm_mix_w, w_in, pool_w, pool_scale, conv_w, conv_b, dt_bias, a_log, d_skip, ssm_norm_w, w_out, norm_ffn_w, w_ff1, w_ff2, norm_f_w, loss_target, m_meta, m_norm_mix_w, m_w_in, m_pool_w, m_pool_scale, m_conv_w, m_conv_b, m_dt_bias, m_a_log, m_d_skip, m_ssm_norm_w, m_w_out, m_norm_ffn_w, m_w_ff1, m_w_ff2, m_norm_f_w, v_meta, v_norm_mix_w, v_w_in, v_pool_w, v_pool_scale, v_conv_w, v_conv_b, v_dt_bias, v_a_log, v_d_skip, v_ssm_norm_w, v_w_out, v_norm_ffn_w, v_w_ff1, v_w_ff2, v_norm_f_w):
    wts = dict(meta=meta, norm_mix_w=norm_mix_w, w_in=w_in, pool_w=pool_w, pool_scale=pool_scale, conv_w=conv_w,
               conv_b=conv_b, dt_bias=dt_bias, a_log=a_log, d_skip=d_skip, ssm_norm_w=ssm_norm_w, w_out=w_out,
               norm_ffn_w=norm_ffn_w, w_ff1=w_ff1, w_ff2=w_ff2, norm_f_w=norm_f_w)
    mom1 = dict(zip(_WEIGHTS, (m_meta, m_norm_mix_w, m_w_in, m_pool_w, m_pool_scale, m_conv_w, m_conv_b, m_dt_bias,
                               m_a_log, m_d_skip, m_ssm_norm_w, m_w_out, m_norm_ffn_w, m_w_ff1, m_w_ff2, m_norm_f_w)))
    mom2 = dict(zip(_WEIGHTS, (v_meta, v_norm_mix_w, v_w_in, v_pool_w, v_pool_scale, v_conv_w, v_conv_b, v_dt_bias,
                               v_a_log, v_d_skip, v_ssm_norm_w, v_w_out, v_norm_ffn_w, v_w_ff1, v_w_ff2, v_norm_f_w)))
    xi, yi, ci = _coords()
    dev = 4 * xi + 2 * yi + ci
    win_cols = w_in.shape[-1]
    cw_cols = conv_w.shape[-1]

    nb, seq, _ = x.shape
    core = jnp.reshape(ci, (1,)).astype(jnp.int32)
    owners = lambda a: a.reshape((4, 2) + a.shape[1:])

    lead_pack = jnp.zeros((N_META, 512), F32)
    lead_pack = lead_pack.at[:, :128].set(meta).at[:4, 128:128 + cw_cols].set(conv_w[0])
    g_win, g_lead = _weight_gather([_mx(w_in[0]), lead_pack])
    win_full = _assemble_bands(g_win, PROJ_W)
    meta_full = jnp.transpose(g_lead[:, :, :128], (1, 0, 2)).reshape(N_META, D_MODEL)
    cw_full = jnp.transpose(g_lead[:, :4, 128:128 + cw_cols], (1, 0, 2)).reshape(4, D_XBC)

    lead = jnp.concatenate([jnp.zeros((PAD_ROWS, D_MODEL), F32), meta_full] * nb, axis=0)
    x_rows = x.reshape(nb * seq, D_MODEL)
    tgt_rows = loss_target.reshape(nb * seq, D_MODEL)
    dt_bias_p, a_log_p = _pad128(dt_bias), _pad128(a_log)
    d_x = jnp.repeat(d_skip.reshape(1, N_HEADS).astype(F32), HEAD_DIM, axis=1)
    norm_f_row = norm_f_w.reshape(1, D_MODEL)

    hn1, proj = _in_proj(x_rows, lead, norm_mix_w, win_full)
    late_cols = [False, True, False]
    y, ypre, conv_pre, states, (g_wout, wff1_full, g_wff2) = _mixer_fwd(
        proj, cw_full, conv_b, dt_bias_p, a_log_p, d_x, ssm_norm_w, pool_w[0], pool_scale, nb,
        [_mx(w_out[0]), _mx(w_ff1[0]), _mx(w_ff2[0])], late_cols)
    wout_full = g_wout.reshape(D_MIX, D_MODEL)
    wff2_full = g_wff2.reshape(D_FF, D_MODEL)
    loss, gr_nf, gr_nffn, ff, da, hn2, dh1, dh2, dy = _ffn_fwd_bwd(
        x_rows, lead, y, tgt_rows, wout_full, norm_ffn_w, wff1_full, wff2_full, norm_f_row)
    gr_wff2 = _tn_matmul(ff, dh2, "grad_w_ff2", tka=1024, max_slab=2816)
    gr_wff1 = _tn_matmul(hn2, da, "grad_w_ff1", tka=512)
    gr_wout = _tn_matmul(y, dh1, "grad_w_out", tka=1024, max_slab=2816)

    by_owner = lambda k: [owners(gr_wout[k].reshape(N_DEV, D_MIX // N_DEV, D_MODEL)), gr_wff1[k],
                          owners(gr_wff2[k].reshape(N_DEV, D_FF // N_DEV, D_MODEL))]
    late_parts = by_owner(0)
    late_got = _grad_exchange_d2d(by_owner(1), late_cols, "grad_exchange_d2d_late")
    late_sums = [_chip_sum(late_parts[t], late_cols[t], late_got[t], core, "chip_sum_late_%d" % t) for t in range(3)]
    (dproj, gr_cw, gr_cb, gr_dtb, gr_alog, gr_d, gr_nw, gr_pw, gr_ps), late_exchanged = _mixer_bwd(
        proj, dy, ypre, conv_pre, states, cw_full, dt_bias_p, a_log_p, d_x, ssm_norm_w, pool_w[0], pool_scale, nb,
        late_sums)

    win_parts = [owners(_tn_matmul_banded(hn1, dproj, "grad_w_in", win_cols, tka=512))]
    win_got = _grad_exchange_d2d(win_parts, [False], "grad_exchange_d2d_w_in")
    win_sum = _chip_sum(win_parts[0], False, win_got[0], core, "chip_sum_w_in")
    send_sems, recv_sems, win_sum, win_landing, started = _chip_exchange_start(win_sum, "w_in_exchange_start")
    gx_rows, gr_nmix, gr_meta = _in_proj_bwd(dproj, x_rows, lead, dh1, norm_mix_w, win_full, started)
    win_sum, win_landed = _chip_exchange_wait(send_sems, recv_sems, win_sum, win_landing, gr_nmix, "w_in_exchange_wait")
    parts = dict(w_in=win_landed, w_out=late_exchanged[0], w_ff1=late_exchanged[1], w_ff2=late_exchanged[2])
    my_chip = jnp.reshape(2 * xi + yi, (1,)).astype(jnp.int32)

    small_full = dict(meta=gr_meta, norm_mix_w=gr_nmix, pool_w=gr_pw, pool_scale=gr_ps, conv_w=gr_cw, conv_b=gr_cb,
                      dt_bias=gr_dtb[:, :N_HEADS], a_log=gr_alog[:, :N_HEADS], d_skip=gr_d[:, :N_HEADS],
                      ssm_norm_w=gr_nw, norm_ffn_w=gr_nffn, norm_f_w=gr_nf, loss=loss[0:1, 0:1])
    small_names = list(small_full)
    small_sum = _small_allreduce(_pack([small_full[n] for n in small_names]))
    gs = dict(zip(small_names, _unpack(small_sum, [small_full[n].shape for n in small_names])))
    gs["meta"] = lax.dynamic_slice_in_dim(gs["meta"], dev * 128, 128, axis=1)
    gs["conv_w"] = lax.dynamic_slice_in_dim(gs["conv_w"], dev * cw_cols, cw_cols, axis=1)

    res = {}
    for n in _BIG:
        shp = wts[n].shape
        own = dict(own=win_sum, chip=my_chip) if n == "w_in" else {}
        res[n] = [o.reshape(shp) for o in _adamw(parts[n], wts[n][0], mom1[n][0], mom2[n][0], "adamw_" + n, **own)]
    as2d = lambda a: a.reshape(-1, a.shape[-1])
    small_g = [as2d(gs[n].reshape(wts[n].shape)) for n in _SMALL]
    small_out = _adamw_small(small_g, *[[as2d(d[n]) for n in _SMALL] for d in (wts, mom1, mom2)])
    for k, n in enumerate(_SMALL):
        res[n] = [o[k].reshape(wts[n].shape) for o in (small_g,) + tuple(small_out)]

    grad_x = gx_rows.reshape(nb, seq, D_MODEL)
    return (gs["loss"][0, 0], grad_x, *[res[n][0] for n in _WEIGHTS], *[res[n][1] for n in _WEIGHTS],
            *[res[n][2] for n in _WEIGHTS], *[res[n][3] for n in _WEIGHTS])
```

```python
import functools

import numpy as np
import jax
import jax.numpy as jnp
from jax import lax
from jax.experimental import pallas as pl
from jax.experimental.pallas import tpu as pltpu

F32 = jnp.float32
MXU_DTYPE = jnp.bfloat16

D_MODEL = 1024
D_POOL = 512
D_SSM = 1536
D_XBC = 2560
N_HEADS = 24
HEAD_DIM = 64
N_GROUPS = 4
GROUP_CH = D_SSM // N_GROUPS
D_STATE = 128
CHUNK = 128
N_META = 16
LEAD = CHUNK
PAD_ROWS = LEAD - N_META
ROW_TILE = 2 * CHUNK
D_MIX = D_POOL + D_SSM
D_FF = 4096
PROJ_W = 4736
OFF_Z = D_POOL
OFF_X = D_POOL + D_SSM
OFF_DT = OFF_X + D_XBC
D_IN_PROJ = OFF_DT + N_HEADS
POOL_WINDOWS = (2, 4, 8, 16)
HALO = 16
EPS = 1e-5
N_DEV = 8

ADAM_LR, ADAM_B1, ADAM_B2, ADAM_EPS, ADAM_WD, ADAM_STEP = 0.001, 0.9, 0.999, 1e-08, 0.01, 10

VMEM_LIMIT = 60 * 1024 * 1024


def _mx(a):
    return a.astype(MXU_DTYPE)


def _dot(a, b):
    return jnp.dot(_mx(a), _mx(b), preferred_element_type=F32)


def _dot_nt(a, b):
    return lax.dot_general(_mx(a), _mx(b), (((1,), (1,)), ((), ())), preferred_element_type=F32)


def _dot_tn(a, b):
    return lax.dot_general(_mx(a), _mx(b), (((0,), (0,)), ((), ())), preferred_element_type=F32)


def _split3(x):
    hi = x.astype(MXU_DTYPE)
    r = x - hi.astype(F32)
    mid = r.astype(MXU_DTYPE)
    lo = (r - mid.astype(F32)).astype(MXU_DTYPE)
    return hi, mid, lo


def _exact_l(c, x):
    hi, mid, lo = _split3(x)
    f = lambda p: jnp.dot(c, p, preferred_element_type=F32)
    return f(hi) + f(mid) + f(lo)


def _exact_r(x, c):
    hi, mid, lo = x if isinstance(x, tuple) else _split3(x)
    f = lambda p: jnp.dot(p, c, preferred_element_type=F32)
    return f(hi) + f(mid) + f(lo)


def _contract(x, c):
    hi = x.astype(MXU_DTYPE)
    lo = (x - hi.astype(F32)).astype(MXU_DTYPE)
    return jnp.dot(hi, c, preferred_element_type=F32) + jnp.dot(lo, c, preferred_element_type=F32)


def _sigmoid(x):
    return jax.nn.sigmoid(x)


def _softplus(x):
    return jnp.maximum(x, 0.0) + jnp.log1p(jnp.exp(-jnp.abs(x)))


def _silu_and_grad(x):
    s = _sigmoid(x)
    y = x * s
    return y, s + y * (1.0 - s)


def _shift_up(ext, s):
    if s == 0:
        return ext[:CHUNK, :]
    return pltpu.roll(ext, ext.shape[0] - s, 0)[:CHUNK, :]


def _by_pool_group(lane, a2, a4, a8, a16):
    return jnp.where(lane < 128, a2, jnp.where(lane < 256, a4, jnp.where(lane < 384, a8, a16)))


def _pool_inv_count(chunk_idx):
    row = lax.broadcasted_iota(jnp.int32, (CHUNK, D_POOL), 0)
    lane = lax.broadcasted_iota(jnp.int32, (CHUNK, D_POOL), 1)
    pos1 = jnp.maximum(chunk_idx * CHUNK + row - (PAD_ROWS - 1), 1)
    w = _by_pool_group(lane, 2, 4, 8, 16)
    return 1.0 / jnp.minimum(pos1, w).astype(F32), lane


def _pool_window_sums(u_ext, lane):
    s2 = u_ext + pltpu.roll(u_ext, 1, 0)
    s4 = s2 + pltpu.roll(s2, 2, 0)
    s8 = s4 + pltpu.roll(s4, 4, 0)
    s16 = s8 + pltpu.roll(s8, 8, 0)
    return _by_pool_group(lane, s2[HALO:], s4[HALO:], s8[HALO:], s16[HALO:])


def _pool_window_sums_ahead(q_ext, lane):
    n = q_ext.shape[0]
    r2 = q_ext + pltpu.roll(q_ext, n - 1, 0)
    r4 = r2 + pltpu.roll(r2, n - 2, 0)
    r8 = r4 + pltpu.roll(r4, n - 4, 0)
    r16 = r8 + pltpu.roll(r8, n - 8, 0)
    return _by_pool_group(lane, r2[:CHUNK], r4[:CHUNK], r8[:CHUNK], r16[:CHUNK])


def _conv_pre(ext, xbc, cw, cb):
    s1 = pltpu.roll(ext, 1, 0)
    near = cw[3:4, :] * xbc + cw[2:3, :] * s1[HALO:, :]
    far = cw[1:2, :] * ext + cw[0:1, :] * s1
    return cb + near + pltpu.roll(far, 2, 0)[HALO:, :]


def _dt_and_cumsum(dtr, dt_bias, a_log, valid, tril):
    lane = lax.broadcasted_iota(jnp.int32, (CHUNK, 128), 1)
    head = lane < N_HEADS
    pre = dtr + dt_bias
    dt = jnp.where(valid & head, _softplus(pre), 0.0)
    a_row = jnp.where(head[0:1, :], -jnp.exp(a_log), 0.0)
    a_col = _exact_l(tril, dt * a_row)
    return dt, a_row, a_col, pre, head


def _decay(a_col, a_row_t, h, causal):
    seg = a_col[:, h:h + 1] - a_row_t[h:h + 1, :]
    return jnp.where(causal, jnp.exp(jnp.minimum(seg, 0.0)), 0.0)


def _ssd_chunk_fwd(xs, bm, cm, dt, a_col, s_prev, d_x, e_mat, et_f32):
    lane = lax.broadcasted_iota(jnp.int32, (CHUNK, 128), 1)
    rowi = lax.broadcasted_iota(jnp.int32, (CHUNK, CHUNK), 0)
    coli = lax.broadcasted_iota(jnp.int32, (CHUNK, CHUNK), 1)
    causal = rowi >= coli
    a_row_t = a_col.T
    ax = _exact_r(a_col, e_mat)
    dtx = _exact_r(dt, e_mat)
    xdt = xs * dtx
    ax_last = ax[CHUNK - 1:CHUNK, :]
    e_a = jnp.exp(ax)
    w_end = xdt * jnp.exp(ax_last - ax)
    cd_col = jnp.exp(jnp.sum(et_f32 * a_col[CHUNK - 1:CHUNK, :], axis=1, keepdims=True))
    ys, s_new = [], []
    for g in range(N_GROUPS):
        gs = slice(g * GROUP_CH, (g + 1) * GROUP_CH)
        bg = bm[:, g * D_STATE:(g + 1) * D_STATE]
        cg = cm[:, g * D_STATE:(g + 1) * D_STATE]
        sg = s_prev[gs, :]
        cb = _dot_nt(cg, bg)
        y_off = _dot_nt(cg, sg) * e_a[:, gs]
        s_new.append(sg * cd_col[gs, :] + _dot_tn(w_end[:, gs], bg))
        for pr in range(3):
            c0 = g * GROUP_CH + pr * 128
            xdt_p = xdt[:, c0:c0 + 128]
            h0 = g * 6 + pr * 2
            y0 = _dot(cb * _decay(a_col, a_row_t, h0, causal), xdt_p)
            y1 = _dot(cb * _decay(a_col, a_row_t, h0 + 1, causal), xdt_p)
            ys.append(jnp.where(lane < HEAD_DIM, y0, y1) + y_off[:, pr * 128:(pr + 1) * 128])
    y = jnp.concatenate(ys, axis=1) + d_x * xs
    return y, jnp.concatenate(s_new, axis=0)


def _ssd_chunk_bwd(xs, bm, cm, dt, a_row, a_col, s_prev, ds_new, dy, d_x, e_mat, et_mat, et_f32, triu):
    lane = lax.broadcasted_iota(jnp.int32, (CHUNK, 128), 1)
    sub = lax.broadcasted_iota(jnp.int32, (CHUNK, 128), 0)
    rowi = lax.broadcasted_iota(jnp.int32, (CHUNK, CHUNK), 0)
    coli = lax.broadcasted_iota(jnp.int32, (CHUNK, CHUNK), 1)
    causal = rowi >= coli
    a_row_t = a_col.T
    a_last = a_col[CHUNK - 1:CHUNK, :]
    a_split, dt_split = _split3(a_col), _split3(dt)
    sub8 = lax.broadcasted_iota(jnp.int32, (8, GROUP_CH), 0)

    dxs, dbs, dcs, dsp = [], [], [], []
    zcol = jnp.zeros((CHUNK, 128), F32)
    zrows = []
    da_col = jnp.zeros((CHUNK, 128), F32)
    ddt = jnp.zeros((CHUNK, 128), F32)
    head_sums = jnp.zeros((8, 128), F32)
    q_row = jnp.zeros((1, 128), F32)
    for g in range(N_GROUPS):
        gs = slice(g * GROUP_CH, (g + 1) * GROUP_CH)
        e_g, et_g = e_mat[:, gs], et_mat[gs, :]
        xs_g, dy_g = xs[:, gs], dy[:, gs]
        ax = _exact_r(a_split, e_g)
        dtx = _exact_r(dt_split, e_g)
        xdt = xs_g * dtx
        dte = jnp.exp(ax[CHUNK - 1:CHUNK, :] - ax)
        w_end = xdt * dte
        cd_col = jnp.exp(jnp.sum(et_f32[gs, :] * a_last, axis=1, keepdims=True))
        dye = dy_g * jnp.exp(ax)
        bg = bm[:, g * D_STATE:(g + 1) * D_STATE]
        cg = cm[:, g * D_STATE:(g + 1) * D_STATE]
        sg = s_prev[gs, :]
        dsg = ds_new[gs, :]
        cb = _dot_nt(cg, bg)
        cs = _dot_nt(cg, sg)
        dcg = _dot(dye, sg)
        dsp.append(dsg * cd_col + _dot_tn(dye, cg))
        dwg = _dot_nt(bg, dsg)
        dbg = _dot(w_end, dsg)
        ww = dwg * w_end
        t1 = jnp.sum(dsg * sg, axis=1, keepdims=True) * cd_col
        dcb = jnp.zeros((CHUNK, CHUNK), F32)
        pairs = []
        for pr in range(3):
            ps = slice(pr * 128, (pr + 1) * 128)
            xdt_p, dy_p = xdt[:, ps], dy_g[:, ps]
            acc = None
            for half in range(2):
                h = g * 6 + pr * 2 + half
                ld = _decay(a_col, a_row_t, h, causal)
                gm = cb * ld
                dym = jnp.where((lane < HEAD_DIM) if half == 0 else (lane >= HEAD_DIM), dy_p, 0.0)
                dg = _dot_nt(dym, xdt_p)
                dseg = dg * gm
                dcb = dcb + dg * ld
                t = _dot_tn(gm, dym)
                acc = t if acc is None else acc + t
                zcol = jnp.where(lane == h, jnp.sum(dseg, axis=1, keepdims=True), zcol)
                zrows.append(jnp.sum(dseg, axis=0, keepdims=True))
            pairs.append(acc)
        dxdt = dwg * dte + jnp.concatenate(pairs, axis=1)
        dcs.append(dcg + _dot(dcb, bg))
        dbs.append(dbg + _dot_tn(dcb, cg))
        dxs.append(dxdt * dtx + d_x[:, gs] * dy_g)
        da_col = da_col + _contract(dye * cs - ww, et_g)
        ddt = ddt + _contract(dxdt * xs_g, et_g)
        col_sums = jnp.where(sub8 == 0, jnp.sum(dy_g * xs_g, axis=0, keepdims=True),
                             jnp.where(sub8 == 1, jnp.sum(ww, axis=0, keepdims=True), 0.0))
        head_sums = head_sums + _exact_r(col_sums, et_g)
        q_row = q_row + jnp.sum(et_f32[gs, :] * t1, axis=0, keepdims=True)

    dd = head_sums[0:1, :]
    q_row = q_row + head_sums[1:2, :]
    zrow = jnp.concatenate(zrows + [jnp.zeros((128 - N_HEADS, CHUNK), F32)], axis=0)
    da_col = da_col + zcol - zrow.T + jnp.where(sub == CHUNK - 1, q_row, 0.0)
    rc = _exact_l(triu, da_col)
    ddt = ddt + rc * a_row
    da = jnp.sum(rc * dt, axis=0, keepdims=True)
    return (jnp.concatenate(dxs, axis=1), jnp.concatenate(dbs, axis=1), jnp.concatenate(dcs, axis=1), ddt, da, dd,
            jnp.concatenate(dsp, axis=0))


_SSD_CONSTANT_SHAPES = dict(e=((128, D_SSM), MXU_DTYPE), et=((D_SSM, 128), MXU_DTYPE), et_f32=((D_SSM, 128), F32),
                            tril=((CHUNK, CHUNK), MXU_DTYPE), triu=((CHUNK, CHUNK), MXU_DTYPE))


def _ssd_constant_scratch(names):
    return [pltpu.VMEM(*_SSD_CONSTANT_SHAPES[n]) for n in names]


def _fill_ssd_constants(**refs):
    iota = lambda shape, d: lax.broadcasted_iota(jnp.int32, shape, d)
    shift = HEAD_DIM.bit_length() - 1
    marks = dict(
        e=lambda: iota((128, D_SSM), 0) == (iota((128, D_SSM), 1) >> shift),
        et=lambda: iota((D_SSM, 128), 1) == (iota((D_SSM, 128), 0) >> shift),
        et_f32=lambda: iota((D_SSM, 128), 1) == (iota((D_SSM, 128), 0) >> shift),
        tril=lambda: iota((CHUNK, CHUNK), 1) <= iota((CHUNK, CHUNK), 0),
        triu=lambda: iota((CHUNK, CHUNK), 1) >= iota((CHUNK, CHUNK), 0))
    for name, ref in refs.items():
        ref[...] = jnp.where(marks[name](), 1.0, 0.0).astype(ref.dtype)


def _row_views(ref, widths):
    views, off = [], 0
    for w in widths:
        views.append(ref.at[:, off:off + w])
        off += w
    return views


def _full(shape):
    nd = len(shape)
    return pl.BlockSpec(shape, lambda *_: (0,) * nd)


def _params(*sem):
    return pltpu.CompilerParams(dimension_semantics=sem, vmem_limit_bytes=VMEM_LIMIT)


def _token_tiles(width, n_tok_tiles):
    return pl.BlockSpec((ROW_TILE, width), lambda i: (jnp.minimum(i, n_tok_tiles - 1), 0))


def _in_proj(x, lead, w1, win):
    nt = x.shape[0] // ROW_TILE
    m = x.shape[0] + ROW_TILE
    tm = ROW_TILE

    def body(x_ref, lead_ref, w1_ref, win_hbm, hn_ref, proj_ref, win_v, sem):
        i = pl.program_id(0)

        @pl.when(i == 0)
        def _():
            cp = pltpu.make_async_copy(win_hbm, win_v, sem)
            cp.start()
            cp.wait()

        x = jnp.where(i == nt, lead_ref[...], x_ref[...])
        r = lax.rsqrt(jnp.mean(x * x, axis=-1, keepdims=True) + EPS)
        hn = _mx(x * r * w1_ref[...])
        hn_ref[...] = hn
        for j in range(0, PROJ_W, 512):
            w = min(512, PROJ_W - j)
            proj_ref[:, j:j + w] = jnp.dot(hn, win_v[:, j:j + w], preferred_element_type=F32)

    return pl.pallas_call(
        body, grid=(m // tm,), name="in_proj",
        in_specs=[_token_tiles(D_MODEL, nt), _full((ROW_TILE, D_MODEL)), _full((1, D_MODEL)),
                  pl.BlockSpec(memory_space=pl.ANY)],
        out_specs=[pl.BlockSpec((tm, D_MODEL), lambda i: (i, 0)), pl.BlockSpec((tm, PROJ_W), lambda i: (i, 0))],
        out_shape=[jax.ShapeDtypeStruct((m, D_MODEL), MXU_DTYPE), jax.ShapeDtypeStruct((m, PROJ_W), F32)],
        scratch_shapes=[pltpu.VMEM((D_MODEL, PROJ_W), MXU_DTYPE), pltpu.SemaphoreType.DMA],
        compiler_params=_params("arbitrary"),
    )(x, lead, w1, win)


def _ffn_fwd_bwd(x, lead, y, tgt, wout, w2n, wff1, wff2, wfn):
    nt = x.shape[0] // ROW_TILE
    m = x.shape[0] + ROW_TILE
    tm = ROW_TILE
    nj = D_FF // 1024

    def body(x_ref, lead_ref, y_ref, tgt_ref, w2n_ref, wfn_ref, wout_hbm, wff1_hbm, wff2_hbm,
             loss_ref, gwf_ref, gw2_ref, ff_ref, da_ref, hn2_ref, dh1_ref, dh2_ref, dy_ref,
             wout_v, wff1_v, wff2_v, a_s, sems):
        i = pl.program_id(0)
        hp = jnp.where(i == nt, lead_ref[...], x_ref[...])

        @pl.when(i == 0)
        def _():
            cps = [pltpu.make_async_copy(s, d, sems.at[k])
                   for k, (s, d) in enumerate(((wout_hbm, wout_v), (wff1_hbm, wff1_v), (wff2_hbm, wff2_v)))]
            for cp in cps:
                cp.start()
            for cp in cps:
                cp.wait()
            loss_ref[...] = jnp.zeros_like(loss_ref)
            gwf_ref[...] = jnp.zeros_like(gwf_ref)
            gw2_ref[...] = jnp.zeros_like(gw2_ref)

        h1 = hp + jnp.dot(y_ref[...], wout_v[...], preferred_element_type=F32)
        r2 = lax.rsqrt(jnp.mean(h1 * h1, axis=-1, keepdims=True) + EPS)
        n2 = h1 * r2
        w2n_row = w2n_ref[...]
        hn2 = _mx(n2 * w2n_row)
        hn2_ref[...] = hn2
        h2 = h1
        for j in range(nj):
            js = slice(j * 1024, (j + 1) * 1024)
            a = jnp.dot(hn2, wff1_v[:, js], preferred_element_type=F32)
            a_s[:, js] = a
            ra = jnp.maximum(a, 0.0)
            ff = _mx(ra * ra)
            ff_ref[:, js] = ff
            h2 = h2 + jnp.dot(ff, wff2_v[js, :], preferred_element_type=F32)

        r3 = lax.rsqrt(jnp.mean(h2 * h2, axis=-1, keepdims=True) + EPS)
        n3 = h2 * r3
        wf_row = wfn_ref[...]
        err = n3 * wf_row - tgt_ref[...]
        tokf = (i < nt).astype(F32)
        loss_ref[...] += 0.5 * jnp.sum(jnp.mean(err * err, axis=-1, keepdims=True) * tokf)
        dout = err * (tokf / D_MODEL)
        gwf_ref[...] += jnp.sum(dout * n3, axis=0, keepdims=True)
        dn3 = dout * wf_row
        dh2 = r3 * (dn3 - n3 * jnp.mean(dn3 * n3, axis=-1, keepdims=True))
        dh2m = _mx(dh2)
        dh2_ref[...] = dh2m

        dhn2 = jnp.zeros((tm, D_MODEL), F32)
        for j in range(nj):
            js = slice(j * 1024, (j + 1) * 1024)
            dff = lax.dot_general(dh2m, wff2_v[js, :], (((1,), (1,)), ((), ())), preferred_element_type=F32)
            da = _mx(dff * (2.0 * jnp.maximum(a_s[:, js], 0.0)))
            da_ref[:, js] = da
            dhn2 = dhn2 + lax.dot_general(da, wff1_v[:, js], (((1,), (1,)), ((), ())), preferred_element_type=F32)
        gw2_ref[...] += jnp.sum(dhn2 * n2, axis=0, keepdims=True)
        dn2 = dhn2 * w2n_row
        dh1 = dh2 + r2 * (dn2 - n2 * jnp.mean(dn2 * n2, axis=-1, keepdims=True))
        dh1_ref[...] = dh1
        dy_ref[...] = lax.dot_general(_mx(dh1), wout_v[...], (((1,), (1,)), ((), ())), preferred_element_type=F32)

    rows = lambda w: pl.BlockSpec((tm, w), lambda i: (i, 0))
    hbm = pl.BlockSpec(memory_space=pl.ANY)
    return pl.pallas_call(
        body, grid=(m // tm,), name="ffn_fwd_bwd",
        in_specs=[_token_tiles(D_MODEL, nt), _full((ROW_TILE, D_MODEL)), rows(D_MIX), _token_tiles(D_MODEL, nt),
                  _full((1, D_MODEL)), _full((1, D_MODEL)), hbm, hbm, hbm],
        out_specs=[_full((1, 128)), _full((1, D_MODEL)), _full((1, D_MODEL)), rows(D_FF), rows(D_FF), rows(D_MODEL),
                   rows(D_MODEL), rows(D_MODEL), rows(D_MIX)],
        out_shape=[jax.ShapeDtypeStruct((1, 128), F32), jax.ShapeDtypeStruct((1, D_MODEL), F32),
                   jax.ShapeDtypeStruct((1, D_MODEL), F32), jax.ShapeDtypeStruct((m, D_FF), MXU_DTYPE),
                   jax.ShapeDtypeStruct((m, D_FF), MXU_DTYPE), jax.ShapeDtypeStruct((m, D_MODEL), MXU_DTYPE),
                   jax.ShapeDtypeStruct((m, D_MODEL), F32), jax.ShapeDtypeStruct((m, D_MODEL), MXU_DTYPE),
                   jax.ShapeDtypeStruct((m, D_MIX), F32)],
        scratch_shapes=[pltpu.VMEM((D_MIX, D_MODEL), MXU_DTYPE), pltpu.VMEM((D_MODEL, D_FF), MXU_DTYPE),
                        pltpu.VMEM((D_FF, D_MODEL), MXU_DTYPE), pltpu.VMEM((tm, D_FF), F32),
                        pltpu.SemaphoreType.DMA((3,))],
        compiler_params=_params("arbitrary"),
    )(x, lead, y, tgt, w2n, wfn, wout, wff1, wff2)


def _in_proj_bwd(dproj, x, lead, dh1, w1, win, after):
    nt = x.shape[0] // ROW_TILE
    m = x.shape[0] + ROW_TILE
    tm = ROW_TILE

    def body(dp_ref, x_ref, lead_ref, dh1_ref, w1_ref, win_hbm, after_ref, gx_ref, gw1_ref, gmeta_ref, win_v, sem):
        i = pl.program_id(0)

        @pl.when(i == 0)
        def _():
            cp = pltpu.make_async_copy(win_hbm, win_v, sem)
            cp.start()
            cp.wait()
            gw1_ref[...] = jnp.zeros_like(gw1_ref)
            gmeta_ref[...] = jnp.zeros_like(gmeta_ref)

        dhn = lax.dot_general(dp_ref[...], win_v[...], (((1,), (1,)), ((), ())), preferred_element_type=F32)
        x = jnp.where(i == nt, lead_ref[...], x_ref[...])
        r = lax.rsqrt(jnp.mean(x * x, axis=-1, keepdims=True) + EPS)
        n = x * r
        gw1_ref[...] += jnp.sum(dhn * n, axis=0, keepdims=True)
        dn = dhn * w1_ref[...]
        dh0 = dh1_ref[...] + r * (dn - n * jnp.mean(dn * n, axis=-1, keepdims=True))

        @pl.when(i < nt)
        def _():
            gx_ref[...] = dh0

        @pl.when(i == nt)
        def _():
            gmeta_ref[...] = dh0[PAD_ROWS:LEAD, :] + dh0[LEAD + PAD_ROWS:2 * LEAD, :]

    rows = lambda w: pl.BlockSpec((tm, w), lambda i: (i, 0))
    hbm = pl.BlockSpec(memory_space=pl.ANY)
    return pl.pallas_call(
        body, grid=(m // tm,), name="in_proj_bwd",
        in_specs=[rows(PROJ_W), _token_tiles(D_MODEL, nt), _full((ROW_TILE, D_MODEL)), rows(D_MODEL),
                  _full((1, D_MODEL)), hbm, hbm],
        out_specs=[_token_tiles(D_MODEL, nt), _full((1, D_MODEL)), _full((N_META, D_MODEL))],
        out_shape=[jax.ShapeDtypeStruct(x.shape, F32), jax.ShapeDtypeStruct((1, D_MODEL), F32),
                   jax.ShapeDtypeStruct((N_META, D_MODEL), F32)],
        scratch_shapes=[pltpu.VMEM((D_MODEL, PROJ_W), MXU_DTYPE), pltpu.SemaphoreType.DMA],
        compiler_params=_params("arbitrary"),
    )(dproj, x, lead, dh1, w1, win, after)


MXU_DEPTH = 256


def _row_slab(m, cap):
    return max(k for k in range(MXU_DEPTH, cap + 1, MXU_DEPTH) if m % k == 0)


def _tn_matmul(a, b, name, tka, max_slab=768, tn=512):
    m, ka = a.shape
    nb = b.shape[1]
    tkm = _row_slab(m, max_slab)
    n_steps = m // tkm

    def body(a_ref, b_ref, o_ref, omx_ref):
        k = pl.program_id(1)

        @pl.when(k == 0)
        def _():
            o_ref[...] = jnp.zeros_like(o_ref)

        at = _mx(a_ref[...])
        for j in range(0, nb, tn):
            w = min(tn, nb - j)
            o_ref[:, j:j + w] += lax.dot_general(at, _mx(b_ref[:, j:j + w]), (((0,), (0,)), ((), ())),
                                                 preferred_element_type=F32)

        @pl.when(k == n_steps - 1)
        def _():
            omx_ref[...] = _mx(o_ref[...])

    out = pl.BlockSpec((tka, nb), lambda i, k: (i, 0))
    return pl.pallas_call(
        body, grid=(ka // tka, n_steps), name=name,
        in_specs=[pl.BlockSpec((tkm, tka), lambda i, k: (k, i)), pl.BlockSpec((tkm, nb), lambda i, k: (k, 0))],
        out_specs=[out, out],
        out_shape=[jax.ShapeDtypeStruct((ka, nb), F32), jax.ShapeDtypeStruct((ka, nb), MXU_DTYPE)],
        compiler_params=_params("arbitrary", "arbitrary"),
    )(a, b)


def _tn_matmul_banded(a, b, name, band, tka, tn=512):
    m, ka = a.shape
    nb = b.shape[1]
    tkm = _row_slab(m, 768)
    n_steps = m // tkm

    def body(a_ref, b_ref, o_ref, acc):
        k = pl.program_id(1)

        @pl.when(k == 0)
        def _():
            acc[...] = jnp.zeros_like(acc)

        at = _mx(a_ref[...])
        for j in range(0, nb, tn):
            w = min(tn, nb - j)
            acc[:, j:j + w] += lax.dot_general(at, _mx(b_ref[:, j:j + w]), (((0,), (0,)), ((), ())),
                                               preferred_element_type=F32)

        @pl.when(k == n_steps - 1)
        def _():
            for j in range(N_DEV):
                o_ref[j] = acc[:, j * band:(j + 1) * band]

    return pl.pallas_call(
        body, grid=(ka // tka, n_steps), name=name,
        in_specs=[pl.BlockSpec((tkm, tka), lambda i, k: (k, i)), pl.BlockSpec((tkm, nb), lambda i, k: (k, 0))],
        out_specs=pl.BlockSpec((N_DEV, tka, band), lambda i, k: (0, i, 0)),
        out_shape=jax.ShapeDtypeStruct((N_DEV, ka, band), F32),
        scratch_shapes=[pltpu.VMEM((tka, nb), F32)],
        compiler_params=_params("arbitrary", "arbitrary"),
    )(a, b)


def _assemble_bands(g, width):
    n, rows, band = g.shape
    tr = 256

    def body(g_ref, o_ref):
        parts = [g_ref[j] for j in range(n)] + [jnp.zeros((tr, width - n * band), g.dtype)]
        o_ref[...] = jnp.concatenate(parts, axis=1)

    return pl.pallas_call(
        body, grid=(rows // tr,), name="assemble_w_in",
        in_specs=[pl.BlockSpec((n, tr, band), lambda i: (0, i, 0))],
        out_specs=pl.BlockSpec((tr, width), lambda i: (i, 0)),
        out_shape=jax.ShapeDtypeStruct((rows, width), g.dtype),
        compiler_params=_params("arbitrary"),
    )(g)


def _chunk_block(b, c, nb, nc):
    return jnp.where(c == 0, nb * (nc - 1) + b, b * (nc - 1) + c - 1)

def _mixer_fwd(proj, cw, cb, dt_bias, a_log, d_x, nw, pool_w, pool_scale, nb, shards, by_cols):
    m = proj.shape[0]
    nc = m // nb // CHUNK
    n_steps = nb * nc
    ns = len(shards)
    row_widths = [D_XBC, 128, 128, D_SSM, D_SSM, D_POOL]
    rows = jnp.concatenate([cb, dt_bias, a_log, d_x, nw, pool_scale], axis=1)

    def body(p_ref, cw_ref, rows_ref, pw_ref, *rest):
        shard_refs, (y_ref, ypre_ref, pre_ref, st_ref) = rest[:ns], rest[ns:ns + 4]
        gathered_refs, (xtail, utail, state, e_ref, et_ref, tril_ref) = rest[ns + 4:2 * ns + 4], rest[2 * ns + 4:2 * ns + 10]
        gather = _Gather(shard_refs, gathered_refs, by_cols, *rest[2 * ns + 10:])
        cb_ref, dtb_ref, alog_ref, dx_ref, nw_ref, ps_ref = _row_views(rows_ref, row_widths)
        c = pl.program_id(1)
        step = pl.program_id(0) * nc + c

        @pl.when(step == 0)
        def _():
            gather.start()
            _fill_ssd_constants(e=e_ref, et_f32=et_ref, tril=tril_ref)

        @pl.when(step == n_steps // 2)
        def _():
            gather.forward()

        @pl.when(c == 0)
        def _():
            xtail[...] = jnp.zeros_like(xtail)
            utail[...] = jnp.zeros_like(utail)
            state[...] = jnp.zeros_like(state)

        valid = (c > 0) | (lax.broadcasted_iota(jnp.int32, (CHUNK, 1), 0) >= PAD_ROWS)

        u = p_ref[:, 0:D_POOL]
        inv_cnt, lane = _pool_inv_count(c)
        win = _pool_window_sums(jnp.concatenate([utail[...], u], axis=0), lane)
        utail[...] = u[CHUNK - HALO:, :]
        pooled = win * inv_cnt - u
        mixed = jnp.concatenate(
            [_dot(pooled[:, g * 128:(g + 1) * 128], pw_ref[g]) for g in range(len(POOL_WINDOWS))], axis=1)
        y_ref[:, 0:D_POOL] = _mx(mixed * ps_ref[...])

        xbc = p_ref[:, OFF_X:OFF_X + D_XBC]
        pre = _conv_pre(jnp.concatenate([xtail[...], xbc], axis=0), xbc, cw_ref[...], cb_ref[...])
        xtail[...] = xbc[CHUNK - HALO:, :]
        pre_ref[...] = pre
        xc = pre * _sigmoid(pre)
        dt, _, a_col, _, _ = _dt_and_cumsum(p_ref[:, OFF_DT:OFF_DT + 128], dtb_ref[...], alog_ref[...], valid,
                                            tril_ref[...])
        s_prev = state[...]
        st_ref[0] = s_prev
        yp, s_new = _ssd_chunk_fwd(xc[:, 0:D_SSM], xc[:, D_SSM:D_SSM + 512], xc[:, D_SSM + 512:], dt, a_col, s_prev,
                                   dx_ref[...], e_ref[...], et_ref[...])
        state[...] = s_new
        ypre_ref[...] = yp
        z = p_ref[:, OFF_Z:OFF_Z + D_SSM]
        yz = yp * (z * _sigmoid(z))
        outs = []
        for g in range(N_GROUPS):
            gs = slice(g * GROUP_CH, (g + 1) * GROUP_CH)
            r = lax.rsqrt(jnp.mean(yz[:, gs] * yz[:, gs], axis=-1, keepdims=True) + EPS)
            outs.append(yz[:, gs] * r)
        y_ref[:, D_POOL:] = _mx(jnp.concatenate(outs, axis=1) * nw_ref[...])

        @pl.when(step == n_steps - 1)
        def _():
            gather.finish()

    blk = lambda w: pl.BlockSpec((CHUNK, w), lambda b, c: (_chunk_block(b, c, nb, nc), 0))
    hbm = pl.BlockSpec(memory_space=pl.ANY)
    outs = pl.pallas_call(
        body, grid=(nb, nc), name="mixer_fwd",
        in_specs=[blk(PROJ_W), _full((4, D_XBC)), _full((1, sum(row_widths))), _full((4, 128, 128))] + [hbm] * ns,
        out_specs=[blk(D_MIX), blk(D_SSM), blk(D_XBC),
                   pl.BlockSpec((1, D_SSM, D_STATE), lambda b, c: (b * nc + c, 0, 0))] + [hbm] * ns,
        out_shape=[jax.ShapeDtypeStruct((m, D_MIX), MXU_DTYPE), jax.ShapeDtypeStruct((m, D_SSM), F32),
                   jax.ShapeDtypeStruct((m, D_XBC), F32), jax.ShapeDtypeStruct((m // CHUNK, D_SSM, D_STATE), F32)]
        + _Gather.out_shapes(shards, by_cols),
        scratch_shapes=[pltpu.VMEM((HALO, D_XBC), F32), pltpu.VMEM((HALO, D_POOL), F32),
                        pltpu.VMEM((D_SSM, D_STATE), F32)] + _ssd_constant_scratch(["e", "et_f32", "tril"])
        + _Gather.scratch(ns),
        compiler_params=_params("arbitrary", "arbitrary"),
    )(proj, cw, rows, pool_w, *shards)
    return outs[0], outs[1], outs[2], outs[3], outs[4:]


def _mixer_bwd(proj, dy, ypre, conv_pre, states, cw, dt_bias, a_log, d_x, nw, pool_w, pool_scale, nb, chip_sums):
    m = proj.shape[0]
    nc = m // nb // CHUNK
    hb = CHUNK // HALO
    ns = len(chip_sums)
    row_widths = [128, 128, D_SSM, D_SSM, D_POOL]
    rows = jnp.concatenate([dt_bias, a_log, d_x, nw, pool_scale], axis=1)
    grad_row_widths = [D_XBC, 128, 128, 128, D_SSM, D_POOL]
    constants = ["e", "et", "et_f32", "tril", "triu"]

    def body(p_ref, halo_ref, dy_ref, ypre_ref, pre_ref, st_ref, cw_ref, rows_ref, pw_ref, *rest):
        cs_refs = rest[:ns]
        dp_ref, gcw_ref, grows_ref, gpw_ref = rest[ns:ns + 4]
        part_refs, (ds_carry, dpre_next, dq_next) = rest[ns + 4:2 * ns + 4], rest[2 * ns + 4:2 * ns + 7]
        e_ref, et_ref, etf_ref, tril_ref, triu_ref = rest[2 * ns + 7:2 * ns + 12]
        exchange = _ChipExchange(cs_refs, part_refs, [False] * ns, *rest[2 * ns + 12:])
        dtb_ref, alog_ref, dx_ref, nw_ref, ps_ref = _row_views(rows_ref, row_widths)
        gcb_ref, gdtb_ref, galog_ref, gd_ref, gnw_ref, gps_ref = _row_views(grows_ref, grad_row_widths)
        b = pl.program_id(0)
        cc = pl.program_id(1)
        c = nc - 1 - cc

        @pl.when((b == 0) & (cc == 0))
        def _():
            exchange.start()
            _fill_ssd_constants(e=e_ref, et=et_ref, et_f32=etf_ref, tril=tril_ref, triu=triu_ref)
            for r in (gcw_ref, grows_ref, gpw_ref):
                r[...] = jnp.zeros_like(r)

        @pl.when(cc == 0)
        def _():
            ds_carry[...] = jnp.zeros_like(ds_carry)
            dpre_next[...] = jnp.zeros_like(dpre_next)
            dq_next[...] = jnp.zeros_like(dq_next)

        valid = (c > 0) | (lax.broadcasted_iota(jnp.int32, (CHUNK, 1), 0) >= PAD_ROWS)
        first = c > 0

        u = p_ref[:, 0:D_POOL]
        u_halo = jnp.where(first, halo_ref[...], 0.0)
        inv_cnt, lane = _pool_inv_count(c)
        pooled = _pool_window_sums(jnp.concatenate([u_halo, u], axis=0), lane) * inv_cnt - u
        dyp = dy_ref[:, 0:D_POOL]
        ps = ps_ref[...]
        dmixed = dyp * ps
        mixed, dpooled = [], []
        for g in range(len(POOL_WINDOWS)):
            gsl = slice(g * 128, (g + 1) * 128)
            pw = pw_ref[g]
            mixed.append(_dot(pooled[:, gsl], pw))
            dpooled.append(_dot_nt(dmixed[:, gsl], pw))
            gpw_ref[g] += _dot_tn(pooled[:, gsl], dmixed[:, gsl])
        gps_ref[...] += jnp.sum(dyp * jnp.concatenate(mixed, axis=1), axis=0, keepdims=True)
        dpooled = jnp.concatenate(dpooled, axis=1)
        dq = dpooled * inv_cnt
        du = _pool_window_sums_ahead(jnp.concatenate([dq, dq_next[...]], axis=0), lane) - dpooled
        dq_next[...] = dq[0:HALO, :]
        dp_ref[:, 0:D_POOL] = _mx(du)

        yp = ypre_ref[...]
        z = p_ref[:, OFF_Z:OFF_Z + D_SSM]
        sz, dsz = _silu_and_grad(z)
        yz = yp * sz
        do = dy_ref[:, D_POOL:]
        nw_row = nw_ref[...]
        dyz = []
        gnw = []
        for g in range(N_GROUPS):
            gs = slice(g * GROUP_CH, (g + 1) * GROUP_CH)
            r = lax.rsqrt(jnp.mean(yz[:, gs] * yz[:, gs], axis=-1, keepdims=True) + EPS)
            n = yz[:, gs] * r
            gnw.append(jnp.sum(do[:, gs] * n, axis=0, keepdims=True))
            dn = do[:, gs] * nw_row[:, gs]
            dyz.append(r * (dn - n * jnp.mean(dn * n, axis=-1, keepdims=True)))
        gnw_ref[...] += jnp.concatenate(gnw, axis=1)
        dyz = jnp.concatenate(dyz, axis=1)
        dp_ref[:, OFF_Z:OFF_Z + D_SSM] = _mx(dyz * yp * dsz)
        dyp_ssm = dyz * sz

        xc, dsilu = _silu_and_grad(pre_ref[...])
        dtr = p_ref[:, OFF_DT:OFF_DT + 128]
        dt, a_row, a_col, dt_pre, head = _dt_and_cumsum(dtr, dtb_ref[...], alog_ref[...], valid, tril_ref[...])
        dxs, dbm, dcm, ddt, da, dd, ds_prev = _ssd_chunk_bwd(
            xc[:, 0:D_SSM], xc[:, D_SSM:D_SSM + 512], xc[:, D_SSM + 512:], dt, a_row, a_col, st_ref[0],
            ds_carry[...], dyp_ssm, dx_ref[...], e_ref[...], et_ref[...], etf_ref[...], triu_ref[...])
        ds_carry[...] = ds_prev
        gd_ref[...] += dd
        galog_ref[...] += da * a_row
        ddtr = jnp.where(valid & head, ddt * _sigmoid(dt_pre), 0.0)
        gdtb_ref[...] += jnp.sum(ddtr, axis=0, keepdims=True)
        dp_ref[:, OFF_DT:OFF_DT + 128] = _mx(ddtr)

        dpre = jnp.concatenate([dxs, dbm, dcm], axis=1) * dsilu
        gcb_ref[...] += jnp.sum(dpre, axis=0, keepdims=True)
        dext = jnp.concatenate([dpre, dpre_next[...]], axis=0)
        dpre_next[...] = dpre[0:HALO, :]
        ups = [_shift_up(dext, 3 - k) for k in range(4)]
        xbc = p_ref[:, OFF_X:OFF_X + D_XBC]
        gcw_ref[...] += jnp.concatenate([jnp.sum(xbc * ups[k], axis=0, keepdims=True) for k in range(4)], axis=0)
        cw = cw_ref[...]
        dp_ref[:, OFF_X:OFF_X + D_XBC] = _mx(cw[3:4, :] * ups[3] + cw[2:3, :] * ups[2]
                                             + cw[1:2, :] * ups[1] + cw[0:1, :] * ups[0])

        @pl.when((b == nb - 1) & (cc == nc - 1))
        def _():
            exchange.finish()

    blk = lambda w: pl.BlockSpec((CHUNK, w), lambda b, cc: (_chunk_block(b, nc - 1 - cc, nb, nc), 0))
    halo = pl.BlockSpec((HALO, D_POOL),
                        lambda b, cc: (_chunk_block(b, jnp.maximum(nc - 2 - cc, 0), nb, nc) * hb + hb - 1, 0))
    hbm = pl.BlockSpec(memory_space=pl.ANY)
    outs = pl.pallas_call(
        body, grid=(nb, nc), name="mixer_bwd",
        in_specs=[blk(PROJ_W), halo, blk(D_MIX), blk(D_SSM), blk(D_XBC),
                  pl.BlockSpec((1, D_SSM, D_STATE), lambda b, cc: (b * nc + nc - 1 - cc, 0, 0)),
                  _full((4, D_XBC)), _full((1, sum(row_widths))), _full((4, 128, 128))] + [hbm] * ns,
        out_specs=[blk(PROJ_W), _full((4, D_XBC)), _full((1, sum(grad_row_widths))), _full((4, 128, 128))] + [hbm] * ns,
        out_shape=[jax.ShapeDtypeStruct((m, PROJ_W), MXU_DTYPE), jax.ShapeDtypeStruct((4, D_XBC), F32),
                   jax.ShapeDtypeStruct((1, sum(grad_row_widths)), F32), jax.ShapeDtypeStruct((4, 128, 128), F32)]
        + _ChipExchange.out_shapes(chip_sums, [False] * ns),
        scratch_shapes=[pltpu.VMEM((D_SSM, D_STATE), F32), pltpu.VMEM((HALO, D_XBC), F32),
                        pltpu.VMEM((HALO, D_POOL), F32)] + _ssd_constant_scratch(constants) + _ChipExchange.scratch(ns),
        compiler_params=_params("arbitrary", "arbitrary"),
    )(proj, proj, dy, ypre, conv_pre, states, cw, rows, pool_w, *chip_sums)
    dproj, g_cw, g_rows, g_pw = outs[:4]
    offs = np.cumsum([0] + grad_row_widths)
    g_cb, g_dtb, g_alog, g_d, g_nw, g_ps = (g_rows[:, a:b] for a, b in zip(offs[:-1], offs[1:]))
    return (dproj, g_cw, g_cb, g_dtb, g_alog, g_d, g_nw, g_pw, g_ps), outs[4:]


MESH_IDS = pl.DeviceIdType.MESH
_HBM = pl.BlockSpec(memory_space=pltpu.HBM)


def _coords():
    return lax.axis_index("x"), lax.axis_index("y"), lax.axis_index("c")


def _other_chips(x, y):
    return [(1 - x, y), (x, 1 - y), (1 - x, 1 - y)]


class _Gather:
    def __init__(self, ins, outs, by_cols, send_sems, recv_sems, local_sems):
        self.ins, self.outs, self.by_cols, self.n = ins, outs, by_cols, len(ins)
        self.send_sems, self.recv_sems, self.local_sems = send_sems, recv_sems, local_sems
        self.x, self.y, self.c = _coords()
        self.me, self.sibling = (self.x, self.y, self.c), (self.x, self.y, 1 - self.c)
        self.chips = _other_chips(self.x, self.y)

    @staticmethod
    def scratch(n):
        return [pltpu.SemaphoreType.DMA((7 * n,)), pltpu.SemaphoreType.DMA((7 * n,)), pltpu.SemaphoreType.DMA((n,))]

    @staticmethod
    def out_shapes(shards, by_cols):
        return [jax.ShapeDtypeStruct((s.shape[0], N_DEV * s.shape[1]) if cols else (N_DEV,) + s.shape, s.dtype)
                for s, cols in zip(shards, by_cols)]

    def _block(self, t, device):
        idx = 4 * device[0] + 2 * device[1] + device[2]
        if not self.by_cols[t]:
            return self.outs[t].at[idx]
        w = self.ins[t].shape[1]
        return self.outs[t].at[:, pl.ds(pl.multiple_of(idx * w, w), w)]

    def _copy(self, t, k, block, to, own=False):
        dst = self._block(t, block)
        return pltpu.make_async_remote_copy(
            src_ref=self.ins[t] if own else dst, dst_ref=dst, send_sem=self.send_sems.at[t * 7 + k],
            recv_sem=self.recv_sems.at[t * 7 + k], device_id=to, device_id_type=MESH_IDS)

    def _mine(self):
        return [pltpu.make_async_copy(self.ins[t], self._block(t, self.me), self.local_sems.at[t])
                for t in range(self.n)]

    def _first(self):
        cps = []
        for t in range(self.n):
            cps.append(self._copy(t, 0, self.me, self.sibling, own=True))
            cps += [self._copy(t, 1 + j, self.me, (*chip, self.c), own=True) for j, chip in enumerate(self.chips)]
        return cps

    def _passed(self):
        return [self._copy(t, 4 + j, (*chip, self.c), self.sibling)
                for j, chip in enumerate(self.chips) for t in range(self.n)]

    def start(self):
        for cp in self._mine() + self._first():
            cp.start()

    def forward(self):
        for j, chip in enumerate(self.chips):
            for t in range(self.n):
                self._copy(t, 1 + j, (*chip, self.c), self.me).wait_recv()
                self._copy(t, 4 + j, (*chip, self.c), self.sibling).start()

    def finish(self):
        for t in range(self.n):
            self._copy(t, 0, self.sibling, self.me).wait_recv()
            for j, chip in enumerate(self.chips):
                self._copy(t, 4 + j, (*chip, 1 - self.c), self.me).wait_recv()
        for cp in self._first() + self._passed():
            cp.wait_send()
        for cp in self._mine():
            cp.wait()


def _weight_gather(shards):
    n = len(shards)

    def body(*refs):
        g = _Gather(refs[:n], refs[n:2 * n], [False] * n, *refs[2 * n:])
        g.start()
        g.forward()
        g.finish()

    return pl.pallas_call(
        body, name="weight_gather",
        in_specs=[_HBM] * n, out_specs=[_HBM] * n,
        out_shape=_Gather.out_shapes(shards, [False] * n),
        scratch_shapes=_Gather.scratch(n),
    )(*shards)


def _owner_blocks(g, by_cols):
    return (g.shape[0], g.shape[1] // N_DEV) if by_cols else g.shape[2:]


def _grad_exchange_d2d(gs, by_cols, name):
    n = len(gs)

    def body(*refs):
        ins, got = refs[:n], refs[n:2 * n]
        send_sems, recv_sems = refs[2 * n:]
        x, y, c = _coords()

        def src(t, k):
            if not by_cols[t]:
                return ins[t].at[k, 1 - c]
            w = ins[t].shape[1] // N_DEV
            return ins[t].at[:, pl.ds(pl.multiple_of((2 * k + 1 - c) * w, w), w)]

        remote = [pltpu.make_async_remote_copy(
            src_ref=src(t, k), dst_ref=got[t].at[k], send_sem=send_sems.at[t * 4 + k],
            recv_sem=recv_sems.at[t * 4 + k], device_id=(x, y, 1 - c), device_id_type=MESH_IDS)
            for t in range(n) for k in range(4)]
        for cp in remote:
            cp.start()
        for cp in remote:
            cp.wait_recv()
        for cp in remote:
            cp.wait_send()

    return pl.pallas_call(
        body, name=name,
        in_specs=[_HBM] * n, out_specs=[_HBM] * n,
        out_shape=[jax.ShapeDtypeStruct((4,) + _owner_blocks(g, cols), g.dtype) for g, cols in zip(gs, by_cols)],
        scratch_shapes=[pltpu.SemaphoreType.DMA((4 * n,)), pltpu.SemaphoreType.DMA((4 * n,))],
    )(*gs)


def _small_allreduce(pack):
    rows = pack.shape[0]

    def body(p_ref, o_ref, sib_ref, parts_ref, send_sems, recv_sems):
        x, y, c = _coords()
        my_chip = 2 * x + y
        swap = pltpu.make_async_remote_copy(src_ref=p_ref, dst_ref=sib_ref, send_sem=send_sems.at[0],
                                            recv_sem=recv_sems.at[0], device_id=(x, y, 1 - c), device_id_type=MESH_IDS)
        swap.start()
        swap.wait_recv()
        parts_ref[my_chip] = p_ref[...] + sib_ref[...]
        remote = [pltpu.make_async_remote_copy(
            src_ref=parts_ref.at[my_chip], dst_ref=parts_ref.at[my_chip], send_sem=send_sems.at[1 + j],
            recv_sem=recv_sems.at[1 + j], device_id=(cx, cy, c), device_id_type=MESH_IDS)
            for j, (cx, cy) in enumerate(_other_chips(x, y))]
        for cp in remote:
            cp.start()
        for j, (cx, cy) in enumerate(_other_chips(x, y)):
            slot = parts_ref.at[2 * cx + cy]
            pltpu.make_async_remote_copy(src_ref=slot, dst_ref=slot, send_sem=send_sems.at[1 + j],
                                         recv_sem=recv_sems.at[1 + j], device_id=(cx, cy, c),
                                         device_id_type=MESH_IDS).wait_recv()
        o_ref[...] = ((parts_ref[0] + parts_ref[1]) + parts_ref[2]) + parts_ref[3]
        swap.wait_send()
        for cp in remote:
            cp.wait_send()

    vmem = pl.BlockSpec(memory_space=pltpu.VMEM)
    return pl.pallas_call(
        body, name="small_allreduce", in_specs=[vmem], out_specs=vmem,
        out_shape=jax.ShapeDtypeStruct((rows, 128), F32),
        scratch_shapes=[pltpu.VMEM((rows, 128), F32), pltpu.VMEM((4, rows, 128), F32),
                        pltpu.SemaphoreType.DMA((4,)), pltpu.SemaphoreType.DMA((4,))],
    )(pack)


class _ChipExchange:
    def __init__(self, ins, outs, whole, send_sems, recv_sems, local_sems):
        self.ins, self.outs, self.whole, self.n = ins, outs, whole, len(ins)
        self.send_sems, self.recv_sems, self.local_sems = send_sems, recv_sems, local_sems
        self.x, self.y, self.c = _coords()
        self.my_chip = 2 * self.x + self.y
        self.chips = _other_chips(self.x, self.y)

    @staticmethod
    def scratch(n):
        return [pltpu.SemaphoreType.DMA((3 * n,)), pltpu.SemaphoreType.DMA((3 * n,)), pltpu.SemaphoreType.DMA((n,))]

    def _src(self, t, k):
        return self.ins[t] if self.whole[t] else self.ins[t].at[k]

    def _local(self):
        return [pltpu.make_async_copy(self._src(t, self.my_chip), self.outs[t].at[self.my_chip], self.local_sems.at[t])
                for t in range(self.n)]

    def _remote(self):
        return [pltpu.make_async_remote_copy(
            src_ref=self._src(t, 2 * cx + cy), dst_ref=self.outs[t].at[self.my_chip],
            send_sem=self.send_sems.at[t * 3 + j], recv_sem=self.recv_sems.at[t * 3 + j],
            device_id=(cx, cy, self.c), device_id_type=MESH_IDS)
            for t in range(self.n) for j, (cx, cy) in enumerate(self.chips)]

    def start(self):
        for cp in self._remote() + self._local():
            cp.start()

    def finish(self):
        for t in range(self.n):
            for j, (cx, cy) in enumerate(self.chips):
                slot = self.outs[t].at[2 * cx + cy]
                pltpu.make_async_remote_copy(
                    src_ref=slot, dst_ref=slot, send_sem=self.send_sems.at[t * 3 + j],
                    recv_sem=self.recv_sems.at[t * 3 + j], device_id=(cx, cy, self.c),
                    device_id_type=MESH_IDS).wait_recv()
        for cp in self._remote():
            cp.wait_send()
        for cp in self._local():
            cp.wait()

    @staticmethod
    def out_shapes(arrs, whole):
        return [jax.ShapeDtypeStruct(((4,) + a.shape) if w else a.shape, a.dtype) for a, w in zip(arrs, whole)]


_SEMAPHORES = pl.BlockSpec(memory_space=pltpu.SEMAPHORE)
_SIDE_EFFECT = pltpu.SideEffectType.DATAFLOW_SIDE_EFFECTING


def _chip_exchange_start(chip_sum, name):
    def body(src_ref, land_ref, send_sems, recv_sems, src_thru, land_thru, token):
        x, y, c = _coords()
        for j, (cx, cy) in enumerate(_other_chips(x, y)):
            pltpu.make_async_remote_copy(
                src_ref=src_ref.at[2 * cx + cy], dst_ref=land_ref.at[2 * x + y], send_sem=send_sems.at[j],
                recv_sem=recv_sems.at[j], device_id=(cx, cy, c), device_id_type=MESH_IDS).start()
        token[...] = jnp.zeros_like(token)

    zone = pltpu.HBM(chip_sum.shape, chip_sum.dtype)
    return pl.pallas_call(
        body, name=name,
        out_shape=(pltpu.SemaphoreType.DMA((3,)), pltpu.SemaphoreType.DMA((3,)), zone, zone,
                   jax.ShapeDtypeStruct((8, 128), F32)),
        in_specs=(_HBM, _HBM), out_specs=(_SEMAPHORES, _SEMAPHORES, _HBM, _HBM, pl.BlockSpec(memory_space=pltpu.VMEM)),
        input_output_aliases={0: 2, 1: 3},
        compiler_params=pltpu.CompilerParams(has_side_effects=_SIDE_EFFECT),
    )(pltpu.with_memory_space_constraint(chip_sum, pltpu.HBM),
      pltpu.with_memory_space_constraint(lax.empty(chip_sum.shape, chip_sum.dtype), pltpu.HBM))


def _chip_exchange_wait(send_sems, recv_sems, src_thru, land_thru, after, name):
    def body(src_ref, land_ref, send_sems, recv_sems, after_ref, src_out, land_out):
        x, y, c = _coords()
        for j, (cx, cy) in enumerate(_other_chips(x, y)):
            cp = pltpu.make_async_remote_copy(
                src_ref=src_ref.at[2 * cx + cy], dst_ref=land_ref.at[2 * cx + cy], send_sem=send_sems.at[j],
                recv_sem=recv_sems.at[j], device_id=(cx, cy, c), device_id_type=MESH_IDS)
            cp.wait_send()
            cp.wait_recv()

    zone = pltpu.HBM(src_thru.shape, src_thru.dtype)
    return pl.pallas_call(
        body, name=name, out_shape=(zone, zone),
        in_specs=(_HBM, _HBM, _SEMAPHORES, _SEMAPHORES, pl.BlockSpec(memory_space=pl.ANY)), out_specs=(_HBM, _HBM),
        input_output_aliases={0: 0, 1: 1},
        compiler_params=pltpu.CompilerParams(has_side_effects=_SIDE_EFFECT),
    )(src_thru, land_thru, send_sems, recv_sems, after)


def _row_tile(rows, cols, n_arrays):
    budget = 24 * 1024 * 1024
    padded = -(-cols // 128) * 128
    step = 16 if rows % 16 == 0 else 8
    tr = max(step, budget // (n_arrays * 2 * 4 * padded) // step * step)
    while rows % tr:
        tr -= step
    return tr


def _chip_sum(g, by_cols, got, core, name):
    rows, cols = _owner_blocks(g, by_cols)
    tr = _row_tile(rows, cols, 3)

    def body(c_ref, a_ref, b_ref, o_ref):
        o_ref[...] = (a_ref[...] + b_ref[...].astype(F32)).astype(o_ref.dtype)

    own = (pl.BlockSpec((tr, cols), lambda k, i, c: (i, 2 * k + c[0])) if by_cols
           else pl.BlockSpec((None, None, tr, cols), lambda k, i, c: (k, c[0], i, 0)))
    grid_spec = pltpu.PrefetchScalarGridSpec(
        num_scalar_prefetch=1, grid=(4, rows // tr),
        in_specs=[own, pl.BlockSpec((None, tr, cols), lambda k, i, c: (k, i, 0))],
        out_specs=pl.BlockSpec((None, tr, cols), lambda k, i, c: (k, i, 0)))
    return pl.pallas_call(body, grid_spec=grid_spec, name=name,
                          out_shape=jax.ShapeDtypeStruct((4, rows, cols), MXU_DTYPE),
                          compiler_params=_params("arbitrary", "arbitrary"))(core, g, got)


def _adamw_math(w, g, m, v):
    m2 = ADAM_B1 * m + (1.0 - ADAM_B1) * g
    v2 = ADAM_B2 * v + (1.0 - ADAM_B2) * (g * g)
    m_hat = m2 / (1.0 - ADAM_B1 ** ADAM_STEP)
    v_hat = v2 / (1.0 - ADAM_B2 ** ADAM_STEP)
    delta = -ADAM_LR * (m_hat / (jnp.sqrt(v_hat) + ADAM_EPS) + ADAM_WD * w)
    return delta, m2, v2


def _adamw(parts, w, m, v, name, own=None, chip=None):
    rows, cols = w.shape
    tr = _row_tile(rows, cols, 11 if own is None else 15)

    def body(*refs):
        if own is None:
            p_ref, w_ref, m_ref, v_ref, g_ref, d_ref, m2_ref, v2_ref = refs
            part = lambda k: p_ref[k].astype(F32)
        else:
            chip_ref, p_ref, own_ref, w_ref, m_ref, v_ref, g_ref, d_ref, m2_ref, v2_ref = refs
            part = lambda k: jnp.where(chip_ref[0] == k, own_ref[k], p_ref[k]).astype(F32)
        g = ((part(0) + part(1)) + part(2)) + part(3)
        d, m2, v2 = _adamw_math(w_ref[...], g, m_ref[...], v_ref[...])
        g_ref[...] = g
        d_ref[...] = d
        m2_ref[...] = m2
        v2_ref[...] = v2

    blk = pl.BlockSpec((tr, cols), lambda i, *_: (i, 0))
    pblk = pl.BlockSpec((4, tr, cols), lambda i, *_: (0, i, 0))
    out = jax.ShapeDtypeStruct((rows, cols), F32)
    if own is None:
        return pl.pallas_call(body, grid=(rows // tr,), name=name, in_specs=[pblk, blk, blk, blk],
                              out_specs=[blk] * 4, out_shape=[out] * 4,
                              compiler_params=_params("arbitrary"))(parts, w, m, v)
    grid_spec = pltpu.PrefetchScalarGridSpec(num_scalar_prefetch=1, grid=(rows // tr,),
                                             in_specs=[pblk, pblk, blk, blk, blk], out_specs=[blk] * 4)
    return pl.pallas_call(body, grid_spec=grid_spec, name=name, out_shape=[out] * 4,
                          compiler_params=_params("arbitrary"))(chip, parts, own, w, m, v)


def _adamw_transposed(parts_t, w, m, v, name):
    rows, cols = w.shape
    tc = 256

    def body(p_ref, w_ref, m_ref, v_ref, g_ref, d_ref, m2_ref, v2_ref):
        part = lambda k: p_ref[k].astype(F32)
        g = (((part(0) + part(1)) + part(2)) + part(3)).T
        d, m2, v2 = _adamw_math(w_ref[...], g, m_ref[...], v_ref[...])
        g_ref[...] = g
        d_ref[...] = d
        m2_ref[...] = m2
        v2_ref[...] = v2

    blk = pl.BlockSpec((rows, tc), lambda i: (0, i))
    out = jax.ShapeDtypeStruct((rows, cols), F32)
    return pl.pallas_call(body, grid=(cols // tc,), name=name,
                          in_specs=[pl.BlockSpec((4, tc, rows), lambda i: (0, i, 0)), blk, blk, blk],
                          out_specs=[blk] * 4, out_shape=[out] * 4,
                          compiler_params=_params("arbitrary"))(parts_t, w, m, v)


def _adamw_small(gs, ws, ms, vs):
    n = len(ws)

    def body(*refs):
        g_refs, w_refs, m_refs, v_refs = (refs[k * n:(k + 1) * n] for k in range(4))
        d_refs, m2_refs, v2_refs = (refs[(4 + k) * n:(5 + k) * n] for k in range(3))
        for t in range(n):
            d, m2, v2 = _adamw_math(w_refs[t][...], g_refs[t][...], m_refs[t][...], v_refs[t][...])
            d_refs[t][...] = d
            m2_refs[t][...] = m2
            v2_refs[t][...] = v2

    outs = pl.pallas_call(body, name="adamw_small",
                          out_shape=[jax.ShapeDtypeStruct(w.shape, F32) for w in ws] * 3)(*gs, *ws, *ms, *vs)
    return outs[:n], outs[n:2 * n], outs[2 * n:]


_PACK_TILE = 8 * 128


def _pack(arrays):
    rows = []
    for a in arrays:
        flat = a.astype(F32).reshape(-1)
        rows.append(jnp.pad(flat, (0, -flat.shape[0] % _PACK_TILE)).reshape(-1, 128))
    return jnp.concatenate(rows, axis=0)


def _unpack(pack, shapes):
    out, r = [], 0
    for s in shapes:
        n = int(np.prod(s))
        out.append(pack[r:r + -(-n // 128)].reshape(-1)[:n].reshape(s))
        r += -(-n // _PACK_TILE) * 8
    return out


def _pad128(v):
    v = v.reshape(1, -1).astype(F32)
    return jnp.pad(v, ((0, 0), (0, 128 - v.shape[1])))


_WEIGHTS = ["meta", "norm_mix_w", "w_in", "pool_w", "pool_scale", "conv_w", "conv_b", "dt_bias", "a_log", "d_skip",
            "ssm_norm_w", "w_out", "norm_ffn_w", "w_ff1", "w_ff2", "norm_f_w"]
_BIG = ["w_in", "w_out", "w_ff1", "w_ff2"]
_SMALL = [n for n in _WEIGHTS if n not in _BIG]


def kernel(x, meta, norm_mix_w, w_in, pool_w, pool_scale, conv_w, conv_b, dt_bias, a_log, d_skip, ssm_norm_w, w_out, norm_ffn_w, w_ff1, w_ff2, norm_f_w, loss_target, m_meta, m_norm_mix_w, m_w_in, m_pool_w, m_pool_scale, m_conv_w, m_conv_b, m_dt_bias, m_a_log, m_d_skip, m_ssm_norm_w, m_w_out, m_norm_ffn_w, m_w_ff1, m_w_ff2, m_norm_f_w, v_meta, v_norm_mix_w, v_w_in, v_pool_w, v_pool_scale, v_conv_w, v_conv_b, v_dt_bias, v_a_log, v_d_skip, v_ssm_norm_w, v_w_out, v_norm_ffn_w, v_w_ff1, v_w_ff2, v_norm_f_w):
    wts = dict(meta=meta, norm_mix_w=norm_mix_w, w_in=w_in, pool_w=pool_w, pool_scale=pool_scale, conv_w=conv_w,
               conv_b=conv_b, dt_bias=dt_bias, a_log=a_log, d_skip=d_skip, ssm_norm_w=ssm_norm_w, w_out=w_out,
               norm_ffn_w=norm_ffn_w, w_ff1=w_ff1, w_ff2=w_ff2, norm_f_w=norm_f_w)
    mom1 = dict(zip(_WEIGHTS, (m_meta, m_norm_mix_w, m_w_in, m_pool_w, m_pool_scale, m_conv_w, m_conv_b, m_dt_bias,
                               m_a_log, m_d_skip, m_ssm_norm_w, m_w_out, m_norm_ffn_w, m_w_ff1, m_w_ff2, m_norm_f_w)))
    mom2 = dict(zip(_WEIGHTS, (v_meta, v_norm_mix_w, v_w_in, v_pool_w, v_pool_scale, v_conv_w, v_conv_b, v_dt_bias,
                               v_a_log, v_d_skip, v_ssm_norm_w, v_w_out, v_norm_ffn_w, v_w_ff1, v_w_ff2, v_norm_f_w)))
    xi, yi, ci = _coords()
    dev = 4 * xi + 2 * yi + ci
    win_cols = w_in.shape[-1]
    cw_cols = conv_w.shape[-1]

    nb, seq, _ = x.shape
    core = jnp.reshape(ci, (1,)).astype(jnp.int32)
    owners = lambda a: a.reshape((4, 2) + a.shape[1:])

    lead_pack = jnp.zeros((N_META, 512), F32)
    lead_pack = lead_pack.at[:, :128].set(meta).at[:4, 128:128 + cw_cols].set(conv_w[0])
    g_win, g_lead = _weight_gather([_mx(w_in[0]), lead_pack])
    win_full = _assemble_bands(g_win, PROJ_W)
    meta_full = jnp.transpose(g_lead[:, :, :128], (1, 0, 2)).reshape(N_META, D_MODEL)
    cw_full = jnp.transpose(g_lead[:, :4, 128:128 + cw_cols], (1, 0, 2)).reshape(4, D_XBC)

    lead = jnp.concatenate([jnp.zeros((PAD_ROWS, D_MODEL), F32), meta_full] * nb, axis=0)
    x_rows = x.reshape(nb * seq, D_MODEL)
    tgt_rows = loss_target.reshape(nb * seq, D_MODEL)
    dt_bias_p, a_log_p = _pad128(dt_bias), _pad128(a_log)
    d_x = jnp.repeat(d_skip.reshape(1, N_HEADS).astype(F32), HEAD_DIM, axis=1)
    norm_f_row = norm_f_w.reshape(1, D_MODEL)

    hn1, proj = _in_proj(x_rows, lead, norm_mix_w, win_full)
    late_cols = [False, True, False]
    y, ypre, conv_pre, states, (g_wout, wff1_full, g_wff2) = _mixer_fwd(
        proj, cw_full, conv_b, dt_bias_p, a_log_p, d_x, ssm_norm_w, pool_w[0], pool_scale, nb,
        [_mx(w_out[0]), _mx(w_ff1[0]), _mx(w_ff2[0])], late_cols)
    wout_full = g_wout.reshape(D_MIX, D_MODEL)
    wff2_full = g_wff2.reshape(D_FF, D_MODEL)
    loss, gr_nf, gr_nffn, ff, da, hn2, dh1, dh2, dy = _ffn_fwd_bwd(
        x_rows, lead, y, tgt_rows, wout_full, norm_ffn_w, wff1_full, wff2_full, norm_f_row)
    gr_wff2 = _tn_matmul(ff, dh2, "grad_w_ff2", tka=1024, max_slab=2816)
    gr_wff1_t = _tn_matmul(da, hn2, "grad_w_ff1", tka=1024, max_slab=2816)
    gr_wout = _tn_matmul(y, dh1, "grad_w_out", tka=1024, max_slab=2816)

    by_owner = lambda k: [owners(gr_wout[k].reshape(N_DEV, D_MIX // N_DEV, D_MODEL)),
                          owners(gr_wff1_t[k].reshape(N_DEV, D_FF // N_DEV, D_MODEL)),
                          owners(gr_wff2[k].reshape(N_DEV, D_FF // N_DEV, D_MODEL))]
    late_parts = by_owner(0)
    late_got = _grad_exchange_d2d(by_owner(1), [False] * 3, "grad_exchange_d2d_late")
    late_sums = [_chip_sum(late_parts[t], False, late_got[t], core, "chip_sum_late_%d" % t) for t in range(3)]
    (dproj, gr_cw, gr_cb, gr_dtb, gr_alog, gr_d, gr_nw, gr_pw, gr_ps), late_exchanged = _mixer_bwd(
        proj, dy, ypre, conv_pre, states, cw_full, dt_bias_p, a_log_p, d_x, ssm_norm_w, pool_w[0], pool_scale, nb,
        late_sums)

    win_parts = [owners(_tn_matmul_banded(hn1, dproj, "grad_w_in", win_cols, tka=512))]
    win_got = _grad_exchange_d2d(win_parts, [False], "grad_exchange_d2d_w_in")
    win_sum = _chip_sum(win_parts[0], False, win_got[0], core, "chip_sum_w_in")
    send_sems, recv_sems, win_sum, win_landing, started = _chip_exchange_start(win_sum, "w_in_exchange_start")
    gx_rows, gr_nmix, gr_meta = _in_proj_bwd(dproj, x_rows, lead, dh1, norm_mix_w, win_full, started)
    win_sum, win_landed = _chip_exchange_wait(send_sems, recv_sems, win_sum, win_landing, gr_nmix, "w_in_exchange_wait")
    parts = dict(w_in=win_landed, w_out=late_exchanged[0], w_ff1=late_exchanged[1], w_ff2=late_exchanged[2])
    my_chip = jnp.reshape(2 * xi + yi, (1,)).astype(jnp.int32)

    small_full = dict(meta=gr_meta, norm_mix_w=gr_nmix, pool_w=gr_pw, pool_scale=gr_ps, conv_w=gr_cw, conv_b=gr_cb,
                      dt_bias=gr_dtb[:, :N_HEADS], a_log=gr_alog[:, :N_HEADS], d_skip=gr_d[:, :N_HEADS],
                      ssm_norm_w=gr_nw, norm_ffn_w=gr_nffn, norm_f_w=gr_nf, loss=loss[0:1, 0:1])
    small_names = list(small_full)
    small_sum = _small_allreduce(_pack([small_full[n] for n in small_names]))
    gs = dict(zip(small_names, _unpack(small_sum, [small_full[n].shape for n in small_names])))
    gs["meta"] = lax.dynamic_slice_in_dim(gs["meta"], dev * 128, 128, axis=1)
    gs["conv_w"] = lax.dynamic_slice_in_dim(gs["conv_w"], dev * cw_cols, cw_cols, axis=1)

    res = {}
    for n in _BIG:
        shp = wts[n].shape
        args = (parts[n], wts[n][0], mom1[n][0], mom2[n][0], "adamw_" + n)
        if n == "w_ff1":
            outs = _adamw_transposed(*args)
        elif n == "w_in":
            outs = _adamw(*args, own=win_sum, chip=my_chip)
        else:
            outs = _adamw(*args)
        res[n] = [o.reshape(shp) for o in outs]
    as2d = lambda a: a.reshape(-1, a.shape[-1])
    small_g = [as2d(gs[n].reshape(wts[n].shape)) for n in _SMALL]
    small_out = _adamw_small(small_g, *[[as2d(d[n]) for n in _SMALL] for d in (wts, mom1, mom2)])
    for k, n in enumerate(_SMALL):
        res[n] = [o[k].reshape(wts[n].shape) for o in (small_g,) + tuple(small_out)]

    grad_x = gx_rows.reshape(nb, seq, D_MODEL)
    return (gs["loss"][0, 0], grad_x, *[res[n][0] for n in _WEIGHTS], *[res[n][1] for n in _WEIGHTS],
            *[res[n][2] for n in _WEIGHTS], *[res[n][3] for n in _WEIGHTS])
```

```python
import numpy as np
import jax
import jax.numpy as jnp
from jax import lax
from jax.experimental import pallas as pl
from jax.experimental.pallas import tpu as pltpu

F32 = jnp.float32
MXU_DTYPE = jnp.bfloat16

D_MODEL = 1024
D_POOL = 512
D_SSM = 1536
D_XBC = 2560
N_HEADS = 24
HEAD_DIM = 64
N_GROUPS = 4
GROUP_CH = D_SSM // N_GROUPS
D_STATE = 128
CHUNK = 128
N_META = 16
LEAD = CHUNK
PAD_ROWS = LEAD - N_META
ROW_TILE = 2 * CHUNK
D_MIX = D_POOL + D_SSM
D_FF = 4096
PROJ_W = 4736
OFF_Z = D_POOL
OFF_X = D_POOL + D_SSM
OFF_DT = OFF_X + D_XBC
D_IN_PROJ = OFF_DT + N_HEADS
POOL_WINDOWS = (2, 4, 8, 16)
HALO = 16
EPS = 1e-5
N_DEV = 8

ADAM_LR, ADAM_B1, ADAM_B2, ADAM_EPS, ADAM_WD, ADAM_STEP = 0.001, 0.9, 0.999, 1e-08, 0.01, 10

VMEM_LIMIT = 60 * 1024 * 1024


def _mx(a):
    return a.astype(MXU_DTYPE)


def _dot(a, b):
    return jnp.dot(_mx(a), _mx(b), preferred_element_type=F32)


def _dot_nt(a, b):
    return lax.dot_general(_mx(a), _mx(b), (((1,), (1,)), ((), ())), preferred_element_type=F32)


def _dot_tn(a, b):
    return lax.dot_general(_mx(a), _mx(b), (((0,), (0,)), ((), ())), preferred_element_type=F32)


def _split3(x):
    hi = x.astype(MXU_DTYPE)
    r = x - hi.astype(F32)
    mid = r.astype(MXU_DTYPE)
    lo = (r - mid.astype(F32)).astype(MXU_DTYPE)
    return hi, mid, lo


def _exact_l(c, x):
    hi, mid, lo = _split3(x)
    f = lambda p: jnp.dot(c, p, preferred_element_type=F32)
    return f(hi) + f(mid) + f(lo)


def _exact_r(x, c):
    hi, mid, lo = x if isinstance(x, tuple) else _split3(x)
    f = lambda p: jnp.dot(p, c, preferred_element_type=F32)
    return f(hi) + f(mid) + f(lo)


def _contract(x, c):
    hi = x.astype(MXU_DTYPE)
    lo = (x - hi.astype(F32)).astype(MXU_DTYPE)
    return jnp.dot(hi, c, preferred_element_type=F32) + jnp.dot(lo, c, preferred_element_type=F32)


def _sigmoid(x):
    return jax.nn.sigmoid(x)


def _softplus(x):
    return jnp.maximum(x, 0.0) + jnp.log1p(jnp.exp(-jnp.abs(x)))


def _silu_and_grad(x):
    s = _sigmoid(x)
    y = x * s
    return y, s + y * (1.0 - s)


def _shift_up(ext, s):
    if s == 0:
        return ext[:CHUNK, :]
    return pltpu.roll(ext, ext.shape[0] - s, 0)[:CHUNK, :]


def _by_pool_group(lane, a2, a4, a8, a16):
    return jnp.where(lane < 128, a2, jnp.where(lane < 256, a4, jnp.where(lane < 384, a8, a16)))


def _pool_inv_count(chunk_idx):
    row = lax.broadcasted_iota(jnp.int32, (CHUNK, D_POOL), 0)
    lane = lax.broadcasted_iota(jnp.int32, (CHUNK, D_POOL), 1)
    pos1 = jnp.maximum(chunk_idx * CHUNK + row - (PAD_ROWS - 1), 1)
    w = _by_pool_group(lane, 2, 4, 8, 16)
    return 1.0 / jnp.minimum(pos1, w).astype(F32), lane


def _pool_window_sums(u_ext, lane):
    s2 = u_ext + pltpu.roll(u_ext, 1, 0)
    s4 = s2 + pltpu.roll(s2, 2, 0)
    s8 = s4 + pltpu.roll(s4, 4, 0)
    s16 = s8 + pltpu.roll(s8, 8, 0)
    return _by_pool_group(lane, s2[HALO:], s4[HALO:], s8[HALO:], s16[HALO:])


def _pool_window_sums_ahead(q_ext, lane):
    n = q_ext.shape[0]
    r2 = q_ext + pltpu.roll(q_ext, n - 1, 0)
    r4 = r2 + pltpu.roll(r2, n - 2, 0)
    r8 = r4 + pltpu.roll(r4, n - 4, 0)
    r16 = r8 + pltpu.roll(r8, n - 8, 0)
    return _by_pool_group(lane, r2[:CHUNK], r4[:CHUNK], r8[:CHUNK], r16[:CHUNK])


def _conv_pre(ext, xbc, cw, cb):
    s1 = pltpu.roll(ext, 1, 0)
    near = cw[3:4, :] * xbc + cw[2:3, :] * s1[HALO:, :]
    far = cw[1:2, :] * ext + cw[0:1, :] * s1
    return cb + near + pltpu.roll(far, 2, 0)[HALO:, :]


def _dt_and_cumsum(dtr, dt_bias, a_log, valid, tril):
    lane = lax.broadcasted_iota(jnp.int32, (CHUNK, 128), 1)
    head = lane < N_HEADS
    pre = dtr + dt_bias
    dt = jnp.where(valid & head, _softplus(pre), 0.0)
    a_row = jnp.where(head[0:1, :], -jnp.exp(a_log), 0.0)
    a_col = _exact_l(tril, dt * a_row)
    return dt, a_row, a_col, pre, head


def _decay(a_col, a_row_t, h, causal):
    seg = a_col[:, h:h + 1] - a_row_t[h:h + 1, :]
    return jnp.where(causal, jnp.exp(jnp.minimum(seg, 0.0)), 0.0)


def _ssd_chunk_fwd(xs, bm, cm, dt, a_col, s_prev, d_x, e_mat, et_f32):
    lane = lax.broadcasted_iota(jnp.int32, (CHUNK, 128), 1)
    rowi = lax.broadcasted_iota(jnp.int32, (CHUNK, CHUNK), 0)
    coli = lax.broadcasted_iota(jnp.int32, (CHUNK, CHUNK), 1)
    causal = rowi >= coli
    a_row_t = a_col.T
    ax = _exact_r(a_col, e_mat)
    dtx = _exact_r(dt, e_mat)
    xdt = xs * dtx
    ax_last = ax[CHUNK - 1:CHUNK, :]
    e_a = jnp.exp(ax)
    w_end = xdt * jnp.exp(ax_last - ax)
    cd_col = jnp.exp(jnp.sum(et_f32 * a_col[CHUNK - 1:CHUNK, :], axis=1, keepdims=True))
    ys, s_new = [], []
    for g in range(N_GROUPS):
        gs = slice(g * GROUP_CH, (g + 1) * GROUP_CH)
        bg = bm[:, g * D_STATE:(g + 1) * D_STATE]
        cg = cm[:, g * D_STATE:(g + 1) * D_STATE]
        sg = s_prev[gs, :]
        cb = _dot_nt(cg, bg)
        y_off = _dot_nt(cg, sg) * e_a[:, gs]
        s_new.append(sg * cd_col[gs, :] + _dot_tn(w_end[:, gs], bg))
        for pr in range(3):
            c0 = g * GROUP_CH + pr * 128
            xdt_p = xdt[:, c0:c0 + 128]
            h0 = g * 6 + pr * 2
            y0 = _dot(cb * _decay(a_col, a_row_t, h0, causal), xdt_p)
            y1 = _dot(cb * _decay(a_col, a_row_t, h0 + 1, causal), xdt_p)
            ys.append(jnp.where(lane < HEAD_DIM, y0, y1) + y_off[:, pr * 128:(pr + 1) * 128])
    y = jnp.concatenate(ys, axis=1) + d_x * xs
    return y, jnp.concatenate(s_new, axis=0)


def _ssd_chunk_bwd(xs, bm, cm, dt, a_row, a_col, s_prev, ds_new, dy, d_x, e_mat, et_mat, et_f32, triu):
    lane = lax.broadcasted_iota(jnp.int32, (CHUNK, 128), 1)
    sub = lax.broadcasted_iota(jnp.int32, (CHUNK, 128), 0)
    rowi = lax.broadcasted_iota(jnp.int32, (CHUNK, CHUNK), 0)
    coli = lax.broadcasted_iota(jnp.int32, (CHUNK, CHUNK), 1)
    causal = rowi >= coli
    a_row_t = a_col.T
    a_last = a_col[CHUNK - 1:CHUNK, :]
    a_split, dt_split = _split3(a_col), _split3(dt)
    sub8 = lax.broadcasted_iota(jnp.int32, (8, GROUP_CH), 0)

    dxs, dbs, dcs, dsp = [], [], [], []
    zcol = jnp.zeros((CHUNK, 128), F32)
    zrows = []
    da_col = jnp.zeros((CHUNK, 128), F32)
    ddt = jnp.zeros((CHUNK, 128), F32)
    head_sums = jnp.zeros((8, 128), F32)
    q_row = jnp.zeros((1, 128), F32)
    for g in range(N_GROUPS):
        gs = slice(g * GROUP_CH, (g + 1) * GROUP_CH)
        e_g, et_g = e_mat[:, gs], et_mat[gs, :]
        xs_g, dy_g = xs[:, gs], dy[:, gs]
        ax = _exact_r(a_split, e_g)
        dtx = _exact_r(dt_split, e_g)
        xdt = xs_g * dtx
        dte = jnp.exp(ax[CHUNK - 1:CHUNK, :] - ax)
        w_end = xdt * dte
        cd_col = jnp.exp(jnp.sum(et_f32[gs, :] * a_last, axis=1, keepdims=True))
        dye = dy_g * jnp.exp(ax)
        bg = bm[:, g * D_STATE:(g + 1) * D_STATE]
        cg = cm[:, g * D_STATE:(g + 1) * D_STATE]
        sg = s_prev[gs, :]
        dsg = ds_new[gs, :]
        cb = _dot_nt(cg, bg)
        cs = _dot_nt(cg, sg)
        dcg = _dot(dye, sg)
        dsp.append(dsg * cd_col + _dot_tn(dye, cg))
        dwg = _dot_nt(bg, dsg)
        dbg = _dot(w_end, dsg)
        ww = dwg * w_end
        t1 = jnp.sum(dsg * sg, axis=1, keepdims=True) * cd_col
        dcb = jnp.zeros((CHUNK, CHUNK), F32)
        pairs = []
        for pr in range(3):
            ps = slice(pr * 128, (pr + 1) * 128)
            xdt_p, dy_p = xdt[:, ps], dy_g[:, ps]
            acc = None
            for half in range(2):
                h = g * 6 + pr * 2 + half
                ld = _decay(a_col, a_row_t, h, causal)
                gm = cb * ld
                dym = jnp.where((lane < HEAD_DIM) if half == 0 else (lane >= HEAD_DIM), dy_p, 0.0)
                dg = _dot_nt(dym, xdt_p)
                dseg = dg * gm
                dcb = dcb + dg * ld
                t = _dot_tn(gm, dym)
                acc = t if acc is None else acc + t
                zcol = jnp.where(lane == h, jnp.sum(dseg, axis=1, keepdims=True), zcol)
                zrows.append(jnp.sum(dseg, axis=0, keepdims=True))
            pairs.append(acc)
        dxdt = dwg * dte + jnp.concatenate(pairs, axis=1)
        dcs.append(dcg + _dot(dcb, bg))
        dbs.append(dbg + _dot_tn(dcb, cg))
        dxs.append(dxdt * dtx + d_x[:, gs] * dy_g)
        da_col = da_col + _contract(dye * cs - ww, et_g)
        ddt = ddt + _contract(dxdt * xs_g, et_g)
        col_sums = jnp.where(sub8 == 0, jnp.sum(dy_g * xs_g, axis=0, keepdims=True),
                             jnp.where(sub8 == 1, jnp.sum(ww, axis=0, keepdims=True), 0.0))
        head_sums = head_sums + _exact_r(col_sums, et_g)
        q_row = q_row + jnp.sum(et_f32[gs, :] * t1, axis=0, keepdims=True)

    dd = head_sums[0:1, :]
    q_row = q_row + head_sums[1:2, :]
    zrow = jnp.concatenate(zrows + [jnp.zeros((128 - N_HEADS, CHUNK), F32)], axis=0)
    da_col = da_col + zcol - zrow.T + jnp.where(sub == CHUNK - 1, q_row, 0.0)
    rc = _exact_l(triu, da_col)
    ddt = ddt + rc * a_row
    da = jnp.sum(rc * dt, axis=0, keepdims=True)
    return (jnp.concatenate(dxs, axis=1), jnp.concatenate(dbs, axis=1), jnp.concatenate(dcs, axis=1), ddt, da, dd,
            jnp.concatenate(dsp, axis=0))


_SSD_CONSTANT_SHAPES = dict(e=((128, D_SSM), MXU_DTYPE), et=((D_SSM, 128), MXU_DTYPE), et_f32=((D_SSM, 128), F32),
                            tril=((CHUNK, CHUNK), MXU_DTYPE), triu=((CHUNK, CHUNK), MXU_DTYPE))


def _ssd_constant_scratch(names):
    return [pltpu.VMEM(*_SSD_CONSTANT_SHAPES[n]) for n in names]


def _fill_ssd_constants(**refs):
    iota = lambda shape, d: lax.broadcasted_iota(jnp.int32, shape, d)
    shift = HEAD_DIM.bit_length() - 1
    marks = dict(
        e=lambda: iota((128, D_SSM), 0) == (iota((128, D_SSM), 1) >> shift),
        et=lambda: iota((D_SSM, 128), 1) == (iota((D_SSM, 128), 0) >> shift),
        et_f32=lambda: iota((D_SSM, 128), 1) == (iota((D_SSM, 128), 0) >> shift),
        tril=lambda: iota((CHUNK, CHUNK), 1) <= iota((CHUNK, CHUNK), 0),
        triu=lambda: iota((CHUNK, CHUNK), 1) >= iota((CHUNK, CHUNK), 0))
    for name, ref in refs.items():
        ref[...] = jnp.where(marks[name](), 1.0, 0.0).astype(ref.dtype)


def _row_views(ref, widths):
    views, off = [], 0
    for w in widths:
        views.append(ref.at[:, off:off + w])
        off += w
    return views


def _full(shape):
    nd = len(shape)
    return pl.BlockSpec(shape, lambda *_: (0,) * nd)


def _params(*sem):
    return pltpu.CompilerParams(dimension_semantics=sem, vmem_limit_bytes=VMEM_LIMIT)


def _token_tiles(width, n_tok_tiles):
    return pl.BlockSpec((ROW_TILE, width), lambda i: (jnp.minimum(i, n_tok_tiles - 1), 0))


def _in_proj(x, lead, w1, win):
    nt = x.shape[0] // ROW_TILE
    m = x.shape[0] + ROW_TILE
    tm = ROW_TILE

    def body(x_ref, lead_ref, w1_ref, win_hbm, hn_ref, proj_ref, win_v, sem):
        i = pl.program_id(0)

        @pl.when(i == 0)
        def _():
            cp = pltpu.make_async_copy(win_hbm, win_v, sem)
            cp.start()
            cp.wait()

        x = jnp.where(i == nt, lead_ref[...], x_ref[...])
        r = lax.rsqrt(jnp.mean(x * x, axis=-1, keepdims=True) + EPS)
        hn = _mx(x * r * w1_ref[...])
        hn_ref[...] = hn
        for j in range(0, PROJ_W, 512):
            w = min(512, PROJ_W - j)
            proj_ref[:, j:j + w] = jnp.dot(hn, win_v[:, j:j + w], preferred_element_type=F32)

    return pl.pallas_call(
        body, grid=(m // tm,), name="in_proj",
        in_specs=[_token_tiles(D_MODEL, nt), _full((ROW_TILE, D_MODEL)), _full((1, D_MODEL)),
                  pl.BlockSpec(memory_space=pl.ANY)],
        out_specs=[pl.BlockSpec((tm, D_MODEL), lambda i: (i, 0)), pl.BlockSpec((tm, PROJ_W), lambda i: (i, 0))],
        out_shape=[jax.ShapeDtypeStruct((m, D_MODEL), MXU_DTYPE), jax.ShapeDtypeStruct((m, PROJ_W), F32)],
        scratch_shapes=[pltpu.VMEM((D_MODEL, PROJ_W), MXU_DTYPE), pltpu.SemaphoreType.DMA],
        compiler_params=_params("arbitrary"),
    )(x, lead, w1, win)


def _ffn_fwd_bwd(x, lead, y, tgt, wout, w2n, wff1, wff2, wfn):
    nt = x.shape[0] // ROW_TILE
    m = x.shape[0] + ROW_TILE
    tm = ROW_TILE
    nj = D_FF // 1024

    def body(x_ref, lead_ref, y_ref, tgt_ref, w2n_ref, wfn_ref, wout_hbm, wff1_hbm, wff2_hbm,
             loss_ref, gwf_ref, gw2_ref, ff_ref, da_ref, hn2_ref, dh1_ref, dh2_ref, dy_ref,
             wout_v, wff1_v, wff2_v, a_s, sems):
        i = pl.program_id(0)
        hp = jnp.where(i == nt, lead_ref[...], x_ref[...])

        @pl.when(i == 0)
        def _():
            cps = [pltpu.make_async_copy(s, d, sems.at[k])
                   for k, (s, d) in enumerate(((wout_hbm, wout_v), (wff1_hbm, wff1_v), (wff2_hbm, wff2_v)))]
            for cp in cps:
                cp.start()
            for cp in cps:
                cp.wait()
            loss_ref[...] = jnp.zeros_like(loss_ref)
            gwf_ref[...] = jnp.zeros_like(gwf_ref)
            gw2_ref[...] = jnp.zeros_like(gw2_ref)

        h1 = hp + jnp.dot(y_ref[...], wout_v[...], preferred_element_type=F32)
        r2 = lax.rsqrt(jnp.mean(h1 * h1, axis=-1, keepdims=True) + EPS)
        n2 = h1 * r2
        w2n_row = w2n_ref[...]
        hn2 = _mx(n2 * w2n_row)
        hn2_ref[...] = hn2
        h2 = h1
        for j in range(nj):
            js = slice(j * 1024, (j + 1) * 1024)
            a = jnp.dot(hn2, wff1_v[:, js], preferred_element_type=F32)
            a_s[:, js] = a
            ra = jnp.maximum(a, 0.0)
            ff = _mx(ra * ra)
            ff_ref[:, js] = ff
            h2 = h2 + jnp.dot(ff, wff2_v[js, :], preferred_element_type=F32)

        r3 = lax.rsqrt(jnp.mean(h2 * h2, axis=-1, keepdims=True) + EPS)
        n3 = h2 * r3
        wf_row = wfn_ref[...]
        err = n3 * wf_row - tgt_ref[...]
        tokf = (i < nt).astype(F32)
        loss_ref[...] += 0.5 * jnp.sum(jnp.mean(err * err, axis=-1, keepdims=True) * tokf)
        dout = err * (tokf / D_MODEL)
        gwf_ref[...] += jnp.sum(dout * n3, axis=0, keepdims=True)
        dn3 = dout * wf_row
        dh2 = r3 * (dn3 - n3 * jnp.mean(dn3 * n3, axis=-1, keepdims=True))
        dh2m = _mx(dh2)
        dh2_ref[...] = dh2m

        dhn2 = jnp.zeros((tm, D_MODEL), F32)
        for j in range(nj):
            js = slice(j * 1024, (j + 1) * 1024)
            dff = lax.dot_general(dh2m, wff2_v[js, :], (((1,), (1,)), ((), ())), preferred_element_type=F32)
            da = _mx(dff * (2.0 * jnp.maximum(a_s[:, js], 0.0)))
            da_ref[:, js] = da
            dhn2 = dhn2 + lax.dot_general(da, wff1_v[:, js], (((1,), (1,)), ((), ())), preferred_element_type=F32)
        gw2_ref[...] += jnp.sum(dhn2 * n2, axis=0, keepdims=True)
        dn2 = dhn2 * w2n_row
        dh1 = dh2 + r2 * (dn2 - n2 * jnp.mean(dn2 * n2, axis=-1, keepdims=True))
        dh1_ref[...] = dh1
        dy_ref[...] = lax.dot_general(_mx(dh1), wout_v[...], (((1,), (1,)), ((), ())), preferred_element_type=F32)

    rows = lambda w: pl.BlockSpec((tm, w), lambda i: (i, 0))
    hbm = pl.BlockSpec(memory_space=pl.ANY)
    return pl.pallas_call(
        body, grid=(m // tm,), name="ffn_fwd_bwd",
        in_specs=[_token_tiles(D_MODEL, nt), _full((ROW_TILE, D_MODEL)), rows(D_MIX), _token_tiles(D_MODEL, nt),
                  _full((1, D_MODEL)), _full((1, D_MODEL)), hbm, hbm, hbm],
        out_specs=[_full((1, 128)), _full((1, D_MODEL)), _full((1, D_MODEL)), rows(D_FF), rows(D_FF), rows(D_MODEL),
                   rows(D_MODEL), rows(D_MODEL), rows(D_MIX)],
        out_shape=[jax.ShapeDtypeStruct((1, 128), F32), jax.ShapeDtypeStruct((1, D_MODEL), F32),
                   jax.ShapeDtypeStruct((1, D_MODEL), F32), jax.ShapeDtypeStruct((m, D_FF), MXU_DTYPE),
                   jax.ShapeDtypeStruct((m, D_FF), MXU_DTYPE), jax.ShapeDtypeStruct((m, D_MODEL), MXU_DTYPE),
                   jax.ShapeDtypeStruct((m, D_MODEL), F32), jax.ShapeDtypeStruct((m, D_MODEL), MXU_DTYPE),
                   jax.ShapeDtypeStruct((m, D_MIX), F32)],
        scratch_shapes=[pltpu.VMEM((D_MIX, D_MODEL), MXU_DTYPE), pltpu.VMEM((D_MODEL, D_FF), MXU_DTYPE),
                        pltpu.VMEM((D_FF, D_MODEL), MXU_DTYPE), pltpu.VMEM((tm, D_FF), F32),
                        pltpu.SemaphoreType.DMA((3,))],
        compiler_params=_params("arbitrary"),
    )(x, lead, y, tgt, w2n, wfn, wout, wff1, wff2)


def _in_proj_bwd(dproj, x, lead, dh1, w1, win, after):
    nt = x.shape[0] // ROW_TILE
    m = x.shape[0] + ROW_TILE
    tm = ROW_TILE

    def body(dp_ref, x_ref, lead_ref, dh1_ref, w1_ref, win_hbm, after_ref, gx_ref, gw1_ref, gmeta_ref, win_v, sem):
        i = pl.program_id(0)

        @pl.when(i == 0)
        def _():
            cp = pltpu.make_async_copy(win_hbm, win_v, sem)
            cp.start()
            cp.wait()
            gw1_ref[...] = jnp.zeros_like(gw1_ref)
            gmeta_ref[...] = jnp.zeros_like(gmeta_ref)

        dhn = lax.dot_general(dp_ref[...], win_v[...], (((1,), (1,)), ((), ())), preferred_element_type=F32)
        x = jnp.where(i == nt, lead_ref[...], x_ref[...])
        r = lax.rsqrt(jnp.mean(x * x, axis=-1, keepdims=True) + EPS)
        n = x * r
        gw1_ref[...] += jnp.sum(dhn * n, axis=0, keepdims=True)
        dn = dhn * w1_ref[...]
        dh0 = dh1_ref[...] + r * (dn - n * jnp.mean(dn * n, axis=-1, keepdims=True))

        @pl.when(i < nt)
        def _():
            gx_ref[...] = dh0

        @pl.when(i == nt)
        def _():
            gmeta_ref[...] = dh0[PAD_ROWS:LEAD, :] + dh0[LEAD + PAD_ROWS:2 * LEAD, :]

    rows = lambda w: pl.BlockSpec((tm, w), lambda i: (i, 0))
    hbm = pl.BlockSpec(memory_space=pl.ANY)
    return pl.pallas_call(
        body, grid=(m // tm,), name="in_proj_bwd",
        in_specs=[rows(PROJ_W), _token_tiles(D_MODEL, nt), _full((ROW_TILE, D_MODEL)), rows(D_MODEL),
                  _full((1, D_MODEL)), hbm, hbm],
        out_specs=[_token_tiles(D_MODEL, nt), _full((1, D_MODEL)), _full((N_META, D_MODEL))],
        out_shape=[jax.ShapeDtypeStruct(x.shape, F32), jax.ShapeDtypeStruct((1, D_MODEL), F32),
                   jax.ShapeDtypeStruct((N_META, D_MODEL), F32)],
        scratch_shapes=[pltpu.VMEM((D_MODEL, PROJ_W), MXU_DTYPE), pltpu.SemaphoreType.DMA],
        compiler_params=_params("arbitrary"),
    )(dproj, x, lead, dh1, w1, win, after)


MXU_DEPTH = 256


def _row_slab(m, cap):
    return max(k for k in range(MXU_DEPTH, cap + 1, MXU_DEPTH) if m % k == 0)


def _tn_matmul(a, b, name, tka, max_slab=768, tn=512):
    m, ka = a.shape
    nb = b.shape[1]
    tkm = _row_slab(m, max_slab)
    n_steps = m // tkm

    def body(a_ref, b_ref, o_ref, omx_ref):
        k = pl.program_id(1)

        @pl.when(k == 0)
        def _():
            o_ref[...] = jnp.zeros_like(o_ref)

        at = _mx(a_ref[...])
        for j in range(0, nb, tn):
            w = min(tn, nb - j)
            o_ref[:, j:j + w] += lax.dot_general(at, _mx(b_ref[:, j:j + w]), (((0,), (0,)), ((), ())),
                                                 preferred_element_type=F32)

        @pl.when(k == n_steps - 1)
        def _():
            omx_ref[...] = _mx(o_ref[...])

    out = pl.BlockSpec((tka, nb), lambda i, k: (i, 0))
    return pl.pallas_call(
        body, grid=(ka // tka, n_steps), name=name,
        in_specs=[pl.BlockSpec((tkm, tka), lambda i, k: (k, i)), pl.BlockSpec((tkm, nb), lambda i, k: (k, 0))],
        out_specs=[out, out],
        out_shape=[jax.ShapeDtypeStruct((ka, nb), F32), jax.ShapeDtypeStruct((ka, nb), MXU_DTYPE)],
        compiler_params=_params("arbitrary", "arbitrary"),
    )(a, b)


def _tn_matmul_banded(a, b, name, band, tka, tn=512):
    m, ka = a.shape
    nb = b.shape[1]
    tkm = _row_slab(m, 768)
    n_steps = m // tkm

    def body(a_ref, b_ref, o_ref, acc):
        k = pl.program_id(1)

        @pl.when(k == 0)
        def _():
            acc[...] = jnp.zeros_like(acc)

        at = _mx(a_ref[...])
        for j in range(0, nb, tn):
            w = min(tn, nb - j)
            acc[:, j:j + w] += lax.dot_general(at, _mx(b_ref[:, j:j + w]), (((0,), (0,)), ((), ())),
                                               preferred_element_type=F32)

        @pl.when(k == n_steps - 1)
        def _():
            for j in range(N_DEV):
                o_ref[j] = acc[:, j * band:(j + 1) * band]

    return pl.pallas_call(
        body, grid=(ka // tka, n_steps), name=name,
        in_specs=[pl.BlockSpec((tkm, tka), lambda i, k: (k, i)), pl.BlockSpec((tkm, nb), lambda i, k: (k, 0))],
        out_specs=pl.BlockSpec((N_DEV, tka, band), lambda i, k: (0, i, 0)),
        out_shape=jax.ShapeDtypeStruct((N_DEV, ka, band), F32),
        scratch_shapes=[pltpu.VMEM((tka, nb), F32)],
        compiler_params=_params("arbitrary", "arbitrary"),
    )(a, b)


def _assemble_bands(g, width):
    n, rows, band = g.shape
    tr = 256

    def body(g_ref, o_ref):
        parts = [g_ref[j] for j in range(n)] + [jnp.zeros((tr, width - n * band), g.dtype)]
        o_ref[...] = jnp.concatenate(parts, axis=1)

    return pl.pallas_call(
        body, grid=(rows // tr,), name="assemble_w_in",
        in_specs=[pl.BlockSpec((n, tr, band), lambda i: (0, i, 0))],
        out_specs=pl.BlockSpec((tr, width), lambda i: (i, 0)),
        out_shape=jax.ShapeDtypeStruct((rows, width), g.dtype),
        compiler_params=_params("arbitrary"),
    )(g)


def _chunk_block(b, c, nb, nc):
    return jnp.where(c == 0, nb * (nc - 1) + b, b * (nc - 1) + c - 1)

def _mixer_fwd(proj, cw, cb, dt_bias, a_log, d_x, nw, pool_w, pool_scale, nb, shards, by_cols):
    m = proj.shape[0]
    nc = m // nb // CHUNK
    n_steps = nb * nc
    ns = len(shards)
    row_widths = [D_XBC, 128, 128, D_SSM, D_SSM, D_POOL]
    rows = jnp.concatenate([cb, dt_bias, a_log, d_x, nw, pool_scale], axis=1)

    def body(p_ref, cw_ref, rows_ref, pw_ref, *rest):
        shard_refs, (y_ref, ypre_ref, pre_ref, st_ref) = rest[:ns], rest[ns:ns + 4]
        gathered_refs, (xtail, utail, state, e_ref, et_ref, tril_ref) = rest[ns + 4:2 * ns + 4], rest[2 * ns + 4:2 * ns + 10]
        gather = _Gather(shard_refs, gathered_refs, by_cols, *rest[2 * ns + 10:])
        cb_ref, dtb_ref, alog_ref, dx_ref, nw_ref, ps_ref = _row_views(rows_ref, row_widths)
        c = pl.program_id(1)
        step = pl.program_id(0) * nc + c

        @pl.when(step == 0)
        def _():
            gather.start()
            _fill_ssd_constants(e=e_ref, et_f32=et_ref, tril=tril_ref)

        @pl.when(step == n_steps // 2)
        def _():
            gather.forward()

        @pl.when(c == 0)
        def _():
            xtail[...] = jnp.zeros_like(xtail)
            utail[...] = jnp.zeros_like(utail)
            state[...] = jnp.zeros_like(state)

        valid = (c > 0) | (lax.broadcasted_iota(jnp.int32, (CHUNK, 1), 0) >= PAD_ROWS)

        u = p_ref[:, 0:D_POOL]
        inv_cnt, lane = _pool_inv_count(c)
        win = _pool_window_sums(jnp.concatenate([utail[...], u], axis=0), lane)
        utail[...] = u[CHUNK - HALO:, :]
        pooled = win * inv_cnt - u
        mixed = jnp.concatenate(
            [_dot(pooled[:, g * 128:(g + 1) * 128], pw_ref[g]) for g in range(len(POOL_WINDOWS))], axis=1)
        y_ref[:, 0:D_POOL] = _mx(mixed * ps_ref[...])

        xbc = p_ref[:, OFF_X:OFF_X + D_XBC]
        pre = _conv_pre(jnp.concatenate([xtail[...], xbc], axis=0), xbc, cw_ref[...], cb_ref[...])
        xtail[...] = xbc[CHUNK - HALO:, :]
        pre_ref[...] = pre
        xc = pre * _sigmoid(pre)
        dt, _, a_col, _, _ = _dt_and_cumsum(p_ref[:, OFF_DT:OFF_DT + 128], dtb_ref[...], alog_ref[...], valid,
                                            tril_ref[...])
        s_prev = state[...]
        st_ref[0] = s_prev
        yp, s_new = _ssd_chunk_fwd(xc[:, 0:D_SSM], xc[:, D_SSM:D_SSM + 512], xc[:, D_SSM + 512:], dt, a_col, s_prev,
                                   dx_ref[...], e_ref[...], et_ref[...])
        state[...] = s_new
        ypre_ref[...] = yp
        z = p_ref[:, OFF_Z:OFF_Z + D_SSM]
        yz = yp * (z * _sigmoid(z))
        outs = []
        for g in range(N_GROUPS):
            gs = slice(g * GROUP_CH, (g + 1) * GROUP_CH)
            r = lax.rsqrt(jnp.mean(yz[:, gs] * yz[:, gs], axis=-1, keepdims=True) + EPS)
            outs.append(yz[:, gs] * r)
        y_ref[:, D_POOL:] = _mx(jnp.concatenate(outs, axis=1) * nw_ref[...])

        @pl.when(step == n_steps - 1)
        def _():
            gather.finish()

    blk = lambda w: pl.BlockSpec((CHUNK, w), lambda b, c: (_chunk_block(b, c, nb, nc), 0))
    hbm = pl.BlockSpec(memory_space=pl.ANY)
    outs = pl.pallas_call(
        body, grid=(nb, nc), name="mixer_fwd",
        in_specs=[blk(PROJ_W), _full((4, D_XBC)), _full((1, sum(row_widths))), _full((4, 128, 128))] + [hbm] * ns,
        out_specs=[blk(D_MIX), blk(D_SSM), blk(D_XBC),
                   pl.BlockSpec((1, D_SSM, D_STATE), lambda b, c: (b * nc + c, 0, 0))] + [hbm] * ns,
        out_shape=[jax.ShapeDtypeStruct((m, D_MIX), MXU_DTYPE), jax.ShapeDtypeStruct((m, D_SSM), F32),
                   jax.ShapeDtypeStruct((m, D_XBC), F32), jax.ShapeDtypeStruct((m // CHUNK, D_SSM, D_STATE), F32)]
        + _Gather.out_shapes(shards, by_cols),
        scratch_shapes=[pltpu.VMEM((HALO, D_XBC), F32), pltpu.VMEM((HALO, D_POOL), F32),
                        pltpu.VMEM((D_SSM, D_STATE), F32)] + _ssd_constant_scratch(["e", "et_f32", "tril"])
        + _Gather.scratch(ns),
        compiler_params=_params("arbitrary", "arbitrary"),
    )(proj, cw, rows, pool_w, *shards)
    return outs[0], outs[1], outs[2], outs[3], outs[4:]


def _mixer_bwd(proj, dy, ypre, conv_pre, states, cw, dt_bias, a_log, d_x, nw, pool_w, pool_scale, nb, chip_sums):
    m = proj.shape[0]
    nc = m // nb // CHUNK
    hb = CHUNK // HALO
    ns = len(chip_sums)
    row_widths = [128, 128, D_SSM, D_SSM, D_POOL]
    rows = jnp.concatenate([dt_bias, a_log, d_x, nw, pool_scale], axis=1)
    grad_row_widths = [D_XBC, 128, 128, 128, D_SSM, D_POOL]
    constants = ["e", "et", "et_f32", "tril", "triu"]

    def body(p_ref, halo_ref, dy_ref, ypre_ref, pre_ref, st_ref, cw_ref, rows_ref, pw_ref, *rest):
        cs_refs = rest[:ns]
        dp_ref, gcw_ref, grows_ref, gpw_ref = rest[ns:ns + 4]
        part_refs, (ds_carry, dpre_next, dq_next) = rest[ns + 4:2 * ns + 4], rest[2 * ns + 4:2 * ns + 7]
        e_ref, et_ref, etf_ref, tril_ref, triu_ref = rest[2 * ns + 7:2 * ns + 12]
        exchange = _ChipExchange(cs_refs, part_refs, [False] * ns, *rest[2 * ns + 12:])
        dtb_ref, alog_ref, dx_ref, nw_ref, ps_ref = _row_views(rows_ref, row_widths)
        gcb_ref, gdtb_ref, galog_ref, gd_ref, gnw_ref, gps_ref = _row_views(grows_ref, grad_row_widths)
        b = pl.program_id(0)
        cc = pl.program_id(1)
        c = nc - 1 - cc

        @pl.when((b == 0) & (cc == 0))
        def _():
            exchange.start()
            _fill_ssd_constants(e=e_ref, et=et_ref, et_f32=etf_ref, tril=tril_ref, triu=triu_ref)
            for r in (gcw_ref, grows_ref, gpw_ref):
                r[...] = jnp.zeros_like(r)

        @pl.when(cc == 0)
        def _():
            ds_carry[...] = jnp.zeros_like(ds_carry)
            dpre_next[...] = jnp.zeros_like(dpre_next)
            dq_next[...] = jnp.zeros_like(dq_next)

        valid = (c > 0) | (lax.broadcasted_iota(jnp.int32, (CHUNK, 1), 0) >= PAD_ROWS)
        first = c > 0

        u = p_ref[:, 0:D_POOL]
        u_halo = jnp.where(first, halo_ref[...], 0.0)
        inv_cnt, lane = _pool_inv_count(c)
        pooled = _pool_window_sums(jnp.concatenate([u_halo, u], axis=0), lane) * inv_cnt - u
        dyp = dy_ref[:, 0:D_POOL]
        ps = ps_ref[...]
        dmixed = dyp * ps
        mixed, dpooled = [], []
        for g in range(len(POOL_WINDOWS)):
            gsl = slice(g * 128, (g + 1) * 128)
            pw = pw_ref[g]
            mixed.append(_dot(pooled[:, gsl], pw))
            dpooled.append(_dot_nt(dmixed[:, gsl], pw))
            gpw_ref[g] += _dot_tn(pooled[:, gsl], dmixed[:, gsl])
        gps_ref[...] += jnp.sum(dyp * jnp.concatenate(mixed, axis=1), axis=0, keepdims=True)
        dpooled = jnp.concatenate(dpooled, axis=1)
        dq = dpooled * inv_cnt
        du = _pool_window_sums_ahead(jnp.concatenate([dq, dq_next[...]], axis=0), lane) - dpooled
        dq_next[...] = dq[0:HALO, :]
        dp_ref[:, 0:D_POOL] = _mx(du)

        yp = ypre_ref[...]
        z = p_ref[:, OFF_Z:OFF_Z + D_SSM]
        sz, dsz = _silu_and_grad(z)
        yz = yp * sz
        do = dy_ref[:, D_POOL:]
        nw_row = nw_ref[...]
        dyz = []
        gnw = []
        for g in range(N_GROUPS):
            gs = slice(g * GROUP_CH, (g + 1) * GROUP_CH)
            r = lax.rsqrt(jnp.mean(yz[:, gs] * yz[:, gs], axis=-1, keepdims=True) + EPS)
            n = yz[:, gs] * r
            gnw.append(jnp.sum(do[:, gs] * n, axis=0, keepdims=True))
            dn = do[:, gs] * nw_row[:, gs]
            dyz.append(r * (dn - n * jnp.mean(dn * n, axis=-1, keepdims=True)))
        gnw_ref[...] += jnp.concatenate(gnw, axis=1)
        dyz = jnp.concatenate(dyz, axis=1)
        dp_ref[:, OFF_Z:OFF_Z + D_SSM] = _mx(dyz * yp * dsz)
        dyp_ssm = dyz * sz

        xc, dsilu = _silu_and_grad(pre_ref[...])
        dtr = p_ref[:, OFF_DT:OFF_DT + 128]
        dt, a_row, a_col, dt_pre, head = _dt_and_cumsum(dtr, dtb_ref[...], alog_ref[...], valid, tril_ref[...])
        dxs, dbm, dcm, ddt, da, dd, ds_prev = _ssd_chunk_bwd(
            xc[:, 0:D_SSM], xc[:, D_SSM:D_SSM + 512], xc[:, D_SSM + 512:], dt, a_row, a_col, st_ref[0],
            ds_carry[...], dyp_ssm, dx_ref[...], e_ref[...], et_ref[...], etf_ref[...], triu_ref[...])
        ds_carry[...] = ds_prev
        gd_ref[...] += dd
        galog_ref[...] += da * a_row
        ddtr = jnp.where(valid & head, ddt * _sigmoid(dt_pre), 0.0)
        gdtb_ref[...] += jnp.sum(ddtr, axis=0, keepdims=True)
        dp_ref[:, OFF_DT:OFF_DT + 128] = _mx(ddtr)

        dpre = jnp.concatenate([dxs, dbm, dcm], axis=1) * dsilu
        gcb_ref[...] += jnp.sum(dpre, axis=0, keepdims=True)
        dext = jnp.concatenate([dpre, dpre_next[...]], axis=0)
        dpre_next[...] = dpre[0:HALO, :]
        ups = [_shift_up(dext, 3 - k) for k in range(4)]
        xbc = p_ref[:, OFF_X:OFF_X + D_XBC]
        gcw_ref[...] += jnp.concatenate([jnp.sum(xbc * ups[k], axis=0, keepdims=True) for k in range(4)], axis=0)
        cw = cw_ref[...]
        dp_ref[:, OFF_X:OFF_X + D_XBC] = _mx(cw[3:4, :] * ups[3] + cw[2:3, :] * ups[2]
                                             + cw[1:2, :] * ups[1] + cw[0:1, :] * ups[0])

        @pl.when((b == nb - 1) & (cc == nc - 1))
        def _():
            exchange.finish()

    blk = lambda w: pl.BlockSpec((CHUNK, w), lambda b, cc: (_chunk_block(b, nc - 1 - cc, nb, nc), 0))
    halo = pl.BlockSpec((HALO, D_POOL),
                        lambda b, cc: (_chunk_block(b, jnp.maximum(nc - 2 - cc, 0), nb, nc) * hb + hb - 1, 0))
    hbm = pl.BlockSpec(memory_space=pl.ANY)
    outs = pl.pallas_call(
        body, grid=(nb, nc), name="mixer_bwd",
        in_specs=[blk(PROJ_W), halo, blk(D_MIX), blk(D_SSM), blk(D_XBC),
                  pl.BlockSpec((1, D_SSM, D_STATE), lambda b, cc: (b * nc + nc - 1 - cc, 0, 0)),
                  _full((4, D_XBC)), _full((1, sum(row_widths))), _full((4, 128, 128))] + [hbm] * ns,
        out_specs=[blk(PROJ_W), _full((4, D_XBC)), _full((1, sum(grad_row_widths))), _full((4, 128, 128))] + [hbm] * ns,
        out_shape=[jax.ShapeDtypeStruct((m, PROJ_W), MXU_DTYPE), jax.ShapeDtypeStruct((4, D_XBC), F32),
                   jax.ShapeDtypeStruct((1, sum(grad_row_widths)), F32), jax.ShapeDtypeStruct((4, 128, 128), F32)]
        + _ChipExchange.out_shapes(chip_sums, [False] * ns),
        scratch_shapes=[pltpu.VMEM((D_SSM, D_STATE), F32), pltpu.VMEM((HALO, D_XBC), F32),
                        pltpu.VMEM((HALO, D_POOL), F32)] + _ssd_constant_scratch(constants) + _ChipExchange.scratch(ns),
        compiler_params=_params("arbitrary", "arbitrary"),
    )(proj, proj, dy, ypre, conv_pre, states, cw, rows, pool_w, *chip_sums)
    dproj, g_cw, g_rows, g_pw = outs[:4]
    offs = np.cumsum([0] + grad_row_widths)
    g_cb, g_dtb, g_alog, g_d, g_nw, g_ps = (g_rows[:, a:b] for a, b in zip(offs[:-1], offs[1:]))
    return (dproj, g_cw, g_cb, g_dtb, g_alog, g_d, g_nw, g_pw, g_ps), outs[4:]


MESH_IDS = pl.DeviceIdType.MESH
_HBM = pl.BlockSpec(memory_space=pltpu.HBM)


def _coords():
    return lax.axis_index("x"), lax.axis_index("y"), lax.axis_index("c")


def _other_chips(x, y):
    return [(1 - x, y), (x, 1 - y), (1 - x, 1 - y)]


class _Gather:
    def __init__(self, ins, outs, by_cols, send_sems, recv_sems, local_sems):
        self.ins, self.outs, self.by_cols, self.n = ins, outs, by_cols, len(ins)
        self.send_sems, self.recv_sems, self.local_sems = send_sems, recv_sems, local_sems
        self.x, self.y, self.c = _coords()
        self.me, self.sibling = (self.x, self.y, self.c), (self.x, self.y, 1 - self.c)
        self.chips = _other_chips(self.x, self.y)

    @staticmethod
    def scratch(n):
        return [pltpu.SemaphoreType.DMA((7 * n,)), pltpu.SemaphoreType.DMA((7 * n,)), pltpu.SemaphoreType.DMA((n,))]

    @staticmethod
    def out_shapes(shards, by_cols):
        return [jax.ShapeDtypeStruct((s.shape[0], N_DEV * s.shape[1]) if cols else (N_DEV,) + s.shape, s.dtype)
                for s, cols in zip(shards, by_cols)]

    def _block(self, t, device):
        idx = 4 * device[0] + 2 * device[1] + device[2]
        if not self.by_cols[t]:
            return self.outs[t].at[idx]
        w = self.ins[t].shape[1]
        return self.outs[t].at[:, pl.ds(pl.multiple_of(idx * w, w), w)]

    def _copy(self, t, k, block, to, own=False):
        dst = self._block(t, block)
        return pltpu.make_async_remote_copy(
            src_ref=self.ins[t] if own else dst, dst_ref=dst, send_sem=self.send_sems.at[t * 7 + k],
            recv_sem=self.recv_sems.at[t * 7 + k], device_id=to, device_id_type=MESH_IDS)

    def _mine(self):
        return [pltpu.make_async_copy(self.ins[t], self._block(t, self.me), self.local_sems.at[t])
                for t in range(self.n)]

    def _first(self):
        cps = []
        for t in range(self.n):
            cps.append(self._copy(t, 0, self.me, self.sibling, own=True))
            cps += [self._copy(t, 1 + j, self.me, (*chip, self.c), own=True) for j, chip in enumerate(self.chips)]
        return cps

    def _passed(self):
        return [self._copy(t, 4 + j, (*chip, self.c), self.sibling)
                for j, chip in enumerate(self.chips) for t in range(self.n)]

    def start(self):
        for cp in self._mine() + self._first():
            cp.start()

    def forward(self):
        for j, chip in enumerate(self.chips):
            for t in range(self.n):
                self._copy(t, 1 + j, (*chip, self.c), self.me).wait_recv()
                self._copy(t, 4 + j, (*chip, self.c), self.sibling).start()

    def finish(self):
        for t in range(self.n):
            self._copy(t, 0, self.sibling, self.me).wait_recv()
            for j, chip in enumerate(self.chips):
                self._copy(t, 4 + j, (*chip, 1 - self.c), self.me).wait_recv()
        for cp in self._first() + self._passed():
            cp.wait_send()
        for cp in self._mine():
            cp.wait()


def _weight_gather(shards):
    n = len(shards)

    def body(*refs):
        g = _Gather(refs[:n], refs[n:2 * n], [False] * n, *refs[2 * n:])
        g.start()
        g.forward()
        g.finish()

    return pl.pallas_call(
        body, name="weight_gather",
        in_specs=[_HBM] * n, out_specs=[_HBM] * n,
        out_shape=_Gather.out_shapes(shards, [False] * n),
        scratch_shapes=_Gather.scratch(n),
    )(*shards)


def _grad_exchange_d2d(gs, name, swapped=()):
    n, ns = len(gs), len(swapped)

    def body(*refs):
        ins, whole_ins = refs[:n], refs[n:n + ns]
        got, whole_got = refs[n + ns:2 * n + ns], refs[2 * n + ns:2 * (n + ns)]
        send_sems, recv_sems = refs[2 * (n + ns):]
        x, y, c = _coords()
        pairs = [(ins[t].at[k, 1 - c], got[t].at[k]) for t in range(n) for k in range(4)] + list(zip(whole_ins, whole_got))
        remote = [pltpu.make_async_remote_copy(
            src_ref=src, dst_ref=dst, send_sem=send_sems.at[i], recv_sem=recv_sems.at[i], device_id=(x, y, 1 - c),
            device_id_type=MESH_IDS) for i, (src, dst) in enumerate(pairs)]
        for cp in remote:
            cp.start()
        for cp in remote:
            cp.wait_recv()
        for cp in remote:
            cp.wait_send()

    outs = pl.pallas_call(
        body, name=name,
        in_specs=[_HBM] * (n + ns), out_specs=[_HBM] * (n + ns),
        out_shape=[jax.ShapeDtypeStruct((4,) + g.shape[2:], g.dtype) for g in gs]
        + [jax.ShapeDtypeStruct(a.shape, a.dtype) for a in swapped],
        scratch_shapes=[pltpu.SemaphoreType.DMA((4 * n + ns,)), pltpu.SemaphoreType.DMA((4 * n + ns,))],
    )(*gs, *swapped)
    return outs[:n], outs[n:]


def _small_allreduce(pack):
    rows = pack.shape[0]

    def body(p_ref, o_ref, sib_ref, parts_ref, send_sems, recv_sems):
        x, y, c = _coords()
        my_chip = 2 * x + y
        swap = pltpu.make_async_remote_copy(src_ref=p_ref, dst_ref=sib_ref, send_sem=send_sems.at[0],
                                            recv_sem=recv_sems.at[0], device_id=(x, y, 1 - c), device_id_type=MESH_IDS)
        swap.start()
        swap.wait_recv()
        parts_ref[my_chip] = p_ref[...] + sib_ref[...]
        remote = [pltpu.make_async_remote_copy(
            src_ref=parts_ref.at[my_chip], dst_ref=parts_ref.at[my_chip], send_sem=send_sems.at[1 + j],
            recv_sem=recv_sems.at[1 + j], device_id=(cx, cy, c), device_id_type=MESH_IDS)
            for j, (cx, cy) in enumerate(_other_chips(x, y))]
        for cp in remote:
            cp.start()
        for j, (cx, cy) in enumerate(_other_chips(x, y)):
            slot = parts_ref.at[2 * cx + cy]
            pltpu.make_async_remote_copy(src_ref=slot, dst_ref=slot, send_sem=send_sems.at[1 + j],
                                         recv_sem=recv_sems.at[1 + j], device_id=(cx, cy, c),
                                         device_id_type=MESH_IDS).wait_recv()
        o_ref[...] = ((parts_ref[0] + parts_ref[1]) + parts_ref[2]) + parts_ref[3]
        swap.wait_send()
        for cp in remote:
            cp.wait_send()

    vmem = pl.BlockSpec(memory_space=pltpu.VMEM)
    return pl.pallas_call(
        body, name="small_allreduce", in_specs=[vmem], out_specs=vmem,
        out_shape=jax.ShapeDtypeStruct((rows, 128), F32),
        scratch_shapes=[pltpu.VMEM((rows, 128), F32), pltpu.VMEM((4, rows, 128), F32),
                        pltpu.SemaphoreType.DMA((4,)), pltpu.SemaphoreType.DMA((4,))],
    )(pack)


class _ChipExchange:
    def __init__(self, ins, outs, whole, send_sems, recv_sems, local_sems):
        self.ins, self.outs, self.whole, self.n = ins, outs, whole, len(ins)
        self.send_sems, self.recv_sems, self.local_sems = send_sems, recv_sems, local_sems
        self.x, self.y, self.c = _coords()
        self.my_chip = 2 * self.x + self.y
        self.chips = _other_chips(self.x, self.y)

    @staticmethod
    def scratch(n):
        return [pltpu.SemaphoreType.DMA((3 * n,)), pltpu.SemaphoreType.DMA((3 * n,)), pltpu.SemaphoreType.DMA((n,))]

    def _src(self, t, k):
        return self.ins[t] if self.whole[t] else self.ins[t].at[k]

    def _local(self):
        return [pltpu.make_async_copy(self._src(t, self.my_chip), self.outs[t].at[self.my_chip], self.local_sems.at[t])
                for t in range(self.n)]

    def _remote(self):
        return [pltpu.make_async_remote_copy(
            src_ref=self._src(t, 2 * cx + cy), dst_ref=self.outs[t].at[self.my_chip],
            send_sem=self.send_sems.at[t * 3 + j], recv_sem=self.recv_sems.at[t * 3 + j],
            device_id=(cx, cy, self.c), device_id_type=MESH_IDS)
            for t in range(self.n) for j, (cx, cy) in enumerate(self.chips)]

    def start(self):
        for cp in self._remote() + self._local():
            cp.start()

    def finish(self):
        for t in range(self.n):
            for j, (cx, cy) in enumerate(self.chips):
                slot = self.outs[t].at[2 * cx + cy]
                pltpu.make_async_remote_copy(
                    src_ref=slot, dst_ref=slot, send_sem=self.send_sems.at[t * 3 + j],
                    recv_sem=self.recv_sems.at[t * 3 + j], device_id=(cx, cy, self.c),
                    device_id_type=MESH_IDS).wait_recv()
        for cp in self._remote():
            cp.wait_send()
        for cp in self._local():
            cp.wait()

    @staticmethod
    def out_shapes(arrs, whole):
        return [jax.ShapeDtypeStruct(((4,) + a.shape) if w else a.shape, a.dtype) for a, w in zip(arrs, whole)]


_SEMAPHORES = pl.BlockSpec(memory_space=pltpu.SEMAPHORE)
_SIDE_EFFECT = pltpu.SideEffectType.DATAFLOW_SIDE_EFFECTING


def _split_exchange_copies(srcs, lands, whole, send_sems, recv_sems, waiting):
    x, y, c = _coords()
    copies = []
    for t in range(len(srcs)):
        for j, (cx, cy) in enumerate(_other_chips(x, y)):
            src = srcs[t] if whole[t] else srcs[t].at[2 * cx + cy]
            dst = lands[t].at[2 * cx + cy] if waiting else lands[t].at[2 * x + y]
            copies.append(pltpu.make_async_remote_copy(
                src_ref=src, dst_ref=dst, send_sem=send_sems.at[3 * t + j], recv_sem=recv_sems.at[3 * t + j],
                device_id=(cx, cy, c), device_id_type=MESH_IDS))
    return copies


def _chip_exchange_start(arrays, whole, name):
    n = len(arrays)

    def body(*refs):
        srcs, lands = refs[:n], refs[n:2 * n]
        send_sems, recv_sems = refs[2 * n:2 * n + 2]
        for cp in _split_exchange_copies(srcs, lands, whole, send_sems, recv_sems, waiting=False):
            cp.start()
        token = refs[-1]
        token[...] = jnp.zeros_like(token)

    land_shapes = [((4,) + a.shape) if w else a.shape for a, w in zip(arrays, whole)]
    hbm = lambda shape, a: pltpu.HBM(shape, a.dtype)
    outs = pl.pallas_call(
        body, name=name,
        out_shape=(pltpu.SemaphoreType.DMA((3 * n,)), pltpu.SemaphoreType.DMA((3 * n,)),
                   *[hbm(a.shape, a) for a in arrays], *[hbm(s, a) for s, a in zip(land_shapes, arrays)],
                   jax.ShapeDtypeStruct((8, 128), F32)),
        in_specs=(_HBM,) * (2 * n),
        out_specs=(_SEMAPHORES, _SEMAPHORES) + (_HBM,) * (2 * n) + (pl.BlockSpec(memory_space=pltpu.VMEM),),
        input_output_aliases={i: 2 + i for i in range(2 * n)},
        compiler_params=pltpu.CompilerParams(has_side_effects=_SIDE_EFFECT),
    )(*[pltpu.with_memory_space_constraint(a, pltpu.HBM) for a in arrays],
      *[pltpu.with_memory_space_constraint(lax.empty(s, a.dtype), pltpu.HBM) for s, a in zip(land_shapes, arrays)])
    return outs[0], outs[1], outs[2:2 + n], outs[2 + n:2 + 2 * n], outs[-1]


def _chip_exchange_wait(send_sems, recv_sems, srcs, lands, whole, after, name):
    n = len(srcs)

    def body(*refs):
        src_refs, land_refs = refs[:n], refs[n:2 * n]
        for cp in _split_exchange_copies(src_refs, land_refs, whole, refs[2 * n], refs[2 * n + 1], waiting=True):
            cp.wait_send()
            cp.wait_recv()

    outs = pl.pallas_call(
        body, name=name, out_shape=tuple(pltpu.HBM(a.shape, a.dtype) for a in (*srcs, *lands)),
        in_specs=(_HBM,) * (2 * n) + (_SEMAPHORES, _SEMAPHORES, pl.BlockSpec(memory_space=pl.ANY)),
        out_specs=(_HBM,) * (2 * n), input_output_aliases={i: i for i in range(2 * n)},
        compiler_params=pltpu.CompilerParams(has_side_effects=_SIDE_EFFECT),
    )(*srcs, *lands, send_sems, recv_sems, after)
    return outs[:n], outs[n:]


def _sum_two(a, b, name):
    def body(a_ref, b_ref, o_ref):
        o_ref[...] = a_ref[...] + b_ref[...]

    return pl.pallas_call(body, name=name, out_shape=jax.ShapeDtypeStruct(a.shape, a.dtype))(a, b)


def _sum_chips(landed, own, chip, name):
    def body(chip_ref, l_ref, own_ref, o_ref):
        part = lambda k: jnp.where(chip_ref[0] == k, own_ref[...], l_ref[k])
        o_ref[...] = ((part(0) + part(1)) + part(2)) + part(3)

    grid_spec = pltpu.PrefetchScalarGridSpec(
        num_scalar_prefetch=1, grid=(1,),
        in_specs=[pl.BlockSpec(landed.shape, lambda i, c: (0, 0, 0)), pl.BlockSpec(own.shape, lambda i, c: (0, 0))],
        out_specs=pl.BlockSpec(own.shape, lambda i, c: (0, 0)))
    return pl.pallas_call(body, grid_spec=grid_spec, name=name,
                          out_shape=jax.ShapeDtypeStruct(own.shape, own.dtype))(chip, landed, own)


def _row_tile(rows, cols, n_arrays):
    budget = 24 * 1024 * 1024
    padded = -(-cols // 128) * 128
    step = 16 if rows % 16 == 0 else 8
    tr = max(step, budget // (n_arrays * 2 * 4 * padded) // step * step)
    while rows % tr:
        tr -= step
    return tr


def _chip_sum(g, got, core, name):
    rows, cols = g.shape[2:]
    tr = _row_tile(rows, cols, 3)

    def body(c_ref, a_ref, b_ref, o_ref):
        o_ref[...] = (a_ref[...] + b_ref[...].astype(F32)).astype(o_ref.dtype)

    grid_spec = pltpu.PrefetchScalarGridSpec(
        num_scalar_prefetch=1, grid=(4, rows // tr),
        in_specs=[pl.BlockSpec((None, None, tr, cols), lambda k, i, c: (k, c[0], i, 0)),
                  pl.BlockSpec((None, tr, cols), lambda k, i, c: (k, i, 0))],
        out_specs=pl.BlockSpec((None, tr, cols), lambda k, i, c: (k, i, 0)))
    return pl.pallas_call(body, grid_spec=grid_spec, name=name,
                          out_shape=jax.ShapeDtypeStruct((4, rows, cols), MXU_DTYPE),
                          compiler_params=_params("arbitrary", "arbitrary"))(core, g, got)


def _adamw_math(w, g, m, v):
    m2 = ADAM_B1 * m + (1.0 - ADAM_B1) * g
    v2 = ADAM_B2 * v + (1.0 - ADAM_B2) * (g * g)
    m_hat = m2 / (1.0 - ADAM_B1 ** ADAM_STEP)
    v_hat = v2 / (1.0 - ADAM_B2 ** ADAM_STEP)
    delta = -ADAM_LR * (m_hat / (jnp.sqrt(v_hat) + ADAM_EPS) + ADAM_WD * w)
    return delta, m2, v2


def _adamw(parts, w, m, v, name, own=None, chip=None):
    rows, cols = w.shape
    tr = _row_tile(rows, cols, 11 if own is None else 15)

    def body(*refs):
        if own is None:
            p_ref, w_ref, m_ref, v_ref, g_ref, d_ref, m2_ref, v2_ref = refs
            part = lambda k: p_ref[k].astype(F32)
        else:
            chip_ref, p_ref, own_ref, w_ref, m_ref, v_ref, g_ref, d_ref, m2_ref, v2_ref = refs
            part = lambda k: jnp.where(chip_ref[0] == k, own_ref[k], p_ref[k]).astype(F32)
        g = ((part(0) + part(1)) + part(2)) + part(3)
        d, m2, v2 = _adamw_math(w_ref[...], g, m_ref[...], v_ref[...])
        g_ref[...] = g
        d_ref[...] = d
        m2_ref[...] = m2
        v2_ref[...] = v2

    blk = pl.BlockSpec((tr, cols), lambda i, *_: (i, 0))
    pblk = pl.BlockSpec((4, tr, cols), lambda i, *_: (0, i, 0))
    out = jax.ShapeDtypeStruct((rows, cols), F32)
    if own is None:
        return pl.pallas_call(body, grid=(rows // tr,), name=name, in_specs=[pblk, blk, blk, blk],
                              out_specs=[blk] * 4, out_shape=[out] * 4,
                              compiler_params=_params("arbitrary"))(parts, w, m, v)
    grid_spec = pltpu.PrefetchScalarGridSpec(num_scalar_prefetch=1, grid=(rows // tr,),
                                             in_specs=[pblk, pblk, blk, blk, blk], out_specs=[blk] * 4)
    return pl.pallas_call(body, grid_spec=grid_spec, name=name, out_shape=[out] * 4,
                          compiler_params=_params("arbitrary"))(chip, parts, own, w, m, v)


def _adamw_transposed(parts_t, w, m, v, name):
    rows, cols = w.shape
    tc = 256

    def body(p_ref, w_ref, m_ref, v_ref, g_ref, d_ref, m2_ref, v2_ref):
        part = lambda k: p_ref[k].astype(F32)
        g = (((part(0) + part(1)) + part(2)) + part(3)).T
        d, m2, v2 = _adamw_math(w_ref[...], g, m_ref[...], v_ref[...])
        g_ref[...] = g
        d_ref[...] = d
        m2_ref[...] = m2
        v2_ref[...] = v2

    blk = pl.BlockSpec((rows, tc), lambda i: (0, i))
    out = jax.ShapeDtypeStruct((rows, cols), F32)
    return pl.pallas_call(body, grid=(cols // tc,), name=name,
                          in_specs=[pl.BlockSpec((4, tc, rows), lambda i: (0, i, 0)), blk, blk, blk],
                          out_specs=[blk] * 4, out_shape=[out] * 4,
                          compiler_params=_params("arbitrary"))(parts_t, w, m, v)


def _adamw_small(gs, ws, ms, vs):
    n = len(ws)

    def body(*refs):
        g_refs, w_refs, m_refs, v_refs = (refs[k * n:(k + 1) * n] for k in range(4))
        d_refs, m2_refs, v2_refs = (refs[(4 + k) * n:(5 + k) * n] for k in range(3))
        for t in range(n):
            d, m2, v2 = _adamw_math(w_refs[t][...], g_refs[t][...], m_refs[t][...], v_refs[t][...])
            d_refs[t][...] = d
            m2_refs[t][...] = m2
            v2_refs[t][...] = v2

    outs = pl.pallas_call(body, name="adamw_small",
                          out_shape=[jax.ShapeDtypeStruct(w.shape, F32) for w in ws] * 3)(*gs, *ws, *ms, *vs)
    return outs[:n], outs[n:2 * n], outs[2 * n:]


_PACK_TILE = 8 * 128


def _pack(arrays):
    rows = []
    for a in arrays:
        flat = a.astype(F32).reshape(-1)
        rows.append(jnp.pad(flat, (0, -flat.shape[0] % _PACK_TILE)).reshape(-1, 128))
    return jnp.concatenate(rows, axis=0)


def _unpack(pack, shapes):
    out, r = [], 0
    for s in shapes:
        n = int(np.prod(s))
        out.append(pack[r:r + -(-n // 128)].reshape(-1)[:n].reshape(s))
        r += -(-n // _PACK_TILE) * 8
    return out


def _pad128(v):
    v = v.reshape(1, -1).astype(F32)
    return jnp.pad(v, ((0, 0), (0, 128 - v.shape[1])))


_WEIGHTS = ["meta", "norm_mix_w", "w_in", "pool_w", "pool_scale", "conv_w", "conv_b", "dt_bias", "a_log", "d_skip",
            "ssm_norm_w", "w_out", "norm_ffn_w", "w_ff1", "w_ff2", "norm_f_w"]
_BIG = ["w_in", "w_out", "w_ff1", "w_ff2"]
_SMALL = [n for n in _WEIGHTS if n not in _BIG]


def kernel(x, meta, norm_mix_w, w_in, pool_w, pool_scale, conv_w, conv_b, dt_bias, a_log, d_skip, ssm_norm_w, w_out, norm_ffn_w, w_ff1, w_ff2, norm_f_w, loss_target, m_meta, m_norm_mix_w, m_w_in, m_pool_w, m_pool_scale, m_conv_w, m_conv_b, m_dt_bias, m_a_log, m_d_skip, m_ssm_norm_w, m_w_out, m_norm_ffn_w, m_w_ff1, m_w_ff2, m_norm_f_w, v_meta, v_norm_mix_w, v_w_in, v_pool_w, v_pool_scale, v_conv_w, v_conv_b, v_dt_bias, v_a_log, v_d_skip, v_ssm_norm_w, v_w_out, v_norm_ffn_w, v_w_ff1, v_w_ff2, v_norm_f_w):
    wts = dict(meta=meta, norm_mix_w=norm_mix_w, w_in=w_in, pool_w=pool_w, pool_scale=pool_scale, conv_w=conv_w,
               conv_b=conv_b, dt_bias=dt_bias, a_log=a_log, d_skip=d_skip, ssm_norm_w=ssm_norm_w, w_out=w_out,
               norm_ffn_w=norm_ffn_w, w_ff1=w_ff1, w_ff2=w_ff2, norm_f_w=norm_f_w)
    mom1 = dict(zip(_WEIGHTS, (m_meta, m_norm_mix_w, m_w_in, m_pool_w, m_pool_scale, m_conv_w, m_conv_b, m_dt_bias,
                               m_a_log, m_d_skip, m_ssm_norm_w, m_w_out, m_norm_ffn_w, m_w_ff1, m_w_ff2, m_norm_f_w)))
    mom2 = dict(zip(_WEIGHTS, (v_meta, v_norm_mix_w, v_w_in, v_pool_w, v_pool_scale, v_conv_w, v_conv_b, v_dt_bias,
                               v_a_log, v_d_skip, v_ssm_norm_w, v_w_out, v_norm_ffn_w, v_w_ff1, v_w_ff2, v_norm_f_w)))
    xi, yi, ci = _coords()
    dev = 4 * xi + 2 * yi + ci
    win_cols = w_in.shape[-1]
    cw_cols = conv_w.shape[-1]

    nb, seq, _ = x.shape
    core = jnp.reshape(ci, (1,)).astype(jnp.int32)
    owners = lambda a: a.reshape((4, 2) + a.shape[1:])

    lead_pack = jnp.zeros((N_META, 512), F32)
    lead_pack = lead_pack.at[:, :128].set(meta).at[:4, 128:128 + cw_cols].set(conv_w[0])
    g_win, g_lead = _weight_gather([_mx(w_in[0]), lead_pack])
    win_full = _assemble_bands(g_win, PROJ_W)
    meta_full = jnp.transpose(g_lead[:, :, :128], (1, 0, 2)).reshape(N_META, D_MODEL)
    cw_full = jnp.transpose(g_lead[:, :4, 128:128 + cw_cols], (1, 0, 2)).reshape(4, D_XBC)

    lead = jnp.concatenate([jnp.zeros((PAD_ROWS, D_MODEL), F32), meta_full] * nb, axis=0)
    x_rows = x.reshape(nb * seq, D_MODEL)
    tgt_rows = loss_target.reshape(nb * seq, D_MODEL)
    dt_bias_p, a_log_p = _pad128(dt_bias), _pad128(a_log)
    d_x = jnp.repeat(d_skip.reshape(1, N_HEADS).astype(F32), HEAD_DIM, axis=1)
    norm_f_row = norm_f_w.reshape(1, D_MODEL)

    hn1, proj = _in_proj(x_rows, lead, norm_mix_w, win_full)
    late_cols = [False, True, False]
    y, ypre, conv_pre, states, (g_wout, wff1_full, g_wff2) = _mixer_fwd(
        proj, cw_full, conv_b, dt_bias_p, a_log_p, d_x, ssm_norm_w, pool_w[0], pool_scale, nb,
        [_mx(w_out[0]), _mx(w_ff1[0]), _mx(w_ff2[0])], late_cols)
    wout_full = g_wout.reshape(D_MIX, D_MODEL)
    wff2_full = g_wff2.reshape(D_FF, D_MODEL)
    loss, gr_nf, gr_nffn, ff, da, hn2, dh1, dh2, dy = _ffn_fwd_bwd(
        x_rows, lead, y, tgt_rows, wout_full, norm_ffn_w, wff1_full, wff2_full, norm_f_row)
    gr_wff2 = _tn_matmul(ff, dh2, "grad_w_ff2", tka=1024, max_slab=2816)
    gr_wff1_t = _tn_matmul(da, hn2, "grad_w_ff1", tka=1024, max_slab=2816)
    gr_wout = _tn_matmul(y, dh1, "grad_w_out", tka=1024, max_slab=2816)

    by_owner = lambda k: [owners(gr_wout[k].reshape(N_DEV, D_MIX // N_DEV, D_MODEL)),
                          owners(gr_wff1_t[k].reshape(N_DEV, D_FF // N_DEV, D_MODEL)),
                          owners(gr_wff2[k].reshape(N_DEV, D_FF // N_DEV, D_MODEL))]
    late_parts = by_owner(0)
    late_got, _ = _grad_exchange_d2d(by_owner(1), "grad_exchange_d2d_late")
    late_sums = [_chip_sum(late_parts[t], late_got[t], core, "chip_sum_late_%d" % t) for t in range(3)]
    (dproj, gr_cw, gr_cb, gr_dtb, gr_alog, gr_d, gr_nw, gr_pw, gr_ps), late_exchanged = _mixer_bwd(
        proj, dy, ypre, conv_pre, states, cw_full, dt_bias_p, a_log_p, d_x, ssm_norm_w, pool_w[0], pool_scale, nb,
        late_sums)

    early = dict(pool_w=gr_pw, pool_scale=gr_ps, conv_w=gr_cw, conv_b=gr_cb, dt_bias=gr_dtb[:, :N_HEADS],
                 a_log=gr_alog[:, :N_HEADS], d_skip=gr_d[:, :N_HEADS], ssm_norm_w=gr_nw, norm_ffn_w=gr_nffn,
                 norm_f_w=gr_nf, loss=loss[0:1, 0:1])
    early_pack = _pack(list(early.values()))
    win_parts = [owners(_tn_matmul_banded(hn1, dproj, "grad_w_in", win_cols, tka=512))]
    win_got, (early_got,) = _grad_exchange_d2d(win_parts, "grad_exchange_d2d_w_in", swapped=[early_pack])
    win_sum = _chip_sum(win_parts[0], win_got[0], core, "chip_sum_w_in")
    early_chip = _sum_two(early_pack, early_got, "chip_sum_small")
    whole = [False, True]
    send_sems, recv_sems, sent, landing, started = _chip_exchange_start([win_sum, early_chip], whole, "w_in_exchange_start")
    gx_rows, gr_nmix, gr_meta = _in_proj_bwd(dproj, x_rows, lead, dh1, norm_mix_w, win_full, started)
    (win_sum, early_chip), (win_landed, early_landed) = _chip_exchange_wait(
        send_sems, recv_sems, sent, landing, whole, gr_nmix, "w_in_exchange_wait")
    parts = dict(w_in=win_landed, w_out=late_exchanged[0], w_ff1=late_exchanged[1], w_ff2=late_exchanged[2])
    my_chip = jnp.reshape(2 * xi + yi, (1,)).astype(jnp.int32)
    early_sum = _sum_chips(early_landed, early_chip, my_chip, "small_sum")

    tail = dict(meta=gr_meta, norm_mix_w=gr_nmix)
    tail_sum = _small_allreduce(_pack(list(tail.values())))
    gs = dict(zip(early, _unpack(early_sum, [a.shape for a in early.values()])))
    gs.update(zip(tail, _unpack(tail_sum, [a.shape for a in tail.values()])))
    gs["meta"] = lax.dynamic_slice_in_dim(gs["meta"], dev * 128, 128, axis=1)
    gs["conv_w"] = lax.dynamic_slice_in_dim(gs["conv_w"], dev * cw_cols, cw_cols, axis=1)

    res = {}
    for n in _BIG:
        shp = wts[n].shape
        args = (parts[n], wts[n][0], mom1[n][0], mom2[n][0], "adamw_" + n)
        if n == "w_ff1":
            outs = _adamw_transposed(*args)
        elif n == "w_in":
            outs = _adamw(*args, own=win_sum, chip=my_chip)
        else:
            outs = _adamw(*args)
        res[n] = [o.reshape(shp) for o in outs]
    as2d = lambda a: a.reshape(-1, a.shape[-1])
    small_g = [as2d(gs[n].reshape(wts[n].shape)) for n in _SMALL]
    small_out = _adamw_small(small_g, *[[as2d(d[n]) for n in _SMALL] for d in (wts, mom1, mom2)])
    for k, n in enumerate(_SMALL):
        res[n] = [o[k].reshape(wts[n].shape) for o in (small_g,) + tuple(small_out)]

    grad_x = gx_rows.reshape(nb, seq, D_MODEL)
    return (gs["loss"][0, 0], grad_x, *[res[n][0] for n in _WEIGHTS], *[res[n][1] for n in _WEIGHTS],
            *[res[n][2] for n in _WEIGHTS], *[res[n][3] for n in _WEIGHTS])
```

```python
import numpy as np
import jax
import jax.numpy as jnp
from jax import lax
from jax.experimental import pallas as pl
from jax.experimental.pallas import tpu as pltpu

F32 = jnp.float32
MXU_DTYPE = jnp.bfloat16

D_MODEL = 1024
D_POOL = 512
D_SSM = 1536
D_XBC = 2560
N_HEADS = 24
HEAD_DIM = 64
N_GROUPS = 4
GROUP_CH = D_SSM // N_GROUPS
D_STATE = 128
CHUNK = 128
N_META = 16
LEAD = CHUNK
PAD_ROWS = LEAD - N_META
ROW_TILE = 2 * CHUNK
D_MIX = D_POOL + D_SSM
D_FF = 4096
PROJ_W = 4736
OFF_Z = D_POOL
OFF_X = D_POOL + D_SSM
OFF_DT = OFF_X + D_XBC
D_IN_PROJ = OFF_DT + N_HEADS
POOL_WINDOWS = (2, 4, 8, 16)
HALO = 16
EPS = 1e-5
N_DEV = 8

ADAM_LR, ADAM_B1, ADAM_B2, ADAM_EPS, ADAM_WD, ADAM_STEP = 0.001, 0.9, 0.999, 1e-08, 0.01, 10

VMEM_LIMIT = 60 * 1024 * 1024


def _mx(a):
    return a.astype(MXU_DTYPE)


def _dot(a, b):
    return jnp.dot(_mx(a), _mx(b), preferred_element_type=F32)


def _dot_nt(a, b):
    return lax.dot_general(_mx(a), _mx(b), (((1,), (1,)), ((), ())), preferred_element_type=F32)


def _dot_tn(a, b):
    return lax.dot_general(_mx(a), _mx(b), (((0,), (0,)), ((), ())), preferred_element_type=F32)


def _split3(x):
    hi = x.astype(MXU_DTYPE)
    r = x - hi.astype(F32)
    mid = r.astype(MXU_DTYPE)
    lo = (r - mid.astype(F32)).astype(MXU_DTYPE)
    return hi, mid, lo


def _exact_l(c, x):
    hi, mid, lo = _split3(x)
    f = lambda p: jnp.dot(c, p, preferred_element_type=F32)
    return f(hi) + f(mid) + f(lo)


def _exact_r(x, c):
    hi, mid, lo = x if isinstance(x, tuple) else _split3(x)
    f = lambda p: jnp.dot(p, c, preferred_element_type=F32)
    return f(hi) + f(mid) + f(lo)


def _contract(x, c):
    hi = x.astype(MXU_DTYPE)
    lo = (x - hi.astype(F32)).astype(MXU_DTYPE)
    return jnp.dot(hi, c, preferred_element_type=F32) + jnp.dot(lo, c, preferred_element_type=F32)


def _sigmoid(x):
    return jax.nn.sigmoid(x)


def _softplus(x):
    return jnp.maximum(x, 0.0) + jnp.log1p(jnp.exp(-jnp.abs(x)))


def _silu_and_grad(x):
    s = _sigmoid(x)
    y = x * s
    return y, s + y * (1.0 - s)


def _shift_up(ext, s):
    if s == 0:
        return ext[:CHUNK, :]
    return pltpu.roll(ext, ext.shape[0] - s, 0)[:CHUNK, :]


def _by_pool_group(lane, a2, a4, a8, a16):
    return jnp.where(lane < 128, a2, jnp.where(lane < 256, a4, jnp.where(lane < 384, a8, a16)))


def _pool_inv_count(chunk_idx):
    row = lax.broadcasted_iota(jnp.int32, (CHUNK, D_POOL), 0)
    lane = lax.broadcasted_iota(jnp.int32, (CHUNK, D_POOL), 1)
    pos1 = jnp.maximum(chunk_idx * CHUNK + row - (PAD_ROWS - 1), 1)
    w = _by_pool_group(lane, 2, 4, 8, 16)
    return 1.0 / jnp.minimum(pos1, w).astype(F32), lane


def _pool_window_sums(u_ext, lane):
    s2 = u_ext + pltpu.roll(u_ext, 1, 0)
    s4 = s2 + pltpu.roll(s2, 2, 0)
    s8 = s4 + pltpu.roll(s4, 4, 0)
    s16 = s8 + pltpu.roll(s8, 8, 0)
    return _by_pool_group(lane, s2[HALO:], s4[HALO:], s8[HALO:], s16[HALO:])


def _pool_window_sums_ahead(q_ext, lane):
    n = q_ext.shape[0]
    r2 = q_ext + pltpu.roll(q_ext, n - 1, 0)
    r4 = r2 + pltpu.roll(r2, n - 2, 0)
    r8 = r4 + pltpu.roll(r4, n - 4, 0)
    r16 = r8 + pltpu.roll(r8, n - 8, 0)
    return _by_pool_group(lane, r2[:CHUNK], r4[:CHUNK], r8[:CHUNK], r16[:CHUNK])


def _conv_pre(ext, xbc, cw, cb):
    s1 = pltpu.roll(ext, 1, 0)
    near = cw[3:4, :] * xbc + cw[2:3, :] * s1[HALO:, :]
    far = cw[1:2, :] * ext + cw[0:1, :] * s1
    return cb + near + pltpu.roll(far, 2, 0)[HALO:, :]


def _dt_and_cumsum(dtr, dt_bias, a_log, valid, tril):
    lane = lax.broadcasted_iota(jnp.int32, (CHUNK, 128), 1)
    head = lane < N_HEADS
    pre = dtr + dt_bias
    dt = jnp.where(valid & head, _softplus(pre), 0.0)
    a_row = jnp.where(head[0:1, :], -jnp.exp(a_log), 0.0)
    a_col = _exact_l(tril, dt * a_row)
    return dt, a_row, a_col, pre, head


def _decay(a_col, a_row_t, h, causal):
    seg = a_col[:, h:h + 1] - a_row_t[h:h + 1, :]
    return jnp.where(causal, jnp.exp(jnp.minimum(seg, 0.0)), 0.0)


def _ssd_chunk_fwd(xs, bm, cm, dt, a_col, s_prev, d_x, e_mat, et_f32):
    lane = lax.broadcasted_iota(jnp.int32, (CHUNK, 128), 1)
    rowi = lax.broadcasted_iota(jnp.int32, (CHUNK, CHUNK), 0)
    coli = lax.broadcasted_iota(jnp.int32, (CHUNK, CHUNK), 1)
    causal = rowi >= coli
    a_row_t = a_col.T
    ax = _exact_r(a_col, e_mat)
    dtx = _exact_r(dt, e_mat)
    xdt = xs * dtx
    ax_last = ax[CHUNK - 1:CHUNK, :]
    e_a = jnp.exp(ax)
    w_end = xdt * jnp.exp(ax_last - ax)
    cd_col = jnp.exp(jnp.sum(et_f32 * a_col[CHUNK - 1:CHUNK, :], axis=1, keepdims=True))
    ys, s_new = [], []
    for g in range(N_GROUPS):
        gs = slice(g * GROUP_CH, (g + 1) * GROUP_CH)
        bg = bm[:, g * D_STATE:(g + 1) * D_STATE]
        cg = cm[:, g * D_STATE:(g + 1) * D_STATE]
        sg = s_prev[gs, :]
        cb = _dot_nt(cg, bg)
        y_off = _dot_nt(cg, sg) * e_a[:, gs]
        s_new.append(sg * cd_col[gs, :] + _dot_tn(w_end[:, gs], bg))
        for pr in range(3):
            c0 = g * GROUP_CH + pr * 128
            xdt_p = xdt[:, c0:c0 + 128]
            h0 = g * 6 + pr * 2
            y0 = _dot(cb * _decay(a_col, a_row_t, h0, causal), xdt_p)
            y1 = _dot(cb * _decay(a_col, a_row_t, h0 + 1, causal), xdt_p)
            ys.append(jnp.where(lane < HEAD_DIM, y0, y1) + y_off[:, pr * 128:(pr + 1) * 128])
    y = jnp.concatenate(ys, axis=1) + d_x * xs
    return y, jnp.concatenate(s_new, axis=0)


def _ssd_chunk_bwd(xs, bm, cm, dt, a_row, a_col, s_prev, ds_new, dy, d_x, e_mat, et_mat, et_f32, triu):
    lane = lax.broadcasted_iota(jnp.int32, (CHUNK, 128), 1)
    sub = lax.broadcasted_iota(jnp.int32, (CHUNK, 128), 0)
    rowi = lax.broadcasted_iota(jnp.int32, (CHUNK, CHUNK), 0)
    coli = lax.broadcasted_iota(jnp.int32, (CHUNK, CHUNK), 1)
    causal = rowi >= coli
    a_row_t = a_col.T
    a_last = a_col[CHUNK - 1:CHUNK, :]
    a_split, dt_split = _split3(a_col), _split3(dt)
    sub8 = lax.broadcasted_iota(jnp.int32, (8, GROUP_CH), 0)

    dxs, dbs, dcs, dsp = [], [], [], []
    zcol = jnp.zeros((CHUNK, 128), F32)
    zrows = []
    da_col = jnp.zeros((CHUNK, 128), F32)
    ddt = jnp.zeros((CHUNK, 128), F32)
    head_sums = jnp.zeros((8, 128), F32)
    q_row = jnp.zeros((1, 128), F32)
    for g in range(N_GROUPS):
        gs = slice(g * GROUP_CH, (g + 1) * GROUP_CH)
        e_g, et_g = e_mat[:, gs], et_mat[gs, :]
        xs_g, dy_g = xs[:, gs], dy[:, gs]
        ax = _exact_r(a_split, e_g)
        dtx = _exact_r(dt_split, e_g)
        xdt = xs_g * dtx
        dte = jnp.exp(ax[CHUNK - 1:CHUNK, :] - ax)
        w_end = xdt * dte
        cd_col = jnp.exp(jnp.sum(et_f32[gs, :] * a_last, axis=1, keepdims=True))
        dye = dy_g * jnp.exp(ax)
        bg = bm[:, g * D_STATE:(g + 1) * D_STATE]
        cg = cm[:, g * D_STATE:(g + 1) * D_STATE]
        sg = s_prev[gs, :]
        dsg = ds_new[gs, :]
        cb = _dot_nt(cg, bg)
        cs = _dot_nt(cg, sg)
        dcg = _dot(dye, sg)
        dsp.append(dsg * cd_col + _dot_tn(dye, cg))
        dwg = _dot_nt(bg, dsg)
        dbg = _dot(w_end, dsg)
        ww = dwg * w_end
        t1 = jnp.sum(dsg * sg, axis=1, keepdims=True) * cd_col
        dcb = jnp.zeros((CHUNK, CHUNK), F32)
        pairs = []
        for pr in range(3):
            ps = slice(pr * 128, (pr + 1) * 128)
            xdt_p, dy_p = xdt[:, ps], dy_g[:, ps]
            acc = None
            for half in range(2):
                h = g * 6 + pr * 2 + half
                ld = _decay(a_col, a_row_t, h, causal)
                gm = cb * ld
                dym = jnp.where((lane < HEAD_DIM) if half == 0 else (lane >= HEAD_DIM), dy_p, 0.0)
                dg = _dot_nt(dym, xdt_p)
                dseg = dg * gm
                dcb = dcb + dg * ld
                t = _dot_tn(gm, dym)
                acc = t if acc is None else acc + t
                zcol = jnp.where(lane == h, jnp.sum(dseg, axis=1, keepdims=True), zcol)
                zrows.append(jnp.sum(dseg, axis=0, keepdims=True))
            pairs.append(acc)
        dxdt = dwg * dte + jnp.concatenate(pairs, axis=1)
        dcs.append(dcg + _dot(dcb, bg))
        dbs.append(dbg + _dot_tn(dcb, cg))
        dxs.append(dxdt * dtx + d_x[:, gs] * dy_g)
        da_col = da_col + _contract(dye * cs - ww, et_g)
        ddt = ddt + _contract(dxdt * xs_g, et_g)
        col_sums = jnp.where(sub8 == 0, jnp.sum(dy_g * xs_g, axis=0, keepdims=True),
                             jnp.where(sub8 == 1, jnp.sum(ww, axis=0, keepdims=True), 0.0))
        head_sums = head_sums + _exact_r(col_sums, et_g)
        q_row = q_row + jnp.sum(et_f32[gs, :] * t1, axis=0, keepdims=True)

    dd = head_sums[0:1, :]
    q_row = q_row + head_sums[1:2, :]
    zrow = jnp.concatenate(zrows + [jnp.zeros((128 - N_HEADS, CHUNK), F32)], axis=0)
    da_col = da_col + zcol - zrow.T + jnp.where(sub == CHUNK - 1, q_row, 0.0)
    rc = _exact_l(triu, da_col)
    ddt = ddt + rc * a_row
    da = jnp.sum(rc * dt, axis=0, keepdims=True)
    return (jnp.concatenate(dxs, axis=1), jnp.concatenate(dbs, axis=1), jnp.concatenate(dcs, axis=1), ddt, da, dd,
            jnp.concatenate(dsp, axis=0))


_SSD_CONSTANT_SHAPES = dict(e=((128, D_SSM), MXU_DTYPE), et=((D_SSM, 128), MXU_DTYPE), et_f32=((D_SSM, 128), F32),
                            tril=((CHUNK, CHUNK), MXU_DTYPE), triu=((CHUNK, CHUNK), MXU_DTYPE))


def _ssd_constant_scratch(names):
    return [pltpu.VMEM(*_SSD_CONSTANT_SHAPES[n]) for n in names]


def _fill_ssd_constants(**refs):
    iota = lambda shape, d: lax.broadcasted_iota(jnp.int32, shape, d)
    shift = HEAD_DIM.bit_length() - 1
    marks = dict(
        e=lambda: iota((128, D_SSM), 0) == (iota((128, D_SSM), 1) >> shift),
        et=lambda: iota((D_SSM, 128), 1) == (iota((D_SSM, 128), 0) >> shift),
        et_f32=lambda: iota((D_SSM, 128), 1) == (iota((D_SSM, 128), 0) >> shift),
        tril=lambda: iota((CHUNK, CHUNK), 1) <= iota((CHUNK, CHUNK), 0),
        triu=lambda: iota((CHUNK, CHUNK), 1) >= iota((CHUNK, CHUNK), 0))
    for name, ref in refs.items():
        ref[...] = jnp.where(marks[name](), 1.0, 0.0).astype(ref.dtype)


def _row_views(ref, widths):
    views, off = [], 0
    for w in widths:
        views.append(ref.at[:, off:off + w])
        off += w
    return views


def _full(shape):
    nd = len(shape)
    return pl.BlockSpec(shape, lambda *_: (0,) * nd)


def _params(*sem):
    return pltpu.CompilerParams(dimension_semantics=sem, vmem_limit_bytes=VMEM_LIMIT)


def _token_tiles(width, n_tok_tiles):
    return pl.BlockSpec((ROW_TILE, width), lambda i: (jnp.minimum(i, n_tok_tiles - 1), 0))


def _in_proj(x, lead, w1, win):
    nt = x.shape[0] // ROW_TILE
    m = x.shape[0] + ROW_TILE
    tm = ROW_TILE

    def body(x_ref, lead_ref, w1_ref, win_hbm, hn_ref, proj_ref, win_v, sem):
        i = pl.program_id(0)

        @pl.when(i == 0)
        def _():
            cp = pltpu.make_async_copy(win_hbm, win_v, sem)
            cp.start()
            cp.wait()

        x = jnp.where(i == nt, lead_ref[...], x_ref[...])
        r = lax.rsqrt(jnp.mean(x * x, axis=-1, keepdims=True) + EPS)
        hn = _mx(x * r * w1_ref[...])
        hn_ref[...] = hn
        for j in range(0, PROJ_W, 512):
            w = min(512, PROJ_W - j)
            proj_ref[:, j:j + w] = jnp.dot(hn, win_v[:, j:j + w], preferred_element_type=F32)

    return pl.pallas_call(
        body, grid=(m // tm,), name="in_proj",
        in_specs=[_token_tiles(D_MODEL, nt), _full((ROW_TILE, D_MODEL)), _full((1, D_MODEL)),
                  pl.BlockSpec(memory_space=pl.ANY)],
        out_specs=[pl.BlockSpec((tm, D_MODEL), lambda i: (i, 0)), pl.BlockSpec((tm, PROJ_W), lambda i: (i, 0))],
        out_shape=[jax.ShapeDtypeStruct((m, D_MODEL), MXU_DTYPE), jax.ShapeDtypeStruct((m, PROJ_W), F32)],
        scratch_shapes=[pltpu.VMEM((D_MODEL, PROJ_W), MXU_DTYPE), pltpu.SemaphoreType.DMA],
        compiler_params=_params("arbitrary"),
    )(x, lead, w1, win)


def _ffn_fwd_bwd(x, lead, y, tgt, wout, w2n, wff1, wff2, wfn):
    nt = x.shape[0] // ROW_TILE
    m = x.shape[0] + ROW_TILE
    tm = ROW_TILE
    nj = D_FF // 1024

    def body(x_ref, lead_ref, y_ref, tgt_ref, w2n_ref, wfn_ref, wout_hbm, wff1_hbm, wff2_hbm,
             loss_ref, gwf_ref, gw2_ref, ff_ref, da_ref, hn2_ref, dh1_ref, dh1m_ref, dh2_ref, dy_ref,
             wout_v, wff1_v, wff2_v, a_s, sems):
        i = pl.program_id(0)
        hp = jnp.where(i == nt, lead_ref[...], x_ref[...])

        @pl.when(i == 0)
        def _():
            cps = [pltpu.make_async_copy(s, d, sems.at[k])
                   for k, (s, d) in enumerate(((wout_hbm, wout_v), (wff1_hbm, wff1_v), (wff2_hbm, wff2_v)))]
            for cp in cps:
                cp.start()
            for cp in cps:
                cp.wait()
            loss_ref[...] = jnp.zeros_like(loss_ref)
            gwf_ref[...] = jnp.zeros_like(gwf_ref)
            gw2_ref[...] = jnp.zeros_like(gw2_ref)

        h1 = hp + jnp.dot(y_ref[...], wout_v[...], preferred_element_type=F32)
        r2 = lax.rsqrt(jnp.mean(h1 * h1, axis=-1, keepdims=True) + EPS)
        n2 = h1 * r2
        w2n_row = w2n_ref[...]
        hn2 = _mx(n2 * w2n_row)
        hn2_ref[...] = hn2
        h2 = h1
        for j in range(nj):
            js = slice(j * 1024, (j + 1) * 1024)
            a = jnp.dot(hn2, wff1_v[:, js], preferred_element_type=F32)
            a_s[:, js] = a
            ra = jnp.maximum(a, 0.0)
            ff = _mx(ra * ra)
            ff_ref[:, js] = ff
            h2 = h2 + jnp.dot(ff, wff2_v[js, :], preferred_element_type=F32)

        r3 = lax.rsqrt(jnp.mean(h2 * h2, axis=-1, keepdims=True) + EPS)
        n3 = h2 * r3
        wf_row = wfn_ref[...]
        err = n3 * wf_row - tgt_ref[...]
        tokf = (i < nt).astype(F32)
        loss_ref[...] += 0.5 * jnp.sum(jnp.mean(err * err, axis=-1, keepdims=True) * tokf)
        dout = err * (tokf / D_MODEL)
        gwf_ref[...] += jnp.sum(dout * n3, axis=0, keepdims=True)
        dn3 = dout * wf_row
        dh2 = r3 * (dn3 - n3 * jnp.mean(dn3 * n3, axis=-1, keepdims=True))
        dh2m = _mx(dh2)
        dh2_ref[...] = dh2m

        dhn2 = jnp.zeros((tm, D_MODEL), F32)
        for j in range(nj):
            js = slice(j * 1024, (j + 1) * 1024)
            dff = lax.dot_general(dh2m, wff2_v[js, :], (((1,), (1,)), ((), ())), preferred_element_type=F32)
            da = _mx(dff * (2.0 * jnp.maximum(a_s[:, js], 0.0)))
            da_ref[:, js] = da
            dhn2 = dhn2 + lax.dot_general(da, wff1_v[:, js], (((1,), (1,)), ((), ())), preferred_element_type=F32)
        gw2_ref[...] += jnp.sum(dhn2 * n2, axis=0, keepdims=True)
        dn2 = dhn2 * w2n_row
        dh1 = dh2 + r2 * (dn2 - n2 * jnp.mean(dn2 * n2, axis=-1, keepdims=True))
        dh1_ref[...] = dh1
        dh1m = _mx(dh1)
        dh1m_ref[...] = dh1m
        dy_ref[...] = lax.dot_general(dh1m, wout_v[...], (((1,), (1,)), ((), ())), preferred_element_type=F32)

    rows = lambda w: pl.BlockSpec((tm, w), lambda i: (i, 0))
    hbm = pl.BlockSpec(memory_space=pl.ANY)
    return pl.pallas_call(
        body, grid=(m // tm,), name="ffn_fwd_bwd",
        in_specs=[_token_tiles(D_MODEL, nt), _full((ROW_TILE, D_MODEL)), rows(D_MIX), _token_tiles(D_MODEL, nt),
                  _full((1, D_MODEL)), _full((1, D_MODEL)), hbm, hbm, hbm],
        out_specs=[_full((1, 128)), _full((1, D_MODEL)), _full((1, D_MODEL)), rows(D_FF), rows(D_FF), rows(D_MODEL),
                   rows(D_MODEL), rows(D_MODEL), rows(D_MODEL), rows(D_MIX)],
        out_shape=[jax.ShapeDtypeStruct((1, 128), F32), jax.ShapeDtypeStruct((1, D_MODEL), F32),
                   jax.ShapeDtypeStruct((1, D_MODEL), F32), jax.ShapeDtypeStruct((m, D_FF), MXU_DTYPE),
                   jax.ShapeDtypeStruct((m, D_FF), MXU_DTYPE), jax.ShapeDtypeStruct((m, D_MODEL), MXU_DTYPE),
                   jax.ShapeDtypeStruct((m, D_MODEL), F32), jax.ShapeDtypeStruct((m, D_MODEL), MXU_DTYPE),
                   jax.ShapeDtypeStruct((m, D_MODEL), MXU_DTYPE), jax.ShapeDtypeStruct((m, D_MIX), F32)],
        scratch_shapes=[pltpu.VMEM((D_MIX, D_MODEL), MXU_DTYPE), pltpu.VMEM((D_MODEL, D_FF), MXU_DTYPE),
                        pltpu.VMEM((D_FF, D_MODEL), MXU_DTYPE), pltpu.VMEM((tm, D_FF), F32),
                        pltpu.SemaphoreType.DMA((3,))],
        compiler_params=_params("arbitrary"),
    )(x, lead, y, tgt, w2n, wfn, wout, wff1, wff2)


def _in_proj_bwd(dproj, x, lead, dh1, w1, win, after):
    nt = x.shape[0] // ROW_TILE
    m = x.shape[0] + ROW_TILE
    tm = ROW_TILE

    def body(dp_ref, x_ref, lead_ref, dh1_ref, w1_ref, win_hbm, after_ref, gx_ref, gw1_ref, gmeta_ref, win_v, sem):
        i = pl.program_id(0)

        @pl.when(i == 0)
        def _():
            cp = pltpu.make_async_copy(win_hbm, win_v, sem)
            cp.start()
            cp.wait()
            gw1_ref[...] = jnp.zeros_like(gw1_ref)
            gmeta_ref[...] = jnp.zeros_like(gmeta_ref)

        dhn = lax.dot_general(dp_ref[...], win_v[...], (((1,), (1,)), ((), ())), preferred_element_type=F32)
        x = jnp.where(i == nt, lead_ref[...], x_ref[...])
        r = lax.rsqrt(jnp.mean(x * x, axis=-1, keepdims=True) + EPS)
        n = x * r
        gw1_ref[...] += jnp.sum(dhn * n, axis=0, keepdims=True)
        dn = dhn * w1_ref[...]
        dh0 = dh1_ref[...] + r * (dn - n * jnp.mean(dn * n, axis=-1, keepdims=True))

        @pl.when(i < nt)
        def _():
            gx_ref[...] = dh0

        @pl.when(i == nt)
        def _():
            gmeta_ref[...] = dh0[PAD_ROWS:LEAD, :] + dh0[LEAD + PAD_ROWS:2 * LEAD, :]

    rows = lambda w: pl.BlockSpec((tm, w), lambda i: (i, 0))
    hbm = pl.BlockSpec(memory_space=pl.ANY)
    return pl.pallas_call(
        body, grid=(m // tm,), name="in_proj_bwd",
        in_specs=[rows(PROJ_W), _token_tiles(D_MODEL, nt), _full((ROW_TILE, D_MODEL)), rows(D_MODEL),
                  _full((1, D_MODEL)), hbm, hbm],
        out_specs=[_token_tiles(D_MODEL, nt), _full((1, D_MODEL)), _full((N_META, D_MODEL))],
        out_shape=[jax.ShapeDtypeStruct(x.shape, F32), jax.ShapeDtypeStruct((1, D_MODEL), F32),
                   jax.ShapeDtypeStruct((N_META, D_MODEL), F32)],
        scratch_shapes=[pltpu.VMEM((D_MODEL, PROJ_W), MXU_DTYPE), pltpu.SemaphoreType.DMA],
        compiler_params=_params("arbitrary"),
    )(dproj, x, lead, dh1, w1, win, after)


MXU_DEPTH = 256


def _row_slab(m, cap):
    return max(k for k in range(MXU_DEPTH, cap + 1, MXU_DEPTH) if m % k == 0)


def _tn_matmul(a, b, name, tka, max_slab=768, tn=512):
    m, ka = a.shape
    nb = b.shape[1]
    tkm = _row_slab(m, max_slab)
    n_steps = m // tkm

    def body(a_ref, b_ref, o_ref, omx_ref):
        k = pl.program_id(1)

        @pl.when(k == 0)
        def _():
            o_ref[...] = jnp.zeros_like(o_ref)

        at = _mx(a_ref[...])
        for j in range(0, nb, tn):
            w = min(tn, nb - j)
            o_ref[:, j:j + w] += lax.dot_general(at, _mx(b_ref[:, j:j + w]), (((0,), (0,)), ((), ())),
                                                 preferred_element_type=F32)

        @pl.when(k == n_steps - 1)
        def _():
            omx_ref[...] = _mx(o_ref[...])

    out = pl.BlockSpec((tka, nb), lambda i, k: (i, 0))
    return pl.pallas_call(
        body, grid=(ka // tka, n_steps), name=name,
        in_specs=[pl.BlockSpec((tkm, tka), lambda i, k: (k, i)), pl.BlockSpec((tkm, nb), lambda i, k: (k, 0))],
        out_specs=[out, out],
        out_shape=[jax.ShapeDtypeStruct((ka, nb), F32), jax.ShapeDtypeStruct((ka, nb), MXU_DTYPE)],
        compiler_params=_params("arbitrary", "arbitrary"),
    )(a, b)


def _tn_matmul_banded(a, b, name, band, tka, tn=512):
    m, ka = a.shape
    nb = b.shape[1]
    tkm = _row_slab(m, 768)
    n_steps = m // tkm

    def body(a_ref, b_ref, o_ref, acc):
        k = pl.program_id(1)

        @pl.when(k == 0)
        def _():
            acc[...] = jnp.zeros_like(acc)

        at = _mx(a_ref[...])
        for j in range(0, nb, tn):
            w = min(tn, nb - j)
            acc[:, j:j + w] += lax.dot_general(at, _mx(b_ref[:, j:j + w]), (((0,), (0,)), ((), ())),
                                               preferred_element_type=F32)

        @pl.when(k == n_steps - 1)
        def _():
            for j in range(N_DEV):
                o_ref[j] = acc[:, j * band:(j + 1) * band]

    return pl.pallas_call(
        body, grid=(ka // tka, n_steps), name=name,
        in_specs=[pl.BlockSpec((tkm, tka), lambda i, k: (k, i)), pl.BlockSpec((tkm, nb), lambda i, k: (k, 0))],
        out_specs=pl.BlockSpec((N_DEV, tka, band), lambda i, k: (0, i, 0)),
        out_shape=jax.ShapeDtypeStruct((N_DEV, ka, band), F32),
        scratch_shapes=[pltpu.VMEM((tka, nb), F32)],
        compiler_params=_params("arbitrary", "arbitrary"),
    )(a, b)


def _assemble_bands(g, width):
    n, rows, band = g.shape
    tr = 256

    def body(g_ref, o_ref):
        parts = [g_ref[j] for j in range(n)] + [jnp.zeros((tr, width - n * band), g.dtype)]
        o_ref[...] = jnp.concatenate(parts, axis=1)

    return pl.pallas_call(
        body, grid=(rows // tr,), name="assemble_w_in",
        in_specs=[pl.BlockSpec((n, tr, band), lambda i: (0, i, 0))],
        out_specs=pl.BlockSpec((tr, width), lambda i: (i, 0)),
        out_shape=jax.ShapeDtypeStruct((rows, width), g.dtype),
        compiler_params=_params("arbitrary"),
    )(g)


def _chunk_block(b, c, nb, nc):
    return jnp.where(c == 0, nb * (nc - 1) + b, b * (nc - 1) + c - 1)

def _mixer_fwd(proj, cw, cb, dt_bias, a_log, d_x, nw, pool_w, pool_scale, nb, shards, by_cols):
    m = proj.shape[0]
    nc = m // nb // CHUNK
    n_steps = nb * nc
    ns = len(shards)
    row_widths = [D_XBC, 128, 128, D_SSM, D_SSM, D_POOL]
    rows = jnp.concatenate([cb, dt_bias, a_log, d_x, nw, pool_scale], axis=1)

    def body(p_ref, cw_ref, rows_ref, pw_ref, *rest):
        shard_refs, (y_ref, ypre_ref, pre_ref, st_ref) = rest[:ns], rest[ns:ns + 4]
        gathered_refs, (xtail, utail, state, e_ref, et_ref, tril_ref) = rest[ns + 4:2 * ns + 4], rest[2 * ns + 4:2 * ns + 10]
        gather = _Gather(shard_refs, gathered_refs, by_cols, *rest[2 * ns + 10:])
        cb_ref, dtb_ref, alog_ref, dx_ref, nw_ref, ps_ref = _row_views(rows_ref, row_widths)
        c = pl.program_id(1)
        step = pl.program_id(0) * nc + c

        @pl.when(step == 0)
        def _():
            gather.start()
            _fill_ssd_constants(e=e_ref, et_f32=et_ref, tril=tril_ref)

        @pl.when(step == n_steps // 2)
        def _():
            gather.forward()

        @pl.when(c == 0)
        def _():
            xtail[...] = jnp.zeros_like(xtail)
            utail[...] = jnp.zeros_like(utail)
            state[...] = jnp.zeros_like(state)

        valid = (c > 0) | (lax.broadcasted_iota(jnp.int32, (CHUNK, 1), 0) >= PAD_ROWS)

        u = p_ref[:, 0:D_POOL]
        inv_cnt, lane = _pool_inv_count(c)
        win = _pool_window_sums(jnp.concatenate([utail[...], u], axis=0), lane)
        utail[...] = u[CHUNK - HALO:, :]
        pooled = win * inv_cnt - u
        mixed = jnp.concatenate(
            [_dot(pooled[:, g * 128:(g + 1) * 128], pw_ref[g]) for g in range(len(POOL_WINDOWS))], axis=1)
        y_ref[:, 0:D_POOL] = _mx(mixed * ps_ref[...])

        xbc = p_ref[:, OFF_X:OFF_X + D_XBC]
        pre = _conv_pre(jnp.concatenate([xtail[...], xbc], axis=0), xbc, cw_ref[...], cb_ref[...])
        xtail[...] = xbc[CHUNK - HALO:, :]
        pre_ref[...] = pre
        xc = pre * _sigmoid(pre)
        dt, _, a_col, _, _ = _dt_and_cumsum(p_ref[:, OFF_DT:OFF_DT + 128], dtb_ref[...], alog_ref[...], valid,
                                            tril_ref[...])
        s_prev = state[...]
        st_ref[0] = s_prev
        yp, s_new = _ssd_chunk_fwd(xc[:, 0:D_SSM], xc[:, D_SSM:D_SSM + 512], xc[:, D_SSM + 512:], dt, a_col, s_prev,
                                   dx_ref[...], e_ref[...], et_ref[...])
        state[...] = s_new
        ypre_ref[...] = yp
        z = p_ref[:, OFF_Z:OFF_Z + D_SSM]
        yz = yp * (z * _sigmoid(z))
        outs = []
        for g in range(N_GROUPS):
            gs = slice(g * GROUP_CH, (g + 1) * GROUP_CH)
            r = lax.rsqrt(jnp.mean(yz[:, gs] * yz[:, gs], axis=-1, keepdims=True) + EPS)
            outs.append(yz[:, gs] * r)
        y_ref[:, D_POOL:] = _mx(jnp.concatenate(outs, axis=1) * nw_ref[...])

        @pl.when(step == n_steps - 1)
        def _():
            gather.finish()

    blk = lambda w: pl.BlockSpec((CHUNK, w), lambda b, c: (_chunk_block(b, c, nb, nc), 0))
    hbm = pl.BlockSpec(memory_space=pl.ANY)
    outs = pl.pallas_call(
        body, grid=(nb, nc), name="mixer_fwd",
        in_specs=[blk(PROJ_W), _full((4, D_XBC)), _full((1, sum(row_widths))), _full((4, 128, 128))] + [hbm] * ns,
        out_specs=[blk(D_MIX), blk(D_SSM), blk(D_XBC),
                   pl.BlockSpec((1, D_SSM, D_STATE), lambda b, c: (b * nc + c, 0, 0))] + [hbm] * ns,
        out_shape=[jax.ShapeDtypeStruct((m, D_MIX), MXU_DTYPE), jax.ShapeDtypeStruct((m, D_SSM), F32),
                   jax.ShapeDtypeStruct((m, D_XBC), F32), jax.ShapeDtypeStruct((m // CHUNK, D_SSM, D_STATE), F32)]
        + _Gather.out_shapes(shards, by_cols),
        scratch_shapes=[pltpu.VMEM((HALO, D_XBC), F32), pltpu.VMEM((HALO, D_POOL), F32),
                        pltpu.VMEM((D_SSM, D_STATE), F32)] + _ssd_constant_scratch(["e", "et_f32", "tril"])
        + _Gather.scratch(ns),
        compiler_params=_params("arbitrary", "arbitrary"),
    )(proj, cw, rows, pool_w, *shards)
    return outs[0], outs[1], outs[2], outs[3], outs[4:]


def _mixer_bwd(proj, dy, ypre, conv_pre, states, cw, dt_bias, a_log, d_x, nw, pool_w, pool_scale, nb, chip_sums):
    m = proj.shape[0]
    nc = m // nb // CHUNK
    hb = CHUNK // HALO
    ns = len(chip_sums)
    row_widths = [128, 128, D_SSM, D_SSM, D_POOL]
    rows = jnp.concatenate([dt_bias, a_log, d_x, nw, pool_scale], axis=1)
    grad_row_widths = [D_XBC, 128, 128, 128, D_SSM, D_POOL]
    constants = ["e", "et", "et_f32", "tril", "triu"]

    def body(p_ref, halo_ref, dy_ref, ypre_ref, pre_ref, st_ref, cw_ref, rows_ref, pw_ref, *rest):
        cs_refs = rest[:ns]
        dp_ref, gcw_ref, grows_ref, gpw_ref = rest[ns:ns + 4]
        part_refs, (ds_carry, dpre_next, dq_next) = rest[ns + 4:2 * ns + 4], rest[2 * ns + 4:2 * ns + 7]
        e_ref, et_ref, etf_ref, tril_ref, triu_ref = rest[2 * ns + 7:2 * ns + 12]
        exchange = _ChipExchange(cs_refs, part_refs, [False] * ns, *rest[2 * ns + 12:])
        dtb_ref, alog_ref, dx_ref, nw_ref, ps_ref = _row_views(rows_ref, row_widths)
        gcb_ref, gdtb_ref, galog_ref, gd_ref, gnw_ref, gps_ref = _row_views(grows_ref, grad_row_widths)
        b = pl.program_id(0)
        cc = pl.program_id(1)
        c = nc - 1 - cc

        @pl.when((b == 0) & (cc == 0))
        def _():
            exchange.start()
            _fill_ssd_constants(e=e_ref, et=et_ref, et_f32=etf_ref, tril=tril_ref, triu=triu_ref)
            for r in (gcw_ref, grows_ref, gpw_ref):
                r[...] = jnp.zeros_like(r)

        @pl.when(cc == 0)
        def _():
            ds_carry[...] = jnp.zeros_like(ds_carry)
            dpre_next[...] = jnp.zeros_like(dpre_next)
            dq_next[...] = jnp.zeros_like(dq_next)

        valid = (c > 0) | (lax.broadcasted_iota(jnp.int32, (CHUNK, 1), 0) >= PAD_ROWS)
        first = c > 0

        u = p_ref[:, 0:D_POOL]
        u_halo = jnp.where(first, halo_ref[...], 0.0)
        inv_cnt, lane = _pool_inv_count(c)
        pooled = _pool_window_sums(jnp.concatenate([u_halo, u], axis=0), lane) * inv_cnt - u
        dyp = dy_ref[:, 0:D_POOL]
        ps = ps_ref[...]
        dmixed = dyp * ps
        mixed, dpooled = [], []
        for g in range(len(POOL_WINDOWS)):
            gsl = slice(g * 128, (g + 1) * 128)
            pw = pw_ref[g]
            mixed.append(_dot(pooled[:, gsl], pw))
            dpooled.append(_dot_nt(dmixed[:, gsl], pw))
            gpw_ref[g] += _dot_tn(pooled[:, gsl], dmixed[:, gsl])
        gps_ref[...] += jnp.sum(dyp * jnp.concatenate(mixed, axis=1), axis=0, keepdims=True)
        dpooled = jnp.concatenate(dpooled, axis=1)
        dq = dpooled * inv_cnt
        du = _pool_window_sums_ahead(jnp.concatenate([dq, dq_next[...]], axis=0), lane) - dpooled
        dq_next[...] = dq[0:HALO, :]
        dp_ref[:, 0:D_POOL] = _mx(du)

        yp = ypre_ref[...]
        z = p_ref[:, OFF_Z:OFF_Z + D_SSM]
        sz, dsz = _silu_and_grad(z)
        yz = yp * sz
        do = dy_ref[:, D_POOL:]
        nw_row = nw_ref[...]
        dyz = []
        gnw = []
        for g in range(N_GROUPS):
            gs = slice(g * GROUP_CH, (g + 1) * GROUP_CH)
            r = lax.rsqrt(jnp.mean(yz[:, gs] * yz[:, gs], axis=-1, keepdims=True) + EPS)
            n = yz[:, gs] * r
            gnw.append(jnp.sum(do[:, gs] * n, axis=0, keepdims=True))
            dn = do[:, gs] * nw_row[:, gs]
            dyz.append(r * (dn - n * jnp.mean(dn * n, axis=-1, keepdims=True)))
        gnw_ref[...] += jnp.concatenate(gnw, axis=1)
        dyz = jnp.concatenate(dyz, axis=1)
        dp_ref[:, OFF_Z:OFF_Z + D_SSM] = _mx(dyz * yp * dsz)
        dyp_ssm = dyz * sz

        xc, dsilu = _silu_and_grad(pre_ref[...])
        dtr = p_ref[:, OFF_DT:OFF_DT + 128]
        dt, a_row, a_col, dt_pre, head = _dt_and_cumsum(dtr, dtb_ref[...], alog_ref[...], valid, tril_ref[...])
        dxs, dbm, dcm, ddt, da, dd, ds_prev = _ssd_chunk_bwd(
            xc[:, 0:D_SSM], xc[:, D_SSM:D_SSM + 512], xc[:, D_SSM + 512:], dt, a_row, a_col, st_ref[0],
            ds_carry[...], dyp_ssm, dx_ref[...], e_ref[...], et_ref[...], etf_ref[...], triu_ref[...])
        ds_carry[...] = ds_prev
        gd_ref[...] += dd
        galog_ref[...] += da * a_row
        ddtr = jnp.where(valid & head, ddt * _sigmoid(dt_pre), 0.0)
        gdtb_ref[...] += jnp.sum(ddtr, axis=0, keepdims=True)
        dp_ref[:, OFF_DT:OFF_DT + 128] = _mx(ddtr)

        dpre = jnp.concatenate([dxs, dbm, dcm], axis=1) * dsilu
        gcb_ref[...] += jnp.sum(dpre, axis=0, keepdims=True)
        dext = jnp.concatenate([dpre, dpre_next[...]], axis=0)
        dpre_next[...] = dpre[0:HALO, :]
        ups = [_shift_up(dext, 3 - k) for k in range(4)]
        xbc = p_ref[:, OFF_X:OFF_X + D_XBC]
        gcw_ref[...] += jnp.concatenate([jnp.sum(xbc * ups[k], axis=0, keepdims=True) for k in range(4)], axis=0)
        cw = cw_ref[...]
        dp_ref[:, OFF_X:OFF_X + D_XBC] = _mx(cw[3:4, :] * ups[3] + cw[2:3, :] * ups[2]
                                             + cw[1:2, :] * ups[1] + cw[0:1, :] * ups[0])

        @pl.when((b == nb - 1) & (cc == nc - 1))
        def _():
            exchange.finish()

    blk = lambda w: pl.BlockSpec((CHUNK, w), lambda b, cc: (_chunk_block(b, nc - 1 - cc, nb, nc), 0))
    halo = pl.BlockSpec((HALO, D_POOL),
                        lambda b, cc: (_chunk_block(b, jnp.maximum(nc - 2 - cc, 0), nb, nc) * hb + hb - 1, 0))
    hbm = pl.BlockSpec(memory_space=pl.ANY)
    outs = pl.pallas_call(
        body, grid=(nb, nc), name="mixer_bwd",
        in_specs=[blk(PROJ_W), halo, blk(D_MIX), blk(D_SSM), blk(D_XBC),
                  pl.BlockSpec((1, D_SSM, D_STATE), lambda b, cc: (b * nc + nc - 1 - cc, 0, 0)),
                  _full((4, D_XBC)), _full((1, sum(row_widths))), _full((4, 128, 128))] + [hbm] * ns,
        out_specs=[blk(PROJ_W), _full((4, D_XBC)), _full((1, sum(grad_row_widths))), _full((4, 128, 128))] + [hbm] * ns,
        out_shape=[jax.ShapeDtypeStruct((m, PROJ_W), MXU_DTYPE), jax.ShapeDtypeStruct((4, D_XBC), F32),
                   jax.ShapeDtypeStruct((1, sum(grad_row_widths)), F32), jax.ShapeDtypeStruct((4, 128, 128), F32)]
        + _ChipExchange.out_shapes(chip_sums, [False] * ns),
        scratch_shapes=[pltpu.VMEM((D_SSM, D_STATE), F32), pltpu.VMEM((HALO, D_XBC), F32),
                        pltpu.VMEM((HALO, D_POOL), F32)] + _ssd_constant_scratch(constants) + _ChipExchange.scratch(ns),
        compiler_params=_params("arbitrary", "arbitrary"),
    )(proj, proj, dy, ypre, conv_pre, states, cw, rows, pool_w, *chip_sums)
    dproj, g_cw, g_rows, g_pw = outs[:4]
    offs = np.cumsum([0] + grad_row_widths)
    g_cb, g_dtb, g_alog, g_d, g_nw, g_ps = (g_rows[:, a:b] for a, b in zip(offs[:-1], offs[1:]))
    return (dproj, g_cw, g_cb, g_dtb, g_alog, g_d, g_nw, g_pw, g_ps), outs[4:]


MESH_IDS = pl.DeviceIdType.MESH
_HBM = pl.BlockSpec(memory_space=pltpu.HBM)


def _coords():
    return lax.axis_index("x"), lax.axis_index("y"), lax.axis_index("c")


def _other_chips(x, y):
    return [(1 - x, y), (x, 1 - y), (1 - x, 1 - y)]


class _Gather:
    def __init__(self, ins, outs, by_cols, send_sems, recv_sems, local_sems):
        self.ins, self.outs, self.by_cols, self.n = ins, outs, by_cols, len(ins)
        self.send_sems, self.recv_sems, self.local_sems = send_sems, recv_sems, local_sems
        self.x, self.y, self.c = _coords()
        self.me, self.sibling = (self.x, self.y, self.c), (self.x, self.y, 1 - self.c)
        self.chips = _other_chips(self.x, self.y)

    @staticmethod
    def scratch(n):
        return [pltpu.SemaphoreType.DMA((7 * n,)), pltpu.SemaphoreType.DMA((7 * n,)), pltpu.SemaphoreType.DMA((n,))]

    @staticmethod
    def out_shapes(shards, by_cols):
        return [jax.ShapeDtypeStruct((s.shape[0], N_DEV * s.shape[1]) if cols else (N_DEV,) + s.shape, s.dtype)
                for s, cols in zip(shards, by_cols)]

    def _block(self, t, device):
        idx = 4 * device[0] + 2 * device[1] + device[2]
        if not self.by_cols[t]:
            return self.outs[t].at[idx]
        w = self.ins[t].shape[1]
        return self.outs[t].at[:, pl.ds(pl.multiple_of(idx * w, w), w)]

    def _copy(self, t, k, block, to, own=False):
        dst = self._block(t, block)
        return pltpu.make_async_remote_copy(
            src_ref=self.ins[t] if own else dst, dst_ref=dst, send_sem=self.send_sems.at[t * 7 + k],
            recv_sem=self.recv_sems.at[t * 7 + k], device_id=to, device_id_type=MESH_IDS)

    def _mine(self):
        return [pltpu.make_async_copy(self.ins[t], self._block(t, self.me), self.local_sems.at[t])
                for t in range(self.n)]

    def _first(self):
        cps = []
        for t in range(self.n):
            cps.append(self._copy(t, 0, self.me, self.sibling, own=True))
            cps += [self._copy(t, 1 + j, self.me, (*chip, self.c), own=True) for j, chip in enumerate(self.chips)]
        return cps

    def _passed(self):
        return [self._copy(t, 4 + j, (*chip, self.c), self.sibling)
                for j, chip in enumerate(self.chips) for t in range(self.n)]

    def start(self):
        for cp in self._mine() + self._first():
            cp.start()

    def forward(self):
        for j, chip in enumerate(self.chips):
            for t in range(self.n):
                self._copy(t, 1 + j, (*chip, self.c), self.me).wait_recv()
                self._copy(t, 4 + j, (*chip, self.c), self.sibling).start()

    def finish(self):
        for t in range(self.n):
            self._copy(t, 0, self.sibling, self.me).wait_recv()
            for j, chip in enumerate(self.chips):
                self._copy(t, 4 + j, (*chip, 1 - self.c), self.me).wait_recv()
        for cp in self._first() + self._passed():
            cp.wait_send()
        for cp in self._mine():
            cp.wait()


def _weight_gather(shards):
    n = len(shards)

    def body(*refs):
        g = _Gather(refs[:n], refs[n:2 * n], [False] * n, *refs[2 * n:])
        g.start()
        g.forward()
        g.finish()

    return pl.pallas_call(
        body, name="weight_gather",
        in_specs=[_HBM] * n, out_specs=[_HBM] * n,
        out_shape=_Gather.out_shapes(shards, [False] * n),
        scratch_shapes=_Gather.scratch(n),
    )(*shards)


def _grad_exchange_d2d(gs, name, swapped=()):
    n, ns = len(gs), len(swapped)

    def body(*refs):
        ins, whole_ins = refs[:n], refs[n:n + ns]
        got, whole_got = refs[n + ns:2 * n + ns], refs[2 * n + ns:2 * (n + ns)]
        send_sems, recv_sems = refs[2 * (n + ns):]
        x, y, c = _coords()
        pairs = [(ins[t].at[k, 1 - c], got[t].at[k]) for t in range(n) for k in range(4)] + list(zip(whole_ins, whole_got))
        remote = [pltpu.make_async_remote_copy(
            src_ref=src, dst_ref=dst, send_sem=send_sems.at[i], recv_sem=recv_sems.at[i], device_id=(x, y, 1 - c),
            device_id_type=MESH_IDS) for i, (src, dst) in enumerate(pairs)]
        for cp in remote:
            cp.start()
        for cp in remote:
            cp.wait_recv()
        for cp in remote:
            cp.wait_send()

    outs = pl.pallas_call(
        body, name=name,
        in_specs=[_HBM] * (n + ns), out_specs=[_HBM] * (n + ns),
        out_shape=[jax.ShapeDtypeStruct((4,) + g.shape[2:], g.dtype) for g in gs]
        + [jax.ShapeDtypeStruct(a.shape, a.dtype) for a in swapped],
        scratch_shapes=[pltpu.SemaphoreType.DMA((4 * n + ns,)), pltpu.SemaphoreType.DMA((4 * n + ns,))],
    )(*gs, *swapped)
    return outs[:n], outs[n:]


def _small_allreduce(pack):
    rows = pack.shape[0]

    def body(p_ref, o_ref, sib_ref, parts_ref, send_sems, recv_sems):
        x, y, c = _coords()
        my_chip = 2 * x + y
        swap = pltpu.make_async_remote_copy(src_ref=p_ref, dst_ref=sib_ref, send_sem=send_sems.at[0],
                                            recv_sem=recv_sems.at[0], device_id=(x, y, 1 - c), device_id_type=MESH_IDS)
        swap.start()
        swap.wait_recv()
        parts_ref[my_chip] = p_ref[...] + sib_ref[...]
        remote = [pltpu.make_async_remote_copy(
            src_ref=parts_ref.at[my_chip], dst_ref=parts_ref.at[my_chip], send_sem=send_sems.at[1 + j],
            recv_sem=recv_sems.at[1 + j], device_id=(cx, cy, c), device_id_type=MESH_IDS)
            for j, (cx, cy) in enumerate(_other_chips(x, y))]
        for cp in remote:
            cp.start()
        for j, (cx, cy) in enumerate(_other_chips(x, y)):
            slot = parts_ref.at[2 * cx + cy]
            pltpu.make_async_remote_copy(src_ref=slot, dst_ref=slot, send_sem=send_sems.at[1 + j],
                                         recv_sem=recv_sems.at[1 + j], device_id=(cx, cy, c),
                                         device_id_type=MESH_IDS).wait_recv()
        o_ref[...] = ((parts_ref[0] + parts_ref[1]) + parts_ref[2]) + parts_ref[3]
        swap.wait_send()
        for cp in remote:
            cp.wait_send()

    vmem = pl.BlockSpec(memory_space=pltpu.VMEM)
    return pl.pallas_call(
        body, name="small_allreduce", in_specs=[vmem], out_specs=vmem,
        out_shape=jax.ShapeDtypeStruct((rows, 128), F32),
        scratch_shapes=[pltpu.VMEM((rows, 128), F32), pltpu.VMEM((4, rows, 128), F32),
                        pltpu.SemaphoreType.DMA((4,)), pltpu.SemaphoreType.DMA((4,))],
    )(pack)


class _ChipExchange:
    def __init__(self, ins, outs, whole, send_sems, recv_sems, local_sems):
        self.ins, self.outs, self.whole, self.n = ins, outs, whole, len(ins)
        self.send_sems, self.recv_sems, self.local_sems = send_sems, recv_sems, local_sems
        self.x, self.y, self.c = _coords()
        self.my_chip = 2 * self.x + self.y
        self.chips = _other_chips(self.x, self.y)

    @staticmethod
    def scratch(n):
        return [pltpu.SemaphoreType.DMA((3 * n,)), pltpu.SemaphoreType.DMA((3 * n,)), pltpu.SemaphoreType.DMA((n,))]

    def _src(self, t, k):
        return self.ins[t] if self.whole[t] else self.ins[t].at[k]

    def _local(self):
        return [pltpu.make_async_copy(self._src(t, self.my_chip), self.outs[t].at[self.my_chip], self.local_sems.at[t])
                for t in range(self.n)]

    def _remote(self):
        return [pltpu.make_async_remote_copy(
            src_ref=self._src(t, 2 * cx + cy), dst_ref=self.outs[t].at[self.my_chip],
            send_sem=self.send_sems.at[t * 3 + j], recv_sem=self.recv_sems.at[t * 3 + j],
            device_id=(cx, cy, self.c), device_id_type=MESH_IDS)
            for t in range(self.n) for j, (cx, cy) in enumerate(self.chips)]

    def start(self):
        for cp in self._remote() + self._local():
            cp.start()

    def finish(self):
        for t in range(self.n):
            for j, (cx, cy) in enumerate(self.chips):
                slot = self.outs[t].at[2 * cx + cy]
                pltpu.make_async_remote_copy(
                    src_ref=slot, dst_ref=slot, send_sem=self.send_sems.at[t * 3 + j],
                    recv_sem=self.recv_sems.at[t * 3 + j], device_id=(cx, cy, self.c),
                    device_id_type=MESH_IDS).wait_recv()
        for cp in self._remote():
            cp.wait_send()
        for cp in self._local():
            cp.wait()

    @staticmethod
    def out_shapes(arrs, whole):
        return [jax.ShapeDtypeStruct(((4,) + a.shape) if w else a.shape, a.dtype) for a, w in zip(arrs, whole)]


_SEMAPHORES = pl.BlockSpec(memory_space=pltpu.SEMAPHORE)
_SIDE_EFFECT = pltpu.SideEffectType.DATAFLOW_SIDE_EFFECTING


def _split_exchange_copies(srcs, lands, whole, send_sems, recv_sems, waiting):
    x, y, c = _coords()
    copies = []
    for t in range(len(srcs)):
        for j, (cx, cy) in enumerate(_other_chips(x, y)):
            src = srcs[t] if whole[t] else srcs[t].at[2 * cx + cy]
            dst = lands[t].at[2 * cx + cy] if waiting else lands[t].at[2 * x + y]
            copies.append(pltpu.make_async_remote_copy(
                src_ref=src, dst_ref=dst, send_sem=send_sems.at[3 * t + j], recv_sem=recv_sems.at[3 * t + j],
                device_id=(cx, cy, c), device_id_type=MESH_IDS))
    return copies


def _chip_exchange_start(arrays, whole, name):
    n = len(arrays)

    def body(*refs):
        srcs, lands = refs[:n], refs[n:2 * n]
        send_sems, recv_sems = refs[2 * n:2 * n + 2]
        for cp in _split_exchange_copies(srcs, lands, whole, send_sems, recv_sems, waiting=False):
            cp.start()
        token = refs[-1]
        token[...] = jnp.zeros_like(token)

    land_shapes = [((4,) + a.shape) if w else a.shape for a, w in zip(arrays, whole)]
    hbm = lambda shape, a: pltpu.HBM(shape, a.dtype)
    outs = pl.pallas_call(
        body, name=name,
        out_shape=(pltpu.SemaphoreType.DMA((3 * n,)), pltpu.SemaphoreType.DMA((3 * n,)),
                   *[hbm(a.shape, a) for a in arrays], *[hbm(s, a) for s, a in zip(land_shapes, arrays)],
                   jax.ShapeDtypeStruct((8, 128), F32)),
        in_specs=(_HBM,) * (2 * n),
        out_specs=(_SEMAPHORES, _SEMAPHORES) + (_HBM,) * (2 * n) + (pl.BlockSpec(memory_space=pltpu.VMEM),),
        input_output_aliases={i: 2 + i for i in range(2 * n)},
        compiler_params=pltpu.CompilerParams(has_side_effects=_SIDE_EFFECT),
    )(*[pltpu.with_memory_space_constraint(a, pltpu.HBM) for a in arrays],
      *[pltpu.with_memory_space_constraint(lax.empty(s, a.dtype), pltpu.HBM) for s, a in zip(land_shapes, arrays)])
    return outs[0], outs[1], outs[2:2 + n], outs[2 + n:2 + 2 * n], outs[-1]


def _chip_exchange_wait(send_sems, recv_sems, srcs, lands, whole, after, name):
    n = len(srcs)

    def body(*refs):
        src_refs, land_refs = refs[:n], refs[n:2 * n]
        for cp in _split_exchange_copies(src_refs, land_refs, whole, refs[2 * n], refs[2 * n + 1], waiting=True):
            cp.wait_send()
            cp.wait_recv()

    outs = pl.pallas_call(
        body, name=name, out_shape=tuple(pltpu.HBM(a.shape, a.dtype) for a in (*srcs, *lands)),
        in_specs=(_HBM,) * (2 * n) + (_SEMAPHORES, _SEMAPHORES, pl.BlockSpec(memory_space=pl.ANY)),
        out_specs=(_HBM,) * (2 * n), input_output_aliases={i: i for i in range(2 * n)},
        compiler_params=pltpu.CompilerParams(has_side_effects=_SIDE_EFFECT),
    )(*srcs, *lands, send_sems, recv_sems, after)
    return outs[:n], outs[n:]


def _sum_two(a, b, name):
    def body(a_ref, b_ref, o_ref):
        o_ref[...] = a_ref[...] + b_ref[...]

    return pl.pallas_call(body, name=name, out_shape=jax.ShapeDtypeStruct(a.shape, a.dtype))(a, b)


def _sum_chips(landed, own, chip, name):
    def body(chip_ref, l_ref, own_ref, o_ref):
        part = lambda k: jnp.where(chip_ref[0] == k, own_ref[...], l_ref[k])
        o_ref[...] = ((part(0) + part(1)) + part(2)) + part(3)

    grid_spec = pltpu.PrefetchScalarGridSpec(
        num_scalar_prefetch=1, grid=(1,),
        in_specs=[pl.BlockSpec(landed.shape, lambda i, c: (0, 0, 0)), pl.BlockSpec(own.shape, lambda i, c: (0, 0))],
        out_specs=pl.BlockSpec(own.shape, lambda i, c: (0, 0)))
    return pl.pallas_call(body, grid_spec=grid_spec, name=name,
                          out_shape=jax.ShapeDtypeStruct(own.shape, own.dtype))(chip, landed, own)


def _row_tile(rows, cols, n_arrays):
    budget = 24 * 1024 * 1024
    padded = -(-cols // 128) * 128
    step = 16 if rows % 16 == 0 else 8
    tr = max(step, budget // (n_arrays * 2 * 4 * padded) // step * step)
    while rows % tr:
        tr -= step
    return tr


def _chip_sum(g, got, core, name):
    rows, cols = g.shape[2:]
    tr = _row_tile(rows, cols, 3)

    def body(c_ref, a_ref, b_ref, o_ref):
        o_ref[...] = (a_ref[...] + b_ref[...].astype(F32)).astype(o_ref.dtype)

    grid_spec = pltpu.PrefetchScalarGridSpec(
        num_scalar_prefetch=1, grid=(4, rows // tr),
        in_specs=[pl.BlockSpec((None, None, tr, cols), lambda k, i, c: (k, c[0], i, 0)),
                  pl.BlockSpec((None, tr, cols), lambda k, i, c: (k, i, 0))],
        out_specs=pl.BlockSpec((None, tr, cols), lambda k, i, c: (k, i, 0)))
    return pl.pallas_call(body, grid_spec=grid_spec, name=name,
                          out_shape=jax.ShapeDtypeStruct((4, rows, cols), MXU_DTYPE),
                          compiler_params=_params("arbitrary", "arbitrary"))(core, g, got)


def _adamw_math(w, g, m, v):
    m2 = ADAM_B1 * m + (1.0 - ADAM_B1) * g
    v2 = ADAM_B2 * v + (1.0 - ADAM_B2) * (g * g)
    m_hat = m2 / (1.0 - ADAM_B1 ** ADAM_STEP)
    v_hat = v2 / (1.0 - ADAM_B2 ** ADAM_STEP)
    delta = -ADAM_LR * (m_hat / (jnp.sqrt(v_hat) + ADAM_EPS) + ADAM_WD * w)
    return delta, m2, v2


def _adamw(parts, w, m, v, name, own=None, chip=None):
    rows, cols = w.shape
    tr = _row_tile(rows, cols, 11 if own is None else 15)

    def body(*refs):
        if own is None:
            p_ref, w_ref, m_ref, v_ref, g_ref, d_ref, m2_ref, v2_ref = refs
            part = lambda k: p_ref[k].astype(F32)
        else:
            chip_ref, p_ref, own_ref, w_ref, m_ref, v_ref, g_ref, d_ref, m2_ref, v2_ref = refs
            part = lambda k: jnp.where(chip_ref[0] == k, own_ref[k], p_ref[k]).astype(F32)
        g = ((part(0) + part(1)) + part(2)) + part(3)
        d, m2, v2 = _adamw_math(w_ref[...], g, m_ref[...], v_ref[...])
        g_ref[...] = g
        d_ref[...] = d
        m2_ref[...] = m2
        v2_ref[...] = v2

    blk = pl.BlockSpec((tr, cols), lambda i, *_: (i, 0))
    pblk = pl.BlockSpec((4, tr, cols), lambda i, *_: (0, i, 0))
    out = jax.ShapeDtypeStruct((rows, cols), F32)
    if own is None:
        return pl.pallas_call(body, grid=(rows // tr,), name=name, in_specs=[pblk, blk, blk, blk],
                              out_specs=[blk] * 4, out_shape=[out] * 4,
                              compiler_params=_params("arbitrary"))(parts, w, m, v)
    grid_spec = pltpu.PrefetchScalarGridSpec(num_scalar_prefetch=1, grid=(rows // tr,),
                                             in_specs=[pblk, pblk, blk, blk, blk], out_specs=[blk] * 4)
    return pl.pallas_call(body, grid_spec=grid_spec, name=name, out_shape=[out] * 4,
                          compiler_params=_params("arbitrary"))(chip, parts, own, w, m, v)


def _adamw_transposed(parts_t, w, m, v, name):
    rows, cols = w.shape
    tc = 256

    def body(p_ref, w_ref, m_ref, v_ref, g_ref, d_ref, m2_ref, v2_ref):
        part = lambda k: p_ref[k].astype(F32)
        g = (((part(0) + part(1)) + part(2)) + part(3)).T
        d, m2, v2 = _adamw_math(w_ref[...], g, m_ref[...], v_ref[...])
        g_ref[...] = g
        d_ref[...] = d
        m2_ref[...] = m2
        v2_ref[...] = v2

    blk = pl.BlockSpec((rows, tc), lambda i: (0, i))
    out = jax.ShapeDtypeStruct((rows, cols), F32)
    return pl.pallas_call(body, grid=(cols // tc,), name=name,
                          in_specs=[pl.BlockSpec((4, tc, rows), lambda i: (0, i, 0)), blk, blk, blk],
                          out_specs=[blk] * 4, out_shape=[out] * 4,
                          compiler_params=_params("arbitrary"))(parts_t, w, m, v)


def _adamw_small(gs, ws, ms, vs):
    n = len(ws)

    def body(*refs):
        g_refs, w_refs, m_refs, v_refs = (refs[k * n:(k + 1) * n] for k in range(4))
        d_refs, m2_refs, v2_refs = (refs[(4 + k) * n:(5 + k) * n] for k in range(3))
        for t in range(n):
            d, m2, v2 = _adamw_math(w_refs[t][...], g_refs[t][...], m_refs[t][...], v_refs[t][...])
            d_refs[t][...] = d
            m2_refs[t][...] = m2
            v2_refs[t][...] = v2

    outs = pl.pallas_call(body, name="adamw_small",
                          out_shape=[jax.ShapeDtypeStruct(w.shape, F32) for w in ws] * 3)(*gs, *ws, *ms, *vs)
    return outs[:n], outs[n:2 * n], outs[2 * n:]


_PACK_TILE = 8 * 128


def _pack(arrays):
    rows = []
    for a in arrays:
        flat = a.astype(F32).reshape(-1)
        rows.append(jnp.pad(flat, (0, -flat.shape[0] % _PACK_TILE)).reshape(-1, 128))
    return jnp.concatenate(rows, axis=0)


def _unpack(pack, shapes):
    out, r = [], 0
    for s in shapes:
        n = int(np.prod(s))
        out.append(pack[r:r + -(-n // 128)].reshape(-1)[:n].reshape(s))
        r += -(-n // _PACK_TILE) * 8
    return out


def _pad128(v):
    v = v.reshape(1, -1).astype(F32)
    return jnp.pad(v, ((0, 0), (0, 128 - v.shape[1])))


_WEIGHTS = ["meta", "norm_mix_w", "w_in", "pool_w", "pool_scale", "conv_w", "conv_b", "dt_bias", "a_log", "d_skip",
            "ssm_norm_w", "w_out", "norm_ffn_w", "w_ff1", "w_ff2", "norm_f_w"]
_BIG = ["w_in", "w_out", "w_ff1", "w_ff2"]
_SMALL = [n for n in _WEIGHTS if n not in _BIG]


def kernel(x, meta, norm_mix_w, w_in, pool_w, pool_scale, conv_w, conv_b, dt_bias, a_log, d_skip, ssm_norm_w, w_out, norm_ffn_w, w_ff1, w_ff2, norm_f_w, loss_target, m_meta, m_norm_mix_w, m_w_in, m_pool_w, m_pool_scale, m_conv_w, m_conv_b, m_dt_bias, m_a_log, m_d_skip, m_ssm_norm_w, m_w_out, m_norm_ffn_w, m_w_ff1, m_w_ff2, m_norm_f_w, v_meta, v_norm_mix_w, v_w_in, v_pool_w, v_pool_scale, v_conv_w, v_conv_b, v_dt_bias, v_a_log, v_d_skip, v_ssm_norm_w, v_w_out, v_norm_ffn_w, v_w_ff1, v_w_ff2, v_norm_f_w):
    wts = dict(meta=meta, norm_mix_w=norm_mix_w, w_in=w_in, pool_w=pool_w, pool_scale=pool_scale, conv_w=conv_w,
               conv_b=conv_b, dt_bias=dt_bias, a_log=a_log, d_skip=d_skip, ssm_norm_w=ssm_norm_w, w_out=w_out,
               norm_ffn_w=norm_ffn_w, w_ff1=w_ff1, w_ff2=w_ff2, norm_f_w=norm_f_w)
    mom1 = dict(zip(_WEIGHTS, (m_meta, m_norm_mix_w, m_w_in, m_pool_w, m_pool_scale, m_conv_w, m_conv_b, m_dt_bias,
                               m_a_log, m_d_skip, m_ssm_norm_w, m_w_out, m_norm_ffn_w, m_w_ff1, m_w_ff2, m_norm_f_w)))
    mom2 = dict(zip(_WEIGHTS, (v_meta, v_norm_mix_w, v_w_in, v_pool_w, v_pool_scale, v_conv_w, v_conv_b, v_dt_bias,
                               v_a_log, v_d_skip, v_ssm_norm_w, v_w_out, v_norm_ffn_w, v_w_ff1, v_w_ff2, v_norm_f_w)))
    xi, yi, ci = _coords()
    dev = 4 * xi + 2 * yi + ci
    win_cols = w_in.shape[-1]
    cw_cols = conv_w.shape[-1]

    nb, seq, _ = x.shape
    core = jnp.reshape(ci, (1,)).astype(jnp.int32)
    owners = lambda a: a.reshape((4, 2) + a.shape[1:])

    lead_pack = jnp.zeros((N_META, 512), F32)
    lead_pack = lead_pack.at[:, :128].set(meta).at[:4, 128:128 + cw_cols].set(conv_w[0])
    g_win, g_lead = _weight_gather([_mx(w_in[0]), lead_pack])
    win_full = _assemble_bands(g_win, PROJ_W)
    meta_full = jnp.transpose(g_lead[:, :, :128], (1, 0, 2)).reshape(N_META, D_MODEL)
    cw_full = jnp.transpose(g_lead[:, :4, 128:128 + cw_cols], (1, 0, 2)).reshape(4, D_XBC)

    lead = jnp.concatenate([jnp.zeros((PAD_ROWS, D_MODEL), F32), meta_full] * nb, axis=0)
    x_rows = x.reshape(nb * seq, D_MODEL)
    tgt_rows = loss_target.reshape(nb * seq, D_MODEL)
    dt_bias_p, a_log_p = _pad128(dt_bias), _pad128(a_log)
    d_x = jnp.repeat(d_skip.reshape(1, N_HEADS).astype(F32), HEAD_DIM, axis=1)
    norm_f_row = norm_f_w.reshape(1, D_MODEL)

    hn1, proj = _in_proj(x_rows, lead, norm_mix_w, win_full)
    late_cols = [False, True, False]
    y, ypre, conv_pre, states, (g_wout, wff1_full, g_wff2) = _mixer_fwd(
        proj, cw_full, conv_b, dt_bias_p, a_log_p, d_x, ssm_norm_w, pool_w[0], pool_scale, nb,
        [_mx(w_out[0]), _mx(w_ff1[0]), _mx(w_ff2[0])], late_cols)
    wout_full = g_wout.reshape(D_MIX, D_MODEL)
    wff2_full = g_wff2.reshape(D_FF, D_MODEL)
    loss, gr_nf, gr_nffn, ff, da, hn2, dh1, dh1_mx, dh2, dy = _ffn_fwd_bwd(
        x_rows, lead, y, tgt_rows, wout_full, norm_ffn_w, wff1_full, wff2_full, norm_f_row)
    gr_wff2 = _tn_matmul(ff, dh2, "grad_w_ff2", tka=1024, max_slab=2816)
    gr_wff1_t = _tn_matmul(da, hn2, "grad_w_ff1", tka=1024, max_slab=2816)
    gr_wout = _tn_matmul(y, dh1_mx, "grad_w_out", tka=1024, max_slab=2816)

    by_owner = lambda k: [owners(gr_wout[k].reshape(N_DEV, D_MIX // N_DEV, D_MODEL)),
                          owners(gr_wff1_t[k].reshape(N_DEV, D_FF // N_DEV, D_MODEL)),
                          owners(gr_wff2[k].reshape(N_DEV, D_FF // N_DEV, D_MODEL))]
    late_parts = by_owner(0)
    late_got, _ = _grad_exchange_d2d(by_owner(1), "grad_exchange_d2d_late")
    late_sums = [_chip_sum(late_parts[t], late_got[t], core, "chip_sum_late_%d" % t) for t in range(3)]
    (dproj, gr_cw, gr_cb, gr_dtb, gr_alog, gr_d, gr_nw, gr_pw, gr_ps), late_exchanged = _mixer_bwd(
        proj, dy, ypre, conv_pre, states, cw_full, dt_bias_p, a_log_p, d_x, ssm_norm_w, pool_w[0], pool_scale, nb,
        late_sums)

    early = dict(pool_w=gr_pw, pool_scale=gr_ps, conv_w=gr_cw, conv_b=gr_cb, dt_bias=gr_dtb[:, :N_HEADS],
                 a_log=gr_alog[:, :N_HEADS], d_skip=gr_d[:, :N_HEADS], ssm_norm_w=gr_nw, norm_ffn_w=gr_nffn,
                 norm_f_w=gr_nf, loss=loss[0:1, 0:1])
    early_pack = _pack(list(early.values()))
    win_parts = [owners(_tn_matmul_banded(hn1, dproj, "grad_w_in", win_cols, tka=512))]
    win_got, (early_got,) = _grad_exchange_d2d(win_parts, "grad_exchange_d2d_w_in", swapped=[early_pack])
    win_sum = _chip_sum(win_parts[0], win_got[0], core, "chip_sum_w_in")
    early_chip = _sum_two(early_pack, early_got, "chip_sum_small")
    whole = [False, True]
    send_sems, recv_sems, sent, landing, started = _chip_exchange_start([win_sum, early_chip], whole, "w_in_exchange_start")
    gx_rows, gr_nmix, gr_meta = _in_proj_bwd(dproj, x_rows, lead, dh1, norm_mix_w, win_full, started)
    (win_sum, early_chip), (win_landed, early_landed) = _chip_exchange_wait(
        send_sems, recv_sems, sent, landing, whole, gr_nmix, "w_in_exchange_wait")
    parts = dict(w_in=win_landed, w_out=late_exchanged[0], w_ff1=late_exchanged[1], w_ff2=late_exchanged[2])
    my_chip = jnp.reshape(2 * xi + yi, (1,)).astype(jnp.int32)
    early_sum = _sum_chips(early_landed, early_chip, my_chip, "small_sum")

    tail = dict(meta=gr_meta, norm_mix_w=gr_nmix)
    tail_sum = _small_allreduce(_pack(list(tail.values())))
    gs = dict(zip(early, _unpack(early_sum, [a.shape for a in early.values()])))
    gs.update(zip(tail, _unpack(tail_sum, [a.shape for a in tail.values()])))
    gs["meta"] = lax.dynamic_slice_in_dim(gs["meta"], dev * 128, 128, axis=1)
    gs["conv_w"] = lax.dynamic_slice_in_dim(gs["conv_w"], dev * cw_cols, cw_cols, axis=1)

    res = {}
    for n in _BIG:
        shp = wts[n].shape
        args = (parts[n], wts[n][0], mom1[n][0], mom2[n][0], "adamw_" + n)
        if n == "w_ff1":
            outs = _adamw_transposed(*args)
        elif n == "w_in":
            outs = _adamw(*args, own=win_sum, chip=my_chip)
        else:
            outs = _adamw(*args)
        res[n] = [o.reshape(shp) for o in outs]
    as2d = lambda a: a.reshape(-1, a.shape[-1])
    small_g = [as2d(gs[n].reshape(wts[n].shape)) for n in _SMALL]
    small_out = _adamw_small(small_g, *[[as2d(d[n]) for n in _SMALL] for d in (wts, mom1, mom2)])
    for k, n in enumerate(_SMALL):
        res[n] = [o[k].reshape(wts[n].shape) for o in (small_g,) + tuple(small_out)]

    grad_x = gx_rows.reshape(nb, seq, D_MODEL)
    return (gs["loss"][0, 0], grad_x, *[res[n][0] for n in _WEIGHTS], *[res[n][1] for n in _WEIGHTS],
            *[res[n][2] for n in _WEIGHTS], *[res[n][3] for n in _WEIGHTS])
```

```python
import numpy as np
import jax
import jax.numpy as jnp
from jax import lax
from jax.experimental import pallas as pl
from jax.experimental.pallas import tpu as pltpu

F32 = jnp.float32
MXU_DTYPE = jnp.bfloat16

D_MODEL = 1024
D_POOL = 512
D_SSM = 1536
D_XBC = 2560
N_HEADS = 24
HEAD_DIM = 64
N_GROUPS = 4
GROUP_CH = D_SSM // N_GROUPS
D_STATE = 128
CHUNK = 128
N_META = 16
LEAD = CHUNK
PAD_ROWS = LEAD - N_META
ROW_TILE = 2 * CHUNK
D_MIX = D_POOL + D_SSM
D_FF = 4096
PROJ_W = 4736
OFF_Z = D_POOL
OFF_X = D_POOL + D_SSM
OFF_DT = OFF_X + D_XBC
D_IN_PROJ = OFF_DT + N_HEADS
POOL_WINDOWS = (2, 4, 8, 16)
HALO = 16
EPS = 1e-5
N_DEV = 8

ADAM_LR, ADAM_B1, ADAM_B2, ADAM_EPS, ADAM_WD, ADAM_STEP = 0.001, 0.9, 0.999, 1e-08, 0.01, 10

VMEM_LIMIT = 60 * 1024 * 1024


def _mx(a):
    return a.astype(MXU_DTYPE)


def _dot(a, b):
    return jnp.dot(_mx(a), _mx(b), preferred_element_type=F32)


def _dot_nt(a, b):
    return lax.dot_general(_mx(a), _mx(b), (((1,), (1,)), ((), ())), preferred_element_type=F32)


def _dot_tn(a, b):
    return lax.dot_general(_mx(a), _mx(b), (((0,), (0,)), ((), ())), preferred_element_type=F32)


def _split3(x):
    hi = x.astype(MXU_DTYPE)
    r = x - hi.astype(F32)
    mid = r.astype(MXU_DTYPE)
    lo = (r - mid.astype(F32)).astype(MXU_DTYPE)
    return hi, mid, lo


def _exact_l(c, x):
    hi, mid, lo = _split3(x)
    f = lambda p: jnp.dot(c, p, preferred_element_type=F32)
    return f(hi) + f(mid) + f(lo)


def _exact_r(x, c):
    hi, mid, lo = x if isinstance(x, tuple) else _split3(x)
    f = lambda p: jnp.dot(p, c, preferred_element_type=F32)
    return f(hi) + f(mid) + f(lo)


def _contract(x, c):
    hi = x.astype(MXU_DTYPE)
    lo = (x - hi.astype(F32)).astype(MXU_DTYPE)
    return jnp.dot(hi, c, preferred_element_type=F32) + jnp.dot(lo, c, preferred_element_type=F32)


def _sigmoid(x):
    return jax.nn.sigmoid(x)


def _softplus(x):
    return jnp.maximum(x, 0.0) + jnp.log1p(jnp.exp(-jnp.abs(x)))


def _silu_and_grad(x):
    s = _sigmoid(x)
    y = x * s
    return y, s + y * (1.0 - s)


def _shift_up(ext, s):
    if s == 0:
        return ext[:CHUNK, :]
    return pltpu.roll(ext, ext.shape[0] - s, 0)[:CHUNK, :]


def _by_pool_group(lane, a2, a4, a8, a16):
    return jnp.where(lane < 128, a2, jnp.where(lane < 256, a4, jnp.where(lane < 384, a8, a16)))


def _pool_inv_count(chunk_idx):
    row = lax.broadcasted_iota(jnp.int32, (CHUNK, D_POOL), 0)
    lane = lax.broadcasted_iota(jnp.int32, (CHUNK, D_POOL), 1)
    pos1 = jnp.maximum(chunk_idx * CHUNK + row - (PAD_ROWS - 1), 1)
    w = _by_pool_group(lane, 2, 4, 8, 16)
    return 1.0 / jnp.minimum(pos1, w).astype(F32), lane


def _pool_window_sums(u_ext, lane):
    s2 = u_ext + pltpu.roll(u_ext, 1, 0)
    s4 = s2 + pltpu.roll(s2, 2, 0)
    s8 = s4 + pltpu.roll(s4, 4, 0)
    s16 = s8 + pltpu.roll(s8, 8, 0)
    return _by_pool_group(lane, s2[HALO:], s4[HALO:], s8[HALO:], s16[HALO:])


def _pool_window_sums_ahead(q_ext, lane):
    n = q_ext.shape[0]
    r2 = q_ext + pltpu.roll(q_ext, n - 1, 0)
    r4 = r2 + pltpu.roll(r2, n - 2, 0)
    r8 = r4 + pltpu.roll(r4, n - 4, 0)
    r16 = r8 + pltpu.roll(r8, n - 8, 0)
    return _by_pool_group(lane, r2[:CHUNK], r4[:CHUNK], r8[:CHUNK], r16[:CHUNK])


def _conv_pre(ext, xbc, cw, cb):
    s1 = pltpu.roll(ext, 1, 0)
    near = cw[3:4, :] * xbc + cw[2:3, :] * s1[HALO:, :]
    far = cw[1:2, :] * ext + cw[0:1, :] * s1
    return cb + near + pltpu.roll(far, 2, 0)[HALO:, :]


def _dt_and_cumsum(dtr, dt_bias, a_log, valid, tril):
    lane = lax.broadcasted_iota(jnp.int32, (CHUNK, 128), 1)
    head = lane < N_HEADS
    pre = dtr + dt_bias
    dt = jnp.where(valid & head, _softplus(pre), 0.0)
    a_row = jnp.where(head[0:1, :], -jnp.exp(a_log), 0.0)
    a_col = _exact_l(tril, dt * a_row)
    return dt, a_row, a_col, pre, head


def _decay(a_col, a_row_t, h, causal):
    seg = a_col[:, h:h + 1] - a_row_t[h:h + 1, :]
    return jnp.where(causal, jnp.exp(jnp.minimum(seg, 0.0)), 0.0)


def _ssd_chunk_fwd(xs, bm, cm, dt, a_col, s_prev, d_x, e_mat, et_f32):
    lane = lax.broadcasted_iota(jnp.int32, (CHUNK, 128), 1)
    rowi = lax.broadcasted_iota(jnp.int32, (CHUNK, CHUNK), 0)
    coli = lax.broadcasted_iota(jnp.int32, (CHUNK, CHUNK), 1)
    causal = rowi >= coli
    a_row_t = a_col.T
    ax = _exact_r(a_col, e_mat)
    dtx = _exact_r(dt, e_mat)
    xdt = xs * dtx
    ax_last = ax[CHUNK - 1:CHUNK, :]
    e_a = jnp.exp(ax)
    w_end = xdt * jnp.exp(ax_last - ax)
    cd_col = jnp.exp(jnp.sum(et_f32 * a_col[CHUNK - 1:CHUNK, :], axis=1, keepdims=True))
    ys, s_new = [], []
    for g in range(N_GROUPS):
        gs = slice(g * GROUP_CH, (g + 1) * GROUP_CH)
        bg = bm[:, g * D_STATE:(g + 1) * D_STATE]
        cg = cm[:, g * D_STATE:(g + 1) * D_STATE]
        sg = s_prev[gs, :]
        cb = _dot_nt(cg, bg)
        y_off = _dot_nt(cg, sg) * e_a[:, gs]
        s_new.append(sg * cd_col[gs, :] + _dot_tn(w_end[:, gs], bg))
        for pr in range(3):
            c0 = g * GROUP_CH + pr * 128
            xdt_p = xdt[:, c0:c0 + 128]
            h0 = g * 6 + pr * 2
            y0 = _dot(cb * _decay(a_col, a_row_t, h0, causal), xdt_p)
            y1 = _dot(cb * _decay(a_col, a_row_t, h0 + 1, causal), xdt_p)
            ys.append(jnp.where(lane < HEAD_DIM, y0, y1) + y_off[:, pr * 128:(pr + 1) * 128])
    y = jnp.concatenate(ys, axis=1) + d_x * xs
    return y, jnp.concatenate(s_new, axis=0)


def _ssd_chunk_bwd(xs, bm, cm, dt, a_row, a_col, s_prev, ds_new, dy, d_x, e_mat, et_mat, et_f32, triu):
    lane = lax.broadcasted_iota(jnp.int32, (CHUNK, 128), 1)
    sub = lax.broadcasted_iota(jnp.int32, (CHUNK, 128), 0)
    rowi = lax.broadcasted_iota(jnp.int32, (CHUNK, CHUNK), 0)
    coli = lax.broadcasted_iota(jnp.int32, (CHUNK, CHUNK), 1)
    causal = rowi >= coli
    a_row_t = a_col.T
    a_last = a_col[CHUNK - 1:CHUNK, :]
    a_split, dt_split = _split3(a_col), _split3(dt)
    sub8 = lax.broadcasted_iota(jnp.int32, (8, GROUP_CH), 0)

    dxs, dbs, dcs, dsp = [], [], [], []
    zcol = jnp.zeros((CHUNK, 128), F32)
    zrows = []
    da_col = jnp.zeros((CHUNK, 128), F32)
    ddt = jnp.zeros((CHUNK, 128), F32)
    head_sums = jnp.zeros((8, 128), F32)
    q_row = jnp.zeros((1, 128), F32)
    for g in range(N_GROUPS):
        gs = slice(g * GROUP_CH, (g + 1) * GROUP_CH)
        e_g, et_g = e_mat[:, gs], et_mat[gs, :]
        xs_g, dy_g = xs[:, gs], dy[:, gs]
        ax = _exact_r(a_split, e_g)
        dtx = _exact_r(dt_split, e_g)
        xdt = xs_g * dtx
        dte = jnp.exp(ax[CHUNK - 1:CHUNK, :] - ax)
        w_end = xdt * dte
        cd_col = jnp.exp(jnp.sum(et_f32[gs, :] * a_last, axis=1, keepdims=True))
        dye = dy_g * jnp.exp(ax)
        bg = bm[:, g * D_STATE:(g + 1) * D_STATE]
        cg = cm[:, g * D_STATE:(g + 1) * D_STATE]
        sg = s_prev[gs, :]
        dsg = ds_new[gs, :]
        cb = _dot_nt(cg, bg)
        cs = _dot_nt(cg, sg)
        dcg = _dot(dye, sg)
        dsp.append(dsg * cd_col + _dot_tn(dye, cg))
        dwg = _dot_nt(bg, dsg)
        dbg = _dot(w_end, dsg)
        ww = dwg * w_end
        t1 = jnp.sum(dsg * sg, axis=1, keepdims=True) * cd_col
        dcb = jnp.zeros((CHUNK, CHUNK), F32)
        pairs = []
        for pr in range(3):
            ps = slice(pr * 128, (pr + 1) * 128)
            xdt_p, dy_p = xdt[:, ps], dy_g[:, ps]
            acc = None
            for half in range(2):
                h = g * 6 + pr * 2 + half
                ld = _decay(a_col, a_row_t, h, causal)
                gm = cb * ld
                dym = jnp.where((lane < HEAD_DIM) if half == 0 else (lane >= HEAD_DIM), dy_p, 0.0)
                dg = _dot_nt(dym, xdt_p)
                dseg = dg * gm
                dcb = dcb + dg * ld
                t = _dot_tn(gm, dym)
                acc = t if acc is None else acc + t
                zcol = jnp.where(lane == h, jnp.sum(dseg, axis=1, keepdims=True), zcol)
                zrows.append(jnp.sum(dseg, axis=0, keepdims=True))
            pairs.append(acc)
        dxdt = dwg * dte + jnp.concatenate(pairs, axis=1)
        dcs.append(dcg + _dot(dcb, bg))
        dbs.append(dbg + _dot_tn(dcb, cg))
        dxs.append(dxdt * dtx + d_x[:, gs] * dy_g)
        da_col = da_col + _contract(dye * cs - ww, et_g)
        ddt = ddt + _contract(dxdt * xs_g, et_g)
        col_sums = jnp.where(sub8 == 0, jnp.sum(dy_g * xs_g, axis=0, keepdims=True),
                             jnp.where(sub8 == 1, jnp.sum(ww, axis=0, keepdims=True), 0.0))
        head_sums = head_sums + _exact_r(col_sums, et_g)
        q_row = q_row + jnp.sum(et_f32[gs, :] * t1, axis=0, keepdims=True)

    dd = head_sums[0:1, :]
    q_row = q_row + head_sums[1:2, :]
    zrow = jnp.concatenate(zrows + [jnp.zeros((128 - N_HEADS, CHUNK), F32)], axis=0)
    da_col = da_col + zcol - zrow.T + jnp.where(sub == CHUNK - 1, q_row, 0.0)
    rc = _exact_l(triu, da_col)
    ddt = ddt + rc * a_row
    da = jnp.sum(rc * dt, axis=0, keepdims=True)
    return (jnp.concatenate(dxs, axis=1), jnp.concatenate(dbs, axis=1), jnp.concatenate(dcs, axis=1), ddt, da, dd,
            jnp.concatenate(dsp, axis=0))


_SSD_CONSTANT_SHAPES = dict(e=((128, D_SSM), MXU_DTYPE), et=((D_SSM, 128), MXU_DTYPE), et_f32=((D_SSM, 128), F32),
                            tril=((CHUNK, CHUNK), MXU_DTYPE), triu=((CHUNK, CHUNK), MXU_DTYPE))


def _ssd_constant_scratch(names):
    return [pltpu.VMEM(*_SSD_CONSTANT_SHAPES[n]) for n in names]


def _fill_ssd_constants(**refs):
    iota = lambda shape, d: lax.broadcasted_iota(jnp.int32, shape, d)
    shift = HEAD_DIM.bit_length() - 1
    marks = dict(
        e=lambda: iota((128, D_SSM), 0) == (iota((128, D_SSM), 1) >> shift),
        et=lambda: iota((D_SSM, 128), 1) == (iota((D_SSM, 128), 0) >> shift),
        et_f32=lambda: iota((D_SSM, 128), 1) == (iota((D_SSM, 128), 0) >> shift),
        tril=lambda: iota((CHUNK, CHUNK), 1) <= iota((CHUNK, CHUNK), 0),
        triu=lambda: iota((CHUNK, CHUNK), 1) >= iota((CHUNK, CHUNK), 0))
    for name, ref in refs.items():
        ref[...] = jnp.where(marks[name](), 1.0, 0.0).astype(ref.dtype)


def _row_views(ref, widths):
    views, off = [], 0
    for w in widths:
        views.append(ref.at[:, off:off + w])
        off += w
    return views


def _full(shape):
    nd = len(shape)
    return pl.BlockSpec(shape, lambda *_: (0,) * nd)


def _params(*sem):
    return pltpu.CompilerParams(dimension_semantics=sem, vmem_limit_bytes=VMEM_LIMIT)


def _token_tiles(width, n_tok_tiles):
    return pl.BlockSpec((ROW_TILE, width), lambda i: (jnp.minimum(i, n_tok_tiles - 1), 0))


def _in_proj(x, lead, w1, win):
    nt = x.shape[0] // ROW_TILE
    m = x.shape[0] + ROW_TILE
    tm = ROW_TILE

    def body(x_ref, lead_ref, w1_ref, win_hbm, hn_ref, proj_ref, win_v, sem):
        i = pl.program_id(0)

        @pl.when(i == 0)
        def _():
            cp = pltpu.make_async_copy(win_hbm, win_v, sem)
            cp.start()
            cp.wait()

        x = jnp.where(i == nt, lead_ref[...], x_ref[...])
        r = lax.rsqrt(jnp.mean(x * x, axis=-1, keepdims=True) + EPS)
        hn = _mx(x * r * w1_ref[...])
        hn_ref[...] = hn
        for j in range(0, PROJ_W, 512):
            w = min(512, PROJ_W - j)
            proj_ref[:, j:j + w] = jnp.dot(hn, win_v[:, j:j + w], preferred_element_type=F32)

    return pl.pallas_call(
        body, grid=(m // tm,), name="in_proj",
        in_specs=[_token_tiles(D_MODEL, nt), _full((ROW_TILE, D_MODEL)), _full((1, D_MODEL)),
                  pl.BlockSpec(memory_space=pl.ANY)],
        out_specs=[pl.BlockSpec((tm, D_MODEL), lambda i: (i, 0)), pl.BlockSpec((tm, PROJ_W), lambda i: (i, 0))],
        out_shape=[jax.ShapeDtypeStruct((m, D_MODEL), MXU_DTYPE), jax.ShapeDtypeStruct((m, PROJ_W), F32)],
        scratch_shapes=[pltpu.VMEM((D_MODEL, PROJ_W), MXU_DTYPE), pltpu.SemaphoreType.DMA],
        compiler_params=_params("arbitrary"),
    )(x, lead, w1, win)


def _ffn_fwd_bwd(x, lead, y, tgt, wout, w2n, wff1, wff2, wfn):
    nt = x.shape[0] // ROW_TILE
    m = x.shape[0] + ROW_TILE
    tm = ROW_TILE
    nj = D_FF // 1024

    def body(x_ref, lead_ref, y_ref, tgt_ref, w2n_ref, wfn_ref, wout_hbm, wff1_hbm, wff2_hbm,
             loss_ref, gwf_ref, gw2_ref, ff_ref, da_ref, hn2_ref, dh1_ref, dh2_ref, dy_ref,
             wout_v, wff1_v, wff2_v, a_s, sems):
        i = pl.program_id(0)
        hp = jnp.where(i == nt, lead_ref[...], x_ref[...])

        @pl.when(i == 0)
        def _():
            cps = [pltpu.make_async_copy(s, d, sems.at[k])
                   for k, (s, d) in enumerate(((wout_hbm, wout_v), (wff1_hbm, wff1_v), (wff2_hbm, wff2_v)))]
            for cp in cps:
                cp.start()
            for cp in cps:
                cp.wait()
            loss_ref[...] = jnp.zeros_like(loss_ref)
            gwf_ref[...] = jnp.zeros_like(gwf_ref)
            gw2_ref[...] = jnp.zeros_like(gw2_ref)

        h1 = hp + jnp.dot(y_ref[...], wout_v[...], preferred_element_type=F32)
        r2 = lax.rsqrt(jnp.mean(h1 * h1, axis=-1, keepdims=True) + EPS)
        n2 = h1 * r2
        w2n_row = w2n_ref[...]
        hn2 = _mx(n2 * w2n_row)
        hn2_ref[...] = hn2
        h2 = h1
        for j in range(nj):
            js = slice(j * 1024, (j + 1) * 1024)
            a = jnp.dot(hn2, wff1_v[:, js], preferred_element_type=F32)
            a_s[:, js] = a
            ra = jnp.maximum(a, 0.0)
            ff = _mx(ra * ra)
            ff_ref[:, js] = ff
            h2 = h2 + jnp.dot(ff, wff2_v[js, :], preferred_element_type=F32)

        r3 = lax.rsqrt(jnp.mean(h2 * h2, axis=-1, keepdims=True) + EPS)
        n3 = h2 * r3
        wf_row = wfn_ref[...]
        err = n3 * wf_row - tgt_ref[...]
        tokf = (i < nt).astype(F32)
        loss_ref[...] += 0.5 * jnp.sum(jnp.mean(err * err, axis=-1, keepdims=True) * tokf)
        dout = err * (tokf / D_MODEL)
        gwf_ref[...] += jnp.sum(dout * n3, axis=0, keepdims=True)
        dn3 = dout * wf_row
        dh2 = r3 * (dn3 - n3 * jnp.mean(dn3 * n3, axis=-1, keepdims=True))
        dh2m = _mx(dh2)
        dh2_ref[...] = dh2m

        dhn2 = jnp.zeros((tm, D_MODEL), F32)
        for j in range(nj):
            js = slice(j * 1024, (j + 1) * 1024)
            dff = lax.dot_general(dh2m, wff2_v[js, :], (((1,), (1,)), ((), ())), preferred_element_type=F32)
            da = _mx(dff * (2.0 * jnp.maximum(a_s[:, js], 0.0)))
            da_ref[:, js] = da
            dhn2 = dhn2 + lax.dot_general(da, wff1_v[:, js], (((1,), (1,)), ((), ())), preferred_element_type=F32)
        gw2_ref[...] += jnp.sum(dhn2 * n2, axis=0, keepdims=True)
        dn2 = dhn2 * w2n_row
        dh1 = dh2 + r2 * (dn2 - n2 * jnp.mean(dn2 * n2, axis=-1, keepdims=True))
        dh1_ref[...] = dh1
        dy_ref[...] = lax.dot_general(_mx(dh1), wout_v[...], (((1,), (1,)), ((), ())), preferred_element_type=F32)

    rows = lambda w: pl.BlockSpec((tm, w), lambda i: (i, 0))
    hbm = pl.BlockSpec(memory_space=pl.ANY)
    return pl.pallas_call(
        body, grid=(m // tm,), name="ffn_fwd_bwd",
        in_specs=[_token_tiles(D_MODEL, nt), _full((ROW_TILE, D_MODEL)), rows(D_MIX), _token_tiles(D_MODEL, nt),
                  _full((1, D_MODEL)), _full((1, D_MODEL)), hbm, hbm, hbm],
        out_specs=[_full((1, 128)), _full((1, D_MODEL)), _full((1, D_MODEL)), rows(D_FF), rows(D_FF), rows(D_MODEL),
                   rows(D_MODEL), rows(D_MODEL), rows(D_MIX)],
        out_shape=[jax.ShapeDtypeStruct((1, 128), F32), jax.ShapeDtypeStruct((1, D_MODEL), F32),
                   jax.ShapeDtypeStruct((1, D_MODEL), F32), jax.ShapeDtypeStruct((m, D_FF), MXU_DTYPE),
                   jax.ShapeDtypeStruct((m, D_FF), MXU_DTYPE), jax.ShapeDtypeStruct((m, D_MODEL), MXU_DTYPE),
                   jax.ShapeDtypeStruct((m, D_MODEL), F32), jax.ShapeDtypeStruct((m, D_MODEL), MXU_DTYPE),
                   jax.ShapeDtypeStruct((m, D_MIX), F32)],
        scratch_shapes=[pltpu.VMEM((D_MIX, D_MODEL), MXU_DTYPE), pltpu.VMEM((D_MODEL, D_FF), MXU_DTYPE),
                        pltpu.VMEM((D_FF, D_MODEL), MXU_DTYPE), pltpu.VMEM((tm, D_FF), F32),
                        pltpu.SemaphoreType.DMA((3,))],
        compiler_params=_params("arbitrary"),
    )(x, lead, y, tgt, w2n, wfn, wout, wff1, wff2)


def _in_proj_bwd(dproj, x, lead, dh1, w1, win, after):
    nt = x.shape[0] // ROW_TILE
    m = x.shape[0] + ROW_TILE
    tm = ROW_TILE

    def body(dp_ref, x_ref, lead_ref, dh1_ref, w1_ref, win_hbm, after_ref, gx_ref, gw1_ref, gmeta_ref, win_v, sem):
        i = pl.program_id(0)

        @pl.when(i == 0)
        def _():
            cp = pltpu.make_async_copy(win_hbm, win_v, sem)
            cp.start()
            cp.wait()
            gw1_ref[...] = jnp.zeros_like(gw1_ref)
            gmeta_ref[...] = jnp.zeros_like(gmeta_ref)

        dhn = lax.dot_general(dp_ref[...], win_v[...], (((1,), (1,)), ((), ())), preferred_element_type=F32)
        x = jnp.where(i == nt, lead_ref[...], x_ref[...])
        r = lax.rsqrt(jnp.mean(x * x, axis=-1, keepdims=True) + EPS)
        n = x * r
        gw1_ref[...] += jnp.sum(dhn * n, axis=0, keepdims=True)
        dn = dhn * w1_ref[...]
        dh0 = dh1_ref[...] + r * (dn - n * jnp.mean(dn * n, axis=-1, keepdims=True))

        @pl.when(i < nt)
        def _():
            gx_ref[...] = dh0

        @pl.when(i == nt)
        def _():
            gmeta_ref[...] = dh0[PAD_ROWS:LEAD, :] + dh0[LEAD + PAD_ROWS:2 * LEAD, :]

    rows = lambda w: pl.BlockSpec((tm, w), lambda i: (i, 0))
    hbm = pl.BlockSpec(memory_space=pl.ANY)
    return pl.pallas_call(
        body, grid=(m // tm,), name="in_proj_bwd",
        in_specs=[rows(PROJ_W), _token_tiles(D_MODEL, nt), _full((ROW_TILE, D_MODEL)), rows(D_MODEL),
                  _full((1, D_MODEL)), hbm, hbm],
        out_specs=[_token_tiles(D_MODEL, nt), _full((1, D_MODEL)), _full((N_META, D_MODEL))],
        out_shape=[jax.ShapeDtypeStruct(x.shape, F32), jax.ShapeDtypeStruct((1, D_MODEL), F32),
                   jax.ShapeDtypeStruct((N_META, D_MODEL), F32)],
        scratch_shapes=[pltpu.VMEM((D_MODEL, PROJ_W), MXU_DTYPE), pltpu.SemaphoreType.DMA],
        compiler_params=_params("arbitrary"),
    )(dproj, x, lead, dh1, w1, win, after)


MXU_DEPTH = 256


def _row_slab(m, cap):
    return max(k for k in range(MXU_DEPTH, cap + 1, MXU_DEPTH) if m % k == 0)


def _tn_matmul(a, b, name, tka, max_slab=768, tn=512):
    m, ka = a.shape
    nb = b.shape[1]
    tkm = _row_slab(m, max_slab)
    n_steps = m // tkm

    def body(a_ref, b_ref, o_ref, omx_ref):
        k = pl.program_id(1)

        @pl.when(k == 0)
        def _():
            o_ref[...] = jnp.zeros_like(o_ref)

        at = _mx(a_ref[...])
        for j in range(0, nb, tn):
            w = min(tn, nb - j)
            o_ref[:, j:j + w] += lax.dot_general(at, _mx(b_ref[:, j:j + w]), (((0,), (0,)), ((), ())),
                                                 preferred_element_type=F32)

        @pl.when(k == n_steps - 1)
        def _():
            omx_ref[...] = _mx(o_ref[...])

    out = pl.BlockSpec((tka, nb), lambda i, k: (i, 0))
    return pl.pallas_call(
        body, grid=(ka // tka, n_steps), name=name,
        in_specs=[pl.BlockSpec((tkm, tka), lambda i, k: (k, i)), pl.BlockSpec((tkm, nb), lambda i, k: (k, 0))],
        out_specs=[out, out],
        out_shape=[jax.ShapeDtypeStruct((ka, nb), F32), jax.ShapeDtypeStruct((ka, nb), MXU_DTYPE)],
        compiler_params=_params("arbitrary", "arbitrary"),
    )(a, b)


def _tn_matmul_banded(a, b, name, band, tka, tn=512):
    m, ka = a.shape
    nb = b.shape[1]
    tkm = _row_slab(m, 768)
    n_steps = m // tkm

    def body(a_ref, b_ref, o_ref, acc):
        k = pl.program_id(1)

        @pl.when(k == 0)
        def _():
            acc[...] = jnp.zeros_like(acc)

        at = _mx(a_ref[...])
        for j in range(0, nb, tn):
            w = min(tn, nb - j)
            acc[:, j:j + w] += lax.dot_general(at, _mx(b_ref[:, j:j + w]), (((0,), (0,)), ((), ())),
                                               preferred_element_type=F32)

        @pl.when(k == n_steps - 1)
        def _():
            for j in range(N_DEV):
                o_ref[j] = acc[:, j * band:(j + 1) * band]

    return pl.pallas_call(
        body, grid=(ka // tka, n_steps), name=name,
        in_specs=[pl.BlockSpec((tkm, tka), lambda i, k: (k, i)), pl.BlockSpec((tkm, nb), lambda i, k: (k, 0))],
        out_specs=pl.BlockSpec((N_DEV, tka, band), lambda i, k: (0, i, 0)),
        out_shape=jax.ShapeDtypeStruct((N_DEV, ka, band), F32),
        scratch_shapes=[pltpu.VMEM((tka, nb), F32)],
        compiler_params=_params("arbitrary", "arbitrary"),
    )(a, b)


def _assemble_bands(g, width):
    n, rows, band = g.shape
    tr = 256

    def body(g_ref, o_ref):
        parts = [g_ref[j] for j in range(n)] + [jnp.zeros((tr, width - n * band), g.dtype)]
        o_ref[...] = jnp.concatenate(parts, axis=1)

    return pl.pallas_call(
        body, grid=(rows // tr,), name="assemble_w_in",
        in_specs=[pl.BlockSpec((n, tr, band), lambda i: (0, i, 0))],
        out_specs=pl.BlockSpec((tr, width), lambda i: (i, 0)),
        out_shape=jax.ShapeDtypeStruct((rows, width), g.dtype),
        compiler_params=_params("arbitrary"),
    )(g)


def _chunk_block(b, c, nb, nc):
    return jnp.where(c == 0, nb * (nc - 1) + b, b * (nc - 1) + c - 1)

def _mixer_fwd(proj, cw, cb, dt_bias, a_log, d_x, nw, pool_w, pool_scale, nb, shards, by_cols):
    m = proj.shape[0]
    nc = m // nb // CHUNK
    n_steps = nb * nc
    ns = len(shards)
    row_widths = [D_XBC, 128, 128, D_SSM, D_SSM, D_POOL]
    rows = jnp.concatenate([cb, dt_bias, a_log, d_x, nw, pool_scale], axis=1)

    def body(p_ref, cw_ref, rows_ref, pw_ref, *rest):
        shard_refs, (y_ref, ypre_ref, pre_ref, st_ref) = rest[:ns], rest[ns:ns + 4]
        gathered_refs, (xtail, utail, state, e_ref, et_ref, tril_ref) = rest[ns + 4:2 * ns + 4], rest[2 * ns + 4:2 * ns + 10]
        gather = _Gather(shard_refs, gathered_refs, by_cols, *rest[2 * ns + 10:])
        cb_ref, dtb_ref, alog_ref, dx_ref, nw_ref, ps_ref = _row_views(rows_ref, row_widths)
        c = pl.program_id(1)
        step = pl.program_id(0) * nc + c

        @pl.when(step == 0)
        def _():
            gather.start()
            _fill_ssd_constants(e=e_ref, et_f32=et_ref, tril=tril_ref)

        @pl.when(step == n_steps // 2)
        def _():
            gather.forward()

        @pl.when(c == 0)
        def _():
            xtail[...] = jnp.zeros_like(xtail)
            utail[...] = jnp.zeros_like(utail)
            state[...] = jnp.zeros_like(state)

        valid = (c > 0) | (lax.broadcasted_iota(jnp.int32, (CHUNK, 1), 0) >= PAD_ROWS)

        u = p_ref[:, 0:D_POOL]
        inv_cnt, lane = _pool_inv_count(c)
        win = _pool_window_sums(jnp.concatenate([utail[...], u], axis=0), lane)
        utail[...] = u[CHUNK - HALO:, :]
        pooled = win * inv_cnt - u
        mixed = jnp.concatenate(
            [_dot(pooled[:, g * 128:(g + 1) * 128], pw_ref[g]) for g in range(len(POOL_WINDOWS))], axis=1)
        y_ref[:, 0:D_POOL] = _mx(mixed * ps_ref[...])

        xbc = p_ref[:, OFF_X:OFF_X + D_XBC]
        pre = _conv_pre(jnp.concatenate([xtail[...], xbc], axis=0), xbc, cw_ref[...], cb_ref[...])
        xtail[...] = xbc[CHUNK - HALO:, :]
        pre_ref[...] = pre
        xc = pre * _sigmoid(pre)
        dt, _, a_col, _, _ = _dt_and_cumsum(p_ref[:, OFF_DT:OFF_DT + 128], dtb_ref[...], alog_ref[...], valid,
                                            tril_ref[...])
        s_prev = state[...]
        st_ref[0] = s_prev
        yp, s_new = _ssd_chunk_fwd(xc[:, 0:D_SSM], xc[:, D_SSM:D_SSM + 512], xc[:, D_SSM + 512:], dt, a_col, s_prev,
                                   dx_ref[...], e_ref[...], et_ref[...])
        state[...] = s_new
        ypre_ref[...] = yp
        z = p_ref[:, OFF_Z:OFF_Z + D_SSM]
        yz = yp * (z * _sigmoid(z))
        outs = []
        for g in range(N_GROUPS):
            gs = slice(g * GROUP_CH, (g + 1) * GROUP_CH)
            r = lax.rsqrt(jnp.mean(yz[:, gs] * yz[:, gs], axis=-1, keepdims=True) + EPS)
            outs.append(yz[:, gs] * r)
        y_ref[:, D_POOL:] = _mx(jnp.concatenate(outs, axis=1) * nw_ref[...])

        @pl.when(step == n_steps - 1)
        def _():
            gather.finish()

    blk = lambda w: pl.BlockSpec((CHUNK, w), lambda b, c: (_chunk_block(b, c, nb, nc), 0))
    hbm = pl.BlockSpec(memory_space=pl.ANY)
    outs = pl.pallas_call(
        body, grid=(nb, nc), name="mixer_fwd",
        in_specs=[blk(PROJ_W), _full((4, D_XBC)), _full((1, sum(row_widths))), _full((4, 128, 128))] + [hbm] * ns,
        out_specs=[blk(D_MIX), blk(D_SSM), blk(D_XBC),
                   pl.BlockSpec((1, D_SSM, D_STATE), lambda b, c: (_chunk_block(b, c, nb, nc), 0, 0))] + [hbm] * ns,
        out_shape=[jax.ShapeDtypeStruct((m, D_MIX), MXU_DTYPE), jax.ShapeDtypeStruct((m, D_SSM), F32),
                   jax.ShapeDtypeStruct((m, D_XBC), F32), jax.ShapeDtypeStruct((m // CHUNK, D_SSM, D_STATE), F32)]
        + _Gather.out_shapes(shards, by_cols),
        scratch_shapes=[pltpu.VMEM((HALO, D_XBC), F32), pltpu.VMEM((HALO, D_POOL), F32),
                        pltpu.VMEM((D_SSM, D_STATE), F32)] + _ssd_constant_scratch(["e", "et_f32", "tril"])
        + _Gather.scratch(ns),
        compiler_params=_params("arbitrary", "arbitrary"),
    )(proj, cw, rows, pool_w, *shards)
    return outs[0], outs[1], outs[2], outs[3], outs[4:]


def _mixer_bwd(proj, dy, ypre, conv_pre, states, cw, dt_bias, a_log, d_x, nw, pool_w, pool_scale, nb, chip_sums):
    m = proj.shape[0]
    nc = m // nb // CHUNK
    hb = CHUNK // HALO
    ns = len(chip_sums)
    row_widths = [128, 128, D_SSM, D_SSM, D_POOL]
    rows = jnp.concatenate([dt_bias, a_log, d_x, nw, pool_scale], axis=1)
    grad_row_widths = [D_XBC, 128, 128, 128, D_SSM, D_POOL]
    constants = ["e", "et", "et_f32", "tril", "triu"]

    pairs = (nc - 1) // 2
    n_steps = nb * pairs + 1

    def block_of(step):
        b = jnp.minimum(step // pairs, nb - 1)
        j = pairs - 1 - (step - b * pairs)
        return jnp.where(step == n_steps - 1, nb * pairs, b * pairs + j)

    def chunks_of(step):
        lead = step == n_steps - 1
        b = jnp.minimum(step // pairs, nb - 1)
        j = pairs - 1 - (step - b * pairs)
        return [(jnp.where(lead, 0, b), jnp.where(lead, 0, 2 * j + 2), jnp.where(lead, 0, CHUNK)),
                (jnp.where(lead, 1, b), jnp.where(lead, 0, 2 * j + 1), jnp.where(lead, CHUNK, 0))]

    def body(p_ref, halo_ref, dy_ref, ypre_ref, pre_ref, st_ref, cw_ref, rows_ref, pw_ref, *rest):
        cs_refs = rest[:ns]
        dp_ref, gcw_ref, grows_ref, gpw_ref = rest[ns:ns + 4]
        part_refs, carries, constants_refs = rest[ns + 4:2 * ns + 4], rest[2 * ns + 4:2 * ns + 7], rest[2 * ns + 7:2 * ns + 12]
        exchange = _ChipExchange(cs_refs, part_refs, [False] * ns, *rest[2 * ns + 12:])
        step = pl.program_id(0)

        @pl.when(step == 0)
        def _():
            exchange.start()
            _fill_ssd_constants(**dict(zip(constants, constants_refs)))
            for r in (gcw_ref, grows_ref, gpw_ref):
                r[...] = jnp.zeros_like(r)

        shared = (cw_ref, pw_ref, _row_views(rows_ref, row_widths), gcw_ref, gpw_ref,
                  _row_views(grows_ref, grad_row_widths), constants_refs)
        halos = [p_ref[CHUNK - HALO:CHUNK, 0:D_POOL], halo_ref[...]]
        for (b, c, row0), halo in zip(chunks_of(step), halos):
            rows_here = pl.ds(pl.multiple_of(row0, CHUNK), CHUNK)
            at = lambda ref: ref.at[rows_here]
            one_chunk(b, c, at(p_ref), halo, at(dy_ref), at(ypre_ref), at(pre_ref), st_ref.at[row0 // CHUNK], at(dp_ref),
                      *shared, [carry.at[b] for carry in carries])

        @pl.when(step == n_steps - 1)
        def _():
            exchange.finish()

    def one_chunk(b, c, p_ref, halo, dy_ref, ypre_ref, pre_ref, st_ref, dp_ref, cw_ref, pw_ref, row_refs, gcw_ref,
                  gpw_ref, grad_row_refs, constants_refs, carries):
        dtb_ref, alog_ref, dx_ref, nw_ref, ps_ref = row_refs
        gcb_ref, gdtb_ref, galog_ref, gd_ref, gnw_ref, gps_ref = grad_row_refs
        ds_carry, dpre_next, dq_next = carries
        e_ref, et_ref, etf_ref, tril_ref, triu_ref = constants_refs

        @pl.when(c == nc - 1)
        def _():
            ds_carry[...] = jnp.zeros_like(ds_carry)
            dpre_next[...] = jnp.zeros_like(dpre_next)
            dq_next[...] = jnp.zeros_like(dq_next)

        valid = (c > 0) | (lax.broadcasted_iota(jnp.int32, (CHUNK, 1), 0) >= PAD_ROWS)
        first = c > 0

        u = p_ref[:, 0:D_POOL]
        u_halo = jnp.where(first, halo, 0.0)
        inv_cnt, lane = _pool_inv_count(c)
        pooled = _pool_window_sums(jnp.concatenate([u_halo, u], axis=0), lane) * inv_cnt - u
        dyp = dy_ref[:, 0:D_POOL]
        ps = ps_ref[...]
        dmixed = dyp * ps
        mixed, dpooled = [], []
        for g in range(len(POOL_WINDOWS)):
            gsl = slice(g * 128, (g + 1) * 128)
            pw = pw_ref[g]
            mixed.append(_dot(pooled[:, gsl], pw))
            dpooled.append(_dot_nt(dmixed[:, gsl], pw))
            gpw_ref[g] += _dot_tn(pooled[:, gsl], dmixed[:, gsl])
        gps_ref[...] += jnp.sum(dyp * jnp.concatenate(mixed, axis=1), axis=0, keepdims=True)
        dpooled = jnp.concatenate(dpooled, axis=1)
        dq = dpooled * inv_cnt
        du = _pool_window_sums_ahead(jnp.concatenate([dq, dq_next[...]], axis=0), lane) - dpooled
        dq_next[...] = dq[0:HALO, :]
        dp_ref[:, 0:D_POOL] = _mx(du)

        yp = ypre_ref[...]
        z = p_ref[:, OFF_Z:OFF_Z + D_SSM]
        sz, dsz = _silu_and_grad(z)
        yz = yp * sz
        do = dy_ref[:, D_POOL:]
        nw_row = nw_ref[...]
        dyz = []
        gnw = []
        for g in range(N_GROUPS):
            gs = slice(g * GROUP_CH, (g + 1) * GROUP_CH)
            r = lax.rsqrt(jnp.mean(yz[:, gs] * yz[:, gs], axis=-1, keepdims=True) + EPS)
            n = yz[:, gs] * r
            gnw.append(jnp.sum(do[:, gs] * n, axis=0, keepdims=True))
            dn = do[:, gs] * nw_row[:, gs]
            dyz.append(r * (dn - n * jnp.mean(dn * n, axis=-1, keepdims=True)))
        gnw_ref[...] += jnp.concatenate(gnw, axis=1)
        dyz = jnp.concatenate(dyz, axis=1)
        dp_ref[:, OFF_Z:OFF_Z + D_SSM] = _mx(dyz * yp * dsz)
        dyp_ssm = dyz * sz

        xc, dsilu = _silu_and_grad(pre_ref[...])
        dtr = p_ref[:, OFF_DT:OFF_DT + 128]
        dt, a_row, a_col, dt_pre, head = _dt_and_cumsum(dtr, dtb_ref[...], alog_ref[...], valid, tril_ref[...])
        dxs, dbm, dcm, ddt, da, dd, ds_prev = _ssd_chunk_bwd(
            xc[:, 0:D_SSM], xc[:, D_SSM:D_SSM + 512], xc[:, D_SSM + 512:], dt, a_row, a_col, st_ref[...],
            ds_carry[...], dyp_ssm, dx_ref[...], e_ref[...], et_ref[...], etf_ref[...], triu_ref[...])
        ds_carry[...] = ds_prev
        gd_ref[...] += dd
        galog_ref[...] += da * a_row
        ddtr = jnp.where(valid & head, ddt * _sigmoid(dt_pre), 0.0)
        gdtb_ref[...] += jnp.sum(ddtr, axis=0, keepdims=True)
        dp_ref[:, OFF_DT:OFF_DT + 128] = _mx(ddtr)

        dpre = jnp.concatenate([dxs, dbm, dcm], axis=1) * dsilu
        gcb_ref[...] += jnp.sum(dpre, axis=0, keepdims=True)
        dext = jnp.concatenate([dpre, dpre_next[...]], axis=0)
        dpre_next[...] = dpre[0:HALO, :]
        ups = [_shift_up(dext, 3 - k) for k in range(4)]
        xbc = p_ref[:, OFF_X:OFF_X + D_XBC]
        gcw_ref[...] += jnp.concatenate([jnp.sum(xbc * ups[k], axis=0, keepdims=True) for k in range(4)], axis=0)
        cw = cw_ref[...]
        dp_ref[:, OFF_X:OFF_X + D_XBC] = _mx(cw[3:4, :] * ups[3] + cw[2:3, :] * ups[2]
                                             + cw[1:2, :] * ups[1] + cw[0:1, :] * ups[0])

    blk = lambda w: pl.BlockSpec((2 * CHUNK, w), lambda s: (block_of(s), 0))

    def halo_rows(s):
        b = jnp.minimum(s // pairs, nb - 1)
        below = block_of(s) * 2 * hb - 1
        return jnp.where(block_of(s) == b * pairs, _chunk_block(b, 0, nb, nc) * hb + hb - 1, below)

    hbm = pl.BlockSpec(memory_space=pl.ANY)
    outs = pl.pallas_call(
        body, grid=(n_steps,), name="mixer_bwd",
        in_specs=[blk(PROJ_W), pl.BlockSpec((HALO, D_POOL), lambda s: (halo_rows(s), 0)), blk(D_MIX), blk(D_SSM),
                  blk(D_XBC), pl.BlockSpec((2, D_SSM, D_STATE), lambda s: (block_of(s), 0, 0)),
                  _full((4, D_XBC)), _full((1, sum(row_widths))), _full((4, 128, 128))] + [hbm] * ns,
        out_specs=[blk(PROJ_W), _full((4, D_XBC)), _full((1, sum(grad_row_widths))), _full((4, 128, 128))] + [hbm] * ns,
        out_shape=[jax.ShapeDtypeStruct((m, PROJ_W), MXU_DTYPE), jax.ShapeDtypeStruct((4, D_XBC), F32),
                   jax.ShapeDtypeStruct((1, sum(grad_row_widths)), F32), jax.ShapeDtypeStruct((4, 128, 128), F32)]
        + _ChipExchange.out_shapes(chip_sums, [False] * ns),
        scratch_shapes=[pltpu.VMEM((nb, D_SSM, D_STATE), F32), pltpu.VMEM((nb, HALO, D_XBC), F32),
                        pltpu.VMEM((nb, HALO, D_POOL), F32)] + _ssd_constant_scratch(constants)
        + _ChipExchange.scratch(ns),
        compiler_params=_params("arbitrary"),
    )(proj, proj, dy, ypre, conv_pre, states, cw, rows, pool_w, *chip_sums)
    dproj, g_cw, g_rows, g_pw = outs[:4]
    offs = np.cumsum([0] + grad_row_widths)
    g_cb, g_dtb, g_alog, g_d, g_nw, g_ps = (g_rows[:, a:b] for a, b in zip(offs[:-1], offs[1:]))
    return (dproj, g_cw, g_cb, g_dtb, g_alog, g_d, g_nw, g_pw, g_ps), outs[4:]


MESH_IDS = pl.DeviceIdType.MESH
_HBM = pl.BlockSpec(memory_space=pltpu.HBM)


def _coords():
    return lax.axis_index("x"), lax.axis_index("y"), lax.axis_index("c")


def _other_chips(x, y):
    return [(1 - x, y), (x, 1 - y), (1 - x, 1 - y)]


class _Gather:
    def __init__(self, ins, outs, by_cols, send_sems, recv_sems, local_sems):
        self.ins, self.outs, self.by_cols, self.n = ins, outs, by_cols, len(ins)
        self.send_sems, self.recv_sems, self.local_sems = send_sems, recv_sems, local_sems
        self.x, self.y, self.c = _coords()
        self.me, self.sibling = (self.x, self.y, self.c), (self.x, self.y, 1 - self.c)
        self.chips = _other_chips(self.x, self.y)

    @staticmethod
    def scratch(n):
        return [pltpu.SemaphoreType.DMA((7 * n,)), pltpu.SemaphoreType.DMA((7 * n,)), pltpu.SemaphoreType.DMA((n,))]

    @staticmethod
    def out_shapes(shards, by_cols):
        return [jax.ShapeDtypeStruct((s.shape[0], N_DEV * s.shape[1]) if cols else (N_DEV,) + s.shape, s.dtype)
                for s, cols in zip(shards, by_cols)]

    def _block(self, t, device):
        idx = 4 * device[0] + 2 * device[1] + device[2]
        if not self.by_cols[t]:
            return self.outs[t].at[idx]
        w = self.ins[t].shape[1]
        return self.outs[t].at[:, pl.ds(pl.multiple_of(idx * w, w), w)]

    def _copy(self, t, k, block, to, own=False):
        dst = self._block(t, block)
        return pltpu.make_async_remote_copy(
            src_ref=self.ins[t] if own else dst, dst_ref=dst, send_sem=self.send_sems.at[t * 7 + k],
            recv_sem=self.recv_sems.at[t * 7 + k], device_id=to, device_id_type=MESH_IDS)

    def _mine(self):
        return [pltpu.make_async_copy(self.ins[t], self._block(t, self.me), self.local_sems.at[t])
                for t in range(self.n)]

    def _first(self):
        cps = []
        for t in range(self.n):
            cps.append(self._copy(t, 0, self.me, self.sibling, own=True))
            cps += [self._copy(t, 1 + j, self.me, (*chip, self.c), own=True) for j, chip in enumerate(self.chips)]
        return cps

    def _passed(self):
        return [self._copy(t, 4 + j, (*chip, self.c), self.sibling)
                for j, chip in enumerate(self.chips) for t in range(self.n)]

    def start(self):
        for cp in self._mine() + self._first():
            cp.start()

    def forward(self):
        for j, chip in enumerate(self.chips):
            for t in range(self.n):
                self._copy(t, 1 + j, (*chip, self.c), self.me).wait_recv()
                self._copy(t, 4 + j, (*chip, self.c), self.sibling).start()

    def finish(self):
        for t in range(self.n):
            self._copy(t, 0, self.sibling, self.me).wait_recv()
            for j, chip in enumerate(self.chips):
                self._copy(t, 4 + j, (*chip, 1 - self.c), self.me).wait_recv()
        for cp in self._first() + self._passed():
            cp.wait_send()
        for cp in self._mine():
            cp.wait()


def _weight_gather(shards):
    n = len(shards)

    def body(*refs):
        g = _Gather(refs[:n], refs[n:2 * n], [False] * n, *refs[2 * n:])
        g.start()
        g.forward()
        g.finish()

    return pl.pallas_call(
        body, name="weight_gather",
        in_specs=[_HBM] * n, out_specs=[_HBM] * n,
        out_shape=_Gather.out_shapes(shards, [False] * n),
        scratch_shapes=_Gather.scratch(n),
    )(*shards)


def _grad_exchange_d2d(gs, name, swapped=()):
    n, ns = len(gs), len(swapped)

    def body(*refs):
        ins, whole_ins = refs[:n], refs[n:n + ns]
        got, whole_got = refs[n + ns:2 * n + ns], refs[2 * n + ns:2 * (n + ns)]
        send_sems, recv_sems = refs[2 * (n + ns):]
        x, y, c = _coords()
        pairs = [(ins[t].at[k, 1 - c], got[t].at[k]) for t in range(n) for k in range(4)] + list(zip(whole_ins, whole_got))
        remote = [pltpu.make_async_remote_copy(
            src_ref=src, dst_ref=dst, send_sem=send_sems.at[i], recv_sem=recv_sems.at[i], device_id=(x, y, 1 - c),
            device_id_type=MESH_IDS) for i, (src, dst) in enumerate(pairs)]
        for cp in remote:
            cp.start()
        for cp in remote:
            cp.wait_recv()
        for cp in remote:
            cp.wait_send()

    outs = pl.pallas_call(
        body, name=name,
        in_specs=[_HBM] * (n + ns), out_specs=[_HBM] * (n + ns),
        out_shape=[jax.ShapeDtypeStruct((4,) + g.shape[2:], g.dtype) for g in gs]
        + [jax.ShapeDtypeStruct(a.shape, a.dtype) for a in swapped],
        scratch_shapes=[pltpu.SemaphoreType.DMA((4 * n + ns,)), pltpu.SemaphoreType.DMA((4 * n + ns,))],
    )(*gs, *swapped)
    return outs[:n], outs[n:]


def _small_allreduce(pack):
    rows = pack.shape[0]

    def body(p_ref, o_ref, sib_ref, parts_ref, send_sems, recv_sems):
        x, y, c = _coords()
        my_chip = 2 * x + y
        swap = pltpu.make_async_remote_copy(src_ref=p_ref, dst_ref=sib_ref, send_sem=send_sems.at[0],
                                            recv_sem=recv_sems.at[0], device_id=(x, y, 1 - c), device_id_type=MESH_IDS)
        swap.start()
        swap.wait_recv()
        parts_ref[my_chip] = p_ref[...] + sib_ref[...]
        remote = [pltpu.make_async_remote_copy(
            src_ref=parts_ref.at[my_chip], dst_ref=parts_ref.at[my_chip], send_sem=send_sems.at[1 + j],
            recv_sem=recv_sems.at[1 + j], device_id=(cx, cy, c), device_id_type=MESH_IDS)
            for j, (cx, cy) in enumerate(_other_chips(x, y))]
        for cp in remote:
            cp.start()
        for j, (cx, cy) in enumerate(_other_chips(x, y)):
            slot = parts_ref.at[2 * cx + cy]
            pltpu.make_async_remote_copy(src_ref=slot, dst_ref=slot, send_sem=send_sems.at[1 + j],
                                         recv_sem=recv_sems.at[1 + j], device_id=(cx, cy, c),
                                         device_id_type=MESH_IDS).wait_recv()
        o_ref[...] = ((parts_ref[0] + parts_ref[1]) + parts_ref[2]) + parts_ref[3]
        swap.wait_send()
        for cp in remote:
            cp.wait_send()

    vmem = pl.BlockSpec(memory_space=pltpu.VMEM)
    return pl.pallas_call(
        body, name="small_allreduce", in_specs=[vmem], out_specs=vmem,
        out_shape=jax.ShapeDtypeStruct((rows, 128), F32),
        scratch_shapes=[pltpu.VMEM((rows, 128), F32), pltpu.VMEM((4, rows, 128), F32),
                        pltpu.SemaphoreType.DMA((4,)), pltpu.SemaphoreType.DMA((4,))],
    )(pack)


class _ChipExchange:
    def __init__(self, ins, outs, whole, send_sems, recv_sems, local_sems):
        self.ins, self.outs, self.whole, self.n = ins, outs, whole, len(ins)
        self.send_sems, self.recv_sems, self.local_sems = send_sems, recv_sems, local_sems
        self.x, self.y, self.c = _coords()
        self.my_chip = 2 * self.x + self.y
        self.chips = _other_chips(self.x, self.y)

    @staticmethod
    def scratch(n):
        return [pltpu.SemaphoreType.DMA((3 * n,)), pltpu.SemaphoreType.DMA((3 * n,)), pltpu.SemaphoreType.DMA((n,))]

    def _src(self, t, k):
        return self.ins[t] if self.whole[t] else self.ins[t].at[k]

    def _local(self):
        return [pltpu.make_async_copy(self._src(t, self.my_chip), self.outs[t].at[self.my_chip], self.local_sems.at[t])
                for t in range(self.n)]

    def _remote(self):
        return [pltpu.make_async_remote_copy(
            src_ref=self._src(t, 2 * cx + cy), dst_ref=self.outs[t].at[self.my_chip],
            send_sem=self.send_sems.at[t * 3 + j], recv_sem=self.recv_sems.at[t * 3 + j],
            device_id=(cx, cy, self.c), device_id_type=MESH_IDS)
            for t in range(self.n) for j, (cx, cy) in enumerate(self.chips)]

    def start(self):
        for cp in self._remote() + self._local():
            cp.start()

    def finish(self):
        for t in range(self.n):
            for j, (cx, cy) in enumerate(self.chips):
                slot = self.outs[t].at[2 * cx + cy]
                pltpu.make_async_remote_copy(
                    src_ref=slot, dst_ref=slot, send_sem=self.send_sems.at[t * 3 + j],
                    recv_sem=self.recv_sems.at[t * 3 + j], device_id=(cx, cy, self.c),
                    device_id_type=MESH_IDS).wait_recv()
        for cp in self._remote():
            cp.wait_send()
        for cp in self._local():
            cp.wait()

    @staticmethod
    def out_shapes(arrs, whole):
        return [jax.ShapeDtypeStruct(((4,) + a.shape) if w else a.shape, a.dtype) for a, w in zip(arrs, whole)]


_SEMAPHORES = pl.BlockSpec(memory_space=pltpu.SEMAPHORE)
_SIDE_EFFECT = pltpu.SideEffectType.DATAFLOW_SIDE_EFFECTING


def _split_exchange_copies(srcs, lands, whole, send_sems, recv_sems, waiting):
    x, y, c = _coords()
    copies = []
    for t in range(len(srcs)):
        for j, (cx, cy) in enumerate(_other_chips(x, y)):
            src = srcs[t] if whole[t] else srcs[t].at[2 * cx + cy]
            dst = lands[t].at[2 * cx + cy] if waiting else lands[t].at[2 * x + y]
            copies.append(pltpu.make_async_remote_copy(
                src_ref=src, dst_ref=dst, send_sem=send_sems.at[3 * t + j], recv_sem=recv_sems.at[3 * t + j],
                device_id=(cx, cy, c), device_id_type=MESH_IDS))
    return copies


def _chip_exchange_start(arrays, whole, name):
    n = len(arrays)

    def body(*refs):
        srcs, lands = refs[:n], refs[n:2 * n]
        send_sems, recv_sems = refs[2 * n:2 * n + 2]
        for cp in _split_exchange_copies(srcs, lands, whole, send_sems, recv_sems, waiting=False):
            cp.start()
        token = refs[-1]
        token[...] = jnp.zeros_like(token)

    land_shapes = [((4,) + a.shape) if w else a.shape for a, w in zip(arrays, whole)]
    hbm = lambda shape, a: pltpu.HBM(shape, a.dtype)
    outs = pl.pallas_call(
        body, name=name,
        out_shape=(pltpu.SemaphoreType.DMA((3 * n,)), pltpu.SemaphoreType.DMA((3 * n,)),
                   *[hbm(a.shape, a) for a in arrays], *[hbm(s, a) for s, a in zip(land_shapes, arrays)],
                   jax.ShapeDtypeStruct((8, 128), F32)),
        in_specs=(_HBM,) * (2 * n),
        out_specs=(_SEMAPHORES, _SEMAPHORES) + (_HBM,) * (2 * n) + (pl.BlockSpec(memory_space=pltpu.VMEM),),
        input_output_aliases={i: 2 + i for i in range(2 * n)},
        compiler_params=pltpu.CompilerParams(has_side_effects=_SIDE_EFFECT),
    )(*[pltpu.with_memory_space_constraint(a, pltpu.HBM) for a in arrays],
      *[pltpu.with_memory_space_constraint(lax.empty(s, a.dtype), pltpu.HBM) for s, a in zip(land_shapes, arrays)])
    return outs[0], outs[1], outs[2:2 + n], outs[2 + n:2 + 2 * n], outs[-1]


def _chip_exchange_wait(send_sems, recv_sems, srcs, lands, whole, after, name):
    n = len(srcs)

    def body(*refs):
        src_refs, land_refs = refs[:n], refs[n:2 * n]
        for cp in _split_exchange_copies(src_refs, land_refs, whole, refs[2 * n], refs[2 * n + 1], waiting=True):
            cp.wait_send()
            cp.wait_recv()

    outs = pl.pallas_call(
        body, name=name, out_shape=tuple(pltpu.HBM(a.shape, a.dtype) for a in (*srcs, *lands)),
        in_specs=(_HBM,) * (2 * n) + (_SEMAPHORES, _SEMAPHORES, pl.BlockSpec(memory_space=pl.ANY)),
        out_specs=(_HBM,) * (2 * n), input_output_aliases={i: i for i in range(2 * n)},
        compiler_params=pltpu.CompilerParams(has_side_effects=_SIDE_EFFECT),
    )(*srcs, *lands, send_sems, recv_sems, after)
    return outs[:n], outs[n:]


def _sum_two(a, b, name):
    def body(a_ref, b_ref, o_ref):
        o_ref[...] = a_ref[...] + b_ref[...]

    return pl.pallas_call(body, name=name, out_shape=jax.ShapeDtypeStruct(a.shape, a.dtype))(a, b)


def _sum_chips(landed, own, chip, name):
    def body(chip_ref, l_ref, own_ref, o_ref):
        part = lambda k: jnp.where(chip_ref[0] == k, own_ref[...], l_ref[k])
        o_ref[...] = ((part(0) + part(1)) + part(2)) + part(3)

    grid_spec = pltpu.PrefetchScalarGridSpec(
        num_scalar_prefetch=1, grid=(1,),
        in_specs=[pl.BlockSpec(landed.shape, lambda i, c: (0, 0, 0)), pl.BlockSpec(own.shape, lambda i, c: (0, 0))],
        out_specs=pl.BlockSpec(own.shape, lambda i, c: (0, 0)))
    return pl.pallas_call(body, grid_spec=grid_spec, name=name,
                          out_shape=jax.ShapeDtypeStruct(own.shape, own.dtype))(chip, landed, own)


def _row_tile(rows, cols, n_arrays):
    budget = 24 * 1024 * 1024
    padded = -(-cols // 128) * 128
    step = 16 if rows % 16 == 0 else 8
    tr = max(step, budget // (n_arrays * 2 * 4 * padded) // step * step)
    while rows % tr:
        tr -= step
    return tr


def _chip_sum(g, got, core, name):
    rows, cols = g.shape[2:]
    tr = _row_tile(rows, cols, 3)

    def body(c_ref, a_ref, b_ref, o_ref):
        o_ref[...] = (a_ref[...] + b_ref[...].astype(F32)).astype(o_ref.dtype)

    grid_spec = pltpu.PrefetchScalarGridSpec(
        num_scalar_prefetch=1, grid=(4, rows // tr),
        in_specs=[pl.BlockSpec((None, None, tr, cols), lambda k, i, c: (k, c[0], i, 0)),
                  pl.BlockSpec((None, tr, cols), lambda k, i, c: (k, i, 0))],
        out_specs=pl.BlockSpec((None, tr, cols), lambda k, i, c: (k, i, 0)))
    return pl.pallas_call(body, grid_spec=grid_spec, name=name,
                          out_shape=jax.ShapeDtypeStruct((4, rows, cols), MXU_DTYPE),
                          compiler_params=_params("arbitrary", "arbitrary"))(core, g, got)


def _adamw_math(w, g, m, v):
    m2 = ADAM_B1 * m + (1.0 - ADAM_B1) * g
    v2 = ADAM_B2 * v + (1.0 - ADAM_B2) * (g * g)
    m_hat = m2 / (1.0 - ADAM_B1 ** ADAM_STEP)
    v_hat = v2 / (1.0 - ADAM_B2 ** ADAM_STEP)
    delta = -ADAM_LR * (m_hat / (jnp.sqrt(v_hat) + ADAM_EPS) + ADAM_WD * w)
    return delta, m2, v2


def _adamw(parts, w, m, v, name, own=None, chip=None):
    rows, cols = w.shape
    tr = _row_tile(rows, cols, 11 if own is None else 15)

    def body(*refs):
        if own is None:
            p_ref, w_ref, m_ref, v_ref, g_ref, d_ref, m2_ref, v2_ref = refs
            part = lambda k: p_ref[k].astype(F32)
        else:
            chip_ref, p_ref, own_ref, w_ref, m_ref, v_ref, g_ref, d_ref, m2_ref, v2_ref = refs
            part = lambda k: jnp.where(chip_ref[0] == k, own_ref[k], p_ref[k]).astype(F32)
        g = ((part(0) + part(1)) + part(2)) + part(3)
        d, m2, v2 = _adamw_math(w_ref[...], g, m_ref[...], v_ref[...])
        g_ref[...] = g
        d_ref[...] = d
        m2_ref[...] = m2
        v2_ref[...] = v2

    blk = pl.BlockSpec((tr, cols), lambda i, *_: (i, 0))
    pblk = pl.BlockSpec((4, tr, cols), lambda i, *_: (0, i, 0))
    out = jax.ShapeDtypeStruct((rows, cols), F32)
    if own is None:
        return pl.pallas_call(body, grid=(rows // tr,), name=name, in_specs=[pblk, blk, blk, blk],
                              out_specs=[blk] * 4, out_shape=[out] * 4,
                              compiler_params=_params("arbitrary"))(parts, w, m, v)
    grid_spec = pltpu.PrefetchScalarGridSpec(num_scalar_prefetch=1, grid=(rows // tr,),
                                             in_specs=[pblk, pblk, blk, blk, blk], out_specs=[blk] * 4)
    return pl.pallas_call(body, grid_spec=grid_spec, name=name, out_shape=[out] * 4,
                          compiler_params=_params("arbitrary"))(chip, parts, own, w, m, v)


def _adamw_transposed(parts_t, w, m, v, name):
    rows, cols = w.shape
    tc = 256

    def body(p_ref, w_ref, m_ref, v_ref, g_ref, d_ref, m2_ref, v2_ref):
        part = lambda k: p_ref[k].astype(F32)
        g = (((part(0) + part(1)) + part(2)) + part(3)).T
        d, m2, v2 = _adamw_math(w_ref[...], g, m_ref[...], v_ref[...])
        g_ref[...] = g
        d_ref[...] = d
        m2_ref[...] = m2
        v2_ref[...] = v2

    blk = pl.BlockSpec((rows, tc), lambda i: (0, i))
    out = jax.ShapeDtypeStruct((rows, cols), F32)
    return pl.pallas_call(body, grid=(cols // tc,), name=name,
                          in_specs=[pl.BlockSpec((4, tc, rows), lambda i: (0, i, 0)), blk, blk, blk],
                          out_specs=[blk] * 4, out_shape=[out] * 4,
                          compiler_params=_params("arbitrary"))(parts_t, w, m, v)


def _adamw_small(gs, ws, ms, vs):
    n = len(ws)

    def body(*refs):
        g_refs, w_refs, m_refs, v_refs = (refs[k * n:(k + 1) * n] for k in range(4))
        d_refs, m2_refs, v2_refs = (refs[(4 + k) * n:(5 + k) * n] for k in range(3))
        for t in range(n):
            d, m2, v2 = _adamw_math(w_refs[t][...], g_refs[t][...], m_refs[t][...], v_refs[t][...])
            d_refs[t][...] = d
            m2_refs[t][...] = m2
            v2_refs[t][...] = v2

    outs = pl.pallas_call(body, name="adamw_small",
                          out_shape=[jax.ShapeDtypeStruct(w.shape, F32) for w in ws] * 3)(*gs, *ws, *ms, *vs)
    return outs[:n], outs[n:2 * n], outs[2 * n:]


_PACK_TILE = 8 * 128


def _pack(arrays):
    rows = []
    for a in arrays:
        flat = a.astype(F32).reshape(-1)
        rows.append(jnp.pad(flat, (0, -flat.shape[0] % _PACK_TILE)).reshape(-1, 128))
    return jnp.concatenate(rows, axis=0)


def _unpack(pack, shapes):
    out, r = [], 0
    for s in shapes:
        n = int(np.prod(s))
        out.append(pack[r:r + -(-n // 128)].reshape(-1)[:n].reshape(s))
        r += -(-n // _PACK_TILE) * 8
    return out


def _pad128(v):
    v = v.reshape(1, -1).astype(F32)
    return jnp.pad(v, ((0, 0), (0, 128 - v.shape[1])))


_WEIGHTS = ["meta", "norm_mix_w", "w_in", "pool_w", "pool_scale", "conv_w", "conv_b", "dt_bias", "a_log", "d_skip",
            "ssm_norm_w", "w_out", "norm_ffn_w", "w_ff1", "w_ff2", "norm_f_w"]
_BIG = ["w_in", "w_out", "w_ff1", "w_ff2"]
_SMALL = [n for n in _WEIGHTS if n not in _BIG]


def kernel(x, meta, norm_mix_w, w_in, pool_w, pool_scale, conv_w, conv_b, dt_bias, a_log, d_skip, ssm_norm_w, w_out, norm_ffn_w, w_ff1, w_ff2, norm_f_w, loss_target, m_meta, m_norm_mix_w, m_w_in, m_pool_w, m_pool_scale, m_conv_w, m_conv_b, m_dt_bias, m_a_log, m_d_skip, m_ssm_norm_w, m_w_out, m_norm_ffn_w, m_w_ff1, m_w_ff2, m_norm_f_w, v_meta, v_norm_mix_w, v_w_in, v_pool_w, v_pool_scale, v_conv_w, v_conv_b, v_dt_bias, v_a_log, v_d_skip, v_ssm_norm_w, v_w_out, v_norm_ffn_w, v_w_ff1, v_w_ff2, v_norm_f_w):
    wts = dict(meta=meta, norm_mix_w=norm_mix_w, w_in=w_in, pool_w=pool_w, pool_scale=pool_scale, conv_w=conv_w,
               conv_b=conv_b, dt_bias=dt_bias, a_log=a_log, d_skip=d_skip, ssm_norm_w=ssm_norm_w, w_out=w_out,
               norm_ffn_w=norm_ffn_w, w_ff1=w_ff1, w_ff2=w_ff2, norm_f_w=norm_f_w)
    mom1 = dict(zip(_WEIGHTS, (m_meta, m_norm_mix_w, m_w_in, m_pool_w, m_pool_scale, m_conv_w, m_conv_b, m_dt_bias,
                               m_a_log, m_d_skip, m_ssm_norm_w, m_w_out, m_norm_ffn_w, m_w_ff1, m_w_ff2, m_norm_f_w)))
    mom2 = dict(zip(_WEIGHTS, (v_meta, v_norm_mix_w, v_w_in, v_pool_w, v_pool_scale, v_conv_w, v_conv_b, v_dt_bias,
                               v_a_log, v_d_skip, v_ssm_norm_w, v_w_out, v_norm_ffn_w, v_w_ff1, v_w_ff2, v_norm_f_w)))
    xi, yi, ci = _coords()
    dev = 4 * xi + 2 * yi + ci
    win_cols = w_in.shape[-1]
    cw_cols = conv_w.shape[-1]

    nb, seq, _ = x.shape
    core = jnp.reshape(ci, (1,)).astype(jnp.int32)
    owners = lambda a: a.reshape((4, 2) + a.shape[1:])

    lead_pack = jnp.zeros((N_META, 512), F32)
    lead_pack = lead_pack.at[:, :128].set(meta).at[:4, 128:128 + cw_cols].set(conv_w[0])
    g_win, g_lead = _weight_gather([_mx(w_in[0]), lead_pack])
    win_full = _assemble_bands(g_win, PROJ_W)
    meta_full = jnp.transpose(g_lead[:, :, :128], (1, 0, 2)).reshape(N_META, D_MODEL)
    cw_full = jnp.transpose(g_lead[:, :4, 128:128 + cw_cols], (1, 0, 2)).reshape(4, D_XBC)

    lead = jnp.concatenate([jnp.zeros((PAD_ROWS, D_MODEL), F32), meta_full] * nb, axis=0)
    x_rows = x.reshape(nb * seq, D_MODEL)
    tgt_rows = loss_target.reshape(nb * seq, D_MODEL)
    dt_bias_p, a_log_p = _pad128(dt_bias), _pad128(a_log)
    d_x = jnp.repeat(d_skip.reshape(1, N_HEADS).astype(F32), HEAD_DIM, axis=1)
    norm_f_row = norm_f_w.reshape(1, D_MODEL)

    hn1, proj = _in_proj(x_rows, lead, norm_mix_w, win_full)
    late_cols = [False, True, False]
    y, ypre, conv_pre, states, (g_wout, wff1_full, g_wff2) = _mixer_fwd(
        proj, cw_full, conv_b, dt_bias_p, a_log_p, d_x, ssm_norm_w, pool_w[0], pool_scale, nb,
        [_mx(w_out[0]), _mx(w_ff1[0]), _mx(w_ff2[0])], late_cols)
    wout_full = g_wout.reshape(D_MIX, D_MODEL)
    wff2_full = g_wff2.reshape(D_FF, D_MODEL)
    loss, gr_nf, gr_nffn, ff, da, hn2, dh1, dh2, dy = _ffn_fwd_bwd(
        x_rows, lead, y, tgt_rows, wout_full, norm_ffn_w, wff1_full, wff2_full, norm_f_row)
    gr_wff2 = _tn_matmul(ff, dh2, "grad_w_ff2", tka=1024, max_slab=2816)
    gr_wff1_t = _tn_matmul(da, hn2, "grad_w_ff1", tka=1024, max_slab=2816)
    gr_wout = _tn_matmul(y, dh1, "grad_w_out", tka=1024, max_slab=2816)

    by_owner = lambda k: [owners(gr_wout[k].reshape(N_DEV, D_MIX // N_DEV, D_MODEL)),
                          owners(gr_wff1_t[k].reshape(N_DEV, D_FF // N_DEV, D_MODEL)),
                          owners(gr_wff2[k].reshape(N_DEV, D_FF // N_DEV, D_MODEL))]
    late_parts = by_owner(0)
    late_got, _ = _grad_exchange_d2d(by_owner(1), "grad_exchange_d2d_late")
    late_sums = [_chip_sum(late_parts[t], late_got[t], core, "chip_sum_late_%d" % t) for t in range(3)]
    (dproj, gr_cw, gr_cb, gr_dtb, gr_alog, gr_d, gr_nw, gr_pw, gr_ps), late_exchanged = _mixer_bwd(
        proj, dy, ypre, conv_pre, states, cw_full, dt_bias_p, a_log_p, d_x, ssm_norm_w, pool_w[0], pool_scale, nb,
        late_sums)

    early = dict(pool_w=gr_pw, pool_scale=gr_ps, conv_w=gr_cw, conv_b=gr_cb, dt_bias=gr_dtb[:, :N_HEADS],
                 a_log=gr_alog[:, :N_HEADS], d_skip=gr_d[:, :N_HEADS], ssm_norm_w=gr_nw, norm_ffn_w=gr_nffn,
                 norm_f_w=gr_nf, loss=loss[0:1, 0:1])
    early_pack = _pack(list(early.values()))
    win_parts = [owners(_tn_matmul_banded(hn1, dproj, "grad_w_in", win_cols, tka=512))]
    win_got, (early_got,) = _grad_exchange_d2d(win_parts, "grad_exchange_d2d_w_in", swapped=[early_pack])
    win_sum = _chip_sum(win_parts[0], win_got[0], core, "chip_sum_w_in")
    early_chip = _sum_two(early_pack, early_got, "chip_sum_small")
    whole = [False, True]
    send_sems, recv_sems, sent, landing, started = _chip_exchange_start([win_sum, early_chip], whole, "w_in_exchange_start")
    gx_rows, gr_nmix, gr_meta = _in_proj_bwd(dproj, x_rows, lead, dh1, norm_mix_w, win_full, started)
    (win_sum, early_chip), (win_landed, early_landed) = _chip_exchange_wait(
        send_sems, recv_sems, sent, landing, whole, gr_nmix, "w_in_exchange_wait")
    parts = dict(w_in=win_landed, w_out=late_exchanged[0], w_ff1=late_exchanged[1], w_ff2=late_exchanged[2])
    my_chip = jnp.reshape(2 * xi + yi, (1,)).astype(jnp.int32)
    early_sum = _sum_chips(early_landed, early_chip, my_chip, "small_sum")

    tail = dict(meta=gr_meta, norm_mix_w=gr_nmix)
    tail_sum = _small_allreduce(_pack(list(tail.values())))
    gs = dict(zip(early, _unpack(early_sum, [a.shape for a in early.values()])))
    gs.update(zip(tail, _unpack(tail_sum, [a.shape for a in tail.values()])))
    gs["meta"] = lax.dynamic_slice_in_dim(gs["meta"], dev * 128, 128, axis=1)
    gs["conv_w"] = lax.dynamic_slice_in_dim(gs["conv_w"], dev * cw_cols, cw_cols, axis=1)

    res = {}
    for n in _BIG:
        shp = wts[n].shape
        args = (parts[n], wts[n][0], mom1[n][0], mom2[n][0], "adamw_" + n)
        if n == "w_ff1":
            outs = _adamw_transposed(*args)
        elif n == "w_in":
            outs = _adamw(*args, own=win_sum, chip=my_chip)
        else:
            outs = _adamw(*args)
        res[n] = [o.reshape(shp) for o in outs]
    as2d = lambda a: a.reshape(-1, a.shape[-1])
    small_g = [as2d(gs[n].reshape(wts[n].shape)) for n in _SMALL]
    small_out = _adamw_small(small_g, *[[as2d(d[n]) for n in _SMALL] for d in (wts, mom1, mom2)])
    for k, n in enumerate(_SMALL):
        res[n] = [o[k].reshape(wts[n].shape) for o in (small_g,) + tuple(small_out)]

    grad_x = gx_rows.reshape(nb, seq, D_MODEL)
    return (gs["loss"][0, 0], grad_x, *[res[n][0] for n in _WEIGHTS], *[res[n][1] for n in _WEIGHTS],
            *[res[n][2] for n in _WEIGHTS], *[res[n][3] for n in _WEIGHTS])
```

```python
import numpy as np
import jax
import jax.numpy as jnp
from jax import lax
from jax.experimental import pallas as pl
from jax.experimental.pallas import tpu as pltpu

F32 = jnp.float32
MXU_DTYPE = jnp.bfloat16

D_MODEL = 1024
D_POOL = 512
D_SSM = 1536
D_XBC = 2560
N_HEADS = 24
HEAD_DIM = 64
N_GROUPS = 4
GROUP_CH = D_SSM // N_GROUPS
D_STATE = 128
CHUNK = 128
N_META = 16
LEAD = CHUNK
PAD_ROWS = LEAD - N_META
ROW_TILE = 2 * CHUNK
D_MIX = D_POOL + D_SSM
D_FF = 4096
PROJ_W = 4736
OFF_Z = D_POOL
OFF_X = D_POOL + D_SSM
OFF_DT = OFF_X + D_XBC
D_IN_PROJ = OFF_DT + N_HEADS
POOL_WINDOWS = (2, 4, 8, 16)
HALO = 16
EPS = 1e-5
N_DEV = 8

ADAM_LR, ADAM_B1, ADAM_B2, ADAM_EPS, ADAM_WD, ADAM_STEP = 0.001, 0.9, 0.999, 1e-08, 0.01, 10

VMEM_LIMIT = 60 * 1024 * 1024


def _mx(a):
    return a.astype(MXU_DTYPE)


def _dot(a, b):
    return jnp.dot(_mx(a), _mx(b), preferred_element_type=F32)


def _dot_nt(a, b):
    return lax.dot_general(_mx(a), _mx(b), (((1,), (1,)), ((), ())), preferred_element_type=F32)


def _dot_tn(a, b):
    return lax.dot_general(_mx(a), _mx(b), (((0,), (0,)), ((), ())), preferred_element_type=F32)


def _split3(x):
    hi = x.astype(MXU_DTYPE)
    r = x - hi.astype(F32)
    mid = r.astype(MXU_DTYPE)
    lo = (r - mid.astype(F32)).astype(MXU_DTYPE)
    return hi, mid, lo


def _exact_l(c, x):
    hi, mid, lo = _split3(x)
    f = lambda p: jnp.dot(c, p, preferred_element_type=F32)
    return f(hi) + f(mid) + f(lo)


def _exact_r(x, c):
    hi, mid, lo = x if isinstance(x, tuple) else _split3(x)
    f = lambda p: jnp.dot(p, c, preferred_element_type=F32)
    return f(hi) + f(mid) + f(lo)


def _contract(x, c):
    hi = x.astype(MXU_DTYPE)
    lo = (x - hi.astype(F32)).astype(MXU_DTYPE)
    return jnp.dot(hi, c, preferred_element_type=F32) + jnp.dot(lo, c, preferred_element_type=F32)


def _sigmoid(x):
    return jax.nn.sigmoid(x)


def _softplus(x):
    return jnp.maximum(x, 0.0) + jnp.log1p(jnp.exp(-jnp.abs(x)))


def _silu_and_grad(x):
    s = _sigmoid(x)
    y = x * s
    return y, s + y * (1.0 - s)


def _shift_up(ext, s):
    if s == 0:
        return ext[:CHUNK, :]
    return pltpu.roll(ext, ext.shape[0] - s, 0)[:CHUNK, :]


def _by_pool_group(lane, a2, a4, a8, a16):
    return jnp.where(lane < 128, a2, jnp.where(lane < 256, a4, jnp.where(lane < 384, a8, a16)))


def _pool_inv_count(chunk_idx):
    row = lax.broadcasted_iota(jnp.int32, (CHUNK, D_POOL), 0)
    lane = lax.broadcasted_iota(jnp.int32, (CHUNK, D_POOL), 1)
    pos1 = jnp.maximum(chunk_idx * CHUNK + row - (PAD_ROWS - 1), 1)
    w = _by_pool_group(lane, 2, 4, 8, 16)
    return 1.0 / jnp.minimum(pos1, w).astype(F32), lane


def _pool_window_sums(u_ext, lane):
    s2 = u_ext + pltpu.roll(u_ext, 1, 0)
    s4 = s2 + pltpu.roll(s2, 2, 0)
    s8 = s4 + pltpu.roll(s4, 4, 0)
    s16 = s8 + pltpu.roll(s8, 8, 0)
    return _by_pool_group(lane, s2[HALO:], s4[HALO:], s8[HALO:], s16[HALO:])


def _pool_window_sums_ahead(q_ext, lane):
    n = q_ext.shape[0]
    r2 = q_ext + pltpu.roll(q_ext, n - 1, 0)
    r4 = r2 + pltpu.roll(r2, n - 2, 0)
    r8 = r4 + pltpu.roll(r4, n - 4, 0)
    r16 = r8 + pltpu.roll(r8, n - 8, 0)
    return _by_pool_group(lane, r2[:CHUNK], r4[:CHUNK], r8[:CHUNK], r16[:CHUNK])


def _conv_pre(ext, xbc, cw, cb):
    s1 = pltpu.roll(ext, 1, 0)
    near = cw[3:4, :] * xbc + cw[2:3, :] * s1[HALO:, :]
    far = cw[1:2, :] * ext + cw[0:1, :] * s1
    return cb + near + pltpu.roll(far, 2, 0)[HALO:, :]


def _dt_and_cumsum(dtr, dt_bias, a_log, valid, tril):
    lane = lax.broadcasted_iota(jnp.int32, (CHUNK, 128), 1)
    head = lane < N_HEADS
    pre = dtr + dt_bias
    dt = jnp.where(valid & head, _softplus(pre), 0.0)
    a_row = jnp.where(head[0:1, :], -jnp.exp(a_log), 0.0)
    a_col = _exact_l(tril, dt * a_row)
    return dt, a_row, a_col, pre, head


def _decay(a_col, a_row_t, h, causal):
    seg = a_col[:, h:h + 1] - a_row_t[h:h + 1, :]
    return jnp.where(causal, jnp.exp(jnp.minimum(seg, 0.0)), 0.0)


def _ssd_chunk_fwd(xs, bm, cm, dt, a_col, s_prev, d_x, e_mat, et_f32):
    lane = lax.broadcasted_iota(jnp.int32, (CHUNK, 128), 1)
    rowi = lax.broadcasted_iota(jnp.int32, (CHUNK, CHUNK), 0)
    coli = lax.broadcasted_iota(jnp.int32, (CHUNK, CHUNK), 1)
    causal = rowi >= coli
    a_row_t = a_col.T
    ax = _exact_r(a_col, e_mat)
    dtx = _exact_r(dt, e_mat)
    xdt = xs * dtx
    ax_last = ax[CHUNK - 1:CHUNK, :]
    e_a = jnp.exp(ax)
    w_end = xdt * jnp.exp(ax_last - ax)
    cd_col = jnp.exp(jnp.sum(et_f32 * a_col[CHUNK - 1:CHUNK, :], axis=1, keepdims=True))
    ys, s_new = [], []
    for g in range(N_GROUPS):
        gs = slice(g * GROUP_CH, (g + 1) * GROUP_CH)
        bg = bm[:, g * D_STATE:(g + 1) * D_STATE]
        cg = cm[:, g * D_STATE:(g + 1) * D_STATE]
        sg = s_prev[gs, :]
        cb = _dot_nt(cg, bg)
        y_off = _dot_nt(cg, sg) * e_a[:, gs]
        s_new.append(sg * cd_col[gs, :] + _dot_tn(w_end[:, gs], bg))
        for pr in range(3):
            c0 = g * GROUP_CH + pr * 128
            xdt_p = xdt[:, c0:c0 + 128]
            h0 = g * 6 + pr * 2
            y0 = _dot(cb * _decay(a_col, a_row_t, h0, causal), xdt_p)
            y1 = _dot(cb * _decay(a_col, a_row_t, h0 + 1, causal), xdt_p)
            ys.append(jnp.where(lane < HEAD_DIM, y0, y1) + y_off[:, pr * 128:(pr + 1) * 128])
    y = jnp.concatenate(ys, axis=1) + d_x * xs
    return y, jnp.concatenate(s_new, axis=0)


def _ssd_chunk_bwd(xs, bm, cm, dt, a_row, a_col, s_prev, ds_new, dy, d_x, e_mat, et_mat, et_f32, triu):
    lane = lax.broadcasted_iota(jnp.int32, (CHUNK, 128), 1)
    sub = lax.broadcasted_iota(jnp.int32, (CHUNK, 128), 0)
    rowi = lax.broadcasted_iota(jnp.int32, (CHUNK, CHUNK), 0)
    coli = lax.broadcasted_iota(jnp.int32, (CHUNK, CHUNK), 1)
    causal = rowi >= coli
    a_row_t = a_col.T
    a_last = a_col[CHUNK - 1:CHUNK, :]
    a_split, dt_split = _split3(a_col), _split3(dt)
    sub8 = lax.broadcasted_iota(jnp.int32, (8, GROUP_CH), 0)

    dxs, dbs, dcs, dsp = [], [], [], []
    zcol = jnp.zeros((CHUNK, 128), F32)
    zrows = []
    da_col = jnp.zeros((CHUNK, 128), F32)
    ddt = jnp.zeros((CHUNK, 128), F32)
    head_sums = jnp.zeros((8, 128), F32)
    q_row = jnp.zeros((1, 128), F32)
    for g in range(N_GROUPS):
        gs = slice(g * GROUP_CH, (g + 1) * GROUP_CH)
        e_g, et_g = e_mat[:, gs], et_mat[gs, :]
        xs_g, dy_g = xs[:, gs], dy[:, gs]
        ax = _exact_r(a_split, e_g)
        dtx = _exact_r(dt_split, e_g)
        xdt = xs_g * dtx
        dte = jnp.exp(ax[CHUNK - 1:CHUNK, :] - ax)
        w_end = xdt * dte
        cd_col = jnp.exp(jnp.sum(et_f32[gs, :] * a_last, axis=1, keepdims=True))
        dye = dy_g * jnp.exp(ax)
        bg = bm[:, g * D_STATE:(g + 1) * D_STATE]
        cg = cm[:, g * D_STATE:(g + 1) * D_STATE]
        sg = s_prev[gs, :]
        dsg = ds_new[gs, :]
        cb = _dot_nt(cg, bg)
        cs = _dot_nt(cg, sg)
        dcg = _dot(dye, sg)
        dsp.append(dsg * cd_col + _dot_tn(dye, cg))
        dwg = _dot_nt(bg, dsg)
        dbg = _dot(w_end, dsg)
        ww = dwg * w_end
        t1 = jnp.sum(dsg * sg, axis=1, keepdims=True) * cd_col
        dcb = jnp.zeros((CHUNK, CHUNK), F32)
        pairs = []
        for pr in range(3):
            ps = slice(pr * 128, (pr + 1) * 128)
            xdt_p, dy_p = xdt[:, ps], dy_g[:, ps]
            acc = None
            for half in range(2):
                h = g * 6 + pr * 2 + half
                ld = _decay(a_col, a_row_t, h, causal)
                gm = cb * ld
                dym = jnp.where((lane < HEAD_DIM) if half == 0 else (lane >= HEAD_DIM), dy_p, 0.0)
                dg = _dot_nt(dym, xdt_p)
                dseg = dg * gm
                dcb = dcb + dg * ld
                t = _dot_tn(gm, dym)
                acc = t if acc is None else acc + t
                zcol = jnp.where(lane == h, jnp.sum(dseg, axis=1, keepdims=True), zcol)
                zrows.append(jnp.sum(dseg, axis=0, keepdims=True))
            pairs.append(acc)
        dxdt = dwg * dte + jnp.concatenate(pairs, axis=1)
        dcs.append(dcg + _dot(dcb, bg))
        dbs.append(dbg + _dot_tn(dcb, cg))
        dxs.append(dxdt * dtx + d_x[:, gs] * dy_g)
        da_col = da_col + _contract(dye * cs - ww, et_g)
        ddt = ddt + _contract(dxdt * xs_g, et_g)
        col_sums = jnp.where(sub8 == 0, jnp.sum(dy_g * xs_g, axis=0, keepdims=True),
                             jnp.where(sub8 == 1, jnp.sum(ww, axis=0, keepdims=True), 0.0))
        head_sums = head_sums + _exact_r(col_sums, et_g)
        q_row = q_row + jnp.sum(et_f32[gs, :] * t1, axis=0, keepdims=True)

    dd = head_sums[0:1, :]
    q_row = q_row + head_sums[1:2, :]
    zrow = jnp.concatenate(zrows + [jnp.zeros((128 - N_HEADS, CHUNK), F32)], axis=0)
    da_col = da_col + zcol - zrow.T + jnp.where(sub == CHUNK - 1, q_row, 0.0)
    rc = _exact_l(triu, da_col)
    ddt = ddt + rc * a_row
    da = jnp.sum(rc * dt, axis=0, keepdims=True)
    return (jnp.concatenate(dxs, axis=1), jnp.concatenate(dbs, axis=1), jnp.concatenate(dcs, axis=1), ddt, da, dd,
            jnp.concatenate(dsp, axis=0))


_SSD_CONSTANT_SHAPES = dict(e=((128, D_SSM), MXU_DTYPE), et=((D_SSM, 128), MXU_DTYPE), et_f32=((D_SSM, 128), F32),
                            tril=((CHUNK, CHUNK), MXU_DTYPE), triu=((CHUNK, CHUNK), MXU_DTYPE))


def _ssd_constant_scratch(names):
    return [pltpu.VMEM(*_SSD_CONSTANT_SHAPES[n]) for n in names]


def _fill_ssd_constants(**refs):
    iota = lambda shape, d: lax.broadcasted_iota(jnp.int32, shape, d)
    shift = HEAD_DIM.bit_length() - 1
    marks = dict(
        e=lambda: iota((128, D_SSM), 0) == (iota((128, D_SSM), 1) >> shift),
        et=lambda: iota((D_SSM, 128), 1) == (iota((D_SSM, 128), 0) >> shift),
        et_f32=lambda: iota((D_SSM, 128), 1) == (iota((D_SSM, 128), 0) >> shift),
        tril=lambda: iota((CHUNK, CHUNK), 1) <= iota((CHUNK, CHUNK), 0),
        triu=lambda: iota((CHUNK, CHUNK), 1) >= iota((CHUNK, CHUNK), 0))
    for name, ref in refs.items():
        ref[...] = jnp.where(marks[name](), 1.0, 0.0).astype(ref.dtype)


def _row_views(ref, widths):
    views, off = [], 0
    for w in widths:
        views.append(ref.at[:, off:off + w])
        off += w
    return views


def _full(shape):
    nd = len(shape)
    return pl.BlockSpec(shape, lambda *_: (0,) * nd)


def _params(*sem):
    return pltpu.CompilerParams(dimension_semantics=sem, vmem_limit_bytes=VMEM_LIMIT)


def _token_tiles(width, n_tok_tiles):
    return pl.BlockSpec((ROW_TILE, width), lambda i: (jnp.minimum(i, n_tok_tiles - 1), 0))


def _in_proj(x, lead, w1, win):
    nt = x.shape[0] // ROW_TILE
    m = x.shape[0] + ROW_TILE
    tm = ROW_TILE

    def body(x_ref, lead_ref, w1_ref, win_hbm, hn_ref, proj_ref, win_v, sem):
        i = pl.program_id(0)

        @pl.when(i == 0)
        def _():
            cp = pltpu.make_async_copy(win_hbm, win_v, sem)
            cp.start()
            cp.wait()

        x = jnp.where(i == nt, lead_ref[...], x_ref[...])
        r = lax.rsqrt(jnp.mean(x * x, axis=-1, keepdims=True) + EPS)
        hn = _mx(x * r * w1_ref[...])
        hn_ref[...] = hn
        for j in range(0, PROJ_W, 512):
            w = min(512, PROJ_W - j)
            proj_ref[:, j:j + w] = jnp.dot(hn, win_v[:, j:j + w], preferred_element_type=F32)

    return pl.pallas_call(
        body, grid=(m // tm,), name="in_proj",
        in_specs=[_token_tiles(D_MODEL, nt), _full((ROW_TILE, D_MODEL)), _full((1, D_MODEL)),
                  pl.BlockSpec(memory_space=pl.ANY)],
        out_specs=[pl.BlockSpec((tm, D_MODEL), lambda i: (i, 0)), pl.BlockSpec((tm, PROJ_W), lambda i: (i, 0))],
        out_shape=[jax.ShapeDtypeStruct((m, D_MODEL), MXU_DTYPE), jax.ShapeDtypeStruct((m, PROJ_W), F32)],
        scratch_shapes=[pltpu.VMEM((D_MODEL, PROJ_W), MXU_DTYPE), pltpu.SemaphoreType.DMA],
        compiler_params=_params("arbitrary"),
    )(x, lead, w1, win)


def _ffn_fwd_bwd(x, lead, y, tgt, wout, w2n, wff1, wff2, wfn):
    nt = x.shape[0] // ROW_TILE
    m = x.shape[0] + ROW_TILE
    tm = ROW_TILE
    nj = D_FF // 1024

    def body(x_ref, lead_ref, y_ref, tgt_ref, w2n_ref, wfn_ref, wout_hbm, wff1_hbm, wff2_hbm,
             loss_ref, gwf_ref, gw2_ref, ff_ref, da_ref, hn2_ref, dh1_ref, dh2_ref, dy_ref,
             wout_v, wff1_v, wff2_v, a_s, sems):
        i = pl.program_id(0)
        hp = jnp.where(i == nt, lead_ref[...], x_ref[...])

        @pl.when(i == 0)
        def _():
            cps = [pltpu.make_async_copy(s, d, sems.at[k])
                   for k, (s, d) in enumerate(((wout_hbm, wout_v), (wff1_hbm, wff1_v), (wff2_hbm, wff2_v)))]
            for cp in cps:
                cp.start()
            for cp in cps:
                cp.wait()
            loss_ref[...] = jnp.zeros_like(loss_ref)
            gwf_ref[...] = jnp.zeros_like(gwf_ref)
            gw2_ref[...] = jnp.zeros_like(gw2_ref)

        h1 = hp + jnp.dot(y_ref[...], wout_v[...], preferred_element_type=F32)
        r2 = lax.rsqrt(jnp.mean(h1 * h1, axis=-1, keepdims=True) + EPS)
        n2 = h1 * r2
        w2n_row = w2n_ref[...]
        hn2 = _mx(n2 * w2n_row)
        hn2_ref[...] = hn2
        h2 = h1
        for j in range(nj):
            js = slice(j * 1024, (j + 1) * 1024)
            a = jnp.dot(hn2, wff1_v[:, js], preferred_element_type=F32)
            a_s[:, js] = a
            ra = jnp.maximum(a, 0.0)
            ff = _mx(ra * ra)
            ff_ref[:, js] = ff
            h2 = h2 + jnp.dot(ff, wff2_v[js, :], preferred_element_type=F32)

        r3 = lax.rsqrt(jnp.mean(h2 * h2, axis=-1, keepdims=True) + EPS)
        n3 = h2 * r3
        wf_row = wfn_ref[...]
        err = n3 * wf_row - tgt_ref[...]
        tokf = (i < nt).astype(F32)
        loss_ref[...] += 0.5 * jnp.sum(jnp.mean(err * err, axis=-1, keepdims=True) * tokf)
        dout = err * (tokf / D_MODEL)
        gwf_ref[...] += jnp.sum(dout * n3, axis=0, keepdims=True)
        dn3 = dout * wf_row
        dh2 = r3 * (dn3 - n3 * jnp.mean(dn3 * n3, axis=-1, keepdims=True))
        dh2m = _mx(dh2)
        dh2_ref[...] = dh2m

        dhn2 = jnp.zeros((tm, D_MODEL), F32)
        for j in range(nj):
            js = slice(j * 1024, (j + 1) * 1024)
            dff = lax.dot_general(dh2m, wff2_v[js, :], (((1,), (1,)), ((), ())), preferred_element_type=F32)
            da = _mx(dff * (2.0 * jnp.maximum(a_s[:, js], 0.0)))
            da_ref[:, js] = da
            dhn2 = dhn2 + lax.dot_general(da, wff1_v[:, js], (((1,), (1,)), ((), ())), preferred_element_type=F32)
        gw2_ref[...] += jnp.sum(dhn2 * n2, axis=0, keepdims=True)
        dn2 = dhn2 * w2n_row
        dh1 = dh2 + r2 * (dn2 - n2 * jnp.mean(dn2 * n2, axis=-1, keepdims=True))
        dh1_ref[...] = dh1
        dy_ref[...] = lax.dot_general(_mx(dh1), wout_v[...], (((1,), (1,)), ((), ())), preferred_element_type=F32)

    rows = lambda w: pl.BlockSpec((tm, w), lambda i: (i, 0))
    hbm = pl.BlockSpec(memory_space=pl.ANY)
    return pl.pallas_call(
        body, grid=(m // tm,), name="ffn_fwd_bwd",
        in_specs=[_token_tiles(D_MODEL, nt), _full((ROW_TILE, D_MODEL)), rows(D_MIX), _token_tiles(D_MODEL, nt),
                  _full((1, D_MODEL)), _full((1, D_MODEL)), hbm, hbm, hbm],
        out_specs=[_full((1, 128)), _full((1, D_MODEL)), _full((1, D_MODEL)), rows(D_FF), rows(D_FF), rows(D_MODEL),
                   rows(D_MODEL), rows(D_MODEL), rows(D_MIX)],
        out_shape=[jax.ShapeDtypeStruct((1, 128), F32), jax.ShapeDtypeStruct((1, D_MODEL), F32),
                   jax.ShapeDtypeStruct((1, D_MODEL), F32), jax.ShapeDtypeStruct((m, D_FF), MXU_DTYPE),
                   jax.ShapeDtypeStruct((m, D_FF), MXU_DTYPE), jax.ShapeDtypeStruct((m, D_MODEL), MXU_DTYPE),
                   jax.ShapeDtypeStruct((m, D_MODEL), F32), jax.ShapeDtypeStruct((m, D_MODEL), MXU_DTYPE),
                   jax.ShapeDtypeStruct((m, D_MIX), F32)],
        scratch_shapes=[pltpu.VMEM((D_MIX, D_MODEL), MXU_DTYPE), pltpu.VMEM((D_MODEL, D_FF), MXU_DTYPE),
                        pltpu.VMEM((D_FF, D_MODEL), MXU_DTYPE), pltpu.VMEM((tm, D_FF), F32),
                        pltpu.SemaphoreType.DMA((3,))],
        compiler_params=_params("arbitrary"),
    )(x, lead, y, tgt, w2n, wfn, wout, wff1, wff2)


def _in_proj_bwd(dproj, x, lead, dh1, w1, win, after):
    nt = x.shape[0] // ROW_TILE
    m = x.shape[0] + ROW_TILE
    tm = ROW_TILE

    def body(dp_ref, x_ref, lead_ref, dh1_ref, w1_ref, win_hbm, after_ref, gx_ref, gw1_ref, gmeta_ref, win_v, sem):
        i = pl.program_id(0)

        @pl.when(i == 0)
        def _():
            cp = pltpu.make_async_copy(win_hbm, win_v, sem)
            cp.start()
            cp.wait()
            gw1_ref[...] = jnp.zeros_like(gw1_ref)
            gmeta_ref[...] = jnp.zeros_like(gmeta_ref)

        dhn = lax.dot_general(dp_ref[...], win_v[...], (((1,), (1,)), ((), ())), preferred_element_type=F32)
        x = jnp.where(i == nt, lead_ref[...], x_ref[...])
        r = lax.rsqrt(jnp.mean(x * x, axis=-1, keepdims=True) + EPS)
        n = x * r
        gw1_ref[...] += jnp.sum(dhn * n, axis=0, keepdims=True)
        dn = dhn * w1_ref[...]
        dh0 = dh1_ref[...] + r * (dn - n * jnp.mean(dn * n, axis=-1, keepdims=True))

        @pl.when(i < nt)
        def _():
            gx_ref[...] = dh0

        @pl.when(i == nt)
        def _():
            gmeta_ref[...] = dh0[PAD_ROWS:LEAD, :] + dh0[LEAD + PAD_ROWS:2 * LEAD, :]

    rows = lambda w: pl.BlockSpec((tm, w), lambda i: (i, 0))
    hbm = pl.BlockSpec(memory_space=pl.ANY)
    return pl.pallas_call(
        body, grid=(m // tm,), name="in_proj_bwd",
        in_specs=[rows(PROJ_W), _token_tiles(D_MODEL, nt), _full((ROW_TILE, D_MODEL)), rows(D_MODEL),
                  _full((1, D_MODEL)), hbm, hbm],
        out_specs=[_token_tiles(D_MODEL, nt), _full((1, D_MODEL)), _full((N_META, D_MODEL))],
        out_shape=[jax.ShapeDtypeStruct(x.shape, F32), jax.ShapeDtypeStruct((1, D_MODEL), F32),
                   jax.ShapeDtypeStruct((N_META, D_MODEL), F32)],
        scratch_shapes=[pltpu.VMEM((D_MODEL, PROJ_W), MXU_DTYPE), pltpu.SemaphoreType.DMA],
        compiler_params=_params("arbitrary"),
    )(dproj, x, lead, dh1, w1, win, after)


MXU_DEPTH = 256


def _row_slab(m, cap):
    return max(k for k in range(MXU_DEPTH, cap + 1, MXU_DEPTH) if m % k == 0)


def _tn_matmul(a, b, name, tka, max_slab=768, tn=512):
    m, ka = a.shape
    nb = b.shape[1]
    tkm = _row_slab(m, max_slab)
    n_steps = m // tkm

    def body(a_ref, b_ref, o_ref, omx_ref):
        k = pl.program_id(1)

        @pl.when(k == 0)
        def _():
            o_ref[...] = jnp.zeros_like(o_ref)

        at = _mx(a_ref[...])
        for j in range(0, nb, tn):
            w = min(tn, nb - j)
            o_ref[:, j:j + w] += lax.dot_general(at, _mx(b_ref[:, j:j + w]), (((0,), (0,)), ((), ())),
                                                 preferred_element_type=F32)

        @pl.when(k == n_steps - 1)
        def _():
            omx_ref[...] = _mx(o_ref[...])

    out = pl.BlockSpec((tka, nb), lambda i, k: (i, 0))
    return pl.pallas_call(
        body, grid=(ka // tka, n_steps), name=name,
        in_specs=[pl.BlockSpec((tkm, tka), lambda i, k: (k, i)), pl.BlockSpec((tkm, nb), lambda i, k: (k, 0))],
        out_specs=[out, out],
        out_shape=[jax.ShapeDtypeStruct((ka, nb), F32), jax.ShapeDtypeStruct((ka, nb), MXU_DTYPE)],
        compiler_params=_params("arbitrary", "arbitrary"),
    )(a, b)


def _tn_matmul_banded(a, b, name, band, tka, tn=512):
    m, ka = a.shape
    nb = b.shape[1]
    tkm = _row_slab(m, 768)
    n_steps = m // tkm

    def body(a_ref, b_ref, o_ref, acc):
        k = pl.program_id(1)

        @pl.when(k == 0)
        def _():
            acc[...] = jnp.zeros_like(acc)

        at = _mx(a_ref[...])
        for j in range(0, nb, tn):
            w = min(tn, nb - j)
            acc[:, j:j + w] += lax.dot_general(at, _mx(b_ref[:, j:j + w]), (((0,), (0,)), ((), ())),
                                               preferred_element_type=F32)

        @pl.when(k == n_steps - 1)
        def _():
            for j in range(N_DEV):
                o_ref[j] = acc[:, j * band:(j + 1) * band]

    return pl.pallas_call(
        body, grid=(ka // tka, n_steps), name=name,
        in_specs=[pl.BlockSpec((tkm, tka), lambda i, k: (k, i)), pl.BlockSpec((tkm, nb), lambda i, k: (k, 0))],
        out_specs=pl.BlockSpec((N_DEV, tka, band), lambda i, k: (0, i, 0)),
        out_shape=jax.ShapeDtypeStruct((N_DEV, ka, band), F32),
        scratch_shapes=[pltpu.VMEM((tka, nb), F32)],
        compiler_params=_params("arbitrary", "arbitrary"),
    )(a, b)


def _assemble_bands(g, width):
    n, rows, band = g.shape
    tr = 256

    def body(g_ref, o_ref):
        parts = [g_ref[j] for j in range(n)] + [jnp.zeros((tr, width - n * band), g.dtype)]
        o_ref[...] = jnp.concatenate(parts, axis=1)

    return pl.pallas_call(
        body, grid=(rows // tr,), name="assemble_w_in",
        in_specs=[pl.BlockSpec((n, tr, band), lambda i: (0, i, 0))],
        out_specs=pl.BlockSpec((tr, width), lambda i: (i, 0)),
        out_shape=jax.ShapeDtypeStruct((rows, width), g.dtype),
        compiler_params=_params("arbitrary"),
    )(g)


def _chunk_block(b, c, nb, nc):
    return jnp.where(c == 0, nb * (nc - 1) + b, b * (nc - 1) + c - 1)

def _mixer_fwd(proj, cw, cb, dt_bias, a_log, d_x, nw, pool_w, pool_scale, nb, shards, by_cols):
    m = proj.shape[0]
    nc = m // nb // CHUNK
    ns = len(shards)
    row_widths = [D_XBC, 128, 128, D_SSM, D_SSM, D_POOL]
    rows = jnp.concatenate([cb, dt_bias, a_log, d_x, nw, pool_scale], axis=1)
    pairs = (nc - 1) // 2
    n_steps = nb * pairs + 1

    def block_of(step):
        return jnp.where(step == 0, nb * pairs, step - 1)

    def chunks_of(step):
        lead = step == 0
        b = jnp.maximum(step - 1, 0) // pairs
        j = jnp.maximum(step - 1, 0) - b * pairs
        return [(jnp.where(lead, 0, b), jnp.where(lead, 0, 2 * j + 1), 0),
                (jnp.where(lead, 1, b), jnp.where(lead, 0, 2 * j + 2), CHUNK)]

    def body(p_ref, cw_ref, rows_ref, pw_ref, *rest):
        shard_refs, (y_ref, ypre_ref, pre_ref, st_ref) = rest[:ns], rest[ns:ns + 4]
        gathered_refs, carries, constants_refs = rest[ns + 4:2 * ns + 4], rest[2 * ns + 4:2 * ns + 7], rest[2 * ns + 7:2 * ns + 10]
        gather = _Gather(shard_refs, gathered_refs, by_cols, *rest[2 * ns + 10:])
        step = pl.program_id(0)

        @pl.when(step == 0)
        def _():
            gather.start()
            _fill_ssd_constants(**dict(zip(["e", "et_f32", "tril"], constants_refs)))

        @pl.when(step == n_steps // 2)
        def _():
            gather.forward()

        shared = (cw_ref, pw_ref, _row_views(rows_ref, row_widths), constants_refs)
        for b, c, row0 in chunks_of(step):
            at = lambda ref: ref.at[row0:row0 + CHUNK]
            one_chunk(c, at(p_ref), at(y_ref), at(ypre_ref), at(pre_ref), st_ref.at[row0 // CHUNK], *shared,
                      [carry.at[b] for carry in carries])

        @pl.when(step == n_steps - 1)
        def _():
            gather.finish()

    def one_chunk(c, p_ref, y_ref, ypre_ref, pre_ref, st_ref, cw_ref, pw_ref, row_refs, constants_refs, carries):
        cb_ref, dtb_ref, alog_ref, dx_ref, nw_ref, ps_ref = row_refs
        e_ref, et_ref, tril_ref = constants_refs
        xtail, utail, state = carries

        @pl.when(c == 0)
        def _():
            xtail[...] = jnp.zeros_like(xtail)
            utail[...] = jnp.zeros_like(utail)
            state[...] = jnp.zeros_like(state)

        valid = (c > 0) | (lax.broadcasted_iota(jnp.int32, (CHUNK, 1), 0) >= PAD_ROWS)

        u = p_ref[:, 0:D_POOL]
        inv_cnt, lane = _pool_inv_count(c)
        win = _pool_window_sums(jnp.concatenate([utail[...], u], axis=0), lane)
        utail[...] = u[CHUNK - HALO:, :]
        pooled = win * inv_cnt - u
        mixed = jnp.concatenate(
            [_dot(pooled[:, g * 128:(g + 1) * 128], pw_ref[g]) for g in range(len(POOL_WINDOWS))], axis=1)
        y_ref[:, 0:D_POOL] = _mx(mixed * ps_ref[...])

        xbc = p_ref[:, OFF_X:OFF_X + D_XBC]
        pre = _conv_pre(jnp.concatenate([xtail[...], xbc], axis=0), xbc, cw_ref[...], cb_ref[...])
        xtail[...] = xbc[CHUNK - HALO:, :]
        pre_ref[...] = pre
        xc = pre * _sigmoid(pre)
        dt, _, a_col, _, _ = _dt_and_cumsum(p_ref[:, OFF_DT:OFF_DT + 128], dtb_ref[...], alog_ref[...], valid,
                                            tril_ref[...])
        s_prev = state[...]
        st_ref[...] = s_prev
        yp, s_new = _ssd_chunk_fwd(xc[:, 0:D_SSM], xc[:, D_SSM:D_SSM + 512], xc[:, D_SSM + 512:], dt, a_col, s_prev,
                                   dx_ref[...], e_ref[...], et_ref[...])
        state[...] = s_new
        ypre_ref[...] = yp
        z = p_ref[:, OFF_Z:OFF_Z + D_SSM]
        yz = yp * (z * _sigmoid(z))
        outs = []
        for g in range(N_GROUPS):
            gs = slice(g * GROUP_CH, (g + 1) * GROUP_CH)
            r = lax.rsqrt(jnp.mean(yz[:, gs] * yz[:, gs], axis=-1, keepdims=True) + EPS)
            outs.append(yz[:, gs] * r)
        y_ref[:, D_POOL:] = _mx(jnp.concatenate(outs, axis=1) * nw_ref[...])

    blk = lambda w: pl.BlockSpec((2 * CHUNK, w), lambda s: (block_of(s), 0))
    hbm = pl.BlockSpec(memory_space=pl.ANY)
    outs = pl.pallas_call(
        body, grid=(n_steps,), name="mixer_fwd",
        in_specs=[blk(PROJ_W), _full((4, D_XBC)), _full((1, sum(row_widths))), _full((4, 128, 128))] + [hbm] * ns,
        out_specs=[blk(D_MIX), blk(D_SSM), blk(D_XBC),
                   pl.BlockSpec((2, D_SSM, D_STATE), lambda s: (block_of(s), 0, 0))] + [hbm] * ns,
        out_shape=[jax.ShapeDtypeStruct((m, D_MIX), MXU_DTYPE), jax.ShapeDtypeStruct((m, D_SSM), F32),
                   jax.ShapeDtypeStruct((m, D_XBC), F32), jax.ShapeDtypeStruct((m // CHUNK, D_SSM, D_STATE), F32)]
        + _Gather.out_shapes(shards, by_cols),
        scratch_shapes=[pltpu.VMEM((nb, HALO, D_XBC), F32), pltpu.VMEM((nb, HALO, D_POOL), F32),
                        pltpu.VMEM((nb, D_SSM, D_STATE), F32)] + _ssd_constant_scratch(["e", "et_f32", "tril"])
        + _Gather.scratch(ns),
        compiler_params=_params("arbitrary"),
    )(proj, cw, rows, pool_w, *shards)
    return outs[0], outs[1], outs[2], outs[3], outs[4:]


def _mixer_bwd(proj, dy, ypre, conv_pre, states, cw, dt_bias, a_log, d_x, nw, pool_w, pool_scale, nb, chip_sums):
    m = proj.shape[0]
    nc = m // nb // CHUNK
    hb = CHUNK // HALO
    ns = len(chip_sums)
    row_widths = [128, 128, D_SSM, D_SSM, D_POOL]
    rows = jnp.concatenate([dt_bias, a_log, d_x, nw, pool_scale], axis=1)
    grad_row_widths = [D_XBC, 128, 128, 128, D_SSM, D_POOL]
    constants = ["e", "et", "et_f32", "tril", "triu"]

    pairs = (nc - 1) // 2
    n_steps = nb * pairs + 1

    def block_of(step):
        b = jnp.minimum(step // pairs, nb - 1)
        j = pairs - 1 - (step - b * pairs)
        return jnp.where(step == n_steps - 1, nb * pairs, b * pairs + j)

    def chunks_of(step):
        lead = step == n_steps - 1
        b = jnp.minimum(step // pairs, nb - 1)
        j = pairs - 1 - (step - b * pairs)
        return [(jnp.where(lead, 0, b), jnp.where(lead, 0, 2 * j + 2), jnp.where(lead, 0, CHUNK)),
                (jnp.where(lead, 1, b), jnp.where(lead, 0, 2 * j + 1), jnp.where(lead, CHUNK, 0))]

    def body(p_ref, halo_ref, dy_ref, ypre_ref, pre_ref, st_ref, cw_ref, rows_ref, pw_ref, *rest):
        cs_refs = rest[:ns]
        dp_ref, gcw_ref, grows_ref, gpw_ref = rest[ns:ns + 4]
        part_refs, carries, constants_refs = rest[ns + 4:2 * ns + 4], rest[2 * ns + 4:2 * ns + 7], rest[2 * ns + 7:2 * ns + 12]
        exchange = _ChipExchange(cs_refs, part_refs, [False] * ns, *rest[2 * ns + 12:])
        step = pl.program_id(0)

        @pl.when(step == 0)
        def _():
            exchange.start()
            _fill_ssd_constants(**dict(zip(constants, constants_refs)))
            for r in (gcw_ref, grows_ref, gpw_ref):
                r[...] = jnp.zeros_like(r)

        shared = (cw_ref, pw_ref, _row_views(rows_ref, row_widths), gcw_ref, gpw_ref,
                  _row_views(grows_ref, grad_row_widths), constants_refs)
        halos = [p_ref[CHUNK - HALO:CHUNK, 0:D_POOL], halo_ref[...]]
        for (b, c, row0), halo in zip(chunks_of(step), halos):
            rows_here = pl.ds(pl.multiple_of(row0, CHUNK), CHUNK)
            at = lambda ref: ref.at[rows_here]
            one_chunk(b, c, at(p_ref), halo, at(dy_ref), at(ypre_ref), at(pre_ref), st_ref.at[row0 // CHUNK], at(dp_ref),
                      *shared, [carry.at[b] for carry in carries])

        @pl.when(step == n_steps - 1)
        def _():
            exchange.finish()

    def one_chunk(b, c, p_ref, halo, dy_ref, ypre_ref, pre_ref, st_ref, dp_ref, cw_ref, pw_ref, row_refs, gcw_ref,
                  gpw_ref, grad_row_refs, constants_refs, carries):
        dtb_ref, alog_ref, dx_ref, nw_ref, ps_ref = row_refs
        gcb_ref, gdtb_ref, galog_ref, gd_ref, gnw_ref, gps_ref = grad_row_refs
        ds_carry, dpre_next, dq_next = carries
        e_ref, et_ref, etf_ref, tril_ref, triu_ref = constants_refs

        @pl.when(c == nc - 1)
        def _():
            ds_carry[...] = jnp.zeros_like(ds_carry)
            dpre_next[...] = jnp.zeros_like(dpre_next)
            dq_next[...] = jnp.zeros_like(dq_next)

        valid = (c > 0) | (lax.broadcasted_iota(jnp.int32, (CHUNK, 1), 0) >= PAD_ROWS)
        first = c > 0

        u = p_ref[:, 0:D_POOL]
        u_halo = jnp.where(first, halo, 0.0)
        inv_cnt, lane = _pool_inv_count(c)
        pooled = _pool_window_sums(jnp.concatenate([u_halo, u], axis=0), lane) * inv_cnt - u
        dyp = dy_ref[:, 0:D_POOL]
        ps = ps_ref[...]
        dmixed = dyp * ps
        mixed, dpooled = [], []
        for g in range(len(POOL_WINDOWS)):
            gsl = slice(g * 128, (g + 1) * 128)
            pw = pw_ref[g]
            mixed.append(_dot(pooled[:, gsl], pw))
            dpooled.append(_dot_nt(dmixed[:, gsl], pw))
            gpw_ref[g] += _dot_tn(pooled[:, gsl], dmixed[:, gsl])
        gps_ref[...] += jnp.sum(dyp * jnp.concatenate(mixed, axis=1), axis=0, keepdims=True)
        dpooled = jnp.concatenate(dpooled, axis=1)
        dq = dpooled * inv_cnt
        du = _pool_window_sums_ahead(jnp.concatenate([dq, dq_next[...]], axis=0), lane) - dpooled
        dq_next[...] = dq[0:HALO, :]
        dp_ref[:, 0:D_POOL] = _mx(du)

        yp = ypre_ref[...]
        z = p_ref[:, OFF_Z:OFF_Z + D_SSM]
        sz, dsz = _silu_and_grad(z)
        yz = yp * sz
        do = dy_ref[:, D_POOL:]
        nw_row = nw_ref[...]
        dyz = []
        gnw = []
        for g in range(N_GROUPS):
            gs = slice(g * GROUP_CH, (g + 1) * GROUP_CH)
            r = lax.rsqrt(jnp.mean(yz[:, gs] * yz[:, gs], axis=-1, keepdims=True) + EPS)
            n = yz[:, gs] * r
            gnw.append(jnp.sum(do[:, gs] * n, axis=0, keepdims=True))
            dn = do[:, gs] * nw_row[:, gs]
            dyz.append(r * (dn - n * jnp.mean(dn * n, axis=-1, keepdims=True)))
        gnw_ref[...] += jnp.concatenate(gnw, axis=1)
        dyz = jnp.concatenate(dyz, axis=1)
        dp_ref[:, OFF_Z:OFF_Z + D_SSM] = _mx(dyz * yp * dsz)
        dyp_ssm = dyz * sz

        xc, dsilu = _silu_and_grad(pre_ref[...])
        dtr = p_ref[:, OFF_DT:OFF_DT + 128]
        dt, a_row, a_col, dt_pre, head = _dt_and_cumsum(dtr, dtb_ref[...], alog_ref[...], valid, tril_ref[...])
        dxs, dbm, dcm, ddt, da, dd, ds_prev = _ssd_chunk_bwd(
            xc[:, 0:D_SSM], xc[:, D_SSM:D_SSM + 512], xc[:, D_SSM + 512:], dt, a_row, a_col, st_ref[...],
            ds_carry[...], dyp_ssm, dx_ref[...], e_ref[...], et_ref[...], etf_ref[...], triu_ref[...])
        ds_carry[...] = ds_prev
        gd_ref[...] += dd
        galog_ref[...] += da * a_row
        ddtr = jnp.where(valid & head, ddt * _sigmoid(dt_pre), 0.0)
        gdtb_ref[...] += jnp.sum(ddtr, axis=0, keepdims=True)
        dp_ref[:, OFF_DT:OFF_DT + 128] = _mx(ddtr)

        dpre = jnp.concatenate([dxs, dbm, dcm], axis=1) * dsilu
        gcb_ref[...] += jnp.sum(dpre, axis=0, keepdims=True)
        dext = jnp.concatenate([dpre, dpre_next[...]], axis=0)
        dpre_next[...] = dpre[0:HALO, :]
        ups = [_shift_up(dext, 3 - k) for k in range(4)]
        xbc = p_ref[:, OFF_X:OFF_X + D_XBC]
        gcw_ref[...] += jnp.concatenate([jnp.sum(xbc * ups[k], axis=0, keepdims=True) for k in range(4)], axis=0)
        cw = cw_ref[...]
        dp_ref[:, OFF_X:OFF_X + D_XBC] = _mx(cw[3:4, :] * ups[3] + cw[2:3, :] * ups[2]
                                             + cw[1:2, :] * ups[1] + cw[0:1, :] * ups[0])

    blk = lambda w: pl.BlockSpec((2 * CHUNK, w), lambda s: (block_of(s), 0))

    def halo_rows(s):
        b = jnp.minimum(s // pairs, nb - 1)
        below = block_of(s) * 2 * hb - 1
        return jnp.where(block_of(s) == b * pairs, _chunk_block(b, 0, nb, nc) * hb + hb - 1, below)

    hbm = pl.BlockSpec(memory_space=pl.ANY)
    outs = pl.pallas_call(
        body, grid=(n_steps,), name="mixer_bwd",
        in_specs=[blk(PROJ_W), pl.BlockSpec((HALO, D_POOL), lambda s: (halo_rows(s), 0)), blk(D_MIX), blk(D_SSM),
                  blk(D_XBC), pl.BlockSpec((2, D_SSM, D_STATE), lambda s: (block_of(s), 0, 0)),
                  _full((4, D_XBC)), _full((1, sum(row_widths))), _full((4, 128, 128))] + [hbm] * ns,
        out_specs=[blk(PROJ_W), _full((4, D_XBC)), _full((1, sum(grad_row_widths))), _full((4, 128, 128))] + [hbm] * ns,
        out_shape=[jax.ShapeDtypeStruct((m, PROJ_W), MXU_DTYPE), jax.ShapeDtypeStruct((4, D_XBC), F32),
                   jax.ShapeDtypeStruct((1, sum(grad_row_widths)), F32), jax.ShapeDtypeStruct((4, 128, 128), F32)]
        + _ChipExchange.out_shapes(chip_sums, [False] * ns),
        scratch_shapes=[pltpu.VMEM((nb, D_SSM, D_STATE), F32), pltpu.VMEM((nb, HALO, D_XBC), F32),
                        pltpu.VMEM((nb, HALO, D_POOL), F32)] + _ssd_constant_scratch(constants)
        + _ChipExchange.scratch(ns),
        compiler_params=_params("arbitrary"),
    )(proj, proj, dy, ypre, conv_pre, states, cw, rows, pool_w, *chip_sums)
    dproj, g_cw, g_rows, g_pw = outs[:4]
    offs = np.cumsum([0] + grad_row_widths)
    g_cb, g_dtb, g_alog, g_d, g_nw, g_ps = (g_rows[:, a:b] for a, b in zip(offs[:-1], offs[1:]))
    return (dproj, g_cw, g_cb, g_dtb, g_alog, g_d, g_nw, g_pw, g_ps), outs[4:]


MESH_IDS = pl.DeviceIdType.MESH
_HBM = pl.BlockSpec(memory_space=pltpu.HBM)


def _coords():
    return lax.axis_index("x"), lax.axis_index("y"), lax.axis_index("c")


def _other_chips(x, y):
    return [(1 - x, y), (x, 1 - y), (1 - x, 1 - y)]


class _Gather:
    def __init__(self, ins, outs, by_cols, send_sems, recv_sems, local_sems):
        self.ins, self.outs, self.by_cols, self.n = ins, outs, by_cols, len(ins)
        self.send_sems, self.recv_sems, self.local_sems = send_sems, recv_sems, local_sems
        self.x, self.y, self.c = _coords()
        self.me, self.sibling = (self.x, self.y, self.c), (self.x, self.y, 1 - self.c)
        self.chips = _other_chips(self.x, self.y)

    @staticmethod
    def scratch(n):
        return [pltpu.SemaphoreType.DMA((7 * n,)), pltpu.SemaphoreType.DMA((7 * n,)), pltpu.SemaphoreType.DMA((n,))]

    @staticmethod
    def out_shapes(shards, by_cols):
        return [jax.ShapeDtypeStruct((s.shape[0], N_DEV * s.shape[1]) if cols else (N_DEV,) + s.shape, s.dtype)
                for s, cols in zip(shards, by_cols)]

    def _block(self, t, device):
        idx = 4 * device[0] + 2 * device[1] + device[2]
        if not self.by_cols[t]:
            return self.outs[t].at[idx]
        w = self.ins[t].shape[1]
        return self.outs[t].at[:, pl.ds(pl.multiple_of(idx * w, w), w)]

    def _copy(self, t, k, block, to, own=False):
        dst = self._block(t, block)
        return pltpu.make_async_remote_copy(
            src_ref=self.ins[t] if own else dst, dst_ref=dst, send_sem=self.send_sems.at[t * 7 + k],
            recv_sem=self.recv_sems.at[t * 7 + k], device_id=to, device_id_type=MESH_IDS)

    def _mine(self):
        return [pltpu.make_async_copy(self.ins[t], self._block(t, self.me), self.local_sems.at[t])
                for t in range(self.n)]

    def _first(self):
        cps = []
        for t in range(self.n):
            cps.append(self._copy(t, 0, self.me, self.sibling, own=True))
            cps += [self._copy(t, 1 + j, self.me, (*chip, self.c), own=True) for j, chip in enumerate(self.chips)]
        return cps

    def _passed(self):
        return [self._copy(t, 4 + j, (*chip, self.c), self.sibling)
                for j, chip in enumerate(self.chips) for t in range(self.n)]

    def start(self):
        for cp in self._mine() + self._first():
            cp.start()

    def forward(self):
        for j, chip in enumerate(self.chips):
            for t in range(self.n):
                self._copy(t, 1 + j, (*chip, self.c), self.me).wait_recv()
                self._copy(t, 4 + j, (*chip, self.c), self.sibling).start()

    def finish(self):
        for t in range(self.n):
            self._copy(t, 0, self.sibling, self.me).wait_recv()
            for j, chip in enumerate(self.chips):
                self._copy(t, 4 + j, (*chip, 1 - self.c), self.me).wait_recv()
        for cp in self._first() + self._passed():
            cp.wait_send()
        for cp in self._mine():
            cp.wait()


def _weight_gather(shards):
    n = len(shards)

    def body(*refs):
        g = _Gather(refs[:n], refs[n:2 * n], [False] * n, *refs[2 * n:])
        g.start()
        g.forward()
        g.finish()

    return pl.pallas_call(
        body, name="weight_gather",
        in_specs=[_HBM] * n, out_specs=[_HBM] * n,
        out_shape=_Gather.out_shapes(shards, [False] * n),
        scratch_shapes=_Gather.scratch(n),
    )(*shards)


def _grad_exchange_d2d(gs, name, swapped=()):
    n, ns = len(gs), len(swapped)

    def body(*refs):
        ins, whole_ins = refs[:n], refs[n:n + ns]
        got, whole_got = refs[n + ns:2 * n + ns], refs[2 * n + ns:2 * (n + ns)]
        send_sems, recv_sems = refs[2 * (n + ns):]
        x, y, c = _coords()
        pairs = [(ins[t].at[k, 1 - c], got[t].at[k]) for t in range(n) for k in range(4)] + list(zip(whole_ins, whole_got))
        remote = [pltpu.make_async_remote_copy(
            src_ref=src, dst_ref=dst, send_sem=send_sems.at[i], recv_sem=recv_sems.at[i], device_id=(x, y, 1 - c),
            device_id_type=MESH_IDS) for i, (src, dst) in enumerate(pairs)]
        for cp in remote:
            cp.start()
        for cp in remote:
            cp.wait_recv()
        for cp in remote:
            cp.wait_send()

    outs = pl.pallas_call(
        body, name=name,
        in_specs=[_HBM] * (n + ns), out_specs=[_HBM] * (n + ns),
        out_shape=[jax.ShapeDtypeStruct((4,) + g.shape[2:], g.dtype) for g in gs]
        + [jax.ShapeDtypeStruct(a.shape, a.dtype) for a in swapped],
        scratch_shapes=[pltpu.SemaphoreType.DMA((4 * n + ns,)), pltpu.SemaphoreType.DMA((4 * n + ns,))],
    )(*gs, *swapped)
    return outs[:n], outs[n:]


def _small_allreduce(pack):
    rows = pack.shape[0]

    def body(p_ref, o_ref, sib_ref, parts_ref, send_sems, recv_sems):
        x, y, c = _coords()
        my_chip = 2 * x + y
        swap = pltpu.make_async_remote_copy(src_ref=p_ref, dst_ref=sib_ref, send_sem=send_sems.at[0],
                                            recv_sem=recv_sems.at[0], device_id=(x, y, 1 - c), device_id_type=MESH_IDS)
        swap.start()
        swap.wait_recv()
        parts_ref[my_chip] = p_ref[...] + sib_ref[...]
        remote = [pltpu.make_async_remote_copy(
            src_ref=parts_ref.at[my_chip], dst_ref=parts_ref.at[my_chip], send_sem=send_sems.at[1 + j],
            recv_sem=recv_sems.at[1 + j], device_id=(cx, cy, c), device_id_type=MESH_IDS)
            for j, (cx, cy) in enumerate(_other_chips(x, y))]
        for cp in remote:
            cp.start()
        for j, (cx, cy) in enumerate(_other_chips(x, y)):
            slot = parts_ref.at[2 * cx + cy]
            pltpu.make_async_remote_copy(src_ref=slot, dst_ref=slot, send_sem=send_sems.at[1 + j],
                                         recv_sem=recv_sems.at[1 + j], device_id=(cx, cy, c),
                                         device_id_type=MESH_IDS).wait_recv()
        o_ref[...] = ((parts_ref[0] + parts_ref[1]) + parts_ref[2]) + parts_ref[3]
        swap.wait_send()
        for cp in remote:
            cp.wait_send()

    vmem = pl.BlockSpec(memory_space=pltpu.VMEM)
    return pl.pallas_call(
        body, name="small_allreduce", in_specs=[vmem], out_specs=vmem,
        out_shape=jax.ShapeDtypeStruct((rows, 128), F32),
        scratch_shapes=[pltpu.VMEM((rows, 128), F32), pltpu.VMEM((4, rows, 128), F32),
                        pltpu.SemaphoreType.DMA((4,)), pltpu.SemaphoreType.DMA((4,))],
    )(pack)


class _ChipExchange:
    def __init__(self, ins, outs, whole, send_sems, recv_sems, local_sems):
        self.ins, self.outs, self.whole, self.n = ins, outs, whole, len(ins)
        self.send_sems, self.recv_sems, self.local_sems = send_sems, recv_sems, local_sems
        self.x, self.y, self.c = _coords()
        self.my_chip = 2 * self.x + self.y
        self.chips = _other_chips(self.x, self.y)

    @staticmethod
    def scratch(n):
        return [pltpu.SemaphoreType.DMA((3 * n,)), pltpu.SemaphoreType.DMA((3 * n,)), pltpu.SemaphoreType.DMA((n,))]

    def _src(self, t, k):
        return self.ins[t] if self.whole[t] else self.ins[t].at[k]

    def _local(self):
        return [pltpu.make_async_copy(self._src(t, self.my_chip), self.outs[t].at[self.my_chip], self.local_sems.at[t])
                for t in range(self.n)]

    def _remote(self):
        return [pltpu.make_async_remote_copy(
            src_ref=self._src(t, 2 * cx + cy), dst_ref=self.outs[t].at[self.my_chip],
            send_sem=self.send_sems.at[t * 3 + j], recv_sem=self.recv_sems.at[t * 3 + j],
            device_id=(cx, cy, self.c), device_id_type=MESH_IDS)
            for t in range(self.n) for j, (cx, cy) in enumerate(self.chips)]

    def start(self):
        for cp in self._remote() + self._local():
            cp.start()

    def finish(self):
        for t in range(self.n):
            for j, (cx, cy) in enumerate(self.chips):
                slot = self.outs[t].at[2 * cx + cy]
                pltpu.make_async_remote_copy(
                    src_ref=slot, dst_ref=slot, send_sem=self.send_sems.at[t * 3 + j],
                    recv_sem=self.recv_sems.at[t * 3 + j], device_id=(cx, cy, self.c),
                    device_id_type=MESH_IDS).wait_recv()
        for cp in self._remote():
            cp.wait_send()
        for cp in self._local():
            cp.wait()

    @staticmethod
    def out_shapes(arrs, whole):
        return [jax.ShapeDtypeStruct(((4,) + a.shape) if w else a.shape, a.dtype) for a, w in zip(arrs, whole)]


_SEMAPHORES = pl.BlockSpec(memory_space=pltpu.SEMAPHORE)
_SIDE_EFFECT = pltpu.SideEffectType.DATAFLOW_SIDE_EFFECTING


def _split_exchange_copies(srcs, lands, whole, send_sems, recv_sems, waiting):
    x, y, c = _coords()
    copies = []
    for t in range(len(srcs)):
        for j, (cx, cy) in enumerate(_other_chips(x, y)):
            src = srcs[t] if whole[t] else srcs[t].at[2 * cx + cy]
            dst = lands[t].at[2 * cx + cy] if waiting else lands[t].at[2 * x + y]
            copies.append(pltpu.make_async_remote_copy(
                src_ref=src, dst_ref=dst, send_sem=send_sems.at[3 * t + j], recv_sem=recv_sems.at[3 * t + j],
                device_id=(cx, cy, c), device_id_type=MESH_IDS))
    return copies


def _chip_exchange_start(arrays, whole, name):
    n = len(arrays)

    def body(*refs):
        srcs, lands = refs[:n], refs[n:2 * n]
        send_sems, recv_sems = refs[2 * n:2 * n + 2]
        for cp in _split_exchange_copies(srcs, lands, whole, send_sems, recv_sems, waiting=False):
            cp.start()
        token = refs[-1]
        token[...] = jnp.zeros_like(token)

    land_shapes = [((4,) + a.shape) if w else a.shape for a, w in zip(arrays, whole)]
    hbm = lambda shape, a: pltpu.HBM(shape, a.dtype)
    outs = pl.pallas_call(
        body, name=name,
        out_shape=(pltpu.SemaphoreType.DMA((3 * n,)), pltpu.SemaphoreType.DMA((3 * n,)),
                   *[hbm(a.shape, a) for a in arrays], *[hbm(s, a) for s, a in zip(land_shapes, arrays)],
                   jax.ShapeDtypeStruct((8, 128), F32)),
        in_specs=(_HBM,) * (2 * n),
        out_specs=(_SEMAPHORES, _SEMAPHORES) + (_HBM,) * (2 * n) + (pl.BlockSpec(memory_space=pltpu.VMEM),),
        input_output_aliases={i: 2 + i for i in range(2 * n)},
        compiler_params=pltpu.CompilerParams(has_side_effects=_SIDE_EFFECT),
    )(*[pltpu.with_memory_space_constraint(a, pltpu.HBM) for a in arrays],
      *[pltpu.with_memory_space_constraint(lax.empty(s, a.dtype), pltpu.HBM) for s, a in zip(land_shapes, arrays)])
    return outs[0], outs[1], outs[2:2 + n], outs[2 + n:2 + 2 * n], outs[-1]


def _chip_exchange_wait(send_sems, recv_sems, srcs, lands, whole, after, name):
    n = len(srcs)

    def body(*refs):
        src_refs, land_refs = refs[:n], refs[n:2 * n]
        for cp in _split_exchange_copies(src_refs, land_refs, whole, refs[2 * n], refs[2 * n + 1], waiting=True):
            cp.wait_send()
            cp.wait_recv()

    outs = pl.pallas_call(
        body, name=name, out_shape=tuple(pltpu.HBM(a.shape, a.dtype) for a in (*srcs, *lands)),
        in_specs=(_HBM,) * (2 * n) + (_SEMAPHORES, _SEMAPHORES, pl.BlockSpec(memory_space=pl.ANY)),
        out_specs=(_HBM,) * (2 * n), input_output_aliases={i: i for i in range(2 * n)},
        compiler_params=pltpu.CompilerParams(has_side_effects=_SIDE_EFFECT),
    )(*srcs, *lands, send_sems, recv_sems, after)
    return outs[:n], outs[n:]


def _sum_two(a, b, name):
    def body(a_ref, b_ref, o_ref):
        o_ref[...] = a_ref[...] + b_ref[...]

    return pl.pallas_call(body, name=name, out_shape=jax.ShapeDtypeStruct(a.shape, a.dtype))(a, b)


def _sum_chips(landed, own, chip, name):
    def body(chip_ref, l_ref, own_ref, o_ref):
        part = lambda k: jnp.where(chip_ref[0] == k, own_ref[...], l_ref[k])
        o_ref[...] = ((part(0) + part(1)) + part(2)) + part(3)

    grid_spec = pltpu.PrefetchScalarGridSpec(
        num_scalar_prefetch=1, grid=(1,),
        in_specs=[pl.BlockSpec(landed.shape, lambda i, c: (0, 0, 0)), pl.BlockSpec(own.shape, lambda i, c: (0, 0))],
        out_specs=pl.BlockSpec(own.shape, lambda i, c: (0, 0)))
    return pl.pallas_call(body, grid_spec=grid_spec, name=name,
                          out_shape=jax.ShapeDtypeStruct(own.shape, own.dtype))(chip, landed, own)


def _row_tile(rows, cols, n_arrays):
    budget = 24 * 1024 * 1024
    padded = -(-cols // 128) * 128
    step = 16 if rows % 16 == 0 else 8
    tr = max(step, budget // (n_arrays * 2 * 4 * padded) // step * step)
    while rows % tr:
        tr -= step
    return tr


def _chip_sum(g, got, core, name):
    rows, cols = g.shape[2:]
    tr = _row_tile(rows, cols, 3)

    def body(c_ref, a_ref, b_ref, o_ref):
        o_ref[...] = (a_ref[...] + b_ref[...].astype(F32)).astype(o_ref.dtype)

    grid_spec = pltpu.PrefetchScalarGridSpec(
        num_scalar_prefetch=1, grid=(4, rows // tr),
        in_specs=[pl.BlockSpec((None, None, tr, cols), lambda k, i, c: (k, c[0], i, 0)),
                  pl.BlockSpec((None, tr, cols), lambda k, i, c: (k, i, 0))],
        out_specs=pl.BlockSpec((None, tr, cols), lambda k, i, c: (k, i, 0)))
    return pl.pallas_call(body, grid_spec=grid_spec, name=name,
                          out_shape=jax.ShapeDtypeStruct((4, rows, cols), MXU_DTYPE),
                          compiler_params=_params("arbitrary", "arbitrary"))(core, g, got)


def _adamw_math(w, g, m, v):
    m2 = ADAM_B1 * m + (1.0 - ADAM_B1) * g
    v2 = ADAM_B2 * v + (1.0 - ADAM_B2) * (g * g)
    m_hat = m2 / (1.0 - ADAM_B1 ** ADAM_STEP)
    v_hat = v2 / (1.0 - ADAM_B2 ** ADAM_STEP)
    delta = -ADAM_LR * (m_hat / (jnp.sqrt(v_hat) + ADAM_EPS) + ADAM_WD * w)
    return delta, m2, v2


def _adamw(parts, w, m, v, name, own=None, chip=None):
    rows, cols = w.shape
    tr = _row_tile(rows, cols, 11 if own is None else 15)

    def body(*refs):
        if own is None:
            p_ref, w_ref, m_ref, v_ref, g_ref, d_ref, m2_ref, v2_ref = refs
            part = lambda k: p_ref[k].astype(F32)
        else:
            chip_ref, p_ref, own_ref, w_ref, m_ref, v_ref, g_ref, d_ref, m2_ref, v2_ref = refs
            part = lambda k: jnp.where(chip_ref[0] == k, own_ref[k], p_ref[k]).astype(F32)
        g = ((part(0) + part(1)) + part(2)) + part(3)
        d, m2, v2 = _adamw_math(w_ref[...], g, m_ref[...], v_ref[...])
        g_ref[...] = g
        d_ref[...] = d
        m2_ref[...] = m2
        v2_ref[...] = v2

    blk = pl.BlockSpec((tr, cols), lambda i, *_: (i, 0))
    pblk = pl.BlockSpec((4, tr, cols), lambda i, *_: (0, i, 0))
    out = jax.ShapeDtypeStruct((rows, cols), F32)
    if own is None:
        return pl.pallas_call(body, grid=(rows // tr,), name=name, in_specs=[pblk, blk, blk, blk],
                              out_specs=[blk] * 4, out_shape=[out] * 4,
                              compiler_params=_params("arbitrary"))(parts, w, m, v)
    grid_spec = pltpu.PrefetchScalarGridSpec(num_scalar_prefetch=1, grid=(rows // tr,),
                                             in_specs=[pblk, pblk, blk, blk, blk], out_specs=[blk] * 4)
    return pl.pallas_call(body, grid_spec=grid_spec, name=name, out_shape=[out] * 4,
                          compiler_params=_params("arbitrary"))(chip, parts, own, w, m, v)


def _adamw_transposed(parts_t, w, m, v, name):
    rows, cols = w.shape
    tc = 256

    def body(p_ref, w_ref, m_ref, v_ref, g_ref, d_ref, m2_ref, v2_ref):
        part = lambda k: p_ref[k].astype(F32)
        g = (((part(0) + part(1)) + part(2)) + part(3)).T
        d, m2, v2 = _adamw_math(w_ref[...], g, m_ref[...], v_ref[...])
        g_ref[...] = g
        d_ref[...] = d
        m2_ref[...] = m2
        v2_ref[...] = v2

    blk = pl.BlockSpec((rows, tc), lambda i: (0, i))
    out = jax.ShapeDtypeStruct((rows, cols), F32)
    return pl.pallas_call(body, grid=(cols // tc,), name=name,
                          in_specs=[pl.BlockSpec((4, tc, rows), lambda i: (0, i, 0)), blk, blk, blk],
                          out_specs=[blk] * 4, out_shape=[out] * 4,
                          compiler_params=_params("arbitrary"))(parts_t, w, m, v)


def _adamw_small(gs, ws, ms, vs):
    n = len(ws)

    def body(*refs):
        g_refs, w_refs, m_refs, v_refs = (refs[k * n:(k + 1) * n] for k in range(4))
        d_refs, m2_refs, v2_refs = (refs[(4 + k) * n:(5 + k) * n] for k in range(3))
        for t in range(n):
            d, m2, v2 = _adamw_math(w_refs[t][...], g_refs[t][...], m_refs[t][...], v_refs[t][...])
            d_refs[t][...] = d
            m2_refs[t][...] = m2
            v2_refs[t][...] = v2

    outs = pl.pallas_call(body, name="adamw_small",
                          out_shape=[jax.ShapeDtypeStruct(w.shape, F32) for w in ws] * 3)(*gs, *ws, *ms, *vs)
    return outs[:n], outs[n:2 * n], outs[2 * n:]


_PACK_TILE = 8 * 128


def _pack(arrays):
    rows = []
    for a in arrays:
        flat = a.astype(F32).reshape(-1)
        rows.append(jnp.pad(flat, (0, -flat.shape[0] % _PACK_TILE)).reshape(-1, 128))
    return jnp.concatenate(rows, axis=0)


def _unpack(pack, shapes):
    out, r = [], 0
    for s in shapes:
        n = int(np.prod(s))
        out.append(pack[r:r + -(-n // 128)].reshape(-1)[:n].reshape(s))
        r += -(-n // _PACK_TILE) * 8
    return out


def _pad128(v):
    v = v.reshape(1, -1).astype(F32)
    return jnp.pad(v, ((0, 0), (0, 128 - v.shape[1])))


_WEIGHTS = ["meta", "norm_mix_w", "w_in", "pool_w", "pool_scale", "conv_w", "conv_b", "dt_bias", "a_log", "d_skip",
            "ssm_norm_w", "w_out", "norm_ffn_w", "w_ff1", "w_ff2", "norm_f_w"]
_BIG = ["w_in", "w_out", "w_ff1", "w_ff2"]
_SMALL = [n for n in _WEIGHTS if n not in _BIG]


def kernel(x, meta, norm_mix_w, w_in, pool_w, pool_scale, conv_w, conv_b, dt_bias, a_log, d_skip, ssm_norm_w, w_out, norm_ffn_w, w_ff1, w_ff2, norm_f_w, loss_target, m_meta, m_norm_mix_w, m_w_in, m_pool_w, m_pool_scale, m_conv_w, m_conv_b, m_dt_bias, m_a_log, m_d_skip, m_ssm_norm_w, m_w_out, m_norm_ffn_w, m_w_ff1, m_w_ff2, m_norm_f_w, v_meta, v_norm_mix_w, v_w_in, v_pool_w, v_pool_scale, v_conv_w, v_conv_b, v_dt_bias, v_a_log, v_d_skip, v_ssm_norm_w, v_w_out, v_norm_ffn_w, v_w_ff1, v_w_ff2, v_norm_f_w):
    wts = dict(meta=meta, norm_mix_w=norm_mix_w, w_in=w_in, pool_w=pool_w, pool_scale=pool_scale, conv_w=conv_w,
               conv_b=conv_b, dt_bias=dt_bias, a_log=a_log, d_skip=d_skip, ssm_norm_w=ssm_norm_w, w_out=w_out,
               norm_ffn_w=norm_ffn_w, w_ff1=w_ff1, w_ff2=w_ff2, norm_f_w=norm_f_w)
    mom1 = dict(zip(_WEIGHTS, (m_meta, m_norm_mix_w, m_w_in, m_pool_w, m_pool_scale, m_conv_w, m_conv_b, m_dt_bias,
                               m_a_log, m_d_skip, m_ssm_norm_w, m_w_out, m_norm_ffn_w, m_w_ff1, m_w_ff2, m_norm_f_w)))
    mom2 = dict(zip(_WEIGHTS, (v_meta, v_norm_mix_w, v_w_in, v_pool_w, v_pool_scale, v_conv_w, v_conv_b, v_dt_bias,
                               v_a_log, v_d_skip, v_ssm_norm_w, v_w_out, v_norm_ffn_w, v_w_ff1, v_w_ff2, v_norm_f_w)))
    xi, yi, ci = _coords()
    dev = 4 * xi + 2 * yi + ci
    win_cols = w_in.shape[-1]
    cw_cols = conv_w.shape[-1]

    nb, seq, _ = x.shape
    core = jnp.reshape(ci, (1,)).astype(jnp.int32)
    owners = lambda a: a.reshape((4, 2) + a.shape[1:])

    lead_pack = jnp.zeros((N_META, 512), F32)
    lead_pack = lead_pack.at[:, :128].set(meta).at[:4, 128:128 + cw_cols].set(conv_w[0])
    g_win, g_lead = _weight_gather([_mx(w_in[0]), lead_pack])
    win_full = _assemble_bands(g_win, PROJ_W)
    meta_full = jnp.transpose(g_lead[:, :, :128], (1, 0, 2)).reshape(N_META, D_MODEL)
    cw_full = jnp.transpose(g_lead[:, :4, 128:128 + cw_cols], (1, 0, 2)).reshape(4, D_XBC)

    lead = jnp.concatenate([jnp.zeros((PAD_ROWS, D_MODEL), F32), meta_full] * nb, axis=0)
    x_rows = x.reshape(nb * seq, D_MODEL)
    tgt_rows = loss_target.reshape(nb * seq, D_MODEL)
    dt_bias_p, a_log_p = _pad128(dt_bias), _pad128(a_log)
    d_x = jnp.repeat(d_skip.reshape(1, N_HEADS).astype(F32), HEAD_DIM, axis=1)
    norm_f_row = norm_f_w.reshape(1, D_MODEL)

    hn1, proj = _in_proj(x_rows, lead, norm_mix_w, win_full)
    late_cols = [False, True, False]
    y, ypre, conv_pre, states, (g_wout, wff1_full, g_wff2) = _mixer_fwd(
        proj, cw_full, conv_b, dt_bias_p, a_log_p, d_x, ssm_norm_w, pool_w[0], pool_scale, nb,
        [_mx(w_out[0]), _mx(w_ff1[0]), _mx(w_ff2[0])], late_cols)
    wout_full = g_wout.reshape(D_MIX, D_MODEL)
    wff2_full = g_wff2.reshape(D_FF, D_MODEL)
    loss, gr_nf, gr_nffn, ff, da, hn2, dh1, dh2, dy = _ffn_fwd_bwd(
        x_rows, lead, y, tgt_rows, wout_full, norm_ffn_w, wff1_full, wff2_full, norm_f_row)
    gr_wff2 = _tn_matmul(ff, dh2, "grad_w_ff2", tka=1024, max_slab=2816)
    gr_wff1_t = _tn_matmul(da, hn2, "grad_w_ff1", tka=1024, max_slab=2816)
    gr_wout = _tn_matmul(y, dh1, "grad_w_out", tka=1024, max_slab=2816)

    by_owner = lambda k: [owners(gr_wout[k].reshape(N_DEV, D_MIX // N_DEV, D_MODEL)),
                          owners(gr_wff1_t[k].reshape(N_DEV, D_FF // N_DEV, D_MODEL)),
                          owners(gr_wff2[k].reshape(N_DEV, D_FF // N_DEV, D_MODEL))]
    late_parts = by_owner(0)
    late_got, _ = _grad_exchange_d2d(by_owner(1), "grad_exchange_d2d_late")
    late_sums = [_chip_sum(late_parts[t], late_got[t], core, "chip_sum_late_%d" % t) for t in range(3)]
    (dproj, gr_cw, gr_cb, gr_dtb, gr_alog, gr_d, gr_nw, gr_pw, gr_ps), late_exchanged = _mixer_bwd(
        proj, dy, ypre, conv_pre, states, cw_full, dt_bias_p, a_log_p, d_x, ssm_norm_w, pool_w[0], pool_scale, nb,
        late_sums)

    early = dict(pool_w=gr_pw, pool_scale=gr_ps, conv_w=gr_cw, conv_b=gr_cb, dt_bias=gr_dtb[:, :N_HEADS],
                 a_log=gr_alog[:, :N_HEADS], d_skip=gr_d[:, :N_HEADS], ssm_norm_w=gr_nw, norm_ffn_w=gr_nffn,
                 norm_f_w=gr_nf, loss=loss[0:1, 0:1])
    early_pack = _pack(list(early.values()))
    win_parts = [owners(_tn_matmul_banded(hn1, dproj, "grad_w_in", win_cols, tka=512))]
    win_got, (early_got,) = _grad_exchange_d2d(win_parts, "grad_exchange_d2d_w_in", swapped=[early_pack])
    win_sum = _chip_sum(win_parts[0], win_got[0], core, "chip_sum_w_in")
    early_chip = _sum_two(early_pack, early_got, "chip_sum_small")
    whole = [False, True]
    send_sems, recv_sems, sent, landing, started = _chip_exchange_start([win_sum, early_chip], whole, "w_in_exchange_start")
    gx_rows, gr_nmix, gr_meta = _in_proj_bwd(dproj, x_rows, lead, dh1, norm_mix_w, win_full, started)
    (win_sum, early_chip), (win_landed, early_landed) = _chip_exchange_wait(
        send_sems, recv_sems, sent, landing, whole, gr_nmix, "w_in_exchange_wait")
    parts = dict(w_in=win_landed, w_out=late_exchanged[0], w_ff1=late_exchanged[1], w_ff2=late_exchanged[2])
    my_chip = jnp.reshape(2 * xi + yi, (1,)).astype(jnp.int32)
    early_sum = _sum_chips(early_landed, early_chip, my_chip, "small_sum")

    tail = dict(meta=gr_meta, norm_mix_w=gr_nmix)
    tail_sum = _small_allreduce(_pack(list(tail.values())))
    gs = dict(zip(early, _unpack(early_sum, [a.shape for a in early.values()])))
    gs.update(zip(tail, _unpack(tail_sum, [a.shape for a in tail.values()])))
    gs["meta"] = lax.dynamic_slice_in_dim(gs["meta"], dev * 128, 128, axis=1)
    gs["conv_w"] = lax.dynamic_slice_in_dim(gs["conv_w"], dev * cw_cols, cw_cols, axis=1)

    res = {}
    for n in _BIG:
        shp = wts[n].shape
        args = (parts[n], wts[n][0], mom1[n][0], mom2[n][0], "adamw_" + n)
        if n == "w_ff1":
            outs = _adamw_transposed(*args)
        elif n == "w_in":
            outs = _adamw(*args, own=win_sum, chip=my_chip)
        else:
            outs = _adamw(*args)
        res[n] = [o.reshape(shp) for o in outs]
    as2d = lambda a: a.reshape(-1, a.shape[-1])
    small_g = [as2d(gs[n].reshape(wts[n].shape)) for n in _SMALL]
    small_out = _adamw_small(small_g, *[[as2d(d[n]) for n in _SMALL] for d in (wts, mom1, mom2)])
    for k, n in enumerate(_SMALL):
        res[n] = [o[k].reshape(wts[n].shape) for o in (small_g,) + tuple(small_out)]

    grad_x = gx_rows.reshape(nb, seq, D_MODEL)
    return (gs["loss"][0, 0], grad_x, *[res[n][0] for n in _WEIGHTS], *[res[n][1] for n in _WEIGHTS],
            *[res[n][2] for n in _WEIGHTS], *[res[n][3] for n in _WEIGHTS])
```

```python
import numpy as np
import jax
import jax.numpy as jnp
from jax import lax
from jax.experimental import pallas as pl
from jax.experimental.pallas import tpu as pltpu

F32 = jnp.float32
MXU_DTYPE = jnp.bfloat16

D_MODEL = 1024
D_POOL = 512
D_SSM = 1536
D_XBC = 2560
N_HEADS = 24
HEAD_DIM = 64
N_GROUPS = 4
GROUP_CH = D_SSM // N_GROUPS
D_STATE = 128
CHUNK = 128
N_META = 16
LEAD = CHUNK
PAD_ROWS = LEAD - N_META
ROW_TILE = 2 * CHUNK
D_MIX = D_POOL + D_SSM
D_FF = 4096
PROJ_W = 4736
OFF_Z = D_POOL
OFF_X = D_POOL + D_SSM
OFF_DT = OFF_X + D_XBC
D_IN_PROJ = OFF_DT + N_HEADS
POOL_WINDOWS = (2, 4, 8, 16)
HALO = 16
EPS = 1e-5
N_DEV = 8

ADAM_LR, ADAM_B1, ADAM_B2, ADAM_EPS, ADAM_WD, ADAM_STEP = 0.001, 0.9, 0.999, 1e-08, 0.01, 10

VMEM_LIMIT = 60 * 1024 * 1024


def _mx(a):
    return a.astype(MXU_DTYPE)


def _dot(a, b):
    return jnp.dot(_mx(a), _mx(b), preferred_element_type=F32)


def _dot_nt(a, b):
    return lax.dot_general(_mx(a), _mx(b), (((1,), (1,)), ((), ())), preferred_element_type=F32)


def _dot_tn(a, b):
    return lax.dot_general(_mx(a), _mx(b), (((0,), (0,)), ((), ())), preferred_element_type=F32)


def _split3(x):
    hi = x.astype(MXU_DTYPE)
    r = x - hi.astype(F32)
    mid = r.astype(MXU_DTYPE)
    lo = (r - mid.astype(F32)).astype(MXU_DTYPE)
    return hi, mid, lo


def _exact_l(c, x):
    hi, mid, lo = _split3(x)
    f = lambda p: jnp.dot(c, p, preferred_element_type=F32)
    return f(hi) + f(mid) + f(lo)


def _exact_r(x, c):
    hi, mid, lo = x if isinstance(x, tuple) else _split3(x)
    f = lambda p: jnp.dot(p, c, preferred_element_type=F32)
    return f(hi) + f(mid) + f(lo)


def _contract(x, c):
    hi = x.astype(MXU_DTYPE)
    lo = (x - hi.astype(F32)).astype(MXU_DTYPE)
    return jnp.dot(hi, c, preferred_element_type=F32) + jnp.dot(lo, c, preferred_element_type=F32)


def _sigmoid(x):
    return jax.nn.sigmoid(x)


def _softplus(x):
    return jnp.maximum(x, 0.0) + jnp.log1p(jnp.exp(-jnp.abs(x)))


def _silu_and_grad(x):
    s = _sigmoid(x)
    y = x * s
    return y, s + y * (1.0 - s)


def _shift_up(ext, s):
    if s == 0:
        return ext[:CHUNK, :]
    return pltpu.roll(ext, ext.shape[0] - s, 0)[:CHUNK, :]


def _by_pool_group(lane, a2, a4, a8, a16):
    return jnp.where(lane < 128, a2, jnp.where(lane < 256, a4, jnp.where(lane < 384, a8, a16)))


def _pool_inv_count(chunk_idx):
    row = lax.broadcasted_iota(jnp.int32, (CHUNK, D_POOL), 0)
    lane = lax.broadcasted_iota(jnp.int32, (CHUNK, D_POOL), 1)
    pos1 = jnp.maximum(chunk_idx * CHUNK + row - (PAD_ROWS - 1), 1)
    w = _by_pool_group(lane, 2, 4, 8, 16)
    return 1.0 / jnp.minimum(pos1, w).astype(F32), lane


def _pool_window_sums(u_ext, lane):
    s2 = u_ext + pltpu.roll(u_ext, 1, 0)
    s4 = s2 + pltpu.roll(s2, 2, 0)
    s8 = s4 + pltpu.roll(s4, 4, 0)
    s16 = s8 + pltpu.roll(s8, 8, 0)
    return _by_pool_group(lane, s2[HALO:], s4[HALO:], s8[HALO:], s16[HALO:])


def _pool_window_sums_ahead(q_ext, lane):
    n = q_ext.shape[0]
    r2 = q_ext + pltpu.roll(q_ext, n - 1, 0)
    r4 = r2 + pltpu.roll(r2, n - 2, 0)
    r8 = r4 + pltpu.roll(r4, n - 4, 0)
    r16 = r8 + pltpu.roll(r8, n - 8, 0)
    return _by_pool_group(lane, r2[:CHUNK], r4[:CHUNK], r8[:CHUNK], r16[:CHUNK])


def _conv_pre(ext, xbc, cw, cb):
    s1 = pltpu.roll(ext, 1, 0)
    near = cw[3:4, :] * xbc + cw[2:3, :] * s1[HALO:, :]
    far = cw[1:2, :] * ext + cw[0:1, :] * s1
    return cb + near + pltpu.roll(far, 2, 0)[HALO:, :]


def _dt_and_cumsum(dtr, dt_bias, a_log, valid, tril):
    lane = lax.broadcasted_iota(jnp.int32, (CHUNK, 128), 1)
    head = lane < N_HEADS
    pre = dtr + dt_bias
    dt = jnp.where(valid & head, _softplus(pre), 0.0)
    a_row = jnp.where(head[0:1, :], -jnp.exp(a_log), 0.0)
    a_col = _exact_l(tril, dt * a_row)
    return dt, a_row, a_col, pre, head


def _decay(a_col, a_row_t, h, causal):
    seg = a_col[:, h:h + 1] - a_row_t[h:h + 1, :]
    return jnp.where(causal, jnp.exp(jnp.minimum(seg, 0.0)), 0.0)


def _ssd_chunk_fwd(xs, bm, cm, dt, a_col, s_prev, d_x, e_mat, et_f32):
    lane = lax.broadcasted_iota(jnp.int32, (CHUNK, 128), 1)
    rowi = lax.broadcasted_iota(jnp.int32, (CHUNK, CHUNK), 0)
    coli = lax.broadcasted_iota(jnp.int32, (CHUNK, CHUNK), 1)
    causal = rowi >= coli
    a_row_t = a_col.T
    ax = _exact_r(a_col, e_mat)
    dtx = _exact_r(dt, e_mat)
    xdt = xs * dtx
    ax_last = ax[CHUNK - 1:CHUNK, :]
    e_a = jnp.exp(ax)
    w_end = xdt * jnp.exp(ax_last - ax)
    cd_col = jnp.exp(jnp.sum(et_f32 * a_col[CHUNK - 1:CHUNK, :], axis=1, keepdims=True))
    ys, s_new = [], []
    for g in range(N_GROUPS):
        gs = slice(g * GROUP_CH, (g + 1) * GROUP_CH)
        bg = bm[:, g * D_STATE:(g + 1) * D_STATE]
        cg = cm[:, g * D_STATE:(g + 1) * D_STATE]
        sg = s_prev[gs, :]
        cb = _dot_nt(cg, bg)
        y_off = _dot_nt(cg, sg) * e_a[:, gs]
        s_new.append(sg * cd_col[gs, :] + _dot_tn(w_end[:, gs], bg))
        for pr in range(3):
            c0 = g * GROUP_CH + pr * 128
            xdt_p = xdt[:, c0:c0 + 128]
            h0 = g * 6 + pr * 2
            y0 = _dot(cb * _decay(a_col, a_row_t, h0, causal), xdt_p)
            y1 = _dot(cb * _decay(a_col, a_row_t, h0 + 1, causal), xdt_p)
            ys.append(jnp.where(lane < HEAD_DIM, y0, y1) + y_off[:, pr * 128:(pr + 1) * 128])
    y = jnp.concatenate(ys, axis=1) + d_x * xs
    return y, jnp.concatenate(s_new, axis=0)


def _ssd_chunk_bwd(xs, bm, cm, dt, a_row, a_col, s_prev, ds_new, dy, d_x, e_mat, et_mat, et_f32, triu):
    lane = lax.broadcasted_iota(jnp.int32, (CHUNK, 128), 1)
    sub = lax.broadcasted_iota(jnp.int32, (CHUNK, 128), 0)
    rowi = lax.broadcasted_iota(jnp.int32, (CHUNK, CHUNK), 0)
    coli = lax.broadcasted_iota(jnp.int32, (CHUNK, CHUNK), 1)
    causal = rowi >= coli
    a_row_t = a_col.T
    a_last = a_col[CHUNK - 1:CHUNK, :]
    a_split, dt_split = _split3(a_col), _split3(dt)
    sub8 = lax.broadcasted_iota(jnp.int32, (8, GROUP_CH), 0)

    dxs, dbs, dcs, dsp = [], [], [], []
    zcol = jnp.zeros((CHUNK, 128), F32)
    zrows = []
    da_col = jnp.zeros((CHUNK, 128), F32)
    ddt = jnp.zeros((CHUNK, 128), F32)
    head_sums = jnp.zeros((8, 128), F32)
    q_row = jnp.zeros((1, 128), F32)
    for g in range(N_GROUPS):
        gs = slice(g * GROUP_CH, (g + 1) * GROUP_CH)
        e_g, et_g = e_mat[:, gs], et_mat[gs, :]
        xs_g, dy_g = xs[:, gs], dy[:, gs]
        ax = _exact_r(a_split, e_g)
        dtx = _exact_r(dt_split, e_g)
        xdt = xs_g * dtx
        dte = jnp.exp(ax[CHUNK - 1:CHUNK, :] - ax)
        w_end = xdt * dte
        cd_col = jnp.exp(jnp.sum(et_f32[gs, :] * a_last, axis=1, keepdims=True))
        dye = dy_g * jnp.exp(ax)
        bg = bm[:, g * D_STATE:(g + 1) * D_STATE]
        cg = cm[:, g * D_STATE:(g + 1) * D_STATE]
        sg = s_prev[gs, :]
        dsg = ds_new[gs, :]
        cb = _dot_nt(cg, bg)
        cs = _dot_nt(cg, sg)
        dcg = _dot(dye, sg)
        dsp.append(dsg * cd_col + _dot_tn(dye, cg))
        dwg = _dot_nt(bg, dsg)
        dbg = _dot(w_end, dsg)
        ww = dwg * w_end
        t1 = jnp.sum(dsg * sg, axis=1, keepdims=True) * cd_col
        dcb = jnp.zeros((CHUNK, CHUNK), F32)
        pairs = []
        for pr in range(3):
            ps = slice(pr * 128, (pr + 1) * 128)
            xdt_p, dy_p = xdt[:, ps], dy_g[:, ps]
            acc = None
            for half in range(2):
                h = g * 6 + pr * 2 + half
                ld = _decay(a_col, a_row_t, h, causal)
                gm = cb * ld
                dym = jnp.where((lane < HEAD_DIM) if half == 0 else (lane >= HEAD_DIM), dy_p, 0.0)
                dg = _dot_nt(dym, xdt_p)
                dseg = dg * gm
                dcb = dcb + dg * ld
                t = _dot_tn(gm, dym)
                acc = t if acc is None else acc + t
                zcol = jnp.where(lane == h, jnp.sum(dseg, axis=1, keepdims=True), zcol)
                zrows.append(jnp.sum(dseg, axis=0, keepdims=True))
            pairs.append(acc)
        dxdt = dwg * dte + jnp.concatenate(pairs, axis=1)
        dcs.append(dcg + _dot(dcb, bg))
        dbs.append(dbg + _dot_tn(dcb, cg))
        dxs.append(dxdt * dtx + d_x[:, gs] * dy_g)
        da_col = da_col + _contract(dye * cs - ww, et_g)
        ddt = ddt + _contract(dxdt * xs_g, et_g)
        col_sums = jnp.where(sub8 == 0, jnp.sum(dy_g * xs_g, axis=0, keepdims=True),
                             jnp.where(sub8 == 1, jnp.sum(ww, axis=0, keepdims=True), 0.0))
        head_sums = head_sums + _exact_r(col_sums, et_g)
        q_row = q_row + jnp.sum(et_f32[gs, :] * t1, axis=0, keepdims=True)

    dd = head_sums[0:1, :]
    q_row = q_row + head_sums[1:2, :]
    zrow = jnp.concatenate(zrows + [jnp.zeros((128 - N_HEADS, CHUNK), F32)], axis=0)
    da_col = da_col + zcol - zrow.T + jnp.where(sub == CHUNK - 1, q_row, 0.0)
    rc = _exact_l(triu, da_col)
    ddt = ddt + rc * a_row
    da = jnp.sum(rc * dt, axis=0, keepdims=True)
    return (jnp.concatenate(dxs, axis=1), jnp.concatenate(dbs, axis=1), jnp.concatenate(dcs, axis=1), ddt, da, dd,
            jnp.concatenate(dsp, axis=0))


_SSD_CONSTANT_SHAPES = dict(e=((128, D_SSM), MXU_DTYPE), et=((D_SSM, 128), MXU_DTYPE), et_f32=((D_SSM, 128), F32),
                            tril=((CHUNK, CHUNK), MXU_DTYPE), triu=((CHUNK, CHUNK), MXU_DTYPE))


def _ssd_constant_scratch(names):
    return [pltpu.VMEM(*_SSD_CONSTANT_SHAPES[n]) for n in names]


def _fill_ssd_constants(**refs):
    iota = lambda shape, d: lax.broadcasted_iota(jnp.int32, shape, d)
    shift = HEAD_DIM.bit_length() - 1
    marks = dict(
        e=lambda: iota((128, D_SSM), 0) == (iota((128, D_SSM), 1) >> shift),
        et=lambda: iota((D_SSM, 128), 1) == (iota((D_SSM, 128), 0) >> shift),
        et_f32=lambda: iota((D_SSM, 128), 1) == (iota((D_SSM, 128), 0) >> shift),
        tril=lambda: iota((CHUNK, CHUNK), 1) <= iota((CHUNK, CHUNK), 0),
        triu=lambda: iota((CHUNK, CHUNK), 1) >= iota((CHUNK, CHUNK), 0))
    for name, ref in refs.items():
        ref[...] = jnp.where(marks[name](), 1.0, 0.0).astype(ref.dtype)


def _row_views(ref, widths):
    views, off = [], 0
    for w in widths:
        views.append(ref.at[:, off:off + w])
        off += w
    return views


def _full(shape):
    nd = len(shape)
    return pl.BlockSpec(shape, lambda *_: (0,) * nd)


def _params(*sem):
    return pltpu.CompilerParams(dimension_semantics=sem, vmem_limit_bytes=VMEM_LIMIT)


def _token_tiles(width, n_tok_tiles):
    return pl.BlockSpec((ROW_TILE, width), lambda i: (jnp.minimum(i, n_tok_tiles - 1), 0))


def _in_proj(x, lead, w1, win):
    nt = x.shape[0] // ROW_TILE
    m = x.shape[0] + ROW_TILE
    tm = ROW_TILE

    def body(x_ref, lead_ref, w1_ref, win_hbm, hn_ref, proj_ref, win_v, sem):
        i = pl.program_id(0)

        @pl.when(i == 0)
        def _():
            cp = pltpu.make_async_copy(win_hbm, win_v, sem)
            cp.start()
            cp.wait()

        x = jnp.where(i == nt, lead_ref[...], x_ref[...])
        r = lax.rsqrt(jnp.mean(x * x, axis=-1, keepdims=True) + EPS)
        hn = _mx(x * r * w1_ref[...])
        hn_ref[...] = hn
        for j in range(0, PROJ_W, 512):
            w = min(512, PROJ_W - j)
            proj_ref[:, j:j + w] = jnp.dot(hn, win_v[:, j:j + w], preferred_element_type=F32)

    return pl.pallas_call(
        body, grid=(m // tm,), name="in_proj",
        in_specs=[_token_tiles(D_MODEL, nt), _full((ROW_TILE, D_MODEL)), _full((1, D_MODEL)),
                  pl.BlockSpec(memory_space=pl.ANY)],
        out_specs=[pl.BlockSpec((tm, D_MODEL), lambda i: (i, 0)), pl.BlockSpec((tm, PROJ_W), lambda i: (i, 0))],
        out_shape=[jax.ShapeDtypeStruct((m, D_MODEL), MXU_DTYPE), jax.ShapeDtypeStruct((m, PROJ_W), F32)],
        scratch_shapes=[pltpu.VMEM((D_MODEL, PROJ_W), MXU_DTYPE), pltpu.SemaphoreType.DMA],
        compiler_params=_params("arbitrary"),
    )(x, lead, w1, win)


def _ffn_fwd_bwd(x, lead, y, tgt, wout, w2n, wff1, wff2, wfn):
    nt = x.shape[0] // ROW_TILE
    m = x.shape[0] + ROW_TILE
    tm = ROW_TILE
    nj = D_FF // 1024

    def body(x_ref, lead_ref, y_ref, tgt_ref, w2n_ref, wfn_ref, wout_hbm, wff1_hbm, wff2_hbm,
             loss_ref, gwf_ref, gw2_ref, ff_ref, da_ref, hn2_ref, dh1_ref, dh2_ref, dy_ref,
             wout_v, wff1_v, wff2_v, a_s, sems):
        i = pl.program_id(0)
        hp = jnp.where(i == nt, lead_ref[...], x_ref[...])

        @pl.when(i == 0)
        def _():
            cps = [pltpu.make_async_copy(s, d, sems.at[k])
                   for k, (s, d) in enumerate(((wout_hbm, wout_v), (wff1_hbm, wff1_v), (wff2_hbm, wff2_v)))]
            for cp in cps:
                cp.start()
            for cp in cps:
                cp.wait()
            loss_ref[...] = jnp.zeros_like(loss_ref)
            gwf_ref[...] = jnp.zeros_like(gwf_ref)
            gw2_ref[...] = jnp.zeros_like(gw2_ref)

        h1 = hp + jnp.dot(y_ref[...], wout_v[...], preferred_element_type=F32)
        r2 = lax.rsqrt(jnp.mean(h1 * h1, axis=-1, keepdims=True) + EPS)
        n2 = h1 * r2
        w2n_row = w2n_ref[...]
        hn2 = _mx(n2 * w2n_row)
        hn2_ref[...] = hn2
        h2 = h1
        for j in range(nj):
            js = slice(j * 1024, (j + 1) * 1024)
            a = jnp.dot(hn2, wff1_v[:, js], preferred_element_type=F32)
            a_s[:, js] = a
            ra = jnp.maximum(a, 0.0)
            ff = _mx(ra * ra)
            ff_ref[:, js] = ff
            h2 = h2 + jnp.dot(ff, wff2_v[js, :], preferred_element_type=F32)

        r3 = lax.rsqrt(jnp.mean(h2 * h2, axis=-1, keepdims=True) + EPS)
        n3 = h2 * r3
        wf_row = wfn_ref[...]
        err = n3 * wf_row - tgt_ref[...]
        tokf = (i < nt).astype(F32)
        loss_ref[...] += 0.5 * jnp.sum(jnp.mean(err * err, axis=-1, keepdims=True) * tokf)
        dout = err * (tokf / D_MODEL)
        gwf_ref[...] += jnp.sum(dout * n3, axis=0, keepdims=True)
        dn3 = dout * wf_row
        dh2 = r3 * (dn3 - n3 * jnp.mean(dn3 * n3, axis=-1, keepdims=True))
        dh2m = _mx(dh2)
        dh2_ref[...] = dh2m

        dhn2 = jnp.zeros((tm, D_MODEL), F32)
        for j in range(nj):
            js = slice(j * 1024, (j + 1) * 1024)
            dff = lax.dot_general(dh2m, wff2_v[js, :], (((1,), (1,)), ((), ())), preferred_element_type=F32)
            da = _mx(dff * (2.0 * jnp.maximum(a_s[:, js], 0.0)))
            da_ref[:, js] = da
            dhn2 = dhn2 + lax.dot_general(da, wff1_v[:, js], (((1,), (1,)), ((), ())), preferred_element_type=F32)
        gw2_ref[...] += jnp.sum(dhn2 * n2, axis=0, keepdims=True)
        dn2 = dhn2 * w2n_row
        dh1 = dh2 + r2 * (dn2 - n2 * jnp.mean(dn2 * n2, axis=-1, keepdims=True))
        dh1_ref[...] = dh1
        dy_ref[...] = lax.dot_general(_mx(dh1), wout_v[...], (((1,), (1,)), ((), ())), preferred_element_type=F32)

    rows = lambda w: pl.BlockSpec((tm, w), lambda i: (i, 0))
    hbm = pl.BlockSpec(memory_space=pl.ANY)
    return pl.pallas_call(
        body, grid=(m // tm,), name="ffn_fwd_bwd",
        in_specs=[_token_tiles(D_MODEL, nt), _full((ROW_TILE, D_MODEL)), rows(D_MIX), _token_tiles(D_MODEL, nt),
                  _full((1, D_MODEL)), _full((1, D_MODEL)), hbm, hbm, hbm],
        out_specs=[_full((1, 128)), _full((1, D_MODEL)), _full((1, D_MODEL)), rows(D_FF), rows(D_FF), rows(D_MODEL),
                   rows(D_MODEL), rows(D_MODEL), rows(D_MIX)],
        out_shape=[jax.ShapeDtypeStruct((1, 128), F32), jax.ShapeDtypeStruct((1, D_MODEL), F32),
                   jax.ShapeDtypeStruct((1, D_MODEL), F32), jax.ShapeDtypeStruct((m, D_FF), MXU_DTYPE),
                   jax.ShapeDtypeStruct((m, D_FF), MXU_DTYPE), jax.ShapeDtypeStruct((m, D_MODEL), MXU_DTYPE),
                   jax.ShapeDtypeStruct((m, D_MODEL), F32), jax.ShapeDtypeStruct((m, D_MODEL), MXU_DTYPE),
                   jax.ShapeDtypeStruct((m, D_MIX), F32)],
        scratch_shapes=[pltpu.VMEM((D_MIX, D_MODEL), MXU_DTYPE), pltpu.VMEM((D_MODEL, D_FF), MXU_DTYPE),
                        pltpu.VMEM((D_FF, D_MODEL), MXU_DTYPE), pltpu.VMEM((tm, D_FF), F32),
                        pltpu.SemaphoreType.DMA((3,))],
        compiler_params=_params("arbitrary"),
    )(x, lead, y, tgt, w2n, wfn, wout, wff1, wff2)


def _in_proj_bwd(dproj, x, lead, dh1, w1, win, after):
    nt = x.shape[0] // ROW_TILE
    m = x.shape[0] + ROW_TILE
    tm = ROW_TILE

    def body(dp_ref, x_ref, lead_ref, dh1_ref, w1_ref, win_hbm, after_ref, gx_ref, gw1_ref, gmeta_ref, win_v, sem):
        i = pl.program_id(0)

        @pl.when(i == 0)
        def _():
            cp = pltpu.make_async_copy(win_hbm, win_v, sem)
            cp.start()
            cp.wait()
            gw1_ref[...] = jnp.zeros_like(gw1_ref)
            gmeta_ref[...] = jnp.zeros_like(gmeta_ref)

        dhn = lax.dot_general(dp_ref[...], win_v[...], (((1,), (1,)), ((), ())), preferred_element_type=F32)
        x = jnp.where(i == nt, lead_ref[...], x_ref[...])
        r = lax.rsqrt(jnp.mean(x * x, axis=-1, keepdims=True) + EPS)
        n = x * r
        gw1_ref[...] += jnp.sum(dhn * n, axis=0, keepdims=True)
        dn = dhn * w1_ref[...]
        dh0 = dh1_ref[...] + r * (dn - n * jnp.mean(dn * n, axis=-1, keepdims=True))

        @pl.when(i < nt)
        def _():
            gx_ref[...] = dh0

        @pl.when(i == nt)
        def _():
            gmeta_ref[...] = dh0[PAD_ROWS:LEAD, :] + dh0[LEAD + PAD_ROWS:2 * LEAD, :]

    rows = lambda w: pl.BlockSpec((tm, w), lambda i: (i, 0))
    hbm = pl.BlockSpec(memory_space=pl.ANY)
    return pl.pallas_call(
        body, grid=(m // tm,), name="in_proj_bwd",
        in_specs=[rows(PROJ_W), _token_tiles(D_MODEL, nt), _full((ROW_TILE, D_MODEL)), rows(D_MODEL),
                  _full((1, D_MODEL)), hbm, hbm],
        out_specs=[_token_tiles(D_MODEL, nt), _full((1, D_MODEL)), _full((N_META, D_MODEL))],
        out_shape=[jax.ShapeDtypeStruct(x.shape, F32), jax.ShapeDtypeStruct((1, D_MODEL), F32),
                   jax.ShapeDtypeStruct((N_META, D_MODEL), F32)],
        scratch_shapes=[pltpu.VMEM((D_MODEL, PROJ_W), MXU_DTYPE), pltpu.SemaphoreType.DMA],
        compiler_params=_params("arbitrary"),
    )(dproj, x, lead, dh1, w1, win, after)


MXU_DEPTH = 256


def _row_slab(m, cap):
    return max(k for k in range(MXU_DEPTH, cap + 1, MXU_DEPTH) if m % k == 0)


def _tn_matmul(a, b, name, tka, max_slab=768, tn=512):
    m, ka = a.shape
    nb = b.shape[1]
    tkm = _row_slab(m, max_slab)
    n_steps = m // tkm

    def body(a_ref, b_ref, o_ref, omx_ref):
        k = pl.program_id(1)

        @pl.when(k == 0)
        def _():
            o_ref[...] = jnp.zeros_like(o_ref)

        at = _mx(a_ref[...])
        for j in range(0, nb, tn):
            w = min(tn, nb - j)
            o_ref[:, j:j + w] += lax.dot_general(at, _mx(b_ref[:, j:j + w]), (((0,), (0,)), ((), ())),
                                                 preferred_element_type=F32)

        @pl.when(k == n_steps - 1)
        def _():
            omx_ref[...] = _mx(o_ref[...])

    out = pl.BlockSpec((tka, nb), lambda i, k: (i, 0))
    return pl.pallas_call(
        body, grid=(ka // tka, n_steps), name=name,
        in_specs=[pl.BlockSpec((tkm, tka), lambda i, k: (k, i)), pl.BlockSpec((tkm, nb), lambda i, k: (k, 0))],
        out_specs=[out, out],
        out_shape=[jax.ShapeDtypeStruct((ka, nb), F32), jax.ShapeDtypeStruct((ka, nb), MXU_DTYPE)],
        compiler_params=_params("arbitrary", "arbitrary"),
    )(a, b)


def _tn_matmul_banded(a, b, core, name, band, tka, tn=512):
    m, ka = a.shape
    nb = b.shape[1]
    tkm = _row_slab(m, 768)
    n_steps = m // tkm

    def body(core_ref, a_ref, b_ref, own_ref, sib_ref, acc):
        k = pl.program_id(1)

        @pl.when(k == 0)
        def _():
            acc[...] = jnp.zeros_like(acc)

        at = _mx(a_ref[...])
        for j in range(0, nb, tn):
            w = min(tn, nb - j)
            acc[:, j:j + w] += lax.dot_general(at, _mx(b_ref[:, j:j + w]), (((0,), (0,)), ((), ())),
                                               preferred_element_type=F32)

        for c in range(2):
            @pl.when((k == n_steps - 1) & (core_ref[0] == c))
            def _():
                for chip in range(4):
                    mine, other = 2 * chip + c, 2 * chip + 1 - c
                    own_ref[chip] = acc[:, mine * band:(mine + 1) * band]
                    sib_ref[chip] = _mx(acc[:, other * band:(other + 1) * band])

    out = pl.BlockSpec((4, tka, band), lambda i, k, c: (0, i, 0))
    grid_spec = pltpu.PrefetchScalarGridSpec(
        num_scalar_prefetch=1, grid=(ka // tka, n_steps),
        in_specs=[pl.BlockSpec((tkm, tka), lambda i, k, c: (k, i)), pl.BlockSpec((tkm, nb), lambda i, k, c: (k, 0))],
        out_specs=[out, out], scratch_shapes=[pltpu.VMEM((tka, nb), F32)])
    return pl.pallas_call(
        body, grid_spec=grid_spec, name=name,
        out_shape=[jax.ShapeDtypeStruct((4, ka, band), F32), jax.ShapeDtypeStruct((4, ka, band), MXU_DTYPE)],
        compiler_params=_params("arbitrary", "arbitrary"),
    )(core, a, b)


def _assemble_bands(g, width):
    n, rows, band = g.shape
    tr = 256

    def body(g_ref, o_ref):
        parts = [g_ref[j] for j in range(n)] + [jnp.zeros((tr, width - n * band), g.dtype)]
        o_ref[...] = jnp.concatenate(parts, axis=1)

    return pl.pallas_call(
        body, grid=(rows // tr,), name="assemble_w_in",
        in_specs=[pl.BlockSpec((n, tr, band), lambda i: (0, i, 0))],
        out_specs=pl.BlockSpec((tr, width), lambda i: (i, 0)),
        out_shape=jax.ShapeDtypeStruct((rows, width), g.dtype),
        compiler_params=_params("arbitrary"),
    )(g)


def _chunk_block(b, c, nb, nc):
    return jnp.where(c == 0, nb * (nc - 1) + b, b * (nc - 1) + c - 1)

def _mixer_fwd(proj, cw, cb, dt_bias, a_log, d_x, nw, pool_w, pool_scale, nb, shards, by_cols):
    m = proj.shape[0]
    nc = m // nb // CHUNK
    ns = len(shards)
    row_widths = [D_XBC, 128, 128, D_SSM, D_SSM, D_POOL]
    rows = jnp.concatenate([cb, dt_bias, a_log, d_x, nw, pool_scale], axis=1)
    pairs = (nc - 1) // 2
    n_steps = nb * pairs + 1

    def block_of(step):
        return jnp.where(step == 0, nb * pairs, step - 1)

    def chunks_of(step):
        lead = step == 0
        b = jnp.maximum(step - 1, 0) // pairs
        j = jnp.maximum(step - 1, 0) - b * pairs
        return [(jnp.where(lead, 0, b), jnp.where(lead, 0, 2 * j + 1), 0),
                (jnp.where(lead, 1, b), jnp.where(lead, 0, 2 * j + 2), CHUNK)]

    def body(p_ref, cw_ref, rows_ref, pw_ref, *rest):
        shard_refs, (y_ref, ypre_ref, pre_ref, st_ref) = rest[:ns], rest[ns:ns + 4]
        gathered_refs, carries, constants_refs = rest[ns + 4:2 * ns + 4], rest[2 * ns + 4:2 * ns + 7], rest[2 * ns + 7:2 * ns + 10]
        gather = _Gather(shard_refs, gathered_refs, by_cols, *rest[2 * ns + 10:])
        step = pl.program_id(0)

        @pl.when(step == 0)
        def _():
            gather.start()
            _fill_ssd_constants(**dict(zip(["e", "et_f32", "tril"], constants_refs)))

        @pl.when(step == n_steps // 2)
        def _():
            gather.forward()

        shared = (cw_ref, pw_ref, _row_views(rows_ref, row_widths), constants_refs)
        for b, c, row0 in chunks_of(step):
            at = lambda ref: ref.at[row0:row0 + CHUNK]
            one_chunk(c, at(p_ref), at(y_ref), at(ypre_ref), at(pre_ref), st_ref.at[row0 // CHUNK], *shared,
                      [carry.at[b] for carry in carries])

        @pl.when(step == n_steps - 1)
        def _():
            gather.finish()

    def one_chunk(c, p_ref, y_ref, ypre_ref, pre_ref, st_ref, cw_ref, pw_ref, row_refs, constants_refs, carries):
        cb_ref, dtb_ref, alog_ref, dx_ref, nw_ref, ps_ref = row_refs
        e_ref, et_ref, tril_ref = constants_refs
        xtail, utail, state = carries

        @pl.when(c == 0)
        def _():
            xtail[...] = jnp.zeros_like(xtail)
            utail[...] = jnp.zeros_like(utail)
            state[...] = jnp.zeros_like(state)

        valid = (c > 0) | (lax.broadcasted_iota(jnp.int32, (CHUNK, 1), 0) >= PAD_ROWS)

        u = p_ref[:, 0:D_POOL]
        inv_cnt, lane = _pool_inv_count(c)
        win = _pool_window_sums(jnp.concatenate([utail[...], u], axis=0), lane)
        utail[...] = u[CHUNK - HALO:, :]
        pooled = win * inv_cnt - u
        mixed = jnp.concatenate(
            [_dot(pooled[:, g * 128:(g + 1) * 128], pw_ref[g]) for g in range(len(POOL_WINDOWS))], axis=1)
        y_ref[:, 0:D_POOL] = _mx(mixed * ps_ref[...])

        xbc = p_ref[:, OFF_X:OFF_X + D_XBC]
        pre = _conv_pre(jnp.concatenate([xtail[...], xbc], axis=0), xbc, cw_ref[...], cb_ref[...])
        xtail[...] = xbc[CHUNK - HALO:, :]
        pre_ref[...] = pre
        xc = pre * _sigmoid(pre)
        dt, _, a_col, _, _ = _dt_and_cumsum(p_ref[:, OFF_DT:OFF_DT + 128], dtb_ref[...], alog_ref[...], valid,
                                            tril_ref[...])
        s_prev = state[...]
        st_ref[...] = s_prev
        yp, s_new = _ssd_chunk_fwd(xc[:, 0:D_SSM], xc[:, D_SSM:D_SSM + 512], xc[:, D_SSM + 512:], dt, a_col, s_prev,
                                   dx_ref[...], e_ref[...], et_ref[...])
        state[...] = s_new
        ypre_ref[...] = yp
        z = p_ref[:, OFF_Z:OFF_Z + D_SSM]
        yz = yp * (z * _sigmoid(z))
        outs = []
        for g in range(N_GROUPS):
            gs = slice(g * GROUP_CH, (g + 1) * GROUP_CH)
            r = lax.rsqrt(jnp.mean(yz[:, gs] * yz[:, gs], axis=-1, keepdims=True) + EPS)
            outs.append(yz[:, gs] * r)
        y_ref[:, D_POOL:] = _mx(jnp.concatenate(outs, axis=1) * nw_ref[...])

    blk = lambda w: pl.BlockSpec((2 * CHUNK, w), lambda s: (block_of(s), 0))
    hbm = pl.BlockSpec(memory_space=pl.ANY)
    outs = pl.pallas_call(
        body, grid=(n_steps,), name="mixer_fwd",
        in_specs=[blk(PROJ_W), _full((4, D_XBC)), _full((1, sum(row_widths))), _full((4, 128, 128))] + [hbm] * ns,
        out_specs=[blk(D_MIX), blk(D_SSM), blk(D_XBC),
                   pl.BlockSpec((2, D_SSM, D_STATE), lambda s: (block_of(s), 0, 0))] + [hbm] * ns,
        out_shape=[jax.ShapeDtypeStruct((m, D_MIX), MXU_DTYPE), jax.ShapeDtypeStruct((m, D_SSM), F32),
                   jax.ShapeDtypeStruct((m, D_XBC), F32), jax.ShapeDtypeStruct((m // CHUNK, D_SSM, D_STATE), F32)]
        + _Gather.out_shapes(shards, by_cols),
        scratch_shapes=[pltpu.VMEM((nb, HALO, D_XBC), F32), pltpu.VMEM((nb, HALO, D_POOL), F32),
                        pltpu.VMEM((nb, D_SSM, D_STATE), F32)] + _ssd_constant_scratch(["e", "et_f32", "tril"])
        + _Gather.scratch(ns),
        compiler_params=_params("arbitrary"),
    )(proj, cw, rows, pool_w, *shards)
    return outs[0], outs[1], outs[2], outs[3], outs[4:]


def _mixer_bwd(proj, dy, ypre, conv_pre, states, cw, dt_bias, a_log, d_x, nw, pool_w, pool_scale, nb, chip_sums):
    m = proj.shape[0]
    nc = m // nb // CHUNK
    hb = CHUNK // HALO
    ns = len(chip_sums)
    row_widths = [128, 128, D_SSM, D_SSM, D_POOL]
    rows = jnp.concatenate([dt_bias, a_log, d_x, nw, pool_scale], axis=1)
    grad_row_widths = [D_XBC, 128, 128, 128, D_SSM, D_POOL]
    constants = ["e", "et", "et_f32", "tril", "triu"]

    pairs = (nc - 1) // 2
    n_steps = nb * pairs + 1

    def block_of(step):
        b = jnp.minimum(step // pairs, nb - 1)
        j = pairs - 1 - (step - b * pairs)
        return jnp.where(step == n_steps - 1, nb * pairs, b * pairs + j)

    def chunks_of(step):
        lead = step == n_steps - 1
        b = jnp.minimum(step // pairs, nb - 1)
        j = pairs - 1 - (step - b * pairs)
        return [(jnp.where(lead, 0, b), jnp.where(lead, 0, 2 * j + 2), jnp.where(lead, 0, CHUNK)),
                (jnp.where(lead, 1, b), jnp.where(lead, 0, 2 * j + 1), jnp.where(lead, CHUNK, 0))]

    def body(p_ref, halo_ref, dy_ref, ypre_ref, pre_ref, st_ref, cw_ref, rows_ref, pw_ref, *rest):
        cs_refs = rest[:ns]
        dp_ref, gcw_ref, grows_ref, gpw_ref = rest[ns:ns + 4]
        part_refs, carries, constants_refs = rest[ns + 4:2 * ns + 4], rest[2 * ns + 4:2 * ns + 7], rest[2 * ns + 7:2 * ns + 12]
        exchange = _ChipExchange(cs_refs, part_refs, [False] * ns, *rest[2 * ns + 12:])
        step = pl.program_id(0)

        @pl.when(step == 0)
        def _():
            exchange.start()
            _fill_ssd_constants(**dict(zip(constants, constants_refs)))
            for r in (gcw_ref, grows_ref, gpw_ref):
                r[...] = jnp.zeros_like(r)

        shared = (cw_ref, pw_ref, _row_views(rows_ref, row_widths), gcw_ref, gpw_ref,
                  _row_views(grows_ref, grad_row_widths), constants_refs)
        halos = [p_ref[CHUNK - HALO:CHUNK, 0:D_POOL], halo_ref[...]]
        for (b, c, row0), halo in zip(chunks_of(step), halos):
            rows_here = pl.ds(pl.multiple_of(row0, CHUNK), CHUNK)
            at = lambda ref: ref.at[rows_here]
            one_chunk(b, c, at(p_ref), halo, at(dy_ref), at(ypre_ref), at(pre_ref), st_ref.at[row0 // CHUNK], at(dp_ref),
                      *shared, [carry.at[b] for carry in carries])

        @pl.when(step == n_steps - 1)
        def _():
            exchange.finish()

    def one_chunk(b, c, p_ref, halo, dy_ref, ypre_ref, pre_ref, st_ref, dp_ref, cw_ref, pw_ref, row_refs, gcw_ref,
                  gpw_ref, grad_row_refs, constants_refs, carries):
        dtb_ref, alog_ref, dx_ref, nw_ref, ps_ref = row_refs
        gcb_ref, gdtb_ref, galog_ref, gd_ref, gnw_ref, gps_ref = grad_row_refs
        ds_carry, dpre_next, dq_next = carries
        e_ref, et_ref, etf_ref, tril_ref, triu_ref = constants_refs

        @pl.when(c == nc - 1)
        def _():
            ds_carry[...] = jnp.zeros_like(ds_carry)
            dpre_next[...] = jnp.zeros_like(dpre_next)
            dq_next[...] = jnp.zeros_like(dq_next)

        valid = (c > 0) | (lax.broadcasted_iota(jnp.int32, (CHUNK, 1), 0) >= PAD_ROWS)
        first = c > 0

        u = p_ref[:, 0:D_POOL]
        u_halo = jnp.where(first, halo, 0.0)
        inv_cnt, lane = _pool_inv_count(c)
        pooled = _pool_window_sums(jnp.concatenate([u_halo, u], axis=0), lane) * inv_cnt - u
        dyp = dy_ref[:, 0:D_POOL]
        ps = ps_ref[...]
        dmixed = dyp * ps
        mixed, dpooled = [], []
        for g in range(len(POOL_WINDOWS)):
            gsl = slice(g * 128, (g + 1) * 128)
            pw = pw_ref[g]
            mixed.append(_dot(pooled[:, gsl], pw))
            dpooled.append(_dot_nt(dmixed[:, gsl], pw))
            gpw_ref[g] += _dot_tn(pooled[:, gsl], dmixed[:, gsl])
        gps_ref[...] += jnp.sum(dyp * jnp.concatenate(mixed, axis=1), axis=0, keepdims=True)
        dpooled = jnp.concatenate(dpooled, axis=1)
        dq = dpooled * inv_cnt
        du = _pool_window_sums_ahead(jnp.concatenate([dq, dq_next[...]], axis=0), lane) - dpooled
        dq_next[...] = dq[0:HALO, :]
        dp_ref[:, 0:D_POOL] = _mx(du)

        yp = ypre_ref[...]
        z = p_ref[:, OFF_Z:OFF_Z + D_SSM]
        sz, dsz = _silu_and_grad(z)
        yz = yp * sz
        do = dy_ref[:, D_POOL:]
        nw_row = nw_ref[...]
        dyz = []
        gnw = []
        for g in range(N_GROUPS):
            gs = slice(g * GROUP_CH, (g + 1) * GROUP_CH)
            r = lax.rsqrt(jnp.mean(yz[:, gs] * yz[:, gs], axis=-1, keepdims=True) + EPS)
            n = yz[:, gs] * r
            gnw.append(jnp.sum(do[:, gs] * n, axis=0, keepdims=True))
            dn = do[:, gs] * nw_row[:, gs]
            dyz.append(r * (dn - n * jnp.mean(dn * n, axis=-1, keepdims=True)))
        gnw_ref[...] += jnp.concatenate(gnw, axis=1)
        dyz = jnp.concatenate(dyz, axis=1)
        dp_ref[:, OFF_Z:OFF_Z + D_SSM] = _mx(dyz * yp * dsz)
        dyp_ssm = dyz * sz

        xc, dsilu = _silu_and_grad(pre_ref[...])
        dtr = p_ref[:, OFF_DT:OFF_DT + 128]
        dt, a_row, a_col, dt_pre, head = _dt_and_cumsum(dtr, dtb_ref[...], alog_ref[...], valid, tril_ref[...])
        dxs, dbm, dcm, ddt, da, dd, ds_prev = _ssd_chunk_bwd(
            xc[:, 0:D_SSM], xc[:, D_SSM:D_SSM + 512], xc[:, D_SSM + 512:], dt, a_row, a_col, st_ref[...],
            ds_carry[...], dyp_ssm, dx_ref[...], e_ref[...], et_ref[...], etf_ref[...], triu_ref[...])
        ds_carry[...] = ds_prev
        gd_ref[...] += dd
        galog_ref[...] += da * a_row
        ddtr = jnp.where(valid & head, ddt * _sigmoid(dt_pre), 0.0)
        gdtb_ref[...] += jnp.sum(ddtr, axis=0, keepdims=True)
        dp_ref[:, OFF_DT:OFF_DT + 128] = _mx(ddtr)

        dpre = jnp.concatenate([dxs, dbm, dcm], axis=1) * dsilu
        gcb_ref[...] += jnp.sum(dpre, axis=0, keepdims=True)
        dext = jnp.concatenate([dpre, dpre_next[...]], axis=0)
        dpre_next[...] = dpre[0:HALO, :]
        ups = [_shift_up(dext, 3 - k) for k in range(4)]
        xbc = p_ref[:, OFF_X:OFF_X + D_XBC]
        gcw_ref[...] += jnp.concatenate([jnp.sum(xbc * ups[k], axis=0, keepdims=True) for k in range(4)], axis=0)
        cw = cw_ref[...]
        dp_ref[:, OFF_X:OFF_X + D_XBC] = _mx(cw[3:4, :] * ups[3] + cw[2:3, :] * ups[2]
                                             + cw[1:2, :] * ups[1] + cw[0:1, :] * ups[0])

    blk = lambda w: pl.BlockSpec((2 * CHUNK, w), lambda s: (block_of(s), 0))

    def halo_rows(s):
        b = jnp.minimum(s // pairs, nb - 1)
        below = block_of(s) * 2 * hb - 1
        return jnp.where(block_of(s) == b * pairs, _chunk_block(b, 0, nb, nc) * hb + hb - 1, below)

    hbm = pl.BlockSpec(memory_space=pl.ANY)
    outs = pl.pallas_call(
        body, grid=(n_steps,), name="mixer_bwd",
        in_specs=[blk(PROJ_W), pl.BlockSpec((HALO, D_POOL), lambda s: (halo_rows(s), 0)), blk(D_MIX), blk(D_SSM),
                  blk(D_XBC), pl.BlockSpec((2, D_SSM, D_STATE), lambda s: (block_of(s), 0, 0)),
                  _full((4, D_XBC)), _full((1, sum(row_widths))), _full((4, 128, 128))] + [hbm] * ns,
        out_specs=[blk(PROJ_W), _full((4, D_XBC)), _full((1, sum(grad_row_widths))), _full((4, 128, 128))] + [hbm] * ns,
        out_shape=[jax.ShapeDtypeStruct((m, PROJ_W), MXU_DTYPE), jax.ShapeDtypeStruct((4, D_XBC), F32),
                   jax.ShapeDtypeStruct((1, sum(grad_row_widths)), F32), jax.ShapeDtypeStruct((4, 128, 128), F32)]
        + _ChipExchange.out_shapes(chip_sums, [False] * ns),
        scratch_shapes=[pltpu.VMEM((nb, D_SSM, D_STATE), F32), pltpu.VMEM((nb, HALO, D_XBC), F32),
                        pltpu.VMEM((nb, HALO, D_POOL), F32)] + _ssd_constant_scratch(constants)
        + _ChipExchange.scratch(ns),
        compiler_params=_params("arbitrary"),
    )(proj, proj, dy, ypre, conv_pre, states, cw, rows, pool_w, *chip_sums)
    dproj, g_cw, g_rows, g_pw = outs[:4]
    offs = np.cumsum([0] + grad_row_widths)
    g_cb, g_dtb, g_alog, g_d, g_nw, g_ps = (g_rows[:, a:b] for a, b in zip(offs[:-1], offs[1:]))
    return (dproj, g_cw, g_cb, g_dtb, g_alog, g_d, g_nw, g_pw, g_ps), outs[4:]


MESH_IDS = pl.DeviceIdType.MESH
_HBM = pl.BlockSpec(memory_space=pltpu.HBM)


def _coords():
    return lax.axis_index("x"), lax.axis_index("y"), lax.axis_index("c")


def _other_chips(x, y):
    return [(1 - x, y), (x, 1 - y), (1 - x, 1 - y)]


class _Gather:
    def __init__(self, ins, outs, by_cols, send_sems, recv_sems, local_sems):
        self.ins, self.outs, self.by_cols, self.n = ins, outs, by_cols, len(ins)
        self.send_sems, self.recv_sems, self.local_sems = send_sems, recv_sems, local_sems
        self.x, self.y, self.c = _coords()
        self.me, self.sibling = (self.x, self.y, self.c), (self.x, self.y, 1 - self.c)
        self.chips = _other_chips(self.x, self.y)

    @staticmethod
    def scratch(n):
        return [pltpu.SemaphoreType.DMA((7 * n,)), pltpu.SemaphoreType.DMA((7 * n,)), pltpu.SemaphoreType.DMA((n,))]

    @staticmethod
    def out_shapes(shards, by_cols):
        return [jax.ShapeDtypeStruct((s.shape[0], N_DEV * s.shape[1]) if cols else (N_DEV,) + s.shape, s.dtype)
                for s, cols in zip(shards, by_cols)]

    def _block(self, t, device):
        idx = 4 * device[0] + 2 * device[1] + device[2]
        if not self.by_cols[t]:
            return self.outs[t].at[idx]
        w = self.ins[t].shape[1]
        return self.outs[t].at[:, pl.ds(pl.multiple_of(idx * w, w), w)]

    def _copy(self, t, k, block, to, own=False):
        dst = self._block(t, block)
        return pltpu.make_async_remote_copy(
            src_ref=self.ins[t] if own else dst, dst_ref=dst, send_sem=self.send_sems.at[t * 7 + k],
            recv_sem=self.recv_sems.at[t * 7 + k], device_id=to, device_id_type=MESH_IDS)

    def _mine(self):
        return [pltpu.make_async_copy(self.ins[t], self._block(t, self.me), self.local_sems.at[t])
                for t in range(self.n)]

    def _first(self):
        cps = []
        for t in range(self.n):
            cps.append(self._copy(t, 0, self.me, self.sibling, own=True))
            cps += [self._copy(t, 1 + j, self.me, (*chip, self.c), own=True) for j, chip in enumerate(self.chips)]
        return cps

    def _passed(self):
        return [self._copy(t, 4 + j, (*chip, self.c), self.sibling)
                for j, chip in enumerate(self.chips) for t in range(self.n)]

    def start(self):
        for cp in self._mine() + self._first():
            cp.start()

    def forward(self):
        for j, chip in enumerate(self.chips):
            for t in range(self.n):
                self._copy(t, 1 + j, (*chip, self.c), self.me).wait_recv()
                self._copy(t, 4 + j, (*chip, self.c), self.sibling).start()

    def finish(self):
        for t in range(self.n):
            self._copy(t, 0, self.sibling, self.me).wait_recv()
            for j, chip in enumerate(self.chips):
                self._copy(t, 4 + j, (*chip, 1 - self.c), self.me).wait_recv()
        for cp in self._first() + self._passed():
            cp.wait_send()
        for cp in self._mine():
            cp.wait()


def _weight_gather(shards):
    n = len(shards)

    def body(*refs):
        g = _Gather(refs[:n], refs[n:2 * n], [False] * n, *refs[2 * n:])
        g.start()
        g.forward()
        g.finish()

    return pl.pallas_call(
        body, name="weight_gather",
        in_specs=[_HBM] * n, out_specs=[_HBM] * n,
        out_shape=_Gather.out_shapes(shards, [False] * n),
        scratch_shapes=_Gather.scratch(n),
    )(*shards)


def _grad_exchange_d2d(gs, name, swapped=()):
    n, ns = len(gs), len(swapped)

    def body(*refs):
        ins, whole_ins = refs[:n], refs[n:n + ns]
        got, whole_got = refs[n + ns:2 * n + ns], refs[2 * n + ns:2 * (n + ns)]
        send_sems, recv_sems = refs[2 * (n + ns):]
        x, y, c = _coords()
        theirs = lambda t, k: ins[t].at[k] if len(ins[t].shape) == 3 else ins[t].at[k, 1 - c]
        pairs = [(theirs(t, k), got[t].at[k]) for t in range(n) for k in range(4)] + list(zip(whole_ins, whole_got))
        remote = [pltpu.make_async_remote_copy(
            src_ref=src, dst_ref=dst, send_sem=send_sems.at[i], recv_sem=recv_sems.at[i], device_id=(x, y, 1 - c),
            device_id_type=MESH_IDS) for i, (src, dst) in enumerate(pairs)]
        for cp in remote:
            cp.start()
        for cp in remote:
            cp.wait_recv()
        for cp in remote:
            cp.wait_send()

    outs = pl.pallas_call(
        body, name=name,
        in_specs=[_HBM] * (n + ns), out_specs=[_HBM] * (n + ns),
        out_shape=[jax.ShapeDtypeStruct((4,) + g.shape[-2:], g.dtype) for g in gs]
        + [jax.ShapeDtypeStruct(a.shape, a.dtype) for a in swapped],
        scratch_shapes=[pltpu.SemaphoreType.DMA((4 * n + ns,)), pltpu.SemaphoreType.DMA((4 * n + ns,))],
    )(*gs, *swapped)
    return outs[:n], outs[n:]


def _small_allreduce(pack):
    rows = pack.shape[0]

    def body(p_ref, o_ref, sib_ref, parts_ref, send_sems, recv_sems):
        x, y, c = _coords()
        my_chip = 2 * x + y
        swap = pltpu.make_async_remote_copy(src_ref=p_ref, dst_ref=sib_ref, send_sem=send_sems.at[0],
                                            recv_sem=recv_sems.at[0], device_id=(x, y, 1 - c), device_id_type=MESH_IDS)
        swap.start()
        swap.wait_recv()
        parts_ref[my_chip] = p_ref[...] + sib_ref[...]
        remote = [pltpu.make_async_remote_copy(
            src_ref=parts_ref.at[my_chip], dst_ref=parts_ref.at[my_chip], send_sem=send_sems.at[1 + j],
            recv_sem=recv_sems.at[1 + j], device_id=(cx, cy, c), device_id_type=MESH_IDS)
            for j, (cx, cy) in enumerate(_other_chips(x, y))]
        for cp in remote:
            cp.start()
        for j, (cx, cy) in enumerate(_other_chips(x, y)):
            slot = parts_ref.at[2 * cx + cy]
            pltpu.make_async_remote_copy(src_ref=slot, dst_ref=slot, send_sem=send_sems.at[1 + j],
                                         recv_sem=recv_sems.at[1 + j], device_id=(cx, cy, c),
                                         device_id_type=MESH_IDS).wait_recv()
        o_ref[...] = ((parts_ref[0] + parts_ref[1]) + parts_ref[2]) + parts_ref[3]
        swap.wait_send()
        for cp in remote:
            cp.wait_send()

    vmem = pl.BlockSpec(memory_space=pltpu.VMEM)
    return pl.pallas_call(
        body, name="small_allreduce", in_specs=[vmem], out_specs=vmem,
        out_shape=jax.ShapeDtypeStruct((rows, 128), F32),
        scratch_shapes=[pltpu.VMEM((rows, 128), F32), pltpu.VMEM((4, rows, 128), F32),
                        pltpu.SemaphoreType.DMA((4,)), pltpu.SemaphoreType.DMA((4,))],
    )(pack)


class _ChipExchange:
    def __init__(self, ins, outs, whole, send_sems, recv_sems, local_sems):
        self.ins, self.outs, self.whole, self.n = ins, outs, whole, len(ins)
        self.send_sems, self.recv_sems, self.local_sems = send_sems, recv_sems, local_sems
        self.x, self.y, self.c = _coords()
        self.my_chip = 2 * self.x + self.y
        self.chips = _other_chips(self.x, self.y)

    @staticmethod
    def scratch(n):
        return [pltpu.SemaphoreType.DMA((3 * n,)), pltpu.SemaphoreType.DMA((3 * n,)), pltpu.SemaphoreType.DMA((n,))]

    def _src(self, t, k):
        return self.ins[t] if self.whole[t] else self.ins[t].at[k]

    def _local(self):
        return [pltpu.make_async_copy(self._src(t, self.my_chip), self.outs[t].at[self.my_chip], self.local_sems.at[t])
                for t in range(self.n)]

    def _remote(self):
        return [pltpu.make_async_remote_copy(
            src_ref=self._src(t, 2 * cx + cy), dst_ref=self.outs[t].at[self.my_chip],
            send_sem=self.send_sems.at[t * 3 + j], recv_sem=self.recv_sems.at[t * 3 + j],
            device_id=(cx, cy, self.c), device_id_type=MESH_IDS)
            for t in range(self.n) for j, (cx, cy) in enumerate(self.chips)]

    def start(self):
        for cp in self._remote() + self._local():
            cp.start()

    def finish(self):
        for t in range(self.n):
            for j, (cx, cy) in enumerate(self.chips):
                slot = self.outs[t].at[2 * cx + cy]
                pltpu.make_async_remote_copy(
                    src_ref=slot, dst_ref=slot, send_sem=self.send_sems.at[t * 3 + j],
                    recv_sem=self.recv_sems.at[t * 3 + j], device_id=(cx, cy, self.c),
                    device_id_type=MESH_IDS).wait_recv()
        for cp in self._remote():
            cp.wait_send()
        for cp in self._local():
            cp.wait()

    @staticmethod
    def out_shapes(arrs, whole):
        return [jax.ShapeDtypeStruct(((4,) + a.shape) if w else a.shape, a.dtype) for a, w in zip(arrs, whole)]


_SEMAPHORES = pl.BlockSpec(memory_space=pltpu.SEMAPHORE)
_SIDE_EFFECT = pltpu.SideEffectType.DATAFLOW_SIDE_EFFECTING


def _split_exchange_copies(srcs, lands, whole, send_sems, recv_sems, waiting):
    x, y, c = _coords()
    copies = []
    for t in range(len(srcs)):
        for j, (cx, cy) in enumerate(_other_chips(x, y)):
            src = srcs[t] if whole[t] else srcs[t].at[2 * cx + cy]
            dst = lands[t].at[2 * cx + cy] if waiting else lands[t].at[2 * x + y]
            copies.append(pltpu.make_async_remote_copy(
                src_ref=src, dst_ref=dst, send_sem=send_sems.at[3 * t + j], recv_sem=recv_sems.at[3 * t + j],
                device_id=(cx, cy, c), device_id_type=MESH_IDS))
    return copies


def _chip_exchange_start(arrays, whole, name):
    n = len(arrays)

    def body(*refs):
        srcs, lands = refs[:n], refs[n:2 * n]
        send_sems, recv_sems = refs[2 * n:2 * n + 2]
        for cp in _split_exchange_copies(srcs, lands, whole, send_sems, recv_sems, waiting=False):
            cp.start()
        token = refs[-1]
        token[...] = jnp.zeros_like(token)

    land_shapes = [((4,) + a.shape) if w else a.shape for a, w in zip(arrays, whole)]
    hbm = lambda shape, a: pltpu.HBM(shape, a.dtype)
    outs = pl.pallas_call(
        body, name=name,
        out_shape=(pltpu.SemaphoreType.DMA((3 * n,)), pltpu.SemaphoreType.DMA((3 * n,)),
                   *[hbm(a.shape, a) for a in arrays], *[hbm(s, a) for s, a in zip(land_shapes, arrays)],
                   jax.ShapeDtypeStruct((8, 128), F32)),
        in_specs=(_HBM,) * (2 * n),
        out_specs=(_SEMAPHORES, _SEMAPHORES) + (_HBM,) * (2 * n) + (pl.BlockSpec(memory_space=pltpu.VMEM),),
        input_output_aliases={i: 2 + i for i in range(2 * n)},
        compiler_params=pltpu.CompilerParams(has_side_effects=_SIDE_EFFECT),
    )(*[pltpu.with_memory_space_constraint(a, pltpu.HBM) for a in arrays],
      *[pltpu.with_memory_space_constraint(lax.empty(s, a.dtype), pltpu.HBM) for s, a in zip(land_shapes, arrays)])
    return outs[0], outs[1], outs[2:2 + n], outs[2 + n:2 + 2 * n], outs[-1]


def _chip_exchange_wait(send_sems, recv_sems, srcs, lands, whole, after, name):
    n = len(srcs)

    def body(*refs):
        src_refs, land_refs = refs[:n], refs[n:2 * n]
        for cp in _split_exchange_copies(src_refs, land_refs, whole, refs[2 * n], refs[2 * n + 1], waiting=True):
            cp.wait_send()
            cp.wait_recv()

    outs = pl.pallas_call(
        body, name=name, out_shape=tuple(pltpu.HBM(a.shape, a.dtype) for a in (*srcs, *lands)),
        in_specs=(_HBM,) * (2 * n) + (_SEMAPHORES, _SEMAPHORES, pl.BlockSpec(memory_space=pl.ANY)),
        out_specs=(_HBM,) * (2 * n), input_output_aliases={i: i for i in range(2 * n)},
        compiler_params=pltpu.CompilerParams(has_side_effects=_SIDE_EFFECT),
    )(*srcs, *lands, send_sems, recv_sems, after)
    return outs[:n], outs[n:]


def _sum_two(a, b, name):
    def body(a_ref, b_ref, o_ref):
        o_ref[...] = a_ref[...] + b_ref[...]

    return pl.pallas_call(body, name=name, out_shape=jax.ShapeDtypeStruct(a.shape, a.dtype))(a, b)


def _sum_chips(landed, own, chip, name):
    def body(chip_ref, l_ref, own_ref, o_ref):
        part = lambda k: jnp.where(chip_ref[0] == k, own_ref[...], l_ref[k])
        o_ref[...] = ((part(0) + part(1)) + part(2)) + part(3)

    grid_spec = pltpu.PrefetchScalarGridSpec(
        num_scalar_prefetch=1, grid=(1,),
        in_specs=[pl.BlockSpec(landed.shape, lambda i, c: (0, 0, 0)), pl.BlockSpec(own.shape, lambda i, c: (0, 0))],
        out_specs=pl.BlockSpec(own.shape, lambda i, c: (0, 0)))
    return pl.pallas_call(body, grid_spec=grid_spec, name=name,
                          out_shape=jax.ShapeDtypeStruct(own.shape, own.dtype))(chip, landed, own)


def _row_tile(rows, cols, n_arrays):
    budget = 24 * 1024 * 1024
    padded = -(-cols // 128) * 128
    step = 16 if rows % 16 == 0 else 8
    tr = max(step, budget // (n_arrays * 2 * 4 * padded) // step * step)
    while rows % tr:
        tr -= step
    return tr


def _chip_sum(g, got, core, name):
    rows, cols = g.shape[-2:]
    tr = _row_tile(rows, cols, 3)

    def body(c_ref, a_ref, b_ref, o_ref):
        o_ref[...] = (a_ref[...] + b_ref[...].astype(F32)).astype(o_ref.dtype)

    mine = (pl.BlockSpec((None, tr, cols), lambda k, i, c: (k, i, 0)) if g.ndim == 3
            else pl.BlockSpec((None, None, tr, cols), lambda k, i, c: (k, c[0], i, 0)))
    grid_spec = pltpu.PrefetchScalarGridSpec(
        num_scalar_prefetch=1, grid=(4, rows // tr),
        in_specs=[mine, pl.BlockSpec((None, tr, cols), lambda k, i, c: (k, i, 0))],
        out_specs=pl.BlockSpec((None, tr, cols), lambda k, i, c: (k, i, 0)))
    return pl.pallas_call(body, grid_spec=grid_spec, name=name,
                          out_shape=jax.ShapeDtypeStruct((4, rows, cols), MXU_DTYPE),
                          compiler_params=_params("arbitrary", "arbitrary"))(core, g, got)


def _adamw_math(w, g, m, v):
    m2 = ADAM_B1 * m + (1.0 - ADAM_B1) * g
    v2 = ADAM_B2 * v + (1.0 - ADAM_B2) * (g * g)
    m_hat = m2 / (1.0 - ADAM_B1 ** ADAM_STEP)
    v_hat = v2 / (1.0 - ADAM_B2 ** ADAM_STEP)
    delta = -ADAM_LR * (m_hat / (jnp.sqrt(v_hat) + ADAM_EPS) + ADAM_WD * w)
    return delta, m2, v2


def _adamw(parts, w, m, v, name, own=None, chip=None):
    rows, cols = w.shape
    tr = _row_tile(rows, cols, 11 if own is None else 15)

    def body(*refs):
        if own is None:
            p_ref, w_ref, m_ref, v_ref, g_ref, d_ref, m2_ref, v2_ref = refs
            part = lambda k: p_ref[k].astype(F32)
        else:
            chip_ref, p_ref, own_ref, w_ref, m_ref, v_ref, g_ref, d_ref, m2_ref, v2_ref = refs
            part = lambda k: jnp.where(chip_ref[0] == k, own_ref[k], p_ref[k]).astype(F32)
        g = ((part(0) + part(1)) + part(2)) + part(3)
        d, m2, v2 = _adamw_math(w_ref[...], g, m_ref[...], v_ref[...])
        g_ref[...] = g
        d_ref[...] = d
        m2_ref[...] = m2
        v2_ref[...] = v2

    blk = pl.BlockSpec((tr, cols), lambda i, *_: (i, 0))
    pblk = pl.BlockSpec((4, tr, cols), lambda i, *_: (0, i, 0))
    out = jax.ShapeDtypeStruct((rows, cols), F32)
    if own is None:
        return pl.pallas_call(body, grid=(rows // tr,), name=name, in_specs=[pblk, blk, blk, blk],
                              out_specs=[blk] * 4, out_shape=[out] * 4,
                              compiler_params=_params("arbitrary"))(parts, w, m, v)
    grid_spec = pltpu.PrefetchScalarGridSpec(num_scalar_prefetch=1, grid=(rows // tr,),
                                             in_specs=[pblk, pblk, blk, blk, blk], out_specs=[blk] * 4)
    return pl.pallas_call(body, grid_spec=grid_spec, name=name, out_shape=[out] * 4,
                          compiler_params=_params("arbitrary"))(chip, parts, own, w, m, v)


def _adamw_transposed(parts_t, w, m, v, name):
    rows, cols = w.shape
    tc = 256

    def body(p_ref, w_ref, m_ref, v_ref, g_ref, d_ref, m2_ref, v2_ref):
        part = lambda k: p_ref[k].astype(F32)
        g = (((part(0) + part(1)) + part(2)) + part(3)).T
        d, m2, v2 = _adamw_math(w_ref[...], g, m_ref[...], v_ref[...])
        g_ref[...] = g
        d_ref[...] = d
        m2_ref[...] = m2
        v2_ref[...] = v2

    blk = pl.BlockSpec((rows, tc), lambda i: (0, i))
    out = jax.ShapeDtypeStruct((rows, cols), F32)
    return pl.pallas_call(body, grid=(cols // tc,), name=name,
                          in_specs=[pl.BlockSpec((4, tc, rows), lambda i: (0, i, 0)), blk, blk, blk],
                          out_specs=[blk] * 4, out_shape=[out] * 4,
                          compiler_params=_params("arbitrary"))(parts_t, w, m, v)


def _adamw_small(gs, ws, ms, vs):
    n = len(ws)

    def body(*refs):
        g_refs, w_refs, m_refs, v_refs = (refs[k * n:(k + 1) * n] for k in range(4))
        d_refs, m2_refs, v2_refs = (refs[(4 + k) * n:(5 + k) * n] for k in range(3))
        for t in range(n):
            d, m2, v2 = _adamw_math(w_refs[t][...], g_refs[t][...], m_refs[t][...], v_refs[t][...])
            d_refs[t][...] = d
            m2_refs[t][...] = m2
            v2_refs[t][...] = v2

    outs = pl.pallas_call(body, name="adamw_small",
                          out_shape=[jax.ShapeDtypeStruct(w.shape, F32) for w in ws] * 3)(*gs, *ws, *ms, *vs)
    return outs[:n], outs[n:2 * n], outs[2 * n:]


_PACK_TILE = 8 * 128


def _pack(arrays):
    rows = []
    for a in arrays:
        flat = a.astype(F32).reshape(-1)
        rows.append(jnp.pad(flat, (0, -flat.shape[0] % _PACK_TILE)).reshape(-1, 128))
    return jnp.concatenate(rows, axis=0)


def _unpack(pack, shapes):
    out, r = [], 0
    for s in shapes:
        n = int(np.prod(s))
        out.append(pack[r:r + -(-n // 128)].reshape(-1)[:n].reshape(s))
        r += -(-n // _PACK_TILE) * 8
    return out


def _pad128(v):
    v = v.reshape(1, -1).astype(F32)
    return jnp.pad(v, ((0, 0), (0, 128 - v.shape[1])))


_WEIGHTS = ["meta", "norm_mix_w", "w_in", "pool_w", "pool_scale", "conv_w", "conv_b", "dt_bias", "a_log", "d_skip",
            "ssm_norm_w", "w_out", "norm_ffn_w", "w_ff1", "w_ff2", "norm_f_w"]
_BIG = ["w_in", "w_out", "w_ff1", "w_ff2"]
_SMALL = [n for n in _WEIGHTS if n not in _BIG]


def kernel(x, meta, norm_mix_w, w_in, pool_w, pool_scale, conv_w, conv_b, dt_bias, a_log, d_skip, ssm_norm_w, w_out, norm_ffn_w, w_ff1, w_ff2, norm_f_w, loss_target, m_meta, m_norm_mix_w, m_w_in, m_pool_w, m_pool_scale, m_conv_w, m_conv_b, m_dt_bias, m_a_log, m_d_skip, m_ssm_norm_w, m_w_out, m_norm_ffn_w, m_w_ff1, m_w_ff2, m_norm_f_w, v_meta, v_norm_mix_w, v_w_in, v_pool_w, v_pool_scale, v_conv_w, v_conv_b, v_dt_bias, v_a_log, v_d_skip, v_ssm_norm_w, v_w_out, v_norm_ffn_w, v_w_ff1, v_w_ff2, v_norm_f_w):
    wts = dict(meta=meta, norm_mix_w=norm_mix_w, w_in=w_in, pool_w=pool_w, pool_scale=pool_scale, conv_w=conv_w,
               conv_b=conv_b, dt_bias=dt_bias, a_log=a_log, d_skip=d_skip, ssm_norm_w=ssm_norm_w, w_out=w_out,
               norm_ffn_w=norm_ffn_w, w_ff1=w_ff1, w_ff2=w_ff2, norm_f_w=norm_f_w)
    mom1 = dict(zip(_WEIGHTS, (m_meta, m_norm_mix_w, m_w_in, m_pool_w, m_pool_scale, m_conv_w, m_conv_b, m_dt_bias,
                               m_a_log, m_d_skip, m_ssm_norm_w, m_w_out, m_norm_ffn_w, m_w_ff1, m_w_ff2, m_norm_f_w)))
    mom2 = dict(zip(_WEIGHTS, (v_meta, v_norm_mix_w, v_w_in, v_pool_w, v_pool_scale, v_conv_w, v_conv_b, v_dt_bias,
                               v_a_log, v_d_skip, v_ssm_norm_w, v_w_out, v_norm_ffn_w, v_w_ff1, v_w_ff2, v_norm_f_w)))
    xi, yi, ci = _coords()
    dev = 4 * xi + 2 * yi + ci
    win_cols = w_in.shape[-1]
    cw_cols = conv_w.shape[-1]

    nb, seq, _ = x.shape
    core = jnp.reshape(ci, (1,)).astype(jnp.int32)
    owners = lambda a: a.reshape((4, 2) + a.shape[1:])

    lead_pack = jnp.zeros((N_META, 512), F32)
    lead_pack = lead_pack.at[:, :128].set(meta).at[:4, 128:128 + cw_cols].set(conv_w[0])
    g_win, g_lead = _weight_gather([_mx(w_in[0]), lead_pack])
    win_full = _assemble_bands(g_win, PROJ_W)
    meta_full = jnp.transpose(g_lead[:, :, :128], (1, 0, 2)).reshape(N_META, D_MODEL)
    cw_full = jnp.transpose(g_lead[:, :4, 128:128 + cw_cols], (1, 0, 2)).reshape(4, D_XBC)

    lead = jnp.concatenate([jnp.zeros((PAD_ROWS, D_MODEL), F32), meta_full] * nb, axis=0)
    x_rows = x.reshape(nb * seq, D_MODEL)
    tgt_rows = loss_target.reshape(nb * seq, D_MODEL)
    dt_bias_p, a_log_p = _pad128(dt_bias), _pad128(a_log)
    d_x = jnp.repeat(d_skip.reshape(1, N_HEADS).astype(F32), HEAD_DIM, axis=1)
    norm_f_row = norm_f_w.reshape(1, D_MODEL)

    hn1, proj = _in_proj(x_rows, lead, norm_mix_w, win_full)
    late_cols = [False, True, False]
    y, ypre, conv_pre, states, (g_wout, wff1_full, g_wff2) = _mixer_fwd(
        proj, cw_full, conv_b, dt_bias_p, a_log_p, d_x, ssm_norm_w, pool_w[0], pool_scale, nb,
        [_mx(w_out[0]), _mx(w_ff1[0]), _mx(w_ff2[0])], late_cols)
    wout_full = g_wout.reshape(D_MIX, D_MODEL)
    wff2_full = g_wff2.reshape(D_FF, D_MODEL)
    loss, gr_nf, gr_nffn, ff, da, hn2, dh1, dh2, dy = _ffn_fwd_bwd(
        x_rows, lead, y, tgt_rows, wout_full, norm_ffn_w, wff1_full, wff2_full, norm_f_row)
    gr_wff2 = _tn_matmul(ff, dh2, "grad_w_ff2", tka=1024, max_slab=2816)
    gr_wff1_t = _tn_matmul(da, hn2, "grad_w_ff1", tka=1024, max_slab=2816)
    gr_wout = _tn_matmul(y, dh1, "grad_w_out", tka=1024, max_slab=2816)

    by_owner = lambda k: [owners(gr_wout[k].reshape(N_DEV, D_MIX // N_DEV, D_MODEL)),
                          owners(gr_wff1_t[k].reshape(N_DEV, D_FF // N_DEV, D_MODEL)),
                          owners(gr_wff2[k].reshape(N_DEV, D_FF // N_DEV, D_MODEL))]
    late_parts = by_owner(0)
    late_got, _ = _grad_exchange_d2d(by_owner(1), "grad_exchange_d2d_late")
    late_sums = [_chip_sum(late_parts[t], late_got[t], core, "chip_sum_late_%d" % t) for t in range(3)]
    (dproj, gr_cw, gr_cb, gr_dtb, gr_alog, gr_d, gr_nw, gr_pw, gr_ps), late_exchanged = _mixer_bwd(
        proj, dy, ypre, conv_pre, states, cw_full, dt_bias_p, a_log_p, d_x, ssm_norm_w, pool_w[0], pool_scale, nb,
        late_sums)

    early = dict(pool_w=gr_pw, pool_scale=gr_ps, conv_w=gr_cw, conv_b=gr_cb, dt_bias=gr_dtb[:, :N_HEADS],
                 a_log=gr_alog[:, :N_HEADS], d_skip=gr_d[:, :N_HEADS], ssm_norm_w=gr_nw, norm_ffn_w=gr_nffn,
                 norm_f_w=gr_nf, loss=loss[0:1, 0:1])
    early_pack = _pack(list(early.values()))
    win_mine, win_theirs = _tn_matmul_banded(hn1, dproj, core, "grad_w_in", win_cols, tka=512)
    win_got, (early_got,) = _grad_exchange_d2d([win_theirs], "grad_exchange_d2d_w_in", swapped=[early_pack])
    win_sum = _chip_sum(win_mine, win_got[0], core, "chip_sum_w_in")
    early_chip = _sum_two(early_pack, early_got, "chip_sum_small")
    whole = [False, True]
    send_sems, recv_sems, sent, landing, started = _chip_exchange_start([win_sum, early_chip], whole, "w_in_exchange_start")
    gx_rows, gr_nmix, gr_meta = _in_proj_bwd(dproj, x_rows, lead, dh1, norm_mix_w, win_full, started)
    (win_sum, early_chip), (win_landed, early_landed) = _chip_exchange_wait(
        send_sems, recv_sems, sent, landing, whole, gr_nmix, "w_in_exchange_wait")
    parts = dict(w_in=win_landed, w_out=late_exchanged[0], w_ff1=late_exchanged[1], w_ff2=late_exchanged[2])
    my_chip = jnp.reshape(2 * xi + yi, (1,)).astype(jnp.int32)
    early_sum = _sum_chips(early_landed, early_chip, my_chip, "small_sum")

    tail = dict(meta=gr_meta, norm_mix_w=gr_nmix)
    tail_sum = _small_allreduce(_pack(list(tail.values())))
    gs = dict(zip(early, _unpack(early_sum, [a.shape for a in early.values()])))
    gs.update(zip(tail, _unpack(tail_sum, [a.shape for a in tail.values()])))
    gs["meta"] = lax.dynamic_slice_in_dim(gs["meta"], dev * 128, 128, axis=1)
    gs["conv_w"] = lax.dynamic_slice_in_dim(gs["conv_w"], dev * cw_cols, cw_cols, axis=1)

    res = {}
    for n in _BIG:
        shp = wts[n].shape
        args = (parts[n], wts[n][0], mom1[n][0], mom2[n][0], "adamw_" + n)
        if n == "w_ff1":
            outs = _adamw_transposed(*args)
        elif n == "w_in":
            outs = _adamw(*args, own=win_sum, chip=my_chip)
        else:
            outs = _adamw(*args)
        res[n] = [o.reshape(shp) for o in outs]
    as2d = lambda a: a.reshape(-1, a.shape[-1])
    small_g = [as2d(gs[n].reshape(wts[n].shape)) for n in _SMALL]
    small_out = _adamw_small(small_g, *[[as2d(d[n]) for n in _SMALL] for d in (wts, mom1, mom2)])
    for k, n in enumerate(_SMALL):
        res[n] = [o[k].reshape(wts[n].shape) for o in (small_g,) + tuple(small_out)]

    grad_x = gx_rows.reshape(nb, seq, D_MODEL)
    return (gs["loss"][0, 0], grad_x, *[res[n][0] for n in _WEIGHTS], *[res[n][1] for n in _WEIGHTS],
            *[res[n][2] for n in _WEIGHTS], *[res[n][3] for n in _WEIGHTS])
```

```python
import numpy as np
import jax
import jax.numpy as jnp
from jax import lax
from jax.experimental import pallas as pl
from jax.experimental.pallas import tpu as pltpu

F32 = jnp.float32
MXU_DTYPE = jnp.bfloat16

D_MODEL = 1024
D_POOL = 512
D_SSM = 1536
D_XBC = 2560
N_HEADS = 24
HEAD_DIM = 64
N_GROUPS = 4
GROUP_CH = D_SSM // N_GROUPS
D_STATE = 128
CHUNK = 128
N_META = 16
LEAD = CHUNK
PAD_ROWS = LEAD - N_META
ROW_TILE = 2 * CHUNK
D_MIX = D_POOL + D_SSM
D_FF = 4096
PROJ_W = 4736
OFF_Z = D_POOL
OFF_X = D_POOL + D_SSM
OFF_DT = OFF_X + D_XBC
D_IN_PROJ = OFF_DT + N_HEADS
POOL_WINDOWS = (2, 4, 8, 16)
HALO = 16
EPS = 1e-5
N_DEV = 8

ADAM_LR, ADAM_B1, ADAM_B2, ADAM_EPS, ADAM_WD, ADAM_STEP = 0.001, 0.9, 0.999, 1e-08, 0.01, 10

VMEM_LIMIT = 60 * 1024 * 1024


def _mx(a):
    return a.astype(MXU_DTYPE)


def _dot(a, b):
    return jnp.dot(_mx(a), _mx(b), preferred_element_type=F32)


def _dot_nt(a, b):
    return lax.dot_general(_mx(a), _mx(b), (((1,), (1,)), ((), ())), preferred_element_type=F32)


def _dot_tn(a, b):
    return lax.dot_general(_mx(a), _mx(b), (((0,), (0,)), ((), ())), preferred_element_type=F32)


def _split3(x):
    hi = x.astype(MXU_DTYPE)
    r = x - hi.astype(F32)
    mid = r.astype(MXU_DTYPE)
    lo = (r - mid.astype(F32)).astype(MXU_DTYPE)
    return hi, mid, lo


def _exact_l(c, x):
    hi, mid, lo = _split3(x)
    f = lambda p: jnp.dot(c, p, preferred_element_type=F32)
    return f(hi) + f(mid) + f(lo)


def _exact_r(x, c):
    hi, mid, lo = x if isinstance(x, tuple) else _split3(x)
    f = lambda p: jnp.dot(p, c, preferred_element_type=F32)
    return f(hi) + f(mid) + f(lo)


def _contract(x, c):
    hi = x.astype(MXU_DTYPE)
    lo = (x - hi.astype(F32)).astype(MXU_DTYPE)
    return jnp.dot(hi, c, preferred_element_type=F32) + jnp.dot(lo, c, preferred_element_type=F32)


def _sigmoid(x):
    return jax.nn.sigmoid(x)


def _softplus(x):
    return jnp.maximum(x, 0.0) + jnp.log1p(jnp.exp(-jnp.abs(x)))


def _silu_and_grad(x):
    s = _sigmoid(x)
    y = x * s
    return y, s + y * (1.0 - s)


def _shift_up(ext, s):
    if s == 0:
        return ext[:CHUNK, :]
    return pltpu.roll(ext, ext.shape[0] - s, 0)[:CHUNK, :]


def _by_pool_group(lane, a2, a4, a8, a16):
    return jnp.where(lane < 128, a2, jnp.where(lane < 256, a4, jnp.where(lane < 384, a8, a16)))


def _pool_inv_count(chunk_idx):
    row = lax.broadcasted_iota(jnp.int32, (CHUNK, D_POOL), 0)
    lane = lax.broadcasted_iota(jnp.int32, (CHUNK, D_POOL), 1)
    pos1 = jnp.maximum(chunk_idx * CHUNK + row - (PAD_ROWS - 1), 1)
    w = _by_pool_group(lane, 2, 4, 8, 16)
    return 1.0 / jnp.minimum(pos1, w).astype(F32), lane


def _pool_window_sums(u_ext, lane):
    s2 = u_ext + pltpu.roll(u_ext, 1, 0)
    s4 = s2 + pltpu.roll(s2, 2, 0)
    s8 = s4 + pltpu.roll(s4, 4, 0)
    s16 = s8 + pltpu.roll(s8, 8, 0)
    return _by_pool_group(lane, s2[HALO:], s4[HALO:], s8[HALO:], s16[HALO:])


def _pool_window_sums_ahead(q_ext, lane):
    n = q_ext.shape[0]
    r2 = q_ext + pltpu.roll(q_ext, n - 1, 0)
    r4 = r2 + pltpu.roll(r2, n - 2, 0)
    r8 = r4 + pltpu.roll(r4, n - 4, 0)
    r16 = r8 + pltpu.roll(r8, n - 8, 0)
    return _by_pool_group(lane, r2[:CHUNK], r4[:CHUNK], r8[:CHUNK], r16[:CHUNK])


def _conv_pre(ext, xbc, cw, cb):
    s1 = pltpu.roll(ext, 1, 0)
    near = cw[3:4, :] * xbc + cw[2:3, :] * s1[HALO:, :]
    far = cw[1:2, :] * ext + cw[0:1, :] * s1
    return cb + near + pltpu.roll(far, 2, 0)[HALO:, :]


def _dt_and_cumsum(dtr, dt_bias, a_log, valid, tril):
    lane = lax.broadcasted_iota(jnp.int32, (CHUNK, 128), 1)
    head = lane < N_HEADS
    pre = dtr + dt_bias
    dt = jnp.where(valid & head, _softplus(pre), 0.0)
    a_row = jnp.where(head[0:1, :], -jnp.exp(a_log), 0.0)
    a_col = _exact_l(tril, dt * a_row)
    return dt, a_row, a_col, pre, head


def _decay(a_col, a_row_t, h, causal):
    seg = a_col[:, h:h + 1] - a_row_t[h:h + 1, :]
    return jnp.where(causal, jnp.exp(jnp.minimum(seg, 0.0)), 0.0)


def _ssd_chunk_fwd(xs, bm, cm, dt, a_col, s_prev, d_x, e_mat, et_f32):
    lane = lax.broadcasted_iota(jnp.int32, (CHUNK, 128), 1)
    rowi = lax.broadcasted_iota(jnp.int32, (CHUNK, CHUNK), 0)
    coli = lax.broadcasted_iota(jnp.int32, (CHUNK, CHUNK), 1)
    causal = rowi >= coli
    a_row_t = a_col.T
    ax = _exact_r(a_col, e_mat)
    dtx = _exact_r(dt, e_mat)
    xdt = xs * dtx
    ax_last = ax[CHUNK - 1:CHUNK, :]
    e_a = jnp.exp(ax)
    w_end = xdt * jnp.exp(ax_last - ax)
    cd_col = jnp.exp(jnp.sum(et_f32 * a_col[CHUNK - 1:CHUNK, :], axis=1, keepdims=True))
    ys, s_new = [], []
    for g in range(N_GROUPS):
        gs = slice(g * GROUP_CH, (g + 1) * GROUP_CH)
        bg = bm[:, g * D_STATE:(g + 1) * D_STATE]
        cg = cm[:, g * D_STATE:(g + 1) * D_STATE]
        sg = s_prev[gs, :]
        cb = _dot_nt(cg, bg)
        y_off = _dot_nt(cg, sg) * e_a[:, gs]
        s_new.append(sg * cd_col[gs, :] + _dot_tn(w_end[:, gs], bg))
        for pr in range(3):
            c0 = g * GROUP_CH + pr * 128
            xdt_p = xdt[:, c0:c0 + 128]
            h0 = g * 6 + pr * 2
            y0 = _dot(cb * _decay(a_col, a_row_t, h0, causal), xdt_p)
            y1 = _dot(cb * _decay(a_col, a_row_t, h0 + 1, causal), xdt_p)
            ys.append(jnp.where(lane < HEAD_DIM, y0, y1) + y_off[:, pr * 128:(pr + 1) * 128])
    y = jnp.concatenate(ys, axis=1) + d_x * xs
    return y, jnp.concatenate(s_new, axis=0)


def _ssd_chunk_bwd(xs, bm, cm, dt, a_row, a_col, s_prev, ds_new, dy, d_x, e_mat, et_mat, et_f32, triu):
    lane = lax.broadcasted_iota(jnp.int32, (CHUNK, 128), 1)
    sub = lax.broadcasted_iota(jnp.int32, (CHUNK, 128), 0)
    rowi = lax.broadcasted_iota(jnp.int32, (CHUNK, CHUNK), 0)
    coli = lax.broadcasted_iota(jnp.int32, (CHUNK, CHUNK), 1)
    causal = rowi >= coli
    a_row_t = a_col.T
    a_last = a_col[CHUNK - 1:CHUNK, :]
    a_split, dt_split = _split3(a_col), _split3(dt)
    sub8 = lax.broadcasted_iota(jnp.int32, (8, GROUP_CH), 0)

    dxs, dbs, dcs, dsp = [], [], [], []
    zcol = jnp.zeros((CHUNK, 128), F32)
    zrows = []
    da_col = jnp.zeros((CHUNK, 128), F32)
    ddt = jnp.zeros((CHUNK, 128), F32)
    head_sums = jnp.zeros((8, 128), F32)
    q_row = jnp.zeros((1, 128), F32)
    for g in range(N_GROUPS):
        gs = slice(g * GROUP_CH, (g + 1) * GROUP_CH)
        e_g, et_g = e_mat[:, gs], et_mat[gs, :]
        xs_g, dy_g = xs[:, gs], dy[:, gs]
        ax = _exact_r(a_split, e_g)
        dtx = _exact_r(dt_split, e_g)
        xdt = xs_g * dtx
        dte = jnp.exp(ax[CHUNK - 1:CHUNK, :] - ax)
        w_end = xdt * dte
        cd_col = jnp.exp(jnp.sum(et_f32[gs, :] * a_last, axis=1, keepdims=True))
        dye = dy_g * jnp.exp(ax)
        bg = bm[:, g * D_STATE:(g + 1) * D_STATE]
        cg = cm[:, g * D_STATE:(g + 1) * D_STATE]
        sg = s_prev[gs, :]
        dsg = ds_new[gs, :]
        cb = _dot_nt(cg, bg)
        cs = _dot_nt(cg, sg)
        dcg = _dot(dye, sg)
        dsp.append(dsg * cd_col + _dot_tn(dye, cg))
        dwg = _dot_nt(bg, dsg)
        dbg = _dot(w_end, dsg)
        ww = dwg * w_end
        t1 = jnp.sum(dsg * sg, axis=1, keepdims=True) * cd_col
        dcb = jnp.zeros((CHUNK, CHUNK), F32)
        pairs = []
        for pr in range(3):
            ps = slice(pr * 128, (pr + 1) * 128)
            xdt_p, dy_p = xdt[:, ps], dy_g[:, ps]
            acc = None
            for half in range(2):
                h = g * 6 + pr * 2 + half
                ld = _decay(a_col, a_row_t, h, causal)
                gm = cb * ld
                dym = jnp.where((lane < HEAD_DIM) if half == 0 else (lane >= HEAD_DIM), dy_p, 0.0)
                dg = _dot_nt(dym, xdt_p)
                dseg = dg * gm
                dcb = dcb + dg * ld
                t = _dot_tn(gm, dym)
                acc = t if acc is None else acc + t
                zcol = jnp.where(lane == h, jnp.sum(dseg, axis=1, keepdims=True), zcol)
                zrows.append(jnp.sum(dseg, axis=0, keepdims=True))
            pairs.append(acc)
        dxdt = dwg * dte + jnp.concatenate(pairs, axis=1)
        dcs.append(dcg + _dot(dcb, bg))
        dbs.append(dbg + _dot_tn(dcb, cg))
        dxs.append(dxdt * dtx + d_x[:, gs] * dy_g)
        da_col = da_col + _contract(dye * cs - ww, et_g)
        ddt = ddt + _contract(dxdt * xs_g, et_g)
        col_sums = jnp.where(sub8 == 0, jnp.sum(dy_g * xs_g, axis=0, keepdims=True),
                             jnp.where(sub8 == 1, jnp.sum(ww, axis=0, keepdims=True), 0.0))
        head_sums = head_sums + _exact_r(col_sums, et_g)
        q_row = q_row + jnp.sum(et_f32[gs, :] * t1, axis=0, keepdims=True)

    dd = head_sums[0:1, :]
    q_row = q_row + head_sums[1:2, :]
    zrow = jnp.concatenate(zrows + [jnp.zeros((128 - N_HEADS, CHUNK), F32)], axis=0)
    da_col = da_col + zcol - zrow.T + jnp.where(sub == CHUNK - 1, q_row, 0.0)
    rc = _exact_l(triu, da_col)
    ddt = ddt + rc * a_row
    da = jnp.sum(rc * dt, axis=0, keepdims=True)
    return (jnp.concatenate(dxs, axis=1), jnp.concatenate(dbs, axis=1), jnp.concatenate(dcs, axis=1), ddt, da, dd,
            jnp.concatenate(dsp, axis=0))


_SSD_CONSTANT_SHAPES = dict(e=((128, D_SSM), MXU_DTYPE), et=((D_SSM, 128), MXU_DTYPE), et_f32=((D_SSM, 128), F32),
                            tril=((CHUNK, CHUNK), MXU_DTYPE), triu=((CHUNK, CHUNK), MXU_DTYPE))


def _ssd_constant_scratch(names):
    return [pltpu.VMEM(*_SSD_CONSTANT_SHAPES[n]) for n in names]


def _fill_ssd_constants(**refs):
    iota = lambda shape, d: lax.broadcasted_iota(jnp.int32, shape, d)
    shift = HEAD_DIM.bit_length() - 1
    marks = dict(
        e=lambda: iota((128, D_SSM), 0) == (iota((128, D_SSM), 1) >> shift),
        et=lambda: iota((D_SSM, 128), 1) == (iota((D_SSM, 128), 0) >> shift),
        et_f32=lambda: iota((D_SSM, 128), 1) == (iota((D_SSM, 128), 0) >> shift),
        tril=lambda: iota((CHUNK, CHUNK), 1) <= iota((CHUNK, CHUNK), 0),
        triu=lambda: iota((CHUNK, CHUNK), 1) >= iota((CHUNK, CHUNK), 0))
    for name, ref in refs.items():
        ref[...] = jnp.where(marks[name](), 1.0, 0.0).astype(ref.dtype)


def _row_views(ref, widths):
    views, off = [], 0
    for w in widths:
        views.append(ref.at[:, off:off + w])
        off += w
    return views


def _full(shape):
    nd = len(shape)
    return pl.BlockSpec(shape, lambda *_: (0,) * nd)


def _params(*sem):
    return pltpu.CompilerParams(dimension_semantics=sem, vmem_limit_bytes=VMEM_LIMIT)


def _token_tiles(width, n_tok_tiles):
    return pl.BlockSpec((ROW_TILE, width), lambda i: (jnp.minimum(i, n_tok_tiles - 1), 0))


def _in_proj(x, lead, w1, win):
    nt = x.shape[0] // ROW_TILE
    m = x.shape[0] + ROW_TILE
    tm = ROW_TILE

    def body(x_ref, lead_ref, w1_ref, win_hbm, hn_ref, proj_ref, win_v, sem):
        i = pl.program_id(0)

        @pl.when(i == 0)
        def _():
            cp = pltpu.make_async_copy(win_hbm, win_v, sem)
            cp.start()
            cp.wait()

        x = jnp.where(i == nt, lead_ref[...], x_ref[...])
        r = lax.rsqrt(jnp.mean(x * x, axis=-1, keepdims=True) + EPS)
        hn = _mx(x * r * w1_ref[...])
        hn_ref[...] = hn
        for j in range(0, PROJ_W, 512):
            w = min(512, PROJ_W - j)
            proj_ref[:, j:j + w] = jnp.dot(hn, win_v[:, j:j + w], preferred_element_type=F32)

    return pl.pallas_call(
        body, grid=(m // tm,), name="in_proj",
        in_specs=[_token_tiles(D_MODEL, nt), _full((ROW_TILE, D_MODEL)), _full((1, D_MODEL)),
                  pl.BlockSpec(memory_space=pl.ANY)],
        out_specs=[pl.BlockSpec((tm, D_MODEL), lambda i: (i, 0)), pl.BlockSpec((tm, PROJ_W), lambda i: (i, 0))],
        out_shape=[jax.ShapeDtypeStruct((m, D_MODEL), MXU_DTYPE), jax.ShapeDtypeStruct((m, PROJ_W), F32)],
        scratch_shapes=[pltpu.VMEM((D_MODEL, PROJ_W), MXU_DTYPE), pltpu.SemaphoreType.DMA],
        compiler_params=_params("arbitrary"),
    )(x, lead, w1, win)


def _ffn_fwd_bwd(x, lead, y, tgt, wout, w2n, wff1, wff2, wfn):
    nt = x.shape[0] // ROW_TILE
    m = x.shape[0] + ROW_TILE
    tm = ROW_TILE
    nj = D_FF // 1024

    def body(x_ref, lead_ref, y_ref, tgt_ref, w2n_ref, wfn_ref, wout_hbm, wff1_hbm, wff2_hbm,
             loss_ref, gwf_ref, gw2_ref, ff_ref, da_ref, hn2_ref, dh1_ref, dh2_ref, dy_ref,
             wout_v, wff1_v, wff2_v, a_s, sems):
        i = pl.program_id(0)
        hp = jnp.where(i == nt, lead_ref[...], x_ref[...])

        @pl.when(i == 0)
        def _():
            cps = [pltpu.make_async_copy(s, d, sems.at[k])
                   for k, (s, d) in enumerate(((wout_hbm, wout_v), (wff1_hbm, wff1_v), (wff2_hbm, wff2_v)))]
            for cp in cps:
                cp.start()
            for cp in cps:
                cp.wait()
            loss_ref[...] = jnp.zeros_like(loss_ref)
            gwf_ref[...] = jnp.zeros_like(gwf_ref)
            gw2_ref[...] = jnp.zeros_like(gw2_ref)

        h1 = hp + jnp.dot(y_ref[...], wout_v[...], preferred_element_type=F32)
        r2 = lax.rsqrt(jnp.mean(h1 * h1, axis=-1, keepdims=True) + EPS)
        n2 = h1 * r2
        w2n_row = w2n_ref[...]
        hn2 = _mx(n2 * w2n_row)
        hn2_ref[...] = hn2
        h2 = h1
        for j in range(nj):
            js = slice(j * 1024, (j + 1) * 1024)
            a = jnp.dot(hn2, wff1_v[:, js], preferred_element_type=F32)
            a_s[:, js] = a
            ra = jnp.maximum(a, 0.0)
            ff = _mx(ra * ra)
            ff_ref[:, js] = ff
            h2 = h2 + jnp.dot(ff, wff2_v[js, :], preferred_element_type=F32)

        r3 = lax.rsqrt(jnp.mean(h2 * h2, axis=-1, keepdims=True) + EPS)
        n3 = h2 * r3
        wf_row = wfn_ref[...]
        err = n3 * wf_row - tgt_ref[...]
        tokf = (i < nt).astype(F32)
        loss_ref[...] += 0.5 * jnp.sum(jnp.mean(err * err, axis=-1, keepdims=True) * tokf)
        dout = err * (tokf / D_MODEL)
        gwf_ref[...] += jnp.sum(dout * n3, axis=0, keepdims=True)
        dn3 = dout * wf_row
        dh2 = r3 * (dn3 - n3 * jnp.mean(dn3 * n3, axis=-1, keepdims=True))
        dh2m = _mx(dh2)
        dh2_ref[...] = dh2m

        dhn2 = jnp.zeros((tm, D_MODEL), F32)
        for j in range(nj):
            js = slice(j * 1024, (j + 1) * 1024)
            dff = lax.dot_general(dh2m, wff2_v[js, :], (((1,), (1,)), ((), ())), preferred_element_type=F32)
            da = _mx(dff * (2.0 * jnp.maximum(a_s[:, js], 0.0)))
            da_ref[:, js] = da
            dhn2 = dhn2 + lax.dot_general(da, wff1_v[:, js], (((1,), (1,)), ((), ())), preferred_element_type=F32)
        gw2_ref[...] += jnp.sum(dhn2 * n2, axis=0, keepdims=True)
        dn2 = dhn2 * w2n_row
        dh1 = dh2 + r2 * (dn2 - n2 * jnp.mean(dn2 * n2, axis=-1, keepdims=True))
        dh1_ref[...] = dh1
        dy_ref[...] = lax.dot_general(_mx(dh1), wout_v[...], (((1,), (1,)), ((), ())), preferred_element_type=F32)

    rows = lambda w: pl.BlockSpec((tm, w), lambda i: (i, 0))
    hbm = pl.BlockSpec(memory_space=pl.ANY)
    return pl.pallas_call(
        body, grid=(m // tm,), name="ffn_fwd_bwd",
        in_specs=[_token_tiles(D_MODEL, nt), _full((ROW_TILE, D_MODEL)), rows(D_MIX), _token_tiles(D_MODEL, nt),
                  _full((1, D_MODEL)), _full((1, D_MODEL)), hbm, hbm, hbm],
        out_specs=[_full((1, 128)), _full((1, D_MODEL)), _full((1, D_MODEL)), rows(D_FF), rows(D_FF), rows(D_MODEL),
                   rows(D_MODEL), rows(D_MODEL), rows(D_MIX)],
        out_shape=[jax.ShapeDtypeStruct((1, 128), F32), jax.ShapeDtypeStruct((1, D_MODEL), F32),
                   jax.ShapeDtypeStruct((1, D_MODEL), F32), jax.ShapeDtypeStruct((m, D_FF), MXU_DTYPE),
                   jax.ShapeDtypeStruct((m, D_FF), MXU_DTYPE), jax.ShapeDtypeStruct((m, D_MODEL), MXU_DTYPE),
                   jax.ShapeDtypeStruct((m, D_MODEL), F32), jax.ShapeDtypeStruct((m, D_MODEL), MXU_DTYPE),
                   jax.ShapeDtypeStruct((m, D_MIX), F32)],
        scratch_shapes=[pltpu.VMEM((D_MIX, D_MODEL), MXU_DTYPE), pltpu.VMEM((D_MODEL, D_FF), MXU_DTYPE),
                        pltpu.VMEM((D_FF, D_MODEL), MXU_DTYPE), pltpu.VMEM((tm, D_FF), F32),
                        pltpu.SemaphoreType.DMA((3,))],
        compiler_params=_params("arbitrary"),
    )(x, lead, y, tgt, w2n, wfn, wout, wff1, wff2)


def _in_proj_bwd(dproj, x, lead, dh1, w1, win, after):
    nt = x.shape[0] // ROW_TILE
    m = x.shape[0] + ROW_TILE
    tm = ROW_TILE

    def body(dp_ref, x_ref, lead_ref, dh1_ref, w1_ref, win_hbm, after_ref, gx_ref, gw1_ref, gmeta_ref, win_v, sem):
        i = pl.program_id(0)

        @pl.when(i == 0)
        def _():
            cp = pltpu.make_async_copy(win_hbm, win_v, sem)
            cp.start()
            cp.wait()
            gw1_ref[...] = jnp.zeros_like(gw1_ref)
            gmeta_ref[...] = jnp.zeros_like(gmeta_ref)

        dhn = lax.dot_general(dp_ref[...], win_v[...], (((1,), (1,)), ((), ())), preferred_element_type=F32)
        x = jnp.where(i == nt, lead_ref[...], x_ref[...])
        r = lax.rsqrt(jnp.mean(x * x, axis=-1, keepdims=True) + EPS)
        n = x * r
        gw1_ref[...] += jnp.sum(dhn * n, axis=0, keepdims=True)
        dn = dhn * w1_ref[...]
        dh0 = dh1_ref[...] + r * (dn - n * jnp.mean(dn * n, axis=-1, keepdims=True))

        @pl.when(i < nt)
        def _():
            gx_ref[...] = dh0

        @pl.when(i == nt)
        def _():
            gmeta_ref[...] = dh0[PAD_ROWS:LEAD, :] + dh0[LEAD + PAD_ROWS:2 * LEAD, :]

    rows = lambda w: pl.BlockSpec((tm, w), lambda i: (i, 0))
    hbm = pl.BlockSpec(memory_space=pl.ANY)
    return pl.pallas_call(
        body, grid=(m // tm,), name="in_proj_bwd",
        in_specs=[rows(PROJ_W), _token_tiles(D_MODEL, nt), _full((ROW_TILE, D_MODEL)), rows(D_MODEL),
                  _full((1, D_MODEL)), hbm, hbm],
        out_specs=[_token_tiles(D_MODEL, nt), _full((1, D_MODEL)), _full((N_META, D_MODEL))],
        out_shape=[jax.ShapeDtypeStruct(x.shape, F32), jax.ShapeDtypeStruct((1, D_MODEL), F32),
                   jax.ShapeDtypeStruct((N_META, D_MODEL), F32)],
        scratch_shapes=[pltpu.VMEM((D_MODEL, PROJ_W), MXU_DTYPE), pltpu.SemaphoreType.DMA],
        compiler_params=_params("arbitrary"),
    )(dproj, x, lead, dh1, w1, win, after)


MXU_DEPTH = 256


def _row_slab(m, cap):
    return max(k for k in range(MXU_DEPTH, cap + 1, MXU_DEPTH) if m % k == 0)


def _tn_matmul(a, b, name, tka, max_slab=768, tn=512):
    m, ka = a.shape
    nb = b.shape[1]
    tkm = _row_slab(m, max_slab)
    n_steps = m // tkm

    def body(a_ref, b_ref, o_ref, omx_ref):
        k = pl.program_id(1)

        @pl.when(k == 0)
        def _():
            o_ref[...] = jnp.zeros_like(o_ref)

        at = _mx(a_ref[...])
        for j in range(0, nb, tn):
            w = min(tn, nb - j)
            o_ref[:, j:j + w] += lax.dot_general(at, _mx(b_ref[:, j:j + w]), (((0,), (0,)), ((), ())),
                                                 preferred_element_type=F32)

        @pl.when(k == n_steps - 1)
        def _():
            omx_ref[...] = _mx(o_ref[...])

    out = pl.BlockSpec((tka, nb), lambda i, k: (i, 0))
    return pl.pallas_call(
        body, grid=(ka // tka, n_steps), name=name,
        in_specs=[pl.BlockSpec((tkm, tka), lambda i, k: (k, i)), pl.BlockSpec((tkm, nb), lambda i, k: (k, 0))],
        out_specs=[out, out],
        out_shape=[jax.ShapeDtypeStruct((ka, nb), F32), jax.ShapeDtypeStruct((ka, nb), MXU_DTYPE)],
        compiler_params=_params("arbitrary", "arbitrary"),
    )(a, b)


def _tn_matmul_banded(a, b, core, name, band, tka, tn=512):
    m, ka = a.shape
    nb = b.shape[1]
    tkm = _row_slab(m, 768)
    n_steps = m // tkm

    def body(core_ref, a_ref, b_ref, own_ref, sib_ref, acc):
        k = pl.program_id(1)

        @pl.when(k == 0)
        def _():
            acc[...] = jnp.zeros_like(acc)

        at = _mx(a_ref[...])
        for j in range(0, nb, tn):
            w = min(tn, nb - j)
            acc[:, j:j + w] += lax.dot_general(at, _mx(b_ref[:, j:j + w]), (((0,), (0,)), ((), ())),
                                               preferred_element_type=F32)

        for c in range(2):
            @pl.when((k == n_steps - 1) & (core_ref[0] == c))
            def _():
                for chip in range(4):
                    mine, other = 2 * chip + c, 2 * chip + 1 - c
                    own_ref[chip] = acc[:, mine * band:(mine + 1) * band]
                    sib_ref[chip] = _mx(acc[:, other * band:(other + 1) * band])

    out = pl.BlockSpec((4, tka, band), lambda i, k, c: (0, i, 0))
    grid_spec = pltpu.PrefetchScalarGridSpec(
        num_scalar_prefetch=1, grid=(ka // tka, n_steps),
        in_specs=[pl.BlockSpec((tkm, tka), lambda i, k, c: (k, i)), pl.BlockSpec((tkm, nb), lambda i, k, c: (k, 0))],
        out_specs=[out, out], scratch_shapes=[pltpu.VMEM((tka, nb), F32)])
    return pl.pallas_call(
        body, grid_spec=grid_spec, name=name,
        out_shape=[jax.ShapeDtypeStruct((4, ka, band), F32), jax.ShapeDtypeStruct((4, ka, band), MXU_DTYPE)],
        compiler_params=_params("arbitrary", "arbitrary"),
    )(core, a, b)


def _assemble_bands(g, width):
    n, rows, band = g.shape
    tr = 256

    def body(g_ref, o_ref):
        parts = [g_ref[j] for j in range(n)] + [jnp.zeros((tr, width - n * band), g.dtype)]
        o_ref[...] = jnp.concatenate(parts, axis=1)

    return pl.pallas_call(
        body, grid=(rows // tr,), name="assemble_w_in",
        in_specs=[pl.BlockSpec((n, tr, band), lambda i: (0, i, 0))],
        out_specs=pl.BlockSpec((tr, width), lambda i: (i, 0)),
        out_shape=jax.ShapeDtypeStruct((rows, width), g.dtype),
        compiler_params=_params("arbitrary"),
    )(g)


def _chunk_block(b, c, nb, nc):
    return jnp.where(c == 0, nb * (nc - 1) + b, b * (nc - 1) + c - 1)

def _mixer_fwd(proj, cw, cb, dt_bias, a_log, d_x, nw, pool_w, pool_scale, nb, shards, by_cols):
    m = proj.shape[0]
    nc = m // nb // CHUNK
    ns = len(shards)
    row_widths = [D_XBC, 128, 128, D_SSM, D_SSM, D_POOL]
    rows = jnp.concatenate([cb, dt_bias, a_log, d_x, nw, pool_scale], axis=1)
    pairs = (nc - 1) // 2
    n_steps = nb * pairs + 1

    def block_of(step):
        return jnp.where(step == 0, nb * pairs, step - 1)

    def chunks_of(step):
        lead = step == 0
        b = jnp.maximum(step - 1, 0) // pairs
        j = jnp.maximum(step - 1, 0) - b * pairs
        return [(jnp.where(lead, 0, b), jnp.where(lead, 0, 2 * j + 1), 0),
                (jnp.where(lead, 1, b), jnp.where(lead, 0, 2 * j + 2), CHUNK)]

    def body(p_ref, cw_ref, rows_ref, pw_ref, *rest):
        shard_refs, (y_ref, ypre_ref, pre_ref, st_ref) = rest[:ns], rest[ns:ns + 4]
        gathered_refs, carries, constants_refs = rest[ns + 4:2 * ns + 4], rest[2 * ns + 4:2 * ns + 7], rest[2 * ns + 7:2 * ns + 10]
        gather = _Gather(shard_refs, gathered_refs, by_cols, *rest[2 * ns + 10:])
        step = pl.program_id(0)

        @pl.when(step == 0)
        def _():
            gather.start()
            _fill_ssd_constants(**dict(zip(["e", "et_f32", "tril"], constants_refs)))

        @pl.when(step == n_steps // 2)
        def _():
            gather.forward()

        shared = (cw_ref, pw_ref, _row_views(rows_ref, row_widths), constants_refs)
        for b, c, row0 in chunks_of(step):
            at = lambda ref: ref.at[row0:row0 + CHUNK]
            one_chunk(c, at(p_ref), at(y_ref), at(ypre_ref), at(pre_ref), st_ref.at[row0 // CHUNK], *shared,
                      [carry.at[b] for carry in carries])

        @pl.when(step == n_steps - 1)
        def _():
            gather.finish()

    def one_chunk(c, p_ref, y_ref, ypre_ref, pre_ref, st_ref, cw_ref, pw_ref, row_refs, constants_refs, carries):
        cb_ref, dtb_ref, alog_ref, dx_ref, nw_ref, ps_ref = row_refs
        e_ref, et_ref, tril_ref = constants_refs
        xtail, utail, state = carries

        @pl.when(c == 0)
        def _():
            xtail[...] = jnp.zeros_like(xtail)
            utail[...] = jnp.zeros_like(utail)
            state[...] = jnp.zeros_like(state)

        valid = (c > 0) | (lax.broadcasted_iota(jnp.int32, (CHUNK, 1), 0) >= PAD_ROWS)

        u = p_ref[:, 0:D_POOL]
        inv_cnt, lane = _pool_inv_count(c)
        win = _pool_window_sums(jnp.concatenate([utail[...], u], axis=0), lane)
        utail[...] = u[CHUNK - HALO:, :]
        pooled = win * inv_cnt - u
        mixed = jnp.concatenate(
            [_dot(pooled[:, g * 128:(g + 1) * 128], pw_ref[g]) for g in range(len(POOL_WINDOWS))], axis=1)
        y_ref[:, 0:D_POOL] = _mx(mixed * ps_ref[...])

        xbc = p_ref[:, OFF_X:OFF_X + D_XBC]
        pre = _conv_pre(jnp.concatenate([xtail[...], xbc], axis=0), xbc, cw_ref[...], cb_ref[...])
        xtail[...] = xbc[CHUNK - HALO:, :]
        pre_ref[...] = pre
        xc = pre * _sigmoid(pre)
        dt, _, a_col, _, _ = _dt_and_cumsum(p_ref[:, OFF_DT:OFF_DT + 128], dtb_ref[...], alog_ref[...], valid,
                                            tril_ref[...])
        s_prev = state[...]
        st_ref[...] = s_prev
        yp, s_new = _ssd_chunk_fwd(xc[:, 0:D_SSM], xc[:, D_SSM:D_SSM + 512], xc[:, D_SSM + 512:], dt, a_col, s_prev,
                                   dx_ref[...], e_ref[...], et_ref[...])
        state[...] = s_new
        ypre_ref[...] = yp
        z = p_ref[:, OFF_Z:OFF_Z + D_SSM]
        yz = yp * (z * _sigmoid(z))
        outs = []
        for g in range(N_GROUPS):
            gs = slice(g * GROUP_CH, (g + 1) * GROUP_CH)
            r = lax.rsqrt(jnp.mean(yz[:, gs] * yz[:, gs], axis=-1, keepdims=True) + EPS)
            outs.append(yz[:, gs] * r)
        y_ref[:, D_POOL:] = _mx(jnp.concatenate(outs, axis=1) * nw_ref[...])

    blk = lambda w: pl.BlockSpec((2 * CHUNK, w), lambda s: (block_of(s), 0))
    hbm = pl.BlockSpec(memory_space=pl.ANY)
    outs = pl.pallas_call(
        body, grid=(n_steps,), name="mixer_fwd",
        in_specs=[blk(PROJ_W), _full((4, D_XBC)), _full((1, sum(row_widths))), _full((4, 128, 128))] + [hbm] * ns,
        out_specs=[blk(D_MIX), blk(D_SSM), blk(D_XBC),
                   pl.BlockSpec((2, D_SSM, D_STATE), lambda s: (block_of(s), 0, 0))] + [hbm] * ns,
        out_shape=[jax.ShapeDtypeStruct((m, D_MIX), MXU_DTYPE), jax.ShapeDtypeStruct((m, D_SSM), F32),
                   jax.ShapeDtypeStruct((m, D_XBC), F32), jax.ShapeDtypeStruct((m // CHUNK, D_SSM, D_STATE), F32)]
        + _Gather.out_shapes(shards, by_cols),
        scratch_shapes=[pltpu.VMEM((nb, HALO, D_XBC), F32), pltpu.VMEM((nb, HALO, D_POOL), F32),
                        pltpu.VMEM((nb, D_SSM, D_STATE), F32)] + _ssd_constant_scratch(["e", "et_f32", "tril"])
        + _Gather.scratch(ns),
        compiler_params=_params("arbitrary"),
    )(proj, cw, rows, pool_w, *shards)
    return outs[0], outs[1], outs[2], outs[3], outs[4:]


def _mixer_bwd(proj, dy, ypre, conv_pre, states, cw, dt_bias, a_log, d_x, nw, pool_w, pool_scale, nb, chip_sums):
    m = proj.shape[0]
    nc = m // nb // CHUNK
    hb = CHUNK // HALO
    ns = len(chip_sums)
    row_widths = [128, 128, D_SSM, D_SSM, D_POOL]
    rows = jnp.concatenate([dt_bias, a_log, d_x, nw, pool_scale], axis=1)
    grad_row_widths = [D_XBC, 128, 128, 128, D_SSM, D_POOL]
    constants = ["e", "et", "et_f32", "tril", "triu"]

    pairs = (nc - 1) // 2
    n_steps = nb * pairs + 1

    def block_of(step):
        b = jnp.minimum(step // pairs, nb - 1)
        j = pairs - 1 - (step - b * pairs)
        return jnp.where(step == n_steps - 1, nb * pairs, b * pairs + j)

    def chunks_of(step):
        lead = step == n_steps - 1
        b = jnp.minimum(step // pairs, nb - 1)
        j = pairs - 1 - (step - b * pairs)
        return [(jnp.where(lead, 0, b), jnp.where(lead, 0, 2 * j + 2), jnp.where(lead, 0, CHUNK)),
                (jnp.where(lead, 1, b), jnp.where(lead, 0, 2 * j + 1), jnp.where(lead, CHUNK, 0))]

    def body(p_ref, halo_ref, dy_ref, ypre_ref, pre_ref, st_ref, cw_ref, rows_ref, pw_ref, *rest):
        cs_refs = rest[:ns]
        dp_ref, gcw_ref, grows_ref, gpw_ref = rest[ns:ns + 4]
        part_refs, carries, constants_refs = rest[ns + 4:2 * ns + 4], rest[2 * ns + 4:2 * ns + 7], rest[2 * ns + 7:2 * ns + 12]
        exchange = _ChipExchange(cs_refs, part_refs, [False] * ns, *rest[2 * ns + 12:])
        step = pl.program_id(0)

        @pl.when(step == 0)
        def _():
            exchange.start()
            _fill_ssd_constants(**dict(zip(constants, constants_refs)))
            for r in (gcw_ref, grows_ref, gpw_ref):
                r[...] = jnp.zeros_like(r)

        shared = (cw_ref, pw_ref, _row_views(rows_ref, row_widths), gcw_ref, gpw_ref,
                  _row_views(grows_ref, grad_row_widths), constants_refs)
        halos = [p_ref[CHUNK - HALO:CHUNK, 0:D_POOL], halo_ref[...]]
        for (b, c, row0), halo in zip(chunks_of(step), halos):
            rows_here = pl.ds(pl.multiple_of(row0, CHUNK), CHUNK)
            at = lambda ref: ref.at[rows_here]
            one_chunk(b, c, at(p_ref), halo, at(dy_ref), at(ypre_ref), at(pre_ref), st_ref.at[row0 // CHUNK], at(dp_ref),
                      *shared, [carry.at[b] for carry in carries])

        @pl.when(step == n_steps - 1)
        def _():
            exchange.finish()

    def one_chunk(b, c, p_ref, halo, dy_ref, ypre_ref, pre_ref, st_ref, dp_ref, cw_ref, pw_ref, row_refs, gcw_ref,
                  gpw_ref, grad_row_refs, constants_refs, carries):
        dtb_ref, alog_ref, dx_ref, nw_ref, ps_ref = row_refs
        gcb_ref, gdtb_ref, galog_ref, gd_ref, gnw_ref, gps_ref = grad_row_refs
        ds_carry, dpre_next, dq_next = carries
        e_ref, et_ref, etf_ref, tril_ref, triu_ref = constants_refs

        @pl.when(c == nc - 1)
        def _():
            ds_carry[...] = jnp.zeros_like(ds_carry)
            dpre_next[...] = jnp.zeros_like(dpre_next)
            dq_next[...] = jnp.zeros_like(dq_next)

        valid = (c > 0) | (lax.broadcasted_iota(jnp.int32, (CHUNK, 1), 0) >= PAD_ROWS)
        first = c > 0

        u = p_ref[:, 0:D_POOL]
        u_halo = jnp.where(first, halo, 0.0)
        inv_cnt, lane = _pool_inv_count(c)
        pooled = _pool_window_sums(jnp.concatenate([u_halo, u], axis=0), lane) * inv_cnt - u
        dyp = dy_ref[:, 0:D_POOL]
        ps = ps_ref[...]
        dmixed = dyp * ps
        mixed, dpooled = [], []
        for g in range(len(POOL_WINDOWS)):
            gsl = slice(g * 128, (g + 1) * 128)
            pw = pw_ref[g]
            mixed.append(_dot(pooled[:, gsl], pw))
            dpooled.append(_dot_nt(dmixed[:, gsl], pw))
            gpw_ref[g] += _dot_tn(pooled[:, gsl], dmixed[:, gsl])
        gps_ref[...] += jnp.sum(dyp * jnp.concatenate(mixed, axis=1), axis=0, keepdims=True)
        dpooled = jnp.concatenate(dpooled, axis=1)
        dq = dpooled * inv_cnt
        du = _pool_window_sums_ahead(jnp.concatenate([dq, dq_next[...]], axis=0), lane) - dpooled
        dq_next[...] = dq[0:HALO, :]
        dp_ref[:, 0:D_POOL] = _mx(du)

        yp = ypre_ref[...]
        z = p_ref[:, OFF_Z:OFF_Z + D_SSM]
        sz, dsz = _silu_and_grad(z)
        yz = yp * sz
        do = dy_ref[:, D_POOL:]
        nw_row = nw_ref[...]
        dyz = []
        gnw = []
        for g in range(N_GROUPS):
            gs = slice(g * GROUP_CH, (g + 1) * GROUP_CH)
            r = lax.rsqrt(jnp.mean(yz[:, gs] * yz[:, gs], axis=-1, keepdims=True) + EPS)
            n = yz[:, gs] * r
            gnw.append(jnp.sum(do[:, gs] * n, axis=0, keepdims=True))
            dn = do[:, gs] * nw_row[:, gs]
            dyz.append(r * (dn - n * jnp.mean(dn * n, axis=-1, keepdims=True)))
        gnw_ref[...] += jnp.concatenate(gnw, axis=1)
        dyz = jnp.concatenate(dyz, axis=1)
        dp_ref[:, OFF_Z:OFF_Z + D_SSM] = _mx(dyz * yp * dsz)
        dyp_ssm = dyz * sz

        xc, dsilu = _silu_and_grad(pre_ref[...])
        dtr = p_ref[:, OFF_DT:OFF_DT + 128]
        dt, a_row, a_col, dt_pre, head = _dt_and_cumsum(dtr, dtb_ref[...], alog_ref[...], valid, tril_ref[...])
        dxs, dbm, dcm, ddt, da, dd, ds_prev = _ssd_chunk_bwd(
            xc[:, 0:D_SSM], xc[:, D_SSM:D_SSM + 512], xc[:, D_SSM + 512:], dt, a_row, a_col, st_ref[...],
            ds_carry[...], dyp_ssm, dx_ref[...], e_ref[...], et_ref[...], etf_ref[...], triu_ref[...])
        ds_carry[...] = ds_prev
        gd_ref[...] += dd
        galog_ref[...] += da * a_row
        ddtr = jnp.where(valid & head, ddt * _sigmoid(dt_pre), 0.0)
        gdtb_ref[...] += jnp.sum(ddtr, axis=0, keepdims=True)
        dp_ref[:, OFF_DT:OFF_DT + 128] = _mx(ddtr)

        dpre = jnp.concatenate([dxs, dbm, dcm], axis=1) * dsilu
        gcb_ref[...] += jnp.sum(dpre, axis=0, keepdims=True)
        dext = jnp.concatenate([dpre, dpre_next[...]], axis=0)
        dpre_next[...] = dpre[0:HALO, :]
        ups = [_shift_up(dext, 3 - k) for k in range(4)]
        xbc = p_ref[:, OFF_X:OFF_X + D_XBC]
        gcw_ref[...] += jnp.concatenate([jnp.sum(xbc * ups[k], axis=0, keepdims=True) for k in range(4)], axis=0)
        cw = cw_ref[...]
        dp_ref[:, OFF_X:OFF_X + D_XBC] = _mx(cw[3:4, :] * ups[3] + cw[2:3, :] * ups[2]
                                             + cw[1:2, :] * ups[1] + cw[0:1, :] * ups[0])

    blk = lambda w: pl.BlockSpec((2 * CHUNK, w), lambda s: (block_of(s), 0))

    def halo_rows(s):
        b = jnp.minimum(s // pairs, nb - 1)
        below = block_of(s) * 2 * hb - 1
        return jnp.where(block_of(s) == b * pairs, _chunk_block(b, 0, nb, nc) * hb + hb - 1, below)

    hbm = pl.BlockSpec(memory_space=pl.ANY)
    outs = pl.pallas_call(
        body, grid=(n_steps,), name="mixer_bwd",
        in_specs=[blk(PROJ_W), pl.BlockSpec((HALO, D_POOL), lambda s: (halo_rows(s), 0)), blk(D_MIX), blk(D_SSM),
                  blk(D_XBC), pl.BlockSpec((2, D_SSM, D_STATE), lambda s: (block_of(s), 0, 0)),
                  _full((4, D_XBC)), _full((1, sum(row_widths))), _full((4, 128, 128))] + [hbm] * ns,
        out_specs=[blk(PROJ_W), _full((4, D_XBC)), _full((1, sum(grad_row_widths))), _full((4, 128, 128))] + [hbm] * ns,
        out_shape=[jax.ShapeDtypeStruct((m, PROJ_W), MXU_DTYPE), jax.ShapeDtypeStruct((4, D_XBC), F32),
                   jax.ShapeDtypeStruct((1, sum(grad_row_widths)), F32), jax.ShapeDtypeStruct((4, 128, 128), F32)]
        + _ChipExchange.out_shapes(chip_sums, [False] * ns),
        scratch_shapes=[pltpu.VMEM((nb, D_SSM, D_STATE), F32), pltpu.VMEM((nb, HALO, D_XBC), F32),
                        pltpu.VMEM((nb, HALO, D_POOL), F32)] + _ssd_constant_scratch(constants)
        + _ChipExchange.scratch(ns),
        compiler_params=_params("arbitrary"),
    )(proj, proj, dy, ypre, conv_pre, states, cw, rows, pool_w, *chip_sums)
    dproj, g_cw, g_rows, g_pw = outs[:4]
    offs = np.cumsum([0] + grad_row_widths)
    g_cb, g_dtb, g_alog, g_d, g_nw, g_ps = (g_rows[:, a:b] for a, b in zip(offs[:-1], offs[1:]))
    return (dproj, g_cw, g_cb, g_dtb, g_alog, g_d, g_nw, g_pw, g_ps), outs[4:]


MESH_IDS = pl.DeviceIdType.MESH
_HBM = pl.BlockSpec(memory_space=pltpu.HBM)


def _coords():
    return lax.axis_index("x"), lax.axis_index("y"), lax.axis_index("c")


def _other_chips(x, y):
    return [(1 - x, y), (x, 1 - y), (1 - x, 1 - y)]


class _Gather:
    def __init__(self, ins, outs, by_cols, send_sems, recv_sems, local_sems):
        self.ins, self.outs, self.by_cols, self.n = ins, outs, by_cols, len(ins)
        self.send_sems, self.recv_sems, self.local_sems = send_sems, recv_sems, local_sems
        self.x, self.y, self.c = _coords()
        self.me, self.sibling = (self.x, self.y, self.c), (self.x, self.y, 1 - self.c)
        self.chips = _other_chips(self.x, self.y)

    @staticmethod
    def scratch(n):
        return [pltpu.SemaphoreType.DMA((7 * n,)), pltpu.SemaphoreType.DMA((7 * n,)), pltpu.SemaphoreType.DMA((n,))]

    @staticmethod
    def out_shapes(shards, by_cols):
        return [jax.ShapeDtypeStruct((s.shape[0], N_DEV * s.shape[1]) if cols else (N_DEV,) + s.shape, s.dtype)
                for s, cols in zip(shards, by_cols)]

    def _block(self, t, device):
        idx = 4 * device[0] + 2 * device[1] + device[2]
        if not self.by_cols[t]:
            return self.outs[t].at[idx]
        w = self.ins[t].shape[1]
        return self.outs[t].at[:, pl.ds(pl.multiple_of(idx * w, w), w)]

    def _copy(self, t, k, block, to, own=False):
        dst = self._block(t, block)
        return pltpu.make_async_remote_copy(
            src_ref=self.ins[t] if own else dst, dst_ref=dst, send_sem=self.send_sems.at[t * 7 + k],
            recv_sem=self.recv_sems.at[t * 7 + k], device_id=to, device_id_type=MESH_IDS)

    def _mine(self):
        return [pltpu.make_async_copy(self.ins[t], self._block(t, self.me), self.local_sems.at[t])
                for t in range(self.n)]

    def _first(self):
        cps = []
        for t in range(self.n):
            cps.append(self._copy(t, 0, self.me, self.sibling, own=True))
            cps += [self._copy(t, 1 + j, self.me, (*chip, self.c), own=True) for j, chip in enumerate(self.chips)]
        return cps

    def _passed(self):
        return [self._copy(t, 4 + j, (*chip, self.c), self.sibling)
                for j, chip in enumerate(self.chips) for t in range(self.n)]

    def start(self):
        first = self._first()
        over_ici = [cp for i, cp in enumerate(first) if i % 4]
        to_sibling = first[::4]
        for cp in over_ici + to_sibling + self._mine():
            cp.start()

    def forward(self):
        for j, chip in enumerate(self.chips):
            for t in range(self.n):
                self._copy(t, 1 + j, (*chip, self.c), self.me).wait_recv()
                self._copy(t, 4 + j, (*chip, self.c), self.sibling).start()

    def finish(self):
        for t in range(self.n):
            self._copy(t, 0, self.sibling, self.me).wait_recv()
            for j, chip in enumerate(self.chips):
                self._copy(t, 4 + j, (*chip, 1 - self.c), self.me).wait_recv()
        for cp in self._first() + self._passed():
            cp.wait_send()
        for cp in self._mine():
            cp.wait()


def _weight_gather(shards):
    n = len(shards)

    def body(*refs):
        g = _Gather(refs[:n], refs[n:2 * n], [False] * n, *refs[2 * n:])
        g.start()
        g.forward()
        g.finish()

    return pl.pallas_call(
        body, name="weight_gather",
        in_specs=[_HBM] * n, out_specs=[_HBM] * n,
        out_shape=_Gather.out_shapes(shards, [False] * n),
        scratch_shapes=_Gather.scratch(n),
    )(*shards)


def _grad_exchange_d2d(gs, name, swapped=()):
    n, ns = len(gs), len(swapped)

    def body(*refs):
        ins, whole_ins = refs[:n], refs[n:n + ns]
        got, whole_got = refs[n + ns:2 * n + ns], refs[2 * n + ns:2 * (n + ns)]
        send_sems, recv_sems = refs[2 * (n + ns):]
        x, y, c = _coords()
        theirs = lambda t, k: ins[t].at[k] if len(ins[t].shape) == 3 else ins[t].at[k, 1 - c]
        pairs = [(theirs(t, k), got[t].at[k]) for t in range(n) for k in range(4)] + list(zip(whole_ins, whole_got))
        remote = [pltpu.make_async_remote_copy(
            src_ref=src, dst_ref=dst, send_sem=send_sems.at[i], recv_sem=recv_sems.at[i], device_id=(x, y, 1 - c),
            device_id_type=MESH_IDS) for i, (src, dst) in enumerate(pairs)]
        for cp in remote:
            cp.start()
        for cp in remote:
            cp.wait_recv()
        for cp in remote:
            cp.wait_send()

    outs = pl.pallas_call(
        body, name=name,
        in_specs=[_HBM] * (n + ns), out_specs=[_HBM] * (n + ns),
        out_shape=[jax.ShapeDtypeStruct((4,) + g.shape[-2:], g.dtype) for g in gs]
        + [jax.ShapeDtypeStruct(a.shape, a.dtype) for a in swapped],
        scratch_shapes=[pltpu.SemaphoreType.DMA((4 * n + ns,)), pltpu.SemaphoreType.DMA((4 * n + ns,))],
    )(*gs, *swapped)
    return outs[:n], outs[n:]


def _small_allreduce(pack):
    rows = pack.shape[0]

    def body(p_ref, o_ref, sib_ref, parts_ref, send_sems, recv_sems):
        x, y, c = _coords()
        my_chip = 2 * x + y
        swap = pltpu.make_async_remote_copy(src_ref=p_ref, dst_ref=sib_ref, send_sem=send_sems.at[0],
                                            recv_sem=recv_sems.at[0], device_id=(x, y, 1 - c), device_id_type=MESH_IDS)
        swap.start()
        swap.wait_recv()
        parts_ref[my_chip] = p_ref[...] + sib_ref[...]
        remote = [pltpu.make_async_remote_copy(
            src_ref=parts_ref.at[my_chip], dst_ref=parts_ref.at[my_chip], send_sem=send_sems.at[1 + j],
            recv_sem=recv_sems.at[1 + j], device_id=(cx, cy, c), device_id_type=MESH_IDS)
            for j, (cx, cy) in enumerate(_other_chips(x, y))]
        for cp in remote:
            cp.start()
        for j, (cx, cy) in enumerate(_other_chips(x, y)):
            slot = parts_ref.at[2 * cx + cy]
            pltpu.make_async_remote_copy(src_ref=slot, dst_ref=slot, send_sem=send_sems.at[1 + j],
                                         recv_sem=recv_sems.at[1 + j], device_id=(cx, cy, c),
                                         device_id_type=MESH_IDS).wait_recv()
        o_ref[...] = ((parts_ref[0] + parts_ref[1]) + parts_ref[2]) + parts_ref[3]
        swap.wait_send()
        for cp in remote:
            cp.wait_send()

    vmem = pl.BlockSpec(memory_space=pltpu.VMEM)
    return pl.pallas_call(
        body, name="small_allreduce", in_specs=[vmem], out_specs=vmem,
        out_shape=jax.ShapeDtypeStruct((rows, 128), F32),
        scratch_shapes=[pltpu.VMEM((rows, 128), F32), pltpu.VMEM((4, rows, 128), F32),
                        pltpu.SemaphoreType.DMA((4,)), pltpu.SemaphoreType.DMA((4,))],
    )(pack)


class _ChipExchange:
    def __init__(self, ins, outs, whole, send_sems, recv_sems, local_sems):
        self.ins, self.outs, self.whole, self.n = ins, outs, whole, len(ins)
        self.send_sems, self.recv_sems, self.local_sems = send_sems, recv_sems, local_sems
        self.x, self.y, self.c = _coords()
        self.my_chip = 2 * self.x + self.y
        self.chips = _other_chips(self.x, self.y)

    @staticmethod
    def scratch(n):
        return [pltpu.SemaphoreType.DMA((3 * n,)), pltpu.SemaphoreType.DMA((3 * n,)), pltpu.SemaphoreType.DMA((n,))]

    def _src(self, t, k):
        return self.ins[t] if self.whole[t] else self.ins[t].at[k]

    def _local(self):
        return [pltpu.make_async_copy(self._src(t, self.my_chip), self.outs[t].at[self.my_chip], self.local_sems.at[t])
                for t in range(self.n)]

    def _remote(self):
        return [pltpu.make_async_remote_copy(
            src_ref=self._src(t, 2 * cx + cy), dst_ref=self.outs[t].at[self.my_chip],
            send_sem=self.send_sems.at[t * 3 + j], recv_sem=self.recv_sems.at[t * 3 + j],
            device_id=(cx, cy, self.c), device_id_type=MESH_IDS)
            for t in range(self.n) for j, (cx, cy) in enumerate(self.chips)]

    def start(self):
        for cp in self._remote() + self._local():
            cp.start()

    def finish(self):
        for t in range(self.n):
            for j, (cx, cy) in enumerate(self.chips):
                slot = self.outs[t].at[2 * cx + cy]
                pltpu.make_async_remote_copy(
                    src_ref=slot, dst_ref=slot, send_sem=self.send_sems.at[t * 3 + j],
                    recv_sem=self.recv_sems.at[t * 3 + j], device_id=(cx, cy, self.c),
                    device_id_type=MESH_IDS).wait_recv()
        for cp in self._remote():
            cp.wait_send()
        for cp in self._local():
            cp.wait()

    @staticmethod
    def out_shapes(arrs, whole):
        return [jax.ShapeDtypeStruct(((4,) + a.shape) if w else a.shape, a.dtype) for a, w in zip(arrs, whole)]


_SEMAPHORES = pl.BlockSpec(memory_space=pltpu.SEMAPHORE)
_SIDE_EFFECT = pltpu.SideEffectType.DATAFLOW_SIDE_EFFECTING


def _split_exchange_copies(srcs, lands, whole, send_sems, recv_sems, waiting):
    x, y, c = _coords()
    copies = []
    for t in range(len(srcs)):
        for j, (cx, cy) in enumerate(_other_chips(x, y)):
            src = srcs[t] if whole[t] else srcs[t].at[2 * cx + cy]
            dst = lands[t].at[2 * cx + cy] if waiting else lands[t].at[2 * x + y]
            copies.append(pltpu.make_async_remote_copy(
                src_ref=src, dst_ref=dst, send_sem=send_sems.at[3 * t + j], recv_sem=recv_sems.at[3 * t + j],
                device_id=(cx, cy, c), device_id_type=MESH_IDS))
    return copies


def _chip_exchange_start(arrays, whole, name):
    n = len(arrays)

    def body(*refs):
        srcs, lands = refs[:n], refs[n:2 * n]
        send_sems, recv_sems = refs[2 * n:2 * n + 2]
        for cp in _split_exchange_copies(srcs, lands, whole, send_sems, recv_sems, waiting=False):
            cp.start()
        token = refs[-1]
        token[...] = jnp.zeros_like(token)

    land_shapes = [((4,) + a.shape) if w else a.shape for a, w in zip(arrays, whole)]
    hbm = lambda shape, a: pltpu.HBM(shape, a.dtype)
    outs = pl.pallas_call(
        body, name=name,
        out_shape=(pltpu.SemaphoreType.DMA((3 * n,)), pltpu.SemaphoreType.DMA((3 * n,)),
                   *[hbm(a.shape, a) for a in arrays], *[hbm(s, a) for s, a in zip(land_shapes, arrays)],
                   jax.ShapeDtypeStruct((8, 128), F32)),
        in_specs=(_HBM,) * (2 * n),
        out_specs=(_SEMAPHORES, _SEMAPHORES) + (_HBM,) * (2 * n) + (pl.BlockSpec(memory_space=pltpu.VMEM),),
        input_output_aliases={i: 2 + i for i in range(2 * n)},
        compiler_params=pltpu.CompilerParams(has_side_effects=_SIDE_EFFECT),
    )(*[pltpu.with_memory_space_constraint(a, pltpu.HBM) for a in arrays],
      *[pltpu.with_memory_space_constraint(lax.empty(s, a.dtype), pltpu.HBM) for s, a in zip(land_shapes, arrays)])
    return outs[0], outs[1], outs[2:2 + n], outs[2 + n:2 + 2 * n], outs[-1]


def _chip_exchange_wait(send_sems, recv_sems, srcs, lands, whole, after, name):
    n = len(srcs)

    def body(*refs):
        src_refs, land_refs = refs[:n], refs[n:2 * n]
        for cp in _split_exchange_copies(src_refs, land_refs, whole, refs[2 * n], refs[2 * n + 1], waiting=True):
            cp.wait_send()
            cp.wait_recv()

    outs = pl.pallas_call(
        body, name=name, out_shape=tuple(pltpu.HBM(a.shape, a.dtype) for a in (*srcs, *lands)),
        in_specs=(_HBM,) * (2 * n) + (_SEMAPHORES, _SEMAPHORES, pl.BlockSpec(memory_space=pl.ANY)),
        out_specs=(_HBM,) * (2 * n), input_output_aliases={i: i for i in range(2 * n)},
        compiler_params=pltpu.CompilerParams(has_side_effects=_SIDE_EFFECT),
    )(*srcs, *lands, send_sems, recv_sems, after)
    return outs[:n], outs[n:]


def _sum_two(a, b, name):
    def body(a_ref, b_ref, o_ref):
        o_ref[...] = a_ref[...] + b_ref[...]

    return pl.pallas_call(body, name=name, out_shape=jax.ShapeDtypeStruct(a.shape, a.dtype))(a, b)


def _sum_chips(landed, own, chip, name):
    def body(chip_ref, l_ref, own_ref, o_ref):
        part = lambda k: jnp.where(chip_ref[0] == k, own_ref[...], l_ref[k])
        o_ref[...] = ((part(0) + part(1)) + part(2)) + part(3)

    grid_spec = pltpu.PrefetchScalarGridSpec(
        num_scalar_prefetch=1, grid=(1,),
        in_specs=[pl.BlockSpec(landed.shape, lambda i, c: (0, 0, 0)), pl.BlockSpec(own.shape, lambda i, c: (0, 0))],
        out_specs=pl.BlockSpec(own.shape, lambda i, c: (0, 0)))
    return pl.pallas_call(body, grid_spec=grid_spec, name=name,
                          out_shape=jax.ShapeDtypeStruct(own.shape, own.dtype))(chip, landed, own)


def _row_tile(rows, cols, n_arrays):
    budget = 24 * 1024 * 1024
    padded = -(-cols // 128) * 128
    step = 16 if rows % 16 == 0 else 8
    tr = max(step, budget // (n_arrays * 2 * 4 * padded) // step * step)
    while rows % tr:
        tr -= step
    return tr


def _chip_sum(g, got, core, name):
    rows, cols = g.shape[-2:]
    tr = _row_tile(rows, cols, 3)

    def body(c_ref, a_ref, b_ref, o_ref):
        o_ref[...] = (a_ref[...] + b_ref[...].astype(F32)).astype(o_ref.dtype)

    mine = (pl.BlockSpec((None, tr, cols), lambda k, i, c: (k, i, 0)) if g.ndim == 3
            else pl.BlockSpec((None, None, tr, cols), lambda k, i, c: (k, c[0], i, 0)))
    grid_spec = pltpu.PrefetchScalarGridSpec(
        num_scalar_prefetch=1, grid=(4, rows // tr),
        in_specs=[mine, pl.BlockSpec((None, tr, cols), lambda k, i, c: (k, i, 0))],
        out_specs=pl.BlockSpec((None, tr, cols), lambda k, i, c: (k, i, 0)))
    return pl.pallas_call(body, grid_spec=grid_spec, name=name,
                          out_shape=jax.ShapeDtypeStruct((4, rows, cols), MXU_DTYPE),
                          compiler_params=_params("arbitrary", "arbitrary"))(core, g, got)


def _adamw_math(w, g, m, v):
    m2 = ADAM_B1 * m + (1.0 - ADAM_B1) * g
    v2 = ADAM_B2 * v + (1.0 - ADAM_B2) * (g * g)
    m_hat = m2 / (1.0 - ADAM_B1 ** ADAM_STEP)
    v_hat = v2 / (1.0 - ADAM_B2 ** ADAM_STEP)
    delta = -ADAM_LR * (m_hat / (jnp.sqrt(v_hat) + ADAM_EPS) + ADAM_WD * w)
    return delta, m2, v2


def _adamw(parts, w, m, v, name, own=None, chip=None):
    rows, cols = w.shape
    tr = _row_tile(rows, cols, 11 if own is None else 15)

    def body(*refs):
        if own is None:
            p_ref, w_ref, m_ref, v_ref, g_ref, d_ref, m2_ref, v2_ref = refs
            part = lambda k: p_ref[k].astype(F32)
        else:
            chip_ref, p_ref, own_ref, w_ref, m_ref, v_ref, g_ref, d_ref, m2_ref, v2_ref = refs
            part = lambda k: jnp.where(chip_ref[0] == k, own_ref[k], p_ref[k]).astype(F32)
        g = ((part(0) + part(1)) + part(2)) + part(3)
        d, m2, v2 = _adamw_math(w_ref[...], g, m_ref[...], v_ref[...])
        g_ref[...] = g
        d_ref[...] = d
        m2_ref[...] = m2
        v2_ref[...] = v2

    blk = pl.BlockSpec((tr, cols), lambda i, *_: (i, 0))
    pblk = pl.BlockSpec((4, tr, cols), lambda i, *_: (0, i, 0))
    out = jax.ShapeDtypeStruct((rows, cols), F32)
    if own is None:
        return pl.pallas_call(body, grid=(rows // tr,), name=name, in_specs=[pblk, blk, blk, blk],
                              out_specs=[blk] * 4, out_shape=[out] * 4,
                              compiler_params=_params("arbitrary"))(parts, w, m, v)
    grid_spec = pltpu.PrefetchScalarGridSpec(num_scalar_prefetch=1, grid=(rows // tr,),
                                             in_specs=[pblk, pblk, blk, blk, blk], out_specs=[blk] * 4)
    return pl.pallas_call(body, grid_spec=grid_spec, name=name, out_shape=[out] * 4,
                          compiler_params=_params("arbitrary"))(chip, parts, own, w, m, v)


def _adamw_transposed(parts_t, w, m, v, name):
    rows, cols = w.shape
    tc = 256

    def body(p_ref, w_ref, m_ref, v_ref, g_ref, d_ref, m2_ref, v2_ref):
        part = lambda k: p_ref[k].astype(F32)
        g = (((part(0) + part(1)) + part(2)) + part(3)).T
        d, m2, v2 = _adamw_math(w_ref[...], g, m_ref[...], v_ref[...])
        g_ref[...] = g
        d_ref[...] = d
        m2_ref[...] = m2
        v2_ref[...] = v2

    blk = pl.BlockSpec((rows, tc), lambda i: (0, i))
    out = jax.ShapeDtypeStruct((rows, cols), F32)
    return pl.pallas_call(body, grid=(cols // tc,), name=name,
                          in_specs=[pl.BlockSpec((4, tc, rows), lambda i: (0, i, 0)), blk, blk, blk],
                          out_specs=[blk] * 4, out_shape=[out] * 4,
                          compiler_params=_params("arbitrary"))(parts_t, w, m, v)


def _adamw_small(gs, ws, ms, vs):
    n = len(ws)

    def body(*refs):
        g_refs, w_refs, m_refs, v_refs = (refs[k * n:(k + 1) * n] for k in range(4))
        d_refs, m2_refs, v2_refs = (refs[(4 + k) * n:(5 + k) * n] for k in range(3))
        for t in range(n):
            d, m2, v2 = _adamw_math(w_refs[t][...], g_refs[t][...], m_refs[t][...], v_refs[t][...])
            d_refs[t][...] = d
            m2_refs[t][...] = m2
            v2_refs[t][...] = v2

    outs = pl.pallas_call(body, name="adamw_small",
                          out_shape=[jax.ShapeDtypeStruct(w.shape, F32) for w in ws] * 3)(*gs, *ws, *ms, *vs)
    return outs[:n], outs[n:2 * n], outs[2 * n:]


_PACK_TILE = 8 * 128


def _pack(arrays):
    rows = []
    for a in arrays:
        flat = a.astype(F32).reshape(-1)
        rows.append(jnp.pad(flat, (0, -flat.shape[0] % _PACK_TILE)).reshape(-1, 128))
    return jnp.concatenate(rows, axis=0)


def _unpack(pack, shapes):
    out, r = [], 0
    for s in shapes:
        n = int(np.prod(s))
        out.append(pack[r:r + -(-n // 128)].reshape(-1)[:n].reshape(s))
        r += -(-n // _PACK_TILE) * 8
    return out


def _pad128(v):
    v = v.reshape(1, -1).astype(F32)
    return jnp.pad(v, ((0, 0), (0, 128 - v.shape[1])))


_WEIGHTS = ["meta", "norm_mix_w", "w_in", "pool_w", "pool_scale", "conv_w", "conv_b", "dt_bias", "a_log", "d_skip",
            "ssm_norm_w", "w_out", "norm_ffn_w", "w_ff1", "w_ff2", "norm_f_w"]
_BIG = ["w_in", "w_out", "w_ff1", "w_ff2"]
_SMALL = [n for n in _WEIGHTS if n not in _BIG]


def kernel(x, meta, norm_mix_w, w_in, pool_w, pool_scale, conv_w, conv_b, dt_bias, a_log, d_skip, ssm_norm_w, w_out, norm_ffn_w, w_ff1, w_ff2, norm_f_w, loss_target, m_meta, m_norm_mix_w, m_w_in, m_pool_w, m_pool_scale, m_conv_w, m_conv_b, m_dt_bias, m_a_log, m_d_skip, m_ssm_norm_w, m_w_out, m_norm_ffn_w, m_w_ff1, m_w_ff2, m_norm_f_w, v_meta, v_norm_mix_w, v_w_in, v_pool_w, v_pool_scale, v_conv_w, v_conv_b, v_dt_bias, v_a_log, v_d_skip, v_ssm_norm_w, v_w_out, v_norm_ffn_w, v_w_ff1, v_w_ff2, v_norm_f_w):
    wts = dict(meta=meta, norm_mix_w=norm_mix_w, w_in=w_in, pool_w=pool_w, pool_scale=pool_scale, conv_w=conv_w,
               conv_b=conv_b, dt_bias=dt_bias, a_log=a_log, d_skip=d_skip, ssm_norm_w=ssm_norm_w, w_out=w_out,
               norm_ffn_w=norm_ffn_w, w_ff1=w_ff1, w_ff2=w_ff2, norm_f_w=norm_f_w)
    mom1 = dict(zip(_WEIGHTS, (m_meta, m_norm_mix_w, m_w_in, m_pool_w, m_pool_scale, m_conv_w, m_conv_b, m_dt_bias,
                               m_a_log, m_d_skip, m_ssm_norm_w, m_w_out, m_norm_ffn_w, m_w_ff1, m_w_ff2, m_norm_f_w)))
    mom2 = dict(zip(_WEIGHTS, (v_meta, v_norm_mix_w, v_w_in, v_pool_w, v_pool_scale, v_conv_w, v_conv_b, v_dt_bias,
                               v_a_log, v_d_skip, v_ssm_norm_w, v_w_out, v_norm_ffn_w, v_w_ff1, v_w_ff2, v_norm_f_w)))
    xi, yi, ci = _coords()
    dev = 4 * xi + 2 * yi + ci
    win_cols = w_in.shape[-1]
    cw_cols = conv_w.shape[-1]

    nb, seq, _ = x.shape
    core = jnp.reshape(ci, (1,)).astype(jnp.int32)
    owners = lambda a: a.reshape((4, 2) + a.shape[1:])

    lead_pack = jnp.zeros((N_META, 512), F32)
    lead_pack = lead_pack.at[:, :128].set(meta).at[:4, 128:128 + cw_cols].set(conv_w[0])
    g_win, g_lead = _weight_gather([_mx(w_in[0]), lead_pack])
    win_full = _assemble_bands(g_win, PROJ_W)
    meta_full = jnp.transpose(g_lead[:, :, :128], (1, 0, 2)).reshape(N_META, D_MODEL)
    cw_full = jnp.transpose(g_lead[:, :4, 128:128 + cw_cols], (1, 0, 2)).reshape(4, D_XBC)

    lead = jnp.concatenate([jnp.zeros((PAD_ROWS, D_MODEL), F32), meta_full] * nb, axis=0)
    x_rows = x.reshape(nb * seq, D_MODEL)
    tgt_rows = loss_target.reshape(nb * seq, D_MODEL)
    dt_bias_p, a_log_p = _pad128(dt_bias), _pad128(a_log)
    d_x = jnp.repeat(d_skip.reshape(1, N_HEADS).astype(F32), HEAD_DIM, axis=1)
    norm_f_row = norm_f_w.reshape(1, D_MODEL)

    hn1, proj = _in_proj(x_rows, lead, norm_mix_w, win_full)
    late_cols = [False, True, False]
    y, ypre, conv_pre, states, (g_wout, wff1_full, g_wff2) = _mixer_fwd(
        proj, cw_full, conv_b, dt_bias_p, a_log_p, d_x, ssm_norm_w, pool_w[0], pool_scale, nb,
        [_mx(w_out[0]), _mx(w_ff1[0]), _mx(w_ff2[0])], late_cols)
    wout_full = g_wout.reshape(D_MIX, D_MODEL)
    wff2_full = g_wff2.reshape(D_FF, D_MODEL)
    loss, gr_nf, gr_nffn, ff, da, hn2, dh1, dh2, dy = _ffn_fwd_bwd(
        x_rows, lead, y, tgt_rows, wout_full, norm_ffn_w, wff1_full, wff2_full, norm_f_row)
    gr_wff2 = _tn_matmul(ff, dh2, "grad_w_ff2", tka=1024, max_slab=2816)
    gr_wff1_t = _tn_matmul(da, hn2, "grad_w_ff1", tka=1024, max_slab=2816)
    gr_wout = _tn_matmul(y, dh1, "grad_w_out", tka=1024, max_slab=2816)

    by_owner = lambda k: [owners(gr_wout[k].reshape(N_DEV, D_MIX // N_DEV, D_MODEL)),
                          owners(gr_wff1_t[k].reshape(N_DEV, D_FF // N_DEV, D_MODEL)),
                          owners(gr_wff2[k].reshape(N_DEV, D_FF // N_DEV, D_MODEL))]
    late_parts = by_owner(0)
    late_got, _ = _grad_exchange_d2d(by_owner(1), "grad_exchange_d2d_late")
    late_sums = [_chip_sum(late_parts[t], late_got[t], core, "chip_sum_late_%d" % t) for t in range(3)]
    (dproj, gr_cw, gr_cb, gr_dtb, gr_alog, gr_d, gr_nw, gr_pw, gr_ps), late_exchanged = _mixer_bwd(
        proj, dy, ypre, conv_pre, states, cw_full, dt_bias_p, a_log_p, d_x, ssm_norm_w, pool_w[0], pool_scale, nb,
        late_sums)

    early = dict(pool_w=gr_pw, pool_scale=gr_ps, conv_w=gr_cw, conv_b=gr_cb, dt_bias=gr_dtb[:, :N_HEADS],
                 a_log=gr_alog[:, :N_HEADS], d_skip=gr_d[:, :N_HEADS], ssm_norm_w=gr_nw, norm_ffn_w=gr_nffn,
                 norm_f_w=gr_nf, loss=loss[0:1, 0:1])
    early_pack = _pack(list(early.values()))
    win_mine, win_theirs = _tn_matmul_banded(hn1, dproj, core, "grad_w_in", win_cols, tka=512)
    win_got, (early_got,) = _grad_exchange_d2d([win_theirs], "grad_exchange_d2d_w_in", swapped=[early_pack])
    win_sum = _chip_sum(win_mine, win_got[0], core, "chip_sum_w_in")
    early_chip = _sum_two(early_pack, early_got, "chip_sum_small")
    whole = [False, True]
    send_sems, recv_sems, sent, landing, started = _chip_exchange_start([win_sum, early_chip], whole, "w_in_exchange_start")
    gx_rows, gr_nmix, gr_meta = _in_proj_bwd(dproj, x_rows, lead, dh1, norm_mix_w, win_full, started)
    (win_sum, early_chip), (win_landed, early_landed) = _chip_exchange_wait(
        send_sems, recv_sems, sent, landing, whole, gr_nmix, "w_in_exchange_wait")
    parts = dict(w_in=win_landed, w_out=late_exchanged[0], w_ff1=late_exchanged[1], w_ff2=late_exchanged[2])
    my_chip = jnp.reshape(2 * xi + yi, (1,)).astype(jnp.int32)
    early_sum = _sum_chips(early_landed, early_chip, my_chip, "small_sum")

    tail = dict(meta=gr_meta, norm_mix_w=gr_nmix)
    tail_sum = _small_allreduce(_pack(list(tail.values())))
    gs = dict(zip(early, _unpack(early_sum, [a.shape for a in early.values()])))
    gs.update(zip(tail, _unpack(tail_sum, [a.shape for a in tail.values()])))
    gs["meta"] = lax.dynamic_slice_in_dim(gs["meta"], dev * 128, 128, axis=1)
    gs["conv_w"] = lax.dynamic_slice_in_dim(gs["conv_w"], dev * cw_cols, cw_cols, axis=1)

    res = {}
    for n in _BIG:
        shp = wts[n].shape
        args = (parts[n], wts[n][0], mom1[n][0], mom2[n][0], "adamw_" + n)
        if n == "w_ff1":
            outs = _adamw_transposed(*args)
        elif n == "w_in":
            outs = _adamw(*args, own=win_sum, chip=my_chip)
        else:
            outs = _adamw(*args)
        res[n] = [o.reshape(shp) for o in outs]
    as2d = lambda a: a.reshape(-1, a.shape[-1])
    small_g = [as2d(gs[n].reshape(wts[n].shape)) for n in _SMALL]
    small_out = _adamw_small(small_g, *[[as2d(d[n]) for n in _SMALL] for d in (wts, mom1, mom2)])
    for k, n in enumerate(_SMALL):
        res[n] = [o[k].reshape(wts[n].shape) for o in (small_g,) + tuple(small_out)]

    grad_x = gx_rows.reshape(nb, seq, D_MODEL)
    return (gs["loss"][0, 0], grad_x, *[res[n][0] for n in _WEIGHTS], *[res[n][1] for n in _WEIGHTS],
            *[res[n][2] for n in _WEIGHTS], *[res[n][3] for n in _WEIGHTS])
```

```python
import numpy as np
import jax
import jax.numpy as jnp
from jax import lax
from jax.experimental import pallas as pl
from jax.experimental.pallas import tpu as pltpu

F32 = jnp.float32
MXU_DTYPE = jnp.bfloat16

D_MODEL = 1024
D_POOL = 512
D_SSM = 1536
D_XBC = 2560
N_HEADS = 24
HEAD_DIM = 64
N_GROUPS = 4
GROUP_CH = D_SSM // N_GROUPS
D_STATE = 128
CHUNK = 128
N_META = 16
LEAD = CHUNK
PAD_ROWS = LEAD - N_META
ROW_TILE = 2 * CHUNK
D_MIX = D_POOL + D_SSM
D_FF = 4096
PROJ_W = 4736
OFF_Z = D_POOL
OFF_X = D_POOL + D_SSM
OFF_DT = OFF_X + D_XBC
D_IN_PROJ = OFF_DT + N_HEADS
POOL_WINDOWS = (2, 4, 8, 16)
HALO = 16
EPS = 1e-5
N_DEV = 8

ADAM_LR, ADAM_B1, ADAM_B2, ADAM_EPS, ADAM_WD, ADAM_STEP = 0.001, 0.9, 0.999, 1e-08, 0.01, 10

VMEM_LIMIT = 60 * 1024 * 1024


def _mx(a):
    return a.astype(MXU_DTYPE)


def _dot(a, b):
    return jnp.dot(_mx(a), _mx(b), preferred_element_type=F32)


def _dot_nt(a, b):
    return lax.dot_general(_mx(a), _mx(b), (((1,), (1,)), ((), ())), preferred_element_type=F32)


def _dot_tn(a, b):
    return lax.dot_general(_mx(a), _mx(b), (((0,), (0,)), ((), ())), preferred_element_type=F32)


def _split3(x):
    hi = x.astype(MXU_DTYPE)
    r = x - hi.astype(F32)
    mid = r.astype(MXU_DTYPE)
    lo = (r - mid.astype(F32)).astype(MXU_DTYPE)
    return hi, mid, lo


def _exact_l(c, x):
    hi, mid, lo = _split3(x)
    f = lambda p: jnp.dot(c, p, preferred_element_type=F32)
    return f(hi) + f(mid) + f(lo)


def _exact_r(x, c):
    hi, mid, lo = x if isinstance(x, tuple) else _split3(x)
    f = lambda p: jnp.dot(p, c, preferred_element_type=F32)
    return f(hi) + f(mid) + f(lo)


def _contract(x, c):
    hi = x.astype(MXU_DTYPE)
    lo = (x - hi.astype(F32)).astype(MXU_DTYPE)
    return jnp.dot(hi, c, preferred_element_type=F32) + jnp.dot(lo, c, preferred_element_type=F32)


def _sigmoid(x):
    return jax.nn.sigmoid(x)


def _softplus(x):
    return jnp.maximum(x, 0.0) + jnp.log1p(jnp.exp(-jnp.abs(x)))


def _silu_and_grad(x):
    s = _sigmoid(x)
    y = x * s
    return y, s + y * (1.0 - s)


def _shift_up(ext, s):
    if s == 0:
        return ext[:CHUNK, :]
    return pltpu.roll(ext, ext.shape[0] - s, 0)[:CHUNK, :]


def _by_pool_group(lane, a2, a4, a8, a16):
    return jnp.where(lane < 128, a2, jnp.where(lane < 256, a4, jnp.where(lane < 384, a8, a16)))


def _pool_inv_count(chunk_idx):
    row = lax.broadcasted_iota(jnp.int32, (CHUNK, D_POOL), 0)
    lane = lax.broadcasted_iota(jnp.int32, (CHUNK, D_POOL), 1)
    pos1 = jnp.maximum(chunk_idx * CHUNK + row - (PAD_ROWS - 1), 1)
    w = _by_pool_group(lane, 2, 4, 8, 16)
    return 1.0 / jnp.minimum(pos1, w).astype(F32), lane


def _pool_window_sums(u_ext, lane):
    s2 = u_ext + pltpu.roll(u_ext, 1, 0)
    s4 = s2 + pltpu.roll(s2, 2, 0)
    s8 = s4 + pltpu.roll(s4, 4, 0)
    s16 = s8 + pltpu.roll(s8, 8, 0)
    return _by_pool_group(lane, s2[HALO:], s4[HALO:], s8[HALO:], s16[HALO:])


def _pool_window_sums_ahead(q_ext, lane):
    n = q_ext.shape[0]
    r2 = q_ext + pltpu.roll(q_ext, n - 1, 0)
    r4 = r2 + pltpu.roll(r2, n - 2, 0)
    r8 = r4 + pltpu.roll(r4, n - 4, 0)
    r16 = r8 + pltpu.roll(r8, n - 8, 0)
    return _by_pool_group(lane, r2[:CHUNK], r4[:CHUNK], r8[:CHUNK], r16[:CHUNK])


def _conv_pre(ext, xbc, cw, cb):
    s1 = pltpu.roll(ext, 1, 0)
    near = cw[3:4, :] * xbc + cw[2:3, :] * s1[HALO:, :]
    far = cw[1:2, :] * ext + cw[0:1, :] * s1
    return cb + near + pltpu.roll(far, 2, 0)[HALO:, :]


def _dt_and_cumsum(dtr, dt_bias, a_log, valid, tril):
    lane = lax.broadcasted_iota(jnp.int32, (CHUNK, 128), 1)
    head = lane < N_HEADS
    pre = dtr + dt_bias
    dt = jnp.where(valid & head, _softplus(pre), 0.0)
    a_row = jnp.where(head[0:1, :], -jnp.exp(a_log), 0.0)
    a_col = _exact_l(tril, dt * a_row)
    return dt, a_row, a_col, pre, head


def _decay(a_col, a_row_t, h, causal):
    seg = a_col[:, h:h + 1] - a_row_t[h:h + 1, :]
    return jnp.where(causal, jnp.exp(jnp.minimum(seg, 0.0)), 0.0)


def _ssd_chunk_fwd(xs, bm, cm, dt, a_col, s_prev, d_x, e_mat, et_f32):
    lane = lax.broadcasted_iota(jnp.int32, (CHUNK, 128), 1)
    rowi = lax.broadcasted_iota(jnp.int32, (CHUNK, CHUNK), 0)
    coli = lax.broadcasted_iota(jnp.int32, (CHUNK, CHUNK), 1)
    causal = rowi >= coli
    a_row_t = a_col.T
    ax = _exact_r(a_col, e_mat)
    dtx = _exact_r(dt, e_mat)
    xdt = xs * dtx
    ax_last = ax[CHUNK - 1:CHUNK, :]
    e_a = jnp.exp(ax)
    w_end = xdt * jnp.exp(ax_last - ax)
    cd_col = jnp.exp(jnp.sum(et_f32 * a_col[CHUNK - 1:CHUNK, :], axis=1, keepdims=True))
    ys, s_new = [], []
    for g in range(N_GROUPS):
        gs = slice(g * GROUP_CH, (g + 1) * GROUP_CH)
        bg = bm[:, g * D_STATE:(g + 1) * D_STATE]
        cg = cm[:, g * D_STATE:(g + 1) * D_STATE]
        sg = s_prev[gs, :]
        cb = _dot_nt(cg, bg)
        y_off = _dot_nt(cg, sg) * e_a[:, gs]
        s_new.append(sg * cd_col[gs, :] + _dot_tn(w_end[:, gs], bg))
        for pr in range(3):
            c0 = g * GROUP_CH + pr * 128
            xdt_p = xdt[:, c0:c0 + 128]
            h0 = g * 6 + pr * 2
            y0 = _dot(cb * _decay(a_col, a_row_t, h0, causal), xdt_p)
            y1 = _dot(cb * _decay(a_col, a_row_t, h0 + 1, causal), xdt_p)
            ys.append(jnp.where(lane < HEAD_DIM, y0, y1) + y_off[:, pr * 128:(pr + 1) * 128])
    y = jnp.concatenate(ys, axis=1) + d_x * xs
    return y, jnp.concatenate(s_new, axis=0)


def _ssd_chunk_bwd(xs, bm, cm, dt, a_row, a_col, s_prev, ds_new, dy, d_x, e_mat, et_mat, et_f32, triu):
    lane = lax.broadcasted_iota(jnp.int32, (CHUNK, 128), 1)
    sub = lax.broadcasted_iota(jnp.int32, (CHUNK, 128), 0)
    rowi = lax.broadcasted_iota(jnp.int32, (CHUNK, CHUNK), 0)
    coli = lax.broadcasted_iota(jnp.int32, (CHUNK, CHUNK), 1)
    causal = rowi >= coli
    a_row_t = a_col.T
    a_last = a_col[CHUNK - 1:CHUNK, :]
    a_split, dt_split = _split3(a_col), _split3(dt)
    sub8 = lax.broadcasted_iota(jnp.int32, (8, GROUP_CH), 0)

    dxs, dbs, dcs, dsp = [], [], [], []
    zcol = jnp.zeros((CHUNK, 128), F32)
    zrows = []
    da_col = jnp.zeros((CHUNK, 128), F32)
    ddt = jnp.zeros((CHUNK, 128), F32)
    head_sums = jnp.zeros((8, 128), F32)
    q_row = jnp.zeros((1, 128), F32)
    for g in range(N_GROUPS):
        gs = slice(g * GROUP_CH, (g + 1) * GROUP_CH)
        e_g, et_g = e_mat[:, gs], et_mat[gs, :]
        xs_g, dy_g = xs[:, gs], dy[:, gs]
        ax = _exact_r(a_split, e_g)
        dtx = _exact_r(dt_split, e_g)
        xdt = xs_g * dtx
        dte = jnp.exp(ax[CHUNK - 1:CHUNK, :] - ax)
        w_end = xdt * dte
        cd_col = jnp.exp(jnp.sum(et_f32[gs, :] * a_last, axis=1, keepdims=True))
        dye = dy_g * jnp.exp(ax)
        bg = bm[:, g * D_STATE:(g + 1) * D_STATE]
        cg = cm[:, g * D_STATE:(g + 1) * D_STATE]
        sg = s_prev[gs, :]
        dsg = ds_new[gs, :]
        cb = _dot_nt(cg, bg)
        cs = _dot_nt(cg, sg)
        dcg = _dot(dye, sg)
        dsp.append(dsg * cd_col + _dot_tn(dye, cg))
        dwg = _dot_nt(bg, dsg)
        dbg = _dot(w_end, dsg)
        ww = dwg * w_end
        t1 = jnp.sum(dsg * sg, axis=1, keepdims=True) * cd_col
        dcb = jnp.zeros((CHUNK, CHUNK), F32)
        pairs = []
        for pr in range(3):
            ps = slice(pr * 128, (pr + 1) * 128)
            xdt_p, dy_p = xdt[:, ps], dy_g[:, ps]
            acc = None
            for half in range(2):
                h = g * 6 + pr * 2 + half
                ld = _decay(a_col, a_row_t, h, causal)
                gm = cb * ld
                dym = jnp.where((lane < HEAD_DIM) if half == 0 else (lane >= HEAD_DIM), dy_p, 0.0)
                dg = _dot_nt(dym, xdt_p)
                dseg = dg * gm
                dcb = dcb + dg * ld
                t = _dot_tn(gm, dym)
                acc = t if acc is None else acc + t
                zcol = jnp.where(lane == h, jnp.sum(dseg, axis=1, keepdims=True), zcol)
                zrows.append(jnp.sum(dseg, axis=0, keepdims=True))
            pairs.append(acc)
        dxdt = dwg * dte + jnp.concatenate(pairs, axis=1)
        dcs.append(dcg + _dot(dcb, bg))
        dbs.append(dbg + _dot_tn(dcb, cg))
        dxs.append(dxdt * dtx + d_x[:, gs] * dy_g)
        da_col = da_col + _contract(dye * cs - ww, et_g)
        ddt = ddt + _contract(dxdt * xs_g, et_g)
        col_sums = jnp.where(sub8 == 0, jnp.sum(dy_g * xs_g, axis=0, keepdims=True),
                             jnp.where(sub8 == 1, jnp.sum(ww, axis=0, keepdims=True), 0.0))
        head_sums = head_sums + _exact_r(col_sums, et_g)
        q_row = q_row + jnp.sum(et_f32[gs, :] * t1, axis=0, keepdims=True)

    dd = head_sums[0:1, :]
    q_row = q_row + head_sums[1:2, :]
    zrow = jnp.concatenate(zrows + [jnp.zeros((128 - N_HEADS, CHUNK), F32)], axis=0)
    da_col = da_col + zcol - zrow.T + jnp.where(sub == CHUNK - 1, q_row, 0.0)
    rc = _exact_l(triu, da_col)
    ddt = ddt + rc * a_row
    da = jnp.sum(rc * dt, axis=0, keepdims=True)
    return (jnp.concatenate(dxs, axis=1), jnp.concatenate(dbs, axis=1), jnp.concatenate(dcs, axis=1), ddt, da, dd,
            jnp.concatenate(dsp, axis=0))


_SSD_CONSTANT_SHAPES = dict(e=((128, D_SSM), MXU_DTYPE), et=((D_SSM, 128), MXU_DTYPE), et_f32=((D_SSM, 128), F32),
                            tril=((CHUNK, CHUNK), MXU_DTYPE), triu=((CHUNK, CHUNK), MXU_DTYPE))


def _ssd_constant_scratch(names):
    return [pltpu.VMEM(*_SSD_CONSTANT_SHAPES[n]) for n in names]


def _fill_ssd_constants(**refs):
    iota = lambda shape, d: lax.broadcasted_iota(jnp.int32, shape, d)
    shift = HEAD_DIM.bit_length() - 1
    marks = dict(
        e=lambda: iota((128, D_SSM), 0) == (iota((128, D_SSM), 1) >> shift),
        et=lambda: iota((D_SSM, 128), 1) == (iota((D_SSM, 128), 0) >> shift),
        et_f32=lambda: iota((D_SSM, 128), 1) == (iota((D_SSM, 128), 0) >> shift),
        tril=lambda: iota((CHUNK, CHUNK), 1) <= iota((CHUNK, CHUNK), 0),
        triu=lambda: iota((CHUNK, CHUNK), 1) >= iota((CHUNK, CHUNK), 0))
    for name, ref in refs.items():
        ref[...] = jnp.where(marks[name](), 1.0, 0.0).astype(ref.dtype)


def _row_views(ref, widths):
    views, off = [], 0
    for w in widths:
        views.append(ref.at[:, off:off + w])
        off += w
    return views


def _full(shape):
    nd = len(shape)
    return pl.BlockSpec(shape, lambda *_: (0,) * nd)


def _params(*sem):
    return pltpu.CompilerParams(dimension_semantics=sem, vmem_limit_bytes=VMEM_LIMIT)


def _token_tiles(width, n_tok_tiles):
    return pl.BlockSpec((ROW_TILE, width), lambda i: (jnp.minimum(i, n_tok_tiles - 1), 0))


def _in_proj(x, lead, w1, win):
    nt = x.shape[0] // ROW_TILE
    m = x.shape[0] + ROW_TILE
    tm = ROW_TILE

    def body(x_ref, lead_ref, w1_ref, win_hbm, hn_ref, proj_ref, win_v, sem):
        i = pl.program_id(0)

        @pl.when(i == 0)
        def _():
            cp = pltpu.make_async_copy(win_hbm, win_v, sem)
            cp.start()
            cp.wait()

        x = jnp.where(i == nt, lead_ref[...], x_ref[...])
        r = lax.rsqrt(jnp.mean(x * x, axis=-1, keepdims=True) + EPS)
        hn = _mx(x * r * w1_ref[...])
        hn_ref[...] = hn
        for j in range(0, PROJ_W, 512):
            w = min(512, PROJ_W - j)
            proj_ref[:, j:j + w] = jnp.dot(hn, win_v[:, j:j + w], preferred_element_type=F32)

    return pl.pallas_call(
        body, grid=(m // tm,), name="in_proj",
        in_specs=[_token_tiles(D_MODEL, nt), _full((ROW_TILE, D_MODEL)), _full((1, D_MODEL)),
                  pl.BlockSpec(memory_space=pl.ANY)],
        out_specs=[pl.BlockSpec((tm, D_MODEL), lambda i: (i, 0)), pl.BlockSpec((tm, PROJ_W), lambda i: (i, 0))],
        out_shape=[jax.ShapeDtypeStruct((m, D_MODEL), MXU_DTYPE), jax.ShapeDtypeStruct((m, PROJ_W), F32)],
        scratch_shapes=[pltpu.VMEM((D_MODEL, PROJ_W), MXU_DTYPE), pltpu.SemaphoreType.DMA],
        compiler_params=_params("arbitrary"),
    )(x, lead, w1, win)


def _ffn_fwd_bwd(x, lead, y, tgt, wout, w2n, wff1, wff2, wfn):
    nt = x.shape[0] // ROW_TILE
    m = x.shape[0] + ROW_TILE
    tm = ROW_TILE
    nj = D_FF // 1024

    def body(x_ref, lead_ref, y_ref, tgt_ref, w2n_ref, wfn_ref, wout_hbm, wff1_hbm, wff2_hbm,
             loss_ref, gwf_ref, gw2_ref, ff_ref, da_ref, hn2_ref, dh1_ref, dh2_ref, dy_ref,
             wout_v, wff1_v, wff2_v, a_s, sems):
        i = pl.program_id(0)
        hp = jnp.where(i == nt, lead_ref[...], x_ref[...])

        @pl.when(i == 0)
        def _():
            cps = [pltpu.make_async_copy(s, d, sems.at[k])
                   for k, (s, d) in enumerate(((wout_hbm, wout_v), (wff1_hbm, wff1_v), (wff2_hbm, wff2_v)))]
            for cp in cps:
                cp.start()
            for cp in cps:
                cp.wait()
            loss_ref[...] = jnp.zeros_like(loss_ref)
            gwf_ref[...] = jnp.zeros_like(gwf_ref)
            gw2_ref[...] = jnp.zeros_like(gw2_ref)

        h1 = hp + jnp.dot(y_ref[...], wout_v[...], preferred_element_type=F32)
        r2 = lax.rsqrt(jnp.mean(h1 * h1, axis=-1, keepdims=True) + EPS)
        n2 = h1 * r2
        w2n_row = w2n_ref[...]
        hn2 = _mx(n2 * w2n_row)
        hn2_ref[...] = hn2
        h2 = h1
        for j in range(nj):
            js = slice(j * 1024, (j + 1) * 1024)
            a = jnp.dot(hn2, wff1_v[:, js], preferred_element_type=F32)
            a_s[:, js] = a
            ra = jnp.maximum(a, 0.0)
            ff = _mx(ra * ra)
            ff_ref[:, js] = ff
            h2 = h2 + jnp.dot(ff, wff2_v[js, :], preferred_element_type=F32)

        r3 = lax.rsqrt(jnp.mean(h2 * h2, axis=-1, keepdims=True) + EPS)
        n3 = h2 * r3
        wf_row = wfn_ref[...]
        err = n3 * wf_row - tgt_ref[...]
        tokf = (i < nt).astype(F32)
        loss_ref[...] += 0.5 * jnp.sum(jnp.mean(err * err, axis=-1, keepdims=True) * tokf)
        dout = err * (tokf / D_MODEL)
        gwf_ref[...] += jnp.sum(dout * n3, axis=0, keepdims=True)
        dn3 = dout * wf_row
        dh2 = r3 * (dn3 - n3 * jnp.mean(dn3 * n3, axis=-1, keepdims=True))
        dh2m = _mx(dh2)
        dh2_ref[...] = dh2m

        dhn2 = jnp.zeros((tm, D_MODEL), F32)
        for j in range(nj):
            js = slice(j * 1024, (j + 1) * 1024)
            dff = lax.dot_general(dh2m, wff2_v[js, :], (((1,), (1,)), ((), ())), preferred_element_type=F32)
            da = _mx(dff * (2.0 * jnp.maximum(a_s[:, js], 0.0)))
            da_ref[:, js] = da
            dhn2 = dhn2 + lax.dot_general(da, wff1_v[:, js], (((1,), (1,)), ((), ())), preferred_element_type=F32)
        gw2_ref[...] += jnp.sum(dhn2 * n2, axis=0, keepdims=True)
        dn2 = dhn2 * w2n_row
        dh1 = dh2 + r2 * (dn2 - n2 * jnp.mean(dn2 * n2, axis=-1, keepdims=True))
        dh1_ref[...] = dh1
        dy_ref[...] = lax.dot_general(_mx(dh1), wout_v[...], (((1,), (1,)), ((), ())), preferred_element_type=F32)

    rows = lambda w: pl.BlockSpec((tm, w), lambda i: (i, 0))
    hbm = pl.BlockSpec(memory_space=pl.ANY)
    return pl.pallas_call(
        body, grid=(m // tm,), name="ffn_fwd_bwd",
        in_specs=[_token_tiles(D_MODEL, nt), _full((ROW_TILE, D_MODEL)), rows(D_MIX), _token_tiles(D_MODEL, nt),
                  _full((1, D_MODEL)), _full((1, D_MODEL)), hbm, hbm, hbm],
        out_specs=[_full((1, 128)), _full((1, D_MODEL)), _full((1, D_MODEL)), rows(D_FF), rows(D_FF), rows(D_MODEL),
                   rows(D_MODEL), rows(D_MODEL), rows(D_MIX)],
        out_shape=[jax.ShapeDtypeStruct((1, 128), F32), jax.ShapeDtypeStruct((1, D_MODEL), F32),
                   jax.ShapeDtypeStruct((1, D_MODEL), F32), jax.ShapeDtypeStruct((m, D_FF), MXU_DTYPE),
                   jax.ShapeDtypeStruct((m, D_FF), MXU_DTYPE), jax.ShapeDtypeStruct((m, D_MODEL), MXU_DTYPE),
                   jax.ShapeDtypeStruct((m, D_MODEL), F32), jax.ShapeDtypeStruct((m, D_MODEL), MXU_DTYPE),
                   jax.ShapeDtypeStruct((m, D_MIX), F32)],
        scratch_shapes=[pltpu.VMEM((D_MIX, D_MODEL), MXU_DTYPE), pltpu.VMEM((D_MODEL, D_FF), MXU_DTYPE),
                        pltpu.VMEM((D_FF, D_MODEL), MXU_DTYPE), pltpu.VMEM((tm, D_FF), F32),
                        pltpu.SemaphoreType.DMA((3,))],
        compiler_params=_params("arbitrary"),
    )(x, lead, y, tgt, w2n, wfn, wout, wff1, wff2)


def _in_proj_bwd(dproj, x, lead, dh1, w1, win, after):
    nt = x.shape[0] // ROW_TILE
    m = x.shape[0] + ROW_TILE
    tm = ROW_TILE

    def body(dp_ref, x_ref, lead_ref, dh1_ref, w1_ref, win_hbm, after_ref, gx_ref, gw1_ref, gmeta_ref, win_v, sem):
        i = pl.program_id(0)

        @pl.when(i == 0)
        def _():
            cp = pltpu.make_async_copy(win_hbm, win_v, sem)
            cp.start()
            cp.wait()
            gw1_ref[...] = jnp.zeros_like(gw1_ref)
            gmeta_ref[...] = jnp.zeros_like(gmeta_ref)

        dhn = lax.dot_general(dp_ref[...], win_v[...], (((1,), (1,)), ((), ())), preferred_element_type=F32)
        x = jnp.where(i == nt, lead_ref[...], x_ref[...])
        r = lax.rsqrt(jnp.mean(x * x, axis=-1, keepdims=True) + EPS)
        n = x * r
        gw1_ref[...] += jnp.sum(dhn * n, axis=0, keepdims=True)
        dn = dhn * w1_ref[...]
        dh0 = dh1_ref[...] + r * (dn - n * jnp.mean(dn * n, axis=-1, keepdims=True))

        @pl.when(i < nt)
        def _():
            gx_ref[...] = dh0

        @pl.when(i == nt)
        def _():
            gmeta_ref[...] = dh0[PAD_ROWS:LEAD, :] + dh0[LEAD + PAD_ROWS:2 * LEAD, :]

    rows = lambda w: pl.BlockSpec((tm, w), lambda i: (i, 0))
    hbm = pl.BlockSpec(memory_space=pl.ANY)
    return pl.pallas_call(
        body, grid=(m // tm,), name="in_proj_bwd",
        in_specs=[rows(PROJ_W), _token_tiles(D_MODEL, nt), _full((ROW_TILE, D_MODEL)), rows(D_MODEL),
                  _full((1, D_MODEL)), hbm, hbm],
        out_specs=[_token_tiles(D_MODEL, nt), _full((1, D_MODEL)), _full((N_META, D_MODEL))],
        out_shape=[jax.ShapeDtypeStruct(x.shape, F32), jax.ShapeDtypeStruct((1, D_MODEL), F32),
                   jax.ShapeDtypeStruct((N_META, D_MODEL), F32)],
        scratch_shapes=[pltpu.VMEM((D_MODEL, PROJ_W), MXU_DTYPE), pltpu.SemaphoreType.DMA],
        compiler_params=_params("arbitrary"),
    )(dproj, x, lead, dh1, w1, win, after)


MXU_DEPTH = 256


def _row_slab(m, cap):
    return max(k for k in range(MXU_DEPTH, cap + 1, MXU_DEPTH) if m % k == 0)


def _tn_matmul(a, b, name, tka, max_slab=768, tn=512, swap=()):
    m, ka = a.shape
    nb = b.shape[1]
    tkm = _row_slab(m, max_slab)
    n_tiles, n_steps, ns = ka // tka, m // tkm, len(swap)

    def body(a_ref, b_ref, *rest):
        swap_ins, (o_ref, omx_ref), swap_got, sems = rest[:ns], rest[ns:ns + 2], rest[ns + 2:2 * ns + 2], rest[2 * ns + 2:]
        i, k = pl.program_id(0), pl.program_id(1)
        copies = _to_sibling(swap_ins, swap_got, *sems) if ns else []

        @pl.when(k == 0)
        def _():
            o_ref[...] = jnp.zeros_like(o_ref)

        if copies:
            @pl.when((i == 0) & (k == 0))
            def _():
                for cp in copies:
                    cp.start()

        at = _mx(a_ref[...])
        for j in range(0, nb, tn):
            w = min(tn, nb - j)
            o_ref[:, j:j + w] += lax.dot_general(at, _mx(b_ref[:, j:j + w]), (((0,), (0,)), ((), ())),
                                                 preferred_element_type=F32)

        @pl.when(k == n_steps - 1)
        def _():
            omx_ref[...] = _mx(o_ref[...])

        if copies:
            @pl.when((i == n_tiles - 1) & (k == n_steps - 1))
            def _():
                for cp in copies:
                    cp.wait_recv()
                for cp in copies:
                    cp.wait_send()

    out = pl.BlockSpec((tka, nb), lambda i, k: (i, 0))
    outs = pl.pallas_call(
        body, grid=(n_tiles, n_steps), name=name,
        in_specs=[pl.BlockSpec((tkm, tka), lambda i, k: (k, i)), pl.BlockSpec((tkm, nb), lambda i, k: (k, 0))]
        + [_HBM] * ns,
        out_specs=[out, out] + [_HBM] * ns,
        out_shape=[jax.ShapeDtypeStruct((ka, nb), F32), jax.ShapeDtypeStruct((ka, nb), MXU_DTYPE)]
        + [jax.ShapeDtypeStruct((4,) + g.shape[-2:], g.dtype) for g in swap],
        scratch_shapes=[pltpu.SemaphoreType.DMA((4 * ns,))] * 2 if ns else [],
        compiler_params=_params("arbitrary", "arbitrary"),
    )(a, b, *swap)
    return outs[:2], outs[2:]


def _tn_matmul_banded(a, b, core, name, band, tka, tn=512):
    m, ka = a.shape
    nb = b.shape[1]
    tkm = _row_slab(m, 768)
    n_steps = m // tkm

    def body(core_ref, a_ref, b_ref, own_ref, sib_ref, acc):
        k = pl.program_id(1)

        @pl.when(k == 0)
        def _():
            acc[...] = jnp.zeros_like(acc)

        at = _mx(a_ref[...])
        for j in range(0, nb, tn):
            w = min(tn, nb - j)
            acc[:, j:j + w] += lax.dot_general(at, _mx(b_ref[:, j:j + w]), (((0,), (0,)), ((), ())),
                                               preferred_element_type=F32)

        for c in range(2):
            @pl.when((k == n_steps - 1) & (core_ref[0] == c))
            def _():
                for chip in range(4):
                    mine, other = 2 * chip + c, 2 * chip + 1 - c
                    own_ref[chip] = acc[:, mine * band:(mine + 1) * band]
                    sib_ref[chip] = _mx(acc[:, other * band:(other + 1) * band])

    out = pl.BlockSpec((4, tka, band), lambda i, k, c: (0, i, 0))
    grid_spec = pltpu.PrefetchScalarGridSpec(
        num_scalar_prefetch=1, grid=(ka // tka, n_steps),
        in_specs=[pl.BlockSpec((tkm, tka), lambda i, k, c: (k, i)), pl.BlockSpec((tkm, nb), lambda i, k, c: (k, 0))],
        out_specs=[out, out], scratch_shapes=[pltpu.VMEM((tka, nb), F32)])
    return pl.pallas_call(
        body, grid_spec=grid_spec, name=name,
        out_shape=[jax.ShapeDtypeStruct((4, ka, band), F32), jax.ShapeDtypeStruct((4, ka, band), MXU_DTYPE)],
        compiler_params=_params("arbitrary", "arbitrary"),
    )(core, a, b)


def _assemble_bands(g, width):
    n, rows, band = g.shape
    tr = 256

    def body(g_ref, o_ref):
        parts = [g_ref[j] for j in range(n)] + [jnp.zeros((tr, width - n * band), g.dtype)]
        o_ref[...] = jnp.concatenate(parts, axis=1)

    return pl.pallas_call(
        body, grid=(rows // tr,), name="assemble_w_in",
        in_specs=[pl.BlockSpec((n, tr, band), lambda i: (0, i, 0))],
        out_specs=pl.BlockSpec((tr, width), lambda i: (i, 0)),
        out_shape=jax.ShapeDtypeStruct((rows, width), g.dtype),
        compiler_params=_params("arbitrary"),
    )(g)


def _chunk_block(b, c, nb, nc):
    return jnp.where(c == 0, nb * (nc - 1) + b, b * (nc - 1) + c - 1)

def _mixer_fwd(proj, cw, cb, dt_bias, a_log, d_x, nw, pool_w, pool_scale, nb, shards, by_cols):
    m = proj.shape[0]
    nc = m // nb // CHUNK
    ns = len(shards)
    row_widths = [D_XBC, 128, 128, D_SSM, D_SSM, D_POOL]
    rows = jnp.concatenate([cb, dt_bias, a_log, d_x, nw, pool_scale], axis=1)
    pairs = (nc - 1) // 2
    n_steps = nb * pairs + 1

    def block_of(step):
        return jnp.where(step == 0, nb * pairs, step - 1)

    def chunks_of(step):
        lead = step == 0
        b = jnp.maximum(step - 1, 0) // pairs
        j = jnp.maximum(step - 1, 0) - b * pairs
        return [(jnp.where(lead, 0, b), jnp.where(lead, 0, 2 * j + 1), 0),
                (jnp.where(lead, 1, b), jnp.where(lead, 0, 2 * j + 2), CHUNK)]

    def body(p_ref, cw_ref, rows_ref, pw_ref, *rest):
        shard_refs, (y_ref, ypre_ref, pre_ref, st_ref) = rest[:ns], rest[ns:ns + 4]
        gathered_refs, carries, constants_refs = rest[ns + 4:2 * ns + 4], rest[2 * ns + 4:2 * ns + 7], rest[2 * ns + 7:2 * ns + 10]
        gather = _Gather(shard_refs, gathered_refs, by_cols, *rest[2 * ns + 10:])
        step = pl.program_id(0)

        @pl.when(step == 0)
        def _():
            gather.start()
            _fill_ssd_constants(**dict(zip(["e", "et_f32", "tril"], constants_refs)))

        @pl.when(step == n_steps // 2)
        def _():
            gather.forward()

        shared = (cw_ref, pw_ref, _row_views(rows_ref, row_widths), constants_refs)
        for b, c, row0 in chunks_of(step):
            at = lambda ref: ref.at[row0:row0 + CHUNK]
            one_chunk(c, at(p_ref), at(y_ref), at(ypre_ref), at(pre_ref), st_ref.at[row0 // CHUNK], *shared,
                      [carry.at[b] for carry in carries])

        @pl.when(step == n_steps - 1)
        def _():
            gather.finish()

    def one_chunk(c, p_ref, y_ref, ypre_ref, pre_ref, st_ref, cw_ref, pw_ref, row_refs, constants_refs, carries):
        cb_ref, dtb_ref, alog_ref, dx_ref, nw_ref, ps_ref = row_refs
        e_ref, et_ref, tril_ref = constants_refs
        xtail, utail, state = carries

        @pl.when(c == 0)
        def _():
            xtail[...] = jnp.zeros_like(xtail)
            utail[...] = jnp.zeros_like(utail)
            state[...] = jnp.zeros_like(state)

        valid = (c > 0) | (lax.broadcasted_iota(jnp.int32, (CHUNK, 1), 0) >= PAD_ROWS)

        u = p_ref[:, 0:D_POOL]
        inv_cnt, lane = _pool_inv_count(c)
        win = _pool_window_sums(jnp.concatenate([utail[...], u], axis=0), lane)
        utail[...] = u[CHUNK - HALO:, :]
        pooled = win * inv_cnt - u
        mixed = jnp.concatenate(
            [_dot(pooled[:, g * 128:(g + 1) * 128], pw_ref[g]) for g in range(len(POOL_WINDOWS))], axis=1)
        y_ref[:, 0:D_POOL] = _mx(mixed * ps_ref[...])

        xbc = p_ref[:, OFF_X:OFF_X + D_XBC]
        pre = _conv_pre(jnp.concatenate([xtail[...], xbc], axis=0), xbc, cw_ref[...], cb_ref[...])
        xtail[...] = xbc[CHUNK - HALO:, :]
        pre_ref[...] = pre
        xc = pre * _sigmoid(pre)
        dt, _, a_col, _, _ = _dt_and_cumsum(p_ref[:, OFF_DT:OFF_DT + 128], dtb_ref[...], alog_ref[...], valid,
                                            tril_ref[...])
        s_prev = state[...]
        st_ref[...] = s_prev
        yp, s_new = _ssd_chunk_fwd(xc[:, 0:D_SSM], xc[:, D_SSM:D_SSM + 512], xc[:, D_SSM + 512:], dt, a_col, s_prev,
                                   dx_ref[...], e_ref[...], et_ref[...])
        state[...] = s_new
        ypre_ref[...] = yp
        z = p_ref[:, OFF_Z:OFF_Z + D_SSM]
        yz = yp * (z * _sigmoid(z))
        outs = []
        for g in range(N_GROUPS):
            gs = slice(g * GROUP_CH, (g + 1) * GROUP_CH)
            r = lax.rsqrt(jnp.mean(yz[:, gs] * yz[:, gs], axis=-1, keepdims=True) + EPS)
            outs.append(yz[:, gs] * r)
        y_ref[:, D_POOL:] = _mx(jnp.concatenate(outs, axis=1) * nw_ref[...])

    blk = lambda w: pl.BlockSpec((2 * CHUNK, w), lambda s: (block_of(s), 0))
    hbm = pl.BlockSpec(memory_space=pl.ANY)
    outs = pl.pallas_call(
        body, grid=(n_steps,), name="mixer_fwd",
        in_specs=[blk(PROJ_W), _full((4, D_XBC)), _full((1, sum(row_widths))), _full((4, 128, 128))] + [hbm] * ns,
        out_specs=[blk(D_MIX), blk(D_SSM), blk(D_XBC),
                   pl.BlockSpec((2, D_SSM, D_STATE), lambda s: (block_of(s), 0, 0))] + [hbm] * ns,
        out_shape=[jax.ShapeDtypeStruct((m, D_MIX), MXU_DTYPE), jax.ShapeDtypeStruct((m, D_SSM), F32),
                   jax.ShapeDtypeStruct((m, D_XBC), F32), jax.ShapeDtypeStruct((m // CHUNK, D_SSM, D_STATE), F32)]
        + _Gather.out_shapes(shards, by_cols),
        scratch_shapes=[pltpu.VMEM((nb, HALO, D_XBC), F32), pltpu.VMEM((nb, HALO, D_POOL), F32),
                        pltpu.VMEM((nb, D_SSM, D_STATE), F32)] + _ssd_constant_scratch(["e", "et_f32", "tril"])
        + _Gather.scratch(ns),
        compiler_params=_params("arbitrary"),
    )(proj, cw, rows, pool_w, *shards)
    return outs[0], outs[1], outs[2], outs[3], outs[4:]


def _mixer_bwd(proj, dy, ypre, conv_pre, states, cw, dt_bias, a_log, d_x, nw, pool_w, pool_scale, nb, chip_sums):
    m = proj.shape[0]
    nc = m // nb // CHUNK
    hb = CHUNK // HALO
    ns = len(chip_sums)
    row_widths = [128, 128, D_SSM, D_SSM, D_POOL]
    rows = jnp.concatenate([dt_bias, a_log, d_x, nw, pool_scale], axis=1)
    grad_row_widths = [D_XBC, 128, 128, 128, D_SSM, D_POOL]
    constants = ["e", "et", "et_f32", "tril", "triu"]

    pairs = (nc - 1) // 2
    n_steps = nb * pairs + 1

    def block_of(step):
        b = jnp.minimum(step // pairs, nb - 1)
        j = pairs - 1 - (step - b * pairs)
        return jnp.where(step == n_steps - 1, nb * pairs, b * pairs + j)

    def chunks_of(step):
        lead = step == n_steps - 1
        b = jnp.minimum(step // pairs, nb - 1)
        j = pairs - 1 - (step - b * pairs)
        return [(jnp.where(lead, 0, b), jnp.where(lead, 0, 2 * j + 2), jnp.where(lead, 0, CHUNK)),
                (jnp.where(lead, 1, b), jnp.where(lead, 0, 2 * j + 1), jnp.where(lead, CHUNK, 0))]

    def body(p_ref, halo_ref, dy_ref, ypre_ref, pre_ref, st_ref, cw_ref, rows_ref, pw_ref, *rest):
        cs_refs = rest[:ns]
        dp_ref, gcw_ref, grows_ref, gpw_ref = rest[ns:ns + 4]
        part_refs, carries, constants_refs = rest[ns + 4:2 * ns + 4], rest[2 * ns + 4:2 * ns + 7], rest[2 * ns + 7:2 * ns + 12]
        exchange = _ChipExchange(cs_refs, part_refs, [False] * ns, *rest[2 * ns + 12:])
        step = pl.program_id(0)

        @pl.when(step == 0)
        def _():
            exchange.start()
            _fill_ssd_constants(**dict(zip(constants, constants_refs)))
            for r in (gcw_ref, grows_ref, gpw_ref):
                r[...] = jnp.zeros_like(r)

        shared = (cw_ref, pw_ref, _row_views(rows_ref, row_widths), gcw_ref, gpw_ref,
                  _row_views(grows_ref, grad_row_widths), constants_refs)
        halos = [p_ref[CHUNK - HALO:CHUNK, 0:D_POOL], halo_ref[...]]
        for (b, c, row0), halo in zip(chunks_of(step), halos):
            rows_here = pl.ds(pl.multiple_of(row0, CHUNK), CHUNK)
            at = lambda ref: ref.at[rows_here]
            one_chunk(b, c, at(p_ref), halo, at(dy_ref), at(ypre_ref), at(pre_ref), st_ref.at[row0 // CHUNK], at(dp_ref),
                      *shared, [carry.at[b] for carry in carries])

        @pl.when(step == n_steps - 1)
        def _():
            exchange.finish()

    def one_chunk(b, c, p_ref, halo, dy_ref, ypre_ref, pre_ref, st_ref, dp_ref, cw_ref, pw_ref, row_refs, gcw_ref,
                  gpw_ref, grad_row_refs, constants_refs, carries):
        dtb_ref, alog_ref, dx_ref, nw_ref, ps_ref = row_refs
        gcb_ref, gdtb_ref, galog_ref, gd_ref, gnw_ref, gps_ref = grad_row_refs
        ds_carry, dpre_next, dq_next = carries
        e_ref, et_ref, etf_ref, tril_ref, triu_ref = constants_refs

        @pl.when(c == nc - 1)
        def _():
            ds_carry[...] = jnp.zeros_like(ds_carry)
            dpre_next[...] = jnp.zeros_like(dpre_next)
            dq_next[...] = jnp.zeros_like(dq_next)

        valid = (c > 0) | (lax.broadcasted_iota(jnp.int32, (CHUNK, 1), 0) >= PAD_ROWS)
        first = c > 0

        u = p_ref[:, 0:D_POOL]
        u_halo = jnp.where(first, halo, 0.0)
        inv_cnt, lane = _pool_inv_count(c)
        pooled = _pool_window_sums(jnp.concatenate([u_halo, u], axis=0), lane) * inv_cnt - u
        dyp = dy_ref[:, 0:D_POOL]
        ps = ps_ref[...]
        dmixed = dyp * ps
        mixed, dpooled = [], []
        for g in range(len(POOL_WINDOWS)):
            gsl = slice(g * 128, (g + 1) * 128)
            pw = pw_ref[g]
            mixed.append(_dot(pooled[:, gsl], pw))
            dpooled.append(_dot_nt(dmixed[:, gsl], pw))
            gpw_ref[g] += _dot_tn(pooled[:, gsl], dmixed[:, gsl])
        gps_ref[...] += jnp.sum(dyp * jnp.concatenate(mixed, axis=1), axis=0, keepdims=True)
        dpooled = jnp.concatenate(dpooled, axis=1)
        dq = dpooled * inv_cnt
        du = _pool_window_sums_ahead(jnp.concatenate([dq, dq_next[...]], axis=0), lane) - dpooled
        dq_next[...] = dq[0:HALO, :]
        dp_ref[:, 0:D_POOL] = _mx(du)

        yp = ypre_ref[...]
        z = p_ref[:, OFF_Z:OFF_Z + D_SSM]
        sz, dsz = _silu_and_grad(z)
        yz = yp * sz
        do = dy_ref[:, D_POOL:]
        nw_row = nw_ref[...]
        dyz = []
        gnw = []
        for g in range(N_GROUPS):
            gs = slice(g * GROUP_CH, (g + 1) * GROUP_CH)
            r = lax.rsqrt(jnp.mean(yz[:, gs] * yz[:, gs], axis=-1, keepdims=True) + EPS)
            n = yz[:, gs] * r
            gnw.append(jnp.sum(do[:, gs] * n, axis=0, keepdims=True))
            dn = do[:, gs] * nw_row[:, gs]
            dyz.append(r * (dn - n * jnp.mean(dn * n, axis=-1, keepdims=True)))
        gnw_ref[...] += jnp.concatenate(gnw, axis=1)
        dyz = jnp.concatenate(dyz, axis=1)
        dp_ref[:, OFF_Z:OFF_Z + D_SSM] = _mx(dyz * yp * dsz)
        dyp_ssm = dyz * sz

        xc, dsilu = _silu_and_grad(pre_ref[...])
        dtr = p_ref[:, OFF_DT:OFF_DT + 128]
        dt, a_row, a_col, dt_pre, head = _dt_and_cumsum(dtr, dtb_ref[...], alog_ref[...], valid, tril_ref[...])
        dxs, dbm, dcm, ddt, da, dd, ds_prev = _ssd_chunk_bwd(
            xc[:, 0:D_SSM], xc[:, D_SSM:D_SSM + 512], xc[:, D_SSM + 512:], dt, a_row, a_col, st_ref[...],
            ds_carry[...], dyp_ssm, dx_ref[...], e_ref[...], et_ref[...], etf_ref[...], triu_ref[...])
        ds_carry[...] = ds_prev
        gd_ref[...] += dd
        galog_ref[...] += da * a_row
        ddtr = jnp.where(valid & head, ddt * _sigmoid(dt_pre), 0.0)
        gdtb_ref[...] += jnp.sum(ddtr, axis=0, keepdims=True)
        dp_ref[:, OFF_DT:OFF_DT + 128] = _mx(ddtr)

        dpre = jnp.concatenate([dxs, dbm, dcm], axis=1) * dsilu
        gcb_ref[...] += jnp.sum(dpre, axis=0, keepdims=True)
        dext = jnp.concatenate([dpre, dpre_next[...]], axis=0)
        dpre_next[...] = dpre[0:HALO, :]
        ups = [_shift_up(dext, 3 - k) for k in range(4)]
        xbc = p_ref[:, OFF_X:OFF_X + D_XBC]
        gcw_ref[...] += jnp.concatenate([jnp.sum(xbc * ups[k], axis=0, keepdims=True) for k in range(4)], axis=0)
        cw = cw_ref[...]
        dp_ref[:, OFF_X:OFF_X + D_XBC] = _mx(cw[3:4, :] * ups[3] + cw[2:3, :] * ups[2]
                                             + cw[1:2, :] * ups[1] + cw[0:1, :] * ups[0])

    blk = lambda w: pl.BlockSpec((2 * CHUNK, w), lambda s: (block_of(s), 0))

    def halo_rows(s):
        b = jnp.minimum(s // pairs, nb - 1)
        below = block_of(s) * 2 * hb - 1
        return jnp.where(block_of(s) == b * pairs, _chunk_block(b, 0, nb, nc) * hb + hb - 1, below)

    hbm = pl.BlockSpec(memory_space=pl.ANY)
    outs = pl.pallas_call(
        body, grid=(n_steps,), name="mixer_bwd",
        in_specs=[blk(PROJ_W), pl.BlockSpec((HALO, D_POOL), lambda s: (halo_rows(s), 0)), blk(D_MIX), blk(D_SSM),
                  blk(D_XBC), pl.BlockSpec((2, D_SSM, D_STATE), lambda s: (block_of(s), 0, 0)),
                  _full((4, D_XBC)), _full((1, sum(row_widths))), _full((4, 128, 128))] + [hbm] * ns,
        out_specs=[blk(PROJ_W), _full((4, D_XBC)), _full((1, sum(grad_row_widths))), _full((4, 128, 128))] + [hbm] * ns,
        out_shape=[jax.ShapeDtypeStruct((m, PROJ_W), MXU_DTYPE), jax.ShapeDtypeStruct((4, D_XBC), F32),
                   jax.ShapeDtypeStruct((1, sum(grad_row_widths)), F32), jax.ShapeDtypeStruct((4, 128, 128), F32)]
        + _ChipExchange.out_shapes(chip_sums, [False] * ns),
        scratch_shapes=[pltpu.VMEM((nb, D_SSM, D_STATE), F32), pltpu.VMEM((nb, HALO, D_XBC), F32),
                        pltpu.VMEM((nb, HALO, D_POOL), F32)] + _ssd_constant_scratch(constants)
        + _ChipExchange.scratch(ns),
        compiler_params=_params("arbitrary"),
    )(proj, proj, dy, ypre, conv_pre, states, cw, rows, pool_w, *chip_sums)
    dproj, g_cw, g_rows, g_pw = outs[:4]
    offs = np.cumsum([0] + grad_row_widths)
    g_cb, g_dtb, g_alog, g_d, g_nw, g_ps = (g_rows[:, a:b] for a, b in zip(offs[:-1], offs[1:]))
    return (dproj, g_cw, g_cb, g_dtb, g_alog, g_d, g_nw, g_pw, g_ps), outs[4:]


MESH_IDS = pl.DeviceIdType.MESH
_HBM = pl.BlockSpec(memory_space=pltpu.HBM)


def _coords():
    return lax.axis_index("x"), lax.axis_index("y"), lax.axis_index("c")


def _to_sibling(ins, got, send_sems, recv_sems, whole=()):
    x, y, c = _coords()
    theirs = lambda t, k: ins[t].at[k] if len(ins[t].shape) == 3 else ins[t].at[k, 1 - c]
    pairs = [(theirs(t, k), got[t].at[k]) for t in range(len(ins)) for k in range(4)] + list(whole)
    return [pltpu.make_async_remote_copy(
        src_ref=src, dst_ref=dst, send_sem=send_sems.at[i], recv_sem=recv_sems.at[i], device_id=(x, y, 1 - c),
        device_id_type=MESH_IDS) for i, (src, dst) in enumerate(pairs)]


def _other_chips(x, y):
    return [(1 - x, y), (x, 1 - y), (1 - x, 1 - y)]


class _Gather:
    def __init__(self, ins, outs, by_cols, send_sems, recv_sems, local_sems):
        self.ins, self.outs, self.by_cols, self.n = ins, outs, by_cols, len(ins)
        self.send_sems, self.recv_sems, self.local_sems = send_sems, recv_sems, local_sems
        self.x, self.y, self.c = _coords()
        self.me, self.sibling = (self.x, self.y, self.c), (self.x, self.y, 1 - self.c)
        self.chips = _other_chips(self.x, self.y)

    @staticmethod
    def scratch(n):
        return [pltpu.SemaphoreType.DMA((7 * n,)), pltpu.SemaphoreType.DMA((7 * n,)), pltpu.SemaphoreType.DMA((n,))]

    @staticmethod
    def out_shapes(shards, by_cols):
        return [jax.ShapeDtypeStruct((s.shape[0], N_DEV * s.shape[1]) if cols else (N_DEV,) + s.shape, s.dtype)
                for s, cols in zip(shards, by_cols)]

    def _block(self, t, device):
        idx = 4 * device[0] + 2 * device[1] + device[2]
        if not self.by_cols[t]:
            return self.outs[t].at[idx]
        w = self.ins[t].shape[1]
        return self.outs[t].at[:, pl.ds(pl.multiple_of(idx * w, w), w)]

    def _copy(self, t, k, block, to, own=False):
        dst = self._block(t, block)
        return pltpu.make_async_remote_copy(
            src_ref=self.ins[t] if own else dst, dst_ref=dst, send_sem=self.send_sems.at[t * 7 + k],
            recv_sem=self.recv_sems.at[t * 7 + k], device_id=to, device_id_type=MESH_IDS)

    def _mine(self):
        return [pltpu.make_async_copy(self.ins[t], self._block(t, self.me), self.local_sems.at[t])
                for t in range(self.n)]

    def _first(self):
        cps = []
        for t in range(self.n):
            cps.append(self._copy(t, 0, self.me, self.sibling, own=True))
            cps += [self._copy(t, 1 + j, self.me, (*chip, self.c), own=True) for j, chip in enumerate(self.chips)]
        return cps

    def _passed(self):
        return [self._copy(t, 4 + j, (*chip, self.c), self.sibling)
                for j, chip in enumerate(self.chips) for t in range(self.n)]

    def start(self):
        for cp in self._mine() + self._first():
            cp.start()

    def forward(self):
        for j, chip in enumerate(self.chips):
            for t in range(self.n):
                self._copy(t, 1 + j, (*chip, self.c), self.me).wait_recv()
                self._copy(t, 4 + j, (*chip, self.c), self.sibling).start()

    def finish(self):
        for t in range(self.n):
            self._copy(t, 0, self.sibling, self.me).wait_recv()
            for j, chip in enumerate(self.chips):
                self._copy(t, 4 + j, (*chip, 1 - self.c), self.me).wait_recv()
        for cp in self._first() + self._passed():
            cp.wait_send()
        for cp in self._mine():
            cp.wait()


def _weight_gather(shards):
    n = len(shards)

    def body(*refs):
        g = _Gather(refs[:n], refs[n:2 * n], [False] * n, *refs[2 * n:])
        g.start()
        g.forward()
        g.finish()

    return pl.pallas_call(
        body, name="weight_gather",
        in_specs=[_HBM] * n, out_specs=[_HBM] * n,
        out_shape=_Gather.out_shapes(shards, [False] * n),
        scratch_shapes=_Gather.scratch(n),
    )(*shards)


def _grad_exchange_d2d(gs, name, swapped=()):
    n, ns = len(gs), len(swapped)

    def body(*refs):
        ins, whole_ins = refs[:n], refs[n:n + ns]
        got, whole_got = refs[n + ns:2 * n + ns], refs[2 * n + ns:2 * (n + ns)]
        send_sems, recv_sems = refs[2 * (n + ns):]
        remote = _to_sibling(ins, got, send_sems, recv_sems, zip(whole_ins, whole_got))
        for cp in remote:
            cp.start()
        for cp in remote:
            cp.wait_recv()
        for cp in remote:
            cp.wait_send()

    outs = pl.pallas_call(
        body, name=name,
        in_specs=[_HBM] * (n + ns), out_specs=[_HBM] * (n + ns),
        out_shape=[jax.ShapeDtypeStruct((4,) + g.shape[-2:], g.dtype) for g in gs]
        + [jax.ShapeDtypeStruct(a.shape, a.dtype) for a in swapped],
        scratch_shapes=[pltpu.SemaphoreType.DMA((4 * n + ns,)), pltpu.SemaphoreType.DMA((4 * n + ns,))],
    )(*gs, *swapped)
    return outs[:n], outs[n:]


def _small_allreduce(pack):
    rows = pack.shape[0]

    def body(p_ref, o_ref, sib_ref, parts_ref, send_sems, recv_sems):
        x, y, c = _coords()
        my_chip = 2 * x + y
        swap = pltpu.make_async_remote_copy(src_ref=p_ref, dst_ref=sib_ref, send_sem=send_sems.at[0],
                                            recv_sem=recv_sems.at[0], device_id=(x, y, 1 - c), device_id_type=MESH_IDS)
        swap.start()
        swap.wait_recv()
        parts_ref[my_chip] = p_ref[...] + sib_ref[...]
        remote = [pltpu.make_async_remote_copy(
            src_ref=parts_ref.at[my_chip], dst_ref=parts_ref.at[my_chip], send_sem=send_sems.at[1 + j],
            recv_sem=recv_sems.at[1 + j], device_id=(cx, cy, c), device_id_type=MESH_IDS)
            for j, (cx, cy) in enumerate(_other_chips(x, y))]
        for cp in remote:
            cp.start()
        for j, (cx, cy) in enumerate(_other_chips(x, y)):
            slot = parts_ref.at[2 * cx + cy]
            pltpu.make_async_remote_copy(src_ref=slot, dst_ref=slot, send_sem=send_sems.at[1 + j],
                                         recv_sem=recv_sems.at[1 + j], device_id=(cx, cy, c),
                                         device_id_type=MESH_IDS).wait_recv()
        o_ref[...] = ((parts_ref[0] + parts_ref[1]) + parts_ref[2]) + parts_ref[3]
        swap.wait_send()
        for cp in remote:
            cp.wait_send()

    vmem = pl.BlockSpec(memory_space=pltpu.VMEM)
    return pl.pallas_call(
        body, name="small_allreduce", in_specs=[vmem], out_specs=vmem,
        out_shape=jax.ShapeDtypeStruct((rows, 128), F32),
        scratch_shapes=[pltpu.VMEM((rows, 128), F32), pltpu.VMEM((4, rows, 128), F32),
                        pltpu.SemaphoreType.DMA((4,)), pltpu.SemaphoreType.DMA((4,))],
    )(pack)


class _ChipExchange:
    def __init__(self, ins, outs, whole, send_sems, recv_sems, local_sems):
        self.ins, self.outs, self.whole, self.n = ins, outs, whole, len(ins)
        self.send_sems, self.recv_sems, self.local_sems = send_sems, recv_sems, local_sems
        self.x, self.y, self.c = _coords()
        self.my_chip = 2 * self.x + self.y
        self.chips = _other_chips(self.x, self.y)

    @staticmethod
    def scratch(n):
        return [pltpu.SemaphoreType.DMA((3 * n,)), pltpu.SemaphoreType.DMA((3 * n,)), pltpu.SemaphoreType.DMA((n,))]

    def _src(self, t, k):
        return self.ins[t] if self.whole[t] else self.ins[t].at[k]

    def _local(self):
        return [pltpu.make_async_copy(self._src(t, self.my_chip), self.outs[t].at[self.my_chip], self.local_sems.at[t])
                for t in range(self.n)]

    def _remote(self):
        return [pltpu.make_async_remote_copy(
            src_ref=self._src(t, 2 * cx + cy), dst_ref=self.outs[t].at[self.my_chip],
            send_sem=self.send_sems.at[t * 3 + j], recv_sem=self.recv_sems.at[t * 3 + j],
            device_id=(cx, cy, self.c), device_id_type=MESH_IDS)
            for t in range(self.n) for j, (cx, cy) in enumerate(self.chips)]

    def start(self):
        for cp in self._remote() + self._local():
            cp.start()

    def finish(self):
        for t in range(self.n):
            for j, (cx, cy) in enumerate(self.chips):
                slot = self.outs[t].at[2 * cx + cy]
                pltpu.make_async_remote_copy(
                    src_ref=slot, dst_ref=slot, send_sem=self.send_sems.at[t * 3 + j],
                    recv_sem=self.recv_sems.at[t * 3 + j], device_id=(cx, cy, self.c),
                    device_id_type=MESH_IDS).wait_recv()
        for cp in self._remote():
            cp.wait_send()
        for cp in self._local():
            cp.wait()

    @staticmethod
    def out_shapes(arrs, whole):
        return [jax.ShapeDtypeStruct(((4,) + a.shape) if w else a.shape, a.dtype) for a, w in zip(arrs, whole)]


_SEMAPHORES = pl.BlockSpec(memory_space=pltpu.SEMAPHORE)
_SIDE_EFFECT = pltpu.SideEffectType.DATAFLOW_SIDE_EFFECTING


def _split_exchange_copies(srcs, lands, whole, send_sems, recv_sems, waiting):
    x, y, c = _coords()
    copies = []
    for t in range(len(srcs)):
        for j, (cx, cy) in enumerate(_other_chips(x, y)):
            src = srcs[t] if whole[t] else srcs[t].at[2 * cx + cy]
            dst = lands[t].at[2 * cx + cy] if waiting else lands[t].at[2 * x + y]
            copies.append(pltpu.make_async_remote_copy(
                src_ref=src, dst_ref=dst, send_sem=send_sems.at[3 * t + j], recv_sem=recv_sems.at[3 * t + j],
                device_id=(cx, cy, c), device_id_type=MESH_IDS))
    return copies


def _chip_exchange_start(arrays, whole, name):
    n = len(arrays)

    def body(*refs):
        srcs, lands = refs[:n], refs[n:2 * n]
        send_sems, recv_sems = refs[2 * n:2 * n + 2]
        for cp in _split_exchange_copies(srcs, lands, whole, send_sems, recv_sems, waiting=False):
            cp.start()
        token = refs[-1]
        token[...] = jnp.zeros_like(token)

    land_shapes = [((4,) + a.shape) if w else a.shape for a, w in zip(arrays, whole)]
    hbm = lambda shape, a: pltpu.HBM(shape, a.dtype)
    outs = pl.pallas_call(
        body, name=name,
        out_shape=(pltpu.SemaphoreType.DMA((3 * n,)), pltpu.SemaphoreType.DMA((3 * n,)),
                   *[hbm(a.shape, a) for a in arrays], *[hbm(s, a) for s, a in zip(land_shapes, arrays)],
                   jax.ShapeDtypeStruct((8, 128), F32)),
        in_specs=(_HBM,) * (2 * n),
        out_specs=(_SEMAPHORES, _SEMAPHORES) + (_HBM,) * (2 * n) + (pl.BlockSpec(memory_space=pltpu.VMEM),),
        input_output_aliases={i: 2 + i for i in range(2 * n)},
        compiler_params=pltpu.CompilerParams(has_side_effects=_SIDE_EFFECT),
    )(*[pltpu.with_memory_space_constraint(a, pltpu.HBM) for a in arrays],
      *[pltpu.with_memory_space_constraint(lax.empty(s, a.dtype), pltpu.HBM) for s, a in zip(land_shapes, arrays)])
    return outs[0], outs[1], outs[2:2 + n], outs[2 + n:2 + 2 * n], outs[-1]


def _chip_exchange_wait(send_sems, recv_sems, srcs, lands, whole, after, name):
    n = len(srcs)

    def body(*refs):
        src_refs, land_refs = refs[:n], refs[n:2 * n]
        for cp in _split_exchange_copies(src_refs, land_refs, whole, refs[2 * n], refs[2 * n + 1], waiting=True):
            cp.wait_send()
            cp.wait_recv()

    outs = pl.pallas_call(
        body, name=name, out_shape=tuple(pltpu.HBM(a.shape, a.dtype) for a in (*srcs, *lands)),
        in_specs=(_HBM,) * (2 * n) + (_SEMAPHORES, _SEMAPHORES, pl.BlockSpec(memory_space=pl.ANY)),
        out_specs=(_HBM,) * (2 * n), input_output_aliases={i: i for i in range(2 * n)},
        compiler_params=pltpu.CompilerParams(has_side_effects=_SIDE_EFFECT),
    )(*srcs, *lands, send_sems, recv_sems, after)
    return outs[:n], outs[n:]


def _sum_two(a, b, name):
    def body(a_ref, b_ref, o_ref):
        o_ref[...] = a_ref[...] + b_ref[...]

    return pl.pallas_call(body, name=name, out_shape=jax.ShapeDtypeStruct(a.shape, a.dtype))(a, b)


def _sum_chips(landed, own, chip, name):
    def body(chip_ref, l_ref, own_ref, o_ref):
        part = lambda k: jnp.where(chip_ref[0] == k, own_ref[...], l_ref[k])
        o_ref[...] = ((part(0) + part(1)) + part(2)) + part(3)

    grid_spec = pltpu.PrefetchScalarGridSpec(
        num_scalar_prefetch=1, grid=(1,),
        in_specs=[pl.BlockSpec(landed.shape, lambda i, c: (0, 0, 0)), pl.BlockSpec(own.shape, lambda i, c: (0, 0))],
        out_specs=pl.BlockSpec(own.shape, lambda i, c: (0, 0)))
    return pl.pallas_call(body, grid_spec=grid_spec, name=name,
                          out_shape=jax.ShapeDtypeStruct(own.shape, own.dtype))(chip, landed, own)


def _row_tile(rows, cols, n_arrays):
    budget = 24 * 1024 * 1024
    padded = -(-cols // 128) * 128
    step = 16 if rows % 16 == 0 else 8
    tr = max(step, budget // (n_arrays * 2 * 4 * padded) // step * step)
    while rows % tr:
        tr -= step
    return tr


def _chip_sum(g, got, core, name):
    rows, cols = g.shape[-2:]
    tr = _row_tile(rows, cols, 3)

    def body(c_ref, a_ref, b_ref, o_ref):
        o_ref[...] = (a_ref[...] + b_ref[...].astype(F32)).astype(o_ref.dtype)

    mine = (pl.BlockSpec((None, tr, cols), lambda k, i, c: (k, i, 0)) if g.ndim == 3
            else pl.BlockSpec((None, None, tr, cols), lambda k, i, c: (k, c[0], i, 0)))
    grid_spec = pltpu.PrefetchScalarGridSpec(
        num_scalar_prefetch=1, grid=(4, rows // tr),
        in_specs=[mine, pl.BlockSpec((None, tr, cols), lambda k, i, c: (k, i, 0))],
        out_specs=pl.BlockSpec((None, tr, cols), lambda k, i, c: (k, i, 0)))
    return pl.pallas_call(body, grid_spec=grid_spec, name=name,
                          out_shape=jax.ShapeDtypeStruct((4, rows, cols), MXU_DTYPE),
                          compiler_params=_params("arbitrary", "arbitrary"))(core, g, got)


def _adamw_math(w, g, m, v):
    m2 = ADAM_B1 * m + (1.0 - ADAM_B1) * g
    v2 = ADAM_B2 * v + (1.0 - ADAM_B2) * (g * g)
    m_hat = m2 / (1.0 - ADAM_B1 ** ADAM_STEP)
    v_hat = v2 / (1.0 - ADAM_B2 ** ADAM_STEP)
    delta = -ADAM_LR * (m_hat / (jnp.sqrt(v_hat) + ADAM_EPS) + ADAM_WD * w)
    return delta, m2, v2


def _adamw(parts, w, m, v, name, own=None, chip=None):
    rows, cols = w.shape
    tr = _row_tile(rows, cols, 11 if own is None else 15)

    def body(*refs):
        if own is None:
            p_ref, w_ref, m_ref, v_ref, g_ref, d_ref, m2_ref, v2_ref = refs
            part = lambda k: p_ref[k].astype(F32)
        else:
            chip_ref, p_ref, own_ref, w_ref, m_ref, v_ref, g_ref, d_ref, m2_ref, v2_ref = refs
            part = lambda k: jnp.where(chip_ref[0] == k, own_ref[k], p_ref[k]).astype(F32)
        g = ((part(0) + part(1)) + part(2)) + part(3)
        d, m2, v2 = _adamw_math(w_ref[...], g, m_ref[...], v_ref[...])
        g_ref[...] = g
        d_ref[...] = d
        m2_ref[...] = m2
        v2_ref[...] = v2

    blk = pl.BlockSpec((tr, cols), lambda i, *_: (i, 0))
    pblk = pl.BlockSpec((4, tr, cols), lambda i, *_: (0, i, 0))
    out = jax.ShapeDtypeStruct((rows, cols), F32)
    if own is None:
        return pl.pallas_call(body, grid=(rows // tr,), name=name, in_specs=[pblk, blk, blk, blk],
                              out_specs=[blk] * 4, out_shape=[out] * 4,
                              compiler_params=_params("arbitrary"))(parts, w, m, v)
    grid_spec = pltpu.PrefetchScalarGridSpec(num_scalar_prefetch=1, grid=(rows // tr,),
                                             in_specs=[pblk, pblk, blk, blk, blk], out_specs=[blk] * 4)
    return pl.pallas_call(body, grid_spec=grid_spec, name=name, out_shape=[out] * 4,
                          compiler_params=_params("arbitrary"))(chip, parts, own, w, m, v)


def _adamw_transposed(parts_t, w, m, v, name):
    rows, cols = w.shape
    tc = 256

    def body(p_ref, w_ref, m_ref, v_ref, g_ref, d_ref, m2_ref, v2_ref):
        part = lambda k: p_ref[k].astype(F32)
        g = (((part(0) + part(1)) + part(2)) + part(3)).T
        d, m2, v2 = _adamw_math(w_ref[...], g, m_ref[...], v_ref[...])
        g_ref[...] = g
        d_ref[...] = d
        m2_ref[...] = m2
        v2_ref[...] = v2

    blk = pl.BlockSpec((rows, tc), lambda i: (0, i))
    out = jax.ShapeDtypeStruct((rows, cols), F32)
    return pl.pallas_call(body, grid=(cols // tc,), name=name,
                          in_specs=[pl.BlockSpec((4, tc, rows), lambda i: (0, i, 0)), blk, blk, blk],
                          out_specs=[blk] * 4, out_shape=[out] * 4,
                          compiler_params=_params("arbitrary"))(parts_t, w, m, v)


def _adamw_small(gs, ws, ms, vs):
    n = len(ws)

    def body(*refs):
        g_refs, w_refs, m_refs, v_refs = (refs[k * n:(k + 1) * n] for k in range(4))
        d_refs, m2_refs, v2_refs = (refs[(4 + k) * n:(5 + k) * n] for k in range(3))
        for t in range(n):
            d, m2, v2 = _adamw_math(w_refs[t][...], g_refs[t][...], m_refs[t][...], v_refs[t][...])
            d_refs[t][...] = d
            m2_refs[t][...] = m2
            v2_refs[t][...] = v2

    outs = pl.pallas_call(body, name="adamw_small",
                          out_shape=[jax.ShapeDtypeStruct(w.shape, F32) for w in ws] * 3)(*gs, *ws, *ms, *vs)
    return outs[:n], outs[n:2 * n], outs[2 * n:]


_PACK_TILE = 8 * 128


def _pack(arrays):
    rows = []
    for a in arrays:
        flat = a.astype(F32).reshape(-1)
        rows.append(jnp.pad(flat, (0, -flat.shape[0] % _PACK_TILE)).reshape(-1, 128))
    return jnp.concatenate(rows, axis=0)


def _unpack(pack, shapes):
    out, r = [], 0
    for s in shapes:
        n = int(np.prod(s))
        out.append(pack[r:r + -(-n // 128)].reshape(-1)[:n].reshape(s))
        r += -(-n // _PACK_TILE) * 8
    return out


def _pad128(v):
    v = v.reshape(1, -1).astype(F32)
    return jnp.pad(v, ((0, 0), (0, 128 - v.shape[1])))


_WEIGHTS = ["meta", "norm_mix_w", "w_in", "pool_w", "pool_scale", "conv_w", "conv_b", "dt_bias", "a_log", "d_skip",
            "ssm_norm_w", "w_out", "norm_ffn_w", "w_ff1", "w_ff2", "norm_f_w"]
_BIG = ["w_in", "w_out", "w_ff1", "w_ff2"]
_SMALL = [n for n in _WEIGHTS if n not in _BIG]


def kernel(x, meta, norm_mix_w, w_in, pool_w, pool_scale, conv_w, conv_b, dt_bias, a_log, d_skip, ssm_norm_w, w_out, norm_ffn_w, w_ff1, w_ff2, norm_f_w, loss_target, m_meta, m_norm_mix_w, m_w_in, m_pool_w, m_pool_scale, m_conv_w, m_conv_b, m_dt_bias, m_a_log, m_d_skip, m_ssm_norm_w, m_w_out, m_norm_ffn_w, m_w_ff1, m_w_ff2, m_norm_f_w, v_meta, v_norm_mix_w, v_w_in, v_pool_w, v_pool_scale, v_conv_w, v_conv_b, v_dt_bias, v_a_log, v_d_skip, v_ssm_norm_w, v_w_out, v_norm_ffn_w, v_w_ff1, v_w_ff2, v_norm_f_w):
    wts = dict(meta=meta, norm_mix_w=norm_mix_w, w_in=w_in, pool_w=pool_w, pool_scale=pool_scale, conv_w=conv_w,
               conv_b=conv_b, dt_bias=dt_bias, a_log=a_log, d_skip=d_skip, ssm_norm_w=ssm_norm_w, w_out=w_out,
               norm_ffn_w=norm_ffn_w, w_ff1=w_ff1, w_ff2=w_ff2, norm_f_w=norm_f_w)
    mom1 = dict(zip(_WEIGHTS, (m_meta, m_norm_mix_w, m_w_in, m_pool_w, m_pool_scale, m_conv_w, m_conv_b, m_dt_bias,
                               m_a_log, m_d_skip, m_ssm_norm_w, m_w_out, m_norm_ffn_w, m_w_ff1, m_w_ff2, m_norm_f_w)))
    mom2 = dict(zip(_WEIGHTS, (v_meta, v_norm_mix_w, v_w_in, v_pool_w, v_pool_scale, v_conv_w, v_conv_b, v_dt_bias,
                               v_a_log, v_d_skip, v_ssm_norm_w, v_w_out, v_norm_ffn_w, v_w_ff1, v_w_ff2, v_norm_f_w)))
    xi, yi, ci = _coords()
    dev = 4 * xi + 2 * yi + ci
    win_cols = w_in.shape[-1]
    cw_cols = conv_w.shape[-1]

    nb, seq, _ = x.shape
    core = jnp.reshape(ci, (1,)).astype(jnp.int32)
    owners = lambda a: a.reshape((4, 2) + a.shape[1:])

    lead_pack = jnp.zeros((N_META, 512), F32)
    lead_pack = lead_pack.at[:, :128].set(meta).at[:4, 128:128 + cw_cols].set(conv_w[0])
    g_win, g_lead = _weight_gather([_mx(w_in[0]), lead_pack])
    win_full = _assemble_bands(g_win, PROJ_W)
    meta_full = jnp.transpose(g_lead[:, :, :128], (1, 0, 2)).reshape(N_META, D_MODEL)
    cw_full = jnp.transpose(g_lead[:, :4, 128:128 + cw_cols], (1, 0, 2)).reshape(4, D_XBC)

    lead = jnp.concatenate([jnp.zeros((PAD_ROWS, D_MODEL), F32), meta_full] * nb, axis=0)
    x_rows = x.reshape(nb * seq, D_MODEL)
    tgt_rows = loss_target.reshape(nb * seq, D_MODEL)
    dt_bias_p, a_log_p = _pad128(dt_bias), _pad128(a_log)
    d_x = jnp.repeat(d_skip.reshape(1, N_HEADS).astype(F32), HEAD_DIM, axis=1)
    norm_f_row = norm_f_w.reshape(1, D_MODEL)

    hn1, proj = _in_proj(x_rows, lead, norm_mix_w, win_full)
    late_cols = [False, True, False]
    y, ypre, conv_pre, states, (g_wout, wff1_full, g_wff2) = _mixer_fwd(
        proj, cw_full, conv_b, dt_bias_p, a_log_p, d_x, ssm_norm_w, pool_w[0], pool_scale, nb,
        [_mx(w_out[0]), _mx(w_ff1[0]), _mx(w_ff2[0])], late_cols)
    wout_full = g_wout.reshape(D_MIX, D_MODEL)
    wff2_full = g_wff2.reshape(D_FF, D_MODEL)
    loss, gr_nf, gr_nffn, ff, da, hn2, dh1, dh2, dy = _ffn_fwd_bwd(
        x_rows, lead, y, tgt_rows, wout_full, norm_ffn_w, wff1_full, wff2_full, norm_f_row)
    by_owner = lambda g, rows: owners(g.reshape(N_DEV, rows // N_DEV, D_MODEL))
    gr_wff2, _ = _tn_matmul(ff, dh2, "grad_w_ff2", tka=1024, max_slab=2816)
    gr_wff1_t, _ = _tn_matmul(da, hn2, "grad_w_ff1", tka=1024, max_slab=2816)
    gr_wout, ff_got = _tn_matmul(y, dh1, "grad_w_out", tka=1024, max_slab=2816,
                                 swap=[by_owner(gr_wff1_t[1], D_FF), by_owner(gr_wff2[1], D_FF)])
    late_parts = [by_owner(gr_wout[0], D_MIX), by_owner(gr_wff1_t[0], D_FF), by_owner(gr_wff2[0], D_FF)]
    wout_got, _ = _grad_exchange_d2d([by_owner(gr_wout[1], D_MIX)], "grad_exchange_d2d_w_out")
    late_got = list(wout_got) + list(ff_got)
    late_sums = [_chip_sum(late_parts[t], late_got[t], core, "chip_sum_late_%d" % t) for t in range(3)]
    (dproj, gr_cw, gr_cb, gr_dtb, gr_alog, gr_d, gr_nw, gr_pw, gr_ps), late_exchanged = _mixer_bwd(
        proj, dy, ypre, conv_pre, states, cw_full, dt_bias_p, a_log_p, d_x, ssm_norm_w, pool_w[0], pool_scale, nb,
        late_sums)

    early = dict(pool_w=gr_pw, pool_scale=gr_ps, conv_w=gr_cw, conv_b=gr_cb, dt_bias=gr_dtb[:, :N_HEADS],
                 a_log=gr_alog[:, :N_HEADS], d_skip=gr_d[:, :N_HEADS], ssm_norm_w=gr_nw, norm_ffn_w=gr_nffn,
                 norm_f_w=gr_nf, loss=loss[0:1, 0:1])
    early_pack = _pack(list(early.values()))
    win_mine, win_theirs = _tn_matmul_banded(hn1, dproj, core, "grad_w_in", win_cols, tka=512)
    win_got, (early_got,) = _grad_exchange_d2d([win_theirs], "grad_exchange_d2d_w_in", swapped=[early_pack])
    win_sum = _chip_sum(win_mine, win_got[0], core, "chip_sum_w_in")
    early_chip = _sum_two(early_pack, early_got, "chip_sum_small")
    whole = [False, True]
    send_sems, recv_sems, sent, landing, started = _chip_exchange_start([win_sum, early_chip], whole, "w_in_exchange_start")
    gx_rows, gr_nmix, gr_meta = _in_proj_bwd(dproj, x_rows, lead, dh1, norm_mix_w, win_full, started)
    (win_sum, early_chip), (win_landed, early_landed) = _chip_exchange_wait(
        send_sems, recv_sems, sent, landing, whole, gr_nmix, "w_in_exchange_wait")
    parts = dict(w_in=win_landed, w_out=late_exchanged[0], w_ff1=late_exchanged[1], w_ff2=late_exchanged[2])
    my_chip = jnp.reshape(2 * xi + yi, (1,)).astype(jnp.int32)
    early_sum = _sum_chips(early_landed, early_chip, my_chip, "small_sum")

    tail = dict(meta=gr_meta, norm_mix_w=gr_nmix)
    tail_sum = _small_allreduce(_pack(list(tail.values())))
    gs = dict(zip(early, _unpack(early_sum, [a.shape for a in early.values()])))
    gs.update(zip(tail, _unpack(tail_sum, [a.shape for a in tail.values()])))
    gs["meta"] = lax.dynamic_slice_in_dim(gs["meta"], dev * 128, 128, axis=1)
    gs["conv_w"] = lax.dynamic_slice_in_dim(gs["conv_w"], dev * cw_cols, cw_cols, axis=1)

    res = {}
    for n in _BIG:
        shp = wts[n].shape
        args = (parts[n], wts[n][0], mom1[n][0], mom2[n][0], "adamw_" + n)
        if n == "w_ff1":
            outs = _adamw_transposed(*args)
        elif n == "w_in":
            outs = _adamw(*args, own=win_sum, chip=my_chip)
        else:
            outs = _adamw(*args)
        res[n] = [o.reshape(shp) for o in outs]
    as2d = lambda a: a.reshape(-1, a.shape[-1])
    small_g = [as2d(gs[n].reshape(wts[n].shape)) for n in _SMALL]
    small_out = _adamw_small(small_g, *[[as2d(d[n]) for n in _SMALL] for d in (wts, mom1, mom2)])
    for k, n in enumerate(_SMALL):
        res[n] = [o[k].reshape(wts[n].shape) for o in (small_g,) + tuple(small_out)]

    grad_x = gx_rows.reshape(nb, seq, D_MODEL)
    return (gs["loss"][0, 0], grad_x, *[res[n][0] for n in _WEIGHTS], *[res[n][1] for n in _WEIGHTS],
            *[res[n][2] for n in _WEIGHTS], *[res[n][3] for n in _WEIGHTS])
```

```python
import numpy as np
import jax
import jax.numpy as jnp
from jax import lax
from jax.experimental import pallas as pl
from jax.experimental.pallas import tpu as pltpu

F32 = jnp.float32
MXU_DTYPE = jnp.bfloat16

D_MODEL = 1024
D_POOL = 512
D_SSM = 1536
D_XBC = 2560
N_HEADS = 24
HEAD_DIM = 64
N_GROUPS = 4
GROUP_CH = D_SSM // N_GROUPS
D_STATE = 128
CHUNK = 128
N_META = 16
LEAD = CHUNK
PAD_ROWS = LEAD - N_META
ROW_TILE = 2 * CHUNK
D_MIX = D_POOL + D_SSM
D_FF = 4096
PROJ_W = 4736
OFF_Z = D_POOL
OFF_X = D_POOL + D_SSM
OFF_DT = OFF_X + D_XBC
D_IN_PROJ = OFF_DT + N_HEADS
POOL_WINDOWS = (2, 4, 8, 16)
HALO = 16
EPS = 1e-5
N_DEV = 8

ADAM_LR, ADAM_B1, ADAM_B2, ADAM_EPS, ADAM_WD, ADAM_STEP = 0.001, 0.9, 0.999, 1e-08, 0.01, 10

VMEM_LIMIT = 60 * 1024 * 1024


def _mx(a):
    return a.astype(MXU_DTYPE)


def _dot(a, b):
    return jnp.dot(_mx(a), _mx(b), preferred_element_type=F32)


def _dot_nt(a, b):
    return lax.dot_general(_mx(a), _mx(b), (((1,), (1,)), ((), ())), preferred_element_type=F32)


def _dot_tn(a, b):
    return lax.dot_general(_mx(a), _mx(b), (((0,), (0,)), ((), ())), preferred_element_type=F32)


def _split3(x):
    hi = x.astype(MXU_DTYPE)
    r = x - hi.astype(F32)
    mid = r.astype(MXU_DTYPE)
    lo = (r - mid.astype(F32)).astype(MXU_DTYPE)
    return hi, mid, lo


def _exact_l(c, x):
    hi, mid, lo = _split3(x)
    f = lambda p: jnp.dot(c, p, preferred_element_type=F32)
    return f(hi) + f(mid) + f(lo)


def _exact_r(x, c):
    hi, mid, lo = x if isinstance(x, tuple) else _split3(x)
    f = lambda p: jnp.dot(p, c, preferred_element_type=F32)
    return f(hi) + f(mid) + f(lo)


def _contract(x, c):
    hi = x.astype(MXU_DTYPE)
    lo = (x - hi.astype(F32)).astype(MXU_DTYPE)
    return jnp.dot(hi, c, preferred_element_type=F32) + jnp.dot(lo, c, preferred_element_type=F32)


def _sigmoid(x):
    return jax.nn.sigmoid(x)


def _softplus(x):
    return jnp.maximum(x, 0.0) + jnp.log1p(jnp.exp(-jnp.abs(x)))


def _silu_and_grad(x):
    s = _sigmoid(x)
    y = x * s
    return y, s + y * (1.0 - s)


def _shift_up(ext, s):
    if s == 0:
        return ext[:CHUNK, :]
    return pltpu.roll(ext, ext.shape[0] - s, 0)[:CHUNK, :]


def _by_pool_group(lane, a2, a4, a8, a16):
    return jnp.where(lane < 128, a2, jnp.where(lane < 256, a4, jnp.where(lane < 384, a8, a16)))


def _pool_inv_count(chunk_idx):
    row = lax.broadcasted_iota(jnp.int32, (CHUNK, D_POOL), 0)
    lane = lax.broadcasted_iota(jnp.int32, (CHUNK, D_POOL), 1)
    pos1 = jnp.maximum(chunk_idx * CHUNK + row - (PAD_ROWS - 1), 1)
    w = _by_pool_group(lane, 2, 4, 8, 16)
    return 1.0 / jnp.minimum(pos1, w).astype(F32), lane


def _pool_window_sums(u_ext, lane):
    s2 = u_ext + pltpu.roll(u_ext, 1, 0)
    s4 = s2 + pltpu.roll(s2, 2, 0)
    s8 = s4 + pltpu.roll(s4, 4, 0)
    s16 = s8 + pltpu.roll(s8, 8, 0)
    return _by_pool_group(lane, s2[HALO:], s4[HALO:], s8[HALO:], s16[HALO:])


def _pool_window_sums_ahead(q_ext, lane):
    n = q_ext.shape[0]
    r2 = q_ext + pltpu.roll(q_ext, n - 1, 0)
    r4 = r2 + pltpu.roll(r2, n - 2, 0)
    r8 = r4 + pltpu.roll(r4, n - 4, 0)
    r16 = r8 + pltpu.roll(r8, n - 8, 0)
    return _by_pool_group(lane, r2[:CHUNK], r4[:CHUNK], r8[:CHUNK], r16[:CHUNK])


def _conv_pre(ext, xbc, cw, cb):
    s1 = pltpu.roll(ext, 1, 0)
    near = cw[3:4, :] * xbc + cw[2:3, :] * s1[HALO:, :]
    far = cw[1:2, :] * ext + cw[0:1, :] * s1
    return cb + near + pltpu.roll(far, 2, 0)[HALO:, :]


def _dt_and_cumsum(dtr, dt_bias, a_log, valid, tril):
    lane = lax.broadcasted_iota(jnp.int32, (CHUNK, 128), 1)
    head = lane < N_HEADS
    pre = dtr + dt_bias
    dt = jnp.where(valid & head, _softplus(pre), 0.0)
    a_row = jnp.where(head[0:1, :], -jnp.exp(a_log), 0.0)
    a_col = _exact_l(tril, dt * a_row)
    return dt, a_row, a_col, pre, head


def _decay(a_col, a_row_t, h, causal):
    seg = a_col[:, h:h + 1] - a_row_t[h:h + 1, :]
    return jnp.where(causal, jnp.exp(jnp.minimum(seg, 0.0)), 0.0)


def _ssd_chunk_fwd(xs, bm, cm, dt, a_col, s_prev, d_x, e_mat, et_f32):
    lane = lax.broadcasted_iota(jnp.int32, (CHUNK, 128), 1)
    rowi = lax.broadcasted_iota(jnp.int32, (CHUNK, CHUNK), 0)
    coli = lax.broadcasted_iota(jnp.int32, (CHUNK, CHUNK), 1)
    causal = rowi >= coli
    a_row_t = a_col.T
    ax = _exact_r(a_col, e_mat)
    dtx = _exact_r(dt, e_mat)
    xdt = xs * dtx
    ax_last = ax[CHUNK - 1:CHUNK, :]
    e_a = jnp.exp(ax)
    w_end = xdt * jnp.exp(ax_last - ax)
    cd_col = jnp.exp(jnp.sum(et_f32 * a_col[CHUNK - 1:CHUNK, :], axis=1, keepdims=True))
    ys, s_new = [], []
    for g in range(N_GROUPS):
        gs = slice(g * GROUP_CH, (g + 1) * GROUP_CH)
        bg = bm[:, g * D_STATE:(g + 1) * D_STATE]
        cg = cm[:, g * D_STATE:(g + 1) * D_STATE]
        sg = s_prev[gs, :]
        cb = _dot_nt(cg, bg)
        y_off = _dot_nt(cg, sg) * e_a[:, gs]
        s_new.append(sg * cd_col[gs, :] + _dot_tn(w_end[:, gs], bg))
        for pr in range(3):
            c0 = g * GROUP_CH + pr * 128
            xdt_p = xdt[:, c0:c0 + 128]
            h0 = g * 6 + pr * 2
            y0 = _dot(cb * _decay(a_col, a_row_t, h0, causal), xdt_p)
            y1 = _dot(cb * _decay(a_col, a_row_t, h0 + 1, causal), xdt_p)
            ys.append(jnp.where(lane < HEAD_DIM, y0, y1) + y_off[:, pr * 128:(pr + 1) * 128])
    y = jnp.concatenate(ys, axis=1) + d_x * xs
    return y, jnp.concatenate(s_new, axis=0)


def _ssd_chunk_bwd(xs, bm, cm, dt, a_row, a_col, s_prev, ds_new, dy, d_x, e_mat, et_mat, et_f32, triu):
    lane = lax.broadcasted_iota(jnp.int32, (CHUNK, 128), 1)
    sub = lax.broadcasted_iota(jnp.int32, (CHUNK, 128), 0)
    rowi = lax.broadcasted_iota(jnp.int32, (CHUNK, CHUNK), 0)
    coli = lax.broadcasted_iota(jnp.int32, (CHUNK, CHUNK), 1)
    causal = rowi >= coli
    a_row_t = a_col.T
    a_last = a_col[CHUNK - 1:CHUNK, :]
    a_split, dt_split = _split3(a_col), _split3(dt)
    sub8 = lax.broadcasted_iota(jnp.int32, (8, GROUP_CH), 0)

    dxs, dbs, dcs, dsp = [], [], [], []
    zcol = jnp.zeros((CHUNK, 128), F32)
    zrows = []
    da_col = jnp.zeros((CHUNK, 128), F32)
    ddt = jnp.zeros((CHUNK, 128), F32)
    head_sums = jnp.zeros((8, 128), F32)
    q_row = jnp.zeros((1, 128), F32)
    for g in range(N_GROUPS):
        gs = slice(g * GROUP_CH, (g + 1) * GROUP_CH)
        e_g, et_g = e_mat[:, gs], et_mat[gs, :]
        xs_g, dy_g = xs[:, gs], dy[:, gs]
        ax = _exact_r(a_split, e_g)
        dtx = _exact_r(dt_split, e_g)
        xdt = xs_g * dtx
        dte = jnp.exp(ax[CHUNK - 1:CHUNK, :] - ax)
        w_end = xdt * dte
        cd_col = jnp.exp(jnp.sum(et_f32[gs, :] * a_last, axis=1, keepdims=True))
        dye = dy_g * jnp.exp(ax)
        bg = bm[:, g * D_STATE:(g + 1) * D_STATE]
        cg = cm[:, g * D_STATE:(g + 1) * D_STATE]
        sg = s_prev[gs, :]
        dsg = ds_new[gs, :]
        cb = _dot_nt(cg, bg)
        cs = _dot_nt(cg, sg)
        dcg = _dot(dye, sg)
        dsp.append(dsg * cd_col + _dot_tn(dye, cg))
        dwg = _dot_nt(bg, dsg)
        dbg = _dot(w_end, dsg)
        ww = dwg * w_end
        t1 = jnp.sum(dsg * sg, axis=1, keepdims=True) * cd_col
        dcb = jnp.zeros((CHUNK, CHUNK), F32)
        pairs = []
        for pr in range(3):
            ps = slice(pr * 128, (pr + 1) * 128)
            xdt_p, dy_p = xdt[:, ps], dy_g[:, ps]
            acc = None
            for half in range(2):
                h = g * 6 + pr * 2 + half
                ld = _decay(a_col, a_row_t, h, causal)
                gm = cb * ld
                dym = jnp.where((lane < HEAD_DIM) if half == 0 else (lane >= HEAD_DIM), dy_p, 0.0)
                dg = _dot_nt(dym, xdt_p)
                dseg = dg * gm
                dcb = dcb + dg * ld
                t = _dot_tn(gm, dym)
                acc = t if acc is None else acc + t
                zcol = jnp.where(lane == h, jnp.sum(dseg, axis=1, keepdims=True), zcol)
                zrows.append(jnp.sum(dseg, axis=0, keepdims=True))
            pairs.append(acc)
        dxdt = dwg * dte + jnp.concatenate(pairs, axis=1)
        dcs.append(dcg + _dot(dcb, bg))
        dbs.append(dbg + _dot_tn(dcb, cg))
        dxs.append(dxdt * dtx + d_x[:, gs] * dy_g)
        da_col = da_col + _contract(dye * cs - ww, et_g)
        ddt = ddt + _contract(dxdt * xs_g, et_g)
        col_sums = jnp.where(sub8 == 0, jnp.sum(dy_g * xs_g, axis=0, keepdims=True),
                             jnp.where(sub8 == 1, jnp.sum(ww, axis=0, keepdims=True), 0.0))
        head_sums = head_sums + _exact_r(col_sums, et_g)
        q_row = q_row + jnp.sum(et_f32[gs, :] * t1, axis=0, keepdims=True)

    dd = head_sums[0:1, :]
    q_row = q_row + head_sums[1:2, :]
    zrow = jnp.concatenate(zrows + [jnp.zeros((128 - N_HEADS, CHUNK), F32)], axis=0)
    da_col = da_col + zcol - zrow.T + jnp.where(sub == CHUNK - 1, q_row, 0.0)
    rc = _exact_l(triu, da_col)
    ddt = ddt + rc * a_row
    da = jnp.sum(rc * dt, axis=0, keepdims=True)
    return (jnp.concatenate(dxs, axis=1), jnp.concatenate(dbs, axis=1), jnp.concatenate(dcs, axis=1), ddt, da, dd,
            jnp.concatenate(dsp, axis=0))


_SSD_CONSTANT_SHAPES = dict(e=((128, D_SSM), MXU_DTYPE), et=((D_SSM, 128), MXU_DTYPE), et_f32=((D_SSM, 128), F32),
                            tril=((CHUNK, CHUNK), MXU_DTYPE), triu=((CHUNK, CHUNK), MXU_DTYPE))


def _ssd_constant_scratch(names):
    return [pltpu.VMEM(*_SSD_CONSTANT_SHAPES[n]) for n in names]


def _fill_ssd_constants(**refs):
    iota = lambda shape, d: lax.broadcasted_iota(jnp.int32, shape, d)
    shift = HEAD_DIM.bit_length() - 1
    marks = dict(
        e=lambda: iota((128, D_SSM), 0) == (iota((128, D_SSM), 1) >> shift),
        et=lambda: iota((D_SSM, 128), 1) == (iota((D_SSM, 128), 0) >> shift),
        et_f32=lambda: iota((D_SSM, 128), 1) == (iota((D_SSM, 128), 0) >> shift),
        tril=lambda: iota((CHUNK, CHUNK), 1) <= iota((CHUNK, CHUNK), 0),
        triu=lambda: iota((CHUNK, CHUNK), 1) >= iota((CHUNK, CHUNK), 0))
    for name, ref in refs.items():
        ref[...] = jnp.where(marks[name](), 1.0, 0.0).astype(ref.dtype)


def _row_views(ref, widths):
    views, off = [], 0
    for w in widths:
        views.append(ref.at[:, off:off + w])
        off += w
    return views


def _full(shape):
    nd = len(shape)
    return pl.BlockSpec(shape, lambda *_: (0,) * nd)


def _params(*sem):
    return pltpu.CompilerParams(dimension_semantics=sem, vmem_limit_bytes=VMEM_LIMIT)


def _token_tiles(width, n_tok_tiles):
    return pl.BlockSpec((ROW_TILE, width), lambda i: (jnp.minimum(i, n_tok_tiles - 1), 0))


def _in_proj(x, lead, w1, win):
    nt = x.shape[0] // ROW_TILE
    m = x.shape[0] + ROW_TILE
    tm = ROW_TILE

    def body(x_ref, lead_ref, w1_ref, win_hbm, hn_ref, proj_ref, win_v, sem):
        i = pl.program_id(0)

        @pl.when(i == 0)
        def _():
            cp = pltpu.make_async_copy(win_hbm, win_v, sem)
            cp.start()
            cp.wait()

        x = jnp.where(i == nt, lead_ref[...], x_ref[...])
        r = lax.rsqrt(jnp.mean(x * x, axis=-1, keepdims=True) + EPS)
        hn = _mx(x * r * w1_ref[...])
        hn_ref[...] = hn
        for j in range(0, PROJ_W, 512):
            w = min(512, PROJ_W - j)
            proj_ref[:, j:j + w] = jnp.dot(hn, win_v[:, j:j + w], preferred_element_type=F32)

    return pl.pallas_call(
        body, grid=(m // tm,), name="in_proj",
        in_specs=[_token_tiles(D_MODEL, nt), _full((ROW_TILE, D_MODEL)), _full((1, D_MODEL)),
                  pl.BlockSpec(memory_space=pl.ANY)],
        out_specs=[pl.BlockSpec((tm, D_MODEL), lambda i: (i, 0)), pl.BlockSpec((tm, PROJ_W), lambda i: (i, 0))],
        out_shape=[jax.ShapeDtypeStruct((m, D_MODEL), MXU_DTYPE), jax.ShapeDtypeStruct((m, PROJ_W), F32)],
        scratch_shapes=[pltpu.VMEM((D_MODEL, PROJ_W), MXU_DTYPE), pltpu.SemaphoreType.DMA],
        compiler_params=_params("arbitrary"),
    )(x, lead, w1, win)


def _ffn_fwd_bwd(x, lead, y, tgt, wout, w2n, wff1, wff2, wfn):
    nt = x.shape[0] // ROW_TILE
    m = x.shape[0] + ROW_TILE
    tm = ROW_TILE
    nj = D_FF // 1024

    def body(x_ref, lead_ref, y_ref, tgt_ref, w2n_ref, wfn_ref, wout_hbm, wff1_hbm, wff2_hbm,
             loss_ref, gwf_ref, gw2_ref, ff_ref, da_ref, hn2_ref, dh1_ref, dh2_ref, dy_ref,
             wout_v, wff1_v, wff2_v, a_s, sems):
        i = pl.program_id(0)
        hp = jnp.where(i == nt, lead_ref[...], x_ref[...])

        @pl.when(i == 0)
        def _():
            cps = [pltpu.make_async_copy(s, d, sems.at[k])
                   for k, (s, d) in enumerate(((wout_hbm, wout_v), (wff1_hbm, wff1_v), (wff2_hbm, wff2_v)))]
            for cp in cps:
                cp.start()
            for cp in cps:
                cp.wait()
            loss_ref[...] = jnp.zeros_like(loss_ref)
            gwf_ref[...] = jnp.zeros_like(gwf_ref)
            gw2_ref[...] = jnp.zeros_like(gw2_ref)

        h1 = hp + jnp.dot(y_ref[...], wout_v[...], preferred_element_type=F32)
        r2 = lax.rsqrt(jnp.mean(h1 * h1, axis=-1, keepdims=True) + EPS)
        n2 = h1 * r2
        w2n_row = w2n_ref[...]
        hn2 = _mx(n2 * w2n_row)
        hn2_ref[...] = hn2
        h2 = h1
        for j in range(nj):
            js = slice(j * 1024, (j + 1) * 1024)
            a = jnp.dot(hn2, wff1_v[:, js], preferred_element_type=F32)
            a_s[:, js] = a
            ra = jnp.maximum(a, 0.0)
            ff = _mx(ra * ra)
            ff_ref[:, js] = ff
            h2 = h2 + jnp.dot(ff, wff2_v[js, :], preferred_element_type=F32)

        r3 = lax.rsqrt(jnp.mean(h2 * h2, axis=-1, keepdims=True) + EPS)
        n3 = h2 * r3
        wf_row = wfn_ref[...]
        err = n3 * wf_row - tgt_ref[...]
        tokf = (i < nt).astype(F32)
        loss_ref[...] += 0.5 * jnp.sum(jnp.mean(err * err, axis=-1, keepdims=True) * tokf)
        dout = err * (tokf / D_MODEL)
        gwf_ref[...] += jnp.sum(dout * n3, axis=0, keepdims=True)
        dn3 = dout * wf_row
        dh2 = r3 * (dn3 - n3 * jnp.mean(dn3 * n3, axis=-1, keepdims=True))
        dh2m = _mx(dh2)
        dh2_ref[...] = dh2m

        dhn2 = jnp.zeros((tm, D_MODEL), F32)
        for j in range(nj):
            js = slice(j * 1024, (j + 1) * 1024)
            dff = lax.dot_general(dh2m, wff2_v[js, :], (((1,), (1,)), ((), ())), preferred_element_type=F32)
            da = _mx(dff * (2.0 * jnp.maximum(a_s[:, js], 0.0)))
            da_ref[:, js] = da
            dhn2 = dhn2 + lax.dot_general(da, wff1_v[:, js], (((1,), (1,)), ((), ())), preferred_element_type=F32)
        gw2_ref[...] += jnp.sum(dhn2 * n2, axis=0, keepdims=True)
        dn2 = dhn2 * w2n_row
        dh1 = dh2 + r2 * (dn2 - n2 * jnp.mean(dn2 * n2, axis=-1, keepdims=True))
        dh1_ref[...] = dh1
        dy_ref[...] = lax.dot_general(_mx(dh1), wout_v[...], (((1,), (1,)), ((), ())), preferred_element_type=F32)

    rows = lambda w: pl.BlockSpec((tm, w), lambda i: (i, 0))
    hbm = pl.BlockSpec(memory_space=pl.ANY)
    return pl.pallas_call(
        body, grid=(m // tm,), name="ffn_fwd_bwd",
        in_specs=[_token_tiles(D_MODEL, nt), _full((ROW_TILE, D_MODEL)), rows(D_MIX), _token_tiles(D_MODEL, nt),
                  _full((1, D_MODEL)), _full((1, D_MODEL)), hbm, hbm, hbm],
        out_specs=[_full((1, 128)), _full((1, D_MODEL)), _full((1, D_MODEL)), rows(D_FF), rows(D_FF), rows(D_MODEL),
                   rows(D_MODEL), rows(D_MODEL), rows(D_MIX)],
        out_shape=[jax.ShapeDtypeStruct((1, 128), F32), jax.ShapeDtypeStruct((1, D_MODEL), F32),
                   jax.ShapeDtypeStruct((1, D_MODEL), F32), jax.ShapeDtypeStruct((m, D_FF), MXU_DTYPE),
                   jax.ShapeDtypeStruct((m, D_FF), MXU_DTYPE), jax.ShapeDtypeStruct((m, D_MODEL), MXU_DTYPE),
                   jax.ShapeDtypeStruct((m, D_MODEL), F32), jax.ShapeDtypeStruct((m, D_MODEL), MXU_DTYPE),
                   jax.ShapeDtypeStruct((m, D_MIX), F32)],
        scratch_shapes=[pltpu.VMEM((D_MIX, D_MODEL), MXU_DTYPE), pltpu.VMEM((D_MODEL, D_FF), MXU_DTYPE),
                        pltpu.VMEM((D_FF, D_MODEL), MXU_DTYPE), pltpu.VMEM((tm, D_FF), F32),
                        pltpu.SemaphoreType.DMA((3,))],
        compiler_params=_params("arbitrary"),
    )(x, lead, y, tgt, w2n, wfn, wout, wff1, wff2)


def _in_proj_bwd(dproj, x, lead, dh1, w1, win, after):
    nt = x.shape[0] // ROW_TILE
    m = x.shape[0] + ROW_TILE
    tm = ROW_TILE

    def body(dp_ref, x_ref, lead_ref, dh1_ref, w1_ref, win_hbm, after_ref, gx_ref, gw1_ref, gmeta_ref, win_v, sem):
        i = pl.program_id(0)

        @pl.when(i == 0)
        def _():
            cp = pltpu.make_async_copy(win_hbm, win_v, sem)
            cp.start()
            cp.wait()
            gw1_ref[...] = jnp.zeros_like(gw1_ref)
            gmeta_ref[...] = jnp.zeros_like(gmeta_ref)

        dhn = lax.dot_general(dp_ref[...], win_v[...], (((1,), (1,)), ((), ())), preferred_element_type=F32)
        x = jnp.where(i == nt, lead_ref[...], x_ref[...])
        r = lax.rsqrt(jnp.mean(x * x, axis=-1, keepdims=True) + EPS)
        n = x * r
        gw1_ref[...] += jnp.sum(dhn * n, axis=0, keepdims=True)
        dn = dhn * w1_ref[...]
        dh0 = dh1_ref[...] + r * (dn - n * jnp.mean(dn * n, axis=-1, keepdims=True))

        @pl.when(i < nt)
        def _():
            gx_ref[...] = dh0

        @pl.when(i == nt)
        def _():
            gmeta_ref[...] = dh0[PAD_ROWS:LEAD, :] + dh0[LEAD + PAD_ROWS:2 * LEAD, :]

    rows = lambda w: pl.BlockSpec((tm, w), lambda i: (i, 0))
    hbm = pl.BlockSpec(memory_space=pl.ANY)
    return pl.pallas_call(
        body, grid=(m // tm,), name="in_proj_bwd",
        in_specs=[rows(PROJ_W), _token_tiles(D_MODEL, nt), _full((ROW_TILE, D_MODEL)), rows(D_MODEL),
                  _full((1, D_MODEL)), hbm, hbm],
        out_specs=[_token_tiles(D_MODEL, nt), _full((1, D_MODEL)), _full((N_META, D_MODEL))],
        out_shape=[jax.ShapeDtypeStruct(x.shape, F32), jax.ShapeDtypeStruct((1, D_MODEL), F32),
                   jax.ShapeDtypeStruct((N_META, D_MODEL), F32)],
        scratch_shapes=[pltpu.VMEM((D_MODEL, PROJ_W), MXU_DTYPE), pltpu.SemaphoreType.DMA],
        compiler_params=_params("arbitrary"),
    )(dproj, x, lead, dh1, w1, win, after)


MXU_DEPTH = 256


def _row_slab(m, cap):
    return max(k for k in range(MXU_DEPTH, cap + 1, MXU_DEPTH) if m % k == 0)


def _tn_matmul(a, b, name, tka, max_slab=768, tn=512):
    m, ka = a.shape
    nb = b.shape[1]
    tkm = _row_slab(m, max_slab)
    n_steps = m // tkm

    def body(a_ref, b_ref, o_ref, omx_ref):
        k = pl.program_id(1)

        @pl.when(k == 0)
        def _():
            o_ref[...] = jnp.zeros_like(o_ref)

        at = _mx(a_ref[...])
        for j in range(0, nb, tn):
            w = min(tn, nb - j)
            o_ref[:, j:j + w] += lax.dot_general(at, _mx(b_ref[:, j:j + w]), (((0,), (0,)), ((), ())),
                                                 preferred_element_type=F32)

        @pl.when(k == n_steps - 1)
        def _():
            omx_ref[...] = _mx(o_ref[...])

    out = pl.BlockSpec((tka, nb), lambda i, k: (i, 0))
    return pl.pallas_call(
        body, grid=(ka // tka, n_steps), name=name,
        in_specs=[pl.BlockSpec((tkm, tka), lambda i, k: (k, i)), pl.BlockSpec((tkm, nb), lambda i, k: (k, 0))],
        out_specs=[out, out],
        out_shape=[jax.ShapeDtypeStruct((ka, nb), F32), jax.ShapeDtypeStruct((ka, nb), MXU_DTYPE)],
        compiler_params=_params("arbitrary", "arbitrary"),
    )(a, b)


def _tn_matmul_banded(a, b, core, name, band, tka, tn=512):
    m, ka = a.shape
    nb = b.shape[1]
    tkm = _row_slab(m, 768)
    n_steps = m // tkm

    def body(core_ref, a_ref, b_ref, own_ref, sib_ref, acc):
        k = pl.program_id(1)

        @pl.when(k == 0)
        def _():
            acc[...] = jnp.zeros_like(acc)

        at = _mx(a_ref[...])
        for j in range(0, nb, tn):
            w = min(tn, nb - j)
            acc[:, j:j + w] += lax.dot_general(at, _mx(b_ref[:, j:j + w]), (((0,), (0,)), ((), ())),
                                               preferred_element_type=F32)

        for c in range(2):
            @pl.when((k == n_steps - 1) & (core_ref[0] == c))
            def _():
                for chip in range(4):
                    mine, other = 2 * chip + c, 2 * chip + 1 - c
                    own_ref[chip] = acc[:, mine * band:(mine + 1) * band]
                    sib_ref[chip] = _mx(acc[:, other * band:(other + 1) * band])

    out = pl.BlockSpec((4, tka, band), lambda i, k, c: (0, i, 0))
    grid_spec = pltpu.PrefetchScalarGridSpec(
        num_scalar_prefetch=1, grid=(ka // tka, n_steps),
        in_specs=[pl.BlockSpec((tkm, tka), lambda i, k, c: (k, i)), pl.BlockSpec((tkm, nb), lambda i, k, c: (k, 0))],
        out_specs=[out, out], scratch_shapes=[pltpu.VMEM((tka, nb), F32)])
    return pl.pallas_call(
        body, grid_spec=grid_spec, name=name,
        out_shape=[jax.ShapeDtypeStruct((4, ka, band), F32), jax.ShapeDtypeStruct((4, ka, band), MXU_DTYPE)],
        compiler_params=_params("arbitrary", "arbitrary"),
    )(core, a, b)


def _assemble_bands(g, width):
    n, rows, band = g.shape
    tr = 256

    def body(g_ref, o_ref):
        parts = [g_ref[j] for j in range(n)] + [jnp.zeros((tr, width - n * band), g.dtype)]
        o_ref[...] = jnp.concatenate(parts, axis=1)

    return pl.pallas_call(
        body, grid=(rows // tr,), name="assemble_w_in",
        in_specs=[pl.BlockSpec((n, tr, band), lambda i: (0, i, 0))],
        out_specs=pl.BlockSpec((tr, width), lambda i: (i, 0)),
        out_shape=jax.ShapeDtypeStruct((rows, width), g.dtype),
        compiler_params=_params("arbitrary"),
    )(g)


def _chunk_block(b, c, nb, nc):
    return jnp.where(c == 0, nb * (nc - 1) + b, b * (nc - 1) + c - 1)

def _mixer_fwd(proj, cw, cb, dt_bias, a_log, d_x, nw, pool_w, pool_scale, nb, shards, by_cols):
    m = proj.shape[0]
    nc = m // nb // CHUNK
    ns = len(shards)
    row_widths = [D_XBC, 128, 128, D_SSM, D_SSM, D_POOL]
    rows = jnp.concatenate([cb, dt_bias, a_log, d_x, nw, pool_scale], axis=1)
    pairs = (nc - 1) // 2
    n_steps = nb * pairs + 1

    def block_of(step):
        return jnp.where(step == 0, nb * pairs, step - 1)

    def chunks_of(step):
        lead = step == 0
        b = jnp.maximum(step - 1, 0) // pairs
        j = jnp.maximum(step - 1, 0) - b * pairs
        return [(jnp.where(lead, 0, b), jnp.where(lead, 0, 2 * j + 1), 0),
                (jnp.where(lead, 1, b), jnp.where(lead, 0, 2 * j + 2), CHUNK)]

    def body(p_ref, cw_ref, rows_ref, pw_ref, *rest):
        shard_refs, (y_ref, ypre_ref, pre_ref, st_ref) = rest[:ns], rest[ns:ns + 4]
        gathered_refs, carries, constants_refs = rest[ns + 4:2 * ns + 4], rest[2 * ns + 4:2 * ns + 7], rest[2 * ns + 7:2 * ns + 10]
        gather = _Gather(shard_refs, gathered_refs, by_cols, *rest[2 * ns + 10:])
        step = pl.program_id(0)

        @pl.when(step == 0)
        def _():
            gather.start()
            _fill_ssd_constants(**dict(zip(["e", "et_f32", "tril"], constants_refs)))

        @pl.when(step == n_steps // 2)
        def _():
            gather.forward()

        shared = (cw_ref, pw_ref, _row_views(rows_ref, row_widths), constants_refs)
        for b, c, row0 in chunks_of(step):
            at = lambda ref: ref.at[row0:row0 + CHUNK]
            one_chunk(c, at(p_ref), at(y_ref), at(ypre_ref), at(pre_ref), st_ref.at[row0 // CHUNK], *shared,
                      [carry.at[b] for carry in carries])

        @pl.when(step == n_steps - 1)
        def _():
            gather.finish()

    def one_chunk(c, p_ref, y_ref, ypre_ref, pre_ref, st_ref, cw_ref, pw_ref, row_refs, constants_refs, carries):
        cb_ref, dtb_ref, alog_ref, dx_ref, nw_ref, ps_ref = row_refs
        e_ref, et_ref, tril_ref = constants_refs
        xtail, utail, state = carries

        @pl.when(c == 0)
        def _():
            xtail[...] = jnp.zeros_like(xtail)
            utail[...] = jnp.zeros_like(utail)
            state[...] = jnp.zeros_like(state)

        valid = (c > 0) | (lax.broadcasted_iota(jnp.int32, (CHUNK, 1), 0) >= PAD_ROWS)

        u = p_ref[:, 0:D_POOL]
        inv_cnt, lane = _pool_inv_count(c)
        win = _pool_window_sums(jnp.concatenate([utail[...], u], axis=0), lane)
        utail[...] = u[CHUNK - HALO:, :]
        pooled = win * inv_cnt - u
        mixed = jnp.concatenate(
            [_dot(pooled[:, g * 128:(g + 1) * 128], pw_ref[g]) for g in range(len(POOL_WINDOWS))], axis=1)
        y_ref[:, 0:D_POOL] = _mx(mixed * ps_ref[...])

        xbc = p_ref[:, OFF_X:OFF_X + D_XBC]
        pre = _conv_pre(jnp.concatenate([xtail[...], xbc], axis=0), xbc, cw_ref[...], cb_ref[...])
        xtail[...] = xbc[CHUNK - HALO:, :]
        pre_ref[...] = pre
        xc = pre * _sigmoid(pre)
        dt, _, a_col, _, _ = _dt_and_cumsum(p_ref[:, OFF_DT:OFF_DT + 128], dtb_ref[...], alog_ref[...], valid,
                                            tril_ref[...])
        s_prev = state[...]
        st_ref[...] = s_prev
        yp, s_new = _ssd_chunk_fwd(xc[:, 0:D_SSM], xc[:, D_SSM:D_SSM + 512], xc[:, D_SSM + 512:], dt, a_col, s_prev,
                                   dx_ref[...], e_ref[...], et_ref[...])
        state[...] = s_new
        ypre_ref[...] = yp
        z = p_ref[:, OFF_Z:OFF_Z + D_SSM]
        yz = yp * (z * _sigmoid(z))
        outs = []
        for g in range(N_GROUPS):
            gs = slice(g * GROUP_CH, (g + 1) * GROUP_CH)
            r = lax.rsqrt(jnp.mean(yz[:, gs] * yz[:, gs], axis=-1, keepdims=True) + EPS)
            outs.append(yz[:, gs] * r)
        y_ref[:, D_POOL:] = _mx(jnp.concatenate(outs, axis=1) * nw_ref[...])

    blk = lambda w: pl.BlockSpec((2 * CHUNK, w), lambda s: (block_of(s), 0))
    hbm = pl.BlockSpec(memory_space=pl.ANY)
    outs = pl.pallas_call(
        body, grid=(n_steps,), name="mixer_fwd",
        in_specs=[blk(PROJ_W), _full((4, D_XBC)), _full((1, sum(row_widths))), _full((4, 128, 128))] + [hbm] * ns,
        out_specs=[blk(D_MIX), blk(D_SSM), blk(D_XBC),
                   pl.BlockSpec((2, D_SSM, D_STATE), lambda s: (block_of(s), 0, 0))] + [hbm] * ns,
        out_shape=[jax.ShapeDtypeStruct((m, D_MIX), MXU_DTYPE), jax.ShapeDtypeStruct((m, D_SSM), F32),
                   jax.ShapeDtypeStruct((m, D_XBC), F32), jax.ShapeDtypeStruct((m // CHUNK, D_SSM, D_STATE), F32)]
        + _Gather.out_shapes(shards, by_cols),
        scratch_shapes=[pltpu.VMEM((nb, HALO, D_XBC), F32), pltpu.VMEM((nb, HALO, D_POOL), F32),
                        pltpu.VMEM((nb, D_SSM, D_STATE), F32)] + _ssd_constant_scratch(["e", "et_f32", "tril"])
        + _Gather.scratch(ns),
        compiler_params=_params("arbitrary"),
    )(proj, cw, rows, pool_w, *shards)
    return outs[0], outs[1], outs[2], outs[3], outs[4:]


def _mixer_bwd(proj, dy, ypre, conv_pre, states, cw, dt_bias, a_log, d_x, nw, pool_w, pool_scale, nb, parts, shares):
    m = proj.shape[0]
    nc = m // nb // CHUNK
    hb = CHUNK // HALO
    ns = len(parts)
    row_widths = [128, 128, D_SSM, D_SSM, D_POOL]
    rows = jnp.concatenate([dt_bias, a_log, d_x, nw, pool_scale], axis=1)
    grad_row_widths = [D_XBC, 128, 128, 128, D_SSM, D_POOL]
    constants = ["e", "et", "et_f32", "tril", "triu"]

    pairs = (nc - 1) // 2
    n_steps = nb * pairs + 1

    def block_of(step):
        b = jnp.minimum(step // pairs, nb - 1)
        j = pairs - 1 - (step - b * pairs)
        return jnp.where(step == n_steps - 1, nb * pairs, b * pairs + j)

    def chunks_of(step):
        lead = step == n_steps - 1
        b = jnp.minimum(step // pairs, nb - 1)
        j = pairs - 1 - (step - b * pairs)
        return [(jnp.where(lead, 0, b), jnp.where(lead, 0, 2 * j + 2), jnp.where(lead, 0, CHUNK)),
                (jnp.where(lead, 1, b), jnp.where(lead, 0, 2 * j + 1), jnp.where(lead, CHUNK, 0))]

    def body(p_ref, halo_ref, dy_ref, ypre_ref, pre_ref, st_ref, cw_ref, rows_ref, pw_ref, *rest):
        own_refs, share_refs, rest = rest[:ns], rest[ns:2 * ns], rest[2 * ns:]
        dp_ref, gcw_ref, grows_ref, gpw_ref = rest[:4]
        got_refs, sum_refs, part_refs = rest[4:ns + 4], rest[ns + 4:2 * ns + 4], rest[2 * ns + 4:3 * ns + 4]
        carries, constants_refs, rest = rest[3 * ns + 4:3 * ns + 7], rest[3 * ns + 7:3 * ns + 12], rest[3 * ns + 12:]
        chip_sums = _ChipSums(own_refs, share_refs, got_refs, sum_refs, *rest[:7])
        exchange = _ChipExchange(sum_refs, part_refs, [False] * ns, *rest[7:])
        step = pl.program_id(0)
        chip_sums.at_step(step)

        @pl.when(step == chip_sums.last_step)
        def _():
            exchange.start()

        @pl.when(step == 0)
        def _():
            _fill_ssd_constants(**dict(zip(constants, constants_refs)))
            for r in (gcw_ref, grows_ref, gpw_ref):
                r[...] = jnp.zeros_like(r)

        shared = (cw_ref, pw_ref, _row_views(rows_ref, row_widths), gcw_ref, gpw_ref,
                  _row_views(grows_ref, grad_row_widths), constants_refs)
        halos = [p_ref[CHUNK - HALO:CHUNK, 0:D_POOL], halo_ref[...]]
        for (b, c, row0), halo in zip(chunks_of(step), halos):
            rows_here = pl.ds(pl.multiple_of(row0, CHUNK), CHUNK)
            at = lambda ref: ref.at[rows_here]
            one_chunk(b, c, at(p_ref), halo, at(dy_ref), at(ypre_ref), at(pre_ref), st_ref.at[row0 // CHUNK], at(dp_ref),
                      *shared, [carry.at[b] for carry in carries])

        @pl.when(step == n_steps - 1)
        def _():
            exchange.finish()

    def one_chunk(b, c, p_ref, halo, dy_ref, ypre_ref, pre_ref, st_ref, dp_ref, cw_ref, pw_ref, row_refs, gcw_ref,
                  gpw_ref, grad_row_refs, constants_refs, carries):
        dtb_ref, alog_ref, dx_ref, nw_ref, ps_ref = row_refs
        gcb_ref, gdtb_ref, galog_ref, gd_ref, gnw_ref, gps_ref = grad_row_refs
        ds_carry, dpre_next, dq_next = carries
        e_ref, et_ref, etf_ref, tril_ref, triu_ref = constants_refs

        @pl.when(c == nc - 1)
        def _():
            ds_carry[...] = jnp.zeros_like(ds_carry)
            dpre_next[...] = jnp.zeros_like(dpre_next)
            dq_next[...] = jnp.zeros_like(dq_next)

        valid = (c > 0) | (lax.broadcasted_iota(jnp.int32, (CHUNK, 1), 0) >= PAD_ROWS)
        first = c > 0

        u = p_ref[:, 0:D_POOL]
        u_halo = jnp.where(first, halo, 0.0)
        inv_cnt, lane = _pool_inv_count(c)
        pooled = _pool_window_sums(jnp.concatenate([u_halo, u], axis=0), lane) * inv_cnt - u
        dyp = dy_ref[:, 0:D_POOL]
        ps = ps_ref[...]
        dmixed = dyp * ps
        mixed, dpooled = [], []
        for g in range(len(POOL_WINDOWS)):
            gsl = slice(g * 128, (g + 1) * 128)
            pw = pw_ref[g]
            mixed.append(_dot(pooled[:, gsl], pw))
            dpooled.append(_dot_nt(dmixed[:, gsl], pw))
            gpw_ref[g] += _dot_tn(pooled[:, gsl], dmixed[:, gsl])
        gps_ref[...] += jnp.sum(dyp * jnp.concatenate(mixed, axis=1), axis=0, keepdims=True)
        dpooled = jnp.concatenate(dpooled, axis=1)
        dq = dpooled * inv_cnt
        du = _pool_window_sums_ahead(jnp.concatenate([dq, dq_next[...]], axis=0), lane) - dpooled
        dq_next[...] = dq[0:HALO, :]
        dp_ref[:, 0:D_POOL] = _mx(du)

        yp = ypre_ref[...]
        z = p_ref[:, OFF_Z:OFF_Z + D_SSM]
        sz, dsz = _silu_and_grad(z)
        yz = yp * sz
        do = dy_ref[:, D_POOL:]
        nw_row = nw_ref[...]
        dyz = []
        gnw = []
        for g in range(N_GROUPS):
            gs = slice(g * GROUP_CH, (g + 1) * GROUP_CH)
            r = lax.rsqrt(jnp.mean(yz[:, gs] * yz[:, gs], axis=-1, keepdims=True) + EPS)
            n = yz[:, gs] * r
            gnw.append(jnp.sum(do[:, gs] * n, axis=0, keepdims=True))
            dn = do[:, gs] * nw_row[:, gs]
            dyz.append(r * (dn - n * jnp.mean(dn * n, axis=-1, keepdims=True)))
        gnw_ref[...] += jnp.concatenate(gnw, axis=1)
        dyz = jnp.concatenate(dyz, axis=1)
        dp_ref[:, OFF_Z:OFF_Z + D_SSM] = _mx(dyz * yp * dsz)
        dyp_ssm = dyz * sz

        xc, dsilu = _silu_and_grad(pre_ref[...])
        dtr = p_ref[:, OFF_DT:OFF_DT + 128]
        dt, a_row, a_col, dt_pre, head = _dt_and_cumsum(dtr, dtb_ref[...], alog_ref[...], valid, tril_ref[...])
        dxs, dbm, dcm, ddt, da, dd, ds_prev = _ssd_chunk_bwd(
            xc[:, 0:D_SSM], xc[:, D_SSM:D_SSM + 512], xc[:, D_SSM + 512:], dt, a_row, a_col, st_ref[...],
            ds_carry[...], dyp_ssm, dx_ref[...], e_ref[...], et_ref[...], etf_ref[...], triu_ref[...])
        ds_carry[...] = ds_prev
        gd_ref[...] += dd
        galog_ref[...] += da * a_row
        ddtr = jnp.where(valid & head, ddt * _sigmoid(dt_pre), 0.0)
        gdtb_ref[...] += jnp.sum(ddtr, axis=0, keepdims=True)
        dp_ref[:, OFF_DT:OFF_DT + 128] = _mx(ddtr)

        dpre = jnp.concatenate([dxs, dbm, dcm], axis=1) * dsilu
        gcb_ref[...] += jnp.sum(dpre, axis=0, keepdims=True)
        dext = jnp.concatenate([dpre, dpre_next[...]], axis=0)
        dpre_next[...] = dpre[0:HALO, :]
        ups = [_shift_up(dext, 3 - k) for k in range(4)]
        xbc = p_ref[:, OFF_X:OFF_X + D_XBC]
        gcw_ref[...] += jnp.concatenate([jnp.sum(xbc * ups[k], axis=0, keepdims=True) for k in range(4)], axis=0)
        cw = cw_ref[...]
        dp_ref[:, OFF_X:OFF_X + D_XBC] = _mx(cw[3:4, :] * ups[3] + cw[2:3, :] * ups[2]
                                             + cw[1:2, :] * ups[1] + cw[0:1, :] * ups[0])

    blk = lambda w: pl.BlockSpec((2 * CHUNK, w), lambda s: (block_of(s), 0))

    def halo_rows(s):
        b = jnp.minimum(s // pairs, nb - 1)
        below = block_of(s) * 2 * hb - 1
        return jnp.where(block_of(s) == b * pairs, _chunk_block(b, 0, nb, nc) * hb + hb - 1, below)

    hbm = pl.BlockSpec(memory_space=pl.ANY)
    sum_shapes = _ChipSums.out_shapes(parts)
    assert n_steps > 3 + 4 * ns, "the chip sums need their grid steps"
    outs = pl.pallas_call(
        body, grid=(n_steps,), name="mixer_bwd",
        in_specs=[blk(PROJ_W), pl.BlockSpec((HALO, D_POOL), lambda s: (halo_rows(s), 0)), blk(D_MIX), blk(D_SSM),
                  blk(D_XBC), pl.BlockSpec((2, D_SSM, D_STATE), lambda s: (block_of(s), 0, 0)),
                  _full((4, D_XBC)), _full((1, sum(row_widths))), _full((4, 128, 128))] + [hbm] * (2 * ns),
        out_specs=[blk(PROJ_W), _full((4, D_XBC)), _full((1, sum(grad_row_widths))), _full((4, 128, 128))]
        + [hbm] * (3 * ns),
        out_shape=[jax.ShapeDtypeStruct((m, PROJ_W), MXU_DTYPE), jax.ShapeDtypeStruct((4, D_XBC), F32),
                   jax.ShapeDtypeStruct((1, sum(grad_row_widths)), F32), jax.ShapeDtypeStruct((4, 128, 128), F32)]
        + sum_shapes + sum_shapes + _ChipExchange.out_shapes(sum_shapes, [False] * ns),
        scratch_shapes=[pltpu.VMEM((nb, D_SSM, D_STATE), F32), pltpu.VMEM((nb, HALO, D_XBC), F32),
                        pltpu.VMEM((nb, HALO, D_POOL), F32)] + _ssd_constant_scratch(constants)
        + _ChipSums.scratch(parts) + _ChipExchange.scratch(ns),
        compiler_params=_params("arbitrary"),
    )(proj, proj, dy, ypre, conv_pre, states, cw, rows, pool_w, *parts, *shares)
    dproj, g_cw, g_rows, g_pw = outs[:4]
    offs = np.cumsum([0] + grad_row_widths)
    g_cb, g_dtb, g_alog, g_d, g_nw, g_ps = (g_rows[:, a:b] for a, b in zip(offs[:-1], offs[1:]))
    return (dproj, g_cw, g_cb, g_dtb, g_alog, g_d, g_nw, g_pw, g_ps), outs[4 + 2 * ns:]


MESH_IDS = pl.DeviceIdType.MESH
_HBM = pl.BlockSpec(memory_space=pltpu.HBM)


def _coords():
    return lax.axis_index("x"), lax.axis_index("y"), lax.axis_index("c")


def _other_chips(x, y):
    return [(1 - x, y), (x, 1 - y), (1 - x, 1 - y)]


class _Gather:
    def __init__(self, ins, outs, by_cols, send_sems, recv_sems, local_sems):
        self.ins, self.outs, self.by_cols, self.n = ins, outs, by_cols, len(ins)
        self.send_sems, self.recv_sems, self.local_sems = send_sems, recv_sems, local_sems
        self.x, self.y, self.c = _coords()
        self.me, self.sibling = (self.x, self.y, self.c), (self.x, self.y, 1 - self.c)
        self.chips = _other_chips(self.x, self.y)

    @staticmethod
    def scratch(n):
        return [pltpu.SemaphoreType.DMA((7 * n,)), pltpu.SemaphoreType.DMA((7 * n,)), pltpu.SemaphoreType.DMA((n,))]

    @staticmethod
    def out_shapes(shards, by_cols):
        return [jax.ShapeDtypeStruct((s.shape[0], N_DEV * s.shape[1]) if cols else (N_DEV,) + s.shape, s.dtype)
                for s, cols in zip(shards, by_cols)]

    def _block(self, t, device):
        idx = 4 * device[0] + 2 * device[1] + device[2]
        if not self.by_cols[t]:
            return self.outs[t].at[idx]
        w = self.ins[t].shape[1]
        return self.outs[t].at[:, pl.ds(pl.multiple_of(idx * w, w), w)]

    def _copy(self, t, k, block, to, own=False):
        dst = self._block(t, block)
        return pltpu.make_async_remote_copy(
            src_ref=self.ins[t] if own else dst, dst_ref=dst, send_sem=self.send_sems.at[t * 7 + k],
            recv_sem=self.recv_sems.at[t * 7 + k], device_id=to, device_id_type=MESH_IDS)

    def _mine(self):
        return [pltpu.make_async_copy(self.ins[t], self._block(t, self.me), self.local_sems.at[t])
                for t in range(self.n)]

    def _first(self):
        cps = []
        for t in range(self.n):
            cps.append(self._copy(t, 0, self.me, self.sibling, own=True))
            cps += [self._copy(t, 1 + j, self.me, (*chip, self.c), own=True) for j, chip in enumerate(self.chips)]
        return cps

    def _passed(self):
        return [self._copy(t, 4 + j, (*chip, self.c), self.sibling)
                for j, chip in enumerate(self.chips) for t in range(self.n)]

    def start(self):
        for cp in self._mine() + self._first():
            cp.start()

    def forward(self):
        for j, chip in enumerate(self.chips):
            for t in range(self.n):
                self._copy(t, 1 + j, (*chip, self.c), self.me).wait_recv()
                self._copy(t, 4 + j, (*chip, self.c), self.sibling).start()

    def finish(self):
        for t in range(self.n):
            self._copy(t, 0, self.sibling, self.me).wait_recv()
            for j, chip in enumerate(self.chips):
                self._copy(t, 4 + j, (*chip, 1 - self.c), self.me).wait_recv()
        for cp in self._first() + self._passed():
            cp.wait_send()
        for cp in self._mine():
            cp.wait()


def _weight_gather(shards):
    n = len(shards)

    def body(*refs):
        g = _Gather(refs[:n], refs[n:2 * n], [False] * n, *refs[2 * n:])
        g.start()
        g.forward()
        g.finish()

    return pl.pallas_call(
        body, name="weight_gather",
        in_specs=[_HBM] * n, out_specs=[_HBM] * n,
        out_shape=_Gather.out_shapes(shards, [False] * n),
        scratch_shapes=_Gather.scratch(n),
    )(*shards)


def _grad_exchange_d2d(gs, name, swapped=()):
    n, ns = len(gs), len(swapped)

    def body(*refs):
        ins, whole_ins = refs[:n], refs[n:n + ns]
        got, whole_got = refs[n + ns:2 * n + ns], refs[2 * n + ns:2 * (n + ns)]
        send_sems, recv_sems = refs[2 * (n + ns):]
        x, y, c = _coords()
        theirs = lambda t, k: ins[t].at[k] if len(ins[t].shape) == 3 else ins[t].at[k, 1 - c]
        pairs = [(theirs(t, k), got[t].at[k]) for t in range(n) for k in range(4)] + list(zip(whole_ins, whole_got))
        remote = [pltpu.make_async_remote_copy(
            src_ref=src, dst_ref=dst, send_sem=send_sems.at[i], recv_sem=recv_sems.at[i], device_id=(x, y, 1 - c),
            device_id_type=MESH_IDS) for i, (src, dst) in enumerate(pairs)]
        for cp in remote:
            cp.start()
        for cp in remote:
            cp.wait_recv()
        for cp in remote:
            cp.wait_send()

    outs = pl.pallas_call(
        body, name=name,
        in_specs=[_HBM] * (n + ns), out_specs=[_HBM] * (n + ns),
        out_shape=[jax.ShapeDtypeStruct((4,) + g.shape[-2:], g.dtype) for g in gs]
        + [jax.ShapeDtypeStruct(a.shape, a.dtype) for a in swapped],
        scratch_shapes=[pltpu.SemaphoreType.DMA((4 * n + ns,)), pltpu.SemaphoreType.DMA((4 * n + ns,))],
    )(*gs, *swapped)
    return outs[:n], outs[n:]


def _small_allreduce(pack):
    rows = pack.shape[0]

    def body(p_ref, o_ref, sib_ref, parts_ref, send_sems, recv_sems):
        x, y, c = _coords()
        my_chip = 2 * x + y
        swap = pltpu.make_async_remote_copy(src_ref=p_ref, dst_ref=sib_ref, send_sem=send_sems.at[0],
                                            recv_sem=recv_sems.at[0], device_id=(x, y, 1 - c), device_id_type=MESH_IDS)
        swap.start()
        swap.wait_recv()
        parts_ref[my_chip] = p_ref[...] + sib_ref[...]
        remote = [pltpu.make_async_remote_copy(
            src_ref=parts_ref.at[my_chip], dst_ref=parts_ref.at[my_chip], send_sem=send_sems.at[1 + j],
            recv_sem=recv_sems.at[1 + j], device_id=(cx, cy, c), device_id_type=MESH_IDS)
            for j, (cx, cy) in enumerate(_other_chips(x, y))]
        for cp in remote:
            cp.start()
        for j, (cx, cy) in enumerate(_other_chips(x, y)):
            slot = parts_ref.at[2 * cx + cy]
            pltpu.make_async_remote_copy(src_ref=slot, dst_ref=slot, send_sem=send_sems.at[1 + j],
                                         recv_sem=recv_sems.at[1 + j], device_id=(cx, cy, c),
                                         device_id_type=MESH_IDS).wait_recv()
        o_ref[...] = ((parts_ref[0] + parts_ref[1]) + parts_ref[2]) + parts_ref[3]
        swap.wait_send()
        for cp in remote:
            cp.wait_send()

    vmem = pl.BlockSpec(memory_space=pltpu.VMEM)
    return pl.pallas_call(
        body, name="small_allreduce", in_specs=[vmem], out_specs=vmem,
        out_shape=jax.ShapeDtypeStruct((rows, 128), F32),
        scratch_shapes=[pltpu.VMEM((rows, 128), F32), pltpu.VMEM((4, rows, 128), F32),
                        pltpu.SemaphoreType.DMA((4,)), pltpu.SemaphoreType.DMA((4,))],
    )(pack)


class _ChipExchange:
    def __init__(self, ins, outs, whole, send_sems, recv_sems, local_sems):
        self.ins, self.outs, self.whole, self.n = ins, outs, whole, len(ins)
        self.send_sems, self.recv_sems, self.local_sems = send_sems, recv_sems, local_sems
        self.x, self.y, self.c = _coords()
        self.my_chip = 2 * self.x + self.y
        self.chips = _other_chips(self.x, self.y)

    @staticmethod
    def scratch(n):
        return [pltpu.SemaphoreType.DMA((3 * n,)), pltpu.SemaphoreType.DMA((3 * n,)), pltpu.SemaphoreType.DMA((n,))]

    def _src(self, t, k):
        return self.ins[t] if self.whole[t] else self.ins[t].at[k]

    def _local(self):
        return [pltpu.make_async_copy(self._src(t, self.my_chip), self.outs[t].at[self.my_chip], self.local_sems.at[t])
                for t in range(self.n)]

    def _remote(self):
        return [pltpu.make_async_remote_copy(
            src_ref=self._src(t, 2 * cx + cy), dst_ref=self.outs[t].at[self.my_chip],
            send_sem=self.send_sems.at[t * 3 + j], recv_sem=self.recv_sems.at[t * 3 + j],
            device_id=(cx, cy, self.c), device_id_type=MESH_IDS)
            for t in range(self.n) for j, (cx, cy) in enumerate(self.chips)]

    def start(self):
        for cp in self._remote() + self._local():
            cp.start()

    def finish(self):
        for t in range(self.n):
            for j, (cx, cy) in enumerate(self.chips):
                slot = self.outs[t].at[2 * cx + cy]
                pltpu.make_async_remote_copy(
                    src_ref=slot, dst_ref=slot, send_sem=self.send_sems.at[t * 3 + j],
                    recv_sem=self.recv_sems.at[t * 3 + j], device_id=(cx, cy, self.c),
                    device_id_type=MESH_IDS).wait_recv()
        for cp in self._remote():
            cp.wait_send()
        for cp in self._local():
            cp.wait()

    @staticmethod
    def out_shapes(arrs, whole):
        return [jax.ShapeDtypeStruct(((4,) + a.shape) if w else a.shape, a.dtype) for a, w in zip(arrs, whole)]


class _ChipSums:
    def __init__(self, parts, shares, got, sums, a_buf, b_buf, o_buf, send_sems, recv_sems, load_sems, store_sems):
        self.parts, self.shares, self.got, self.sums = parts, shares, got, sums
        self.a_buf, self.b_buf, self.o_buf = a_buf, b_buf, o_buf
        self.send_sems, self.recv_sems, self.load_sems, self.store_sems = send_sems, recv_sems, load_sems, store_sems
        self.x, self.y, self.c = _coords()
        self.n_blocks = 4 * len(parts)
        self.last_step = 3 + self.n_blocks

    @staticmethod
    def scratch(parts):
        r, c, n = max(p.shape[-2] for p in parts), parts[0].shape[-1], 4 * len(parts)
        return [pltpu.VMEM((2, r, c), F32), pltpu.VMEM((2, r, c), MXU_DTYPE), pltpu.VMEM((2, r, c), MXU_DTYPE),
                pltpu.SemaphoreType.DMA((n,)), pltpu.SemaphoreType.DMA((n,)), pltpu.SemaphoreType.DMA((2, 2)),
                pltpu.SemaphoreType.DMA((2,))]

    @staticmethod
    def out_shapes(parts):
        return [jax.ShapeDtypeStruct((4,) + p.shape[-2:], MXU_DTYPE) for p in parts]

    def _swap(self):
        return [pltpu.make_async_remote_copy(
            src_ref=self.shares[t].at[k, 1 - self.c], dst_ref=self.got[t].at[k], send_sem=self.send_sems.at[4 * t + k],
            recv_sem=self.recv_sems.at[4 * t + k], device_id=(self.x, self.y, 1 - self.c), device_id_type=MESH_IDS)
            for t in range(len(self.parts)) for k in range(4)]

    def _slot(self, j, buf):
        t = j // 4
        return buf.at[j % 2, pl.ds(0, self.parts[t].shape[-2])]

    def _loads(self, j):
        t, k = divmod(j, 4)
        return [pltpu.make_async_copy(self.parts[t].at[k, self.c], self._slot(j, self.a_buf), self.load_sems.at[j % 2, 0]),
                pltpu.make_async_copy(self.got[t].at[k], self._slot(j, self.b_buf), self.load_sems.at[j % 2, 1])]

    def _store(self, j):
        t, k = divmod(j, 4)
        return pltpu.make_async_copy(self._slot(j, self.o_buf), self.sums[t].at[k], self.store_sems.at[j % 2])

    def at_step(self, step):
        @pl.when(step == 0)
        def _():
            for cp in self._swap():
                cp.start()

        @pl.when(step == 2)
        def _():
            for cp in self._swap():
                cp.wait_recv()
            for cp in self._swap():
                cp.wait_send()
            for cp in self._loads(0):
                cp.start()

        for j in range(self.n_blocks):
            @pl.when(step == 3 + j)
            def _(j=j):
                for cp in self._loads(j):
                    cp.wait()
                if j + 1 < self.n_blocks:
                    for cp in self._loads(j + 1):
                        cp.start()
                if j >= 2:
                    self._store(j - 2).wait()
                a, b, o = (self._slot(j, buf) for buf in (self.a_buf, self.b_buf, self.o_buf))
                o[...] = (a[...] + b[...].astype(F32)).astype(o.dtype)
                self._store(j).start()

        @pl.when(step == self.last_step)
        def _():
            self._store(self.n_blocks - 2).wait()
            self._store(self.n_blocks - 1).wait()


_SEMAPHORES = pl.BlockSpec(memory_space=pltpu.SEMAPHORE)
_SIDE_EFFECT = pltpu.SideEffectType.DATAFLOW_SIDE_EFFECTING


def _split_exchange_copies(srcs, lands, whole, send_sems, recv_sems, waiting):
    x, y, c = _coords()
    copies = []
    for t in range(len(srcs)):
        for j, (cx, cy) in enumerate(_other_chips(x, y)):
            src = srcs[t] if whole[t] else srcs[t].at[2 * cx + cy]
            dst = lands[t].at[2 * cx + cy] if waiting else lands[t].at[2 * x + y]
            copies.append(pltpu.make_async_remote_copy(
                src_ref=src, dst_ref=dst, send_sem=send_sems.at[3 * t + j], recv_sem=recv_sems.at[3 * t + j],
                device_id=(cx, cy, c), device_id_type=MESH_IDS))
    return copies


def _chip_exchange_start(arrays, whole, name):
    n = len(arrays)

    def body(*refs):
        srcs, lands = refs[:n], refs[n:2 * n]
        send_sems, recv_sems = refs[2 * n:2 * n + 2]
        for cp in _split_exchange_copies(srcs, lands, whole, send_sems, recv_sems, waiting=False):
            cp.start()
        token = refs[-1]
        token[...] = jnp.zeros_like(token)

    land_shapes = [((4,) + a.shape) if w else a.shape for a, w in zip(arrays, whole)]
    hbm = lambda shape, a: pltpu.HBM(shape, a.dtype)
    outs = pl.pallas_call(
        body, name=name,
        out_shape=(pltpu.SemaphoreType.DMA((3 * n,)), pltpu.SemaphoreType.DMA((3 * n,)),
                   *[hbm(a.shape, a) for a in arrays], *[hbm(s, a) for s, a in zip(land_shapes, arrays)],
                   jax.ShapeDtypeStruct((8, 128), F32)),
        in_specs=(_HBM,) * (2 * n),
        out_specs=(_SEMAPHORES, _SEMAPHORES) + (_HBM,) * (2 * n) + (pl.BlockSpec(memory_space=pltpu.VMEM),),
        input_output_aliases={i: 2 + i for i in range(2 * n)},
        compiler_params=pltpu.CompilerParams(has_side_effects=_SIDE_EFFECT),
    )(*[pltpu.with_memory_space_constraint(a, pltpu.HBM) for a in arrays],
      *[pltpu.with_memory_space_constraint(lax.empty(s, a.dtype), pltpu.HBM) for s, a in zip(land_shapes, arrays)])
    return outs[0], outs[1], outs[2:2 + n], outs[2 + n:2 + 2 * n], outs[-1]


def _chip_exchange_wait(send_sems, recv_sems, srcs, lands, whole, after, name):
    n = len(srcs)

    def body(*refs):
        src_refs, land_refs = refs[:n], refs[n:2 * n]
        for cp in _split_exchange_copies(src_refs, land_refs, whole, refs[2 * n], refs[2 * n + 1], waiting=True):
            cp.wait_send()
            cp.wait_recv()

    outs = pl.pallas_call(
        body, name=name, out_shape=tuple(pltpu.HBM(a.shape, a.dtype) for a in (*srcs, *lands)),
        in_specs=(_HBM,) * (2 * n) + (_SEMAPHORES, _SEMAPHORES, pl.BlockSpec(memory_space=pl.ANY)),
        out_specs=(_HBM,) * (2 * n), input_output_aliases={i: i for i in range(2 * n)},
        compiler_params=pltpu.CompilerParams(has_side_effects=_SIDE_EFFECT),
    )(*srcs, *lands, send_sems, recv_sems, after)
    return outs[:n], outs[n:]


def _sum_two(a, b, name):
    def body(a_ref, b_ref, o_ref):
        o_ref[...] = a_ref[...] + b_ref[...]

    return pl.pallas_call(body, name=name, out_shape=jax.ShapeDtypeStruct(a.shape, a.dtype))(a, b)


def _sum_chips(landed, own, chip, name):
    def body(chip_ref, l_ref, own_ref, o_ref):
        part = lambda k: jnp.where(chip_ref[0] == k, own_ref[...], l_ref[k])
        o_ref[...] = ((part(0) + part(1)) + part(2)) + part(3)

    grid_spec = pltpu.PrefetchScalarGridSpec(
        num_scalar_prefetch=1, grid=(1,),
        in_specs=[pl.BlockSpec(landed.shape, lambda i, c: (0, 0, 0)), pl.BlockSpec(own.shape, lambda i, c: (0, 0))],
        out_specs=pl.BlockSpec(own.shape, lambda i, c: (0, 0)))
    return pl.pallas_call(body, grid_spec=grid_spec, name=name,
                          out_shape=jax.ShapeDtypeStruct(own.shape, own.dtype))(chip, landed, own)


def _row_tile(rows, cols, n_arrays):
    budget = 24 * 1024 * 1024
    padded = -(-cols // 128) * 128
    step = 16 if rows % 16 == 0 else 8
    tr = max(step, budget // (n_arrays * 2 * 4 * padded) // step * step)
    while rows % tr:
        tr -= step
    return tr


def _chip_sum(g, got, core, name):
    rows, cols = g.shape[-2:]
    tr = _row_tile(rows, cols, 3)

    def body(c_ref, a_ref, b_ref, o_ref):
        o_ref[...] = (a_ref[...] + b_ref[...].astype(F32)).astype(o_ref.dtype)

    mine = (pl.BlockSpec((None, tr, cols), lambda k, i, c: (k, i, 0)) if g.ndim == 3
            else pl.BlockSpec((None, None, tr, cols), lambda k, i, c: (k, c[0], i, 0)))
    grid_spec = pltpu.PrefetchScalarGridSpec(
        num_scalar_prefetch=1, grid=(4, rows // tr),
        in_specs=[mine, pl.BlockSpec((None, tr, cols), lambda k, i, c: (k, i, 0))],
        out_specs=pl.BlockSpec((None, tr, cols), lambda k, i, c: (k, i, 0)))
    return pl.pallas_call(body, grid_spec=grid_spec, name=name,
                          out_shape=jax.ShapeDtypeStruct((4, rows, cols), MXU_DTYPE),
                          compiler_params=_params("arbitrary", "arbitrary"))(core, g, got)


def _adamw_math(w, g, m, v):
    m2 = ADAM_B1 * m + (1.0 - ADAM_B1) * g
    v2 = ADAM_B2 * v + (1.0 - ADAM_B2) * (g * g)
    m_hat = m2 / (1.0 - ADAM_B1 ** ADAM_STEP)
    v_hat = v2 / (1.0 - ADAM_B2 ** ADAM_STEP)
    delta = -ADAM_LR * (m_hat / (jnp.sqrt(v_hat) + ADAM_EPS) + ADAM_WD * w)
    return delta, m2, v2


def _adamw(parts, w, m, v, name, own=None, chip=None):
    rows, cols = w.shape
    tr = _row_tile(rows, cols, 11 if own is None else 15)

    def body(*refs):
        if own is None:
            p_ref, w_ref, m_ref, v_ref, g_ref, d_ref, m2_ref, v2_ref = refs
            part = lambda k: p_ref[k].astype(F32)
        else:
            chip_ref, p_ref, own_ref, w_ref, m_ref, v_ref, g_ref, d_ref, m2_ref, v2_ref = refs
            part = lambda k: jnp.where(chip_ref[0] == k, own_ref[k], p_ref[k]).astype(F32)
        g = ((part(0) + part(1)) + part(2)) + part(3)
        d, m2, v2 = _adamw_math(w_ref[...], g, m_ref[...], v_ref[...])
        g_ref[...] = g
        d_ref[...] = d
        m2_ref[...] = m2
        v2_ref[...] = v2

    blk = pl.BlockSpec((tr, cols), lambda i, *_: (i, 0))
    pblk = pl.BlockSpec((4, tr, cols), lambda i, *_: (0, i, 0))
    out = jax.ShapeDtypeStruct((rows, cols), F32)
    if own is None:
        return pl.pallas_call(body, grid=(rows // tr,), name=name, in_specs=[pblk, blk, blk, blk],
                              out_specs=[blk] * 4, out_shape=[out] * 4,
                              compiler_params=_params("arbitrary"))(parts, w, m, v)
    grid_spec = pltpu.PrefetchScalarGridSpec(num_scalar_prefetch=1, grid=(rows // tr,),
                                             in_specs=[pblk, pblk, blk, blk, blk], out_specs=[blk] * 4)
    return pl.pallas_call(body, grid_spec=grid_spec, name=name, out_shape=[out] * 4,
                          compiler_params=_params("arbitrary"))(chip, parts, own, w, m, v)


def _adamw_transposed(parts_t, w, m, v, name):
    rows, cols = w.shape
    tc = 256

    def body(p_ref, w_ref, m_ref, v_ref, g_ref, d_ref, m2_ref, v2_ref):
        part = lambda k: p_ref[k].astype(F32)
        g = (((part(0) + part(1)) + part(2)) + part(3)).T
        d, m2, v2 = _adamw_math(w_ref[...], g, m_ref[...], v_ref[...])
        g_ref[...] = g
        d_ref[...] = d
        m2_ref[...] = m2
        v2_ref[...] = v2

    blk = pl.BlockSpec((rows, tc), lambda i: (0, i))
    out = jax.ShapeDtypeStruct((rows, cols), F32)
    return pl.pallas_call(body, grid=(cols // tc,), name=name,
                          in_specs=[pl.BlockSpec((4, tc, rows), lambda i: (0, i, 0)), blk, blk, blk],
                          out_specs=[blk] * 4, out_shape=[out] * 4,
                          compiler_params=_params("arbitrary"))(parts_t, w, m, v)


def _adamw_small(gs, ws, ms, vs):
    n = len(ws)

    def body(*refs):
        g_refs, w_refs, m_refs, v_refs = (refs[k * n:(k + 1) * n] for k in range(4))
        d_refs, m2_refs, v2_refs = (refs[(4 + k) * n:(5 + k) * n] for k in range(3))
        for t in range(n):
            d, m2, v2 = _adamw_math(w_refs[t][...], g_refs[t][...], m_refs[t][...], v_refs[t][...])
            d_refs[t][...] = d
            m2_refs[t][...] = m2
            v2_refs[t][...] = v2

    outs = pl.pallas_call(body, name="adamw_small",
                          out_shape=[jax.ShapeDtypeStruct(w.shape, F32) for w in ws] * 3)(*gs, *ws, *ms, *vs)
    return outs[:n], outs[n:2 * n], outs[2 * n:]


_PACK_TILE = 8 * 128


def _pack(arrays):
    rows = []
    for a in arrays:
        flat = a.astype(F32).reshape(-1)
        rows.append(jnp.pad(flat, (0, -flat.shape[0] % _PACK_TILE)).reshape(-1, 128))
    return jnp.concatenate(rows, axis=0)


def _unpack(pack, shapes):
    out, r = [], 0
    for s in shapes:
        n = int(np.prod(s))
        out.append(pack[r:r + -(-n // 128)].reshape(-1)[:n].reshape(s))
        r += -(-n // _PACK_TILE) * 8
    return out


def _pad128(v):
    v = v.reshape(1, -1).astype(F32)
    return jnp.pad(v, ((0, 0), (0, 128 - v.shape[1])))


_WEIGHTS = ["meta", "norm_mix_w", "w_in", "pool_w", "pool_scale", "conv_w", "conv_b", "dt_bias", "a_log", "d_skip",
            "ssm_norm_w", "w_out", "norm_ffn_w", "w_ff1", "w_ff2", "norm_f_w"]
_BIG = ["w_in", "w_out", "w_ff1", "w_ff2"]
_SMALL = [n for n in _WEIGHTS if n not in _BIG]


def kernel(x, meta, norm_mix_w, w_in, pool_w, pool_scale, conv_w, conv_b, dt_bias, a_log, d_skip, ssm_norm_w, w_out, norm_ffn_w, w_ff1, w_ff2, norm_f_w, loss_target, m_meta, m_norm_mix_w, m_w_in, m_pool_w, m_pool_scale, m_conv_w, m_conv_b, m_dt_bias, m_a_log, m_d_skip, m_ssm_norm_w, m_w_out, m_norm_ffn_w, m_w_ff1, m_w_ff2, m_norm_f_w, v_meta, v_norm_mix_w, v_w_in, v_pool_w, v_pool_scale, v_conv_w, v_conv_b, v_dt_bias, v_a_log, v_d_skip, v_ssm_norm_w, v_w_out, v_norm_ffn_w, v_w_ff1, v_w_ff2, v_norm_f_w):
    wts = dict(meta=meta, norm_mix_w=norm_mix_w, w_in=w_in, pool_w=pool_w, pool_scale=pool_scale, conv_w=conv_w,
               conv_b=conv_b, dt_bias=dt_bias, a_log=a_log, d_skip=d_skip, ssm_norm_w=ssm_norm_w, w_out=w_out,
               norm_ffn_w=norm_ffn_w, w_ff1=w_ff1, w_ff2=w_ff2, norm_f_w=norm_f_w)
    mom1 = dict(zip(_WEIGHTS, (m_meta, m_norm_mix_w, m_w_in, m_pool_w, m_pool_scale, m_conv_w, m_conv_b, m_dt_bias,
                               m_a_log, m_d_skip, m_ssm_norm_w, m_w_out, m_norm_ffn_w, m_w_ff1, m_w_ff2, m_norm_f_w)))
    mom2 = dict(zip(_WEIGHTS, (v_meta, v_norm_mix_w, v_w_in, v_pool_w, v_pool_scale, v_conv_w, v_conv_b, v_dt_bias,
                               v_a_log, v_d_skip, v_ssm_norm_w, v_w_out, v_norm_ffn_w, v_w_ff1, v_w_ff2, v_norm_f_w)))
    xi, yi, ci = _coords()
    dev = 4 * xi + 2 * yi + ci
    win_cols = w_in.shape[-1]
    cw_cols = conv_w.shape[-1]

    nb, seq, _ = x.shape
    core = jnp.reshape(ci, (1,)).astype(jnp.int32)
    owners = lambda a: a.reshape((4, 2) + a.shape[1:])

    lead_pack = jnp.zeros((N_META, 512), F32)
    lead_pack = lead_pack.at[:, :128].set(meta).at[:4, 128:128 + cw_cols].set(conv_w[0])
    g_win, g_lead = _weight_gather([_mx(w_in[0]), lead_pack])
    win_full = _assemble_bands(g_win, PROJ_W)
    meta_full = jnp.transpose(g_lead[:, :, :128], (1, 0, 2)).reshape(N_META, D_MODEL)
    cw_full = jnp.transpose(g_lead[:, :4, 128:128 + cw_cols], (1, 0, 2)).reshape(4, D_XBC)

    lead = jnp.concatenate([jnp.zeros((PAD_ROWS, D_MODEL), F32), meta_full] * nb, axis=0)
    x_rows = x.reshape(nb * seq, D_MODEL)
    tgt_rows = loss_target.reshape(nb * seq, D_MODEL)
    dt_bias_p, a_log_p = _pad128(dt_bias), _pad128(a_log)
    d_x = jnp.repeat(d_skip.reshape(1, N_HEADS).astype(F32), HEAD_DIM, axis=1)
    norm_f_row = norm_f_w.reshape(1, D_MODEL)

    hn1, proj = _in_proj(x_rows, lead, norm_mix_w, win_full)
    late_cols = [False, True, False]
    y, ypre, conv_pre, states, (g_wout, wff1_full, g_wff2) = _mixer_fwd(
        proj, cw_full, conv_b, dt_bias_p, a_log_p, d_x, ssm_norm_w, pool_w[0], pool_scale, nb,
        [_mx(w_out[0]), _mx(w_ff1[0]), _mx(w_ff2[0])], late_cols)
    wout_full = g_wout.reshape(D_MIX, D_MODEL)
    wff2_full = g_wff2.reshape(D_FF, D_MODEL)
    loss, gr_nf, gr_nffn, ff, da, hn2, dh1, dh2, dy = _ffn_fwd_bwd(
        x_rows, lead, y, tgt_rows, wout_full, norm_ffn_w, wff1_full, wff2_full, norm_f_row)
    gr_wff2 = _tn_matmul(ff, dh2, "grad_w_ff2", tka=1024, max_slab=2816)
    gr_wff1_t = _tn_matmul(da, hn2, "grad_w_ff1", tka=1024, max_slab=2816)
    gr_wout = _tn_matmul(y, dh1, "grad_w_out", tka=1024, max_slab=2816)

    by_owner = lambda k: [owners(gr_wout[k].reshape(N_DEV, D_MIX // N_DEV, D_MODEL)),
                          owners(gr_wff1_t[k].reshape(N_DEV, D_FF // N_DEV, D_MODEL)),
                          owners(gr_wff2[k].reshape(N_DEV, D_FF // N_DEV, D_MODEL))]
    (dproj, gr_cw, gr_cb, gr_dtb, gr_alog, gr_d, gr_nw, gr_pw, gr_ps), late_exchanged = _mixer_bwd(
        proj, dy, ypre, conv_pre, states, cw_full, dt_bias_p, a_log_p, d_x, ssm_norm_w, pool_w[0], pool_scale, nb,
        by_owner(0), by_owner(1))

    early = dict(pool_w=gr_pw, pool_scale=gr_ps, conv_w=gr_cw, conv_b=gr_cb, dt_bias=gr_dtb[:, :N_HEADS],
                 a_log=gr_alog[:, :N_HEADS], d_skip=gr_d[:, :N_HEADS], ssm_norm_w=gr_nw, norm_ffn_w=gr_nffn,
                 norm_f_w=gr_nf, loss=loss[0:1, 0:1])
    early_pack = _pack(list(early.values()))
    win_mine, win_theirs = _tn_matmul_banded(hn1, dproj, core, "grad_w_in", win_cols, tka=512)
    win_got, (early_got,) = _grad_exchange_d2d([win_theirs], "grad_exchange_d2d_w_in", swapped=[early_pack])
    win_sum = _chip_sum(win_mine, win_got[0], core, "chip_sum_w_in")
    early_chip = _sum_two(early_pack, early_got, "chip_sum_small")
    whole = [False, True]
    send_sems, recv_sems, sent, landing, started = _chip_exchange_start([win_sum, early_chip], whole, "w_in_exchange_start")
    gx_rows, gr_nmix, gr_meta = _in_proj_bwd(dproj, x_rows, lead, dh1, norm_mix_w, win_full, started)
    (win_sum, early_chip), (win_landed, early_landed) = _chip_exchange_wait(
        send_sems, recv_sems, sent, landing, whole, gr_nmix, "w_in_exchange_wait")
    parts = dict(w_in=win_landed, w_out=late_exchanged[0], w_ff1=late_exchanged[1], w_ff2=late_exchanged[2])
    my_chip = jnp.reshape(2 * xi + yi, (1,)).astype(jnp.int32)
    early_sum = _sum_chips(early_landed, early_chip, my_chip, "small_sum")

    tail = dict(meta=gr_meta, norm_mix_w=gr_nmix)
    tail_sum = _small_allreduce(_pack(list(tail.values())))
    gs = dict(zip(early, _unpack(early_sum, [a.shape for a in early.values()])))
    gs.update(zip(tail, _unpack(tail_sum, [a.shape for a in tail.values()])))
    gs["meta"] = lax.dynamic_slice_in_dim(gs["meta"], dev * 128, 128, axis=1)
    gs["conv_w"] = lax.dynamic_slice_in_dim(gs["conv_w"], dev * cw_cols, cw_cols, axis=1)

    res = {}
    for n in _BIG:
        shp = wts[n].shape
        args = (parts[n], wts[n][0], mom1[n][0], mom2[n][0], "adamw_" + n)
        if n == "w_ff1":
            outs = _adamw_transposed(*args)
        elif n == "w_in":
            outs = _adamw(*args, own=win_sum, chip=my_chip)
        else:
            outs = _adamw(*args)
        res[n] = [o.reshape(shp) for o in outs]
    as2d = lambda a: a.reshape(-1, a.shape[-1])
    small_g = [as2d(gs[n].reshape(wts[n].shape)) for n in _SMALL]
    small_out = _adamw_small(small_g, *[[as2d(d[n]) for n in _SMALL] for d in (wts, mom1, mom2)])
    for k, n in enumerate(_SMALL):
        res[n] = [o[k].reshape(wts[n].shape) for o in (small_g,) + tuple(small_out)]

    grad_x = gx_rows.reshape(nb, seq, D_MODEL)
    return (gs["loss"][0, 0], grad_x, *[res[n][0] for n in _WEIGHTS], *[res[n][1] for n in _WEIGHTS],
            *[res[n][2] for n in _WEIGHTS], *[res[n][3] for n in _WEIGHTS])
```

```python
import numpy as np
import jax
import jax.numpy as jnp
from jax import lax
from jax.experimental import pallas as pl
from jax.experimental.pallas import tpu as pltpu

F32 = jnp.float32
MXU_DTYPE = jnp.bfloat16

D_MODEL = 1024
D_POOL = 512
D_SSM = 1536
D_XBC = 2560
N_HEADS = 24
HEAD_DIM = 64
N_GROUPS = 4
GROUP_CH = D_SSM // N_GROUPS
D_STATE = 128
CHUNK = 128
N_META = 16
LEAD = CHUNK
PAD_ROWS = LEAD - N_META
ROW_TILE = 2 * CHUNK
D_MIX = D_POOL + D_SSM
D_FF = 4096
PROJ_W = 4736
OFF_Z = D_POOL
OFF_X = D_POOL + D_SSM
OFF_DT = OFF_X + D_XBC
D_IN_PROJ = OFF_DT + N_HEADS
POOL_WINDOWS = (2, 4, 8, 16)
HALO = 16
EPS = 1e-5
N_DEV = 8

ADAM_LR, ADAM_B1, ADAM_B2, ADAM_EPS, ADAM_WD, ADAM_STEP = 0.001, 0.9, 0.999, 1e-08, 0.01, 10

VMEM_LIMIT = 60 * 1024 * 1024


def _mx(a):
    return a.astype(MXU_DTYPE)


def _dot(a, b):
    return jnp.dot(_mx(a), _mx(b), preferred_element_type=F32)


def _dot_nt(a, b):
    return lax.dot_general(_mx(a), _mx(b), (((1,), (1,)), ((), ())), preferred_element_type=F32)


def _dot_tn(a, b):
    return lax.dot_general(_mx(a), _mx(b), (((0,), (0,)), ((), ())), preferred_element_type=F32)


def _split3(x):
    hi = x.astype(MXU_DTYPE)
    r = x - hi.astype(F32)
    mid = r.astype(MXU_DTYPE)
    lo = (r - mid.astype(F32)).astype(MXU_DTYPE)
    return hi, mid, lo


def _exact_l(c, x):
    hi, mid, lo = _split3(x)
    f = lambda p: jnp.dot(c, p, preferred_element_type=F32)
    return f(hi) + f(mid) + f(lo)


def _exact_r(x, c):
    hi, mid, lo = x if isinstance(x, tuple) else _split3(x)
    f = lambda p: jnp.dot(p, c, preferred_element_type=F32)
    return f(hi) + f(mid) + f(lo)


def _contract(x, c):
    hi = x.astype(MXU_DTYPE)
    lo = (x - hi.astype(F32)).astype(MXU_DTYPE)
    return jnp.dot(hi, c, preferred_element_type=F32) + jnp.dot(lo, c, preferred_element_type=F32)


def _sigmoid(x):
    return jax.nn.sigmoid(x)


def _softplus(x):
    return jnp.maximum(x, 0.0) + jnp.log1p(jnp.exp(-jnp.abs(x)))


def _silu_and_grad(x):
    s = _sigmoid(x)
    y = x * s
    return y, s + y * (1.0 - s)


def _shift_up(ext, s):
    if s == 0:
        return ext[:CHUNK, :]
    return pltpu.roll(ext, ext.shape[0] - s, 0)[:CHUNK, :]


def _by_pool_group(lane, a2, a4, a8, a16):
    return jnp.where(lane < 128, a2, jnp.where(lane < 256, a4, jnp.where(lane < 384, a8, a16)))


def _pool_inv_count(chunk_idx):
    row = lax.broadcasted_iota(jnp.int32, (CHUNK, D_POOL), 0)
    lane = lax.broadcasted_iota(jnp.int32, (CHUNK, D_POOL), 1)
    pos1 = jnp.maximum(chunk_idx * CHUNK + row - (PAD_ROWS - 1), 1)
    w = _by_pool_group(lane, 2, 4, 8, 16)
    return 1.0 / jnp.minimum(pos1, w).astype(F32), lane


def _pool_window_sums(u_ext, lane):
    s2 = u_ext + pltpu.roll(u_ext, 1, 0)
    s4 = s2 + pltpu.roll(s2, 2, 0)
    s8 = s4 + pltpu.roll(s4, 4, 0)
    s16 = s8 + pltpu.roll(s8, 8, 0)
    return _by_pool_group(lane, s2[HALO:], s4[HALO:], s8[HALO:], s16[HALO:])


def _pool_window_sums_ahead(q_ext, lane):
    n = q_ext.shape[0]
    r2 = q_ext + pltpu.roll(q_ext, n - 1, 0)
    r4 = r2 + pltpu.roll(r2, n - 2, 0)
    r8 = r4 + pltpu.roll(r4, n - 4, 0)
    r16 = r8 + pltpu.roll(r8, n - 8, 0)
    return _by_pool_group(lane, r2[:CHUNK], r4[:CHUNK], r8[:CHUNK], r16[:CHUNK])


def _conv_pre(ext, xbc, cw, cb):
    s1 = pltpu.roll(ext, 1, 0)
    near = cw[3:4, :] * xbc + cw[2:3, :] * s1[HALO:, :]
    far = cw[1:2, :] * ext + cw[0:1, :] * s1
    return cb + near + pltpu.roll(far, 2, 0)[HALO:, :]


def _dt_and_cumsum(dtr, dt_bias, a_log, valid, tril):
    lane = lax.broadcasted_iota(jnp.int32, (CHUNK, 128), 1)
    head = lane < N_HEADS
    pre = dtr + dt_bias
    dt = jnp.where(valid & head, _softplus(pre), 0.0)
    a_row = jnp.where(head[0:1, :], -jnp.exp(a_log), 0.0)
    a_col = _exact_l(tril, dt * a_row)
    return dt, a_row, a_col, pre, head


def _decay(a_col, a_row_t, h, causal):
    seg = a_col[:, h:h + 1] - a_row_t[h:h + 1, :]
    return jnp.where(causal, jnp.exp(jnp.minimum(seg, 0.0)), 0.0)


def _ssd_chunk_fwd(xs, bm, cm, dt, a_col, s_prev, d_x, e_mat, et_f32):
    lane = lax.broadcasted_iota(jnp.int32, (CHUNK, 128), 1)
    rowi = lax.broadcasted_iota(jnp.int32, (CHUNK, CHUNK), 0)
    coli = lax.broadcasted_iota(jnp.int32, (CHUNK, CHUNK), 1)
    causal = rowi >= coli
    a_row_t = a_col.T
    ax = _exact_r(a_col, e_mat)
    dtx = _exact_r(dt, e_mat)
    xdt = xs * dtx
    ax_last = ax[CHUNK - 1:CHUNK, :]
    e_a = jnp.exp(ax)
    w_end = xdt * jnp.exp(ax_last - ax)
    cd_col = jnp.exp(jnp.sum(et_f32 * a_col[CHUNK - 1:CHUNK, :], axis=1, keepdims=True))
    ys, s_new = [], []
    for g in range(N_GROUPS):
        gs = slice(g * GROUP_CH, (g + 1) * GROUP_CH)
        bg = bm[:, g * D_STATE:(g + 1) * D_STATE]
        cg = cm[:, g * D_STATE:(g + 1) * D_STATE]
        sg = s_prev[gs, :]
        cb = _dot_nt(cg, bg)
        y_off = _dot_nt(cg, sg) * e_a[:, gs]
        s_new.append(sg * cd_col[gs, :] + _dot_tn(w_end[:, gs], bg))
        for pr in range(3):
            c0 = g * GROUP_CH + pr * 128
            xdt_p = xdt[:, c0:c0 + 128]
            h0 = g * 6 + pr * 2
            y0 = _dot(cb * _decay(a_col, a_row_t, h0, causal), xdt_p)
            y1 = _dot(cb * _decay(a_col, a_row_t, h0 + 1, causal), xdt_p)
            ys.append(jnp.where(lane < HEAD_DIM, y0, y1) + y_off[:, pr * 128:(pr + 1) * 128])
    y = jnp.concatenate(ys, axis=1) + d_x * xs
    return y, jnp.concatenate(s_new, axis=0)


def _ssd_chunk_bwd(xs, bm, cm, dt, a_row, a_col, s_prev, ds_new, dy, d_x, e_mat, et_mat, et_f32, triu):
    lane = lax.broadcasted_iota(jnp.int32, (CHUNK, 128), 1)
    sub = lax.broadcasted_iota(jnp.int32, (CHUNK, 128), 0)
    rowi = lax.broadcasted_iota(jnp.int32, (CHUNK, CHUNK), 0)
    coli = lax.broadcasted_iota(jnp.int32, (CHUNK, CHUNK), 1)
    causal = rowi >= coli
    a_row_t = a_col.T
    a_last = a_col[CHUNK - 1:CHUNK, :]
    a_split, dt_split = _split3(a_col), _split3(dt)
    sub8 = lax.broadcasted_iota(jnp.int32, (8, GROUP_CH), 0)

    dxs, dbs, dcs, dsp = [], [], [], []
    zcol = jnp.zeros((CHUNK, 128), F32)
    zrows = []
    da_col = jnp.zeros((CHUNK, 128), F32)
    ddt = jnp.zeros((CHUNK, 128), F32)
    head_sums = jnp.zeros((8, 128), F32)
    q_row = jnp.zeros((1, 128), F32)
    for g in range(N_GROUPS):
        gs = slice(g * GROUP_CH, (g + 1) * GROUP_CH)
        e_g, et_g = e_mat[:, gs], et_mat[gs, :]
        xs_g, dy_g = xs[:, gs], dy[:, gs]
        ax = _exact_r(a_split, e_g)
        dtx = _exact_r(dt_split, e_g)
        xdt = xs_g * dtx
        dte = jnp.exp(ax[CHUNK - 1:CHUNK, :] - ax)
        w_end = xdt * dte
        cd_col = jnp.exp(jnp.sum(et_f32[gs, :] * a_last, axis=1, keepdims=True))
        dye = dy_g * jnp.exp(ax)
        bg = bm[:, g * D_STATE:(g + 1) * D_STATE]
        cg = cm[:, g * D_STATE:(g + 1) * D_STATE]
        sg = s_prev[gs, :]
        dsg = ds_new[gs, :]
        cb = _dot_nt(cg, bg)
        cs = _dot_nt(cg, sg)
        dcg = _dot(dye, sg)
        dsp.append(dsg * cd_col + _dot_tn(dye, cg))
        dwg = _dot_nt(bg, dsg)
        dbg = _dot(w_end, dsg)
        ww = dwg * w_end
        t1 = jnp.sum(dsg * sg, axis=1, keepdims=True) * cd_col
        dcb = jnp.zeros((CHUNK, CHUNK), F32)
        pairs = []
        for pr in range(3):
            ps = slice(pr * 128, (pr + 1) * 128)
            xdt_p, dy_p = xdt[:, ps], dy_g[:, ps]
            acc = None
            for half in range(2):
                h = g * 6 + pr * 2 + half
                ld = _decay(a_col, a_row_t, h, causal)
                gm = cb * ld
                dym = jnp.where((lane < HEAD_DIM) if half == 0 else (lane >= HEAD_DIM), dy_p, 0.0)
                dg = _dot_nt(dym, xdt_p)
                dseg = dg * gm
                dcb = dcb + dg * ld
                t = _dot_tn(gm, dym)
                acc = t if acc is None else acc + t
                zcol = jnp.where(lane == h, jnp.sum(dseg, axis=1, keepdims=True), zcol)
                zrows.append(jnp.sum(dseg, axis=0, keepdims=True))
            pairs.append(acc)
        dxdt = dwg * dte + jnp.concatenate(pairs, axis=1)
        dcs.append(dcg + _dot(dcb, bg))
        dbs.append(dbg + _dot_tn(dcb, cg))
        dxs.append(dxdt * dtx + d_x[:, gs] * dy_g)
        da_col = da_col + _contract(dye * cs - ww, et_g)
        ddt = ddt + _contract(dxdt * xs_g, et_g)
        col_sums = jnp.where(sub8 == 0, jnp.sum(dy_g * xs_g, axis=0, keepdims=True),
                             jnp.where(sub8 == 1, jnp.sum(ww, axis=0, keepdims=True), 0.0))
        head_sums = head_sums + _exact_r(col_sums, et_g)
        q_row = q_row + jnp.sum(et_f32[gs, :] * t1, axis=0, keepdims=True)

    dd = head_sums[0:1, :]
    q_row = q_row + head_sums[1:2, :]
    zrow = jnp.concatenate(zrows + [jnp.zeros((128 - N_HEADS, CHUNK), F32)], axis=0)
    da_col = da_col + zcol - zrow.T + jnp.where(sub == CHUNK - 1, q_row, 0.0)
    rc = _exact_l(triu, da_col)
    ddt = ddt + rc * a_row
    da = jnp.sum(rc * dt, axis=0, keepdims=True)
    return (jnp.concatenate(dxs, axis=1), jnp.concatenate(dbs, axis=1), jnp.concatenate(dcs, axis=1), ddt, da, dd,
            jnp.concatenate(dsp, axis=0))


_SSD_CONSTANT_SHAPES = dict(e=((128, D_SSM), MXU_DTYPE), et=((D_SSM, 128), MXU_DTYPE), et_f32=((D_SSM, 128), F32),
                            tril=((CHUNK, CHUNK), MXU_DTYPE), triu=((CHUNK, CHUNK), MXU_DTYPE))


def _ssd_constant_scratch(names):
    return [pltpu.VMEM(*_SSD_CONSTANT_SHAPES[n]) for n in names]


def _fill_ssd_constants(**refs):
    iota = lambda shape, d: lax.broadcasted_iota(jnp.int32, shape, d)
    shift = HEAD_DIM.bit_length() - 1
    marks = dict(
        e=lambda: iota((128, D_SSM), 0) == (iota((128, D_SSM), 1) >> shift),
        et=lambda: iota((D_SSM, 128), 1) == (iota((D_SSM, 128), 0) >> shift),
        et_f32=lambda: iota((D_SSM, 128), 1) == (iota((D_SSM, 128), 0) >> shift),
        tril=lambda: iota((CHUNK, CHUNK), 1) <= iota((CHUNK, CHUNK), 0),
        triu=lambda: iota((CHUNK, CHUNK), 1) >= iota((CHUNK, CHUNK), 0))
    for name, ref in refs.items():
        ref[...] = jnp.where(marks[name](), 1.0, 0.0).astype(ref.dtype)


def _row_views(ref, widths):
    views, off = [], 0
    for w in widths:
        views.append(ref.at[:, off:off + w])
        off += w
    return views


def _full(shape):
    nd = len(shape)
    return pl.BlockSpec(shape, lambda *_: (0,) * nd)


def _params(*sem):
    return pltpu.CompilerParams(dimension_semantics=sem, vmem_limit_bytes=VMEM_LIMIT)


def _token_tiles(width, n_tok_tiles):
    return pl.BlockSpec((ROW_TILE, width), lambda i: (jnp.minimum(i, n_tok_tiles - 1), 0))


def _in_proj(x, lead, w1, win):
    nt = x.shape[0] // ROW_TILE
    m = x.shape[0] + ROW_TILE
    tm = ROW_TILE

    def body(x_ref, lead_ref, w1_ref, win_hbm, hn_ref, proj_ref, win_v, sem):
        i = pl.program_id(0)

        @pl.when(i == 0)
        def _():
            cp = pltpu.make_async_copy(win_hbm, win_v, sem)
            cp.start()
            cp.wait()

        x = jnp.where(i == nt, lead_ref[...], x_ref[...])
        r = lax.rsqrt(jnp.mean(x * x, axis=-1, keepdims=True) + EPS)
        hn = _mx(x * r * w1_ref[...])
        hn_ref[...] = hn
        for j in range(0, PROJ_W, 512):
            w = min(512, PROJ_W - j)
            proj_ref[:, j:j + w] = jnp.dot(hn, win_v[:, j:j + w], preferred_element_type=F32)

    return pl.pallas_call(
        body, grid=(m // tm,), name="in_proj",
        in_specs=[_token_tiles(D_MODEL, nt), _full((ROW_TILE, D_MODEL)), _full((1, D_MODEL)),
                  pl.BlockSpec(memory_space=pl.ANY)],
        out_specs=[pl.BlockSpec((tm, D_MODEL), lambda i: (i, 0)), pl.BlockSpec((tm, PROJ_W), lambda i: (i, 0))],
        out_shape=[jax.ShapeDtypeStruct((m, D_MODEL), MXU_DTYPE), jax.ShapeDtypeStruct((m, PROJ_W), F32)],
        scratch_shapes=[pltpu.VMEM((D_MODEL, PROJ_W), MXU_DTYPE), pltpu.SemaphoreType.DMA],
        compiler_params=_params("arbitrary"),
    )(x, lead, w1, win)


def _ffn_fwd_bwd(x, lead, y, tgt, wout, w2n, wff1, wff2, wfn):
    nt = x.shape[0] // ROW_TILE
    m = x.shape[0] + ROW_TILE
    tm = ROW_TILE
    nj = D_FF // 1024

    def body(x_ref, lead_ref, y_ref, tgt_ref, w2n_ref, wfn_ref, wout_hbm, wff1_hbm, wff2_hbm,
             loss_ref, gwf_ref, gw2_ref, ff_ref, da_ref, hn2_ref, dh1_ref, dh2_ref, dy_ref,
             wout_v, wff1_v, wff2_v, a_s, sems):
        i = pl.program_id(0)
        hp = jnp.where(i == nt, lead_ref[...], x_ref[...])

        @pl.when(i == 0)
        def _():
            cps = [pltpu.make_async_copy(s, d, sems.at[k])
                   for k, (s, d) in enumerate(((wout_hbm, wout_v), (wff1_hbm, wff1_v), (wff2_hbm, wff2_v)))]
            for cp in cps:
                cp.start()
            for cp in cps:
                cp.wait()
            loss_ref[...] = jnp.zeros_like(loss_ref)
            gwf_ref[...] = jnp.zeros_like(gwf_ref)
            gw2_ref[...] = jnp.zeros_like(gw2_ref)

        h1 = hp + jnp.dot(y_ref[...], wout_v[...], preferred_element_type=F32)
        r2 = lax.rsqrt(jnp.mean(h1 * h1, axis=-1, keepdims=True) + EPS)
        n2 = h1 * r2
        w2n_row = w2n_ref[...]
        hn2 = _mx(n2 * w2n_row)
        hn2_ref[...] = hn2
        h2 = h1
        for j in range(nj):
            js = slice(j * 1024, (j + 1) * 1024)
            a = jnp.dot(hn2, wff1_v[:, js], preferred_element_type=F32)
            a_s[:, js] = a
            ra = jnp.maximum(a, 0.0)
            ff = _mx(ra * ra)
            ff_ref[:, js] = ff
            h2 = h2 + jnp.dot(ff, wff2_v[js, :], preferred_element_type=F32)

        r3 = lax.rsqrt(jnp.mean(h2 * h2, axis=-1, keepdims=True) + EPS)
        n3 = h2 * r3
        wf_row = wfn_ref[...]
        err = n3 * wf_row - tgt_ref[...]
        tokf = (i < nt).astype(F32)
        loss_ref[...] += 0.5 * jnp.sum(jnp.mean(err * err, axis=-1, keepdims=True) * tokf)
        dout = err * (tokf / D_MODEL)
        gwf_ref[...] += jnp.sum(dout * n3, axis=0, keepdims=True)
        dn3 = dout * wf_row
        dh2 = r3 * (dn3 - n3 * jnp.mean(dn3 * n3, axis=-1, keepdims=True))
        dh2m = _mx(dh2)
        dh2_ref[...] = dh2m

        dhn2 = jnp.zeros((tm, D_MODEL), F32)
        for j in range(nj):
            js = slice(j * 1024, (j + 1) * 1024)
            dff = lax.dot_general(dh2m, wff2_v[js, :], (((1,), (1,)), ((), ())), preferred_element_type=F32)
            da = _mx(dff * (2.0 * jnp.maximum(a_s[:, js], 0.0)))
            da_ref[:, js] = da
            dhn2 = dhn2 + lax.dot_general(da, wff1_v[:, js], (((1,), (1,)), ((), ())), preferred_element_type=F32)
        gw2_ref[...] += jnp.sum(dhn2 * n2, axis=0, keepdims=True)
        dn2 = dhn2 * w2n_row
        dh1 = dh2 + r2 * (dn2 - n2 * jnp.mean(dn2 * n2, axis=-1, keepdims=True))
        dh1_ref[...] = dh1
        dy_ref[...] = lax.dot_general(_mx(dh1), wout_v[...], (((1,), (1,)), ((), ())), preferred_element_type=F32)

    rows = lambda w: pl.BlockSpec((tm, w), lambda i: (i, 0))
    hbm = pl.BlockSpec(memory_space=pl.ANY)
    return pl.pallas_call(
        body, grid=(m // tm,), name="ffn_fwd_bwd",
        in_specs=[_token_tiles(D_MODEL, nt), _full((ROW_TILE, D_MODEL)), rows(D_MIX), _token_tiles(D_MODEL, nt),
                  _full((1, D_MODEL)), _full((1, D_MODEL)), hbm, hbm, hbm],
        out_specs=[_full((1, 128)), _full((1, D_MODEL)), _full((1, D_MODEL)), rows(D_FF), rows(D_FF), rows(D_MODEL),
                   rows(D_MODEL), rows(D_MODEL), rows(D_MIX)],
        out_shape=[jax.ShapeDtypeStruct((1, 128), F32), jax.ShapeDtypeStruct((1, D_MODEL), F32),
                   jax.ShapeDtypeStruct((1, D_MODEL), F32), jax.ShapeDtypeStruct((m, D_FF), MXU_DTYPE),
                   jax.ShapeDtypeStruct((m, D_FF), MXU_DTYPE), jax.ShapeDtypeStruct((m, D_MODEL), MXU_DTYPE),
                   jax.ShapeDtypeStruct((m, D_MODEL), F32), jax.ShapeDtypeStruct((m, D_MODEL), MXU_DTYPE),
                   jax.ShapeDtypeStruct((m, D_MIX), F32)],
        scratch_shapes=[pltpu.VMEM((D_MIX, D_MODEL), MXU_DTYPE), pltpu.VMEM((D_MODEL, D_FF), MXU_DTYPE),
                        pltpu.VMEM((D_FF, D_MODEL), MXU_DTYPE), pltpu.VMEM((tm, D_FF), F32),
                        pltpu.SemaphoreType.DMA((3,))],
        compiler_params=_params("arbitrary"),
    )(x, lead, y, tgt, w2n, wfn, wout, wff1, wff2)


def _in_proj_bwd(dproj, x, lead, dh1, w1, win, after):
    nt = x.shape[0] // ROW_TILE
    m = x.shape[0] + ROW_TILE
    tm = ROW_TILE

    def body(dp_ref, x_ref, lead_ref, dh1_ref, w1_ref, win_hbm, after_ref, gx_ref, gw1_ref, gmeta_ref, win_v, sem):
        i = pl.program_id(0)

        @pl.when(i == 0)
        def _():
            cp = pltpu.make_async_copy(win_hbm, win_v, sem)
            cp.start()
            cp.wait()
            gw1_ref[...] = jnp.zeros_like(gw1_ref)
            gmeta_ref[...] = jnp.zeros_like(gmeta_ref)

        dhn = lax.dot_general(dp_ref[...], win_v[...], (((1,), (1,)), ((), ())), preferred_element_type=F32)
        x = jnp.where(i == nt, lead_ref[...], x_ref[...])
        r = lax.rsqrt(jnp.mean(x * x, axis=-1, keepdims=True) + EPS)
        n = x * r
        gw1_ref[...] += jnp.sum(dhn * n, axis=0, keepdims=True)
        dn = dhn * w1_ref[...]
        dh0 = dh1_ref[...] + r * (dn - n * jnp.mean(dn * n, axis=-1, keepdims=True))

        @pl.when(i < nt)
        def _():
            gx_ref[...] = dh0

        @pl.when(i == nt)
        def _():
            gmeta_ref[...] = dh0[PAD_ROWS:LEAD, :] + dh0[LEAD + PAD_ROWS:2 * LEAD, :]

    rows = lambda w: pl.BlockSpec((tm, w), lambda i: (i, 0))
    hbm = pl.BlockSpec(memory_space=pl.ANY)
    return pl.pallas_call(
        body, grid=(m // tm,), name="in_proj_bwd",
        in_specs=[rows(PROJ_W), _token_tiles(D_MODEL, nt), _full((ROW_TILE, D_MODEL)), rows(D_MODEL),
                  _full((1, D_MODEL)), hbm, hbm],
        out_specs=[_token_tiles(D_MODEL, nt), _full((1, D_MODEL)), _full((N_META, D_MODEL))],
        out_shape=[jax.ShapeDtypeStruct(x.shape, F32), jax.ShapeDtypeStruct((1, D_MODEL), F32),
                   jax.ShapeDtypeStruct((N_META, D_MODEL), F32)],
        scratch_shapes=[pltpu.VMEM((D_MODEL, PROJ_W), MXU_DTYPE), pltpu.SemaphoreType.DMA],
        compiler_params=_params("arbitrary"),
    )(dproj, x, lead, dh1, w1, win, after)


MXU_DEPTH = 256


def _row_slab(m, cap):
    return max(k for k in range(MXU_DEPTH, cap + 1, MXU_DEPTH) if m % k == 0)


def _tn_matmul(a, b, name, tka, max_slab=768, tn=512):
    m, ka = a.shape
    nb = b.shape[1]
    tkm = _row_slab(m, max_slab)
    n_steps = m // tkm

    def body(a_ref, b_ref, o_ref, omx_ref):
        k = pl.program_id(1)

        @pl.when(k == 0)
        def _():
            o_ref[...] = jnp.zeros_like(o_ref)

        at = _mx(a_ref[...])
        for j in range(0, nb, tn):
            w = min(tn, nb - j)
            o_ref[:, j:j + w] += lax.dot_general(at, _mx(b_ref[:, j:j + w]), (((0,), (0,)), ((), ())),
                                                 preferred_element_type=F32)

        @pl.when(k == n_steps - 1)
        def _():
            omx_ref[...] = _mx(o_ref[...])

    out = pl.BlockSpec((tka, nb), lambda i, k: (i, 0))
    return pl.pallas_call(
        body, grid=(ka // tka, n_steps), name=name,
        in_specs=[pl.BlockSpec((tkm, tka), lambda i, k: (k, i)), pl.BlockSpec((tkm, nb), lambda i, k: (k, 0))],
        out_specs=[out, out],
        out_shape=[jax.ShapeDtypeStruct((ka, nb), F32), jax.ShapeDtypeStruct((ka, nb), MXU_DTYPE)],
        compiler_params=_params("arbitrary", "arbitrary"),
    )(a, b)


def _tn_matmul_banded(a, b, core, name, band, tka, tn=512):
    m, ka = a.shape
    nb = b.shape[1]
    tkm = _row_slab(m, 768)
    n_steps = m // tkm

    def body(core_ref, a_ref, b_ref, own_ref, sib_ref, acc):
        k = pl.program_id(1)

        @pl.when(k == 0)
        def _():
            acc[...] = jnp.zeros_like(acc)

        at = _mx(a_ref[...])
        for j in range(0, nb, tn):
            w = min(tn, nb - j)
            acc[:, j:j + w] += lax.dot_general(at, _mx(b_ref[:, j:j + w]), (((0,), (0,)), ((), ())),
                                               preferred_element_type=F32)

        for c in range(2):
            @pl.when((k == n_steps - 1) & (core_ref[0] == c))
            def _():
                for chip in range(4):
                    mine, other = 2 * chip + c, 2 * chip + 1 - c
                    own_ref[chip] = acc[:, mine * band:(mine + 1) * band]
                    sib_ref[chip] = _mx(acc[:, other * band:(other + 1) * band])

    out = pl.BlockSpec((4, tka, band), lambda i, k, c: (0, i, 0))
    grid_spec = pltpu.PrefetchScalarGridSpec(
        num_scalar_prefetch=1, grid=(ka // tka, n_steps),
        in_specs=[pl.BlockSpec((tkm, tka), lambda i, k, c: (k, i)), pl.BlockSpec((tkm, nb), lambda i, k, c: (k, 0))],
        out_specs=[out, out], scratch_shapes=[pltpu.VMEM((tka, nb), F32)])
    return pl.pallas_call(
        body, grid_spec=grid_spec, name=name,
        out_shape=[jax.ShapeDtypeStruct((4, ka, band), F32), jax.ShapeDtypeStruct((4, ka, band), MXU_DTYPE)],
        compiler_params=_params("arbitrary", "arbitrary"),
    )(core, a, b)


def _assemble_bands(g, width):
    n, rows, band = g.shape
    tr = 256

    def body(g_ref, o_ref):
        parts = [g_ref[j] for j in range(n)] + [jnp.zeros((tr, width - n * band), g.dtype)]
        o_ref[...] = jnp.concatenate(parts, axis=1)

    return pl.pallas_call(
        body, grid=(rows // tr,), name="assemble_w_in",
        in_specs=[pl.BlockSpec((n, tr, band), lambda i: (0, i, 0))],
        out_specs=pl.BlockSpec((tr, width), lambda i: (i, 0)),
        out_shape=jax.ShapeDtypeStruct((rows, width), g.dtype),
        compiler_params=_params("arbitrary"),
    )(g)


def _chunk_block(b, c, nb, nc):
    return jnp.where(c == 0, nb * (nc - 1) + b, b * (nc - 1) + c - 1)

def _mixer_fwd(proj, cw, cb, dt_bias, a_log, d_x, nw, pool_w, pool_scale, nb, shards, by_cols):
    m = proj.shape[0]
    nc = m // nb // CHUNK
    ns = len(shards)
    row_widths = [D_XBC, 128, 128, D_SSM, D_SSM, D_POOL]
    rows = jnp.concatenate([cb, dt_bias, a_log, d_x, nw, pool_scale], axis=1)
    pairs = (nc - 1) // 2
    n_steps = nb * pairs + 1

    def block_of(step):
        return jnp.where(step == 0, nb * pairs, step - 1)

    def chunks_of(step):
        lead = step == 0
        b = jnp.maximum(step - 1, 0) // pairs
        j = jnp.maximum(step - 1, 0) - b * pairs
        return [(jnp.where(lead, 0, b), jnp.where(lead, 0, 2 * j + 1), 0),
                (jnp.where(lead, 1, b), jnp.where(lead, 0, 2 * j + 2), CHUNK)]

    def body(p_ref, cw_ref, rows_ref, pw_ref, *rest):
        shard_refs, (y_ref, ypre_ref, pre_ref, st_ref) = rest[:ns], rest[ns:ns + 4]
        gathered_refs, carries, constants_refs = rest[ns + 4:2 * ns + 4], rest[2 * ns + 4:2 * ns + 7], rest[2 * ns + 7:2 * ns + 10]
        gather = _Gather(shard_refs, gathered_refs, by_cols, *rest[2 * ns + 10:])
        step = pl.program_id(0)

        @pl.when(step == 0)
        def _():
            gather.start()
            _fill_ssd_constants(**dict(zip(["e", "et_f32", "tril"], constants_refs)))

        @pl.when(step == n_steps // 2)
        def _():
            gather.forward()

        shared = (cw_ref, pw_ref, _row_views(rows_ref, row_widths), constants_refs)
        for b, c, row0 in chunks_of(step):
            at = lambda ref: ref.at[row0:row0 + CHUNK]
            one_chunk(c, at(p_ref), at(y_ref), at(ypre_ref), at(pre_ref), st_ref.at[row0 // CHUNK], *shared,
                      [carry.at[b] for carry in carries])

        @pl.when(step == n_steps - 1)
        def _():
            gather.finish()

    def one_chunk(c, p_ref, y_ref, ypre_ref, pre_ref, st_ref, cw_ref, pw_ref, row_refs, constants_refs, carries):
        cb_ref, dtb_ref, alog_ref, dx_ref, nw_ref, ps_ref = row_refs
        e_ref, et_ref, tril_ref = constants_refs
        xtail, utail, state = carries

        @pl.when(c == 0)
        def _():
            xtail[...] = jnp.zeros_like(xtail)
            utail[...] = jnp.zeros_like(utail)
            state[...] = jnp.zeros_like(state)

        valid = (c > 0) | (lax.broadcasted_iota(jnp.int32, (CHUNK, 1), 0) >= PAD_ROWS)

        u = p_ref[:, 0:D_POOL]
        inv_cnt, lane = _pool_inv_count(c)
        win = _pool_window_sums(jnp.concatenate([utail[...], u], axis=0), lane)
        utail[...] = u[CHUNK - HALO:, :]
        pooled = win * inv_cnt - u
        mixed = jnp.concatenate(
            [_dot(pooled[:, g * 128:(g + 1) * 128], pw_ref[g]) for g in range(len(POOL_WINDOWS))], axis=1)
        y_ref[:, 0:D_POOL] = _mx(mixed * ps_ref[...])

        xbc = p_ref[:, OFF_X:OFF_X + D_XBC]
        pre = _conv_pre(jnp.concatenate([xtail[...], xbc], axis=0), xbc, cw_ref[...], cb_ref[...])
        xtail[...] = xbc[CHUNK - HALO:, :]
        pre_ref[...] = pre
        xc = pre * _sigmoid(pre)
        dt, _, a_col, _, _ = _dt_and_cumsum(p_ref[:, OFF_DT:OFF_DT + 128], dtb_ref[...], alog_ref[...], valid,
                                            tril_ref[...])
        s_prev = state[...]
        st_ref[...] = s_prev
        yp, s_new = _ssd_chunk_fwd(xc[:, 0:D_SSM], xc[:, D_SSM:D_SSM + 512], xc[:, D_SSM + 512:], dt, a_col, s_prev,
                                   dx_ref[...], e_ref[...], et_ref[...])
        state[...] = s_new
        ypre_ref[...] = yp
        z = p_ref[:, OFF_Z:OFF_Z + D_SSM]
        yz = yp * (z * _sigmoid(z))
        outs = []
        for g in range(N_GROUPS):
            gs = slice(g * GROUP_CH, (g + 1) * GROUP_CH)
            r = lax.rsqrt(jnp.mean(yz[:, gs] * yz[:, gs], axis=-1, keepdims=True) + EPS)
            outs.append(yz[:, gs] * r)
        y_ref[:, D_POOL:] = _mx(jnp.concatenate(outs, axis=1) * nw_ref[...])

    blk = lambda w: pl.BlockSpec((2 * CHUNK, w), lambda s: (block_of(s), 0))
    hbm = pl.BlockSpec(memory_space=pl.ANY)
    outs = pl.pallas_call(
        body, grid=(n_steps,), name="mixer_fwd",
        in_specs=[blk(PROJ_W), _full((4, D_XBC)), _full((1, sum(row_widths))), _full((4, 128, 128))] + [hbm] * ns,
        out_specs=[blk(D_MIX), blk(D_SSM), blk(D_XBC),
                   pl.BlockSpec((2, D_SSM, D_STATE), lambda s: (block_of(s), 0, 0))] + [hbm] * ns,
        out_shape=[jax.ShapeDtypeStruct((m, D_MIX), MXU_DTYPE), jax.ShapeDtypeStruct((m, D_SSM), F32),
                   jax.ShapeDtypeStruct((m, D_XBC), F32), jax.ShapeDtypeStruct((m // CHUNK, D_SSM, D_STATE), F32)]
        + _Gather.out_shapes(shards, by_cols),
        scratch_shapes=[pltpu.VMEM((nb, HALO, D_XBC), F32), pltpu.VMEM((nb, HALO, D_POOL), F32),
                        pltpu.VMEM((nb, D_SSM, D_STATE), F32)] + _ssd_constant_scratch(["e", "et_f32", "tril"])
        + _Gather.scratch(ns),
        compiler_params=_params("arbitrary"),
    )(proj, cw, rows, pool_w, *shards)
    return outs[0], outs[1], outs[2], outs[3], outs[4:]


def _mixer_bwd(proj, dy, ypre, conv_pre, states, cw, dt_bias, a_log, d_x, nw, pool_w, pool_scale, nb, parts, shares):
    m = proj.shape[0]
    nc = m // nb // CHUNK
    hb = CHUNK // HALO
    ns = len(parts)
    row_widths = [128, 128, D_SSM, D_SSM, D_POOL]
    rows = jnp.concatenate([dt_bias, a_log, d_x, nw, pool_scale], axis=1)
    grad_row_widths = [D_XBC, 128, 128, 128, D_SSM, D_POOL]
    constants = ["e", "et", "et_f32", "tril", "triu"]

    pairs = (nc - 1) // 2
    n_steps = nb * pairs + 1

    def block_of(step):
        b = jnp.minimum(step // pairs, nb - 1)
        j = pairs - 1 - (step - b * pairs)
        return jnp.where(step == n_steps - 1, nb * pairs, b * pairs + j)

    def chunks_of(step):
        lead = step == n_steps - 1
        b = jnp.minimum(step // pairs, nb - 1)
        j = pairs - 1 - (step - b * pairs)
        return [(jnp.where(lead, 0, b), jnp.where(lead, 0, 2 * j + 2), jnp.where(lead, 0, CHUNK)),
                (jnp.where(lead, 1, b), jnp.where(lead, 0, 2 * j + 1), jnp.where(lead, CHUNK, 0))]

    def body(p_ref, halo_ref, dy_ref, ypre_ref, pre_ref, st_ref, cw_ref, rows_ref, pw_ref, *rest):
        own_refs, share_refs, rest = rest[:ns], rest[ns:2 * ns], rest[2 * ns:]
        dp_ref, gcw_ref, grows_ref, gpw_ref = rest[:4]
        got_refs, sum_refs, part_refs = rest[4:ns + 4], rest[ns + 4:2 * ns + 4], rest[2 * ns + 4:3 * ns + 4]
        carries, constants_refs, rest = rest[3 * ns + 4:3 * ns + 7], rest[3 * ns + 7:3 * ns + 12], rest[3 * ns + 12:]
        chip_sums = _ChipSums(own_refs, share_refs, got_refs, sum_refs, *rest[:7])
        exchange = _ChipExchange(sum_refs, part_refs, [False] * ns, *rest[7:])
        step = pl.program_id(0)
        chip_sums.at_step(step)

        @pl.when(step == chip_sums.last_step)
        def _():
            exchange.start()

        @pl.when(step == 0)
        def _():
            _fill_ssd_constants(**dict(zip(constants, constants_refs)))
            for r in (gcw_ref, grows_ref, gpw_ref):
                r[...] = jnp.zeros_like(r)

        shared = (cw_ref, pw_ref, _row_views(rows_ref, row_widths), gcw_ref, gpw_ref,
                  _row_views(grows_ref, grad_row_widths), constants_refs)
        halos = [p_ref[CHUNK - HALO:CHUNK, 0:D_POOL], halo_ref[...]]
        for (b, c, row0), halo in zip(chunks_of(step), halos):
            rows_here = pl.ds(pl.multiple_of(row0, CHUNK), CHUNK)
            at = lambda ref: ref.at[rows_here]
            one_chunk(b, c, at(p_ref), halo, at(dy_ref), at(ypre_ref), at(pre_ref), st_ref.at[row0 // CHUNK], at(dp_ref),
                      *shared, [carry.at[b] for carry in carries])

        @pl.when(step == n_steps - 1)
        def _():
            exchange.finish()

    def one_chunk(b, c, p_ref, halo, dy_ref, ypre_ref, pre_ref, st_ref, dp_ref, cw_ref, pw_ref, row_refs, gcw_ref,
                  gpw_ref, grad_row_refs, constants_refs, carries):
        dtb_ref, alog_ref, dx_ref, nw_ref, ps_ref = row_refs
        gcb_ref, gdtb_ref, galog_ref, gd_ref, gnw_ref, gps_ref = grad_row_refs
        ds_carry, dpre_next, dq_next = carries
        e_ref, et_ref, etf_ref, tril_ref, triu_ref = constants_refs

        @pl.when(c == nc - 1)
        def _():
            ds_carry[...] = jnp.zeros_like(ds_carry)
            dpre_next[...] = jnp.zeros_like(dpre_next)
            dq_next[...] = jnp.zeros_like(dq_next)

        valid = (c > 0) | (lax.broadcasted_iota(jnp.int32, (CHUNK, 1), 0) >= PAD_ROWS)
        first = c > 0

        u = p_ref[:, 0:D_POOL]
        u_halo = jnp.where(first, halo, 0.0)
        inv_cnt, lane = _pool_inv_count(c)
        pooled = _pool_window_sums(jnp.concatenate([u_halo, u], axis=0), lane) * inv_cnt - u
        dyp = dy_ref[:, 0:D_POOL]
        ps = ps_ref[...]
        dmixed = dyp * ps
        mixed, dpooled = [], []
        for g in range(len(POOL_WINDOWS)):
            gsl = slice(g * 128, (g + 1) * 128)
            pw = pw_ref[g]
            mixed.append(_dot(pooled[:, gsl], pw))
            dpooled.append(_dot_nt(dmixed[:, gsl], pw))
            gpw_ref[g] += _dot_tn(pooled[:, gsl], dmixed[:, gsl])
        gps_ref[...] += jnp.sum(dyp * jnp.concatenate(mixed, axis=1), axis=0, keepdims=True)
        dpooled = jnp.concatenate(dpooled, axis=1)
        dq = dpooled * inv_cnt
        du = _pool_window_sums_ahead(jnp.concatenate([dq, dq_next[...]], axis=0), lane) - dpooled
        dq_next[...] = dq[0:HALO, :]
        dp_ref[:, 0:D_POOL] = _mx(du)

        yp = ypre_ref[...]
        z = p_ref[:, OFF_Z:OFF_Z + D_SSM]
        sz, dsz = _silu_and_grad(z)
        yz = yp * sz
        do = dy_ref[:, D_POOL:]
        nw_row = nw_ref[...]
        dyz = []
        gnw = []
        for g in range(N_GROUPS):
            gs = slice(g * GROUP_CH, (g + 1) * GROUP_CH)
            r = lax.rsqrt(jnp.mean(yz[:, gs] * yz[:, gs], axis=-1, keepdims=True) + EPS)
            n = yz[:, gs] * r
            gnw.append(jnp.sum(do[:, gs] * n, axis=0, keepdims=True))
            dn = do[:, gs] * nw_row[:, gs]
            dyz.append(r * (dn - n * jnp.mean(dn * n, axis=-1, keepdims=True)))
        gnw_ref[...] += jnp.concatenate(gnw, axis=1)
        dyz = jnp.concatenate(dyz, axis=1)
        dp_ref[:, OFF_Z:OFF_Z + D_SSM] = _mx(dyz * yp * dsz)
        dyp_ssm = dyz * sz

        xc, dsilu = _silu_and_grad(pre_ref[...])
        dtr = p_ref[:, OFF_DT:OFF_DT + 128]
        dt, a_row, a_col, dt_pre, head = _dt_and_cumsum(dtr, dtb_ref[...], alog_ref[...], valid, tril_ref[...])
        dxs, dbm, dcm, ddt, da, dd, ds_prev = _ssd_chunk_bwd(
            xc[:, 0:D_SSM], xc[:, D_SSM:D_SSM + 512], xc[:, D_SSM + 512:], dt, a_row, a_col, st_ref[...],
            ds_carry[...], dyp_ssm, dx_ref[...], e_ref[...], et_ref[...], etf_ref[...], triu_ref[...])
        ds_carry[...] = ds_prev
        gd_ref[...] += dd
        galog_ref[...] += da * a_row
        ddtr = jnp.where(valid & head, ddt * _sigmoid(dt_pre), 0.0)
        gdtb_ref[...] += jnp.sum(ddtr, axis=0, keepdims=True)
        dp_ref[:, OFF_DT:OFF_DT + 128] = _mx(ddtr)

        dpre = jnp.concatenate([dxs, dbm, dcm], axis=1) * dsilu
        gcb_ref[...] += jnp.sum(dpre, axis=0, keepdims=True)
        dext = jnp.concatenate([dpre, dpre_next[...]], axis=0)
        dpre_next[...] = dpre[0:HALO, :]
        ups = [_shift_up(dext, 3 - k) for k in range(4)]
        xbc = p_ref[:, OFF_X:OFF_X + D_XBC]
        gcw_ref[...] += jnp.concatenate([jnp.sum(xbc * ups[k], axis=0, keepdims=True) for k in range(4)], axis=0)
        cw = cw_ref[...]
        dp_ref[:, OFF_X:OFF_X + D_XBC] = _mx(cw[3:4, :] * ups[3] + cw[2:3, :] * ups[2]
                                             + cw[1:2, :] * ups[1] + cw[0:1, :] * ups[0])

    blk = lambda w: pl.BlockSpec((2 * CHUNK, w), lambda s: (block_of(s), 0))

    def halo_rows(s):
        b = jnp.minimum(s // pairs, nb - 1)
        below = block_of(s) * 2 * hb - 1
        return jnp.where(block_of(s) == b * pairs, _chunk_block(b, 0, nb, nc) * hb + hb - 1, below)

    hbm = pl.BlockSpec(memory_space=pl.ANY)
    sum_shapes = _ChipSums.out_shapes(parts)
    assert n_steps > 3 + 4 * ns, "the chip sums need their grid steps"
    outs = pl.pallas_call(
        body, grid=(n_steps,), name="mixer_bwd",
        in_specs=[blk(PROJ_W), pl.BlockSpec((HALO, D_POOL), lambda s: (halo_rows(s), 0)), blk(D_MIX), blk(D_SSM),
                  blk(D_XBC), pl.BlockSpec((2, D_SSM, D_STATE), lambda s: (block_of(s), 0, 0)),
                  _full((4, D_XBC)), _full((1, sum(row_widths))), _full((4, 128, 128))] + [hbm] * (2 * ns),
        out_specs=[blk(PROJ_W), _full((4, D_XBC)), _full((1, sum(grad_row_widths))), _full((4, 128, 128))]
        + [hbm] * (3 * ns),
        out_shape=[jax.ShapeDtypeStruct((m, PROJ_W), MXU_DTYPE), jax.ShapeDtypeStruct((4, D_XBC), F32),
                   jax.ShapeDtypeStruct((1, sum(grad_row_widths)), F32), jax.ShapeDtypeStruct((4, 128, 128), F32)]
        + sum_shapes + sum_shapes + _ChipExchange.out_shapes(sum_shapes, [False] * ns),
        scratch_shapes=[pltpu.VMEM((nb, D_SSM, D_STATE), F32), pltpu.VMEM((nb, HALO, D_XBC), F32),
                        pltpu.VMEM((nb, HALO, D_POOL), F32)] + _ssd_constant_scratch(constants)
        + _ChipSums.scratch(parts) + _ChipExchange.scratch(ns),
        compiler_params=_params("arbitrary"),
    )(proj, proj, dy, ypre, conv_pre, states, cw, rows, pool_w, *parts, *shares)
    dproj, g_cw, g_rows, g_pw = outs[:4]
    offs = np.cumsum([0] + grad_row_widths)
    g_cb, g_dtb, g_alog, g_d, g_nw, g_ps = (g_rows[:, a:b] for a, b in zip(offs[:-1], offs[1:]))
    return (dproj, g_cw, g_cb, g_dtb, g_alog, g_d, g_nw, g_pw, g_ps), outs[4 + 2 * ns:]


MESH_IDS = pl.DeviceIdType.MESH
_HBM = pl.BlockSpec(memory_space=pltpu.HBM)


def _coords():
    return lax.axis_index("x"), lax.axis_index("y"), lax.axis_index("c")


def _other_chips(x, y):
    return [(1 - x, y), (x, 1 - y), (1 - x, 1 - y)]


class _Gather:
    def __init__(self, ins, outs, by_cols, send_sems, recv_sems, local_sems):
        self.ins, self.outs, self.by_cols, self.n = ins, outs, by_cols, len(ins)
        self.send_sems, self.recv_sems, self.local_sems = send_sems, recv_sems, local_sems
        self.x, self.y, self.c = _coords()
        self.me, self.sibling = (self.x, self.y, self.c), (self.x, self.y, 1 - self.c)
        self.chips = _other_chips(self.x, self.y)

    @staticmethod
    def scratch(n):
        return [pltpu.SemaphoreType.DMA((7 * n,)), pltpu.SemaphoreType.DMA((7 * n,)), pltpu.SemaphoreType.DMA((n,))]

    @staticmethod
    def out_shapes(shards, by_cols):
        return [jax.ShapeDtypeStruct((s.shape[0], N_DEV * s.shape[1]) if cols else (N_DEV,) + s.shape, s.dtype)
                for s, cols in zip(shards, by_cols)]

    def _block(self, t, device):
        idx = 4 * device[0] + 2 * device[1] + device[2]
        if not self.by_cols[t]:
            return self.outs[t].at[idx]
        w = self.ins[t].shape[1]
        return self.outs[t].at[:, pl.ds(pl.multiple_of(idx * w, w), w)]

    def _copy(self, t, k, block, to, own=False):
        dst = self._block(t, block)
        return pltpu.make_async_remote_copy(
            src_ref=self.ins[t] if own else dst, dst_ref=dst, send_sem=self.send_sems.at[t * 7 + k],
            recv_sem=self.recv_sems.at[t * 7 + k], device_id=to, device_id_type=MESH_IDS)

    def _mine(self):
        return [pltpu.make_async_copy(self.ins[t], self._block(t, self.me), self.local_sems.at[t])
                for t in range(self.n)]

    def _first(self):
        cps = []
        for t in range(self.n):
            cps.append(self._copy(t, 0, self.me, self.sibling, own=True))
            cps += [self._copy(t, 1 + j, self.me, (*chip, self.c), own=True) for j, chip in enumerate(self.chips)]
        return cps

    def _passed(self):
        return [self._copy(t, 4 + j, (*chip, self.c), self.sibling)
                for j, chip in enumerate(self.chips) for t in range(self.n)]

    def start(self):
        for cp in self._mine() + self._first():
            cp.start()

    def forward(self):
        for j, chip in enumerate(self.chips):
            for t in range(self.n):
                self._copy(t, 1 + j, (*chip, self.c), self.me).wait_recv()
                self._copy(t, 4 + j, (*chip, self.c), self.sibling).start()

    def finish(self):
        for t in range(self.n):
            self._copy(t, 0, self.sibling, self.me).wait_recv()
            for j, chip in enumerate(self.chips):
                self._copy(t, 4 + j, (*chip, 1 - self.c), self.me).wait_recv()
        for cp in self._first() + self._passed():
            cp.wait_send()
        for cp in self._mine():
            cp.wait()


def _weight_gather(shards):
    n = len(shards)

    def body(*refs):
        g = _Gather(refs[:n], refs[n:2 * n], [False] * n, *refs[2 * n:])
        g.start()
        g.forward()
        g.finish()

    return pl.pallas_call(
        body, name="weight_gather",
        in_specs=[_HBM] * n, out_specs=[_HBM] * n,
        out_shape=_Gather.out_shapes(shards, [False] * n),
        scratch_shapes=_Gather.scratch(n),
    )(*shards)


def _grad_exchange_d2d(gs, name, swapped=()):
    n, ns = len(gs), len(swapped)

    def body(*refs):
        ins, whole_ins = refs[:n], refs[n:n + ns]
        got, whole_got = refs[n + ns:2 * n + ns], refs[2 * n + ns:2 * (n + ns)]
        send_sems, recv_sems = refs[2 * (n + ns):]
        x, y, c = _coords()
        theirs = lambda t, k: ins[t].at[k] if len(ins[t].shape) == 3 else ins[t].at[k, 1 - c]
        pairs = [(theirs(t, k), got[t].at[k]) for t in range(n) for k in range(4)] + list(zip(whole_ins, whole_got))
        remote = [pltpu.make_async_remote_copy(
            src_ref=src, dst_ref=dst, send_sem=send_sems.at[i], recv_sem=recv_sems.at[i], device_id=(x, y, 1 - c),
            device_id_type=MESH_IDS) for i, (src, dst) in enumerate(pairs)]
        for cp in remote:
            cp.start()
        for cp in remote:
            cp.wait_recv()
        for cp in remote:
            cp.wait_send()

    outs = pl.pallas_call(
        body, name=name,
        in_specs=[_HBM] * (n + ns), out_specs=[_HBM] * (n + ns),
        out_shape=[jax.ShapeDtypeStruct((4,) + g.shape[-2:], g.dtype) for g in gs]
        + [jax.ShapeDtypeStruct(a.shape, a.dtype) for a in swapped],
        scratch_shapes=[pltpu.SemaphoreType.DMA((4 * n + ns,)), pltpu.SemaphoreType.DMA((4 * n + ns,))],
    )(*gs, *swapped)
    return outs[:n], outs[n:]


class _ChipExchange:
    def __init__(self, ins, outs, whole, send_sems, recv_sems, local_sems):
        self.ins, self.outs, self.whole, self.n = ins, outs, whole, len(ins)
        self.send_sems, self.recv_sems, self.local_sems = send_sems, recv_sems, local_sems
        self.x, self.y, self.c = _coords()
        self.my_chip = 2 * self.x + self.y
        self.chips = _other_chips(self.x, self.y)

    @staticmethod
    def scratch(n):
        return [pltpu.SemaphoreType.DMA((3 * n,)), pltpu.SemaphoreType.DMA((3 * n,)), pltpu.SemaphoreType.DMA((n,))]

    def _src(self, t, k):
        return self.ins[t] if self.whole[t] else self.ins[t].at[k]

    def _local(self):
        return [pltpu.make_async_copy(self._src(t, self.my_chip), self.outs[t].at[self.my_chip], self.local_sems.at[t])
                for t in range(self.n)]

    def _remote(self):
        return [pltpu.make_async_remote_copy(
            src_ref=self._src(t, 2 * cx + cy), dst_ref=self.outs[t].at[self.my_chip],
            send_sem=self.send_sems.at[t * 3 + j], recv_sem=self.recv_sems.at[t * 3 + j],
            device_id=(cx, cy, self.c), device_id_type=MESH_IDS)
            for t in range(self.n) for j, (cx, cy) in enumerate(self.chips)]

    def start(self):
        for cp in self._remote() + self._local():
            cp.start()

    def finish(self):
        for t in range(self.n):
            for j, (cx, cy) in enumerate(self.chips):
                slot = self.outs[t].at[2 * cx + cy]
                pltpu.make_async_remote_copy(
                    src_ref=slot, dst_ref=slot, send_sem=self.send_sems.at[t * 3 + j],
                    recv_sem=self.recv_sems.at[t * 3 + j], device_id=(cx, cy, self.c),
                    device_id_type=MESH_IDS).wait_recv()
        for cp in self._remote():
            cp.wait_send()
        for cp in self._local():
            cp.wait()

    @staticmethod
    def out_shapes(arrs, whole):
        return [jax.ShapeDtypeStruct(((4,) + a.shape) if w else a.shape, a.dtype) for a, w in zip(arrs, whole)]


class _ChipSums:
    def __init__(self, parts, shares, got, sums, a_buf, b_buf, o_buf, send_sems, recv_sems, load_sems, store_sems):
        self.parts, self.shares, self.got, self.sums = parts, shares, got, sums
        self.a_buf, self.b_buf, self.o_buf = a_buf, b_buf, o_buf
        self.send_sems, self.recv_sems, self.load_sems, self.store_sems = send_sems, recv_sems, load_sems, store_sems
        self.x, self.y, self.c = _coords()
        self.n_blocks = 4 * len(parts)
        self.last_step = 3 + self.n_blocks

    @staticmethod
    def scratch(parts):
        r, c, n = max(p.shape[-2] for p in parts), parts[0].shape[-1], 4 * len(parts)
        return [pltpu.VMEM((2, r, c), F32), pltpu.VMEM((2, r, c), MXU_DTYPE), pltpu.VMEM((2, r, c), MXU_DTYPE),
                pltpu.SemaphoreType.DMA((n,)), pltpu.SemaphoreType.DMA((n,)), pltpu.SemaphoreType.DMA((2, 2)),
                pltpu.SemaphoreType.DMA((2,))]

    @staticmethod
    def out_shapes(parts):
        return [jax.ShapeDtypeStruct((4,) + p.shape[-2:], MXU_DTYPE) for p in parts]

    def _swap(self):
        return [pltpu.make_async_remote_copy(
            src_ref=self.shares[t].at[k, 1 - self.c], dst_ref=self.got[t].at[k], send_sem=self.send_sems.at[4 * t + k],
            recv_sem=self.recv_sems.at[4 * t + k], device_id=(self.x, self.y, 1 - self.c), device_id_type=MESH_IDS)
            for t in range(len(self.parts)) for k in range(4)]

    def _slot(self, j, buf):
        t = j // 4
        return buf.at[j % 2, pl.ds(0, self.parts[t].shape[-2])]

    def _loads(self, j):
        t, k = divmod(j, 4)
        return [pltpu.make_async_copy(self.parts[t].at[k, self.c], self._slot(j, self.a_buf), self.load_sems.at[j % 2, 0]),
                pltpu.make_async_copy(self.got[t].at[k], self._slot(j, self.b_buf), self.load_sems.at[j % 2, 1])]

    def _store(self, j):
        t, k = divmod(j, 4)
        return pltpu.make_async_copy(self._slot(j, self.o_buf), self.sums[t].at[k], self.store_sems.at[j % 2])

    def at_step(self, step):
        @pl.when(step == 0)
        def _():
            for cp in self._swap():
                cp.start()

        @pl.when(step == 2)
        def _():
            for cp in self._swap():
                cp.wait_recv()
            for cp in self._swap():
                cp.wait_send()
            for cp in self._loads(0):
                cp.start()

        for j in range(self.n_blocks):
            @pl.when(step == 3 + j)
            def _(j=j):
                for cp in self._loads(j):
                    cp.wait()
                if j + 1 < self.n_blocks:
                    for cp in self._loads(j + 1):
                        cp.start()
                if j >= 2:
                    self._store(j - 2).wait()
                a, b, o = (self._slot(j, buf) for buf in (self.a_buf, self.b_buf, self.o_buf))
                o[...] = (a[...] + b[...].astype(F32)).astype(o.dtype)
                self._store(j).start()

        @pl.when(step == self.last_step)
        def _():
            self._store(self.n_blocks - 2).wait()
            self._store(self.n_blocks - 1).wait()


_SEMAPHORES = pl.BlockSpec(memory_space=pltpu.SEMAPHORE)
_SIDE_EFFECT = pltpu.SideEffectType.DATAFLOW_SIDE_EFFECTING


def _exchange_peers(everyone):
    x, y, c = _coords()
    if not everyone:
        return [((cx, cy, c), 2 * cx + cy) for cx, cy in _other_chips(x, y)], 2 * x + y
    flip = lambda v, bit: 1 - v if bit else v
    peers = [(flip(x, r & 4), flip(y, r & 2), flip(c, r & 1)) for r in range(1, N_DEV)]
    return [(p, 4 * p[0] + 2 * p[1] + p[2]) for p in peers], 4 * x + 2 * y + c


def _split_exchange_copies(srcs, lands, whole, send_sems, recv_sems, waiting, everyone=False):
    peers, my_slot = _exchange_peers(everyone)
    copies = []
    for t in range(len(srcs)):
        for j, (peer, slot) in enumerate(peers):
            src = srcs[t] if whole[t] else srcs[t].at[slot]
            dst = lands[t].at[slot] if waiting else lands[t].at[my_slot]
            copies.append(pltpu.make_async_remote_copy(
                src_ref=src, dst_ref=dst, send_sem=send_sems.at[len(peers) * t + j],
                recv_sem=recv_sems.at[len(peers) * t + j], device_id=peer, device_id_type=MESH_IDS))
    return copies


def _chip_exchange_start(arrays, whole, name, everyone=False):
    n = len(arrays)
    n_slots = N_DEV if everyone else 4
    n_sems = (n_slots - 1) * n
    assert all(whole) or not everyone

    def body(*refs):
        srcs, lands = refs[:n], refs[n:2 * n]
        send_sems, recv_sems = refs[2 * n:2 * n + 2]
        for cp in _split_exchange_copies(srcs, lands, whole, send_sems, recv_sems, waiting=False, everyone=everyone):
            cp.start()
        token = refs[-1]
        token[...] = jnp.zeros_like(token)

    land_shapes = [((n_slots,) + a.shape) if w else a.shape for a, w in zip(arrays, whole)]
    hbm = lambda shape, a: pltpu.HBM(shape, a.dtype)
    outs = pl.pallas_call(
        body, name=name,
        out_shape=(pltpu.SemaphoreType.DMA((n_sems,)), pltpu.SemaphoreType.DMA((n_sems,)),
                   *[hbm(a.shape, a) for a in arrays], *[hbm(s, a) for s, a in zip(land_shapes, arrays)],
                   jax.ShapeDtypeStruct((8, 128), F32)),
        in_specs=(_HBM,) * (2 * n),
        out_specs=(_SEMAPHORES, _SEMAPHORES) + (_HBM,) * (2 * n) + (pl.BlockSpec(memory_space=pltpu.VMEM),),
        input_output_aliases={i: 2 + i for i in range(2 * n)},
        compiler_params=pltpu.CompilerParams(has_side_effects=_SIDE_EFFECT),
    )(*[pltpu.with_memory_space_constraint(a, pltpu.HBM) for a in arrays],
      *[pltpu.with_memory_space_constraint(lax.empty(s, a.dtype), pltpu.HBM) for s, a in zip(land_shapes, arrays)])
    return outs[0], outs[1], outs[2:2 + n], outs[2 + n:2 + 2 * n], outs[-1]


def _chip_exchange_wait(send_sems, recv_sems, srcs, lands, whole, after, name, everyone=False):
    n = len(srcs)

    def body(*refs):
        src_refs, land_refs = refs[:n], refs[n:2 * n]
        for cp in _split_exchange_copies(src_refs, land_refs, whole, refs[2 * n], refs[2 * n + 1], waiting=True,
                                         everyone=everyone):
            cp.wait_send()
            cp.wait_recv()

    outs = pl.pallas_call(
        body, name=name, out_shape=tuple(pltpu.HBM(a.shape, a.dtype) for a in (*srcs, *lands)),
        in_specs=(_HBM,) * (2 * n) + (_SEMAPHORES, _SEMAPHORES, pl.BlockSpec(memory_space=pl.ANY)),
        out_specs=(_HBM,) * (2 * n), input_output_aliases={i: i for i in range(2 * n)},
        compiler_params=pltpu.CompilerParams(has_side_effects=_SIDE_EFFECT),
    )(*srcs, *lands, send_sems, recv_sems, after)
    return outs[:n], outs[n:]


def _sum_two(a, b, name):
    def body(a_ref, b_ref, o_ref):
        o_ref[...] = a_ref[...] + b_ref[...]

    return pl.pallas_call(body, name=name, out_shape=jax.ShapeDtypeStruct(a.shape, a.dtype))(a, b)


def _sum_chips(landed, own, chip, name):
    def body(chip_ref, l_ref, own_ref, o_ref):
        part = lambda k: jnp.where(chip_ref[0] == k, own_ref[...], l_ref[k])
        total = part(0)
        for k in range(1, landed.shape[0]):
            total = total + part(k)
        o_ref[...] = total

    grid_spec = pltpu.PrefetchScalarGridSpec(
        num_scalar_prefetch=1, grid=(1,),
        in_specs=[pl.BlockSpec(landed.shape, lambda i, c: (0, 0, 0)), pl.BlockSpec(own.shape, lambda i, c: (0, 0))],
        out_specs=pl.BlockSpec(own.shape, lambda i, c: (0, 0)))
    return pl.pallas_call(body, grid_spec=grid_spec, name=name,
                          out_shape=jax.ShapeDtypeStruct(own.shape, own.dtype))(chip, landed, own)


def _row_tile(rows, cols, n_arrays):
    budget = 24 * 1024 * 1024
    padded = -(-cols // 128) * 128
    step = 16 if rows % 16 == 0 else 8
    tr = max(step, budget // (n_arrays * 2 * 4 * padded) // step * step)
    while rows % tr:
        tr -= step
    return tr


def _chip_sum(g, got, core, name):
    rows, cols = g.shape[-2:]
    tr = _row_tile(rows, cols, 3)

    def body(c_ref, a_ref, b_ref, o_ref):
        o_ref[...] = (a_ref[...] + b_ref[...].astype(F32)).astype(o_ref.dtype)

    mine = (pl.BlockSpec((None, tr, cols), lambda k, i, c: (k, i, 0)) if g.ndim == 3
            else pl.BlockSpec((None, None, tr, cols), lambda k, i, c: (k, c[0], i, 0)))
    grid_spec = pltpu.PrefetchScalarGridSpec(
        num_scalar_prefetch=1, grid=(4, rows // tr),
        in_specs=[mine, pl.BlockSpec((None, tr, cols), lambda k, i, c: (k, i, 0))],
        out_specs=pl.BlockSpec((None, tr, cols), lambda k, i, c: (k, i, 0)))
    return pl.pallas_call(body, grid_spec=grid_spec, name=name,
                          out_shape=jax.ShapeDtypeStruct((4, rows, cols), MXU_DTYPE),
                          compiler_params=_params("arbitrary", "arbitrary"))(core, g, got)


def _adamw_math(w, g, m, v):
    m2 = ADAM_B1 * m + (1.0 - ADAM_B1) * g
    v2 = ADAM_B2 * v + (1.0 - ADAM_B2) * (g * g)
    m_hat = m2 / (1.0 - ADAM_B1 ** ADAM_STEP)
    v_hat = v2 / (1.0 - ADAM_B2 ** ADAM_STEP)
    delta = -ADAM_LR * (m_hat / (jnp.sqrt(v_hat) + ADAM_EPS) + ADAM_WD * w)
    return delta, m2, v2


def _adamw(parts, w, m, v, name, own=None, chip=None):
    rows, cols = w.shape
    tr = _row_tile(rows, cols, 11 if own is None else 15)

    def body(*refs):
        if own is None:
            p_ref, w_ref, m_ref, v_ref, g_ref, d_ref, m2_ref, v2_ref = refs
            part = lambda k: p_ref[k].astype(F32)
        else:
            chip_ref, p_ref, own_ref, w_ref, m_ref, v_ref, g_ref, d_ref, m2_ref, v2_ref = refs
            part = lambda k: jnp.where(chip_ref[0] == k, own_ref[k], p_ref[k]).astype(F32)
        g = ((part(0) + part(1)) + part(2)) + part(3)
        d, m2, v2 = _adamw_math(w_ref[...], g, m_ref[...], v_ref[...])
        g_ref[...] = g
        d_ref[...] = d
        m2_ref[...] = m2
        v2_ref[...] = v2

    blk = pl.BlockSpec((tr, cols), lambda i, *_: (i, 0))
    pblk = pl.BlockSpec((4, tr, cols), lambda i, *_: (0, i, 0))
    out = jax.ShapeDtypeStruct((rows, cols), F32)
    if own is None:
        return pl.pallas_call(body, grid=(rows // tr,), name=name, in_specs=[pblk, blk, blk, blk],
                              out_specs=[blk] * 4, out_shape=[out] * 4,
                              compiler_params=_params("arbitrary"))(parts, w, m, v)
    grid_spec = pltpu.PrefetchScalarGridSpec(num_scalar_prefetch=1, grid=(rows // tr,),
                                             in_specs=[pblk, pblk, blk, blk, blk], out_specs=[blk] * 4)
    return pl.pallas_call(body, grid_spec=grid_spec, name=name, out_shape=[out] * 4,
                          compiler_params=_params("arbitrary"))(chip, parts, own, w, m, v)


def _adamw_transposed(parts_t, w, m, v, name):
    rows, cols = w.shape
    tc = 256

    def body(p_ref, w_ref, m_ref, v_ref, g_ref, d_ref, m2_ref, v2_ref):
        part = lambda k: p_ref[k].astype(F32)
        g = (((part(0) + part(1)) + part(2)) + part(3)).T
        d, m2, v2 = _adamw_math(w_ref[...], g, m_ref[...], v_ref[...])
        g_ref[...] = g
        d_ref[...] = d
        m2_ref[...] = m2
        v2_ref[...] = v2

    blk = pl.BlockSpec((rows, tc), lambda i: (0, i))
    out = jax.ShapeDtypeStruct((rows, cols), F32)
    return pl.pallas_call(body, grid=(cols // tc,), name=name,
                          in_specs=[pl.BlockSpec((4, tc, rows), lambda i: (0, i, 0)), blk, blk, blk],
                          out_specs=[blk] * 4, out_shape=[out] * 4,
                          compiler_params=_params("arbitrary"))(parts_t, w, m, v)


def _adamw_small(gs, ws, ms, vs):
    n = len(ws)

    def body(*refs):
        g_refs, w_refs, m_refs, v_refs = (refs[k * n:(k + 1) * n] for k in range(4))
        d_refs, m2_refs, v2_refs = (refs[(4 + k) * n:(5 + k) * n] for k in range(3))
        for t in range(n):
            d, m2, v2 = _adamw_math(w_refs[t][...], g_refs[t][...], m_refs[t][...], v_refs[t][...])
            d_refs[t][...] = d
            m2_refs[t][...] = m2
            v2_refs[t][...] = v2

    outs = pl.pallas_call(body, name="adamw_small",
                          out_shape=[jax.ShapeDtypeStruct(w.shape, F32) for w in ws] * 3)(*gs, *ws, *ms, *vs)
    return outs[:n], outs[n:2 * n], outs[2 * n:]


_PACK_TILE = 8 * 128


def _pack(arrays):
    rows = []
    for a in arrays:
        flat = a.astype(F32).reshape(-1)
        rows.append(jnp.pad(flat, (0, -flat.shape[0] % _PACK_TILE)).reshape(-1, 128))
    return jnp.concatenate(rows, axis=0)


def _unpack(pack, shapes):
    out, r = [], 0
    for s in shapes:
        n = int(np.prod(s))
        out.append(pack[r:r + -(-n // 128)].reshape(-1)[:n].reshape(s))
        r += -(-n // _PACK_TILE) * 8
    return out


def _pad128(v):
    v = v.reshape(1, -1).astype(F32)
    return jnp.pad(v, ((0, 0), (0, 128 - v.shape[1])))


_WEIGHTS = ["meta", "norm_mix_w", "w_in", "pool_w", "pool_scale", "conv_w", "conv_b", "dt_bias", "a_log", "d_skip",
            "ssm_norm_w", "w_out", "norm_ffn_w", "w_ff1", "w_ff2", "norm_f_w"]
_BIG = ["w_in", "w_out", "w_ff1", "w_ff2"]
_SMALL = [n for n in _WEIGHTS if n not in _BIG]


def kernel(x, meta, norm_mix_w, w_in, pool_w, pool_scale, conv_w, conv_b, dt_bias, a_log, d_skip, ssm_norm_w, w_out, norm_ffn_w, w_ff1, w_ff2, norm_f_w, loss_target, m_meta, m_norm_mix_w, m_w_in, m_pool_w, m_pool_scale, m_conv_w, m_conv_b, m_dt_bias, m_a_log, m_d_skip, m_ssm_norm_w, m_w_out, m_norm_ffn_w, m_w_ff1, m_w_ff2, m_norm_f_w, v_meta, v_norm_mix_w, v_w_in, v_pool_w, v_pool_scale, v_conv_w, v_conv_b, v_dt_bias, v_a_log, v_d_skip, v_ssm_norm_w, v_w_out, v_norm_ffn_w, v_w_ff1, v_w_ff2, v_norm_f_w):
    wts = dict(meta=meta, norm_mix_w=norm_mix_w, w_in=w_in, pool_w=pool_w, pool_scale=pool_scale, conv_w=conv_w,
               conv_b=conv_b, dt_bias=dt_bias, a_log=a_log, d_skip=d_skip, ssm_norm_w=ssm_norm_w, w_out=w_out,
               norm_ffn_w=norm_ffn_w, w_ff1=w_ff1, w_ff2=w_ff2, norm_f_w=norm_f_w)
    mom1 = dict(zip(_WEIGHTS, (m_meta, m_norm_mix_w, m_w_in, m_pool_w, m_pool_scale, m_conv_w, m_conv_b, m_dt_bias,
                               m_a_log, m_d_skip, m_ssm_norm_w, m_w_out, m_norm_ffn_w, m_w_ff1, m_w_ff2, m_norm_f_w)))
    mom2 = dict(zip(_WEIGHTS, (v_meta, v_norm_mix_w, v_w_in, v_pool_w, v_pool_scale, v_conv_w, v_conv_b, v_dt_bias,
                               v_a_log, v_d_skip, v_ssm_norm_w, v_w_out, v_norm_ffn_w, v_w_ff1, v_w_ff2, v_norm_f_w)))
    xi, yi, ci = _coords()
    dev = 4 * xi + 2 * yi + ci
    win_cols = w_in.shape[-1]
    cw_cols = conv_w.shape[-1]

    nb, seq, _ = x.shape
    core = jnp.reshape(ci, (1,)).astype(jnp.int32)
    owners = lambda a: a.reshape((4, 2) + a.shape[1:])

    lead_pack = jnp.zeros((N_META, 512), F32)
    lead_pack = lead_pack.at[:, :128].set(meta).at[:4, 128:128 + cw_cols].set(conv_w[0])
    g_win, g_lead = _weight_gather([_mx(w_in[0]), lead_pack])
    win_full = _assemble_bands(g_win, PROJ_W)
    meta_full = jnp.transpose(g_lead[:, :, :128], (1, 0, 2)).reshape(N_META, D_MODEL)
    cw_full = jnp.transpose(g_lead[:, :4, 128:128 + cw_cols], (1, 0, 2)).reshape(4, D_XBC)

    lead = jnp.concatenate([jnp.zeros((PAD_ROWS, D_MODEL), F32), meta_full] * nb, axis=0)
    x_rows = x.reshape(nb * seq, D_MODEL)
    tgt_rows = loss_target.reshape(nb * seq, D_MODEL)
    dt_bias_p, a_log_p = _pad128(dt_bias), _pad128(a_log)
    d_x = jnp.repeat(d_skip.reshape(1, N_HEADS).astype(F32), HEAD_DIM, axis=1)
    norm_f_row = norm_f_w.reshape(1, D_MODEL)

    hn1, proj = _in_proj(x_rows, lead, norm_mix_w, win_full)
    late_cols = [False, True, False]
    y, ypre, conv_pre, states, (g_wout, wff1_full, g_wff2) = _mixer_fwd(
        proj, cw_full, conv_b, dt_bias_p, a_log_p, d_x, ssm_norm_w, pool_w[0], pool_scale, nb,
        [_mx(w_out[0]), _mx(w_ff1[0]), _mx(w_ff2[0])], late_cols)
    wout_full = g_wout.reshape(D_MIX, D_MODEL)
    wff2_full = g_wff2.reshape(D_FF, D_MODEL)
    loss, gr_nf, gr_nffn, ff, da, hn2, dh1, dh2, dy = _ffn_fwd_bwd(
        x_rows, lead, y, tgt_rows, wout_full, norm_ffn_w, wff1_full, wff2_full, norm_f_row)
    gr_wff2 = _tn_matmul(ff, dh2, "grad_w_ff2", tka=1024, max_slab=2816)
    gr_wff1_t = _tn_matmul(da, hn2, "grad_w_ff1", tka=1024, max_slab=2816)
    gr_wout = _tn_matmul(y, dh1, "grad_w_out", tka=1024, max_slab=2816)

    by_owner = lambda k: [owners(gr_wout[k].reshape(N_DEV, D_MIX // N_DEV, D_MODEL)),
                          owners(gr_wff1_t[k].reshape(N_DEV, D_FF // N_DEV, D_MODEL)),
                          owners(gr_wff2[k].reshape(N_DEV, D_FF // N_DEV, D_MODEL))]
    (dproj, gr_cw, gr_cb, gr_dtb, gr_alog, gr_d, gr_nw, gr_pw, gr_ps), late_exchanged = _mixer_bwd(
        proj, dy, ypre, conv_pre, states, cw_full, dt_bias_p, a_log_p, d_x, ssm_norm_w, pool_w[0], pool_scale, nb,
        by_owner(0), by_owner(1))

    early = dict(pool_w=gr_pw, pool_scale=gr_ps, conv_w=gr_cw, conv_b=gr_cb, dt_bias=gr_dtb[:, :N_HEADS],
                 a_log=gr_alog[:, :N_HEADS], d_skip=gr_d[:, :N_HEADS], ssm_norm_w=gr_nw, norm_ffn_w=gr_nffn,
                 norm_f_w=gr_nf, loss=loss[0:1, 0:1])
    early_pack = _pack(list(early.values()))
    win_mine, win_theirs = _tn_matmul_banded(hn1, dproj, core, "grad_w_in", win_cols, tka=512)
    win_got, (early_got,) = _grad_exchange_d2d([win_theirs], "grad_exchange_d2d_w_in", swapped=[early_pack])
    win_sum = _chip_sum(win_mine, win_got[0], core, "chip_sum_w_in")
    early_chip = _sum_two(early_pack, early_got, "chip_sum_small")
    whole = [False, True]
    send_sems, recv_sems, sent, landing, started = _chip_exchange_start([win_sum, early_chip], whole, "w_in_exchange_start")
    gx_rows, gr_nmix, gr_meta = _in_proj_bwd(dproj, x_rows, lead, dh1, norm_mix_w, win_full, started)
    (win_sum, early_chip), (win_landed, early_landed) = _chip_exchange_wait(
        send_sems, recv_sems, sent, landing, whole, gr_nmix, "w_in_exchange_wait")
    parts = dict(w_in=win_landed, w_out=late_exchanged[0], w_ff1=late_exchanged[1], w_ff2=late_exchanged[2])
    my_chip = jnp.reshape(2 * xi + yi, (1,)).astype(jnp.int32)
    early_sum = _sum_chips(early_landed, early_chip, my_chip, "small_sum")

    tail = dict(meta=gr_meta, norm_mix_w=gr_nmix)
    tail_pack = _pack(list(tail.values()))
    tail_sems = _chip_exchange_start([tail_pack], [True], "tail_exchange_start", everyone=True)

    res = {}
    for n in _BIG:
        shp = wts[n].shape
        args = (parts[n], wts[n][0], mom1[n][0], mom2[n][0], "adamw_" + n)
        if n == "w_ff1":
            outs = _adamw_transposed(*args)
        elif n == "w_in":
            outs = _adamw(*args, own=win_sum, chip=my_chip)
        else:
            outs = _adamw(*args)
        res[n] = [o.reshape(shp) for o in outs]
    send_sems, recv_sems, sent, landing, _ = tail_sems
    (tail_pack,), (tail_landed,) = _chip_exchange_wait(
        send_sems, recv_sems, sent, landing, [True], res["w_in"][1], "tail_exchange_wait", everyone=True)
    tail_sum = _sum_chips(tail_landed, tail_pack, jnp.reshape(dev, (1,)).astype(jnp.int32), "tail_sum")
    gs = dict(zip(early, _unpack(early_sum, [a.shape for a in early.values()])))
    gs.update(zip(tail, _unpack(tail_sum, [a.shape for a in tail.values()])))
    gs["meta"] = lax.dynamic_slice_in_dim(gs["meta"], dev * 128, 128, axis=1)
    gs["conv_w"] = lax.dynamic_slice_in_dim(gs["conv_w"], dev * cw_cols, cw_cols, axis=1)
    as2d = lambda a: a.reshape(-1, a.shape[-1])
    small_g = [as2d(gs[n].reshape(wts[n].shape)) for n in _SMALL]
    small_out = _adamw_small(small_g, *[[as2d(d[n]) for n in _SMALL] for d in (wts, mom1, mom2)])
    for k, n in enumerate(_SMALL):
        res[n] = [o[k].reshape(wts[n].shape) for o in (small_g,) + tuple(small_out)]

    grad_x = gx_rows.reshape(nb, seq, D_MODEL)
    return (gs["loss"][0, 0], grad_x, *[res[n][0] for n in _WEIGHTS], *[res[n][1] for n in _WEIGHTS],
            *[res[n][2] for n in _WEIGHTS], *[res[n][3] for n in _WEIGHTS])
```
